```python
import math
import jax, jax.numpy as jnp
from jax import lax
import numpy as np

D_MODEL = 1024
BATCH = 8
SEQ = 8192
DEPTH = 4

N_MIXERS = 2
BRANCH = D_MODEL
SSM_GROUP = 16
SSM_GROUPS = BRANCH // SSM_GROUP
SSM_STATE = 64
SSM_CHUNK = 128
DT_MIN = 1e-3
DT_MAX = 1e-1
HEAD_DIM = 64
N_Q_HEADS = BRANCH // HEAD_DIM
N_KV_HEADS = 2
GQA_GROUP = N_Q_HEADS // N_KV_HEADS
WINDOW = 128
ATTN_BLOCK = 128
Q_DIM = N_Q_HEADS * HEAD_DIM
KV_DIM = N_KV_HEADS * HEAD_DIM
ROPE_THETA = 10000.0
NORM_EPS = 1e-5
NEG_INF = -1e30

kernel_name = "hybrid_s5_swa_sink_trunk"


def _rmsnorm(x, g):
    xf = x.astype(jnp.float32)
    y = xf * lax.rsqrt(jnp.mean(xf * xf, axis=-1, keepdims=True) + NORM_EPS)
    return (y * g.astype(jnp.float32)).astype(x.dtype)


def _ssm_combine(e1, e2):
    a1, b1 = e1
    a2, b2 = e2
    return a1 * a2, a2 * b1 + b2


def _s5_scan(u, a_re, a_im, log_step, b_re, b_im, c_re, c_im, d):
    bsz, seq, _ = u.shape
    f32 = jnp.float32
    u = u.astype(f32)
    lam = lax.complex(a_re.astype(f32), a_im.astype(f32))
    step = jnp.exp(log_step.astype(f32))[:, None]
    a_bar = jnp.exp(lam * step)
    b = lax.complex(b_re.astype(f32), b_im.astype(f32))
    b_bar = ((a_bar - 1.0) / lam)[..., None] * b
    c = lax.complex(c_re.astype(f32), c_im.astype(f32))
    n_chunks = seq // SSM_CHUNK
    u_c = u.reshape(bsz, n_chunks, SSM_CHUNK, SSM_GROUPS, SSM_GROUP).transpose(1, 2, 0, 3, 4)

    def step_fn(h_prev, u_blk):
        bu = jnp.einsum("tbgc,gpc->tbgp", u_blk.astype(b_bar.dtype), b_bar)
        a = jnp.broadcast_to(a_bar, bu.shape)
        a_cum, h_loc = lax.associative_scan(_ssm_combine, (a, bu), axis=0)
        h = h_loc + a_cum * h_prev[None]
        y = jnp.real(jnp.einsum("tbgp,gcp->tbgc", h, c))
        return h[-1], y

    h0 = jnp.zeros((bsz, SSM_GROUPS, SSM_STATE), dtype=b_bar.dtype)
    _, y = lax.scan(step_fn, h0, u_c)
    y = y.transpose(2, 0, 1, 3, 4).reshape(bsz, seq, BRANCH)
    return y + d.astype(f32) * u


def _ssm_layer(x, norm, w_in, a_re, a_im, log_step, b_re, b_im, c_re, c_im, d, w_glu, b_glu, w_out):
    f32 = jnp.float32
    h = _rmsnorm(x, norm)
    proj = h @ w_in
    u, gate = jnp.split(proj, [BRANCH], axis=-1)
    y = _s5_scan(u, a_re, a_im, log_step, b_re, b_im, c_re, c_im, d)
    z = jax.nn.gelu(y)
    z = z * jax.nn.sigmoid(z @ w_glu.astype(f32) + b_glu.astype(f32))
    out = (z * jax.nn.silu(gate.astype(f32))).astype(x.dtype) @ w_out
    return x + out


def _rope(t, cos, sin):
    t1, t2 = jnp.split(t, 2, axis=-1)
    return jnp.concatenate([t1 * cos - t2 * sin, t2 * cos + t1 * sin], axis=-1)


def _swa_sinks(q, k, v, sinks):
    f32 = jnp.float32
    bsz, seq = q.shape[:2]
    nb = seq // ATTN_BLOCK
    qb = q.reshape(bsz, nb, ATTN_BLOCK, N_KV_HEADS, GQA_GROUP, HEAD_DIM)
    kb = k.reshape(bsz, nb, ATTN_BLOCK, N_KV_HEADS, HEAD_DIM)
    vb = v.reshape(bsz, nb, ATTN_BLOCK, N_KV_HEADS, HEAD_DIM)

    def with_prev(t):
        prev = jnp.concatenate([jnp.zeros_like(t[:, :1]), t[:, :-1]], axis=1)
        return jnp.concatenate([prev, t], axis=2)

    kk = with_prev(kb)
    vv = with_prev(vb)
    s = jnp.einsum("bnqhgd,bnkhd->bnhgqk", qb, kk) * (HEAD_DIM ** -0.5)
    qi = jnp.arange(ATTN_BLOCK)[:, None]
    kj = jnp.arange(2 * ATTN_BLOCK)[None, :]
    dist = qi + ATTN_BLOCK - kj
    band = (dist >= 0) & (dist < WINDOW)
    blk = jnp.arange(nb)[:, None, None]
    valid = band[None] & ((blk > 0) | (kj[None] >= ATTN_BLOCK))
    s = jnp.where(valid[None, :, None, None], s, NEG_INF)
    sink = sinks.astype(f32).reshape(N_KV_HEADS, GQA_GROUP)[None, None, :, :, None, None]
    m = jnp.maximum(jnp.max(s, axis=-1, keepdims=True), sink)
    p = jnp.exp(s - m)
    denom = jnp.sum(p, axis=-1, keepdims=True) + jnp.exp(sink - m)
    o = jnp.einsum("bnhgqk,bnkhd->bnqhgd", p / denom, vv)
    return o.reshape(bsz, seq, Q_DIM)


def _attn_layer(x, norm, w_in, sinks, w_out):
    f32 = jnp.float32
    bsz, seq, _ = x.shape
    h = _rmsnorm(x, norm)
    proj = (h @ w_in).astype(f32)
    q, k, v, gate = jnp.split(proj, [Q_DIM, Q_DIM + KV_DIM, Q_DIM + 2 * KV_DIM], axis=-1)
    q = q.reshape(bsz, seq, N_Q_HEADS, HEAD_DIM)
    k = k.reshape(bsz, seq, N_KV_HEADS, HEAD_DIM)
    v = v.reshape(bsz, seq, N_KV_HEADS, HEAD_DIM)
    pos = jnp.arange(seq, dtype=f32)
    inv_freq = ROPE_THETA ** (-jnp.arange(0, HEAD_DIM, 2, dtype=f32) / HEAD_DIM)
    ang = pos[:, None] * inv_freq[None, :]
    cos = jnp.cos(ang)[None, :, None, :]
    sin = jnp.sin(ang)[None, :, None, :]
    o = _swa_sinks(_rope(q, cos, sin), _rope(k, cos, sin), v, sinks)
    out = (o * jax.nn.silu(gate)).astype(x.dtype) @ w_out
    return x + out


def _fwd_setup_inputs(seed: int = 0) -> dict:
    key = jax.random.key(seed)
    keys = iter(jax.random.split(key, 64))
    f32 = jnp.float32

    def nrm(shape, scale):
        return jax.random.normal(next(keys), shape, f32) * scale

    inputs = {"x": nrm((BATCH, SEQ, D_MODEL), 1.0)}
    for i in range(DEPTH):
        p = "l%d_" % i
        inputs[p + "norm"] = 1.0 + nrm((D_MODEL,), 0.05)
        if i % N_MIXERS == 0:
            inputs[p + "w_in"] = nrm((D_MODEL, 2 * BRANCH), D_MODEL ** -0.5)
            inputs[p + "a_re"] = -0.5 + nrm((SSM_GROUPS, SSM_STATE), 0.01)
            inputs[p + "a_im"] = math.pi * jnp.arange(SSM_STATE, dtype=f32)[None, :] + nrm((SSM_GROUPS, SSM_STATE), 0.01)
            inputs[p + "log_step"] = jax.random.uniform(next(keys), (SSM_GROUPS,), f32, math.log(DT_MIN), math.log(DT_MAX))
            inputs[p + "b_re"] = nrm((SSM_GROUPS, SSM_STATE, SSM_GROUP), (2 * SSM_GROUP) ** -0.5)
            inputs[p + "b_im"] = nrm((SSM_GROUPS, SSM_STATE, SSM_GROUP), (2 * SSM_GROUP) ** -0.5)
            inputs[p + "c_re"] = nrm((SSM_GROUPS, SSM_GROUP, SSM_STATE), SSM_STATE ** -0.5)
            inputs[p + "c_im"] = nrm((SSM_GROUPS, SSM_GROUP, SSM_STATE), SSM_STATE ** -0.5)
            inputs[p + "d"] = nrm((BRANCH,), 1.0)
            inputs[p + "w_glu"] = nrm((BRANCH, BRANCH), BRANCH ** -0.5)
            inputs[p + "b_glu"] = nrm((BRANCH,), 0.01)
            inputs[p + "w_out"] = nrm((BRANCH, D_MODEL), BRANCH ** -0.5)
        else:
            inputs[p + "w_in"] = nrm((D_MODEL, Q_DIM + 2 * KV_DIM + BRANCH), D_MODEL ** -0.5)
            inputs[p + "sinks"] = nrm((N_Q_HEADS,), 1.0)
            inputs[p + "w_out"] = nrm((Q_DIM, D_MODEL), Q_DIM ** -0.5)
    inputs["final_norm"] = 1.0 + nrm((D_MODEL,), 0.05)
    return inputs


def _fwd_reference(x,
              l0_norm, l0_w_in, l0_a_re, l0_a_im, l0_log_step, l0_b_re, l0_b_im, l0_c_re, l0_c_im, l0_d, l0_w_glu, l0_b_glu, l0_w_out,
              l1_norm, l1_w_in, l1_sinks, l1_w_out,
              l2_norm, l2_w_in, l2_a_re, l2_a_im, l2_log_step, l2_b_re, l2_b_im, l2_c_re, l2_c_im, l2_d, l2_w_glu, l2_b_glu, l2_w_out,
              l3_norm, l3_w_in, l3_sinks, l3_w_out,
              final_norm):
    ssm_params = [
        (l0_norm, l0_w_in, l0_a_re, l0_a_im, l0_log_step, l0_b_re, l0_b_im, l0_c_re, l0_c_im, l0_d, l0_w_glu, l0_b_glu, l0_w_out),
        (l2_norm, l2_w_in, l2_a_re, l2_a_im, l2_log_step, l2_b_re, l2_b_im, l2_c_re, l2_c_im, l2_d, l2_w_glu, l2_b_glu, l2_w_out),
    ]
    attn_params = [
        (l1_norm, l1_w_in, l1_sinks, l1_w_out),
        (l3_norm, l3_w_in, l3_sinks, l3_w_out),
    ]
    for i in range(DEPTH):
        if i % N_MIXERS == 0:
            x = _ssm_layer(x, *ssm_params[i // N_MIXERS])
        else:
            x = _attn_layer(x, *attn_params[i // N_MIXERS])
    return _rmsnorm(x, final_norm)


import jax as _jax
import jax.numpy as _jnp

TWIN_FORMAT = 'train_step'
FWD_PARAMS = ['x', 'l0_norm', 'l0_w_in', 'l0_a_re', 'l0_a_im', 'l0_log_step', 'l0_b_re', 'l0_b_im', 'l0_c_re', 'l0_c_im', 'l0_d', 'l0_w_glu', 'l0_b_glu', 'l0_w_out', 'l1_norm', 'l1_w_in', 'l1_sinks', 'l1_w_out', 'l2_norm', 'l2_w_in', 'l2_a_re', 'l2_a_im', 'l2_log_step', 'l2_b_re', 'l2_b_im', 'l2_c_re', 'l2_c_im', 'l2_d', 'l2_w_glu', 'l2_b_glu', 'l2_w_out', 'l3_norm', 'l3_w_in', 'l3_sinks', 'l3_w_out', 'final_norm']
TWIN_WEIGHTS = ['l0_norm', 'l0_w_in', 'l0_a_re', 'l0_a_im', 'l0_log_step', 'l0_b_re', 'l0_b_im', 'l0_c_re', 'l0_c_im', 'l0_d', 'l0_w_glu', 'l0_b_glu', 'l0_w_out', 'l1_norm', 'l1_w_in', 'l1_sinks', 'l1_w_out', 'l2_norm', 'l2_w_in', 'l2_a_re', 'l2_a_im', 'l2_log_step', 'l2_b_re', 'l2_b_im', 'l2_c_re', 'l2_c_im', 'l2_d', 'l2_w_glu', 'l2_b_glu', 'l2_w_out', 'l3_norm', 'l3_w_in', 'l3_sinks', 'l3_w_out', 'final_norm']
TWIN_DIFF_INPUT = 'x'
TWIN_INPUTS = ['x', 'l0_norm', 'l0_w_in', 'l0_a_re', 'l0_a_im', 'l0_log_step', 'l0_b_re', 'l0_b_im', 'l0_c_re', 'l0_c_im', 'l0_d', 'l0_w_glu', 'l0_b_glu', 'l0_w_out', 'l1_norm', 'l1_w_in', 'l1_sinks', 'l1_w_out', 'l2_norm', 'l2_w_in', 'l2_a_re', 'l2_a_im', 'l2_log_step', 'l2_b_re', 'l2_b_im', 'l2_c_re', 'l2_c_im', 'l2_d', 'l2_w_glu', 'l2_b_glu', 'l2_w_out', 'l3_norm', 'l3_w_in', 'l3_sinks', 'l3_w_out', 'final_norm', 'loss_target', 'm_l0_norm', 'm_l0_w_in', 'm_l0_a_re', 'm_l0_a_im', 'm_l0_log_step', 'm_l0_b_re', 'm_l0_b_im', 'm_l0_c_re', 'm_l0_c_im', 'm_l0_d', 'm_l0_w_glu', 'm_l0_b_glu', 'm_l0_w_out', 'm_l1_norm', 'm_l1_w_in', 'm_l1_sinks', 'm_l1_w_out', 'm_l2_norm', 'm_l2_w_in', 'm_l2_a_re', 'm_l2_a_im', 'm_l2_log_step', 'm_l2_b_re', 'm_l2_b_im', 'm_l2_c_re', 'm_l2_c_im', 'm_l2_d', 'm_l2_w_glu', 'm_l2_b_glu', 'm_l2_w_out', 'm_l3_norm', 'm_l3_w_in', 'm_l3_sinks', 'm_l3_w_out', 'm_final_norm', 'v_l0_norm', 'v_l0_w_in', 'v_l0_a_re', 'v_l0_a_im', 'v_l0_log_step', 'v_l0_b_re', 'v_l0_b_im', 'v_l0_c_re', 'v_l0_c_im', 'v_l0_d', 'v_l0_w_glu', 'v_l0_b_glu', 'v_l0_w_out', 'v_l1_norm', 'v_l1_w_in', 'v_l1_sinks', 'v_l1_w_out', 'v_l2_norm', 'v_l2_w_in', 'v_l2_a_re', 'v_l2_a_im', 'v_l2_log_step', 'v_l2_b_re', 'v_l2_b_im', 'v_l2_c_re', 'v_l2_c_im', 'v_l2_d', 'v_l2_w_glu', 'v_l2_b_glu', 'v_l2_w_out', 'v_l3_norm', 'v_l3_w_in', 'v_l3_sinks', 'v_l3_w_out', 'v_final_norm']
TWIN_OUTPUTS = ['loss', 'grad_x', 'grad_l0_norm', 'grad_l0_w_in', 'grad_l0_a_re', 'grad_l0_a_im', 'grad_l0_log_step', 'grad_l0_b_re', 'grad_l0_b_im', 'grad_l0_c_re', 'grad_l0_c_im', 'grad_l0_d', 'grad_l0_w_glu', 'grad_l0_b_glu', 'grad_l0_w_out', 'grad_l1_norm', 'grad_l1_w_in', 'grad_l1_sinks', 'grad_l1_w_out', 'grad_l2_norm', 'grad_l2_w_in', 'grad_l2_a_re', 'grad_l2_a_im', 'grad_l2_log_step', 'grad_l2_b_re', 'grad_l2_b_im', 'grad_l2_c_re', 'grad_l2_c_im', 'grad_l2_d', 'grad_l2_w_glu', 'grad_l2_b_glu', 'grad_l2_w_out', 'grad_l3_norm', 'grad_l3_w_in', 'grad_l3_sinks', 'grad_l3_w_out', 'grad_final_norm', 'delta_l0_norm', 'delta_l0_w_in', 'delta_l0_a_re', 'delta_l0_a_im', 'delta_l0_log_step', 'delta_l0_b_re', 'delta_l0_b_im', 'delta_l0_c_re', 'delta_l0_c_im', 'delta_l0_d', 'delta_l0_w_glu', 'delta_l0_b_glu', 'delta_l0_w_out', 'delta_l1_norm', 'delta_l1_w_in', 'delta_l1_sinks', 'delta_l1_w_out', 'delta_l2_norm', 'delta_l2_w_in', 'delta_l2_a_re', 'delta_l2_a_im', 'delta_l2_log_step', 'delta_l2_b_re', 'delta_l2_b_im', 'delta_l2_c_re', 'delta_l2_c_im', 'delta_l2_d', 'delta_l2_w_glu', 'delta_l2_b_glu', 'delta_l2_w_out', 'delta_l3_norm', 'delta_l3_w_in', 'delta_l3_sinks', 'delta_l3_w_out', 'delta_final_norm', 'new_m_l0_norm', 'new_m_l0_w_in', 'new_m_l0_a_re', 'new_m_l0_a_im', 'new_m_l0_log_step', 'new_m_l0_b_re', 'new_m_l0_b_im', 'new_m_l0_c_re', 'new_m_l0_c_im', 'new_m_l0_d', 'new_m_l0_w_glu', 'new_m_l0_b_glu', 'new_m_l0_w_out', 'new_m_l1_norm', 'new_m_l1_w_in', 'new_m_l1_sinks', 'new_m_l1_w_out', 'new_m_l2_norm', 'new_m_l2_w_in', 'new_m_l2_a_re', 'new_m_l2_a_im', 'new_m_l2_log_step', 'new_m_l2_b_re', 'new_m_l2_b_im', 'new_m_l2_c_re', 'new_m_l2_c_im', 'new_m_l2_d', 'new_m_l2_w_glu', 'new_m_l2_b_glu', 'new_m_l2_w_out', 'new_m_l3_norm', 'new_m_l3_w_in', 'new_m_l3_sinks', 'new_m_l3_w_out', 'new_m_final_norm', 'new_v_l0_norm', 'new_v_l0_w_in', 'new_v_l0_a_re', 'new_v_l0_a_im', 'new_v_l0_log_step', 'new_v_l0_b_re', 'new_v_l0_b_im', 'new_v_l0_c_re', 'new_v_l0_c_im', 'new_v_l0_d', 'new_v_l0_w_glu', 'new_v_l0_b_glu', 'new_v_l0_w_out', 'new_v_l1_norm', 'new_v_l1_w_in', 'new_v_l1_sinks', 'new_v_l1_w_out', 'new_v_l2_norm', 'new_v_l2_w_in', 'new_v_l2_a_re', 'new_v_l2_a_im', 'new_v_l2_log_step', 'new_v_l2_b_re', 'new_v_l2_b_im', 'new_v_l2_c_re', 'new_v_l2_c_im', 'new_v_l2_d', 'new_v_l2_w_glu', 'new_v_l2_b_glu', 'new_v_l2_w_out', 'new_v_l3_norm', 'new_v_l3_w_in', 'new_v_l3_sinks', 'new_v_l3_w_out', 'new_v_final_norm']
TWIN_LEAF_KINDS = {'loss': 'loss', 'grad_x': 'grad_x', 'grad_l0_norm': 'grad_w', 'grad_l0_w_in': 'grad_w', 'grad_l0_a_re': 'grad_w', 'grad_l0_a_im': 'grad_w', 'grad_l0_log_step': 'grad_w', 'grad_l0_b_re': 'grad_w', 'grad_l0_b_im': 'grad_w', 'grad_l0_c_re': 'grad_w', 'grad_l0_c_im': 'grad_w', 'grad_l0_d': 'grad_w', 'grad_l0_w_glu': 'grad_w', 'grad_l0_b_glu': 'grad_w', 'grad_l0_w_out': 'grad_w', 'grad_l1_norm': 'grad_w', 'grad_l1_w_in': 'grad_w', 'grad_l1_sinks': 'grad_w', 'grad_l1_w_out': 'grad_w', 'grad_l2_norm': 'grad_w', 'grad_l2_w_in': 'grad_w', 'grad_l2_a_re': 'grad_w', 'grad_l2_a_im': 'grad_w', 'grad_l2_log_step': 'grad_w', 'grad_l2_b_re': 'grad_w', 'grad_l2_b_im': 'grad_w', 'grad_l2_c_re': 'grad_w', 'grad_l2_c_im': 'grad_w', 'grad_l2_d': 'grad_w', 'grad_l2_w_glu': 'grad_w', 'grad_l2_b_glu': 'grad_w', 'grad_l2_w_out': 'grad_w', 'grad_l3_norm': 'grad_w', 'grad_l3_w_in': 'grad_w', 'grad_l3_sinks': 'grad_w', 'grad_l3_w_out': 'grad_w', 'grad_final_norm': 'grad_w', 'delta_l0_norm': 'delta_w', 'delta_l0_w_in': 'delta_w', 'delta_l0_a_re': 'delta_w', 'delta_l0_a_im': 'delta_w', 'delta_l0_log_step': 'delta_w', 'delta_l0_b_re': 'delta_w', 'delta_l0_b_im': 'delta_w', 'delta_l0_c_re': 'delta_w', 'delta_l0_c_im': 'delta_w', 'delta_l0_d': 'delta_w', 'delta_l0_w_glu': 'delta_w', 'delta_l0_b_glu': 'delta_w', 'delta_l0_w_out': 'delta_w', 'delta_l1_norm': 'delta_w', 'delta_l1_w_in': 'delta_w', 'delta_l1_sinks': 'delta_w', 'delta_l1_w_out': 'delta_w', 'delta_l2_norm': 'delta_w', 'delta_l2_w_in': 'delta_w', 'delta_l2_a_re': 'delta_w', 'delta_l2_a_im': 'delta_w', 'delta_l2_log_step': 'delta_w', 'delta_l2_b_re': 'delta_w', 'delta_l2_b_im': 'delta_w', 'delta_l2_c_re': 'delta_w', 'delta_l2_c_im': 'delta_w', 'delta_l2_d': 'delta_w', 'delta_l2_w_glu': 'delta_w', 'delta_l2_b_glu': 'delta_w', 'delta_l2_w_out': 'delta_w', 'delta_l3_norm': 'delta_w', 'delta_l3_w_in': 'delta_w', 'delta_l3_sinks': 'delta_w', 'delta_l3_w_out': 'delta_w', 'delta_final_norm': 'delta_w', 'new_m_l0_norm': 'new_m', 'new_m_l0_w_in': 'new_m', 'new_m_l0_a_re': 'new_m', 'new_m_l0_a_im': 'new_m', 'new_m_l0_log_step': 'new_m', 'new_m_l0_b_re': 'new_m', 'new_m_l0_b_im': 'new_m', 'new_m_l0_c_re': 'new_m', 'new_m_l0_c_im': 'new_m', 'new_m_l0_d': 'new_m', 'new_m_l0_w_glu': 'new_m', 'new_m_l0_b_glu': 'new_m', 'new_m_l0_w_out': 'new_m', 'new_m_l1_norm': 'new_m', 'new_m_l1_w_in': 'new_m', 'new_m_l1_sinks': 'new_m', 'new_m_l1_w_out': 'new_m', 'new_m_l2_norm': 'new_m', 'new_m_l2_w_in': 'new_m', 'new_m_l2_a_re': 'new_m', 'new_m_l2_a_im': 'new_m', 'new_m_l2_log_step': 'new_m', 'new_m_l2_b_re': 'new_m', 'new_m_l2_b_im': 'new_m', 'new_m_l2_c_re': 'new_m', 'new_m_l2_c_im': 'new_m', 'new_m_l2_d': 'new_m', 'new_m_l2_w_glu': 'new_m', 'new_m_l2_b_glu': 'new_m', 'new_m_l2_w_out': 'new_m', 'new_m_l3_norm': 'new_m', 'new_m_l3_w_in': 'new_m', 'new_m_l3_sinks': 'new_m', 'new_m_l3_w_out': 'new_m', 'new_m_final_norm': 'new_m', 'new_v_l0_norm': 'new_v', 'new_v_l0_w_in': 'new_v', 'new_v_l0_a_re': 'new_v', 'new_v_l0_a_im': 'new_v', 'new_v_l0_log_step': 'new_v', 'new_v_l0_b_re': 'new_v', 'new_v_l0_b_im': 'new_v', 'new_v_l0_c_re': 'new_v', 'new_v_l0_c_im': 'new_v', 'new_v_l0_d': 'new_v', 'new_v_l0_w_glu': 'new_v', 'new_v_l0_b_glu': 'new_v', 'new_v_l0_w_out': 'new_v', 'new_v_l1_norm': 'new_v', 'new_v_l1_w_in': 'new_v', 'new_v_l1_sinks': 'new_v', 'new_v_l1_w_out': 'new_v', 'new_v_l2_norm': 'new_v', 'new_v_l2_w_in': 'new_v', 'new_v_l2_a_re': 'new_v', 'new_v_l2_a_im': 'new_v', 'new_v_l2_log_step': 'new_v', 'new_v_l2_b_re': 'new_v', 'new_v_l2_b_im': 'new_v', 'new_v_l2_c_re': 'new_v', 'new_v_l2_c_im': 'new_v', 'new_v_l2_d': 'new_v', 'new_v_l2_w_glu': 'new_v', 'new_v_l2_b_glu': 'new_v', 'new_v_l2_w_out': 'new_v', 'new_v_l3_norm': 'new_v', 'new_v_l3_w_in': 'new_v', 'new_v_l3_sinks': 'new_v', 'new_v_l3_w_out': 'new_v', 'new_v_final_norm': 'new_v'}


def _forward(args):
    return _fwd_reference(*[args[k] for k in FWD_PARAMS])


def _output_shape():
    out = _jax.eval_shape(lambda: _forward(_fwd_setup_inputs(0)))
    return out.shape, out.dtype

N_MICROBATCH = 1
ADAM_LR = 0.001
ADAM_B1 = 0.9
ADAM_B2 = 0.999
ADAM_EPS = 1e-08
ADAM_WD = 0.01
ADAM_STEP = 10
PER_EXAMPLE_BATCH_AXIS = {'x': 0, 'loss_target': 0}
SHARED_INPUTS = []
_WEIGHT_DTYPES = {'l0_norm': _jnp.float32, 'l0_w_in': _jnp.float32, 'l0_a_re': _jnp.float32, 'l0_a_im': _jnp.float32, 'l0_log_step': _jnp.float32, 'l0_b_re': _jnp.float32, 'l0_b_im': _jnp.float32, 'l0_c_re': _jnp.float32, 'l0_c_im': _jnp.float32, 'l0_d': _jnp.float32, 'l0_w_glu': _jnp.float32, 'l0_b_glu': _jnp.float32, 'l0_w_out': _jnp.float32, 'l1_norm': _jnp.float32, 'l1_w_in': _jnp.float32, 'l1_sinks': _jnp.float32, 'l1_w_out': _jnp.float32, 'l2_norm': _jnp.float32, 'l2_w_in': _jnp.float32, 'l2_a_re': _jnp.float32, 'l2_a_im': _jnp.float32, 'l2_log_step': _jnp.float32, 'l2_b_re': _jnp.float32, 'l2_b_im': _jnp.float32, 'l2_c_re': _jnp.float32, 'l2_c_im': _jnp.float32, 'l2_d': _jnp.float32, 'l2_w_glu': _jnp.float32, 'l2_b_glu': _jnp.float32, 'l2_w_out': _jnp.float32, 'l3_norm': _jnp.float32, 'l3_w_in': _jnp.float32, 'l3_sinks': _jnp.float32, 'l3_w_out': _jnp.float32, 'final_norm': _jnp.float32}
MOMENT_SCALE = {'l0_norm': 8.140539e-02, 'l0_w_in': 5.690790e-02, 'l0_a_re': 4.911624e-03, 'l0_a_im': 4.827816e-03, 'l0_log_step': 1.979970e+00, 'l0_b_re': 2.745182e-03, 'l0_b_im': 2.774903e-03, 'l0_c_re': 4.140473e-03, 'l0_c_im': 4.030641e-03, 'l0_d': 6.183264e-02, 'l0_w_glu': 1.636117e-02, 'l0_b_glu': 2.749175e-02, 'l0_w_out': 5.439130e-02, 'l1_norm': 6.147718e-02, 'l1_w_in': 4.034186e-02, 'l1_sinks': 2.742613e-02, 'l1_w_out': 3.195630e-02, 'l2_norm': 7.294461e-02, 'l2_w_in': 5.178785e-02, 'l2_a_re': 5.191075e-03, 'l2_a_im': 4.520650e-03, 'l2_log_step': 6.526003e+00, 'l2_b_re': 2.677687e-03, 'l2_b_im': 2.641005e-03, 'l2_c_re': 3.770319e-03, 'l2_c_im': 3.645860e-03, 'l2_d': 5.590213e-02, 'l2_w_glu': 1.365442e-02, 'l2_b_glu': 2.174752e-02, 'l2_w_out': 5.055036e-02, 'l3_norm': 5.322910e-02, 'l3_w_in': 3.696152e-02, 'l3_sinks': 2.157409e-02, 'l3_w_out': 2.961170e-02, 'final_norm': 6.402724e+01}


def _to_microbatches(a, axis):
    t = _jnp.moveaxis(a, axis, 0)
    t = t.reshape((N_MICROBATCH, t.shape[0] // N_MICROBATCH) + t.shape[1:])
    return _jnp.moveaxis(t, 1, axis + 1)


def setup_inputs(seed: int = 0) -> dict:
    inp = _fwd_setup_inputs(seed)
    key = _jax.random.fold_in(_jax.random.key(seed), 7919)
    shape, _ = _output_shape()
    out = dict(inp)
    out["loss_target"] = _jax.random.normal(_jax.random.fold_in(key, 0), shape, _jnp.float32)
    for i, name in enumerate(TWIN_WEIGHTS):
        w = inp[name].astype(_jnp.float32)
        if MOMENT_SCALE is None:
            s = _jnp.sqrt(_jnp.mean(_jnp.square(w)) + 1e-30)
        else:
            s = MOMENT_SCALE[name]
        km, kv = _jax.random.split(_jax.random.fold_in(key, i + 1))
        out[name] = w
        out["m_" + name] = s * _jax.random.normal(km, w.shape, _jnp.float32)
        out["v_" + name] = (s * s) * _jax.random.uniform(kv, w.shape, _jnp.float32, 0.5, 1.5)
    if N_MICROBATCH > 1:
        for name, axis in PER_EXAMPLE_BATCH_AXIS.items():
            out[name] = _to_microbatches(out[name], axis)
    return {'x': out['x'], 'l0_norm': out['l0_norm'], 'l0_w_in': out['l0_w_in'], 'l0_a_re': out['l0_a_re'], 'l0_a_im': out['l0_a_im'], 'l0_log_step': out['l0_log_step'], 'l0_b_re': out['l0_b_re'], 'l0_b_im': out['l0_b_im'], 'l0_c_re': out['l0_c_re'], 'l0_c_im': out['l0_c_im'], 'l0_d': out['l0_d'], 'l0_w_glu': out['l0_w_glu'], 'l0_b_glu': out['l0_b_glu'], 'l0_w_out': out['l0_w_out'], 'l1_norm': out['l1_norm'], 'l1_w_in': out['l1_w_in'], 'l1_sinks': out['l1_sinks'], 'l1_w_out': out['l1_w_out'], 'l2_norm': out['l2_norm'], 'l2_w_in': out['l2_w_in'], 'l2_a_re': out['l2_a_re'], 'l2_a_im': out['l2_a_im'], 'l2_log_step': out['l2_log_step'], 'l2_b_re': out['l2_b_re'], 'l2_b_im': out['l2_b_im'], 'l2_c_re': out['l2_c_re'], 'l2_c_im': out['l2_c_im'], 'l2_d': out['l2_d'], 'l2_w_glu': out['l2_w_glu'], 'l2_b_glu': out['l2_b_glu'], 'l2_w_out': out['l2_w_out'], 'l3_norm': out['l3_norm'], 'l3_w_in': out['l3_w_in'], 'l3_sinks': out['l3_sinks'], 'l3_w_out': out['l3_w_out'], 'final_norm': out['final_norm'], 'loss_target': out['loss_target'], 'm_l0_norm': out['m_l0_norm'], 'm_l0_w_in': out['m_l0_w_in'], 'm_l0_a_re': out['m_l0_a_re'], 'm_l0_a_im': out['m_l0_a_im'], 'm_l0_log_step': out['m_l0_log_step'], 'm_l0_b_re': out['m_l0_b_re'], 'm_l0_b_im': out['m_l0_b_im'], 'm_l0_c_re': out['m_l0_c_re'], 'm_l0_c_im': out['m_l0_c_im'], 'm_l0_d': out['m_l0_d'], 'm_l0_w_glu': out['m_l0_w_glu'], 'm_l0_b_glu': out['m_l0_b_glu'], 'm_l0_w_out': out['m_l0_w_out'], 'm_l1_norm': out['m_l1_norm'], 'm_l1_w_in': out['m_l1_w_in'], 'm_l1_sinks': out['m_l1_sinks'], 'm_l1_w_out': out['m_l1_w_out'], 'm_l2_norm': out['m_l2_norm'], 'm_l2_w_in': out['m_l2_w_in'], 'm_l2_a_re': out['m_l2_a_re'], 'm_l2_a_im': out['m_l2_a_im'], 'm_l2_log_step': out['m_l2_log_step'], 'm_l2_b_re': out['m_l2_b_re'], 'm_l2_b_im': out['m_l2_b_im'], 'm_l2_c_re': out['m_l2_c_re'], 'm_l2_c_im': out['m_l2_c_im'], 'm_l2_d': out['m_l2_d'], 'm_l2_w_glu': out['m_l2_w_glu'], 'm_l2_b_glu': out['m_l2_b_glu'], 'm_l2_w_out': out['m_l2_w_out'], 'm_l3_norm': out['m_l3_norm'], 'm_l3_w_in': out['m_l3_w_in'], 'm_l3_sinks': out['m_l3_sinks'], 'm_l3_w_out': out['m_l3_w_out'], 'm_final_norm': out['m_final_norm'], 'v_l0_norm': out['v_l0_norm'], 'v_l0_w_in': out['v_l0_w_in'], 'v_l0_a_re': out['v_l0_a_re'], 'v_l0_a_im': out['v_l0_a_im'], 'v_l0_log_step': out['v_l0_log_step'], 'v_l0_b_re': out['v_l0_b_re'], 'v_l0_b_im': out['v_l0_b_im'], 'v_l0_c_re': out['v_l0_c_re'], 'v_l0_c_im': out['v_l0_c_im'], 'v_l0_d': out['v_l0_d'], 'v_l0_w_glu': out['v_l0_w_glu'], 'v_l0_b_glu': out['v_l0_b_glu'], 'v_l0_w_out': out['v_l0_w_out'], 'v_l1_norm': out['v_l1_norm'], 'v_l1_w_in': out['v_l1_w_in'], 'v_l1_sinks': out['v_l1_sinks'], 'v_l1_w_out': out['v_l1_w_out'], 'v_l2_norm': out['v_l2_norm'], 'v_l2_w_in': out['v_l2_w_in'], 'v_l2_a_re': out['v_l2_a_re'], 'v_l2_a_im': out['v_l2_a_im'], 'v_l2_log_step': out['v_l2_log_step'], 'v_l2_b_re': out['v_l2_b_re'], 'v_l2_b_im': out['v_l2_b_im'], 'v_l2_c_re': out['v_l2_c_re'], 'v_l2_c_im': out['v_l2_c_im'], 'v_l2_d': out['v_l2_d'], 'v_l2_w_glu': out['v_l2_w_glu'], 'v_l2_b_glu': out['v_l2_b_glu'], 'v_l2_w_out': out['v_l2_w_out'], 'v_l3_norm': out['v_l3_norm'], 'v_l3_w_in': out['v_l3_w_in'], 'v_l3_sinks': out['v_l3_sinks'], 'v_l3_w_out': out['v_l3_w_out'], 'v_final_norm': out['v_final_norm']}


def _loss(weights, diff, rest, loss_target):
    with _jax.named_scope("forward"):
        args = {**rest, TWIN_DIFF_INPUT: diff, **{k: w.astype(_WEIGHT_DTYPES[k]) for k, w in weights.items()}}
        y = _forward(args)
    with _jax.named_scope("loss_head"):
        err = _jnp.square(y.astype(_jnp.float32) - loss_target)
        return 0.5 * _jnp.sum(_jnp.mean(err, axis=-1)) if err.ndim else 0.5 * err


def _adamw(w, g, m, v):
    m = ADAM_B1 * m + (1.0 - ADAM_B1) * g
    v = ADAM_B2 * v + (1.0 - ADAM_B2) * _jnp.square(g)
    m_hat = m / (1.0 - ADAM_B1 ** ADAM_STEP)
    v_hat = v / (1.0 - ADAM_B2 ** ADAM_STEP)
    delta = -ADAM_LR * (m_hat / (_jnp.sqrt(v_hat) + ADAM_EPS) + ADAM_WD * w)
    return delta, m, v


def reference(x, l0_norm, l0_w_in, l0_a_re, l0_a_im, l0_log_step, l0_b_re, l0_b_im, l0_c_re, l0_c_im, l0_d, l0_w_glu, l0_b_glu, l0_w_out, l1_norm, l1_w_in, l1_sinks, l1_w_out, l2_norm, l2_w_in, l2_a_re, l2_a_im, l2_log_step, l2_b_re, l2_b_im, l2_c_re, l2_c_im, l2_d, l2_w_glu, l2_b_glu, l2_w_out, l3_norm, l3_w_in, l3_sinks, l3_w_out, final_norm, loss_target, m_l0_norm, m_l0_w_in, m_l0_a_re, m_l0_a_im, m_l0_log_step, m_l0_b_re, m_l0_b_im, m_l0_c_re, m_l0_c_im, m_l0_d, m_l0_w_glu, m_l0_b_glu, m_l0_w_out, m_l1_norm, m_l1_w_in, m_l1_sinks, m_l1_w_out, m_l2_norm, m_l2_w_in, m_l2_a_re, m_l2_a_im, m_l2_log_step, m_l2_b_re, m_l2_b_im, m_l2_c_re, m_l2_c_im, m_l2_d, m_l2_w_glu, m_l2_b_glu, m_l2_w_out, m_l3_norm, m_l3_w_in, m_l3_sinks, m_l3_w_out, m_final_norm, v_l0_norm, v_l0_w_in, v_l0_a_re, v_l0_a_im, v_l0_log_step, v_l0_b_re, v_l0_b_im, v_l0_c_re, v_l0_c_im, v_l0_d, v_l0_w_glu, v_l0_b_glu, v_l0_w_out, v_l1_norm, v_l1_w_in, v_l1_sinks, v_l1_w_out, v_l2_norm, v_l2_w_in, v_l2_a_re, v_l2_a_im, v_l2_log_step, v_l2_b_re, v_l2_b_im, v_l2_c_re, v_l2_c_im, v_l2_d, v_l2_w_glu, v_l2_b_glu, v_l2_w_out, v_l3_norm, v_l3_w_in, v_l3_sinks, v_l3_w_out, v_final_norm):
    given = dict(x=x, l0_norm=l0_norm, l0_w_in=l0_w_in, l0_a_re=l0_a_re, l0_a_im=l0_a_im, l0_log_step=l0_log_step, l0_b_re=l0_b_re, l0_b_im=l0_b_im, l0_c_re=l0_c_re, l0_c_im=l0_c_im, l0_d=l0_d, l0_w_glu=l0_w_glu, l0_b_glu=l0_b_glu, l0_w_out=l0_w_out, l1_norm=l1_norm, l1_w_in=l1_w_in, l1_sinks=l1_sinks, l1_w_out=l1_w_out, l2_norm=l2_norm, l2_w_in=l2_w_in, l2_a_re=l2_a_re, l2_a_im=l2_a_im, l2_log_step=l2_log_step, l2_b_re=l2_b_re, l2_b_im=l2_b_im, l2_c_re=l2_c_re, l2_c_im=l2_c_im, l2_d=l2_d, l2_w_glu=l2_w_glu, l2_b_glu=l2_b_glu, l2_w_out=l2_w_out, l3_norm=l3_norm, l3_w_in=l3_w_in, l3_sinks=l3_sinks, l3_w_out=l3_w_out, final_norm=final_norm, loss_target=loss_target, m_l0_norm=m_l0_norm, m_l0_w_in=m_l0_w_in, m_l0_a_re=m_l0_a_re, m_l0_a_im=m_l0_a_im, m_l0_log_step=m_l0_log_step, m_l0_b_re=m_l0_b_re, m_l0_b_im=m_l0_b_im, m_l0_c_re=m_l0_c_re, m_l0_c_im=m_l0_c_im, m_l0_d=m_l0_d, m_l0_w_glu=m_l0_w_glu, m_l0_b_glu=m_l0_b_glu, m_l0_w_out=m_l0_w_out, m_l1_norm=m_l1_norm, m_l1_w_in=m_l1_w_in, m_l1_sinks=m_l1_sinks, m_l1_w_out=m_l1_w_out, m_l2_norm=m_l2_norm, m_l2_w_in=m_l2_w_in, m_l2_a_re=m_l2_a_re, m_l2_a_im=m_l2_a_im, m_l2_log_step=m_l2_log_step, m_l2_b_re=m_l2_b_re, m_l2_b_im=m_l2_b_im, m_l2_c_re=m_l2_c_re, m_l2_c_im=m_l2_c_im, m_l2_d=m_l2_d, m_l2_w_glu=m_l2_w_glu, m_l2_b_glu=m_l2_b_glu, m_l2_w_out=m_l2_w_out, m_l3_norm=m_l3_norm, m_l3_w_in=m_l3_w_in, m_l3_sinks=m_l3_sinks, m_l3_w_out=m_l3_w_out, m_final_norm=m_final_norm, v_l0_norm=v_l0_norm, v_l0_w_in=v_l0_w_in, v_l0_a_re=v_l0_a_re, v_l0_a_im=v_l0_a_im, v_l0_log_step=v_l0_log_step, v_l0_b_re=v_l0_b_re, v_l0_b_im=v_l0_b_im, v_l0_c_re=v_l0_c_re, v_l0_c_im=v_l0_c_im, v_l0_d=v_l0_d, v_l0_w_glu=v_l0_w_glu, v_l0_b_glu=v_l0_b_glu, v_l0_w_out=v_l0_w_out, v_l1_norm=v_l1_norm, v_l1_w_in=v_l1_w_in, v_l1_sinks=v_l1_sinks, v_l1_w_out=v_l1_w_out, v_l2_norm=v_l2_norm, v_l2_w_in=v_l2_w_in, v_l2_a_re=v_l2_a_re, v_l2_a_im=v_l2_a_im, v_l2_log_step=v_l2_log_step, v_l2_b_re=v_l2_b_re, v_l2_b_im=v_l2_b_im, v_l2_c_re=v_l2_c_re, v_l2_c_im=v_l2_c_im, v_l2_d=v_l2_d, v_l2_w_glu=v_l2_w_glu, v_l2_b_glu=v_l2_b_glu, v_l2_w_out=v_l2_w_out, v_l3_norm=v_l3_norm, v_l3_w_in=v_l3_w_in, v_l3_sinks=v_l3_sinks, v_l3_w_out=v_l3_w_out, v_final_norm=v_final_norm)
    weights = {n: given[n] for n in TWIN_WEIGHTS}
    shared = {n: given[n] for n in SHARED_INPUTS}
    per_example = {n: given[n] for n in ['x']}
    grad_fn = _jax.value_and_grad(_loss, argnums=(0, 1))

    def one_microbatch(ex, loss_target):
        ex = dict(ex)
        diff = ex.pop(TWIN_DIFF_INPUT)
        return grad_fn(weights, diff, {**shared, **ex}, loss_target)

    if N_MICROBATCH == 1:
        loss, (grad_w, grad_x) = one_microbatch(per_example, given["loss_target"])
    else:
        def body(carry, xs):
            loss_sum, grad_sum = carry
            l_k, (gw_k, gx_k) = one_microbatch(xs[0], xs[1])
            with _jax.named_scope("update"):
                return (loss_sum + l_k, _jax.tree.map(_jnp.add, grad_sum, gw_k)), gx_k

        init = (_jnp.zeros((), _jnp.float32), _jax.tree.map(_jnp.zeros_like, weights))
        (loss, grad_w), grad_x = _jax.lax.scan(body, init, (per_example, given["loss_target"]))
    with _jax.named_scope("update"):
        delta_w, new_m, new_v = {}, {}, {}
        for n in TWIN_WEIGHTS:
            delta_w[n], new_m[n], new_v[n] = _adamw(weights[n], grad_w[n], given["m_" + n], given["v_" + n])
    return (loss, grad_x, *[grad_w[n] for n in TWIN_WEIGHTS], *[delta_w[n] for n in TWIN_WEIGHTS],
            *[new_m[n] for n in TWIN_WEIGHTS], *[new_v[n] for n in TWIN_WEIGHTS])
```

```python
import functools
import math

import jax
import jax.numpy as jnp
from jax import lax
from jax.experimental import pallas as pl
from jax.experimental.pallas import tpu as pltpu

F32 = jnp.float32
BF16 = jnp.bfloat16

D_MODEL = 1024
SSM_GROUP = 16
SSM_GROUPS = D_MODEL // SSM_GROUP
SSM_STATE = 64
S5_BLOCK = 16
S5_LANES = S5_BLOCK * SSM_GROUP
S5_PAIRS = SSM_GROUPS // 2
HEAD_DIM = 64
N_Q_HEADS = 16
N_KV_HEADS = 2
GQA = N_Q_HEADS // N_KV_HEADS
Q_DIM = N_Q_HEADS * HEAD_DIM
KV_DIM = N_KV_HEADS * HEAD_DIM
ATTN_BLOCK = 128
ROPE_THETA = 10000.0
NORM_EPS = 1e-5
NEG_INF = -1e30
ATTN_SCALE = HEAD_DIM ** -0.5
N_DEV = 8

ADAM_LR = 0.001
ADAM_B1 = 0.9
ADAM_B2 = 0.999
ADAM_EPS = 1e-08
ADAM_WD = 0.01
ADAM_STEP = 10

VMEM_LIMIT = 56 * 1024 * 1024
ROWS_FWD = 512
ROWS_BWD = 256

NT = (((1,), (1,)), ((), ()))
TN = (((0,), (0,)), ((), ()))


def _params(n_grid):
    return pltpu.CompilerParams(dimension_semantics=("arbitrary",) * n_grid, vmem_limit_bytes=VMEM_LIMIT)


def _dot(a, b):
    return jnp.dot(a, b, preferred_element_type=F32)


def _dot_nt(a, b):
    return lax.dot_general(a, b, NT, preferred_element_type=F32)


def _dot_tn(a, b):
    return lax.dot_general(a, b, TN, preferred_element_type=F32)


def _sigmoid(x):
    return 1.0 / (1.0 + jnp.exp(-x))


_GELU_K = math.sqrt(2.0 / math.pi)


def _gelu(x):
    return x * (0.5 * (1.0 + jnp.tanh(_GELU_K * (x + 0.044715 * (x * x * x)))))


def _gelu_grad(x):
    t = jnp.tanh(_GELU_K * (x + 0.044715 * (x * x * x)))
    return 0.5 * (1.0 + t) + 0.5 * x * (1.0 - t * t) * (_GELU_K * (1.0 + 3.0 * 0.044715 * (x * x)))


def _row_spec(rows, cols):
    return pl.BlockSpec((rows, cols), lambda i: (i, 0))


def _const_spec(shape):
    zeros = (0,) * len(shape)
    return pl.BlockSpec(shape, lambda i: zeros)


def _rope_apply(t, cos, sin_signed, sign):
    lane = lax.broadcasted_iota(jnp.int32, (1, 128), 1)
    first_half = (lane % HEAD_DIM) < (HEAD_DIM // 2)
    out = []
    for j in range(t.shape[1] // 128):
        tj = t[:, 128 * j:128 * (j + 1)]
        partner = jnp.where(first_half, pltpu.roll(tj, 128 - HEAD_DIM // 2, 1), pltpu.roll(tj, HEAD_DIM // 2, 1))
        out.append(tj * cos + sign * (partner * sin_signed))
    return out[0] if len(out) == 1 else jnp.concatenate(out, axis=1)


def _inproj_fwd(x, norm, w, splits, rope, name):
    t = x.shape[0]
    n = w.shape[1]
    rows = ROWS_FWD

    def body(*refs):
        if rope is None:
            x_ref, n_ref, w_ref = refs[:3]
            outs = refs[3:]
        else:
            x_ref, n_ref, w_ref, cos_ref, sin_ref = refs[:5]
            outs = refs[5:]
        xv = x_ref[...]
        rstd = lax.rsqrt(jnp.mean(xv * xv, axis=-1, keepdims=True) + NORM_EPS)
        h = (xv * rstd) * n_ref[...]
        proj = _dot(h.astype(BF16), w_ref[...])
        off = 0
        for i, width in enumerate(splits):
            piece = proj[:, off:off + width]
            if rope is not None and i < 2:
                piece = _rope_apply(piece, cos_ref[...], sin_ref[...], 1.0)
            outs[i][...] = piece
            off += width

    in_specs = [_row_spec(rows, D_MODEL), _const_spec((1, D_MODEL)), _const_spec((D_MODEL, n))]
    args = [x, norm.reshape(1, D_MODEL), w]
    if rope is not None:
        in_specs += [_row_spec(rows, 128), _row_spec(rows, 128)]
        args += list(rope)
    return pl.pallas_call(
        body, name=name, grid=(t // rows,), in_specs=in_specs,
        out_specs=[_row_spec(rows, width) for width in splits],
        out_shape=[jax.ShapeDtypeStruct((t, width), F32) for width in splits],
        compiler_params=_params(1),
    )(*args)


def _inproj_bwd(x, norm, w, dpieces, dxn, name):
    t = x.shape[0]
    n = w.shape[1]
    rows = ROWS_BWD
    widths = [p.shape[1] for p in dpieces]
    k = len(dpieces)

    def body(*refs):
        x_ref, n_ref, w_ref, dxn_ref = refs[:4]
        d_refs = refs[4:4 + k]
        dx_ref, dw_ref, dn_ref = refs[4 + k:]
        i = pl.program_id(0)
        xv = x_ref[...]
        rstd = lax.rsqrt(jnp.mean(xv * xv, axis=-1, keepdims=True) + NORM_EPS)
        xhat = xv * rstd
        h = xhat * n_ref[...]
        dproj = [r[...].astype(BF16) for r in d_refs]
        dproj = dproj[0] if k == 1 else jnp.concatenate(dproj, axis=1)
        dh = _dot_nt(dproj, w_ref[...])
        dw = _dot_tn(h.astype(BF16), dproj)
        dn = jnp.sum(dh * xhat, axis=0, keepdims=True)

        @pl.when(i == 0)
        def _():
            dw_ref[...] = dw
            dn_ref[...] = dn

        @pl.when(i > 0)
        def _():
            dw_ref[...] += dw
            dn_ref[...] += dn

        dxhat = dh * n_ref[...]
        dx_ref[...] = rstd * (dxhat - xhat * jnp.mean(dxhat * xhat, axis=-1, keepdims=True)) + dxn_ref[...]

    return pl.pallas_call(
        body, name=name, grid=(t // rows,),
        in_specs=[_row_spec(rows, D_MODEL), _const_spec((1, D_MODEL)), _const_spec((D_MODEL, n)),
                  _row_spec(rows, D_MODEL)] + [_row_spec(rows, width) for width in widths],
        out_specs=[_row_spec(rows, D_MODEL), _const_spec((D_MODEL, n)), _const_spec((1, D_MODEL))],
        out_shape=[jax.ShapeDtypeStruct((t, D_MODEL), F32), jax.ShapeDtypeStruct((D_MODEL, n), F32),
                   jax.ShapeDtypeStruct((1, D_MODEL), F32)],
        compiler_params=_params(1),
    )(x, norm.reshape(1, D_MODEL), w, dxn, *dpieces)


def _pair_rows(m):
    z = jnp.zeros_like(m[0::2])
    top = jnp.concatenate([m[0::2], z], axis=2)
    bot = jnp.concatenate([z, m[1::2]], axis=2)
    return jnp.concatenate([top, bot], axis=1)


def _s5_matrices(a_re, a_im, log_step, b_re, b_im, c_re, c_im):
    hi = lax.Precision.HIGHEST
    r = S5_BLOCK
    step = jnp.exp(log_step)[:, None]
    lr, li = a_re * step, a_im * step
    k = jnp.arange(r + 1, dtype=F32)[:, None, None]
    mag = jnp.exp(lr[None] * k)
    pr, pi = mag * jnp.cos(li[None] * k), mag * jnp.sin(li[None] * k)
    nr, ni = pr[1] - 1.0, pi[1]
    den = a_re * a_re + a_im * a_im
    qr, qi = (nr * a_re + ni * a_im) / den, (ni * a_re - nr * a_im) / den
    bbr = qr[..., None] * b_re - qi[..., None] * b_im
    bbi = qr[..., None] * b_im + qi[..., None] * b_re
    wr = c_re[None] * pr[:, :, None, :] - c_im[None] * pi[:, :, None, :]
    wi = c_re[None] * pi[:, :, None, :] + c_im[None] * pr[:, :, None, :]
    kern = (jnp.einsum("tgop,gpi->tgoi", wr[:r], bbr, precision=hi)
            - jnp.einsum("tgop,gpi->tgoi", wi[:r], bbi, precision=hi))
    tau = jnp.arange(r)[:, None, None]
    shift = (jnp.arange(r)[None, None, :] - jnp.arange(r)[None, :, None] == tau).astype(F32)
    tmt = jnp.einsum("tsr,tgoi->gsiro", shift, kern, precision=hi).reshape(SSM_GROUPS, S5_LANES, S5_LANES)
    cp_re = wr[1:].transpose(1, 3, 0, 2).reshape(SSM_GROUPS, SSM_STATE, S5_LANES)
    cp_im = -wi[1:].transpose(1, 3, 0, 2).reshape(SSM_GROUPS, SSM_STATE, S5_LANES)
    prs, pis = pr[r - 1::-1][:r], pi[r - 1::-1][:r]
    bp_re = prs[:, :, :, None] * bbr[None] - pis[:, :, :, None] * bbi[None]
    bp_im = prs[:, :, :, None] * bbi[None] + pis[:, :, :, None] * bbr[None]
    bp_re = bp_re.transpose(1, 0, 3, 2).reshape(SSM_GROUPS, S5_LANES, SSM_STATE)
    bp_im = bp_im.transpose(1, 0, 3, 2).reshape(SSM_GROUPS, S5_LANES, SSM_STATE)
    ar = pr[r].reshape(1, SSM_GROUPS * SSM_STATE)
    ai = pi[r].reshape(1, SSM_GROUPS * SSM_STATE)
    return tmt, _pair_rows(bp_re), _pair_rows(bp_im), _pair_rows(cp_re), _pair_rows(cp_im), ar, ai


def _to_groups(a):
    t = a.shape[0]
    return a.reshape(t // S5_BLOCK, S5_BLOCK, SSM_GROUPS, SSM_GROUP).transpose(2, 0, 1, 3).reshape(
        SSM_GROUPS, t // S5_BLOCK, S5_LANES)


def _to_tokens(a):
    nb = a.shape[1]
    return a.reshape(SSM_GROUPS, nb, S5_BLOCK, SSM_GROUP).transpose(1, 2, 0, 3).reshape(nb * S5_BLOCK, D_MODEL)


def _pair_spec(nb):
    return pl.BlockSpec((2, nb, S5_LANES), lambda j: (j, 0, 0))


def _state_spec(nb):
    return pl.BlockSpec((nb, 128), lambda j: (0, j))


def _mat_spec(a, b):
    return pl.BlockSpec((1, a, b), lambda j: (j, 0, 0))


def _s5_block_inputs(u_g, bp_re, bp_im, name):
    nb = u_g.shape[1]

    def body(u_ref, bre_ref, bim_ref, xre_ref, xim_ref):
        up = jnp.concatenate([u_ref[0], u_ref[1]], axis=1).astype(BF16)
        xre_ref[...] = _dot(up, bre_ref[0])
        xim_ref[...] = _dot(up, bim_ref[0])

    return pl.pallas_call(
        body, name=name, grid=(S5_PAIRS,),
        in_specs=[_pair_spec(nb), _mat_spec(2 * S5_LANES, 128), _mat_spec(2 * S5_LANES, 128)],
        out_specs=[_state_spec(nb), _state_spec(nb)],
        out_shape=[jax.ShapeDtypeStruct((nb, SSM_GROUPS * SSM_STATE), F32)] * 2,
        compiler_params=_params(1),
    )(u_g, bp_re, bp_im)


_SCAN_LANES = 1024


def _s5_scan_fwd(xre, xim, ar, ai, name):
    nb = xre.shape[0]
    col = pl.BlockSpec((nb, _SCAN_LANES), lambda j: (0, j))
    par = pl.BlockSpec((1, _SCAN_LANES), lambda j: (0, j))

    def body(xre_ref, xim_ref, ar_ref, ai_ref, hre_ref, him_ref):
        a_r, a_i = ar_ref[...], ai_ref[...]

        def step(b, carry):
            hr, hi = carry
            hre_ref[pl.ds(b, 1), :] = hr
            him_ref[pl.ds(b, 1), :] = hi
            xr, xi = xre_ref[pl.ds(b, 1), :], xim_ref[pl.ds(b, 1), :]
            return a_r * hr - a_i * hi + xr, a_r * hi + a_i * hr + xi

        zero = jnp.zeros((1, _SCAN_LANES), F32)
        lax.fori_loop(0, nb, step, (zero, zero))

    return pl.pallas_call(
        body, name=name, grid=(xre.shape[1] // _SCAN_LANES,),
        in_specs=[col, col, par, par], out_specs=[col, col],
        out_shape=[jax.ShapeDtypeStruct(xre.shape, F32)] * 2,
        compiler_params=_params(1),
    )(xre, xim, ar, ai)


def _s5_scan_bwd(dhre, dhim, hre, him, ar, ai, name):
    nb = dhre.shape[0]
    col = pl.BlockSpec((nb, _SCAN_LANES), lambda j: (0, j))
    par = pl.BlockSpec((1, _SCAN_LANES), lambda j: (0, j))

    def body(dhre_ref, dhim_ref, hre_ref, him_ref, ar_ref, ai_ref, dxre_ref, dxim_ref, dar_ref, dai_ref):
        a_r, a_i = ar_ref[...], ai_ref[...]

        def step(s, carry):
            gr, gi, dar, dai = carry
            b = nb - 1 - s
            dxre_ref[pl.ds(b, 1), :] = gr
            dxim_ref[pl.ds(b, 1), :] = gi
            hr, hi = hre_ref[pl.ds(b, 1), :], him_ref[pl.ds(b, 1), :]
            dar = dar + (hr * gr + hi * gi)
            dai = dai + (hr * gi - hi * gr)
            dr, di = dhre_ref[pl.ds(b, 1), :], dhim_ref[pl.ds(b, 1), :]
            return dr + (a_r * gr + a_i * gi), di + (a_r * gi - a_i * gr), dar, dai

        zero = jnp.zeros((1, _SCAN_LANES), F32)
        _, _, dar, dai = lax.fori_loop(0, nb, step, (zero, zero, zero, zero))
        dar_ref[...] = dar
        dai_ref[...] = dai

    return pl.pallas_call(
        body, name=name, grid=(dhre.shape[1] // _SCAN_LANES,),
        in_specs=[col, col, col, col, par, par], out_specs=[col, col, par, par],
        out_shape=[jax.ShapeDtypeStruct(dhre.shape, F32)] * 2 + [jax.ShapeDtypeStruct(ar.shape, F32)] * 2,
        compiler_params=_params(1),
    )(dhre, dhim, hre, him, ar, ai)


def _s5_outputs(u_g, hre, him, tmt, cp_re, cp_im, d_lanes, name):
    nb = u_g.shape[1]

    def body(u_ref, hre_ref, him_ref, tmt_ref, cre_ref, cim_ref, d_ref, y_ref):
        yh = _dot(hre_ref[...].astype(BF16), cre_ref[0]) + _dot(him_ref[...].astype(BF16), cim_ref[0])
        for g in range(2):
            u = u_ref[g]
            y_ref[g] = _dot(u.astype(BF16), tmt_ref[g]) + yh[:, S5_LANES * g:S5_LANES * (g + 1)] + d_ref[g] * u

    return pl.pallas_call(
        body, name=name, grid=(S5_PAIRS,),
        in_specs=[_pair_spec(nb), _state_spec(nb), _state_spec(nb),
                  pl.BlockSpec((2, S5_LANES, S5_LANES), lambda j: (j, 0, 0)),
                  _mat_spec(128, 2 * S5_LANES), _mat_spec(128, 2 * S5_LANES),
                  pl.BlockSpec((2, 1, S5_LANES), lambda j: (j, 0, 0))],
        out_specs=_pair_spec(nb),
        out_shape=jax.ShapeDtypeStruct(u_g.shape, F32),
        compiler_params=_params(1),
    )(u_g, hre, him, tmt, cp_re, cp_im, d_lanes)


def _s5_state_grads(dy_g, cp_re, cp_im, name):
    nb = dy_g.shape[1]

    def body(dy_ref, cre_ref, cim_ref, dhre_ref, dhim_ref):
        dyp = jnp.concatenate([dy_ref[0], dy_ref[1]], axis=1).astype(BF16)
        dhre_ref[...] = _dot_nt(dyp, cre_ref[0])
        dhim_ref[...] = _dot_nt(dyp, cim_ref[0])

    return pl.pallas_call(
        body, name=name, grid=(S5_PAIRS,),
        in_specs=[_pair_spec(nb), _mat_spec(128, 2 * S5_LANES), _mat_spec(128, 2 * S5_LANES)],
        out_specs=[_state_spec(nb), _state_spec(nb)],
        out_shape=[jax.ShapeDtypeStruct((nb, SSM_GROUPS * SSM_STATE), F32)] * 2,
        compiler_params=_params(1),
    )(dy_g, cp_re, cp_im)


def _s5_backward(dy_g, u_g, hre, him, dxre, dxim, tmt, bp_re, bp_im, d_lanes, name):
    nb = u_g.shape[1]

    def body(dy_ref, u_ref, hre_ref, him_ref, dxre_ref, dxim_ref, tmt_ref, bre_ref, bim_ref, d_ref,
             du_ref, dtmt_ref, dcre_ref, dcim_ref, dbre_ref, dbim_ref, dd_ref):
        dxr, dxi = dxre_ref[...].astype(BF16), dxim_ref[...].astype(BF16)
        dux = _dot_nt(dxr, bre_ref[0]) + _dot_nt(dxi, bim_ref[0])
        dys, us = [], []
        for g in range(2):
            dy, u = dy_ref[g], u_ref[g]
            dyb, ub = dy.astype(BF16), u.astype(BF16)
            du_ref[g] = _dot_nt(dyb, tmt_ref[g]) + dux[:, S5_LANES * g:S5_LANES * (g + 1)] + d_ref[g] * dy
            dtmt_ref[g] = _dot_tn(ub, dyb)
            dd_ref[g] = jnp.sum(dy * u, axis=0, keepdims=True)
            dys.append(dyb)
            us.append(ub)
        dyp = jnp.concatenate(dys, axis=1)
        up = jnp.concatenate(us, axis=1)
        dcre_ref[0] = _dot_tn(hre_ref[...].astype(BF16), dyp)
        dcim_ref[0] = _dot_tn(him_ref[...].astype(BF16), dyp)
        dbre_ref[0] = _dot_tn(up, dxr)
        dbim_ref[0] = _dot_tn(up, dxi)

    sq = pl.BlockSpec((2, S5_LANES, S5_LANES), lambda j: (j, 0, 0))
    dl = pl.BlockSpec((2, 1, S5_LANES), lambda j: (j, 0, 0))
    return pl.pallas_call(
        body, name=name, grid=(S5_PAIRS,),
        in_specs=[_pair_spec(nb), _pair_spec(nb), _state_spec(nb), _state_spec(nb), _state_spec(nb), _state_spec(nb),
                  sq, _mat_spec(2 * S5_LANES, 128), _mat_spec(2 * S5_LANES, 128), dl],
        out_specs=[_pair_spec(nb), sq, _mat_spec(128, 2 * S5_LANES), _mat_spec(128, 2 * S5_LANES),
                   _mat_spec(2 * S5_LANES, 128), _mat_spec(2 * S5_LANES, 128), dl],
        out_shape=[jax.ShapeDtypeStruct(u_g.shape, F32),
                   jax.ShapeDtypeStruct((SSM_GROUPS, S5_LANES, S5_LANES), F32),
                   jax.ShapeDtypeStruct((S5_PAIRS, 128, 2 * S5_LANES), F32),
                   jax.ShapeDtypeStruct((S5_PAIRS, 128, 2 * S5_LANES), F32),
                   jax.ShapeDtypeStruct((S5_PAIRS, 2 * S5_LANES, 128), F32),
                   jax.ShapeDtypeStruct((S5_PAIRS, 2 * S5_LANES, 128), F32),
                   jax.ShapeDtypeStruct((SSM_GROUPS, 1, S5_LANES), F32)],
        compiler_params=_params(1),
    )(dy_g, u_g, hre, him, dxre, dxim, tmt, bp_re, bp_im, d_lanes)


def _ssm_out_fwd(y, gate, x, w_glu, b_glu, w_out, name):
    t = x.shape[0]
    rows = ROWS_FWD

    def body(y_ref, g_ref, x_ref, wg_ref, bg_ref, wo_ref, o_ref):
        z0 = _gelu(y_ref[...])
        s = _dot(z0.astype(BF16), wg_ref[...]) + bg_ref[...]
        gate_v = g_ref[...]
        a = (z0 * _sigmoid(s)) * (gate_v * _sigmoid(gate_v))
        o_ref[...] = x_ref[...] + _dot(a.astype(BF16), wo_ref[...])

    return pl.pallas_call(
        body, name=name, grid=(t // rows,),
        in_specs=[_row_spec(rows, D_MODEL)] * 3 + [_const_spec((D_MODEL, D_MODEL)), _const_spec((1, D_MODEL)),
                                                   _const_spec((D_MODEL, D_MODEL))],
        out_specs=_row_spec(rows, D_MODEL),
        out_shape=jax.ShapeDtypeStruct((t, D_MODEL), F32),
        compiler_params=_params(1),
    )(y, gate, x, w_glu, b_glu.reshape(1, D_MODEL), w_out)


def _ssm_out_bwd(dxn, y, gate, w_glu, b_glu, w_out, name):
    t = y.shape[0]
    rows = ROWS_BWD

    def body(dxn_ref, y_ref, g_ref, wg_ref, bg_ref, wo_ref, dy_ref, dg_ref, dwg_ref, dbg_ref, dwo_ref):
        i = pl.program_id(0)
        yv = y_ref[...]
        z0 = _gelu(yv)
        z0b = z0.astype(BF16)
        sg = _sigmoid(_dot(z0b, wg_ref[...]) + bg_ref[...])
        z = z0 * sg
        gate_v = g_ref[...]
        sgg = _sigmoid(gate_v)
        silu = gate_v * sgg
        a = z * silu
        dob = dxn_ref[...].astype(BF16)
        da = _dot_nt(dob, wo_ref[...])
        dwo = _dot_tn(a.astype(BF16), dob)
        dz = da * silu
        dg_ref[...] = da * z * (sgg * (1.0 + gate_v * (1.0 - sgg)))
        ds = dz * z0 * (sg * (1.0 - sg))
        dsb = ds.astype(BF16)
        dz0 = dz * sg + _dot_nt(dsb, wg_ref[...])
        dwg = _dot_tn(z0b, dsb)
        dbg = jnp.sum(ds, axis=0, keepdims=True)
        dy_ref[...] = dz0 * _gelu_grad(yv)

        @pl.when(i == 0)
        def _():
            dwo_ref[...] = dwo
            dwg_ref[...] = dwg
            dbg_ref[...] = dbg

        @pl.when(i > 0)
        def _():
            dwo_ref[...] += dwo
            dwg_ref[...] += dwg
            dbg_ref[...] += dbg

    sq = _const_spec((D_MODEL, D_MODEL))
    vec = _const_spec((1, D_MODEL))
    return pl.pallas_call(
        body, name=name, grid=(t // rows,),
        in_specs=[_row_spec(rows, D_MODEL)] * 3 + [sq, vec, sq],
        out_specs=[_row_spec(rows, D_MODEL), _row_spec(rows, D_MODEL), sq, vec, sq],
        out_shape=[jax.ShapeDtypeStruct((t, D_MODEL), F32)] * 2 + [
            jax.ShapeDtypeStruct((D_MODEL, D_MODEL), F32), jax.ShapeDtypeStruct((1, D_MODEL), F32),
            jax.ShapeDtypeStruct((D_MODEL, D_MODEL), F32)],
        compiler_params=_params(1),
    )(dxn, y, gate, w_glu, b_glu.reshape(1, D_MODEL), w_out)


def _attn_valid(block_is_first):
    qi = lax.broadcasted_iota(jnp.int32, (ATTN_BLOCK, 2 * ATTN_BLOCK), 0)
    kj = lax.broadcasted_iota(jnp.int32, (ATTN_BLOCK, 2 * ATTN_BLOCK), 1)
    dist = qi + ATTN_BLOCK - kj
    return (dist >= 0) & (dist < ATTN_BLOCK) & (jnp.logical_not(block_is_first) | (kj >= ATTN_BLOCK))


def _attn_probs(qh, kk, sink, valid):
    s = _dot_nt(qh, kk) * ATTN_SCALE
    s = jnp.where(valid, s, NEG_INF)
    m = jnp.maximum(jnp.max(s, axis=1, keepdims=True), sink)
    p = jnp.exp(s - m)
    e_sink = jnp.exp(sink - m)
    den = jnp.sum(p, axis=1, keepdims=True) + e_sink
    return p / den, e_sink / den


def _head(ref, h):
    return ref[:, HEAD_DIM * h:HEAD_DIM * (h + 1)]


def _attn_fwd(q, k, v, sinks, name):
    t = q.shape[0]
    nblk = t // ATTN_BLOCK

    def body(s_ref, q_ref, kc_ref, kp_ref, vc_ref, vp_ref, o_ref):
        valid = _attn_valid(pl.program_id(0) == 0)
        for kvh in range(N_KV_HEADS):
            kk = jnp.concatenate([_head(kp_ref, kvh), _head(kc_ref, kvh)], axis=0).astype(BF16)
            vv = jnp.concatenate([_head(vp_ref, kvh), _head(vc_ref, kvh)], axis=0).astype(BF16)
            for g in range(GQA):
                h = kvh * GQA + g
                p, _ = _attn_probs(_head(q_ref, h).astype(BF16), kk, s_ref[h], valid)
                o_ref[:, HEAD_DIM * h:HEAD_DIM * (h + 1)] = _dot(p.astype(BF16), vv)

    cur = lambda i: (i, 0)
    prev = lambda i: (jnp.maximum(i - 1, 0), 0)
    return pl.pallas_call(
        body, name=name, grid=(nblk,),
        in_specs=[pl.BlockSpec(memory_space=pltpu.SMEM),
                  pl.BlockSpec((ATTN_BLOCK, Q_DIM), cur),
                  pl.BlockSpec((ATTN_BLOCK, KV_DIM), cur), pl.BlockSpec((ATTN_BLOCK, KV_DIM), prev),
                  pl.BlockSpec((ATTN_BLOCK, KV_DIM), cur), pl.BlockSpec((ATTN_BLOCK, KV_DIM), prev)],
        out_specs=pl.BlockSpec((ATTN_BLOCK, Q_DIM), cur),
        out_shape=jax.ShapeDtypeStruct((t, Q_DIM), F32),
        compiler_params=_params(1),
    )(sinks, q, k, k, v, v)


def _attn_bwd(q, k, v, o, do, sinks, rope, name):
    t = q.shape[0]
    nblk = t // ATTN_BLOCK

    def body(s_ref, q0_ref, q1_ref, o0_ref, o1_ref, do0_ref, do1_ref, kp_ref, kc_ref, kn_ref, vp_ref, vc_ref, vn_ref,
             cos_ref, sin_ref, dq_ref, dk_ref, dv_ref, ds_ref):
        n = pl.program_id(0)
        has_next = (n + 1 < nblk).astype(F32)
        valid0 = _attn_valid(n == 0)
        valid1 = _attn_valid(n < 0)

        @pl.when(n == 0)
        def _():
            ds_ref[...] = jnp.zeros_like(ds_ref)

        for kvh in range(N_KV_HEADS):
            kc, vc = _head(kc_ref, kvh), _head(vc_ref, kvh)
            kk0 = jnp.concatenate([_head(kp_ref, kvh), kc], axis=0).astype(BF16)
            vv0 = jnp.concatenate([_head(vp_ref, kvh), vc], axis=0).astype(BF16)
            kk1 = jnp.concatenate([kc, _head(kn_ref, kvh)], axis=0).astype(BF16)
            vv1 = jnp.concatenate([vc, _head(vn_ref, kvh)], axis=0).astype(BF16)
            dk = jnp.zeros((ATTN_BLOCK, HEAD_DIM), F32)
            dv = jnp.zeros((ATTN_BLOCK, HEAD_DIM), F32)
            for g in range(GQA):
                h = kvh * GQA + g
                sink = s_ref[h]
                qh, doh = _head(q0_ref, h).astype(BF16), _head(do0_ref, h)
                p, p_sink = _attn_probs(qh, kk0, sink, valid0)
                delta = jnp.sum(doh * _head(o0_ref, h), axis=1, keepdims=True)
                dohb = doh.astype(BF16)
                dsc = (p * (_dot_nt(dohb, vv0) - delta) * ATTN_SCALE).astype(BF16)
                dq_ref[:, HEAD_DIM * h:HEAD_DIM * (h + 1)] = _dot(dsc, kk0)
                dk = dk + _dot_tn(dsc[:, ATTN_BLOCK:], qh)
                dv = dv + _dot_tn(p[:, ATTN_BLOCK:].astype(BF16), dohb)
                ds_ref[h:h + 1, :] += jnp.zeros((1, 128), F32) - jnp.sum(p_sink * delta, axis=0, keepdims=True)
                qh, doh = _head(q1_ref, h).astype(BF16), _head(do1_ref, h)
                p, _ = _attn_probs(qh, kk1, sink, valid1)
                delta = jnp.sum(doh * _head(o1_ref, h), axis=1, keepdims=True)
                dohb = doh.astype(BF16)
                dsc = (p * (_dot_nt(dohb, vv1) - delta) * (ATTN_SCALE * has_next)).astype(BF16)
                dk = dk + _dot_tn(dsc[:, :ATTN_BLOCK], qh)
                dv = dv + _dot_tn((p[:, :ATTN_BLOCK] * has_next).astype(BF16), dohb)
            dk_ref[:, HEAD_DIM * kvh:HEAD_DIM * (kvh + 1)] = dk
            dv_ref[:, HEAD_DIM * kvh:HEAD_DIM * (kvh + 1)] = dv
        cos, sin_signed = cos_ref[...], sin_ref[...]
        dq_ref[...] = _rope_apply(dq_ref[...], cos, sin_signed, -1.0)
        dk_ref[...] = _rope_apply(dk_ref[...], cos, sin_signed, -1.0)

    cur = lambda i: (i, 0)
    prev = lambda i: (jnp.maximum(i - 1, 0), 0)
    nxt = lambda i: (jnp.minimum(i + 1, nblk - 1), 0)
    qs = lambda f: pl.BlockSpec((ATTN_BLOCK, Q_DIM), f)
    ks = lambda f: pl.BlockSpec((ATTN_BLOCK, KV_DIM), f)
    return pl.pallas_call(
        body, name=name, grid=(nblk,),
        in_specs=[pl.BlockSpec(memory_space=pltpu.SMEM), qs(cur), qs(nxt), qs(cur), qs(nxt), qs(cur), qs(nxt),
                  ks(prev), ks(cur), ks(nxt), ks(prev), ks(cur), ks(nxt), ks(cur), ks(cur)],
        out_specs=[qs(cur), ks(cur), ks(cur), _const_spec((N_Q_HEADS, 128))],
        out_shape=[jax.ShapeDtypeStruct((t, Q_DIM), F32), jax.ShapeDtypeStruct((t, KV_DIM), F32),
                   jax.ShapeDtypeStruct((t, KV_DIM), F32), jax.ShapeDtypeStruct((N_Q_HEADS, 128), F32)],
        compiler_params=_params(1),
    )(sinks, q, q, o, o, do, do, k, k, k, v, v, v, rope[0], rope[1])


def _attn_out_fwd(o, gate, x, w_out, name):
    t = x.shape[0]
    rows = ROWS_FWD

    def body(o_ref, g_ref, x_ref, wo_ref, xn_ref):
        gate_v = g_ref[...]
        a = o_ref[...] * (gate_v * _sigmoid(gate_v))
        xn_ref[...] = x_ref[...] + _dot(a.astype(BF16), wo_ref[...])

    return pl.pallas_call(
        body, name=name, grid=(t // rows,),
        in_specs=[_row_spec(rows, D_MODEL)] * 3 + [_const_spec((D_MODEL, D_MODEL))],
        out_specs=_row_spec(rows, D_MODEL),
        out_shape=jax.ShapeDtypeStruct((t, D_MODEL), F32),
        compiler_params=_params(1),
    )(o, gate, x, w_out)


def _attn_out_bwd(dxn, o, gate, w_out, name):
    t = o.shape[0]
    rows = ROWS_BWD

    def body(dxn_ref, o_ref, g_ref, wo_ref, do_ref, dg_ref, dwo_ref):
        i = pl.program_id(0)
        gate_v, ov = g_ref[...], o_ref[...]
        sgg = _sigmoid(gate_v)
        silu = gate_v * sgg
        dob = dxn_ref[...].astype(BF16)
        da = _dot_nt(dob, wo_ref[...])
        dwo = _dot_tn((ov * silu).astype(BF16), dob)
        do_ref[...] = da * silu
        dg_ref[...] = da * ov * (sgg * (1.0 + gate_v * (1.0 - sgg)))

        @pl.when(i == 0)
        def _():
            dwo_ref[...] = dwo

        @pl.when(i > 0)
        def _():
            dwo_ref[...] += dwo

    sq = _const_spec((D_MODEL, D_MODEL))
    return pl.pallas_call(
        body, name=name, grid=(t // rows,),
        in_specs=[_row_spec(rows, D_MODEL)] * 3 + [sq],
        out_specs=[_row_spec(rows, D_MODEL), _row_spec(rows, D_MODEL), sq],
        out_shape=[jax.ShapeDtypeStruct((t, D_MODEL), F32)] * 2 + [jax.ShapeDtypeStruct((D_MODEL, D_MODEL), F32)],
        compiler_params=_params(1),
    )(dxn, o, gate, w_out)


def _loss_head(x, norm, target, name):
    t = x.shape[0]
    rows = ROWS_FWD

    def body(x_ref, n_ref, t_ref, loss_ref, dx_ref, dn_ref):
        i = pl.program_id(0)
        xv = x_ref[...]
        rstd = lax.rsqrt(jnp.mean(xv * xv, axis=-1, keepdims=True) + NORM_EPS)
        xhat = xv * rstd
        err = xhat * n_ref[...] - t_ref[...]
        part = 0.5 * jnp.sum(jnp.mean(err * err, axis=-1, keepdims=True), axis=0, keepdims=True)
        dy = err * (1.0 / D_MODEL)
        dn = jnp.sum(dy * xhat, axis=0, keepdims=True)
        dxhat = dy * n_ref[...]
        dx_ref[...] = rstd * (dxhat - xhat * jnp.mean(dxhat * xhat, axis=-1, keepdims=True))

        @pl.when(i == 0)
        def _():
            loss_ref[...] = jnp.zeros((8, 128), F32) + part
            dn_ref[...] = dn

        @pl.when(i > 0)
        def _():
            loss_ref[...] += part
            dn_ref[...] += dn

    return pl.pallas_call(
        body, name=name, grid=(t // rows,),
        in_specs=[_row_spec(rows, D_MODEL), _const_spec((1, D_MODEL)), _row_spec(rows, D_MODEL)],
        out_specs=[_const_spec((8, 128)), _row_spec(rows, D_MODEL), _const_spec((1, D_MODEL))],
        out_shape=[jax.ShapeDtypeStruct((8, 128), F32), jax.ShapeDtypeStruct((t, D_MODEL), F32),
                   jax.ShapeDtypeStruct((1, D_MODEL), F32)],
        compiler_params=_params(1),
    )(x, norm.reshape(1, D_MODEL), target)


def _exchange(send, per_dest, name):
    shape = send.shape[1:] if per_dest else send.shape

    def body(send_ref, recv_ref, send_sems, recv_sems, local_sem):
        x, y, c = lax.axis_index("x"), lax.axis_index("y"), lax.axis_index("c")
        me = 4 * x + 2 * y + c

        def peer(k):
            kx, ky, kc = (k >> 2) & 1, (k >> 1) & 1, k & 1
            px = x + kx - 2 * x * kx
            py = y + ky - 2 * y * ky
            pc = c + kc - 2 * c * kc
            return (px, py, pc), 4 * px + 2 * py + pc

        def copy(k, landing):
            to, to_index = peer(k)
            return pltpu.make_async_remote_copy(
                src_ref=send_ref.at[to_index] if per_dest else send_ref,
                dst_ref=recv_ref.at[landing],
                send_sem=send_sems.at[k - 1], recv_sem=recv_sems.at[k - 1],
                device_id=to, device_id_type=pl.DeviceIdType.MESH)

        own = pltpu.make_async_copy(send_ref.at[me] if per_dest else send_ref, recv_ref.at[me], local_sem)
        own.start()
        for k in range(1, N_DEV):
            copy(k, me).start()
        for k in range(1, N_DEV):
            copy(k, me).wait_send()
            copy(k, peer(k)[1]).wait_recv()
        own.wait()

    hbm = pl.BlockSpec(memory_space=pltpu.HBM)
    return pl.pallas_call(
        body, name=name, in_specs=[hbm], out_specs=hbm,
        out_shape=jax.ShapeDtypeStruct((N_DEV,) + tuple(shape), send.dtype),
        scratch_shapes=[pltpu.SemaphoreType.DMA((N_DEV - 1,)), pltpu.SemaphoreType.DMA((N_DEV - 1,)),
                        pltpu.SemaphoreType.DMA],
    )(send)


def _adamw(parts, w, m, v, name):
    n, cols = w.shape
    rows = min(n, 256)
    while n % rows:
        rows -= 8
    c1 = 1.0 - ADAM_B1 ** ADAM_STEP
    c2 = 1.0 - ADAM_B2 ** ADAM_STEP

    def body(p_ref, w_ref, m_ref, v_ref, g_ref, d_ref, nm_ref, nv_ref):
        g = p_ref[0]
        for s in range(1, N_DEV):
            g = g + p_ref[s]
        nm = ADAM_B1 * m_ref[...] + (1.0 - ADAM_B1) * g
        nv = ADAM_B2 * v_ref[...] + (1.0 - ADAM_B2) * (g * g)
        g_ref[...] = g
        nm_ref[...] = nm
        nv_ref[...] = nv
        d_ref[...] = -ADAM_LR * ((nm / c1) / (jnp.sqrt(nv / c2) + ADAM_EPS) + ADAM_WD * w_ref[...])

    blk = _row_spec(rows, cols)
    return pl.pallas_call(
        body, name=name, grid=(n // rows,),
        in_specs=[pl.BlockSpec((N_DEV, rows, cols), lambda i: (0, i, 0)), blk, blk, blk],
        out_specs=[blk] * 4,
        out_shape=[jax.ShapeDtypeStruct((n, cols), F32)] * 4,
        compiler_params=_params(1),
    )(parts, w, m, v)


SSM_KEYS = ("norm", "w_in", "a_re", "a_im", "log_step", "b_re", "b_im", "c_re", "c_im", "d", "w_glu", "b_glu", "w_out")
ATTN_KEYS = ("norm", "w_in", "sinks", "w_out")
LAYER_KEYS = (SSM_KEYS, ATTN_KEYS, SSM_KEYS, ATTN_KEYS)
BIG_KEYS = ("w_in", "w_glu", "w_out")
ATTN_SPLITS = (Q_DIM, KV_DIM, KV_DIM, D_MODEL)


def _rope_tables(t):
    pos = jnp.arange(t, dtype=F32)
    inv_freq = ROPE_THETA ** (-jnp.arange(0, HEAD_DIM, 2, dtype=F32) / HEAD_DIM)
    ang = pos[:, None] * inv_freq[None, :]
    cos, sin = jnp.cos(ang), jnp.sin(ang)
    return jnp.tile(jnp.concatenate([cos, cos], axis=1), (1, 2)), jnp.tile(jnp.concatenate([-sin, sin], axis=1), (1, 2))


def _d_lanes(d):
    return jnp.tile(d.reshape(SSM_GROUPS, 1, SSM_GROUP), (1, 1, S5_BLOCK))


def _ssm_layer_fwd(i, x, p, w):
    tag = "l%d_" % i
    mats, mats_vjp = jax.vjp(_s5_matrices, p["a_re"], p["a_im"], p["log_step"], p["b_re"], p["b_im"], p["c_re"], p["c_im"])
    tmt, bp_re, bp_im, cp_re, cp_im, ar, ai = mats
    mb = [m.astype(BF16) for m in (tmt, bp_re, bp_im, cp_re, cp_im)]
    u, gate = _inproj_fwd(x, p["norm"], w["w_in"], (D_MODEL, D_MODEL), None, tag + "inproj_fwd")
    u_g = _to_groups(u)
    xre, xim = _s5_block_inputs(u_g, mb[1], mb[2], tag + "s5_block_inputs")
    hre, him = _s5_scan_fwd(xre, xim, ar, ai, tag + "s5_scan_fwd")
    d_lanes = _d_lanes(p["d"])
    y = _to_tokens(_s5_outputs(u_g, hre, him, mb[0], mb[3], mb[4], d_lanes, tag + "s5_outputs"))
    xn = _ssm_out_fwd(y, gate, x, w["w_glu"], p["b_glu"], w["w_out"], tag + "out_fwd")
    return xn, (x, u_g, gate, y, hre, him, mb, ar, ai, d_lanes, mats_vjp)


def _ssm_layer_bwd(i, dxn, saved, p, w):
    tag = "l%d_" % i
    x, u_g, gate, y, hre, him, mb, ar, ai, d_lanes, mats_vjp = saved
    dy, dgate, dw_glu, db_glu, dw_out = _ssm_out_bwd(dxn, y, gate, w["w_glu"], p["b_glu"], w["w_out"], tag + "out_bwd")
    dy_g = _to_groups(dy)
    dhre, dhim = _s5_state_grads(dy_g, mb[3], mb[4], tag + "s5_state_grads")
    dxre, dxim, dar, dai = _s5_scan_bwd(dhre, dhim, hre, him, ar, ai, tag + "s5_scan_bwd")
    du_g, dtmt, dcre, dcim, dbre, dbim, dd = _s5_backward(
        dy_g, u_g, hre, him, dxre, dxim, mb[0], mb[1], mb[2], d_lanes, tag + "s5_backward")
    da_re, da_im, dlog_step, db_re, db_im, dc_re, dc_im = mats_vjp((dtmt, dbre, dbim, dcre, dcim, dar, dai))
    dd = dd.reshape(SSM_GROUPS, S5_BLOCK, SSM_GROUP).sum(axis=1).reshape(D_MODEL)
    dx, dw_in, dnorm = _inproj_bwd(x, p["norm"], w["w_in"], [_to_tokens(du_g), dgate], dxn, tag + "inproj_bwd")
    grads = dict(norm=dnorm.reshape(D_MODEL), w_in=dw_in, a_re=da_re, a_im=da_im, log_step=dlog_step, b_re=db_re,
                 b_im=db_im, c_re=dc_re, c_im=dc_im, d=dd, w_glu=dw_glu, b_glu=db_glu.reshape(D_MODEL), w_out=dw_out)
    return dx, grads


def _attn_layer_fwd(i, x, p, w, rope):
    tag = "l%d_" % i
    q, k, v, gate = _inproj_fwd(x, p["norm"], w["w_in"], ATTN_SPLITS, rope, tag + "inproj_fwd")
    o = _attn_fwd(q, k, v, p["sinks"], tag + "attn_fwd")
    xn = _attn_out_fwd(o, gate, x, w["w_out"], tag + "out_fwd")
    return xn, (x, q, k, v, gate, o)


def _attn_layer_bwd(i, dxn, saved, p, w, rope):
    tag = "l%d_" % i
    x, q, k, v, gate, o = saved
    do, dgate, dw_out = _attn_out_bwd(dxn, o, gate, w["w_out"], tag + "out_bwd")
    dq, dk, dv, dsinks = _attn_bwd(q, k, v, o, do, p["sinks"], rope, tag + "attn_bwd")
    dx, dw_in, dnorm = _inproj_bwd(x, p["norm"], w["w_in"], [dq, dk, dv, dgate], dxn, tag + "inproj_bwd")
    return dx, dict(norm=dnorm.reshape(D_MODEL), w_in=dw_in, sinks=dsinks[:, 0], w_out=dw_out)


def _local_step(x, target, small, big):
    rope = _rope_tables(x.shape[0])
    saved = []
    for i in range(4):
        if i % 2 == 0:
            x, s = _ssm_layer_fwd(i, x, small[i], big[i])
        else:
            x, s = _attn_layer_fwd(i, x, small[i], big[i], rope)
        saved.append(s)
    loss, dx, dfinal = _loss_head(x, small[4]["norm"], target, "loss_head")
    grads = [None] * 4 + [dict(norm=dfinal.reshape(D_MODEL))]
    for i in (3, 2, 1, 0):
        if i % 2 == 0:
            dx, grads[i] = _ssm_layer_bwd(i, dx, saved[i], small[i], big[i])
        else:
            dx, grads[i] = _attn_layer_bwd(i, dx, saved[i], small[i], big[i], rope)
    return loss[0, 0], dx, grads


def _shard_rows(key, a):
    return a.reshape(-1, D_MODEL)


def _owner_major(key, g):
    if key == "w_in":
        g = g.reshape(D_MODEL, N_DEV, -1).transpose(1, 0, 2)
    return g.reshape(N_DEV, -1, D_MODEL)


def _from_gathered(key, a, shard_shape):
    if key == "w_in":
        return a.reshape(N_DEV, D_MODEL, shard_shape[1]).transpose(1, 0, 2).reshape(D_MODEL, N_DEV * shard_shape[1])
    return a.reshape(N_DEV * shard_shape[0], shard_shape[1])


def _pad_rows(flat):
    n = flat.shape[0]
    rows = -(-n // (64 * D_MODEL)) * 64
    return jnp.pad(flat, (0, rows * D_MODEL - n)).reshape(rows, D_MODEL)


def kernel(*args):
    names = ["x"]
    layer_names = []
    for i, keys in enumerate(LAYER_KEYS):
        layer_names += ["l%d_%s" % (i, k) for k in keys]
    layer_names.append("final_norm")
    names += layer_names + ["loss_target"] + ["m_" + n for n in layer_names] + ["v_" + n for n in layer_names]
    given = dict(zip(names, args))
    big_names = [n for n in layer_names if n.split("_", 1)[1] in BIG_KEYS]
    small_names = [n for n in layer_names if n not in big_names]

    offsets, rows_at = {}, 0
    for n in big_names:
        offsets[n] = rows_at
        rows_at += given[n].size // D_MODEL
    local_rows = jnp.concatenate([_shard_rows(n, given[n]) for n in big_names], axis=0)
    gathered = _exchange(local_rows.astype(BF16), False, "gather_weights")
    big = [dict() for _ in range(4)]
    for n in big_names:
        layer, key = int(n[1]), n.split("_", 1)[1]
        rows = given[n].size // D_MODEL
        big[layer][key] = _from_gathered(key, gathered[:, offsets[n]:offsets[n] + rows, :], given[n].shape)
    small = [dict() for _ in range(5)]
    for n in small_names:
        if n == "final_norm":
            small[4]["norm"] = given[n]
        else:
            small[int(n[1])][n.split("_", 1)[1]] = given[n]

    loss, dx, grads = _local_step(given["x"][0], given["loss_target"][0], small, big)
    loss = lax.psum(loss, ("x", "y", "c"))

    def grad_of(n):
        return grads[4]["norm"] if n == "final_norm" else grads[int(n[1])][n.split("_", 1)[1]]

    send = jnp.concatenate([_owner_major(n.split("_", 1)[1], grad_of(n)) for n in big_names], axis=1)
    parts = _exchange(send, True, "scatter_weight_grads")
    cat = lambda pre: jnp.concatenate([_shard_rows(n, given[pre + n]) for n in big_names], axis=0)
    g_big, d_big, m_big, v_big = _adamw(parts, cat(""), cat("m_"), cat("v_"), "adamw_matrices")

    flat = lambda f: _pad_rows(jnp.concatenate([f(n).reshape(-1) for n in small_names]))
    parts = _exchange(flat(grad_of), False, "gather_small_grads")
    g_small, d_small, m_small, v_small = _adamw(
        parts, flat(lambda n: given[n]), flat(lambda n: given["m_" + n]), flat(lambda n: given["v_" + n]), "adamw_small")

    outs = {}
    for tag, a_big, a_small in (("grad_", g_big, g_small), ("delta_", d_big, d_small),
                                ("new_m_", m_big, m_small), ("new_v_", v_big, v_small)):
        for n in big_names:
            rows = given[n].size // D_MODEL
            outs[tag + n] = a_big[offsets[n]:offsets[n] + rows].reshape(given[n].shape)
        a_flat, at = a_small.reshape(-1), 0
        for n in small_names:
            outs[tag + n] = a_flat[at:at + given[n].size].reshape(given[n].shape)
            at += given[n].size
    result = [loss, dx[None]]
    for tag in ("grad_", "delta_", "new_m_", "new_v_"):
        result += [outs[tag + n] for n in layer_names]
    return tuple(result)
```

```python
import functools
import math

import jax
import jax.numpy as jnp
from jax import lax
from jax.experimental import pallas as pl
from jax.experimental.pallas import tpu as pltpu

F32 = jnp.float32
BF16 = jnp.bfloat16

D_MODEL = 1024
SSM_GROUP = 16
SSM_GROUPS = D_MODEL // SSM_GROUP
SSM_STATE = 64
S5_BLOCK = 16
S5_LANES = S5_BLOCK * SSM_GROUP
HEAD_DIM = 64
N_Q_HEADS = 16
N_KV_HEADS = 2
GQA = N_Q_HEADS // N_KV_HEADS
Q_DIM = N_Q_HEADS * HEAD_DIM
KV_DIM = N_KV_HEADS * HEAD_DIM
ATTN_BLOCK = 128
ROPE_THETA = 10000.0
NORM_EPS = 1e-5
NEG_INF = -1e30
ATTN_SCALE = HEAD_DIM ** -0.5
N_DEV = 8

ADAM_LR = 0.001
ADAM_B1 = 0.9
ADAM_B2 = 0.999
ADAM_EPS = 1e-08
ADAM_WD = 0.01
ADAM_STEP = 10

VMEM_LIMIT = 56 * 1024 * 1024
ROWS_FWD = 512
ROWS_BWD = 256

NT = (((1,), (1,)), ((), ()))
TN = (((0,), (0,)), ((), ()))


def _params(n_grid):
    return pltpu.CompilerParams(dimension_semantics=("arbitrary",) * n_grid, vmem_limit_bytes=VMEM_LIMIT)


def _dot(a, b):
    return jnp.dot(a, b, preferred_element_type=F32)


def _dot_nt(a, b):
    return lax.dot_general(a, b, NT, preferred_element_type=F32)


def _dot_tn(a, b):
    return lax.dot_general(a, b, TN, preferred_element_type=F32)


def _sigmoid(x):
    return 1.0 / (1.0 + jnp.exp(-x))


_GELU_K = math.sqrt(2.0 / math.pi)


def _gelu(x):
    return x * (0.5 * (1.0 + jnp.tanh(_GELU_K * (x + 0.044715 * (x * x * x)))))


def _gelu_grad(x):
    t = jnp.tanh(_GELU_K * (x + 0.044715 * (x * x * x)))
    return 0.5 * (1.0 + t) + 0.5 * x * (1.0 - t * t) * (_GELU_K * (1.0 + 3.0 * 0.044715 * (x * x)))


def _row_spec(rows, cols):
    return pl.BlockSpec((rows, cols), lambda i: (i, 0))


def _const_spec(shape):
    zeros = (0,) * len(shape)
    return pl.BlockSpec(shape, lambda i: zeros)


def _rope_apply(t, cos, sin_signed, sign):
    lane = lax.broadcasted_iota(jnp.int32, (1, 128), 1)
    first_half = (lane % HEAD_DIM) < (HEAD_DIM // 2)
    out = []
    for j in range(t.shape[1] // 128):
        tj = t[:, 128 * j:128 * (j + 1)]
        partner = jnp.where(first_half, pltpu.roll(tj, 128 - HEAD_DIM // 2, 1), pltpu.roll(tj, HEAD_DIM // 2, 1))
        out.append(tj * cos + sign * (partner * sin_signed))
    return out[0] if len(out) == 1 else jnp.concatenate(out, axis=1)


def _inproj_fwd(x, norm, w, splits, rope, name):
    t = x.shape[0]
    n = w.shape[1]
    rows = ROWS_FWD

    def body(*refs):
        if rope is None:
            x_ref, n_ref, w_ref = refs[:3]
            outs = refs[3:]
        else:
            x_ref, n_ref, w_ref, cos_ref, sin_ref = refs[:5]
            outs = refs[5:]
        xv = x_ref[...]
        rstd = lax.rsqrt(jnp.mean(xv * xv, axis=-1, keepdims=True) + NORM_EPS)
        h = (xv * rstd) * n_ref[...]
        proj = _dot(h.astype(BF16), w_ref[...])
        off = 0
        for i, width in enumerate(splits):
            piece = proj[:, off:off + width]
            if rope is not None and i < 2:
                piece = _rope_apply(piece, cos_ref[...], sin_ref[...], 1.0)
            outs[i][...] = piece
            off += width

    in_specs = [_row_spec(rows, D_MODEL), _const_spec((1, D_MODEL)), _const_spec((D_MODEL, n))]
    args = [x, norm.reshape(1, D_MODEL), w]
    if rope is not None:
        in_specs += [_row_spec(rows, 128), _row_spec(rows, 128)]
        args += list(rope)
    return pl.pallas_call(
        body, name=name, grid=(t // rows,), in_specs=in_specs,
        out_specs=[_row_spec(rows, width) for width in splits],
        out_shape=[jax.ShapeDtypeStruct((t, width), F32) for width in splits],
        compiler_params=_params(1),
    )(*args)


def _inproj_bwd(x, norm, w, dpieces, dxn, name):
    t = x.shape[0]
    n = w.shape[1]
    rows = ROWS_BWD
    widths = [p.shape[1] for p in dpieces]
    k = len(dpieces)

    def body(*refs):
        x_ref, n_ref, w_ref, dxn_ref = refs[:4]
        d_refs = refs[4:4 + k]
        dx_ref, dw_ref, dn_ref = refs[4 + k:]
        i = pl.program_id(0)
        xv = x_ref[...]
        rstd = lax.rsqrt(jnp.mean(xv * xv, axis=-1, keepdims=True) + NORM_EPS)
        xhat = xv * rstd
        h = xhat * n_ref[...]
        dproj = [r[...].astype(BF16) for r in d_refs]
        dproj = dproj[0] if k == 1 else jnp.concatenate(dproj, axis=1)
        dh = _dot_nt(dproj, w_ref[...])
        dw = _dot_tn(h.astype(BF16), dproj)
        dn = jnp.sum(dh * xhat, axis=0, keepdims=True)

        @pl.when(i == 0)
        def _():
            dw_ref[...] = dw
            dn_ref[...] = dn

        @pl.when(i > 0)
        def _():
            dw_ref[...] += dw
            dn_ref[...] += dn

        dxhat = dh * n_ref[...]
        dx_ref[...] = rstd * (dxhat - xhat * jnp.mean(dxhat * xhat, axis=-1, keepdims=True)) + dxn_ref[...]

    return pl.pallas_call(
        body, name=name, grid=(t // rows,),
        in_specs=[_row_spec(rows, D_MODEL), _const_spec((1, D_MODEL)), _const_spec((D_MODEL, n)),
                  _row_spec(rows, D_MODEL)] + [_row_spec(rows, width) for width in widths],
        out_specs=[_row_spec(rows, D_MODEL), _const_spec((D_MODEL, n)), _const_spec((1, D_MODEL))],
        out_shape=[jax.ShapeDtypeStruct((t, D_MODEL), F32), jax.ShapeDtypeStruct((D_MODEL, n), F32),
                   jax.ShapeDtypeStruct((1, D_MODEL), F32)],
        compiler_params=_params(1),
    )(x, norm.reshape(1, D_MODEL), w, dxn, *dpieces)


def _s5_matrices(a_re, a_im, log_step, b_re, b_im, c_re, c_im):
    hi = lax.Precision.HIGHEST
    r = S5_BLOCK
    step = jnp.exp(log_step)[:, None]
    lr, li = a_re * step, a_im * step
    k = jnp.arange(r + 1, dtype=F32)[:, None, None]
    mag = jnp.exp(lr[None] * k)
    pr, pi = mag * jnp.cos(li[None] * k), mag * jnp.sin(li[None] * k)
    nr, ni = pr[1] - 1.0, pi[1]
    den = a_re * a_re + a_im * a_im
    qr, qi = (nr * a_re + ni * a_im) / den, (ni * a_re - nr * a_im) / den
    bbr = qr[..., None] * b_re - qi[..., None] * b_im
    bbi = qr[..., None] * b_im + qi[..., None] * b_re
    wr = c_re[None] * pr[:, :, None, :] - c_im[None] * pi[:, :, None, :]
    wi = c_re[None] * pi[:, :, None, :] + c_im[None] * pr[:, :, None, :]
    kern = (jnp.einsum("tgop,gpi->tgoi", wr[:r], bbr, precision=hi)
            - jnp.einsum("tgop,gpi->tgoi", wi[:r], bbi, precision=hi))
    tau = jnp.arange(r)[:, None, None]
    shift = (jnp.arange(r)[None, None, :] - jnp.arange(r)[None, :, None] == tau).astype(F32)
    tm = jnp.einsum("tsr,tgoi->grosi", shift, kern, precision=hi).reshape(SSM_GROUPS, S5_LANES, S5_LANES)
    cpt = jnp.concatenate([wr[1:], -wi[1:]], axis=-1).transpose(1, 0, 2, 3).reshape(SSM_GROUPS, S5_LANES, 2 * SSM_STATE)
    prs, pis = pr[r - 1::-1][:r], pi[r - 1::-1][:r]
    bp_re = prs[:, :, :, None] * bbr[None] - pis[:, :, :, None] * bbi[None]
    bp_im = prs[:, :, :, None] * bbi[None] + pis[:, :, :, None] * bbr[None]
    bpt = jnp.concatenate([bp_re, bp_im], axis=2).transpose(1, 2, 0, 3).reshape(SSM_GROUPS, 2 * SSM_STATE, S5_LANES)
    ar = pr[r].reshape(1, SSM_GROUPS * SSM_STATE)
    ai = pi[r].reshape(1, SSM_GROUPS * SSM_STATE)
    return tm, cpt, bpt, ar, ai


S5_OCTET = 128 // SSM_GROUP
S5_STEPS = SSM_GROUPS // S5_OCTET


def _oct_spec(t):
    return pl.BlockSpec((t, 128), lambda j: (0, j))


def _state_spec(nb):
    return pl.BlockSpec((nb, S5_OCTET * SSM_STATE), lambda j: (0, j))


def _gmat_spec(a, b):
    return pl.BlockSpec((S5_OCTET, a, b), lambda j: (j, 0, 0))


def _block_rows(ref, nb):
    return [ref[pl.ds(r, nb, stride=S5_BLOCK), :] for r in range(S5_BLOCK)]


def _group_cols(pieces_t, g):
    return jnp.concatenate([p[SSM_GROUP * g:SSM_GROUP * (g + 1)] for p in pieces_t], axis=0)


def _state_cols(re_t, im_t, g):
    return jnp.concatenate([re_t[SSM_STATE * g:SSM_STATE * (g + 1)], im_t[SSM_STATE * g:SSM_STATE * (g + 1)]], axis=0)


def _s5_project(a, mat, name):
    t = a.shape[0]
    nb = t // S5_BLOCK

    def body(a_ref, m_ref, re_ref, im_ref):
        at = [p.T for p in _block_rows(a_ref, nb)]
        for pair in range(S5_OCTET // 2):
            xs = [_dot(m_ref[2 * pair + k], _group_cols(at, 2 * pair + k).astype(BF16)) for k in (0, 1)]
            lanes = slice(128 * pair, 128 * (pair + 1))
            re_ref[:, lanes] = jnp.concatenate([xs[0][:SSM_STATE], xs[1][:SSM_STATE]], axis=0).T
            im_ref[:, lanes] = jnp.concatenate([xs[0][SSM_STATE:], xs[1][SSM_STATE:]], axis=0).T

    return pl.pallas_call(
        body, name=name, grid=(S5_STEPS,),
        in_specs=[_oct_spec(t), _gmat_spec(2 * SSM_STATE, S5_LANES)],
        out_specs=[_state_spec(nb), _state_spec(nb)],
        out_shape=[jax.ShapeDtypeStruct((nb, SSM_GROUPS * SSM_STATE), F32)] * 2,
        compiler_params=_params(1),
    )(a, mat)


_SCAN_LANES = 1024


def _s5_scan_fwd(xre, xim, ar, ai, name):
    nb = xre.shape[0]
    col = pl.BlockSpec((nb, _SCAN_LANES), lambda j: (0, j))
    par = pl.BlockSpec((1, _SCAN_LANES), lambda j: (0, j))

    def body(xre_ref, xim_ref, ar_ref, ai_ref, hre_ref, him_ref):
        a_r, a_i = ar_ref[...], ai_ref[...]

        def step(b, carry):
            hr, hi = carry
            hre_ref[pl.ds(b, 1), :] = hr
            him_ref[pl.ds(b, 1), :] = hi
            xr, xi = xre_ref[pl.ds(b, 1), :], xim_ref[pl.ds(b, 1), :]
            return a_r * hr - a_i * hi + xr, a_r * hi + a_i * hr + xi

        zero = jnp.zeros((1, _SCAN_LANES), F32)
        lax.fori_loop(0, nb, step, (zero, zero))

    return pl.pallas_call(
        body, name=name, grid=(xre.shape[1] // _SCAN_LANES,),
        in_specs=[col, col, par, par], out_specs=[col, col],
        out_shape=[jax.ShapeDtypeStruct(xre.shape, F32)] * 2,
        compiler_params=_params(1),
    )(xre, xim, ar, ai)


def _s5_scan_bwd(dhre, dhim, hre, him, ar, ai, name):
    nb = dhre.shape[0]
    col = pl.BlockSpec((nb, _SCAN_LANES), lambda j: (0, j))
    par = pl.BlockSpec((1, _SCAN_LANES), lambda j: (0, j))

    def body(dhre_ref, dhim_ref, hre_ref, him_ref, ar_ref, ai_ref, dxre_ref, dxim_ref, dar_ref, dai_ref):
        a_r, a_i = ar_ref[...], ai_ref[...]

        def step(s, carry):
            gr, gi, dar, dai = carry
            b = nb - 1 - s
            dxre_ref[pl.ds(b, 1), :] = gr
            dxim_ref[pl.ds(b, 1), :] = gi
            hr, hi = hre_ref[pl.ds(b, 1), :], him_ref[pl.ds(b, 1), :]
            dar = dar + (hr * gr + hi * gi)
            dai = dai + (hr * gi - hi * gr)
            dr, di = dhre_ref[pl.ds(b, 1), :], dhim_ref[pl.ds(b, 1), :]
            return dr + (a_r * gr + a_i * gi), di + (a_r * gi - a_i * gr), dar, dai

        zero = jnp.zeros((1, _SCAN_LANES), F32)
        _, _, dar, dai = lax.fori_loop(0, nb, step, (zero, zero, zero, zero))
        dar_ref[...] = dar
        dai_ref[...] = dai

    return pl.pallas_call(
        body, name=name, grid=(dhre.shape[1] // _SCAN_LANES,),
        in_specs=[col, col, col, col, par, par], out_specs=[col, col, par, par],
        out_shape=[jax.ShapeDtypeStruct(dhre.shape, F32)] * 2 + [jax.ShapeDtypeStruct(ar.shape, F32)] * 2,
        compiler_params=_params(1),
    )(dhre, dhim, hre, him, ar, ai)


def _s5_outputs(u, hre, him, tm, cpt, d, name):
    t = u.shape[0]
    nb = t // S5_BLOCK

    def body(u_ref, hre_ref, him_ref, tm_ref, cpt_ref, d_ref, y_ref):
        u_rows = _block_rows(u_ref, nb)
        ut = [p.T for p in u_rows]
        hre_t, him_t = hre_ref[...].T, him_ref[...].T
        yts = []
        for g in range(S5_OCTET):
            yts.append(_dot(tm_ref[g], _group_cols(ut, g).astype(BF16))
                       + _dot(cpt_ref[g], _state_cols(hre_t, him_t, g).astype(BF16)))
        for r in range(S5_BLOCK):
            rows = jnp.concatenate([yt[SSM_GROUP * r:SSM_GROUP * (r + 1)] for yt in yts], axis=0)
            y_ref[pl.ds(r, nb, stride=S5_BLOCK), :] = rows.T + d_ref[...] * u_rows[r]

    return pl.pallas_call(
        body, name=name, grid=(S5_STEPS,),
        in_specs=[_oct_spec(t), _state_spec(nb), _state_spec(nb), _gmat_spec(S5_LANES, S5_LANES),
                  _gmat_spec(S5_LANES, 2 * SSM_STATE), _oct_spec(1)],
        out_specs=_oct_spec(t),
        out_shape=jax.ShapeDtypeStruct(u.shape, F32),
        compiler_params=_params(1),
    )(u, hre, him, tm, cpt, d.reshape(1, D_MODEL))


def _s5_backward(dy, u, hre, him, dxre, dxim, tmt, bp, d, name):
    t = u.shape[0]
    nb = t // S5_BLOCK

    def body(dy_ref, u_ref, hre_ref, him_ref, dxre_ref, dxim_ref, tmt_ref, bp_ref, d_ref,
             du_ref, dtm_ref, dcpt_ref, dbpt_ref, dd_ref):
        dy_rows, u_rows = _block_rows(dy_ref, nb), _block_rows(u_ref, nb)
        dyt, ut = [p.T for p in dy_rows], [p.T for p in u_rows]
        hre_t, him_t = hre_ref[...].T, him_ref[...].T
        dxre_t, dxim_t = dxre_ref[...].T, dxim_ref[...].T
        duts = []
        for g in range(S5_OCTET):
            dyg, ug = _group_cols(dyt, g).astype(BF16), _group_cols(ut, g).astype(BF16)
            hg = _state_cols(hre_t, him_t, g).astype(BF16)
            dxg = _state_cols(dxre_t, dxim_t, g).astype(BF16)
            duts.append(_dot(tmt_ref[g], dyg) + _dot(bp_ref[g], dxg))
            dtm_ref[g] = _dot_nt(dyg, ug)
            dcpt_ref[g] = _dot_nt(dyg, hg)
            dbpt_ref[g] = _dot_nt(dxg, ug)
        dd = jnp.zeros((1, 128), F32)
        for r in range(S5_BLOCK):
            rows = jnp.concatenate([dut[SSM_GROUP * r:SSM_GROUP * (r + 1)] for dut in duts], axis=0)
            du_ref[pl.ds(r, nb, stride=S5_BLOCK), :] = rows.T + d_ref[...] * dy_rows[r]
            dd = dd + jnp.sum(dy_rows[r] * u_rows[r], axis=0, keepdims=True)
        dd_ref[...] = dd

    return pl.pallas_call(
        body, name=name, grid=(S5_STEPS,),
        in_specs=[_oct_spec(t), _oct_spec(t), _state_spec(nb), _state_spec(nb), _state_spec(nb), _state_spec(nb),
                  _gmat_spec(S5_LANES, S5_LANES), _gmat_spec(S5_LANES, 2 * SSM_STATE), _oct_spec(1)],
        out_specs=[_oct_spec(t), _gmat_spec(S5_LANES, S5_LANES), _gmat_spec(S5_LANES, 2 * SSM_STATE),
                   _gmat_spec(2 * SSM_STATE, S5_LANES), _oct_spec(1)],
        out_shape=[jax.ShapeDtypeStruct(u.shape, F32),
                   jax.ShapeDtypeStruct((SSM_GROUPS, S5_LANES, S5_LANES), F32),
                   jax.ShapeDtypeStruct((SSM_GROUPS, S5_LANES, 2 * SSM_STATE), F32),
                   jax.ShapeDtypeStruct((SSM_GROUPS, 2 * SSM_STATE, S5_LANES), F32),
                   jax.ShapeDtypeStruct((1, D_MODEL), F32)],
        compiler_params=_params(1),
    )(dy, u, hre, him, dxre, dxim, tmt, bp, d.reshape(1, D_MODEL))


def _ssm_out_fwd(y, gate, x, w_glu, b_glu, w_out, name):
    t = x.shape[0]
    rows = ROWS_FWD

    def body(y_ref, g_ref, x_ref, wg_ref, bg_ref, wo_ref, o_ref):
        z0 = _gelu(y_ref[...])
        s = _dot(z0.astype(BF16), wg_ref[...]) + bg_ref[...]
        gate_v = g_ref[...]
        a = (z0 * _sigmoid(s)) * (gate_v * _sigmoid(gate_v))
        o_ref[...] = x_ref[...] + _dot(a.astype(BF16), wo_ref[...])

    return pl.pallas_call(
        body, name=name, grid=(t // rows,),
        in_specs=[_row_spec(rows, D_MODEL)] * 3 + [_const_spec((D_MODEL, D_MODEL)), _const_spec((1, D_MODEL)),
                                                   _const_spec((D_MODEL, D_MODEL))],
        out_specs=_row_spec(rows, D_MODEL),
        out_shape=jax.ShapeDtypeStruct((t, D_MODEL), F32),
        compiler_params=_params(1),
    )(y, gate, x, w_glu, b_glu.reshape(1, D_MODEL), w_out)


def _ssm_out_bwd(dxn, y, gate, w_glu, b_glu, w_out, name):
    t = y.shape[0]
    rows = ROWS_BWD

    def body(dxn_ref, y_ref, g_ref, wg_ref, bg_ref, wo_ref, dy_ref, dg_ref, dwg_ref, dbg_ref, dwo_ref):
        i = pl.program_id(0)
        yv = y_ref[...]
        z0 = _gelu(yv)
        z0b = z0.astype(BF16)
        sg = _sigmoid(_dot(z0b, wg_ref[...]) + bg_ref[...])
        z = z0 * sg
        gate_v = g_ref[...]
        sgg = _sigmoid(gate_v)
        silu = gate_v * sgg
        a = z * silu
        dob = dxn_ref[...].astype(BF16)
        da = _dot_nt(dob, wo_ref[...])
        dwo = _dot_tn(a.astype(BF16), dob)
        dz = da * silu
        dg_ref[...] = da * z * (sgg * (1.0 + gate_v * (1.0 - sgg)))
        ds = dz * z0 * (sg * (1.0 - sg))
        dsb = ds.astype(BF16)
        dz0 = dz * sg + _dot_nt(dsb, wg_ref[...])
        dwg = _dot_tn(z0b, dsb)
        dbg = jnp.sum(ds, axis=0, keepdims=True)
        dy_ref[...] = dz0 * _gelu_grad(yv)

        @pl.when(i == 0)
        def _():
            dwo_ref[...] = dwo
            dwg_ref[...] = dwg
            dbg_ref[...] = dbg

        @pl.when(i > 0)
        def _():
            dwo_ref[...] += dwo
            dwg_ref[...] += dwg
            dbg_ref[...] += dbg

    sq = _const_spec((D_MODEL, D_MODEL))
    vec = _const_spec((1, D_MODEL))
    return pl.pallas_call(
        body, name=name, grid=(t // rows,),
        in_specs=[_row_spec(rows, D_MODEL)] * 3 + [sq, vec, sq],
        out_specs=[_row_spec(rows, D_MODEL), _row_spec(rows, D_MODEL), sq, vec, sq],
        out_shape=[jax.ShapeDtypeStruct((t, D_MODEL), F32)] * 2 + [
            jax.ShapeDtypeStruct((D_MODEL, D_MODEL), F32), jax.ShapeDtypeStruct((1, D_MODEL), F32),
            jax.ShapeDtypeStruct((D_MODEL, D_MODEL), F32)],
        compiler_params=_params(1),
    )(dxn, y, gate, w_glu, b_glu.reshape(1, D_MODEL), w_out)


KV_LANES = GQA * ATTN_BLOCK


def _attn_bias(block_is_first):
    kj = lax.broadcasted_iota(jnp.int32, (2 * ATTN_BLOCK, ATTN_BLOCK), 0)
    qi = lax.broadcasted_iota(jnp.int32, (2 * ATTN_BLOCK, ATTN_BLOCK), 1)
    dist = qi + ATTN_BLOCK - kj
    valid = (dist >= 0) & (dist < ATTN_BLOCK) & (jnp.logical_not(block_is_first) | (kj >= ATTN_BLOCK))
    return jnp.tile(jnp.where(valid, 0.0, NEG_INF).astype(F32), (1, GQA))


def _head_cols(a_t, kvh):
    heads = range(kvh * GQA, (kvh + 1) * GQA)
    return jnp.concatenate([a_t[HEAD_DIM * h:HEAD_DIM * (h + 1)] for h in heads], axis=1)


def _head_rows(a_cols):
    stacked = jnp.concatenate([a_cols[:, ATTN_BLOCK * g:ATTN_BLOCK * (g + 1)] for g in range(GQA)], axis=0)
    return stacked.T


def _kv_rows(prev_ref, cur_ref, kvh):
    lanes = slice(HEAD_DIM * kvh, HEAD_DIM * (kvh + 1))
    return jnp.concatenate([prev_ref[:, lanes], cur_ref[:, lanes]], axis=0).astype(BF16)


def _kv_cols(prev_t, cur_t, kvh):
    rows = slice(HEAD_DIM * kvh, HEAD_DIM * (kvh + 1))
    return jnp.concatenate([prev_t[rows], cur_t[rows]], axis=1).astype(BF16)


def _attn_probs(kk, q_cols, sink_row, bias):
    s = _dot(kk, q_cols) * ATTN_SCALE + bias
    m = jnp.maximum(jnp.max(s, axis=0, keepdims=True), sink_row)
    p = jnp.exp(s - m)
    e_sink = jnp.exp(sink_row - m)
    inv = 1.0 / (jnp.sum(p, axis=0, keepdims=True) + e_sink)
    return p * inv, e_sink * inv


def _sink_cols(sinks):
    return jnp.repeat(sinks, ATTN_BLOCK).reshape(N_KV_HEADS, 1, KV_LANES)


def _attn_fwd(q, k, v, sinks, name):
    t = q.shape[0]
    nblk = t // ATTN_BLOCK

    def body(s_ref, q_ref, kc_ref, kp_ref, vc_ref, vp_ref, o_ref):
        bias = _attn_bias(pl.program_id(0) == 0)
        q_t = q_ref[...].T
        vp_t, vc_t = vp_ref[...].T, vc_ref[...].T
        for kvh in range(N_KV_HEADS):
            p, _ = _attn_probs(_kv_rows(kp_ref, kc_ref, kvh), _head_cols(q_t, kvh).astype(BF16), s_ref[kvh], bias)
            o_cols = _dot(_kv_cols(vp_t, vc_t, kvh), p.astype(BF16))
            o_ref[:, GQA * HEAD_DIM * kvh:GQA * HEAD_DIM * (kvh + 1)] = _head_rows(o_cols)

    cur = lambda i: (i, 0)
    prev = lambda i: (jnp.maximum(i - 1, 0), 0)
    return pl.pallas_call(
        body, name=name, grid=(nblk,),
        in_specs=[_const_spec((N_KV_HEADS, 1, KV_LANES)),
                  pl.BlockSpec((ATTN_BLOCK, Q_DIM), cur),
                  pl.BlockSpec((ATTN_BLOCK, KV_DIM), cur), pl.BlockSpec((ATTN_BLOCK, KV_DIM), prev),
                  pl.BlockSpec((ATTN_BLOCK, KV_DIM), cur), pl.BlockSpec((ATTN_BLOCK, KV_DIM), prev)],
        out_specs=pl.BlockSpec((ATTN_BLOCK, Q_DIM), cur),
        out_shape=jax.ShapeDtypeStruct((t, Q_DIM), F32),
        compiler_params=_params(1),
    )(_sink_cols(sinks), q, k, k, v, v)


def _attn_bwd(q, k, v, o, do, sinks, rope, name):
    t = q.shape[0]
    nblk = t // ATTN_BLOCK

    def body(s_ref, q_ref, o_ref, do_ref, kp_ref, kc_ref, vp_ref, vc_ref, cosq_ref, sinq_ref, cosk_ref, sinkey_ref,
             dq_ref, dk_ref, dv_ref, ds_ref, new_k, new_v, wait_k, wait_v):
        n = pl.program_id(0)

        @pl.when(n == 0)
        def _():
            ds_ref[...] = jnp.zeros_like(ds_ref)
            wait_k[...] = jnp.zeros_like(wait_k)
            wait_v[...] = jnp.zeros_like(wait_v)

        @pl.when(n < nblk)
        def _():
            bias = _attn_bias(n == 0)
            q_t, o_t, do_t = q_ref[...].T, o_ref[...].T, do_ref[...].T
            kp_t, kc_t = kp_ref[...].T, kc_ref[...].T
            for kvh in range(N_KV_HEADS):
                q_cols = _head_cols(q_t, kvh).astype(BF16)
                do_cols = _head_cols(do_t, kvh)
                delta = jnp.sum(do_cols * _head_cols(o_t, kvh), axis=0, keepdims=True)
                do_cols = do_cols.astype(BF16)
                p, p_sink = _attn_probs(_kv_rows(kp_ref, kc_ref, kvh), q_cols, s_ref[kvh], bias)
                dp = _dot(_kv_rows(vp_ref, vc_ref, kvh), do_cols)
                ds = (p * (dp - delta) * ATTN_SCALE).astype(BF16)
                lanes = slice(GQA * HEAD_DIM * kvh, GQA * HEAD_DIM * (kvh + 1))
                dq_ref[:, lanes] = _head_rows(_dot(_kv_cols(kp_t, kc_t, kvh), ds))
                head = slice(HEAD_DIM * kvh, HEAD_DIM * (kvh + 1))
                new_k[:, head] = _dot_nt(ds, q_cols)
                new_v[:, head] = _dot_nt(p.astype(BF16), do_cols)
                ds_ref[kvh] += -(p_sink * delta)
            dq_ref[...] = _rope_apply(dq_ref[...], cosq_ref[...], sinq_ref[...], -1.0)

        @pl.when(n == nblk)
        def _():
            new_k[...] = jnp.zeros_like(new_k)
            new_v[...] = jnp.zeros_like(new_v)

        dk_ref[...] = _rope_apply(wait_k[...] + new_k[:ATTN_BLOCK], cosk_ref[...], sinkey_ref[...], -1.0)
        dv_ref[...] = wait_v[...] + new_v[:ATTN_BLOCK]
        wait_k[...] = new_k[ATTN_BLOCK:]
        wait_v[...] = new_v[ATTN_BLOCK:]

    cur = lambda i: (jnp.minimum(i, nblk - 1), 0)
    prev = lambda i: (jnp.maximum(i - 1, 0), 0)
    qs = lambda f: pl.BlockSpec((ATTN_BLOCK, Q_DIM), f)
    ks = lambda f: pl.BlockSpec((ATTN_BLOCK, KV_DIM), f)
    sink_spec = _const_spec((N_KV_HEADS, 1, KV_LANES))
    return pl.pallas_call(
        body, name=name, grid=(nblk + 1,),
        in_specs=[sink_spec, qs(cur), qs(cur), qs(cur), ks(prev), ks(cur), ks(prev), ks(cur),
                  ks(cur), ks(cur), ks(prev), ks(prev)],
        out_specs=[qs(cur), ks(prev), ks(prev), sink_spec],
        out_shape=[jax.ShapeDtypeStruct((t, Q_DIM), F32), jax.ShapeDtypeStruct((t, KV_DIM), F32),
                   jax.ShapeDtypeStruct((t, KV_DIM), F32), jax.ShapeDtypeStruct((N_KV_HEADS, 1, KV_LANES), F32)],
        scratch_shapes=[pltpu.VMEM((2 * ATTN_BLOCK, KV_DIM), F32), pltpu.VMEM((2 * ATTN_BLOCK, KV_DIM), F32),
                        pltpu.VMEM((ATTN_BLOCK, KV_DIM), F32), pltpu.VMEM((ATTN_BLOCK, KV_DIM), F32)],
        compiler_params=_params(1),
    )(_sink_cols(sinks), q, o, do, k, k, v, v, rope[0], rope[1], rope[0], rope[1])


def _attn_out_fwd(o, gate, x, w_out, name):
    t = x.shape[0]
    rows = ROWS_FWD

    def body(o_ref, g_ref, x_ref, wo_ref, xn_ref):
        gate_v = g_ref[...]
        a = o_ref[...] * (gate_v * _sigmoid(gate_v))
        xn_ref[...] = x_ref[...] + _dot(a.astype(BF16), wo_ref[...])

    return pl.pallas_call(
        body, name=name, grid=(t // rows,),
        in_specs=[_row_spec(rows, D_MODEL)] * 3 + [_const_spec((D_MODEL, D_MODEL))],
        out_specs=_row_spec(rows, D_MODEL),
        out_shape=jax.ShapeDtypeStruct((t, D_MODEL), F32),
        compiler_params=_params(1),
    )(o, gate, x, w_out)


def _attn_out_bwd(dxn, o, gate, w_out, name):
    t = o.shape[0]
    rows = ROWS_BWD

    def body(dxn_ref, o_ref, g_ref, wo_ref, do_ref, dg_ref, dwo_ref):
        i = pl.program_id(0)
        gate_v, ov = g_ref[...], o_ref[...]
        sgg = _sigmoid(gate_v)
        silu = gate_v * sgg
        dob = dxn_ref[...].astype(BF16)
        da = _dot_nt(dob, wo_ref[...])
        dwo = _dot_tn((ov * silu).astype(BF16), dob)
        do_ref[...] = da * silu
        dg_ref[...] = da * ov * (sgg * (1.0 + gate_v * (1.0 - sgg)))

        @pl.when(i == 0)
        def _():
            dwo_ref[...] = dwo

        @pl.when(i > 0)
        def _():
            dwo_ref[...] += dwo

    sq = _const_spec((D_MODEL, D_MODEL))
    return pl.pallas_call(
        body, name=name, grid=(t // rows,),
        in_specs=[_row_spec(rows, D_MODEL)] * 3 + [sq],
        out_specs=[_row_spec(rows, D_MODEL), _row_spec(rows, D_MODEL), sq],
        out_shape=[jax.ShapeDtypeStruct((t, D_MODEL), F32)] * 2 + [jax.ShapeDtypeStruct((D_MODEL, D_MODEL), F32)],
        compiler_params=_params(1),
    )(dxn, o, gate, w_out)


def _loss_head(x, norm, target, name):
    t = x.shape[0]
    rows = ROWS_FWD

    def body(x_ref, n_ref, t_ref, loss_ref, dx_ref, dn_ref):
        i = pl.program_id(0)
        xv = x_ref[...]
        rstd = lax.rsqrt(jnp.mean(xv * xv, axis=-1, keepdims=True) + NORM_EPS)
        xhat = xv * rstd
        err = xhat * n_ref[...] - t_ref[...]
        part = 0.5 * jnp.sum(jnp.mean(err * err, axis=-1, keepdims=True), axis=0, keepdims=True)
        dy = err * (1.0 / D_MODEL)
        dn = jnp.sum(dy * xhat, axis=0, keepdims=True)
        dxhat = dy * n_ref[...]
        dx_ref[...] = rstd * (dxhat - xhat * jnp.mean(dxhat * xhat, axis=-1, keepdims=True))

        @pl.when(i == 0)
        def _():
            loss_ref[...] = jnp.zeros((8, 128), F32) + part
            dn_ref[...] = dn

        @pl.when(i > 0)
        def _():
            loss_ref[...] += part
            dn_ref[...] += dn

    return pl.pallas_call(
        body, name=name, grid=(t // rows,),
        in_specs=[_row_spec(rows, D_MODEL), _const_spec((1, D_MODEL)), _row_spec(rows, D_MODEL)],
        out_specs=[_const_spec((8, 128)), _row_spec(rows, D_MODEL), _const_spec((1, D_MODEL))],
        out_shape=[jax.ShapeDtypeStruct((8, 128), F32), jax.ShapeDtypeStruct((t, D_MODEL), F32),
                   jax.ShapeDtypeStruct((1, D_MODEL), F32)],
        compiler_params=_params(1),
    )(x, norm.reshape(1, D_MODEL), target)


def _exchange(send, per_dest, name):
    shape = send.shape[1:] if per_dest else send.shape

    def body(send_ref, recv_ref, send_sems, recv_sems, local_sem):
        x, y, c = lax.axis_index("x"), lax.axis_index("y"), lax.axis_index("c")
        me = 4 * x + 2 * y + c

        def peer(k):
            kx, ky, kc = (k >> 2) & 1, (k >> 1) & 1, k & 1
            px = x + kx - 2 * x * kx
            py = y + ky - 2 * y * ky
            pc = c + kc - 2 * c * kc
            return (px, py, pc), 4 * px + 2 * py + pc

        def copy(k, landing):
            to, to_index = peer(k)
            return pltpu.make_async_remote_copy(
                src_ref=send_ref.at[to_index] if per_dest else send_ref,
                dst_ref=recv_ref.at[landing],
                send_sem=send_sems.at[k - 1], recv_sem=recv_sems.at[k - 1],
                device_id=to, device_id_type=pl.DeviceIdType.MESH)

        own = pltpu.make_async_copy(send_ref.at[me] if per_dest else send_ref, recv_ref.at[me], local_sem)
        own.start()
        for k in range(1, N_DEV):
            copy(k, me).start()
        for k in range(1, N_DEV):
            copy(k, me).wait_send()
            copy(k, peer(k)[1]).wait_recv()
        own.wait()

    hbm = pl.BlockSpec(memory_space=pltpu.HBM)
    return pl.pallas_call(
        body, name=name, in_specs=[hbm], out_specs=hbm,
        out_shape=jax.ShapeDtypeStruct((N_DEV,) + tuple(shape), send.dtype),
        scratch_shapes=[pltpu.SemaphoreType.DMA((N_DEV - 1,)), pltpu.SemaphoreType.DMA((N_DEV - 1,)),
                        pltpu.SemaphoreType.DMA],
    )(send)


def _adamw(parts, w, m, v, name):
    n, cols = w.shape
    rows = min(n, 256)
    while n % rows:
        rows -= 8
    c1 = 1.0 - ADAM_B1 ** ADAM_STEP
    c2 = 1.0 - ADAM_B2 ** ADAM_STEP

    def body(p_ref, w_ref, m_ref, v_ref, g_ref, d_ref, nm_ref, nv_ref):
        g = p_ref[0]
        for s in range(1, N_DEV):
            g = g + p_ref[s]
        nm = ADAM_B1 * m_ref[...] + (1.0 - ADAM_B1) * g
        nv = ADAM_B2 * v_ref[...] + (1.0 - ADAM_B2) * (g * g)
        g_ref[...] = g
        nm_ref[...] = nm
        nv_ref[...] = nv
        d_ref[...] = -ADAM_LR * ((nm / c1) / (jnp.sqrt(nv / c2) + ADAM_EPS) + ADAM_WD * w_ref[...])

    blk = _row_spec(rows, cols)
    return pl.pallas_call(
        body, name=name, grid=(n // rows,),
        in_specs=[pl.BlockSpec((N_DEV, rows, cols), lambda i: (0, i, 0)), blk, blk, blk],
        out_specs=[blk] * 4,
        out_shape=[jax.ShapeDtypeStruct((n, cols), F32)] * 4,
        compiler_params=_params(1),
    )(parts, w, m, v)


SSM_KEYS = ("norm", "w_in", "a_re", "a_im", "log_step", "b_re", "b_im", "c_re", "c_im", "d", "w_glu", "b_glu", "w_out")
ATTN_KEYS = ("norm", "w_in", "sinks", "w_out")
LAYER_KEYS = (SSM_KEYS, ATTN_KEYS, SSM_KEYS, ATTN_KEYS)
BIG_KEYS = ("w_in", "w_glu", "w_out")
ATTN_SPLITS = (Q_DIM, KV_DIM, KV_DIM, D_MODEL)


def _rope_tables(t):
    pos = jnp.arange(t, dtype=F32)
    inv_freq = ROPE_THETA ** (-jnp.arange(0, HEAD_DIM, 2, dtype=F32) / HEAD_DIM)
    ang = pos[:, None] * inv_freq[None, :]
    cos, sin = jnp.cos(ang), jnp.sin(ang)
    return jnp.tile(jnp.concatenate([cos, cos], axis=1), (1, 2)), jnp.tile(jnp.concatenate([-sin, sin], axis=1), (1, 2))


def _ssm_layer_fwd(i, x, p, w):
    tag = "l%d_" % i
    mats, mats_vjp = jax.vjp(_s5_matrices, p["a_re"], p["a_im"], p["log_step"], p["b_re"], p["b_im"], p["c_re"], p["c_im"])
    tm, cpt, bpt, ar, ai = mats
    mb = dict(tm=tm.astype(BF16), tmt=jnp.swapaxes(tm, 1, 2).astype(BF16), cpt=cpt.astype(BF16),
              cp=jnp.swapaxes(cpt, 1, 2).astype(BF16), bpt=bpt.astype(BF16), bp=jnp.swapaxes(bpt, 1, 2).astype(BF16))
    u, gate = _inproj_fwd(x, p["norm"], w["w_in"], (D_MODEL, D_MODEL), None, tag + "inproj_fwd")
    xre, xim = _s5_project(u, mb["bpt"], tag + "s5_block_inputs")
    hre, him = _s5_scan_fwd(xre, xim, ar, ai, tag + "s5_scan_fwd")
    y = _s5_outputs(u, hre, him, mb["tm"], mb["cpt"], p["d"], tag + "s5_outputs")
    xn = _ssm_out_fwd(y, gate, x, w["w_glu"], p["b_glu"], w["w_out"], tag + "out_fwd")
    return xn, (x, u, gate, y, hre, him, mb, ar, ai, mats_vjp)


def _ssm_layer_bwd(i, dxn, saved, p, w):
    tag = "l%d_" % i
    x, u, gate, y, hre, him, mb, ar, ai, mats_vjp = saved
    dy, dgate, dw_glu, db_glu, dw_out = _ssm_out_bwd(dxn, y, gate, w["w_glu"], p["b_glu"], w["w_out"], tag + "out_bwd")
    dhre, dhim = _s5_project(dy, mb["cp"], tag + "s5_state_grads")
    dxre, dxim, dar, dai = _s5_scan_bwd(dhre, dhim, hre, him, ar, ai, tag + "s5_scan_bwd")
    du, dtm, dcpt, dbpt, dd = _s5_backward(dy, u, hre, him, dxre, dxim, mb["tmt"], mb["bp"], p["d"], tag + "s5_backward")
    da_re, da_im, dlog_step, db_re, db_im, dc_re, dc_im = mats_vjp((dtm, dcpt, dbpt, dar, dai))
    dx, dw_in, dnorm = _inproj_bwd(x, p["norm"], w["w_in"], [du, dgate], dxn, tag + "inproj_bwd")
    grads = dict(norm=dnorm.reshape(D_MODEL), w_in=dw_in, a_re=da_re, a_im=da_im, log_step=dlog_step, b_re=db_re,
                 b_im=db_im, c_re=dc_re, c_im=dc_im, d=dd.reshape(D_MODEL), w_glu=dw_glu, b_glu=db_glu.reshape(D_MODEL),
                 w_out=dw_out)
    return dx, grads


def _attn_layer_fwd(i, x, p, w, rope):
    tag = "l%d_" % i
    q, k, v, gate = _inproj_fwd(x, p["norm"], w["w_in"], ATTN_SPLITS, rope, tag + "inproj_fwd")
    o = _attn_fwd(q, k, v, p["sinks"], tag + "attn_fwd")
    xn = _attn_out_fwd(o, gate, x, w["w_out"], tag + "out_fwd")
    return xn, (x, q, k, v, gate, o)


def _attn_layer_bwd(i, dxn, saved, p, w, rope):
    tag = "l%d_" % i
    x, q, k, v, gate, o = saved
    do, dgate, dw_out = _attn_out_bwd(dxn, o, gate, w["w_out"], tag + "out_bwd")
    dq, dk, dv, dsinks = _attn_bwd(q, k, v, o, do, p["sinks"], rope, tag + "attn_bwd")
    dx, dw_in, dnorm = _inproj_bwd(x, p["norm"], w["w_in"], [dq, dk, dv, dgate], dxn, tag + "inproj_bwd")
    return dx, dict(norm=dnorm.reshape(D_MODEL), w_in=dw_in, sinks=dsinks.reshape(N_Q_HEADS, ATTN_BLOCK).sum(axis=1), w_out=dw_out)


def _local_step(x, target, small, big):
    rope = _rope_tables(x.shape[0])
    saved = []
    for i in range(4):
        if i % 2 == 0:
            x, s = _ssm_layer_fwd(i, x, small[i], big[i])
        else:
            x, s = _attn_layer_fwd(i, x, small[i], big[i], rope)
        saved.append(s)
    loss, dx, dfinal = _loss_head(x, small[4]["norm"], target, "loss_head")
    grads = [None] * 4 + [dict(norm=dfinal.reshape(D_MODEL))]
    for i in (3, 2, 1, 0):
        if i % 2 == 0:
            dx, grads[i] = _ssm_layer_bwd(i, dx, saved[i], small[i], big[i])
        else:
            dx, grads[i] = _attn_layer_bwd(i, dx, saved[i], small[i], big[i], rope)
    return loss[0, 0], dx, grads


def _shard_rows(key, a):
    return a.reshape(-1, D_MODEL)


def _owner_major(key, g):
    if key == "w_in":
        g = g.reshape(D_MODEL, N_DEV, -1).transpose(1, 0, 2)
    return g.reshape(N_DEV, -1, D_MODEL)


def _from_gathered(key, a, shard_shape):
    if key == "w_in":
        return a.reshape(N_DEV, D_MODEL, shard_shape[1]).transpose(1, 0, 2).reshape(D_MODEL, N_DEV * shard_shape[1])
    return a.reshape(N_DEV * shard_shape[0], shard_shape[1])


def _pad_rows(flat):
    n = flat.shape[0]
    rows = -(-n // (64 * D_MODEL)) * 64
    return jnp.pad(flat, (0, rows * D_MODEL - n)).reshape(rows, D_MODEL)


def kernel(*args):
    names = ["x"]
    layer_names = []
    for i, keys in enumerate(LAYER_KEYS):
        layer_names += ["l%d_%s" % (i, k) for k in keys]
    layer_names.append("final_norm")
    names += layer_names + ["loss_target"] + ["m_" + n for n in layer_names] + ["v_" + n for n in layer_names]
    given = dict(zip(names, args))
    big_names = [n for n in layer_names if n.split("_", 1)[1] in BIG_KEYS]
    small_names = [n for n in layer_names if n not in big_names]

    offsets, rows_at = {}, 0
    for n in big_names:
        offsets[n] = rows_at
        rows_at += given[n].size // D_MODEL
    local_rows = jnp.concatenate([_shard_rows(n, given[n]) for n in big_names], axis=0)
    gathered = _exchange(local_rows.astype(BF16), False, "gather_weights")
    big = [dict() for _ in range(4)]
    for n in big_names:
        layer, key = int(n[1]), n.split("_", 1)[1]
        rows = given[n].size // D_MODEL
        big[layer][key] = _from_gathered(key, gathered[:, offsets[n]:offsets[n] + rows, :], given[n].shape)
    small = [dict() for _ in range(5)]
    for n in small_names:
        if n == "final_norm":
            small[4]["norm"] = given[n]
        else:
            small[int(n[1])][n.split("_", 1)[1]] = given[n]

    loss, dx, grads = _local_step(given["x"][0], given["loss_target"][0], small, big)
    loss = lax.psum(loss, ("x", "y", "c"))

    def grad_of(n):
        return grads[4]["norm"] if n == "final_norm" else grads[int(n[1])][n.split("_", 1)[1]]

    send = jnp.concatenate([_owner_major(n.split("_", 1)[1], grad_of(n)) for n in big_names], axis=1)
    parts = _exchange(send, True, "scatter_weight_grads")
    cat = lambda pre: jnp.concatenate([_shard_rows(n, given[pre + n]) for n in big_names], axis=0)
    g_big, d_big, m_big, v_big = _adamw(parts, cat(""), cat("m_"), cat("v_"), "adamw_matrices")

    flat = lambda f: _pad_rows(jnp.concatenate([f(n).reshape(-1) for n in small_names]))
    parts = _exchange(flat(grad_of), False, "gather_small_grads")
    g_small, d_small, m_small, v_small = _adamw(
        parts, flat(lambda n: given[n]), flat(lambda n: given["m_" + n]), flat(lambda n: given["v_" + n]), "adamw_small")

    outs = {}
    for tag, a_big, a_small in (("grad_", g_big, g_small), ("delta_", d_big, d_small),
                                ("new_m_", m_big, m_small), ("new_v_", v_big, v_small)):
        for n in big_names:
            rows = given[n].size // D_MODEL
            outs[tag + n] = a_big[offsets[n]:offsets[n] + rows].reshape(given[n].shape)
        a_flat, at = a_small.reshape(-1), 0
        for n in small_names:
            outs[tag + n] = a_flat[at:at + given[n].size].reshape(given[n].shape)
            at += given[n].size
    result = [loss, dx[None]]
    for tag in ("grad_", "delta_", "new_m_", "new_v_"):
        result += [outs[tag + n] for n in layer_names]
    return tuple(result)
```

```python
import functools
import math

import jax
import jax.numpy as jnp
from jax import lax
from jax.experimental import pallas as pl
from jax.experimental.pallas import tpu as pltpu

F32 = jnp.float32
BF16 = jnp.bfloat16

D_MODEL = 1024
SSM_GROUP = 16
SSM_GROUPS = D_MODEL // SSM_GROUP
SSM_STATE = 64
S5_BLOCK = 16
S5_LANES = S5_BLOCK * SSM_GROUP
HEAD_DIM = 64
N_Q_HEADS = 16
N_KV_HEADS = 2
GQA = N_Q_HEADS // N_KV_HEADS
Q_DIM = N_Q_HEADS * HEAD_DIM
KV_DIM = N_KV_HEADS * HEAD_DIM
ATTN_BLOCK = 128
ROPE_THETA = 10000.0
NORM_EPS = 1e-5
NEG_INF = -1e30
ATTN_SCALE = HEAD_DIM ** -0.5
N_DEV = 8

ADAM_LR = 0.001
ADAM_B1 = 0.9
ADAM_B2 = 0.999
ADAM_EPS = 1e-08
ADAM_WD = 0.01
ADAM_STEP = 10

VMEM_LIMIT = 56 * 1024 * 1024
ROWS_FWD = 512
ROWS_BWD = 256

NT = (((1,), (1,)), ((), ()))
TN = (((0,), (0,)), ((), ()))


def _params(n_grid):
    return pltpu.CompilerParams(dimension_semantics=("arbitrary",) * n_grid, vmem_limit_bytes=VMEM_LIMIT)


def _dot(a, b):
    return jnp.dot(a, b, preferred_element_type=F32)


def _dot_nt(a, b):
    return lax.dot_general(a, b, NT, preferred_element_type=F32)


def _dot_tn(a, b):
    return lax.dot_general(a, b, TN, preferred_element_type=F32)


def _sigmoid(x):
    return 1.0 / (1.0 + jnp.exp(-x))


_GELU_K = math.sqrt(2.0 / math.pi)


def _gelu(x):
    return x * (0.5 * (1.0 + jnp.tanh(_GELU_K * (x + 0.044715 * (x * x * x)))))


def _gelu_grad(x):
    t = jnp.tanh(_GELU_K * (x + 0.044715 * (x * x * x)))
    return 0.5 * (1.0 + t) + 0.5 * x * (1.0 - t * t) * (_GELU_K * (1.0 + 3.0 * 0.044715 * (x * x)))


def _row_spec(rows, cols):
    return pl.BlockSpec((rows, cols), lambda i: (i, 0))


def _const_spec(shape):
    zeros = (0,) * len(shape)
    return pl.BlockSpec(shape, lambda i: zeros)


def _rope_apply(t, cos, sin_signed, sign):
    lane = lax.broadcasted_iota(jnp.int32, (1, 128), 1)
    first_half = (lane % HEAD_DIM) < (HEAD_DIM // 2)
    out = []
    for j in range(t.shape[1] // 128):
        tj = t[:, 128 * j:128 * (j + 1)]
        partner = jnp.where(first_half, pltpu.roll(tj, 128 - HEAD_DIM // 2, 1), pltpu.roll(tj, HEAD_DIM // 2, 1))
        out.append(tj * cos + sign * (partner * sin_signed))
    return out[0] if len(out) == 1 else jnp.concatenate(out, axis=1)


def _inproj_fwd(x, norm, w, splits, rope, name):
    t = x.shape[0]
    n = w.shape[1]
    rows = ROWS_FWD

    def body(*refs):
        if rope is None:
            x_ref, n_ref, w_ref = refs[:3]
            outs = refs[3:]
        else:
            x_ref, n_ref, w_ref, cos_ref, sin_ref = refs[:5]
            outs = refs[5:]
        xv = x_ref[...]
        rstd = lax.rsqrt(jnp.mean(xv * xv, axis=-1, keepdims=True) + NORM_EPS)
        h = (xv * rstd) * n_ref[...]
        proj = _dot(h.astype(BF16), w_ref[...])
        off = 0
        for i, width in enumerate(splits):
            piece = proj[:, off:off + width]
            if rope is not None and i < 2:
                piece = _rope_apply(piece, cos_ref[...], sin_ref[...], 1.0)
            outs[i][...] = piece
            off += width

    in_specs = [_row_spec(rows, D_MODEL), _const_spec((1, D_MODEL)), _const_spec((D_MODEL, n))]
    args = [x, norm.reshape(1, D_MODEL), w]
    if rope is not None:
        in_specs += [_row_spec(rows, 128), _row_spec(rows, 128)]
        args += list(rope)
    return pl.pallas_call(
        body, name=name, grid=(t // rows,), in_specs=in_specs,
        out_specs=[_row_spec(rows, width) for width in splits],
        out_shape=[jax.ShapeDtypeStruct((t, width), F32) for width in splits],
        compiler_params=_params(1),
    )(*args)


def _inproj_bwd(x, norm, w, dpieces, dxn, name):
    t = x.shape[0]
    n = w.shape[1]
    rows = ROWS_BWD
    widths = [p.shape[1] for p in dpieces]
    k = len(dpieces)

    def body(*refs):
        x_ref, n_ref, w_ref, dxn_ref = refs[:4]
        d_refs = refs[4:4 + k]
        dx_ref, dw_ref, dn_ref = refs[4 + k:]
        i = pl.program_id(0)
        xv = x_ref[...]
        rstd = lax.rsqrt(jnp.mean(xv * xv, axis=-1, keepdims=True) + NORM_EPS)
        xhat = xv * rstd
        h = xhat * n_ref[...]
        dproj = [r[...].astype(BF16) for r in d_refs]
        dproj = dproj[0] if k == 1 else jnp.concatenate(dproj, axis=1)
        dh = _dot_nt(dproj, w_ref[...])
        dw = _dot_tn(h.astype(BF16), dproj)
        dn = jnp.sum(dh * xhat, axis=0, keepdims=True)

        @pl.when(i == 0)
        def _():
            dw_ref[...] = dw
            dn_ref[...] = dn

        @pl.when(i > 0)
        def _():
            dw_ref[...] += dw
            dn_ref[...] += dn

        dxhat = dh * n_ref[...]
        dx_ref[...] = rstd * (dxhat - xhat * jnp.mean(dxhat * xhat, axis=-1, keepdims=True)) + dxn_ref[...]

    return pl.pallas_call(
        body, name=name, grid=(t // rows,),
        in_specs=[_row_spec(rows, D_MODEL), _const_spec((1, D_MODEL)), _const_spec((D_MODEL, n)),
                  _row_spec(rows, D_MODEL)] + [_row_spec(rows, width) for width in widths],
        out_specs=[_row_spec(rows, D_MODEL), _const_spec((D_MODEL, n)), _const_spec((1, D_MODEL))],
        out_shape=[jax.ShapeDtypeStruct((t, D_MODEL), F32), jax.ShapeDtypeStruct((D_MODEL, n), F32),
                   jax.ShapeDtypeStruct((1, D_MODEL), F32)],
        compiler_params=_params(1),
    )(x, norm.reshape(1, D_MODEL), w, dxn, *dpieces)


def _s5_matrices(a_re, a_im, log_step, b_re, b_im, c_re, c_im):
    r = S5_BLOCK
    step = jnp.exp(log_step)[:, None]
    lr, li = a_re * step, a_im * step
    k = jnp.arange(r + 1, dtype=F32)
    mag = jnp.exp(lr[:, None, :] * k[:, None])
    pr = mag * jnp.cos(li[:, None, :] * k[:, None])
    pi = mag * jnp.sin(li[:, None, :] * k[:, None])
    nr, ni = pr[:, 1] - 1.0, pi[:, 1]
    den = a_re * a_re + a_im * a_im
    qr, qi = (nr * a_re + ni * a_im) / den, (ni * a_re - nr * a_im) / den
    bbr = qr[..., None] * b_re - qi[..., None] * b_im
    bbi = qr[..., None] * b_im + qi[..., None] * b_re
    wr = c_re[:, None] * pr[:, :, None, :] - c_im[:, None] * pi[:, :, None, :]
    wi = c_re[:, None] * pi[:, :, None, :] + c_im[:, None] * pr[:, :, None, :]
    w = jnp.concatenate([wr, -wi], axis=-1)
    bb = jnp.concatenate([bbr, bbi], axis=1)
    kern = jnp.einsum("gxp,gpi->gxi", w[:, :r].reshape(SSM_GROUPS, S5_LANES, 2 * SSM_STATE), bb,
                      precision=lax.Precision.HIGHEST).reshape(SSM_GROUPS, r, SSM_GROUP, SSM_GROUP)
    cols = [jnp.pad(kern[:, :r - s], ((0, 0), (s, 0), (0, 0), (0, 0))) for s in range(r)]
    tm = jnp.stack(cols, axis=3).reshape(SSM_GROUPS, S5_LANES, S5_LANES)
    cpt = w[:, 1:].reshape(SSM_GROUPS, S5_LANES, 2 * SSM_STATE)
    prs = jnp.swapaxes(pr[:, r - 1::-1][:, :r], 1, 2)[..., None]
    pis = jnp.swapaxes(pi[:, r - 1::-1][:, :r], 1, 2)[..., None]
    bp_re = prs * bbr[:, :, None, :] - pis * bbi[:, :, None, :]
    bp_im = prs * bbi[:, :, None, :] + pis * bbr[:, :, None, :]
    bpt = jnp.concatenate([bp_re, bp_im], axis=1).reshape(SSM_GROUPS, 2 * SSM_STATE, S5_LANES)
    ar = pr[:, r].reshape(1, SSM_GROUPS * SSM_STATE)
    ai = pi[:, r].reshape(1, SSM_GROUPS * SSM_STATE)
    return tm, cpt, bpt, ar, ai


S5_OCTET = 128 // SSM_GROUP
S5_STEPS = SSM_GROUPS // S5_OCTET


def _oct_spec(t):
    return pl.BlockSpec((t, 128), lambda j: (0, j))


def _state_spec(nb):
    return pl.BlockSpec((nb, S5_OCTET * SSM_STATE), lambda j: (0, j))


def _gmat_spec(a, b):
    return pl.BlockSpec((S5_OCTET, a, b), lambda j: (j, 0, 0))


def _block_rows(ref, nb):
    return [ref[pl.ds(r, nb, stride=S5_BLOCK), :] for r in range(S5_BLOCK)]


def _group_cols(pieces_t, g):
    return jnp.concatenate([p[SSM_GROUP * g:SSM_GROUP * (g + 1)] for p in pieces_t], axis=0)


def _state_cols(re_t, im_t, g):
    return jnp.concatenate([re_t[SSM_STATE * g:SSM_STATE * (g + 1)], im_t[SSM_STATE * g:SSM_STATE * (g + 1)]], axis=0)


def _s5_project(a, mat, name):
    t = a.shape[0]
    nb = t // S5_BLOCK

    def body(a_ref, m_ref, re_ref, im_ref):
        at = [p.T for p in _block_rows(a_ref, nb)]
        for pair in range(S5_OCTET // 2):
            xs = [_dot(m_ref[2 * pair + k], _group_cols(at, 2 * pair + k).astype(BF16)) for k in (0, 1)]
            lanes = slice(128 * pair, 128 * (pair + 1))
            re_ref[:, lanes] = jnp.concatenate([xs[0][:SSM_STATE], xs[1][:SSM_STATE]], axis=0).T
            im_ref[:, lanes] = jnp.concatenate([xs[0][SSM_STATE:], xs[1][SSM_STATE:]], axis=0).T

    return pl.pallas_call(
        body, name=name, grid=(S5_STEPS,),
        in_specs=[_oct_spec(t), _gmat_spec(2 * SSM_STATE, S5_LANES)],
        out_specs=[_state_spec(nb), _state_spec(nb)],
        out_shape=[jax.ShapeDtypeStruct((nb, SSM_GROUPS * SSM_STATE), F32)] * 2,
        compiler_params=_params(1),
    )(a, mat)


_SCAN_LANES = 1024


def _s5_scan_fwd(xre, xim, ar, ai, name):
    nb = xre.shape[0]
    col = pl.BlockSpec((nb, _SCAN_LANES), lambda j: (0, j))
    par = pl.BlockSpec((1, _SCAN_LANES), lambda j: (0, j))

    def body(xre_ref, xim_ref, ar_ref, ai_ref, hre_ref, him_ref):
        a_r, a_i = ar_ref[...], ai_ref[...]

        def step(b, carry):
            hr, hi = carry
            hre_ref[pl.ds(b, 1), :] = hr
            him_ref[pl.ds(b, 1), :] = hi
            xr, xi = xre_ref[pl.ds(b, 1), :], xim_ref[pl.ds(b, 1), :]
            return a_r * hr - a_i * hi + xr, a_r * hi + a_i * hr + xi

        zero = jnp.zeros((1, _SCAN_LANES), F32)
        lax.fori_loop(0, nb, step, (zero, zero))

    return pl.pallas_call(
        body, name=name, grid=(xre.shape[1] // _SCAN_LANES,),
        in_specs=[col, col, par, par], out_specs=[col, col],
        out_shape=[jax.ShapeDtypeStruct(xre.shape, F32)] * 2,
        compiler_params=_params(1),
    )(xre, xim, ar, ai)


def _s5_scan_bwd(dhre, dhim, hre, him, ar, ai, name):
    nb = dhre.shape[0]
    col = pl.BlockSpec((nb, _SCAN_LANES), lambda j: (0, j))
    par = pl.BlockSpec((1, _SCAN_LANES), lambda j: (0, j))

    def body(dhre_ref, dhim_ref, hre_ref, him_ref, ar_ref, ai_ref, dxre_ref, dxim_ref, dar_ref, dai_ref):
        a_r, a_i = ar_ref[...], ai_ref[...]

        def step(s, carry):
            gr, gi, dar, dai = carry
            b = nb - 1 - s
            dxre_ref[pl.ds(b, 1), :] = gr
            dxim_ref[pl.ds(b, 1), :] = gi
            hr, hi = hre_ref[pl.ds(b, 1), :], him_ref[pl.ds(b, 1), :]
            dar = dar + (hr * gr + hi * gi)
            dai = dai + (hr * gi - hi * gr)
            dr, di = dhre_ref[pl.ds(b, 1), :], dhim_ref[pl.ds(b, 1), :]
            return dr + (a_r * gr + a_i * gi), di + (a_r * gi - a_i * gr), dar, dai

        zero = jnp.zeros((1, _SCAN_LANES), F32)
        _, _, dar, dai = lax.fori_loop(0, nb, step, (zero, zero, zero, zero))
        dar_ref[...] = dar
        dai_ref[...] = dai

    return pl.pallas_call(
        body, name=name, grid=(dhre.shape[1] // _SCAN_LANES,),
        in_specs=[col, col, col, col, par, par], out_specs=[col, col, par, par],
        out_shape=[jax.ShapeDtypeStruct(dhre.shape, F32)] * 2 + [jax.ShapeDtypeStruct(ar.shape, F32)] * 2,
        compiler_params=_params(1),
    )(dhre, dhim, hre, him, ar, ai)


def _s5_outputs(u, hre, him, tm, cpt, d, name):
    t = u.shape[0]
    nb = t // S5_BLOCK

    def body(u_ref, hre_ref, him_ref, tm_ref, cpt_ref, d_ref, y_ref):
        u_rows = _block_rows(u_ref, nb)
        ut = [p.T for p in u_rows]
        hre_t, him_t = hre_ref[...].T, him_ref[...].T
        yts = []
        for g in range(S5_OCTET):
            yts.append(_dot(tm_ref[g], _group_cols(ut, g).astype(BF16))
                       + _dot(cpt_ref[g], _state_cols(hre_t, him_t, g).astype(BF16)))
        for r in range(S5_BLOCK):
            rows = jnp.concatenate([yt[SSM_GROUP * r:SSM_GROUP * (r + 1)] for yt in yts], axis=0)
            y_ref[pl.ds(r, nb, stride=S5_BLOCK), :] = rows.T + d_ref[...] * u_rows[r]

    return pl.pallas_call(
        body, name=name, grid=(S5_STEPS,),
        in_specs=[_oct_spec(t), _state_spec(nb), _state_spec(nb), _gmat_spec(S5_LANES, S5_LANES),
                  _gmat_spec(S5_LANES, 2 * SSM_STATE), _oct_spec(1)],
        out_specs=_oct_spec(t),
        out_shape=jax.ShapeDtypeStruct(u.shape, F32),
        compiler_params=_params(1),
    )(u, hre, him, tm, cpt, d.reshape(1, D_MODEL))


def _s5_backward(dy, u, hre, him, dxre, dxim, tmt, bp, d, name):
    t = u.shape[0]
    nb = t // S5_BLOCK

    def body(dy_ref, u_ref, hre_ref, him_ref, dxre_ref, dxim_ref, tmt_ref, bp_ref, d_ref,
             du_ref, dtm_ref, dcpt_ref, dbpt_ref, dd_ref):
        dy_rows, u_rows = _block_rows(dy_ref, nb), _block_rows(u_ref, nb)
        dyt, ut = [p.T for p in dy_rows], [p.T for p in u_rows]
        hre_t, him_t = hre_ref[...].T, him_ref[...].T
        dxre_t, dxim_t = dxre_ref[...].T, dxim_ref[...].T
        duts = []
        for g in range(S5_OCTET):
            dyg, ug = _group_cols(dyt, g).astype(BF16), _group_cols(ut, g).astype(BF16)
            hg = _state_cols(hre_t, him_t, g).astype(BF16)
            dxg = _state_cols(dxre_t, dxim_t, g).astype(BF16)
            duts.append(_dot(tmt_ref[g], dyg) + _dot(bp_ref[g], dxg))
            dtm_ref[g] = _dot_nt(dyg, ug)
            dcpt_ref[g] = _dot_nt(dyg, hg)
            dbpt_ref[g] = _dot_nt(dxg, ug)
        dd = jnp.zeros((1, 128), F32)
        for r in range(S5_BLOCK):
            rows = jnp.concatenate([dut[SSM_GROUP * r:SSM_GROUP * (r + 1)] for dut in duts], axis=0)
            du_ref[pl.ds(r, nb, stride=S5_BLOCK), :] = rows.T + d_ref[...] * dy_rows[r]
            dd = dd + jnp.sum(dy_rows[r] * u_rows[r], axis=0, keepdims=True)
        dd_ref[...] = dd

    return pl.pallas_call(
        body, name=name, grid=(S5_STEPS,),
        in_specs=[_oct_spec(t), _oct_spec(t), _state_spec(nb), _state_spec(nb), _state_spec(nb), _state_spec(nb),
                  _gmat_spec(S5_LANES, S5_LANES), _gmat_spec(S5_LANES, 2 * SSM_STATE), _oct_spec(1)],
        out_specs=[_oct_spec(t), _gmat_spec(S5_LANES, S5_LANES), _gmat_spec(S5_LANES, 2 * SSM_STATE),
                   _gmat_spec(2 * SSM_STATE, S5_LANES), _oct_spec(1)],
        out_shape=[jax.ShapeDtypeStruct(u.shape, F32),
                   jax.ShapeDtypeStruct((SSM_GROUPS, S5_LANES, S5_LANES), F32),
                   jax.ShapeDtypeStruct((SSM_GROUPS, S5_LANES, 2 * SSM_STATE), F32),
                   jax.ShapeDtypeStruct((SSM_GROUPS, 2 * SSM_STATE, S5_LANES), F32),
                   jax.ShapeDtypeStruct((1, D_MODEL), F32)],
        compiler_params=_params(1),
    )(dy, u, hre, him, dxre, dxim, tmt, bp, d.reshape(1, D_MODEL))


def _ssm_out_fwd(y, gate, x, w_glu, b_glu, w_out, name):
    t = x.shape[0]
    rows = ROWS_FWD

    def body(y_ref, g_ref, x_ref, wg_ref, bg_ref, wo_ref, o_ref):
        z0 = _gelu(y_ref[...])
        s = _dot(z0.astype(BF16), wg_ref[...]) + bg_ref[...]
        gate_v = g_ref[...]
        a = (z0 * _sigmoid(s)) * (gate_v * _sigmoid(gate_v))
        o_ref[...] = x_ref[...] + _dot(a.astype(BF16), wo_ref[...])

    return pl.pallas_call(
        body, name=name, grid=(t // rows,),
        in_specs=[_row_spec(rows, D_MODEL)] * 3 + [_const_spec((D_MODEL, D_MODEL)), _const_spec((1, D_MODEL)),
                                                   _const_spec((D_MODEL, D_MODEL))],
        out_specs=_row_spec(rows, D_MODEL),
        out_shape=jax.ShapeDtypeStruct((t, D_MODEL), F32),
        compiler_params=_params(1),
    )(y, gate, x, w_glu, b_glu.reshape(1, D_MODEL), w_out)


def _ssm_out_bwd(dxn, y, gate, w_glu, b_glu, w_out, name):
    t = y.shape[0]
    rows = ROWS_BWD

    def body(dxn_ref, y_ref, g_ref, wg_ref, bg_ref, wo_ref, dy_ref, dg_ref, dwg_ref, dbg_ref, dwo_ref):
        i = pl.program_id(0)
        yv = y_ref[...]
        z0 = _gelu(yv)
        z0b = z0.astype(BF16)
        sg = _sigmoid(_dot(z0b, wg_ref[...]) + bg_ref[...])
        z = z0 * sg
        gate_v = g_ref[...]
        sgg = _sigmoid(gate_v)
        silu = gate_v * sgg
        a = z * silu
        dob = dxn_ref[...].astype(BF16)
        da = _dot_nt(dob, wo_ref[...])
        dwo = _dot_tn(a.astype(BF16), dob)
        dz = da * silu
        dg_ref[...] = da * z * (sgg * (1.0 + gate_v * (1.0 - sgg)))
        ds = dz * z0 * (sg * (1.0 - sg))
        dsb = ds.astype(BF16)
        dz0 = dz * sg + _dot_nt(dsb, wg_ref[...])
        dwg = _dot_tn(z0b, dsb)
        dbg = jnp.sum(ds, axis=0, keepdims=True)
        dy_ref[...] = dz0 * _gelu_grad(yv)

        @pl.when(i == 0)
        def _():
            dwo_ref[...] = dwo
            dwg_ref[...] = dwg
            dbg_ref[...] = dbg

        @pl.when(i > 0)
        def _():
            dwo_ref[...] += dwo
            dwg_ref[...] += dwg
            dbg_ref[...] += dbg

    sq = _const_spec((D_MODEL, D_MODEL))
    vec = _const_spec((1, D_MODEL))
    return pl.pallas_call(
        body, name=name, grid=(t // rows,),
        in_specs=[_row_spec(rows, D_MODEL)] * 3 + [sq, vec, sq],
        out_specs=[_row_spec(rows, D_MODEL), _row_spec(rows, D_MODEL), sq, vec, sq],
        out_shape=[jax.ShapeDtypeStruct((t, D_MODEL), F32)] * 2 + [
            jax.ShapeDtypeStruct((D_MODEL, D_MODEL), F32), jax.ShapeDtypeStruct((1, D_MODEL), F32),
            jax.ShapeDtypeStruct((D_MODEL, D_MODEL), F32)],
        compiler_params=_params(1),
    )(dxn, y, gate, w_glu, b_glu.reshape(1, D_MODEL), w_out)


KV_LANES = GQA * ATTN_BLOCK


def _attn_bias(block_is_first):
    kj = lax.broadcasted_iota(jnp.int32, (2 * ATTN_BLOCK, ATTN_BLOCK), 0)
    qi = lax.broadcasted_iota(jnp.int32, (2 * ATTN_BLOCK, ATTN_BLOCK), 1)
    dist = qi + ATTN_BLOCK - kj
    valid = (dist >= 0) & (dist < ATTN_BLOCK) & (jnp.logical_not(block_is_first) | (kj >= ATTN_BLOCK))
    return jnp.tile(jnp.where(valid, 0.0, NEG_INF).astype(F32), (1, GQA))


def _head_cols(a_t, kvh):
    heads = range(kvh * GQA, (kvh + 1) * GQA)
    return jnp.concatenate([a_t[HEAD_DIM * h:HEAD_DIM * (h + 1)] for h in heads], axis=1)


def _head_rows(a_cols):
    stacked = jnp.concatenate([a_cols[:, ATTN_BLOCK * g:ATTN_BLOCK * (g + 1)] for g in range(GQA)], axis=0)
    return stacked.T


def _kv_rows(prev_ref, cur_ref, kvh):
    lanes = slice(HEAD_DIM * kvh, HEAD_DIM * (kvh + 1))
    return jnp.concatenate([prev_ref[:, lanes], cur_ref[:, lanes]], axis=0).astype(BF16)


def _kv_cols(prev_t, cur_t, kvh):
    rows = slice(HEAD_DIM * kvh, HEAD_DIM * (kvh + 1))
    return jnp.concatenate([prev_t[rows], cur_t[rows]], axis=1).astype(BF16)


def _attn_probs(kk, q_cols, sink_row, bias):
    s = _dot(kk, q_cols) * ATTN_SCALE + bias
    m = jnp.maximum(jnp.max(s, axis=0, keepdims=True), sink_row)
    p = jnp.exp(s - m)
    e_sink = jnp.exp(sink_row - m)
    inv = 1.0 / (jnp.sum(p, axis=0, keepdims=True) + e_sink)
    return p * inv, e_sink * inv


def _sink_cols(sinks):
    return jnp.repeat(sinks, ATTN_BLOCK).reshape(N_KV_HEADS, 1, KV_LANES)


def _attn_fwd(q, k, v, sinks, name):
    t = q.shape[0]
    nblk = t // ATTN_BLOCK

    def body(s_ref, q_ref, kc_ref, kp_ref, vc_ref, vp_ref, o_ref):
        bias = _attn_bias(pl.program_id(0) == 0)
        q_t = q_ref[...].T
        vp_t, vc_t = vp_ref[...].T, vc_ref[...].T
        for kvh in range(N_KV_HEADS):
            p, _ = _attn_probs(_kv_rows(kp_ref, kc_ref, kvh), _head_cols(q_t, kvh).astype(BF16), s_ref[kvh], bias)
            o_cols = _dot(_kv_cols(vp_t, vc_t, kvh), p.astype(BF16))
            o_ref[:, GQA * HEAD_DIM * kvh:GQA * HEAD_DIM * (kvh + 1)] = _head_rows(o_cols)

    cur = lambda i: (i, 0)
    prev = lambda i: (jnp.maximum(i - 1, 0), 0)
    return pl.pallas_call(
        body, name=name, grid=(nblk,),
        in_specs=[_const_spec((N_KV_HEADS, 1, KV_LANES)),
                  pl.BlockSpec((ATTN_BLOCK, Q_DIM), cur),
                  pl.BlockSpec((ATTN_BLOCK, KV_DIM), cur), pl.BlockSpec((ATTN_BLOCK, KV_DIM), prev),
                  pl.BlockSpec((ATTN_BLOCK, KV_DIM), cur), pl.BlockSpec((ATTN_BLOCK, KV_DIM), prev)],
        out_specs=pl.BlockSpec((ATTN_BLOCK, Q_DIM), cur),
        out_shape=jax.ShapeDtypeStruct((t, Q_DIM), F32),
        compiler_params=_params(1),
    )(_sink_cols(sinks), q, k, k, v, v)


def _attn_bwd(q, k, v, o, do, sinks, rope, name):
    t = q.shape[0]
    nblk = t // ATTN_BLOCK

    def body(s_ref, q_ref, o_ref, do_ref, kp_ref, kc_ref, vp_ref, vc_ref, cosq_ref, sinq_ref, cosk_ref, sinkey_ref,
             dq_ref, dk_ref, dv_ref, ds_ref, new_k, new_v, wait_k, wait_v):
        n = pl.program_id(0)

        @pl.when(n == 0)
        def _():
            ds_ref[...] = jnp.zeros_like(ds_ref)
            wait_k[...] = jnp.zeros_like(wait_k)
            wait_v[...] = jnp.zeros_like(wait_v)

        @pl.when(n < nblk)
        def _():
            bias = _attn_bias(n == 0)
            q_t, o_t, do_t = q_ref[...].T, o_ref[...].T, do_ref[...].T
            kp_t, kc_t = kp_ref[...].T, kc_ref[...].T
            for kvh in range(N_KV_HEADS):
                q_cols = _head_cols(q_t, kvh).astype(BF16)
                do_cols = _head_cols(do_t, kvh)
                delta = jnp.sum(do_cols * _head_cols(o_t, kvh), axis=0, keepdims=True)
                do_cols = do_cols.astype(BF16)
                p, p_sink = _attn_probs(_kv_rows(kp_ref, kc_ref, kvh), q_cols, s_ref[kvh], bias)
                dp = _dot(_kv_rows(vp_ref, vc_ref, kvh), do_cols)
                ds = (p * (dp - delta) * ATTN_SCALE).astype(BF16)
                lanes = slice(GQA * HEAD_DIM * kvh, GQA * HEAD_DIM * (kvh + 1))
                dq_ref[:, lanes] = _head_rows(_dot(_kv_cols(kp_t, kc_t, kvh), ds))
                head = slice(HEAD_DIM * kvh, HEAD_DIM * (kvh + 1))
                new_k[:, head] = _dot_nt(ds, q_cols)
                new_v[:, head] = _dot_nt(p.astype(BF16), do_cols)
                ds_ref[kvh] += -(p_sink * delta)
            dq_ref[...] = _rope_apply(dq_ref[...], cosq_ref[...], sinq_ref[...], -1.0)

        @pl.when(n == nblk)
        def _():
            new_k[...] = jnp.zeros_like(new_k)
            new_v[...] = jnp.zeros_like(new_v)

        dk_ref[...] = _rope_apply(wait_k[...] + new_k[:ATTN_BLOCK], cosk_ref[...], sinkey_ref[...], -1.0)
        dv_ref[...] = wait_v[...] + new_v[:ATTN_BLOCK]
        wait_k[...] = new_k[ATTN_BLOCK:]
        wait_v[...] = new_v[ATTN_BLOCK:]

    cur = lambda i: (jnp.minimum(i, nblk - 1), 0)
    prev = lambda i: (jnp.maximum(i - 1, 0), 0)
    qs = lambda f: pl.BlockSpec((ATTN_BLOCK, Q_DIM), f)
    ks = lambda f: pl.BlockSpec((ATTN_BLOCK, KV_DIM), f)
    sink_spec = _const_spec((N_KV_HEADS, 1, KV_LANES))
    return pl.pallas_call(
        body, name=name, grid=(nblk + 1,),
        in_specs=[sink_spec, qs(cur), qs(cur), qs(cur), ks(prev), ks(cur), ks(prev), ks(cur),
                  ks(cur), ks(cur), ks(prev), ks(prev)],
        out_specs=[qs(cur), ks(prev), ks(prev), sink_spec],
        out_shape=[jax.ShapeDtypeStruct((t, Q_DIM), F32), jax.ShapeDtypeStruct((t, KV_DIM), F32),
                   jax.ShapeDtypeStruct((t, KV_DIM), F32), jax.ShapeDtypeStruct((N_KV_HEADS, 1, KV_LANES), F32)],
        scratch_shapes=[pltpu.VMEM((2 * ATTN_BLOCK, KV_DIM), F32), pltpu.VMEM((2 * ATTN_BLOCK, KV_DIM), F32),
                        pltpu.VMEM((ATTN_BLOCK, KV_DIM), F32), pltpu.VMEM((ATTN_BLOCK, KV_DIM), F32)],
        compiler_params=_params(1),
    )(_sink_cols(sinks), q, o, do, k, k, v, v, rope[0], rope[1], rope[0], rope[1])


def _attn_out_fwd(o, gate, x, w_out, name):
    t = x.shape[0]
    rows = ROWS_FWD

    def body(o_ref, g_ref, x_ref, wo_ref, xn_ref):
        gate_v = g_ref[...]
        a = o_ref[...] * (gate_v * _sigmoid(gate_v))
        xn_ref[...] = x_ref[...] + _dot(a.astype(BF16), wo_ref[...])

    return pl.pallas_call(
        body, name=name, grid=(t // rows,),
        in_specs=[_row_spec(rows, D_MODEL)] * 3 + [_const_spec((D_MODEL, D_MODEL))],
        out_specs=_row_spec(rows, D_MODEL),
        out_shape=jax.ShapeDtypeStruct((t, D_MODEL), F32),
        compiler_params=_params(1),
    )(o, gate, x, w_out)


def _attn_out_bwd(dxn, o, gate, w_out, name):
    t = o.shape[0]
    rows = ROWS_BWD

    def body(dxn_ref, o_ref, g_ref, wo_ref, do_ref, dg_ref, dwo_ref):
        i = pl.program_id(0)
        gate_v, ov = g_ref[...], o_ref[...]
        sgg = _sigmoid(gate_v)
        silu = gate_v * sgg
        dob = dxn_ref[...].astype(BF16)
        da = _dot_nt(dob, wo_ref[...])
        dwo = _dot_tn((ov * silu).astype(BF16), dob)
        do_ref[...] = da * silu
        dg_ref[...] = da * ov * (sgg * (1.0 + gate_v * (1.0 - sgg)))

        @pl.when(i == 0)
        def _():
            dwo_ref[...] = dwo

        @pl.when(i > 0)
        def _():
            dwo_ref[...] += dwo

    sq = _const_spec((D_MODEL, D_MODEL))
    return pl.pallas_call(
        body, name=name, grid=(t // rows,),
        in_specs=[_row_spec(rows, D_MODEL)] * 3 + [sq],
        out_specs=[_row_spec(rows, D_MODEL), _row_spec(rows, D_MODEL), sq],
        out_shape=[jax.ShapeDtypeStruct((t, D_MODEL), F32)] * 2 + [jax.ShapeDtypeStruct((D_MODEL, D_MODEL), F32)],
        compiler_params=_params(1),
    )(dxn, o, gate, w_out)


def _loss_head(x, norm, target, name):
    t = x.shape[0]
    rows = ROWS_FWD

    def body(x_ref, n_ref, t_ref, loss_ref, dx_ref, dn_ref):
        i = pl.program_id(0)
        xv = x_ref[...]
        rstd = lax.rsqrt(jnp.mean(xv * xv, axis=-1, keepdims=True) + NORM_EPS)
        xhat = xv * rstd
        err = xhat * n_ref[...] - t_ref[...]
        part = 0.5 * jnp.sum(jnp.mean(err * err, axis=-1, keepdims=True), axis=0, keepdims=True)
        dy = err * (1.0 / D_MODEL)
        dn = jnp.sum(dy * xhat, axis=0, keepdims=True)
        dxhat = dy * n_ref[...]
        dx_ref[...] = rstd * (dxhat - xhat * jnp.mean(dxhat * xhat, axis=-1, keepdims=True))

        @pl.when(i == 0)
        def _():
            loss_ref[...] = jnp.zeros((8, 128), F32) + part
            dn_ref[...] = dn

        @pl.when(i > 0)
        def _():
            loss_ref[...] += part
            dn_ref[...] += dn

    return pl.pallas_call(
        body, name=name, grid=(t // rows,),
        in_specs=[_row_spec(rows, D_MODEL), _const_spec((1, D_MODEL)), _row_spec(rows, D_MODEL)],
        out_specs=[_const_spec((8, 128)), _row_spec(rows, D_MODEL), _const_spec((1, D_MODEL))],
        out_shape=[jax.ShapeDtypeStruct((8, 128), F32), jax.ShapeDtypeStruct((t, D_MODEL), F32),
                   jax.ShapeDtypeStruct((1, D_MODEL), F32)],
        compiler_params=_params(1),
    )(x, norm.reshape(1, D_MODEL), target)


N_CHIPS = 4
N_CORES = 2
CHIP_FLIPS = ((0, 1, 0), (1, 0, 0), (1, 1, 0))
CORE_FLIPS = ((0, 0, 1),)


def _exchange(send, among, per_dest, name):
    flips, parties = (CHIP_FLIPS, N_CHIPS) if among == "chip" else (CORE_FLIPS, N_CORES)
    shape = send.shape[1:] if per_dest else send.shape

    def body(send_ref, recv_ref, send_sems, recv_sems, local_sem):
        x, y, c = lax.axis_index("x"), lax.axis_index("y"), lax.axis_index("c")

        def number(px, py, pc):
            return 2 * px + py if among == "chip" else pc

        me = number(x, y, c)

        def peer(k):
            fx, fy, fc = flips[k]
            to = (x + fx - 2 * x * fx, y + fy - 2 * y * fy, c + fc - 2 * c * fc)
            return to, number(*to)

        def copy(k, landing):
            to, to_number = peer(k)
            return pltpu.make_async_remote_copy(
                src_ref=send_ref.at[to_number] if per_dest else send_ref,
                dst_ref=recv_ref.at[landing],
                send_sem=send_sems.at[k], recv_sem=recv_sems.at[k],
                device_id=to, device_id_type=pl.DeviceIdType.MESH)

        own = pltpu.make_async_copy(send_ref.at[me] if per_dest else send_ref, recv_ref.at[me], local_sem)
        own.start()
        for k in range(len(flips)):
            copy(k, me).start()
        for k in range(len(flips)):
            copy(k, me).wait_send()
            copy(k, peer(k)[1]).wait_recv()
        own.wait()

    hbm = pl.BlockSpec(memory_space=pltpu.HBM)
    return pl.pallas_call(
        body, name=name, in_specs=[hbm], out_specs=hbm,
        out_shape=jax.ShapeDtypeStruct((parties,) + tuple(shape), send.dtype),
        scratch_shapes=[pltpu.SemaphoreType.DMA((len(flips),)), pltpu.SemaphoreType.DMA((len(flips),)),
                        pltpu.SemaphoreType.DMA],
    )(send)


def _all_gather(a, name):
    return _exchange(_exchange(a, "chip", False, name + "_chips"), "core", False, name + "_cores")


def _sum_parts(parts, name):
    k, n, cols = parts.shape
    rows = min(n, 256)
    while n % rows:
        rows -= 8

    def body(p_ref, o_ref):
        acc = p_ref[0]
        for s in range(1, k):
            acc = acc + p_ref[s]
        o_ref[...] = acc

    return pl.pallas_call(
        body, name=name, grid=(n // rows,),
        in_specs=[pl.BlockSpec((k, rows, cols), lambda i: (0, i, 0))],
        out_specs=_row_spec(rows, cols),
        out_shape=jax.ShapeDtypeStruct((n, cols), F32),
        compiler_params=_params(1),
    )(parts)


def _adamw(parts, w, m, v, name):
    n, cols = w.shape
    k = parts.shape[0]
    rows = min(n, 256)
    while n % rows:
        rows -= 8
    c1 = 1.0 - ADAM_B1 ** ADAM_STEP
    c2 = 1.0 - ADAM_B2 ** ADAM_STEP

    def body(p_ref, w_ref, m_ref, v_ref, g_ref, d_ref, nm_ref, nv_ref):
        g = p_ref[0]
        for s in range(1, k):
            g = g + p_ref[s]
        nm = ADAM_B1 * m_ref[...] + (1.0 - ADAM_B1) * g
        nv = ADAM_B2 * v_ref[...] + (1.0 - ADAM_B2) * (g * g)
        g_ref[...] = g
        nm_ref[...] = nm
        nv_ref[...] = nv
        d_ref[...] = -ADAM_LR * ((nm / c1) / (jnp.sqrt(nv / c2) + ADAM_EPS) + ADAM_WD * w_ref[...])

    blk = _row_spec(rows, cols)
    return pl.pallas_call(
        body, name=name, grid=(n // rows,),
        in_specs=[pl.BlockSpec((k, rows, cols), lambda i: (0, i, 0)), blk, blk, blk],
        out_specs=[blk] * 4,
        out_shape=[jax.ShapeDtypeStruct((n, cols), F32)] * 4,
        compiler_params=_params(1),
    )(parts, w, m, v)


SSM_KEYS = ("norm", "w_in", "a_re", "a_im", "log_step", "b_re", "b_im", "c_re", "c_im", "d", "w_glu", "b_glu", "w_out")
ATTN_KEYS = ("norm", "w_in", "sinks", "w_out")
LAYER_KEYS = (SSM_KEYS, ATTN_KEYS, SSM_KEYS, ATTN_KEYS)
BIG_KEYS = ("w_in", "w_glu", "w_out")
ATTN_SPLITS = (Q_DIM, KV_DIM, KV_DIM, D_MODEL)


def _rope_tables(t):
    pos = jnp.arange(t, dtype=F32)
    inv_freq = ROPE_THETA ** (-jnp.arange(0, HEAD_DIM, 2, dtype=F32) / HEAD_DIM)
    ang = pos[:, None] * inv_freq[None, :]
    cos, sin = jnp.cos(ang), jnp.sin(ang)
    return jnp.tile(jnp.concatenate([cos, cos], axis=1), (1, 2)), jnp.tile(jnp.concatenate([-sin, sin], axis=1), (1, 2))


def _ssm_layer_fwd(i, x, p, w):
    tag = "l%d_" % i
    mats, mats_vjp = jax.vjp(_s5_matrices, p["a_re"], p["a_im"], p["log_step"], p["b_re"], p["b_im"], p["c_re"], p["c_im"])
    tm, cpt, bpt, ar, ai = mats
    mb = dict(tm=tm.astype(BF16), tmt=jnp.swapaxes(tm, 1, 2).astype(BF16), cpt=cpt.astype(BF16),
              cp=jnp.swapaxes(cpt, 1, 2).astype(BF16), bpt=bpt.astype(BF16), bp=jnp.swapaxes(bpt, 1, 2).astype(BF16))
    u, gate = _inproj_fwd(x, p["norm"], w["w_in"], (D_MODEL, D_MODEL), None, tag + "inproj_fwd")
    xre, xim = _s5_project(u, mb["bpt"], tag + "s5_block_inputs")
    hre, him = _s5_scan_fwd(xre, xim, ar, ai, tag + "s5_scan_fwd")
    y = _s5_outputs(u, hre, him, mb["tm"], mb["cpt"], p["d"], tag + "s5_outputs")
    xn = _ssm_out_fwd(y, gate, x, w["w_glu"], p["b_glu"], w["w_out"], tag + "out_fwd")
    return xn, (x, u, gate, y, hre, him, mb, ar, ai, mats_vjp)


def _ssm_layer_bwd(i, dxn, saved, p, w):
    tag = "l%d_" % i
    x, u, gate, y, hre, him, mb, ar, ai, mats_vjp = saved
    dy, dgate, dw_glu, db_glu, dw_out = _ssm_out_bwd(dxn, y, gate, w["w_glu"], p["b_glu"], w["w_out"], tag + "out_bwd")
    dhre, dhim = _s5_project(dy, mb["cp"], tag + "s5_state_grads")
    dxre, dxim, dar, dai = _s5_scan_bwd(dhre, dhim, hre, him, ar, ai, tag + "s5_scan_bwd")
    du, dtm, dcpt, dbpt, dd = _s5_backward(dy, u, hre, him, dxre, dxim, mb["tmt"], mb["bp"], p["d"], tag + "s5_backward")
    da_re, da_im, dlog_step, db_re, db_im, dc_re, dc_im = mats_vjp((dtm, dcpt, dbpt, dar, dai))
    dx, dw_in, dnorm = _inproj_bwd(x, p["norm"], w["w_in"], [du, dgate], dxn, tag + "inproj_bwd")
    grads = dict(norm=dnorm.reshape(D_MODEL), w_in=dw_in, a_re=da_re, a_im=da_im, log_step=dlog_step, b_re=db_re,
                 b_im=db_im, c_re=dc_re, c_im=dc_im, d=dd.reshape(D_MODEL), w_glu=dw_glu, b_glu=db_glu.reshape(D_MODEL),
                 w_out=dw_out)
    return dx, grads


def _attn_layer_fwd(i, x, p, w, rope):
    tag = "l%d_" % i
    q, k, v, gate = _inproj_fwd(x, p["norm"], w["w_in"], ATTN_SPLITS, rope, tag + "inproj_fwd")
    o = _attn_fwd(q, k, v, p["sinks"], tag + "attn_fwd")
    xn = _attn_out_fwd(o, gate, x, w["w_out"], tag + "out_fwd")
    return xn, (x, q, k, v, gate, o)


def _attn_layer_bwd(i, dxn, saved, p, w, rope):
    tag = "l%d_" % i
    x, q, k, v, gate, o = saved
    do, dgate, dw_out = _attn_out_bwd(dxn, o, gate, w["w_out"], tag + "out_bwd")
    dq, dk, dv, dsinks = _attn_bwd(q, k, v, o, do, p["sinks"], rope, tag + "attn_bwd")
    dx, dw_in, dnorm = _inproj_bwd(x, p["norm"], w["w_in"], [dq, dk, dv, dgate], dxn, tag + "inproj_bwd")
    return dx, dict(norm=dnorm.reshape(D_MODEL), w_in=dw_in, sinks=dsinks.reshape(N_Q_HEADS, ATTN_BLOCK).sum(axis=1), w_out=dw_out)


def _local_step(x, target, small, big):
    rope = _rope_tables(x.shape[0])
    saved = []
    for i in range(4):
        if i % 2 == 0:
            x, s = _ssm_layer_fwd(i, x, small[i], big[i])
        else:
            x, s = _attn_layer_fwd(i, x, small[i], big[i], rope)
        saved.append(s)
    loss, dx, dfinal = _loss_head(x, small[4]["norm"], target, "loss_head")
    grads = [None] * 4 + [dict(norm=dfinal.reshape(D_MODEL))]
    for i in (3, 2, 1, 0):
        if i % 2 == 0:
            dx, grads[i] = _ssm_layer_bwd(i, dx, saved[i], small[i], big[i])
        else:
            dx, grads[i] = _attn_layer_bwd(i, dx, saved[i], small[i], big[i], rope)
    return loss[0, 0], dx, grads


def _shard_rows(key, a):
    return a.reshape(-1, D_MODEL)


def _owner_major(key, g):
    if key == "w_in":
        g = g.reshape(D_MODEL, N_CHIPS, N_CORES, -1).transpose(2, 1, 0, 3)
    else:
        g = g.reshape(N_CHIPS, N_CORES, -1, D_MODEL).transpose(1, 0, 2, 3)
    return g.reshape(N_CORES, N_CHIPS, -1, D_MODEL)


def _from_gathered(key, a, shard_shape):
    if key == "w_in":
        cols = shard_shape[1]
        return a.reshape(N_CORES, N_CHIPS, D_MODEL, cols).transpose(2, 1, 0, 3).reshape(D_MODEL, N_DEV * cols)
    return a.transpose(1, 0, 2, 3).reshape(N_DEV * shard_shape[0], shard_shape[1])


SMALL_ROWS = 144


def _pad_rows(flat):
    n = flat.shape[0]
    rows = N_DEV * SMALL_ROWS
    assert n <= rows * D_MODEL
    return jnp.pad(flat, (0, rows * D_MODEL - n)).reshape(rows, D_MODEL)


def kernel(*args):
    names = ["x"]
    layer_names = []
    for i, keys in enumerate(LAYER_KEYS):
        layer_names += ["l%d_%s" % (i, k) for k in keys]
    layer_names.append("final_norm")
    names += layer_names + ["loss_target"] + ["m_" + n for n in layer_names] + ["v_" + n for n in layer_names]
    given = dict(zip(names, args))
    big_names = [n for n in layer_names if n.split("_", 1)[1] in BIG_KEYS]
    small_names = [n for n in layer_names if n not in big_names]

    offsets, rows_at = {}, 0
    for n in big_names:
        offsets[n] = rows_at
        rows_at += given[n].size // D_MODEL
    local_rows = jnp.concatenate([_shard_rows(n, given[n]) for n in big_names], axis=0)
    big_rows = rows_at
    gathered = _all_gather(local_rows.astype(BF16), "gather_weights")
    big = [dict() for _ in range(4)]
    for n in big_names:
        layer, key = int(n[1]), n.split("_", 1)[1]
        rows = given[n].size // D_MODEL
        big[layer][key] = _from_gathered(key, gathered[:, :, offsets[n]:offsets[n] + rows, :], given[n].shape)
    small = [dict() for _ in range(5)]
    for n in small_names:
        if n == "final_norm":
            small[4]["norm"] = given[n]
        else:
            small[int(n[1])][n.split("_", 1)[1]] = given[n]

    loss, dx, grads = _local_step(given["x"][0], given["loss_target"][0], small, big)
    loss = lax.psum(loss, ("x", "y", "c"))

    def grad_of(n):
        return grads[4]["norm"] if n == "final_norm" else grads[int(n[1])][n.split("_", 1)[1]]

    flat = lambda f: _pad_rows(jnp.concatenate([f(n).reshape(-1) for n in small_names]))
    small_grads = flat(grad_of)
    small_rows = small_grads.shape[0] // N_DEV
    send = jnp.concatenate([_owner_major(n.split("_", 1)[1], grad_of(n)) for n in big_names]
                           + [small_grads.reshape(N_CORES, N_CHIPS, small_rows, D_MODEL)], axis=2)
    owner_rows = big_rows + small_rows
    halves = _exchange(send.reshape(N_CORES, N_CHIPS * owner_rows, D_MODEL), "core", True, "scatter_grads_cores")
    chip_sums = _sum_parts(halves, "sum_core_grads").reshape(N_CHIPS, owner_rows, D_MODEL)
    parts = _exchange(chip_sums, "chip", True, "scatter_grads_chips")
    cat = lambda pre: jnp.concatenate([_shard_rows(n, given[pre + n]) for n in big_names], axis=0)
    g_big, d_big, m_big, v_big = _adamw(parts, cat(""), cat("m_"), cat("v_"), "adamw_matrices")

    my_slice = _sum_parts(parts[:, big_rows:], "sum_small_grads")
    g_all = _all_gather(my_slice, "gather_small_grads").reshape(1, N_DEV * small_rows, D_MODEL)
    g_small, d_small, m_small, v_small = _adamw(
        g_all, flat(lambda n: given[n]), flat(lambda n: given["m_" + n]), flat(lambda n: given["v_" + n]), "adamw_small")

    outs = {}
    for tag, a_big, a_small in (("grad_", g_big, g_small), ("delta_", d_big, d_small),
                                ("new_m_", m_big, m_small), ("new_v_", v_big, v_small)):
        for n in big_names:
            rows = given[n].size // D_MODEL
            outs[tag + n] = a_big[offsets[n]:offsets[n] + rows].reshape(given[n].shape)
        a_flat, at = a_small.reshape(-1), 0
        for n in small_names:
            outs[tag + n] = a_flat[at:at + given[n].size].reshape(given[n].shape)
            at += given[n].size
    result = [loss, dx[None]]
    for tag in ("grad_", "delta_", "new_m_", "new_v_"):
        result += [outs[tag + n] for n in layer_names]
    return tuple(result)
```

```python
import functools
import math

import jax
import jax.numpy as jnp
from jax import lax
from jax.experimental import pallas as pl
from jax.experimental.pallas import tpu as pltpu

F32 = jnp.float32
BF16 = jnp.bfloat16

D_MODEL = 1024
SSM_GROUP = 16
SSM_GROUPS = D_MODEL // SSM_GROUP
SSM_STATE = 64
S5_BLOCK = 16
S5_LANES = S5_BLOCK * SSM_GROUP
HEAD_DIM = 64
N_Q_HEADS = 16
N_KV_HEADS = 2
GQA = N_Q_HEADS // N_KV_HEADS
Q_DIM = N_Q_HEADS * HEAD_DIM
KV_DIM = N_KV_HEADS * HEAD_DIM
ATTN_BLOCK = 128
ROPE_THETA = 10000.0
NORM_EPS = 1e-5
NEG_INF = -1e30
ATTN_SCALE = HEAD_DIM ** -0.5
N_DEV = 8

ADAM_LR = 0.001
ADAM_B1 = 0.9
ADAM_B2 = 0.999
ADAM_EPS = 1e-08
ADAM_WD = 0.01
ADAM_STEP = 10

VMEM_LIMIT = 56 * 1024 * 1024
ROWS_FWD = 512
ROWS_BWD = 256

NT = (((1,), (1,)), ((), ()))
TN = (((0,), (0,)), ((), ()))


def _params(n_grid):
    return pltpu.CompilerParams(dimension_semantics=("arbitrary",) * n_grid, vmem_limit_bytes=VMEM_LIMIT)


def _dot(a, b):
    return jnp.dot(a, b, preferred_element_type=F32)


def _dot_nt(a, b):
    return lax.dot_general(a, b, NT, preferred_element_type=F32)


def _dot_tn(a, b):
    return lax.dot_general(a, b, TN, preferred_element_type=F32)


def _sigmoid(x):
    return 1.0 / (1.0 + jnp.exp(-x))


_GELU_K = math.sqrt(2.0 / math.pi)


def _gelu(x):
    return x * (0.5 * (1.0 + jnp.tanh(_GELU_K * (x + 0.044715 * (x * x * x)))))


def _gelu_grad(x):
    t = jnp.tanh(_GELU_K * (x + 0.044715 * (x * x * x)))
    return 0.5 * (1.0 + t) + 0.5 * x * (1.0 - t * t) * (_GELU_K * (1.0 + 3.0 * 0.044715 * (x * x)))


def _row_spec(rows, cols):
    return pl.BlockSpec((rows, cols), lambda i: (i, 0))


def _const_spec(shape):
    zeros = (0,) * len(shape)
    return pl.BlockSpec(shape, lambda i: zeros)


def _rope_apply(t, cos, sin_signed, sign):
    lane = lax.broadcasted_iota(jnp.int32, (1, 128), 1)
    first_half = (lane % HEAD_DIM) < (HEAD_DIM // 2)
    out = []
    for j in range(t.shape[1] // 128):
        tj = t[:, 128 * j:128 * (j + 1)]
        partner = jnp.where(first_half, pltpu.roll(tj, 128 - HEAD_DIM // 2, 1), pltpu.roll(tj, HEAD_DIM // 2, 1))
        out.append(tj * cos + sign * (partner * sin_signed))
    return out[0] if len(out) == 1 else jnp.concatenate(out, axis=1)


def _inproj_fwd(x, norm, w, splits, rope, name):
    t = x.shape[0]
    n = w.shape[1]
    rows = ROWS_FWD

    def body(*refs):
        if rope is None:
            x_ref, n_ref, w_ref = refs[:3]
            outs = refs[3:]
        else:
            x_ref, n_ref, w_ref, cos_ref, sin_ref = refs[:5]
            outs = refs[5:]
        xv = x_ref[...]
        rstd = lax.rsqrt(jnp.mean(xv * xv, axis=-1, keepdims=True) + NORM_EPS)
        h = (xv * rstd) * n_ref[...]
        proj = _dot(h.astype(BF16), w_ref[...])
        off = 0
        for i, width in enumerate(splits):
            piece = proj[:, off:off + width]
            if rope is not None and i < 2:
                piece = _rope_apply(piece, cos_ref[...], sin_ref[...], 1.0)
            outs[i][...] = piece
            off += width

    in_specs = [_row_spec(rows, D_MODEL), _const_spec((1, D_MODEL)), _const_spec((D_MODEL, n))]
    args = [x, norm.reshape(1, D_MODEL), w]
    if rope is not None:
        in_specs += [_row_spec(rows, 128), _row_spec(rows, 128)]
        args += list(rope)
    return pl.pallas_call(
        body, name=name, grid=(t // rows,), in_specs=in_specs,
        out_specs=[_row_spec(rows, width) for width in splits],
        out_shape=[jax.ShapeDtypeStruct((t, width), F32) for width in splits],
        compiler_params=_params(1),
    )(*args)


def _inproj_bwd(x, norm, w, dpieces, dxn, name):
    t = x.shape[0]
    n = w.shape[1]
    rows = ROWS_BWD
    widths = [p.shape[1] for p in dpieces]
    k = len(dpieces)

    def body(*refs):
        x_ref, n_ref, w_ref, dxn_ref = refs[:4]
        d_refs = refs[4:4 + k]
        dx_ref, dw_ref, dn_ref = refs[4 + k:]
        i = pl.program_id(0)
        xv = x_ref[...]
        rstd = lax.rsqrt(jnp.mean(xv * xv, axis=-1, keepdims=True) + NORM_EPS)
        xhat = xv * rstd
        h = xhat * n_ref[...]
        dproj = [r[...].astype(BF16) for r in d_refs]
        dproj = dproj[0] if k == 1 else jnp.concatenate(dproj, axis=1)
        dh = _dot_nt(dproj, w_ref[...])
        dw = _dot_tn(h.astype(BF16), dproj)
        dn = jnp.sum(dh * xhat, axis=0, keepdims=True)

        @pl.when(i == 0)
        def _():
            dw_ref[...] = dw
            dn_ref[...] = dn

        @pl.when(i > 0)
        def _():
            dw_ref[...] += dw
            dn_ref[...] += dn

        dxhat = dh * n_ref[...]
        dx_ref[...] = rstd * (dxhat - xhat * jnp.mean(dxhat * xhat, axis=-1, keepdims=True)) + dxn_ref[...]

    return pl.pallas_call(
        body, name=name, grid=(t // rows,),
        in_specs=[_row_spec(rows, D_MODEL), _const_spec((1, D_MODEL)), _const_spec((D_MODEL, n)),
                  _row_spec(rows, D_MODEL)] + [_row_spec(rows, width) for width in widths],
        out_specs=[_row_spec(rows, D_MODEL), _const_spec((D_MODEL, n)), _const_spec((1, D_MODEL))],
        out_shape=[jax.ShapeDtypeStruct((t, D_MODEL), F32), jax.ShapeDtypeStruct((D_MODEL, n), F32),
                   jax.ShapeDtypeStruct((1, D_MODEL), F32)],
        compiler_params=_params(1),
    )(x, norm.reshape(1, D_MODEL), w, dxn, *dpieces)


def _s5_matrices(a_re, a_im, log_step, b_re, b_im, c_re, c_im):
    r = S5_BLOCK
    step = jnp.exp(log_step)[:, None]
    lr, li = a_re * step, a_im * step
    k = jnp.arange(r + 1, dtype=F32)
    mag = jnp.exp(lr[:, None, :] * k[:, None])
    pr = mag * jnp.cos(li[:, None, :] * k[:, None])
    pi = mag * jnp.sin(li[:, None, :] * k[:, None])
    nr, ni = pr[:, 1] - 1.0, pi[:, 1]
    den = a_re * a_re + a_im * a_im
    qr, qi = (nr * a_re + ni * a_im) / den, (ni * a_re - nr * a_im) / den
    bbr = qr[..., None] * b_re - qi[..., None] * b_im
    bbi = qr[..., None] * b_im + qi[..., None] * b_re
    wr = c_re[:, None] * pr[:, :, None, :] - c_im[:, None] * pi[:, :, None, :]
    wi = c_re[:, None] * pi[:, :, None, :] + c_im[:, None] * pr[:, :, None, :]
    w = jnp.concatenate([wr, -wi], axis=-1)
    bb = jnp.concatenate([bbr, bbi], axis=1)
    kern = jnp.einsum("gxp,gpi->gxi", w[:, :r].reshape(SSM_GROUPS, S5_LANES, 2 * SSM_STATE), bb,
                      precision=lax.Precision.HIGHEST).reshape(SSM_GROUPS, r, SSM_GROUP, SSM_GROUP)
    cols = [jnp.pad(kern[:, :r - s], ((0, 0), (s, 0), (0, 0), (0, 0))) for s in range(r)]
    tm = jnp.stack(cols, axis=3).reshape(SSM_GROUPS, S5_LANES, S5_LANES)
    cpt = w[:, 1:].reshape(SSM_GROUPS, S5_LANES, 2 * SSM_STATE)
    prs = jnp.swapaxes(pr[:, r - 1::-1][:, :r], 1, 2)[..., None]
    pis = jnp.swapaxes(pi[:, r - 1::-1][:, :r], 1, 2)[..., None]
    bp_re = prs * bbr[:, :, None, :] - pis * bbi[:, :, None, :]
    bp_im = prs * bbi[:, :, None, :] + pis * bbr[:, :, None, :]
    bpt = jnp.concatenate([bp_re, bp_im], axis=1).reshape(SSM_GROUPS, 2 * SSM_STATE, S5_LANES)
    ar = pr[:, r].reshape(1, SSM_GROUPS * SSM_STATE)
    ai = pi[:, r].reshape(1, SSM_GROUPS * SSM_STATE)
    return tm, cpt, bpt, ar, ai


S5_OCTET = 128 // SSM_GROUP
S5_STEPS = SSM_GROUPS // S5_OCTET


def _oct_spec(t):
    return pl.BlockSpec((t, 128), lambda j: (0, j))


def _state_spec(nb):
    return pl.BlockSpec((nb, S5_OCTET * SSM_STATE), lambda j: (0, j))


def _gmat_spec(a, b):
    return pl.BlockSpec((S5_OCTET, a, b), lambda j: (j, 0, 0))


def _block_rows(ref, nb):
    return [ref[pl.ds(r, nb, stride=S5_BLOCK), :] for r in range(S5_BLOCK)]


def _group_cols(pieces_t, g):
    return jnp.concatenate([p[SSM_GROUP * g:SSM_GROUP * (g + 1)] for p in pieces_t], axis=0)


def _state_cols(re_t, im_t, g):
    return jnp.concatenate([re_t[SSM_STATE * g:SSM_STATE * (g + 1)], im_t[SSM_STATE * g:SSM_STATE * (g + 1)]], axis=0)


def _s5_project(a, mat, name):
    t = a.shape[0]
    nb = t // S5_BLOCK

    def body(a_ref, m_ref, re_ref, im_ref):
        at = [p.T for p in _block_rows(a_ref, nb)]
        for pair in range(S5_OCTET // 2):
            xs = [_dot(m_ref[2 * pair + k], _group_cols(at, 2 * pair + k).astype(BF16)) for k in (0, 1)]
            lanes = slice(128 * pair, 128 * (pair + 1))
            re_ref[:, lanes] = jnp.concatenate([xs[0][:SSM_STATE], xs[1][:SSM_STATE]], axis=0).T
            im_ref[:, lanes] = jnp.concatenate([xs[0][SSM_STATE:], xs[1][SSM_STATE:]], axis=0).T

    return pl.pallas_call(
        body, name=name, grid=(S5_STEPS,),
        in_specs=[_oct_spec(t), _gmat_spec(2 * SSM_STATE, S5_LANES)],
        out_specs=[_state_spec(nb), _state_spec(nb)],
        out_shape=[jax.ShapeDtypeStruct((nb, SSM_GROUPS * SSM_STATE), F32)] * 2,
        compiler_params=_params(1),
    )(a, mat)


_SCAN_LANES = 1024


def _s5_scan_fwd(xre, xim, ar, ai, name):
    nb = xre.shape[0]
    col = pl.BlockSpec((nb, _SCAN_LANES), lambda j: (0, j))
    par = pl.BlockSpec((1, _SCAN_LANES), lambda j: (0, j))

    def body(xre_ref, xim_ref, ar_ref, ai_ref, hre_ref, him_ref):
        a_r, a_i = ar_ref[...], ai_ref[...]

        def step(b, carry):
            hr, hi = carry
            hre_ref[pl.ds(b, 1), :] = hr
            him_ref[pl.ds(b, 1), :] = hi
            xr, xi = xre_ref[pl.ds(b, 1), :], xim_ref[pl.ds(b, 1), :]
            return a_r * hr - a_i * hi + xr, a_r * hi + a_i * hr + xi

        zero = jnp.zeros((1, _SCAN_LANES), F32)
        lax.fori_loop(0, nb, step, (zero, zero))

    return pl.pallas_call(
        body, name=name, grid=(xre.shape[1] // _SCAN_LANES,),
        in_specs=[col, col, par, par], out_specs=[col, col],
        out_shape=[jax.ShapeDtypeStruct(xre.shape, F32)] * 2,
        compiler_params=_params(1),
    )(xre, xim, ar, ai)


def _s5_scan_bwd(dhre, dhim, hre, him, ar, ai, name):
    nb = dhre.shape[0]
    col = pl.BlockSpec((nb, _SCAN_LANES), lambda j: (0, j))
    par = pl.BlockSpec((1, _SCAN_LANES), lambda j: (0, j))

    def body(dhre_ref, dhim_ref, hre_ref, him_ref, ar_ref, ai_ref, dxre_ref, dxim_ref, dar_ref, dai_ref):
        a_r, a_i = ar_ref[...], ai_ref[...]

        def step(s, carry):
            gr, gi, dar, dai = carry
            b = nb - 1 - s
            dxre_ref[pl.ds(b, 1), :] = gr
            dxim_ref[pl.ds(b, 1), :] = gi
            hr, hi = hre_ref[pl.ds(b, 1), :], him_ref[pl.ds(b, 1), :]
            dar = dar + (hr * gr + hi * gi)
            dai = dai + (hr * gi - hi * gr)
            dr, di = dhre_ref[pl.ds(b, 1), :], dhim_ref[pl.ds(b, 1), :]
            return dr + (a_r * gr + a_i * gi), di + (a_r * gi - a_i * gr), dar, dai

        zero = jnp.zeros((1, _SCAN_LANES), F32)
        _, _, dar, dai = lax.fori_loop(0, nb, step, (zero, zero, zero, zero))
        dar_ref[...] = dar
        dai_ref[...] = dai

    return pl.pallas_call(
        body, name=name, grid=(dhre.shape[1] // _SCAN_LANES,),
        in_specs=[col, col, col, col, par, par], out_specs=[col, col, par, par],
        out_shape=[jax.ShapeDtypeStruct(dhre.shape, F32)] * 2 + [jax.ShapeDtypeStruct(ar.shape, F32)] * 2,
        compiler_params=_params(1),
    )(dhre, dhim, hre, him, ar, ai)


def _s5_outputs(u, hre, him, tm, cpt, d, name):
    t = u.shape[0]
    nb = t // S5_BLOCK

    def body(u_ref, hre_ref, him_ref, tm_ref, cpt_ref, d_ref, y_ref):
        u_rows = _block_rows(u_ref, nb)
        ut = [p.T for p in u_rows]
        hre_t, him_t = hre_ref[...].T, him_ref[...].T
        yts = []
        for g in range(S5_OCTET):
            yts.append(_dot(tm_ref[g], _group_cols(ut, g).astype(BF16))
                       + _dot(cpt_ref[g], _state_cols(hre_t, him_t, g).astype(BF16)))
        for r in range(S5_BLOCK):
            rows = jnp.concatenate([yt[SSM_GROUP * r:SSM_GROUP * (r + 1)] for yt in yts], axis=0)
            y_ref[pl.ds(r, nb, stride=S5_BLOCK), :] = rows.T + d_ref[...] * u_rows[r]

    return pl.pallas_call(
        body, name=name, grid=(S5_STEPS,),
        in_specs=[_oct_spec(t), _state_spec(nb), _state_spec(nb), _gmat_spec(S5_LANES, S5_LANES),
                  _gmat_spec(S5_LANES, 2 * SSM_STATE), _oct_spec(1)],
        out_specs=_oct_spec(t),
        out_shape=jax.ShapeDtypeStruct(u.shape, F32),
        compiler_params=_params(1),
    )(u, hre, him, tm, cpt, d.reshape(1, D_MODEL))


def _s5_backward(dy, u, hre, him, dxre, dxim, tmt, bp, d, name):
    t = u.shape[0]
    nb = t // S5_BLOCK

    def body(dy_ref, u_ref, hre_ref, him_ref, dxre_ref, dxim_ref, tmt_ref, bp_ref, d_ref,
             du_ref, dtm_ref, dcpt_ref, dbpt_ref, dd_ref):
        dy_rows, u_rows = _block_rows(dy_ref, nb), _block_rows(u_ref, nb)
        dyt, ut = [p.T for p in dy_rows], [p.T for p in u_rows]
        hre_t, him_t = hre_ref[...].T, him_ref[...].T
        dxre_t, dxim_t = dxre_ref[...].T, dxim_ref[...].T
        duts = []
        for g in range(S5_OCTET):
            dyg, ug = _group_cols(dyt, g).astype(BF16), _group_cols(ut, g).astype(BF16)
            hg = _state_cols(hre_t, him_t, g).astype(BF16)
            dxg = _state_cols(dxre_t, dxim_t, g).astype(BF16)
            duts.append(_dot(tmt_ref[g], dyg) + _dot(bp_ref[g], dxg))
            dtm_ref[g] = _dot_nt(dyg, ug)
            dcpt_ref[g] = _dot_nt(dyg, hg)
            dbpt_ref[g] = _dot_nt(dxg, ug)
        dd = jnp.zeros((1, 128), F32)
        for r in range(S5_BLOCK):
            rows = jnp.concatenate([dut[SSM_GROUP * r:SSM_GROUP * (r + 1)] for dut in duts], axis=0)
            du_ref[pl.ds(r, nb, stride=S5_BLOCK), :] = rows.T + d_ref[...] * dy_rows[r]
            dd = dd + jnp.sum(dy_rows[r] * u_rows[r], axis=0, keepdims=True)
        dd_ref[...] = dd

    return pl.pallas_call(
        body, name=name, grid=(S5_STEPS,),
        in_specs=[_oct_spec(t), _oct_spec(t), _state_spec(nb), _state_spec(nb), _state_spec(nb), _state_spec(nb),
                  _gmat_spec(S5_LANES, S5_LANES), _gmat_spec(S5_LANES, 2 * SSM_STATE), _oct_spec(1)],
        out_specs=[_oct_spec(t), _gmat_spec(S5_LANES, S5_LANES), _gmat_spec(S5_LANES, 2 * SSM_STATE),
                   _gmat_spec(2 * SSM_STATE, S5_LANES), _oct_spec(1)],
        out_shape=[jax.ShapeDtypeStruct(u.shape, F32),
                   jax.ShapeDtypeStruct((SSM_GROUPS, S5_LANES, S5_LANES), F32),
                   jax.ShapeDtypeStruct((SSM_GROUPS, S5_LANES, 2 * SSM_STATE), F32),
                   jax.ShapeDtypeStruct((SSM_GROUPS, 2 * SSM_STATE, S5_LANES), F32),
                   jax.ShapeDtypeStruct((1, D_MODEL), F32)],
        compiler_params=_params(1),
    )(dy, u, hre, him, dxre, dxim, tmt, bp, d.reshape(1, D_MODEL))


def _ssm_out_fwd(y, gate, x, w_glu, b_glu, w_out, name):
    t = x.shape[0]
    rows = ROWS_FWD

    def body(y_ref, g_ref, x_ref, wg_ref, bg_ref, wo_ref, o_ref):
        z0 = _gelu(y_ref[...])
        s = _dot(z0.astype(BF16), wg_ref[...]) + bg_ref[...]
        gate_v = g_ref[...]
        a = (z0 * _sigmoid(s)) * (gate_v * _sigmoid(gate_v))
        o_ref[...] = x_ref[...] + _dot(a.astype(BF16), wo_ref[...])

    return pl.pallas_call(
        body, name=name, grid=(t // rows,),
        in_specs=[_row_spec(rows, D_MODEL)] * 3 + [_const_spec((D_MODEL, D_MODEL)), _const_spec((1, D_MODEL)),
                                                   _const_spec((D_MODEL, D_MODEL))],
        out_specs=_row_spec(rows, D_MODEL),
        out_shape=jax.ShapeDtypeStruct((t, D_MODEL), F32),
        compiler_params=_params(1),
    )(y, gate, x, w_glu, b_glu.reshape(1, D_MODEL), w_out)


def _ssm_out_bwd(dxn, y, gate, w_glu, b_glu, w_out, name):
    t = y.shape[0]
    rows = ROWS_BWD

    def body(dxn_ref, y_ref, g_ref, wg_ref, bg_ref, wo_ref, dy_ref, dg_ref, dwg_ref, dbg_ref, dwo_ref):
        i = pl.program_id(0)
        yv = y_ref[...]
        z0 = _gelu(yv)
        z0b = z0.astype(BF16)
        sg = _sigmoid(_dot(z0b, wg_ref[...]) + bg_ref[...])
        z = z0 * sg
        gate_v = g_ref[...]
        sgg = _sigmoid(gate_v)
        silu = gate_v * sgg
        a = z * silu
        dob = dxn_ref[...].astype(BF16)
        da = _dot_nt(dob, wo_ref[...])
        dwo = _dot_tn(a.astype(BF16), dob)
        dz = da * silu
        dg_ref[...] = da * z * (sgg * (1.0 + gate_v * (1.0 - sgg)))
        ds = dz * z0 * (sg * (1.0 - sg))
        dsb = ds.astype(BF16)
        dz0 = dz * sg + _dot_nt(dsb, wg_ref[...])
        dwg = _dot_tn(z0b, dsb)
        dbg = jnp.sum(ds, axis=0, keepdims=True)
        dy_ref[...] = dz0 * _gelu_grad(yv)

        @pl.when(i == 0)
        def _():
            dwo_ref[...] = dwo
            dwg_ref[...] = dwg
            dbg_ref[...] = dbg

        @pl.when(i > 0)
        def _():
            dwo_ref[...] += dwo
            dwg_ref[...] += dwg
            dbg_ref[...] += dbg

    sq = _const_spec((D_MODEL, D_MODEL))
    vec = _const_spec((1, D_MODEL))
    return pl.pallas_call(
        body, name=name, grid=(t // rows,),
        in_specs=[_row_spec(rows, D_MODEL)] * 3 + [sq, vec, sq],
        out_specs=[_row_spec(rows, D_MODEL), _row_spec(rows, D_MODEL), sq, vec, sq],
        out_shape=[jax.ShapeDtypeStruct((t, D_MODEL), F32)] * 2 + [
            jax.ShapeDtypeStruct((D_MODEL, D_MODEL), F32), jax.ShapeDtypeStruct((1, D_MODEL), F32),
            jax.ShapeDtypeStruct((D_MODEL, D_MODEL), F32)],
        compiler_params=_params(1),
    )(dxn, y, gate, w_glu, b_glu.reshape(1, D_MODEL), w_out)


KV_LANES = GQA * ATTN_BLOCK


def _attn_bias(block_is_first):
    kj = lax.broadcasted_iota(jnp.int32, (2 * ATTN_BLOCK, ATTN_BLOCK), 0)
    qi = lax.broadcasted_iota(jnp.int32, (2 * ATTN_BLOCK, ATTN_BLOCK), 1)
    dist = qi + ATTN_BLOCK - kj
    valid = (dist >= 0) & (dist < ATTN_BLOCK) & (jnp.logical_not(block_is_first) | (kj >= ATTN_BLOCK))
    return jnp.tile(jnp.where(valid, 0.0, NEG_INF).astype(F32), (1, GQA))


def _head_cols(a_t, kvh):
    heads = range(kvh * GQA, (kvh + 1) * GQA)
    return jnp.concatenate([a_t[HEAD_DIM * h:HEAD_DIM * (h + 1)] for h in heads], axis=1)


def _head_rows(a_cols):
    stacked = jnp.concatenate([a_cols[:, ATTN_BLOCK * g:ATTN_BLOCK * (g + 1)] for g in range(GQA)], axis=0)
    return stacked.T


def _kv_rows(prev_ref, cur_ref, kvh):
    lanes = slice(HEAD_DIM * kvh, HEAD_DIM * (kvh + 1))
    return jnp.concatenate([prev_ref[:, lanes], cur_ref[:, lanes]], axis=0).astype(BF16)


def _kv_cols(prev_t, cur_t, kvh):
    rows = slice(HEAD_DIM * kvh, HEAD_DIM * (kvh + 1))
    return jnp.concatenate([prev_t[rows], cur_t[rows]], axis=1).astype(BF16)


def _attn_probs(kk, q_cols, sink_row, bias):
    s = _dot(kk, q_cols) * ATTN_SCALE + bias
    m = jnp.maximum(jnp.max(s, axis=0, keepdims=True), sink_row)
    p = jnp.exp(s - m)
    e_sink = jnp.exp(sink_row - m)
    inv = 1.0 / (jnp.sum(p, axis=0, keepdims=True) + e_sink)
    return p * inv, e_sink * inv


def _sink_cols(sinks):
    return jnp.repeat(sinks, ATTN_BLOCK).reshape(N_KV_HEADS, 1, KV_LANES)


def _attn_fwd(q, k, v, sinks, name):
    t = q.shape[0]
    nblk = t // ATTN_BLOCK

    def body(s_ref, q_ref, kc_ref, kp_ref, vc_ref, vp_ref, o_ref):
        bias = _attn_bias(pl.program_id(0) == 0)
        q_t = q_ref[...].T
        vp_t, vc_t = vp_ref[...].T, vc_ref[...].T
        for kvh in range(N_KV_HEADS):
            p, _ = _attn_probs(_kv_rows(kp_ref, kc_ref, kvh), _head_cols(q_t, kvh).astype(BF16), s_ref[kvh], bias)
            o_cols = _dot(_kv_cols(vp_t, vc_t, kvh), p.astype(BF16))
            o_ref[:, GQA * HEAD_DIM * kvh:GQA * HEAD_DIM * (kvh + 1)] = _head_rows(o_cols)

    cur = lambda i: (i, 0)
    prev = lambda i: (jnp.maximum(i - 1, 0), 0)
    return pl.pallas_call(
        body, name=name, grid=(nblk,),
        in_specs=[_const_spec((N_KV_HEADS, 1, KV_LANES)),
                  pl.BlockSpec((ATTN_BLOCK, Q_DIM), cur),
                  pl.BlockSpec((ATTN_BLOCK, KV_DIM), cur), pl.BlockSpec((ATTN_BLOCK, KV_DIM), prev),
                  pl.BlockSpec((ATTN_BLOCK, KV_DIM), cur), pl.BlockSpec((ATTN_BLOCK, KV_DIM), prev)],
        out_specs=pl.BlockSpec((ATTN_BLOCK, Q_DIM), cur),
        out_shape=jax.ShapeDtypeStruct((t, Q_DIM), F32),
        compiler_params=_params(1),
    )(_sink_cols(sinks), q, k, k, v, v)


def _attn_bwd(q, k, v, o, do, sinks, rope, name):
    t = q.shape[0]
    nblk = t // ATTN_BLOCK

    def body(s_ref, q_ref, o_ref, do_ref, kp_ref, kc_ref, vp_ref, vc_ref, cosq_ref, sinq_ref, cosk_ref, sinkey_ref,
             dq_ref, dk_ref, dv_ref, ds_ref, new_k, new_v, wait_k, wait_v):
        n = pl.program_id(0)

        @pl.when(n == 0)
        def _():
            ds_ref[...] = jnp.zeros_like(ds_ref)
            wait_k[...] = jnp.zeros_like(wait_k)
            wait_v[...] = jnp.zeros_like(wait_v)

        @pl.when(n < nblk)
        def _():
            bias = _attn_bias(n == 0)
            q_t, o_t, do_t = q_ref[...].T, o_ref[...].T, do_ref[...].T
            kp_t, kc_t = kp_ref[...].T, kc_ref[...].T
            for kvh in range(N_KV_HEADS):
                q_cols = _head_cols(q_t, kvh).astype(BF16)
                do_cols = _head_cols(do_t, kvh)
                delta = jnp.sum(do_cols * _head_cols(o_t, kvh), axis=0, keepdims=True)
                do_cols = do_cols.astype(BF16)
                p, p_sink = _attn_probs(_kv_rows(kp_ref, kc_ref, kvh), q_cols, s_ref[kvh], bias)
                dp = _dot(_kv_rows(vp_ref, vc_ref, kvh), do_cols)
                ds = (p * (dp - delta) * ATTN_SCALE).astype(BF16)
                lanes = slice(GQA * HEAD_DIM * kvh, GQA * HEAD_DIM * (kvh + 1))
                dq_ref[:, lanes] = _head_rows(_dot(_kv_cols(kp_t, kc_t, kvh), ds))
                head = slice(HEAD_DIM * kvh, HEAD_DIM * (kvh + 1))
                new_k[:, head] = _dot_nt(ds, q_cols)
                new_v[:, head] = _dot_nt(p.astype(BF16), do_cols)
                ds_ref[kvh] += -(p_sink * delta)
            dq_ref[...] = _rope_apply(dq_ref[...], cosq_ref[...], sinq_ref[...], -1.0)

        @pl.when(n == nblk)
        def _():
            new_k[...] = jnp.zeros_like(new_k)
            new_v[...] = jnp.zeros_like(new_v)

        dk_ref[...] = _rope_apply(wait_k[...] + new_k[:ATTN_BLOCK], cosk_ref[...], sinkey_ref[...], -1.0)
        dv_ref[...] = wait_v[...] + new_v[:ATTN_BLOCK]
        wait_k[...] = new_k[ATTN_BLOCK:]
        wait_v[...] = new_v[ATTN_BLOCK:]

    cur = lambda i: (jnp.minimum(i, nblk - 1), 0)
    prev = lambda i: (jnp.maximum(i - 1, 0), 0)
    qs = lambda f: pl.BlockSpec((ATTN_BLOCK, Q_DIM), f)
    ks = lambda f: pl.BlockSpec((ATTN_BLOCK, KV_DIM), f)
    sink_spec = _const_spec((N_KV_HEADS, 1, KV_LANES))
    return pl.pallas_call(
        body, name=name, grid=(nblk + 1,),
        in_specs=[sink_spec, qs(cur), qs(cur), qs(cur), ks(prev), ks(cur), ks(prev), ks(cur),
                  ks(cur), ks(cur), ks(prev), ks(prev)],
        out_specs=[qs(cur), ks(prev), ks(prev), sink_spec],
        out_shape=[jax.ShapeDtypeStruct((t, Q_DIM), F32), jax.ShapeDtypeStruct((t, KV_DIM), F32),
                   jax.ShapeDtypeStruct((t, KV_DIM), F32), jax.ShapeDtypeStruct((N_KV_HEADS, 1, KV_LANES), F32)],
        scratch_shapes=[pltpu.VMEM((2 * ATTN_BLOCK, KV_DIM), F32), pltpu.VMEM((2 * ATTN_BLOCK, KV_DIM), F32),
                        pltpu.VMEM((ATTN_BLOCK, KV_DIM), F32), pltpu.VMEM((ATTN_BLOCK, KV_DIM), F32)],
        compiler_params=_params(1),
    )(_sink_cols(sinks), q, o, do, k, k, v, v, rope[0], rope[1], rope[0], rope[1])


def _attn_out_fwd(o, gate, x, w_out, name):
    t = x.shape[0]
    rows = ROWS_FWD

    def body(o_ref, g_ref, x_ref, wo_ref, xn_ref):
        gate_v = g_ref[...]
        a = o_ref[...] * (gate_v * _sigmoid(gate_v))
        xn_ref[...] = x_ref[...] + _dot(a.astype(BF16), wo_ref[...])

    return pl.pallas_call(
        body, name=name, grid=(t // rows,),
        in_specs=[_row_spec(rows, D_MODEL)] * 3 + [_const_spec((D_MODEL, D_MODEL))],
        out_specs=_row_spec(rows, D_MODEL),
        out_shape=jax.ShapeDtypeStruct((t, D_MODEL), F32),
        compiler_params=_params(1),
    )(o, gate, x, w_out)


def _attn_out_bwd(dxn, o, gate, w_out, name):
    t = o.shape[0]
    rows = ROWS_BWD

    def body(dxn_ref, o_ref, g_ref, wo_ref, do_ref, dg_ref, dwo_ref):
        i = pl.program_id(0)
        gate_v, ov = g_ref[...], o_ref[...]
        sgg = _sigmoid(gate_v)
        silu = gate_v * sgg
        dob = dxn_ref[...].astype(BF16)
        da = _dot_nt(dob, wo_ref[...])
        dwo = _dot_tn((ov * silu).astype(BF16), dob)
        do_ref[...] = da * silu
        dg_ref[...] = da * ov * (sgg * (1.0 + gate_v * (1.0 - sgg)))

        @pl.when(i == 0)
        def _():
            dwo_ref[...] = dwo

        @pl.when(i > 0)
        def _():
            dwo_ref[...] += dwo

    sq = _const_spec((D_MODEL, D_MODEL))
    return pl.pallas_call(
        body, name=name, grid=(t // rows,),
        in_specs=[_row_spec(rows, D_MODEL)] * 3 + [sq],
        out_specs=[_row_spec(rows, D_MODEL), _row_spec(rows, D_MODEL), sq],
        out_shape=[jax.ShapeDtypeStruct((t, D_MODEL), F32)] * 2 + [jax.ShapeDtypeStruct((D_MODEL, D_MODEL), F32)],
        compiler_params=_params(1),
    )(dxn, o, gate, w_out)


def _loss_head(x, norm, target, name):
    t = x.shape[0]
    rows = ROWS_FWD

    def body(x_ref, n_ref, t_ref, loss_ref, dx_ref, dn_ref):
        i = pl.program_id(0)
        xv = x_ref[...]
        rstd = lax.rsqrt(jnp.mean(xv * xv, axis=-1, keepdims=True) + NORM_EPS)
        xhat = xv * rstd
        err = xhat * n_ref[...] - t_ref[...]
        part = 0.5 * jnp.sum(jnp.mean(err * err, axis=-1, keepdims=True), axis=0, keepdims=True)
        dy = err * (1.0 / D_MODEL)
        dn = jnp.sum(dy * xhat, axis=0, keepdims=True)
        dxhat = dy * n_ref[...]
        dx_ref[...] = rstd * (dxhat - xhat * jnp.mean(dxhat * xhat, axis=-1, keepdims=True))

        @pl.when(i == 0)
        def _():
            loss_ref[...] = jnp.zeros((8, 128), F32) + part
            dn_ref[...] = dn

        @pl.when(i > 0)
        def _():
            loss_ref[...] += part
            dn_ref[...] += dn

    return pl.pallas_call(
        body, name=name, grid=(t // rows,),
        in_specs=[_row_spec(rows, D_MODEL), _const_spec((1, D_MODEL)), _row_spec(rows, D_MODEL)],
        out_specs=[_const_spec((8, 128)), _row_spec(rows, D_MODEL), _const_spec((1, D_MODEL))],
        out_shape=[jax.ShapeDtypeStruct((8, 128), F32), jax.ShapeDtypeStruct((t, D_MODEL), F32),
                   jax.ShapeDtypeStruct((1, D_MODEL), F32)],
        compiler_params=_params(1),
    )(x, norm.reshape(1, D_MODEL), target)


N_CHIPS = 4
N_CORES = 2
CHIP_FLIPS = ((0, 1, 0), (1, 0, 0), (1, 1, 0))
CORE_FLIPS = ((0, 0, 1),)


MAX_CHUNKS = {"chip": 8, "core": 16}


def _n_chunks(rows, dtype, most):
    unit = 16 if dtype == BF16 else 8
    return max(n for n in range(1, most + 1) if rows % n == 0 and (rows // n) % unit == 0)


def _exchange(send, among, per_dest, name):
    flips, parties = (CHIP_FLIPS, N_CHIPS) if among == "chip" else (CORE_FLIPS, N_CORES)
    rows, cols = send.shape[-2:]
    chunks = _n_chunks(rows, send.dtype, MAX_CHUNKS[among])
    chunk_rows = rows // chunks

    def body(send_ref, recv_ref, send_sems, recv_sems, local_sems):
        x, y, c = lax.axis_index("x"), lax.axis_index("y"), lax.axis_index("c")

        def number(px, py, pc):
            return 2 * px + py if among == "chip" else pc

        me = number(x, y, c)

        def peer(k):
            fx, fy, fc = flips[k]
            to = (x + fx - 2 * x * fx, y + fy - 2 * y * fy, c + fc - 2 * c * fc)
            return to, number(*to)

        def src(to_number, j):
            part = pl.ds(j * chunk_rows, chunk_rows)
            return send_ref.at[to_number, part] if per_dest else send_ref.at[part]

        def copy(k, j, landing):
            to, to_number = peer(k)
            return pltpu.make_async_remote_copy(
                src_ref=src(to_number, j), dst_ref=recv_ref.at[landing, pl.ds(j * chunk_rows, chunk_rows)],
                send_sem=send_sems.at[k, j], recv_sem=recv_sems.at[k, j],
                device_id=to, device_id_type=pl.DeviceIdType.MESH)

        def own(j):
            return pltpu.make_async_copy(src(me, j), recv_ref.at[me, pl.ds(j * chunk_rows, chunk_rows)], local_sems.at[j])

        for j in range(chunks):
            for k in range(len(flips)):
                copy(k, j, me).start()
            own(j).start()
        for j in range(chunks):
            for k in range(len(flips)):
                copy(k, j, me).wait_send()
                copy(k, j, peer(k)[1]).wait_recv()
            own(j).wait()

    hbm = pl.BlockSpec(memory_space=pltpu.HBM)
    return pl.pallas_call(
        body, name=name, in_specs=[hbm], out_specs=hbm,
        out_shape=jax.ShapeDtypeStruct((parties, rows, cols), send.dtype),
        scratch_shapes=[pltpu.SemaphoreType.DMA((len(flips), chunks)), pltpu.SemaphoreType.DMA((len(flips), chunks)),
                        pltpu.SemaphoreType.DMA((chunks,))],
    )(send)


def _all_gather(a, name):
    rows, cols = a.shape
    by_chip = _exchange(a, "chip", False, name + "_chips").reshape(N_CHIPS * rows, cols)
    return _exchange(by_chip, "core", False, name + "_cores").reshape(N_CORES, N_CHIPS, rows, cols)


def _sum_parts(parts, name):
    k, n, cols = parts.shape
    rows = min(n, 256)
    while n % rows:
        rows -= 8

    def body(p_ref, o_ref):
        acc = p_ref[0]
        for s in range(1, k):
            acc = acc + p_ref[s]
        o_ref[...] = acc

    return pl.pallas_call(
        body, name=name, grid=(n // rows,),
        in_specs=[pl.BlockSpec((k, rows, cols), lambda i: (0, i, 0))],
        out_specs=_row_spec(rows, cols),
        out_shape=jax.ShapeDtypeStruct((n, cols), F32),
        compiler_params=_params(1),
    )(parts)


def _adamw(parts, w, m, v, name):
    n, cols = w.shape
    k = parts.shape[0]
    rows = min(n, 256)
    while n % rows:
        rows -= 8
    c1 = 1.0 - ADAM_B1 ** ADAM_STEP
    c2 = 1.0 - ADAM_B2 ** ADAM_STEP

    def body(p_ref, w_ref, m_ref, v_ref, g_ref, d_ref, nm_ref, nv_ref):
        g = p_ref[0]
        for s in range(1, k):
            g = g + p_ref[s]
        nm = ADAM_B1 * m_ref[...] + (1.0 - ADAM_B1) * g
        nv = ADAM_B2 * v_ref[...] + (1.0 - ADAM_B2) * (g * g)
        g_ref[...] = g
        nm_ref[...] = nm
        nv_ref[...] = nv
        d_ref[...] = -ADAM_LR * ((nm / c1) / (jnp.sqrt(nv / c2) + ADAM_EPS) + ADAM_WD * w_ref[...])

    blk = _row_spec(rows, cols)
    return pl.pallas_call(
        body, name=name, grid=(n // rows,),
        in_specs=[pl.BlockSpec((k, rows, cols), lambda i: (0, i, 0)), blk, blk, blk],
        out_specs=[blk] * 4,
        out_shape=[jax.ShapeDtypeStruct((n, cols), F32)] * 4,
        compiler_params=_params(1),
    )(parts, w, m, v)


SSM_KEYS = ("norm", "w_in", "a_re", "a_im", "log_step", "b_re", "b_im", "c_re", "c_im", "d", "w_glu", "b_glu", "w_out")
ATTN_KEYS = ("norm", "w_in", "sinks", "w_out")
LAYER_KEYS = (SSM_KEYS, ATTN_KEYS, SSM_KEYS, ATTN_KEYS)
BIG_KEYS = ("w_in", "w_glu", "w_out")
ATTN_SPLITS = (Q_DIM, KV_DIM, KV_DIM, D_MODEL)


def _rope_tables(t):
    pos = jnp.arange(t, dtype=F32)
    inv_freq = ROPE_THETA ** (-jnp.arange(0, HEAD_DIM, 2, dtype=F32) / HEAD_DIM)
    ang = pos[:, None] * inv_freq[None, :]
    cos, sin = jnp.cos(ang), jnp.sin(ang)
    return jnp.tile(jnp.concatenate([cos, cos], axis=1), (1, 2)), jnp.tile(jnp.concatenate([-sin, sin], axis=1), (1, 2))


def _ssm_layer_fwd(i, x, p, w):
    tag = "l%d_" % i
    mats, mats_vjp = jax.vjp(_s5_matrices, p["a_re"], p["a_im"], p["log_step"], p["b_re"], p["b_im"], p["c_re"], p["c_im"])
    tm, cpt, bpt, ar, ai = mats
    mb = dict(tm=tm.astype(BF16), tmt=jnp.swapaxes(tm, 1, 2).astype(BF16), cpt=cpt.astype(BF16),
              cp=jnp.swapaxes(cpt, 1, 2).astype(BF16), bpt=bpt.astype(BF16), bp=jnp.swapaxes(bpt, 1, 2).astype(BF16))
    u, gate = _inproj_fwd(x, p["norm"], w["w_in"], (D_MODEL, D_MODEL), None, tag + "inproj_fwd")
    xre, xim = _s5_project(u, mb["bpt"], tag + "s5_block_inputs")
    hre, him = _s5_scan_fwd(xre, xim, ar, ai, tag + "s5_scan_fwd")
    y = _s5_outputs(u, hre, him, mb["tm"], mb["cpt"], p["d"], tag + "s5_outputs")
    xn = _ssm_out_fwd(y, gate, x, w["w_glu"], p["b_glu"], w["w_out"], tag + "out_fwd")
    return xn, (x, u, gate, y, hre, him, mb, ar, ai, mats_vjp)


def _ssm_layer_bwd(i, dxn, saved, p, w):
    tag = "l%d_" % i
    x, u, gate, y, hre, him, mb, ar, ai, mats_vjp = saved
    dy, dgate, dw_glu, db_glu, dw_out = _ssm_out_bwd(dxn, y, gate, w["w_glu"], p["b_glu"], w["w_out"], tag + "out_bwd")
    dhre, dhim = _s5_project(dy, mb["cp"], tag + "s5_state_grads")
    dxre, dxim, dar, dai = _s5_scan_bwd(dhre, dhim, hre, him, ar, ai, tag + "s5_scan_bwd")
    du, dtm, dcpt, dbpt, dd = _s5_backward(dy, u, hre, him, dxre, dxim, mb["tmt"], mb["bp"], p["d"], tag + "s5_backward")
    da_re, da_im, dlog_step, db_re, db_im, dc_re, dc_im = mats_vjp((dtm, dcpt, dbpt, dar, dai))
    dx, dw_in, dnorm = _inproj_bwd(x, p["norm"], w["w_in"], [du, dgate], dxn, tag + "inproj_bwd")
    grads = dict(norm=dnorm.reshape(D_MODEL), w_in=dw_in, a_re=da_re, a_im=da_im, log_step=dlog_step, b_re=db_re,
                 b_im=db_im, c_re=dc_re, c_im=dc_im, d=dd.reshape(D_MODEL), w_glu=dw_glu, b_glu=db_glu.reshape(D_MODEL),
                 w_out=dw_out)
    return dx, grads


def _attn_layer_fwd(i, x, p, w, rope):
    tag = "l%d_" % i
    q, k, v, gate = _inproj_fwd(x, p["norm"], w["w_in"], ATTN_SPLITS, rope, tag + "inproj_fwd")
    o = _attn_fwd(q, k, v, p["sinks"], tag + "attn_fwd")
    xn = _attn_out_fwd(o, gate, x, w["w_out"], tag + "out_fwd")
    return xn, (x, q, k, v, gate, o)


def _attn_layer_bwd(i, dxn, saved, p, w, rope):
    tag = "l%d_" % i
    x, q, k, v, gate, o = saved
    do, dgate, dw_out = _attn_out_bwd(dxn, o, gate, w["w_out"], tag + "out_bwd")
    dq, dk, dv, dsinks = _attn_bwd(q, k, v, o, do, p["sinks"], rope, tag + "attn_bwd")
    dx, dw_in, dnorm = _inproj_bwd(x, p["norm"], w["w_in"], [dq, dk, dv, dgate], dxn, tag + "inproj_bwd")
    return dx, dict(norm=dnorm.reshape(D_MODEL), w_in=dw_in, sinks=dsinks.reshape(N_Q_HEADS, ATTN_BLOCK).sum(axis=1), w_out=dw_out)


def _local_step(x, target, small, big):
    rope = _rope_tables(x.shape[0])
    saved = []
    for i in range(4):
        if i % 2 == 0:
            x, s = _ssm_layer_fwd(i, x, small[i], big[i])
        else:
            x, s = _attn_layer_fwd(i, x, small[i], big[i], rope)
        saved.append(s)
    loss, dx, dfinal = _loss_head(x, small[4]["norm"], target, "loss_head")
    grads = [None] * 4 + [dict(norm=dfinal.reshape(D_MODEL))]
    for i in (3, 2, 1, 0):
        if i % 2 == 0:
            dx, grads[i] = _ssm_layer_bwd(i, dx, saved[i], small[i], big[i])
        else:
            dx, grads[i] = _attn_layer_bwd(i, dx, saved[i], small[i], big[i], rope)
    return loss[0, 0], dx, grads


def _shard_rows(key, a):
    return a.reshape(-1, D_MODEL)


def _owner_major(key, g):
    if key == "w_in":
        g = g.reshape(D_MODEL, N_CHIPS, N_CORES, -1).transpose(2, 1, 0, 3)
    else:
        g = g.reshape(N_CHIPS, N_CORES, -1, D_MODEL).transpose(1, 0, 2, 3)
    return g.reshape(N_CORES, N_CHIPS, -1, D_MODEL)


def _from_gathered(key, a, shard_shape):
    if key == "w_in":
        cols = shard_shape[1]
        return a.reshape(N_CORES, N_CHIPS, D_MODEL, cols).transpose(2, 1, 0, 3).reshape(D_MODEL, N_DEV * cols)
    return a.transpose(1, 0, 2, 3).reshape(N_DEV * shard_shape[0], shard_shape[1])


SMALL_ROWS = 144


def _pad_rows(flat):
    n = flat.shape[0]
    rows = N_DEV * SMALL_ROWS
    assert n <= rows * D_MODEL
    return jnp.pad(flat, (0, rows * D_MODEL - n)).reshape(rows, D_MODEL)


def kernel(*args):
    names = ["x"]
    layer_names = []
    for i, keys in enumerate(LAYER_KEYS):
        layer_names += ["l%d_%s" % (i, k) for k in keys]
    layer_names.append("final_norm")
    names += layer_names + ["loss_target"] + ["m_" + n for n in layer_names] + ["v_" + n for n in layer_names]
    given = dict(zip(names, args))
    big_names = [n for n in layer_names if n.split("_", 1)[1] in BIG_KEYS]
    small_names = [n for n in layer_names if n not in big_names]

    offsets, rows_at = {}, 0
    for n in big_names:
        offsets[n] = rows_at
        rows_at += given[n].size // D_MODEL
    local_rows = jnp.concatenate([_shard_rows(n, given[n]) for n in big_names], axis=0)
    big_rows = rows_at
    gathered = _all_gather(local_rows.astype(BF16), "gather_weights")
    big = [dict() for _ in range(4)]
    for n in big_names:
        layer, key = int(n[1]), n.split("_", 1)[1]
        rows = given[n].size // D_MODEL
        big[layer][key] = _from_gathered(key, gathered[:, :, offsets[n]:offsets[n] + rows, :], given[n].shape)
    small = [dict() for _ in range(5)]
    for n in small_names:
        if n == "final_norm":
            small[4]["norm"] = given[n]
        else:
            small[int(n[1])][n.split("_", 1)[1]] = given[n]

    loss, dx, grads = _local_step(given["x"][0], given["loss_target"][0], small, big)
    loss = lax.psum(loss, ("x", "y", "c"))

    def grad_of(n):
        return grads[4]["norm"] if n == "final_norm" else grads[int(n[1])][n.split("_", 1)[1]]

    flat = lambda f: _pad_rows(jnp.concatenate([f(n).reshape(-1) for n in small_names]))
    small_grads = flat(grad_of)
    small_rows = small_grads.shape[0] // N_DEV
    send = jnp.concatenate([_owner_major(n.split("_", 1)[1], grad_of(n)) for n in big_names]
                           + [small_grads.reshape(N_CORES, N_CHIPS, small_rows, D_MODEL)], axis=2)
    owner_rows = big_rows + small_rows
    halves = _exchange(send.reshape(N_CORES, N_CHIPS * owner_rows, D_MODEL), "core", True, "scatter_grads_cores")
    chip_sums = _sum_parts(halves, "sum_core_grads").reshape(N_CHIPS, owner_rows, D_MODEL)
    parts = _exchange(chip_sums, "chip", True, "scatter_grads_chips")
    cat = lambda pre: jnp.concatenate([_shard_rows(n, given[pre + n]) for n in big_names], axis=0)
    g_big, d_big, m_big, v_big = _adamw(parts, cat(""), cat("m_"), cat("v_"), "adamw_matrices")

    my_slice = _sum_parts(parts[:, big_rows:], "sum_small_grads")
    g_all = _all_gather(my_slice, "gather_small_grads").reshape(1, N_DEV * small_rows, D_MODEL)
    g_small, d_small, m_small, v_small = _adamw(
        g_all, flat(lambda n: given[n]), flat(lambda n: given["m_" + n]), flat(lambda n: given["v_" + n]), "adamw_small")

    outs = {}
    for tag, a_big, a_small in (("grad_", g_big, g_small), ("delta_", d_big, d_small),
                                ("new_m_", m_big, m_small), ("new_v_", v_big, v_small)):
        for n in big_names:
            rows = given[n].size // D_MODEL
            outs[tag + n] = a_big[offsets[n]:offsets[n] + rows].reshape(given[n].shape)
        a_flat, at = a_small.reshape(-1), 0
        for n in small_names:
            outs[tag + n] = a_flat[at:at + given[n].size].reshape(given[n].shape)
            at += given[n].size
    result = [loss, dx[None]]
    for tag in ("grad_", "delta_", "new_m_", "new_v_"):
        result += [outs[tag + n] for n in layer_names]
    return tuple(result)
```

```python
import functools
import math

import jax
import jax.numpy as jnp
from jax import lax
from jax.experimental import pallas as pl
from jax.experimental.pallas import tpu as pltpu

F32 = jnp.float32
BF16 = jnp.bfloat16

D_MODEL = 1024
SSM_GROUP = 16
SSM_GROUPS = D_MODEL // SSM_GROUP
SSM_STATE = 64
S5_BLOCK = 16
S5_LANES = S5_BLOCK * SSM_GROUP
HEAD_DIM = 64
N_Q_HEADS = 16
N_KV_HEADS = 2
GQA = N_Q_HEADS // N_KV_HEADS
Q_DIM = N_Q_HEADS * HEAD_DIM
KV_DIM = N_KV_HEADS * HEAD_DIM
ATTN_BLOCK = 128
ROPE_THETA = 10000.0
NORM_EPS = 1e-5
NEG_INF = -1e30
ATTN_SCALE = HEAD_DIM ** -0.5
N_DEV = 8

ADAM_LR = 0.001
ADAM_B1 = 0.9
ADAM_B2 = 0.999
ADAM_EPS = 1e-08
ADAM_WD = 0.01
ADAM_STEP = 10

VMEM_LIMIT = 56 * 1024 * 1024
ROWS_FWD = 512
ROWS_BWD = 256

NT = (((1,), (1,)), ((), ()))
TN = (((0,), (0,)), ((), ()))


def _params(n_grid):
    return pltpu.CompilerParams(dimension_semantics=("arbitrary",) * n_grid, vmem_limit_bytes=VMEM_LIMIT)


def _dot(a, b):
    return jnp.dot(a, b, preferred_element_type=F32)


def _dot_nt(a, b):
    return lax.dot_general(a, b, NT, preferred_element_type=F32)


def _dot_tn(a, b):
    return lax.dot_general(a, b, TN, preferred_element_type=F32)


def _sigmoid(x):
    return 1.0 / (1.0 + jnp.exp(-x))


_GELU_K = math.sqrt(2.0 / math.pi)


def _gelu(x):
    return x * (0.5 * (1.0 + jnp.tanh(_GELU_K * (x + 0.044715 * (x * x * x)))))


def _gelu_grad(x):
    t = jnp.tanh(_GELU_K * (x + 0.044715 * (x * x * x)))
    return 0.5 * (1.0 + t) + 0.5 * x * (1.0 - t * t) * (_GELU_K * (1.0 + 3.0 * 0.044715 * (x * x)))


def _row_spec(rows, cols):
    return pl.BlockSpec((rows, cols), lambda i: (i, 0))


def _const_spec(shape):
    zeros = (0,) * len(shape)
    return pl.BlockSpec(shape, lambda i: zeros)


def _rope_apply(t, cos, sin_signed, sign):
    lane = lax.broadcasted_iota(jnp.int32, (1, 128), 1)
    first_half = (lane % HEAD_DIM) < (HEAD_DIM // 2)
    out = []
    for j in range(t.shape[1] // 128):
        tj = t[:, 128 * j:128 * (j + 1)]
        partner = jnp.where(first_half, pltpu.roll(tj, 128 - HEAD_DIM // 2, 1), pltpu.roll(tj, HEAD_DIM // 2, 1))
        out.append(tj * cos + sign * (partner * sin_signed))
    return out[0] if len(out) == 1 else jnp.concatenate(out, axis=1)


def _inproj_fwd(x, norm, w, splits, rope, name):
    t = x.shape[0]
    n = w.shape[1]
    rows = ROWS_FWD

    def body(*refs):
        if rope is None:
            x_ref, n_ref, w_ref = refs[:3]
            outs = refs[3:]
        else:
            x_ref, n_ref, w_ref, cos_ref, sin_ref = refs[:5]
            outs = refs[5:]
        xv = x_ref[...]
        rstd = lax.rsqrt(jnp.mean(xv * xv, axis=-1, keepdims=True) + NORM_EPS)
        h = (xv * rstd) * n_ref[...]
        proj = _dot(h.astype(BF16), w_ref[...])
        off = 0
        for i, width in enumerate(splits):
            piece = proj[:, off:off + width]
            if rope is not None and i < 2:
                piece = _rope_apply(piece, cos_ref[...], sin_ref[...], 1.0)
            outs[i][...] = piece
            off += width

    in_specs = [_row_spec(rows, D_MODEL), _const_spec((1, D_MODEL)), _const_spec((D_MODEL, n))]
    args = [x, norm.reshape(1, D_MODEL), w]
    if rope is not None:
        in_specs += [_row_spec(rows, 128), _row_spec(rows, 128)]
        args += list(rope)
    return pl.pallas_call(
        body, name=name, grid=(t // rows,), in_specs=in_specs,
        out_specs=[_row_spec(rows, width) for width in splits],
        out_shape=[jax.ShapeDtypeStruct((t, width), F32) for width in splits],
        compiler_params=_params(1),
    )(*args)


def _inproj_bwd(x, norm, w, dpieces, dxn, name):
    t = x.shape[0]
    n = w.shape[1]
    rows = ROWS_BWD
    widths = [p.shape[1] for p in dpieces]
    k = len(dpieces)

    def body(*refs):
        x_ref, n_ref, w_ref, dxn_ref = refs[:4]
        d_refs = refs[4:4 + k]
        dx_ref, dw_ref, dn_ref = refs[4 + k:]
        i = pl.program_id(0)
        xv = x_ref[...]
        rstd = lax.rsqrt(jnp.mean(xv * xv, axis=-1, keepdims=True) + NORM_EPS)
        xhat = xv * rstd
        h = xhat * n_ref[...]
        dproj = [r[...].astype(BF16) for r in d_refs]
        dproj = dproj[0] if k == 1 else jnp.concatenate(dproj, axis=1)
        dh = _dot_nt(dproj, w_ref[...])
        dw = _dot_tn(h.astype(BF16), dproj)
        dn = jnp.sum(dh * xhat, axis=0, keepdims=True)

        @pl.when(i == 0)
        def _():
            dw_ref[...] = dw
            dn_ref[...] = dn

        @pl.when(i > 0)
        def _():
            dw_ref[...] += dw
            dn_ref[...] += dn

        dxhat = dh * n_ref[...]
        dx_ref[...] = rstd * (dxhat - xhat * jnp.mean(dxhat * xhat, axis=-1, keepdims=True)) + dxn_ref[...]

    return pl.pallas_call(
        body, name=name, grid=(t // rows,),
        in_specs=[_row_spec(rows, D_MODEL), _const_spec((1, D_MODEL)), _const_spec((D_MODEL, n)),
                  _row_spec(rows, D_MODEL)] + [_row_spec(rows, width) for width in widths],
        out_specs=[_row_spec(rows, D_MODEL), _const_spec((D_MODEL, n)), _const_spec((1, D_MODEL))],
        out_shape=[jax.ShapeDtypeStruct((t, D_MODEL), F32), jax.ShapeDtypeStruct((D_MODEL, n), F32),
                   jax.ShapeDtypeStruct((1, D_MODEL), F32)],
        compiler_params=_params(1),
    )(x, norm.reshape(1, D_MODEL), w, dxn, *dpieces)


def _s5_matrices(a_re, a_im, log_step, b_re, b_im, c_re, c_im):
    r = S5_BLOCK
    step = jnp.exp(log_step)[:, None]
    lr, li = a_re * step, a_im * step
    k = jnp.arange(r + 1, dtype=F32)
    mag = jnp.exp(lr[:, None, :] * k[:, None])
    pr = mag * jnp.cos(li[:, None, :] * k[:, None])
    pi = mag * jnp.sin(li[:, None, :] * k[:, None])
    nr, ni = pr[:, 1] - 1.0, pi[:, 1]
    den = a_re * a_re + a_im * a_im
    qr, qi = (nr * a_re + ni * a_im) / den, (ni * a_re - nr * a_im) / den
    bbr = qr[..., None] * b_re - qi[..., None] * b_im
    bbi = qr[..., None] * b_im + qi[..., None] * b_re
    wr = c_re[:, None] * pr[:, :, None, :] - c_im[:, None] * pi[:, :, None, :]
    wi = c_re[:, None] * pi[:, :, None, :] + c_im[:, None] * pr[:, :, None, :]
    w = jnp.concatenate([wr, -wi], axis=-1)
    bb = jnp.concatenate([bbr, bbi], axis=1)
    kern = jnp.einsum("gxp,gpi->gxi", w[:, :r].reshape(SSM_GROUPS, S5_LANES, 2 * SSM_STATE), bb,
                      precision=lax.Precision.HIGHEST).reshape(SSM_GROUPS, r, SSM_GROUP, SSM_GROUP)
    cols = [jnp.pad(kern[:, :r - s], ((0, 0), (s, 0), (0, 0), (0, 0))) for s in range(r)]
    tm = jnp.stack(cols, axis=3).reshape(SSM_GROUPS, S5_LANES, S5_LANES)
    cpt = w[:, 1:].reshape(SSM_GROUPS, S5_LANES, 2 * SSM_STATE)
    prs = jnp.swapaxes(pr[:, r - 1::-1][:, :r], 1, 2)[..., None]
    pis = jnp.swapaxes(pi[:, r - 1::-1][:, :r], 1, 2)[..., None]
    bp_re = prs * bbr[:, :, None, :] - pis * bbi[:, :, None, :]
    bp_im = prs * bbi[:, :, None, :] + pis * bbr[:, :, None, :]
    bpt = jnp.concatenate([bp_re, bp_im], axis=1).reshape(SSM_GROUPS, 2 * SSM_STATE, S5_LANES)
    ar = pr[:, r].reshape(1, SSM_GROUPS * SSM_STATE)
    ai = pi[:, r].reshape(1, SSM_GROUPS * SSM_STATE)
    return tm, cpt, bpt, ar, ai


S5_OCTET = 128 // SSM_GROUP
S5_STEPS = SSM_GROUPS // S5_OCTET


def _oct_spec(t):
    return pl.BlockSpec((t, 128), lambda j: (0, j))


def _state_spec(nb):
    return pl.BlockSpec((nb, S5_OCTET * SSM_STATE), lambda j: (0, j))


def _gmat_spec(a, b):
    return pl.BlockSpec((S5_OCTET, a, b), lambda j: (j, 0, 0))


def _block_rows(ref, nb):
    return [ref[pl.ds(r, nb, stride=S5_BLOCK), :] for r in range(S5_BLOCK)]


def _group_cols(pieces_t, g):
    return jnp.concatenate([p[SSM_GROUP * g:SSM_GROUP * (g + 1)] for p in pieces_t], axis=0)


def _state_cols(re_t, im_t, g):
    return jnp.concatenate([re_t[SSM_STATE * g:SSM_STATE * (g + 1)], im_t[SSM_STATE * g:SSM_STATE * (g + 1)]], axis=0)


def _s5_project(a, mat, name):
    t = a.shape[0]
    nb = t // S5_BLOCK

    def body(a_ref, m_ref, re_ref, im_ref):
        at = [p.T for p in _block_rows(a_ref, nb)]
        for pair in range(S5_OCTET // 2):
            xs = [_dot(m_ref[2 * pair + k], _group_cols(at, 2 * pair + k).astype(BF16)) for k in (0, 1)]
            lanes = slice(128 * pair, 128 * (pair + 1))
            re_ref[:, lanes] = jnp.concatenate([xs[0][:SSM_STATE], xs[1][:SSM_STATE]], axis=0).T
            im_ref[:, lanes] = jnp.concatenate([xs[0][SSM_STATE:], xs[1][SSM_STATE:]], axis=0).T

    return pl.pallas_call(
        body, name=name, grid=(S5_STEPS,),
        in_specs=[_oct_spec(t), _gmat_spec(2 * SSM_STATE, S5_LANES)],
        out_specs=[_state_spec(nb), _state_spec(nb)],
        out_shape=[jax.ShapeDtypeStruct((nb, SSM_GROUPS * SSM_STATE), F32)] * 2,
        compiler_params=_params(1),
    )(a, mat)


_SCAN_LANES = 1024


def _s5_scan_fwd(xre, xim, ar, ai, name):
    nb = xre.shape[0]
    col = pl.BlockSpec((nb, _SCAN_LANES), lambda j: (0, j))
    par = pl.BlockSpec((1, _SCAN_LANES), lambda j: (0, j))

    def body(xre_ref, xim_ref, ar_ref, ai_ref, hre_ref, him_ref):
        a_r, a_i = ar_ref[...], ai_ref[...]

        def step(b, carry):
            hr, hi = carry
            hre_ref[pl.ds(b, 1), :] = hr
            him_ref[pl.ds(b, 1), :] = hi
            xr, xi = xre_ref[pl.ds(b, 1), :], xim_ref[pl.ds(b, 1), :]
            return a_r * hr - a_i * hi + xr, a_r * hi + a_i * hr + xi

        zero = jnp.zeros((1, _SCAN_LANES), F32)
        lax.fori_loop(0, nb, step, (zero, zero))

    return pl.pallas_call(
        body, name=name, grid=(xre.shape[1] // _SCAN_LANES,),
        in_specs=[col, col, par, par], out_specs=[col, col],
        out_shape=[jax.ShapeDtypeStruct(xre.shape, F32)] * 2,
        compiler_params=_params(1),
    )(xre, xim, ar, ai)


def _s5_scan_bwd(dhre, dhim, hre, him, ar, ai, name):
    nb = dhre.shape[0]
    col = pl.BlockSpec((nb, _SCAN_LANES), lambda j: (0, j))
    par = pl.BlockSpec((1, _SCAN_LANES), lambda j: (0, j))

    def body(dhre_ref, dhim_ref, hre_ref, him_ref, ar_ref, ai_ref, dxre_ref, dxim_ref, dar_ref, dai_ref):
        a_r, a_i = ar_ref[...], ai_ref[...]

        def step(s, carry):
            gr, gi, dar, dai = carry
            b = nb - 1 - s
            dxre_ref[pl.ds(b, 1), :] = gr
            dxim_ref[pl.ds(b, 1), :] = gi
            hr, hi = hre_ref[pl.ds(b, 1), :], him_ref[pl.ds(b, 1), :]
            dar = dar + (hr * gr + hi * gi)
            dai = dai + (hr * gi - hi * gr)
            dr, di = dhre_ref[pl.ds(b, 1), :], dhim_ref[pl.ds(b, 1), :]
            return dr + (a_r * gr + a_i * gi), di + (a_r * gi - a_i * gr), dar, dai

        zero = jnp.zeros((1, _SCAN_LANES), F32)
        _, _, dar, dai = lax.fori_loop(0, nb, step, (zero, zero, zero, zero))
        dar_ref[...] = dar
        dai_ref[...] = dai

    return pl.pallas_call(
        body, name=name, grid=(dhre.shape[1] // _SCAN_LANES,),
        in_specs=[col, col, col, col, par, par], out_specs=[col, col, par, par],
        out_shape=[jax.ShapeDtypeStruct(dhre.shape, F32)] * 2 + [jax.ShapeDtypeStruct(ar.shape, F32)] * 2,
        compiler_params=_params(1),
    )(dhre, dhim, hre, him, ar, ai)


def _s5_outputs(u, hre, him, tm, cpt, d, name):
    t = u.shape[0]
    nb = t // S5_BLOCK

    def body(u_ref, hre_ref, him_ref, tm_ref, cpt_ref, d_ref, y_ref):
        u_rows = _block_rows(u_ref, nb)
        ut = [p.T for p in u_rows]
        hre_t, him_t = hre_ref[...].T, him_ref[...].T
        yts = []
        for g in range(S5_OCTET):
            yts.append(_dot(tm_ref[g], _group_cols(ut, g).astype(BF16))
                       + _dot(cpt_ref[g], _state_cols(hre_t, him_t, g).astype(BF16)))
        for r in range(S5_BLOCK):
            rows = jnp.concatenate([yt[SSM_GROUP * r:SSM_GROUP * (r + 1)] for yt in yts], axis=0)
            y_ref[pl.ds(r, nb, stride=S5_BLOCK), :] = rows.T + d_ref[...] * u_rows[r]

    return pl.pallas_call(
        body, name=name, grid=(S5_STEPS,),
        in_specs=[_oct_spec(t), _state_spec(nb), _state_spec(nb), _gmat_spec(S5_LANES, S5_LANES),
                  _gmat_spec(S5_LANES, 2 * SSM_STATE), _oct_spec(1)],
        out_specs=_oct_spec(t),
        out_shape=jax.ShapeDtypeStruct(u.shape, F32),
        compiler_params=_params(1),
    )(u, hre, him, tm, cpt, d.reshape(1, D_MODEL))


def _s5_backward(dy, u, hre, him, dxre, dxim, tmt, bp, d, name):
    t = u.shape[0]
    nb = t // S5_BLOCK

    def body(dy_ref, u_ref, hre_ref, him_ref, dxre_ref, dxim_ref, tmt_ref, bp_ref, d_ref,
             du_ref, dtm_ref, dcpt_ref, dbpt_ref, dd_ref):
        dy_rows, u_rows = _block_rows(dy_ref, nb), _block_rows(u_ref, nb)
        dyt, ut = [p.T for p in dy_rows], [p.T for p in u_rows]
        hre_t, him_t = hre_ref[...].T, him_ref[...].T
        dxre_t, dxim_t = dxre_ref[...].T, dxim_ref[...].T
        duts = []
        for g in range(S5_OCTET):
            dyg, ug = _group_cols(dyt, g).astype(BF16), _group_cols(ut, g).astype(BF16)
            hg = _state_cols(hre_t, him_t, g).astype(BF16)
            dxg = _state_cols(dxre_t, dxim_t, g).astype(BF16)
            duts.append(_dot(tmt_ref[g], dyg) + _dot(bp_ref[g], dxg))
            dtm_ref[g] = _dot_nt(dyg, ug)
            dcpt_ref[g] = _dot_nt(dyg, hg)
            dbpt_ref[g] = _dot_nt(dxg, ug)
        dd = jnp.zeros((1, 128), F32)
        for r in range(S5_BLOCK):
            rows = jnp.concatenate([dut[SSM_GROUP * r:SSM_GROUP * (r + 1)] for dut in duts], axis=0)
            du_ref[pl.ds(r, nb, stride=S5_BLOCK), :] = rows.T + d_ref[...] * dy_rows[r]
            dd = dd + jnp.sum(dy_rows[r] * u_rows[r], axis=0, keepdims=True)
        dd_ref[...] = dd

    return pl.pallas_call(
        body, name=name, grid=(S5_STEPS,),
        in_specs=[_oct_spec(t), _oct_spec(t), _state_spec(nb), _state_spec(nb), _state_spec(nb), _state_spec(nb),
                  _gmat_spec(S5_LANES, S5_LANES), _gmat_spec(S5_LANES, 2 * SSM_STATE), _oct_spec(1)],
        out_specs=[_oct_spec(t), _gmat_spec(S5_LANES, S5_LANES), _gmat_spec(S5_LANES, 2 * SSM_STATE),
                   _gmat_spec(2 * SSM_STATE, S5_LANES), _oct_spec(1)],
        out_shape=[jax.ShapeDtypeStruct(u.shape, F32),
                   jax.ShapeDtypeStruct((SSM_GROUPS, S5_LANES, S5_LANES), F32),
                   jax.ShapeDtypeStruct((SSM_GROUPS, S5_LANES, 2 * SSM_STATE), F32),
                   jax.ShapeDtypeStruct((SSM_GROUPS, 2 * SSM_STATE, S5_LANES), F32),
                   jax.ShapeDtypeStruct((1, D_MODEL), F32)],
        compiler_params=_params(1),
    )(dy, u, hre, him, dxre, dxim, tmt, bp, d.reshape(1, D_MODEL))


def _ssm_out_fwd(y, gate, x, w_glu, b_glu, w_out, name):
    t = x.shape[0]
    rows = ROWS_FWD

    def body(y_ref, g_ref, x_ref, wg_ref, bg_ref, wo_ref, o_ref):
        z0 = _gelu(y_ref[...])
        s = _dot(z0.astype(BF16), wg_ref[...]) + bg_ref[...]
        gate_v = g_ref[...]
        a = (z0 * _sigmoid(s)) * (gate_v * _sigmoid(gate_v))
        o_ref[...] = x_ref[...] + _dot(a.astype(BF16), wo_ref[...])

    return pl.pallas_call(
        body, name=name, grid=(t // rows,),
        in_specs=[_row_spec(rows, D_MODEL)] * 3 + [_const_spec((D_MODEL, D_MODEL)), _const_spec((1, D_MODEL)),
                                                   _const_spec((D_MODEL, D_MODEL))],
        out_specs=_row_spec(rows, D_MODEL),
        out_shape=jax.ShapeDtypeStruct((t, D_MODEL), F32),
        compiler_params=_params(1),
    )(y, gate, x, w_glu, b_glu.reshape(1, D_MODEL), w_out)


def _ssm_out_bwd(dxn, y, gate, w_glu, b_glu, w_out, name):
    t = y.shape[0]
    rows = ROWS_BWD

    def body(dxn_ref, y_ref, g_ref, wg_ref, bg_ref, wo_ref, dy_ref, dg_ref, dwg_ref, dbg_ref, dwo_ref):
        i = pl.program_id(0)
        yv = y_ref[...]
        z0 = _gelu(yv)
        z0b = z0.astype(BF16)
        sg = _sigmoid(_dot(z0b, wg_ref[...]) + bg_ref[...])
        z = z0 * sg
        gate_v = g_ref[...]
        sgg = _sigmoid(gate_v)
        silu = gate_v * sgg
        a = z * silu
        dob = dxn_ref[...].astype(BF16)
        da = _dot_nt(dob, wo_ref[...])
        dwo = _dot_tn(a.astype(BF16), dob)
        dz = da * silu
        dg_ref[...] = da * z * (sgg * (1.0 + gate_v * (1.0 - sgg)))
        ds = dz * z0 * (sg * (1.0 - sg))
        dsb = ds.astype(BF16)
        dz0 = dz * sg + _dot_nt(dsb, wg_ref[...])
        dwg = _dot_tn(z0b, dsb)
        dbg = jnp.sum(ds, axis=0, keepdims=True)
        dy_ref[...] = dz0 * _gelu_grad(yv)

        @pl.when(i == 0)
        def _():
            dwo_ref[...] = dwo
            dwg_ref[...] = dwg
            dbg_ref[...] = dbg

        @pl.when(i > 0)
        def _():
            dwo_ref[...] += dwo
            dwg_ref[...] += dwg
            dbg_ref[...] += dbg

    sq = _const_spec((D_MODEL, D_MODEL))
    vec = _const_spec((1, D_MODEL))
    return pl.pallas_call(
        body, name=name, grid=(t // rows,),
        in_specs=[_row_spec(rows, D_MODEL)] * 3 + [sq, vec, sq],
        out_specs=[_row_spec(rows, D_MODEL), _row_spec(rows, D_MODEL), sq, vec, sq],
        out_shape=[jax.ShapeDtypeStruct((t, D_MODEL), F32)] * 2 + [
            jax.ShapeDtypeStruct((D_MODEL, D_MODEL), F32), jax.ShapeDtypeStruct((1, D_MODEL), F32),
            jax.ShapeDtypeStruct((D_MODEL, D_MODEL), F32)],
        compiler_params=_params(1),
    )(dxn, y, gate, w_glu, b_glu.reshape(1, D_MODEL), w_out)


KV_LANES = GQA * ATTN_BLOCK


def _attn_bias(block_is_first):
    kj = lax.broadcasted_iota(jnp.int32, (2 * ATTN_BLOCK, ATTN_BLOCK), 0)
    qi = lax.broadcasted_iota(jnp.int32, (2 * ATTN_BLOCK, ATTN_BLOCK), 1)
    dist = qi + ATTN_BLOCK - kj
    valid = (dist >= 0) & (dist < ATTN_BLOCK) & (jnp.logical_not(block_is_first) | (kj >= ATTN_BLOCK))
    return jnp.tile(jnp.where(valid, 0.0, NEG_INF).astype(F32), (1, GQA))


def _head_cols(a_t, kvh):
    heads = range(kvh * GQA, (kvh + 1) * GQA)
    return jnp.concatenate([a_t[HEAD_DIM * h:HEAD_DIM * (h + 1)] for h in heads], axis=1)


def _head_rows(a_cols):
    stacked = jnp.concatenate([a_cols[:, ATTN_BLOCK * g:ATTN_BLOCK * (g + 1)] for g in range(GQA)], axis=0)
    return stacked.T


def _kv_rows(prev_ref, cur_ref, kvh):
    lanes = slice(HEAD_DIM * kvh, HEAD_DIM * (kvh + 1))
    return jnp.concatenate([prev_ref[:, lanes], cur_ref[:, lanes]], axis=0).astype(BF16)


def _kv_cols(prev_t, cur_t, kvh):
    rows = slice(HEAD_DIM * kvh, HEAD_DIM * (kvh + 1))
    return jnp.concatenate([prev_t[rows], cur_t[rows]], axis=1).astype(BF16)


def _attn_probs(kk, q_cols, sink_row, bias):
    s = _dot(kk, q_cols) * ATTN_SCALE + bias
    m = jnp.maximum(jnp.max(s, axis=0, keepdims=True), sink_row)
    p = jnp.exp(s - m)
    e_sink = jnp.exp(sink_row - m)
    inv = 1.0 / (jnp.sum(p, axis=0, keepdims=True) + e_sink)
    return p * inv, e_sink * inv


def _sink_cols(sinks):
    return jnp.repeat(sinks, ATTN_BLOCK).reshape(N_KV_HEADS, 1, KV_LANES)


def _attn_fwd(q, k, v, sinks, name):
    t = q.shape[0]
    nblk = t // ATTN_BLOCK

    def body(s_ref, q_ref, kc_ref, kp_ref, vc_ref, vp_ref, o_ref):
        bias = _attn_bias(pl.program_id(0) == 0)
        q_t = q_ref[...].T
        vp_t, vc_t = vp_ref[...].T, vc_ref[...].T
        for kvh in range(N_KV_HEADS):
            p, _ = _attn_probs(_kv_rows(kp_ref, kc_ref, kvh), _head_cols(q_t, kvh).astype(BF16), s_ref[kvh], bias)
            o_cols = _dot(_kv_cols(vp_t, vc_t, kvh), p.astype(BF16))
            o_ref[:, GQA * HEAD_DIM * kvh:GQA * HEAD_DIM * (kvh + 1)] = _head_rows(o_cols)

    cur = lambda i: (i, 0)
    prev = lambda i: (jnp.maximum(i - 1, 0), 0)
    return pl.pallas_call(
        body, name=name, grid=(nblk,),
        in_specs=[_const_spec((N_KV_HEADS, 1, KV_LANES)),
                  pl.BlockSpec((ATTN_BLOCK, Q_DIM), cur),
                  pl.BlockSpec((ATTN_BLOCK, KV_DIM), cur), pl.BlockSpec((ATTN_BLOCK, KV_DIM), prev),
                  pl.BlockSpec((ATTN_BLOCK, KV_DIM), cur), pl.BlockSpec((ATTN_BLOCK, KV_DIM), prev)],
        out_specs=pl.BlockSpec((ATTN_BLOCK, Q_DIM), cur),
        out_shape=jax.ShapeDtypeStruct((t, Q_DIM), F32),
        compiler_params=_params(1),
    )(_sink_cols(sinks), q, k, k, v, v)


def _attn_bwd(q, k, v, o, do, sinks, rope, name):
    t = q.shape[0]
    nblk = t // ATTN_BLOCK

    def body(s_ref, q_ref, o_ref, do_ref, kp_ref, kc_ref, vp_ref, vc_ref, cosq_ref, sinq_ref, cosk_ref, sinkey_ref,
             dq_ref, dk_ref, dv_ref, ds_ref, new_k, new_v, wait_k, wait_v):
        n = pl.program_id(0)

        @pl.when(n == 0)
        def _():
            ds_ref[...] = jnp.zeros_like(ds_ref)
            wait_k[...] = jnp.zeros_like(wait_k)
            wait_v[...] = jnp.zeros_like(wait_v)

        @pl.when(n < nblk)
        def _():
            bias = _attn_bias(n == 0)
            q_t, o_t, do_t = q_ref[...].T, o_ref[...].T, do_ref[...].T
            kp_t, kc_t = kp_ref[...].T, kc_ref[...].T
            for kvh in range(N_KV_HEADS):
                q_cols = _head_cols(q_t, kvh).astype(BF16)
                do_cols = _head_cols(do_t, kvh)
                delta = jnp.sum(do_cols * _head_cols(o_t, kvh), axis=0, keepdims=True)
                do_cols = do_cols.astype(BF16)
                p, p_sink = _attn_probs(_kv_rows(kp_ref, kc_ref, kvh), q_cols, s_ref[kvh], bias)
                dp = _dot(_kv_rows(vp_ref, vc_ref, kvh), do_cols)
                ds = (p * (dp - delta) * ATTN_SCALE).astype(BF16)
                lanes = slice(GQA * HEAD_DIM * kvh, GQA * HEAD_DIM * (kvh + 1))
                dq_ref[:, lanes] = _head_rows(_dot(_kv_cols(kp_t, kc_t, kvh), ds))
                head = slice(HEAD_DIM * kvh, HEAD_DIM * (kvh + 1))
                new_k[:, head] = _dot_nt(ds, q_cols)
                new_v[:, head] = _dot_nt(p.astype(BF16), do_cols)
                ds_ref[kvh] += -(p_sink * delta)
            dq_ref[...] = _rope_apply(dq_ref[...], cosq_ref[...], sinq_ref[...], -1.0)

        @pl.when(n == nblk)
        def _():
            new_k[...] = jnp.zeros_like(new_k)
            new_v[...] = jnp.zeros_like(new_v)

        dk_ref[...] = _rope_apply(wait_k[...] + new_k[:ATTN_BLOCK], cosk_ref[...], sinkey_ref[...], -1.0)
        dv_ref[...] = wait_v[...] + new_v[:ATTN_BLOCK]
        wait_k[...] = new_k[ATTN_BLOCK:]
        wait_v[...] = new_v[ATTN_BLOCK:]

    cur = lambda i: (jnp.minimum(i, nblk - 1), 0)
    prev = lambda i: (jnp.maximum(i - 1, 0), 0)
    qs = lambda f: pl.BlockSpec((ATTN_BLOCK, Q_DIM), f)
    ks = lambda f: pl.BlockSpec((ATTN_BLOCK, KV_DIM), f)
    sink_spec = _const_spec((N_KV_HEADS, 1, KV_LANES))
    return pl.pallas_call(
        body, name=name, grid=(nblk + 1,),
        in_specs=[sink_spec, qs(cur), qs(cur), qs(cur), ks(prev), ks(cur), ks(prev), ks(cur),
                  ks(cur), ks(cur), ks(prev), ks(prev)],
        out_specs=[qs(cur), ks(prev), ks(prev), sink_spec],
        out_shape=[jax.ShapeDtypeStruct((t, Q_DIM), F32), jax.ShapeDtypeStruct((t, KV_DIM), F32),
                   jax.ShapeDtypeStruct((t, KV_DIM), F32), jax.ShapeDtypeStruct((N_KV_HEADS, 1, KV_LANES), F32)],
        scratch_shapes=[pltpu.VMEM((2 * ATTN_BLOCK, KV_DIM), F32), pltpu.VMEM((2 * ATTN_BLOCK, KV_DIM), F32),
                        pltpu.VMEM((ATTN_BLOCK, KV_DIM), F32), pltpu.VMEM((ATTN_BLOCK, KV_DIM), F32)],
        compiler_params=_params(1),
    )(_sink_cols(sinks), q, o, do, k, k, v, v, rope[0], rope[1], rope[0], rope[1])


def _attn_out_fwd(o, gate, x, w_out, name):
    t = x.shape[0]
    rows = ROWS_FWD

    def body(o_ref, g_ref, x_ref, wo_ref, xn_ref):
        gate_v = g_ref[...]
        a = o_ref[...] * (gate_v * _sigmoid(gate_v))
        xn_ref[...] = x_ref[...] + _dot(a.astype(BF16), wo_ref[...])

    return pl.pallas_call(
        body, name=name, grid=(t // rows,),
        in_specs=[_row_spec(rows, D_MODEL)] * 3 + [_const_spec((D_MODEL, D_MODEL))],
        out_specs=_row_spec(rows, D_MODEL),
        out_shape=jax.ShapeDtypeStruct((t, D_MODEL), F32),
        compiler_params=_params(1),
    )(o, gate, x, w_out)


def _attn_out_bwd(dxn, o, gate, w_out, name):
    t = o.shape[0]
    rows = ROWS_BWD

    def body(dxn_ref, o_ref, g_ref, wo_ref, do_ref, dg_ref, dwo_ref):
        i = pl.program_id(0)
        gate_v, ov = g_ref[...], o_ref[...]
        sgg = _sigmoid(gate_v)
        silu = gate_v * sgg
        dob = dxn_ref[...].astype(BF16)
        da = _dot_nt(dob, wo_ref[...])
        dwo = _dot_tn((ov * silu).astype(BF16), dob)
        do_ref[...] = da * silu
        dg_ref[...] = da * ov * (sgg * (1.0 + gate_v * (1.0 - sgg)))

        @pl.when(i == 0)
        def _():
            dwo_ref[...] = dwo

        @pl.when(i > 0)
        def _():
            dwo_ref[...] += dwo

    sq = _const_spec((D_MODEL, D_MODEL))
    return pl.pallas_call(
        body, name=name, grid=(t // rows,),
        in_specs=[_row_spec(rows, D_MODEL)] * 3 + [sq],
        out_specs=[_row_spec(rows, D_MODEL), _row_spec(rows, D_MODEL), sq],
        out_shape=[jax.ShapeDtypeStruct((t, D_MODEL), F32)] * 2 + [jax.ShapeDtypeStruct((D_MODEL, D_MODEL), F32)],
        compiler_params=_params(1),
    )(dxn, o, gate, w_out)


def _loss_head(x, norm, target, name):
    t = x.shape[0]
    rows = ROWS_FWD

    def body(x_ref, n_ref, t_ref, loss_ref, dx_ref, dn_ref):
        i = pl.program_id(0)
        xv = x_ref[...]
        rstd = lax.rsqrt(jnp.mean(xv * xv, axis=-1, keepdims=True) + NORM_EPS)
        xhat = xv * rstd
        err = xhat * n_ref[...] - t_ref[...]
        part = 0.5 * jnp.sum(jnp.mean(err * err, axis=-1, keepdims=True), axis=0, keepdims=True)
        dy = err * (1.0 / D_MODEL)
        dn = jnp.sum(dy * xhat, axis=0, keepdims=True)
        dxhat = dy * n_ref[...]
        dx_ref[...] = rstd * (dxhat - xhat * jnp.mean(dxhat * xhat, axis=-1, keepdims=True))

        @pl.when(i == 0)
        def _():
            loss_ref[...] = jnp.zeros((8, 128), F32) + part
            dn_ref[...] = dn

        @pl.when(i > 0)
        def _():
            loss_ref[...] += part
            dn_ref[...] += dn

    return pl.pallas_call(
        body, name=name, grid=(t // rows,),
        in_specs=[_row_spec(rows, D_MODEL), _const_spec((1, D_MODEL)), _row_spec(rows, D_MODEL)],
        out_specs=[_const_spec((8, 128)), _row_spec(rows, D_MODEL), _const_spec((1, D_MODEL))],
        out_shape=[jax.ShapeDtypeStruct((8, 128), F32), jax.ShapeDtypeStruct((t, D_MODEL), F32),
                   jax.ShapeDtypeStruct((1, D_MODEL), F32)],
        compiler_params=_params(1),
    )(x, norm.reshape(1, D_MODEL), target)


N_CHIPS = 4
N_CORES = 2
CHIP_FLIPS = ((0, 1, 0), (1, 0, 0), (1, 1, 0))
CORE_FLIPS = ((0, 0, 1),)


MAX_CHUNKS = {"chip": 8, "core": 16}


def _n_chunks(rows, dtype, most):
    unit = 16 if dtype == BF16 else 8
    return max(n for n in range(1, most + 1) if rows % n == 0 and (rows // n) % unit == 0)


def _exchange(send, among, per_dest, name):
    flips, parties = (CHIP_FLIPS, N_CHIPS) if among == "chip" else (CORE_FLIPS, N_CORES)
    rows, cols = send.shape[-2:]
    chunks = _n_chunks(rows, send.dtype, MAX_CHUNKS[among])
    chunk_rows = rows // chunks

    def body(send_ref, recv_ref, send_sems, recv_sems, local_sems):
        x, y, c = lax.axis_index("x"), lax.axis_index("y"), lax.axis_index("c")

        def number(px, py, pc):
            return 2 * px + py if among == "chip" else pc

        me = number(x, y, c)

        def peer(k):
            fx, fy, fc = flips[k]
            to = (x + fx - 2 * x * fx, y + fy - 2 * y * fy, c + fc - 2 * c * fc)
            return to, number(*to)

        def src(to_number, j):
            part = pl.ds(j * chunk_rows, chunk_rows)
            return send_ref.at[to_number, part] if per_dest else send_ref.at[part]

        def copy(k, j, landing):
            to, to_number = peer(k)
            return pltpu.make_async_remote_copy(
                src_ref=src(to_number, j), dst_ref=recv_ref.at[landing, pl.ds(j * chunk_rows, chunk_rows)],
                send_sem=send_sems.at[k, j], recv_sem=recv_sems.at[k, j],
                device_id=to, device_id_type=pl.DeviceIdType.MESH)

        def own(j):
            return pltpu.make_async_copy(src(me, j), recv_ref.at[me, pl.ds(j * chunk_rows, chunk_rows)], local_sems.at[j])

        for j in range(chunks):
            for k in range(len(flips)):
                copy(k, j, me).start()
            own(j).start()
        for j in range(chunks):
            for k in range(len(flips)):
                copy(k, j, me).wait_send()
                copy(k, j, peer(k)[1]).wait_recv()
            own(j).wait()

    hbm = pl.BlockSpec(memory_space=pltpu.HBM)
    return pl.pallas_call(
        body, name=name, in_specs=[hbm], out_specs=hbm,
        out_shape=jax.ShapeDtypeStruct((parties, rows, cols), send.dtype),
        scratch_shapes=[pltpu.SemaphoreType.DMA((len(flips), chunks)), pltpu.SemaphoreType.DMA((len(flips), chunks)),
                        pltpu.SemaphoreType.DMA((chunks,))],
    )(send)


def _swap_cores(a, name):
    rows, cols = a.shape
    chunks = _n_chunks(rows, a.dtype, MAX_CHUNKS["core"])
    chunk_rows = rows // chunks

    def body(a_ref, got_ref, send_sems, recv_sems):
        sibling = (lax.axis_index("x"), lax.axis_index("y"), 1 - lax.axis_index("c"))

        def copy(j):
            part = pl.ds(j * chunk_rows, chunk_rows)
            return pltpu.make_async_remote_copy(
                src_ref=a_ref.at[part], dst_ref=got_ref.at[part], send_sem=send_sems.at[j], recv_sem=recv_sems.at[j],
                device_id=sibling, device_id_type=pl.DeviceIdType.MESH)

        for j in range(chunks):
            copy(j).start()
        for j in range(chunks):
            copy(j).wait()

    hbm = pl.BlockSpec(memory_space=pltpu.HBM)
    return pl.pallas_call(
        body, name=name, in_specs=[hbm], out_specs=hbm, out_shape=jax.ShapeDtypeStruct(a.shape, a.dtype),
        scratch_shapes=[pltpu.SemaphoreType.DMA((chunks,)), pltpu.SemaphoreType.DMA((chunks,))],
    )(a)


def _by_core(mine, other):
    first = lax.axis_index("c") == 0
    return jnp.stack([jnp.where(first, mine, other), jnp.where(first, other, mine)])


def _all_gather(a, name):
    rows, cols = a.shape
    by_chip = _exchange(a, "chip", False, name + "_chips").reshape(N_CHIPS * rows, cols)
    return _by_core(by_chip, _swap_cores(by_chip, name + "_cores")).reshape(N_CORES, N_CHIPS, rows, cols)


def _sum_parts(parts, name):
    n, cols = parts[0].shape
    rows = min(n, 256)
    while n % rows:
        rows -= 8

    def body(*refs):
        acc = refs[0][...]
        for ref in refs[1:-1]:
            acc = acc + ref[...]
        refs[-1][...] = acc

    return pl.pallas_call(
        body, name=name, grid=(n // rows,),
        in_specs=[_row_spec(rows, cols)] * len(parts),
        out_specs=_row_spec(rows, cols),
        out_shape=jax.ShapeDtypeStruct((n, cols), F32),
        compiler_params=_params(1),
    )(*parts)


def _adamw(parts, w, m, v, name):
    n, cols = w.shape
    k = parts.shape[0]
    rows = min(n, 256)
    while n % rows:
        rows -= 8
    c1 = 1.0 - ADAM_B1 ** ADAM_STEP
    c2 = 1.0 - ADAM_B2 ** ADAM_STEP

    def body(p_ref, w_ref, m_ref, v_ref, g_ref, d_ref, nm_ref, nv_ref):
        g = p_ref[0]
        for s in range(1, k):
            g = g + p_ref[s]
        nm = ADAM_B1 * m_ref[...] + (1.0 - ADAM_B1) * g
        nv = ADAM_B2 * v_ref[...] + (1.0 - ADAM_B2) * (g * g)
        g_ref[...] = g
        nm_ref[...] = nm
        nv_ref[...] = nv
        d_ref[...] = -ADAM_LR * ((nm / c1) / (jnp.sqrt(nv / c2) + ADAM_EPS) + ADAM_WD * w_ref[...])

    blk = _row_spec(rows, cols)
    return pl.pallas_call(
        body, name=name, grid=(n // rows,),
        in_specs=[pl.BlockSpec((k, rows, cols), lambda i: (0, i, 0)), blk, blk, blk],
        out_specs=[blk] * 4,
        out_shape=[jax.ShapeDtypeStruct((n, cols), F32)] * 4,
        compiler_params=_params(1),
    )(parts, w, m, v)


SSM_KEYS = ("norm", "w_in", "a_re", "a_im", "log_step", "b_re", "b_im", "c_re", "c_im", "d", "w_glu", "b_glu", "w_out")
ATTN_KEYS = ("norm", "w_in", "sinks", "w_out")
LAYER_KEYS = (SSM_KEYS, ATTN_KEYS, SSM_KEYS, ATTN_KEYS)
BIG_KEYS = ("w_in", "w_glu", "w_out")
ATTN_SPLITS = (Q_DIM, KV_DIM, KV_DIM, D_MODEL)


def _rope_tables(t):
    pos = jnp.arange(t, dtype=F32)
    inv_freq = ROPE_THETA ** (-jnp.arange(0, HEAD_DIM, 2, dtype=F32) / HEAD_DIM)
    ang = pos[:, None] * inv_freq[None, :]
    cos, sin = jnp.cos(ang), jnp.sin(ang)
    return jnp.tile(jnp.concatenate([cos, cos], axis=1), (1, 2)), jnp.tile(jnp.concatenate([-sin, sin], axis=1), (1, 2))


def _ssm_layer_fwd(i, x, p, w):
    tag = "l%d_" % i
    mats, mats_vjp = jax.vjp(_s5_matrices, p["a_re"], p["a_im"], p["log_step"], p["b_re"], p["b_im"], p["c_re"], p["c_im"])
    tm, cpt, bpt, ar, ai = mats
    mb = dict(tm=tm.astype(BF16), tmt=jnp.swapaxes(tm, 1, 2).astype(BF16), cpt=cpt.astype(BF16),
              cp=jnp.swapaxes(cpt, 1, 2).astype(BF16), bpt=bpt.astype(BF16), bp=jnp.swapaxes(bpt, 1, 2).astype(BF16))
    u, gate = _inproj_fwd(x, p["norm"], w["w_in"], (D_MODEL, D_MODEL), None, tag + "inproj_fwd")
    xre, xim = _s5_project(u, mb["bpt"], tag + "s5_block_inputs")
    hre, him = _s5_scan_fwd(xre, xim, ar, ai, tag + "s5_scan_fwd")
    y = _s5_outputs(u, hre, him, mb["tm"], mb["cpt"], p["d"], tag + "s5_outputs")
    xn = _ssm_out_fwd(y, gate, x, w["w_glu"], p["b_glu"], w["w_out"], tag + "out_fwd")
    return xn, (x, u, gate, y, hre, him, mb, ar, ai, mats_vjp)


def _ssm_layer_bwd(i, dxn, saved, p, w):
    tag = "l%d_" % i
    x, u, gate, y, hre, him, mb, ar, ai, mats_vjp = saved
    dy, dgate, dw_glu, db_glu, dw_out = _ssm_out_bwd(dxn, y, gate, w["w_glu"], p["b_glu"], w["w_out"], tag + "out_bwd")
    dhre, dhim = _s5_project(dy, mb["cp"], tag + "s5_state_grads")
    dxre, dxim, dar, dai = _s5_scan_bwd(dhre, dhim, hre, him, ar, ai, tag + "s5_scan_bwd")
    du, dtm, dcpt, dbpt, dd = _s5_backward(dy, u, hre, him, dxre, dxim, mb["tmt"], mb["bp"], p["d"], tag + "s5_backward")
    da_re, da_im, dlog_step, db_re, db_im, dc_re, dc_im = mats_vjp((dtm, dcpt, dbpt, dar, dai))
    dx, dw_in, dnorm = _inproj_bwd(x, p["norm"], w["w_in"], [du, dgate], dxn, tag + "inproj_bwd")
    grads = dict(norm=dnorm.reshape(D_MODEL), w_in=dw_in, a_re=da_re, a_im=da_im, log_step=dlog_step, b_re=db_re,
                 b_im=db_im, c_re=dc_re, c_im=dc_im, d=dd.reshape(D_MODEL), w_glu=dw_glu, b_glu=db_glu.reshape(D_MODEL),
                 w_out=dw_out)
    return dx, grads


def _attn_layer_fwd(i, x, p, w, rope):
    tag = "l%d_" % i
    q, k, v, gate = _inproj_fwd(x, p["norm"], w["w_in"], ATTN_SPLITS, rope, tag + "inproj_fwd")
    o = _attn_fwd(q, k, v, p["sinks"], tag + "attn_fwd")
    xn = _attn_out_fwd(o, gate, x, w["w_out"], tag + "out_fwd")
    return xn, (x, q, k, v, gate, o)


def _attn_layer_bwd(i, dxn, saved, p, w, rope):
    tag = "l%d_" % i
    x, q, k, v, gate, o = saved
    do, dgate, dw_out = _attn_out_bwd(dxn, o, gate, w["w_out"], tag + "out_bwd")
    dq, dk, dv, dsinks = _attn_bwd(q, k, v, o, do, p["sinks"], rope, tag + "attn_bwd")
    dx, dw_in, dnorm = _inproj_bwd(x, p["norm"], w["w_in"], [dq, dk, dv, dgate], dxn, tag + "inproj_bwd")
    return dx, dict(norm=dnorm.reshape(D_MODEL), w_in=dw_in, sinks=dsinks.reshape(N_Q_HEADS, ATTN_BLOCK).sum(axis=1), w_out=dw_out)


def _local_step(x, target, small, big):
    rope = _rope_tables(x.shape[0])
    saved = []
    for i in range(4):
        if i % 2 == 0:
            x, s = _ssm_layer_fwd(i, x, small[i], big[i])
        else:
            x, s = _attn_layer_fwd(i, x, small[i], big[i], rope)
        saved.append(s)
    loss, dx, dfinal = _loss_head(x, small[4]["norm"], target, "loss_head")
    grads = [None] * 4 + [dict(norm=dfinal.reshape(D_MODEL))]
    for i in (3, 2, 1, 0):
        if i % 2 == 0:
            dx, grads[i] = _ssm_layer_bwd(i, dx, saved[i], small[i], big[i])
        else:
            dx, grads[i] = _attn_layer_bwd(i, dx, saved[i], small[i], big[i], rope)
    return loss[0, 0], dx, grads


def _shard_rows(key, a):
    return a.reshape(-1, D_MODEL)


def _owner_major(key, g):
    if key == "w_in":
        g = g.reshape(D_MODEL, N_CHIPS, N_CORES, -1).transpose(2, 1, 0, 3)
    else:
        g = g.reshape(N_CHIPS, N_CORES, -1, D_MODEL).transpose(1, 0, 2, 3)
    return g.reshape(N_CORES, N_CHIPS, -1, D_MODEL)


def _from_gathered(key, a, shard_shape):
    if key == "w_in":
        cols = shard_shape[1]
        return a.reshape(N_CORES, N_CHIPS, D_MODEL, cols).transpose(2, 1, 0, 3).reshape(D_MODEL, N_DEV * cols)
    return a.transpose(1, 0, 2, 3).reshape(N_DEV * shard_shape[0], shard_shape[1])


SMALL_ROWS = 144


def _pad_rows(flat):
    n = flat.shape[0]
    rows = N_DEV * SMALL_ROWS
    assert n <= rows * D_MODEL
    return jnp.pad(flat, (0, rows * D_MODEL - n)).reshape(rows, D_MODEL)


def kernel(*args):
    names = ["x"]
    layer_names = []
    for i, keys in enumerate(LAYER_KEYS):
        layer_names += ["l%d_%s" % (i, k) for k in keys]
    layer_names.append("final_norm")
    names += layer_names + ["loss_target"] + ["m_" + n for n in layer_names] + ["v_" + n for n in layer_names]
    given = dict(zip(names, args))
    big_names = [n for n in layer_names if n.split("_", 1)[1] in BIG_KEYS]
    small_names = [n for n in layer_names if n not in big_names]

    offsets, rows_at = {}, 0
    for n in big_names:
        offsets[n] = rows_at
        rows_at += given[n].size // D_MODEL
    local_rows = jnp.concatenate([_shard_rows(n, given[n]) for n in big_names], axis=0)
    big_rows = rows_at
    gathered = _all_gather(local_rows.astype(BF16), "gather_weights")
    big = [dict() for _ in range(4)]
    for n in big_names:
        layer, key = int(n[1]), n.split("_", 1)[1]
        rows = given[n].size // D_MODEL
        big[layer][key] = _from_gathered(key, gathered[:, :, offsets[n]:offsets[n] + rows, :], given[n].shape)
    small = [dict() for _ in range(5)]
    for n in small_names:
        if n == "final_norm":
            small[4]["norm"] = given[n]
        else:
            small[int(n[1])][n.split("_", 1)[1]] = given[n]

    loss, dx, grads = _local_step(given["x"][0], given["loss_target"][0], small, big)
    loss = lax.psum(loss, ("x", "y", "c"))

    def grad_of(n):
        return grads[4]["norm"] if n == "final_norm" else grads[int(n[1])][n.split("_", 1)[1]]

    flat = lambda f: _pad_rows(jnp.concatenate([f(n).reshape(-1) for n in small_names]))
    small_grads = flat(grad_of)
    small_rows = small_grads.shape[0] // N_DEV
    send = jnp.concatenate([_owner_major(n.split("_", 1)[1], grad_of(n)) for n in big_names]
                           + [small_grads.reshape(N_CORES, N_CHIPS, small_rows, D_MODEL)], axis=2)
    owner_rows = big_rows + small_rows
    send = send.reshape(N_CORES, N_CHIPS * owner_rows, D_MODEL)
    first = lax.axis_index("c") == 0
    keep, give = jnp.where(first, send[0], send[1]), jnp.where(first, send[1], send[0])
    chip_sums = _sum_parts([keep, _swap_cores(give, "scatter_grads_cores")], "sum_core_grads")
    chip_sums = chip_sums.reshape(N_CHIPS, owner_rows, D_MODEL)
    parts = _exchange(chip_sums, "chip", True, "scatter_grads_chips")
    cat = lambda pre: jnp.concatenate([_shard_rows(n, given[pre + n]) for n in big_names], axis=0)
    g_big, d_big, m_big, v_big = _adamw(parts, cat(""), cat("m_"), cat("v_"), "adamw_matrices")

    my_slice = _sum_parts([parts[s, big_rows:] for s in range(N_CHIPS)], "sum_small_grads")
    g_all = _all_gather(my_slice, "gather_small_grads").reshape(1, N_DEV * small_rows, D_MODEL)
    g_small, d_small, m_small, v_small = _adamw(
        g_all, flat(lambda n: given[n]), flat(lambda n: given["m_" + n]), flat(lambda n: given["v_" + n]), "adamw_small")

    outs = {}
    for tag, a_big, a_small in (("grad_", g_big, g_small), ("delta_", d_big, d_small),
                                ("new_m_", m_big, m_small), ("new_v_", v_big, v_small)):
        for n in big_names:
            rows = given[n].size // D_MODEL
            outs[tag + n] = a_big[offsets[n]:offsets[n] + rows].reshape(given[n].shape)
        a_flat, at = a_small.reshape(-1), 0
        for n in small_names:
            outs[tag + n] = a_flat[at:at + given[n].size].reshape(given[n].shape)
            at += given[n].size
    result = [loss, dx[None]]
    for tag in ("grad_", "delta_", "new_m_", "new_v_"):
        result += [outs[tag + n] for n in layer_names]
    return tuple(result)
```

```python
import functools
import math

import jax
import jax.numpy as jnp
from jax import lax
from jax.experimental import pallas as pl
from jax.experimental.pallas import tpu as pltpu

F32 = jnp.float32
BF16 = jnp.bfloat16

D_MODEL = 1024
SSM_GROUP = 16
SSM_GROUPS = D_MODEL // SSM_GROUP
SSM_STATE = 64
S5_BLOCK = 16
S5_LANES = S5_BLOCK * SSM_GROUP
HEAD_DIM = 64
N_Q_HEADS = 16
N_KV_HEADS = 2
GQA = N_Q_HEADS // N_KV_HEADS
Q_DIM = N_Q_HEADS * HEAD_DIM
KV_DIM = N_KV_HEADS * HEAD_DIM
ATTN_BLOCK = 128
ROPE_THETA = 10000.0
NORM_EPS = 1e-5
NEG_INF = -1e30
ATTN_SCALE = HEAD_DIM ** -0.5
N_DEV = 8

ADAM_LR = 0.001
ADAM_B1 = 0.9
ADAM_B2 = 0.999
ADAM_EPS = 1e-08
ADAM_WD = 0.01
ADAM_STEP = 10

VMEM_LIMIT = 56 * 1024 * 1024
ROWS_FWD = 512
ROWS_BWD = 512

NT = (((1,), (1,)), ((), ()))
TN = (((0,), (0,)), ((), ()))


def _params(n_grid):
    return pltpu.CompilerParams(dimension_semantics=("arbitrary",) * n_grid, vmem_limit_bytes=VMEM_LIMIT)


def _dot(a, b):
    return jnp.dot(a, b, preferred_element_type=F32)


def _dot_nt(a, b):
    return lax.dot_general(a, b, NT, preferred_element_type=F32)


def _dot_tn(a, b):
    return lax.dot_general(a, b, TN, preferred_element_type=F32)


def _sigmoid(x):
    return 1.0 / (1.0 + jnp.exp(-x))


_GELU_K = math.sqrt(2.0 / math.pi)


def _gelu(x):
    return x * (0.5 * (1.0 + jnp.tanh(_GELU_K * (x + 0.044715 * (x * x * x)))))


def _gelu_grad(x):
    t = jnp.tanh(_GELU_K * (x + 0.044715 * (x * x * x)))
    return 0.5 * (1.0 + t) + 0.5 * x * (1.0 - t * t) * (_GELU_K * (1.0 + 3.0 * 0.044715 * (x * x)))


def _row_spec(rows, cols):
    return pl.BlockSpec((rows, cols), lambda i: (i, 0))


def _const_spec(shape):
    zeros = (0,) * len(shape)
    return pl.BlockSpec(shape, lambda i: zeros, pipeline_mode=pl.Buffered(1))


def _rope_apply(t, cos, sin_signed, sign):
    lane = lax.broadcasted_iota(jnp.int32, (1, 128), 1)
    first_half = (lane % HEAD_DIM) < (HEAD_DIM // 2)
    out = []
    for j in range(t.shape[1] // 128):
        tj = t[:, 128 * j:128 * (j + 1)]
        partner = jnp.where(first_half, pltpu.roll(tj, 128 - HEAD_DIM // 2, 1), pltpu.roll(tj, HEAD_DIM // 2, 1))
        out.append(tj * cos + sign * (partner * sin_signed))
    return out[0] if len(out) == 1 else jnp.concatenate(out, axis=1)


def _inproj_fwd(x, norm, w, splits, rope, name):
    t = x.shape[0]
    n = w.shape[1]
    rows = ROWS_FWD

    def body(*refs):
        if rope is None:
            x_ref, n_ref, w_ref = refs[:3]
            outs = refs[3:]
        else:
            x_ref, n_ref, w_ref, cos_ref, sin_ref = refs[:5]
            outs = refs[5:]
        xv = x_ref[...]
        rstd = lax.rsqrt(jnp.mean(xv * xv, axis=-1, keepdims=True) + NORM_EPS)
        h = (xv * rstd) * n_ref[...]
        proj = _dot(h.astype(BF16), w_ref[...])
        off = 0
        for i, width in enumerate(splits):
            piece = proj[:, off:off + width]
            if rope is not None and i < 2:
                piece = _rope_apply(piece, cos_ref[...], sin_ref[...], 1.0)
            outs[i][...] = piece
            off += width

    in_specs = [_row_spec(rows, D_MODEL), _const_spec((1, D_MODEL)), _const_spec((D_MODEL, n))]
    args = [x, norm.reshape(1, D_MODEL), w]
    if rope is not None:
        in_specs += [_row_spec(rows, 128), _row_spec(rows, 128)]
        args += list(rope)
    return pl.pallas_call(
        body, name=name, grid=(t // rows,), in_specs=in_specs,
        out_specs=[_row_spec(rows, width) for width in splits],
        out_shape=[jax.ShapeDtypeStruct((t, width), F32) for width in splits],
        compiler_params=_params(1),
    )(*args)


def _inproj_bwd(x, norm, w, dpieces, dxn, name):
    t = x.shape[0]
    n = w.shape[1]
    rows = ROWS_BWD
    widths = [p.shape[1] for p in dpieces]
    k = len(dpieces)

    def body(*refs):
        x_ref, n_ref, w_ref, dxn_ref = refs[:4]
        d_refs = refs[4:4 + k]
        dx_ref, dw_ref, dn_ref = refs[4 + k:]
        @pl.when(pl.program_id(0) == 0)
        def _():
            dw_ref[...] = jnp.zeros_like(dw_ref)
            dn_ref[...] = jnp.zeros_like(dn_ref)

        xv = x_ref[...]
        rstd = lax.rsqrt(jnp.mean(xv * xv, axis=-1, keepdims=True) + NORM_EPS)
        xhat = xv * rstd
        h = xhat * n_ref[...]
        dproj = [r[...].astype(BF16) for r in d_refs]
        dproj = dproj[0] if k == 1 else jnp.concatenate(dproj, axis=1)
        dh = _dot_nt(dproj, w_ref[...])
        dw_ref[...] += _dot_tn(h.astype(BF16), dproj)
        dn_ref[...] += jnp.sum(dh * xhat, axis=0, keepdims=True)
        dxhat = dh * n_ref[...]
        dx_ref[...] = rstd * (dxhat - xhat * jnp.mean(dxhat * xhat, axis=-1, keepdims=True)) + dxn_ref[...]

    return pl.pallas_call(
        body, name=name, grid=(t // rows,),
        in_specs=[_row_spec(rows, D_MODEL), _const_spec((1, D_MODEL)), _const_spec((D_MODEL, n)),
                  _row_spec(rows, D_MODEL)] + [_row_spec(rows, width) for width in widths],
        out_specs=[_row_spec(rows, D_MODEL), _const_spec((D_MODEL, n)), _const_spec((1, D_MODEL))],
        out_shape=[jax.ShapeDtypeStruct((t, D_MODEL), F32), jax.ShapeDtypeStruct((D_MODEL, n), F32),
                   jax.ShapeDtypeStruct((1, D_MODEL), F32)],
        compiler_params=_params(1),
    )(x, norm.reshape(1, D_MODEL), w, dxn, *dpieces)


def _s5_matrices(a_re, a_im, log_step, b_re, b_im, c_re, c_im):
    r = S5_BLOCK
    step = jnp.exp(log_step)[:, None]
    lr, li = a_re * step, a_im * step
    k = jnp.arange(r + 1, dtype=F32)
    mag = jnp.exp(lr[:, None, :] * k[:, None])
    pr = mag * jnp.cos(li[:, None, :] * k[:, None])
    pi = mag * jnp.sin(li[:, None, :] * k[:, None])
    nr, ni = pr[:, 1] - 1.0, pi[:, 1]
    den = a_re * a_re + a_im * a_im
    qr, qi = (nr * a_re + ni * a_im) / den, (ni * a_re - nr * a_im) / den
    bbr = qr[..., None] * b_re - qi[..., None] * b_im
    bbi = qr[..., None] * b_im + qi[..., None] * b_re
    wr = c_re[:, None] * pr[:, :, None, :] - c_im[:, None] * pi[:, :, None, :]
    wi = c_re[:, None] * pi[:, :, None, :] + c_im[:, None] * pr[:, :, None, :]
    w = jnp.concatenate([wr, -wi], axis=-1)
    bb = jnp.concatenate([bbr, bbi], axis=1)
    kern = jnp.einsum("gxp,gpi->gxi", w[:, :r].reshape(SSM_GROUPS, S5_LANES, 2 * SSM_STATE), bb,
                      precision=lax.Precision.HIGHEST).reshape(SSM_GROUPS, r, SSM_GROUP, SSM_GROUP)
    cols = [jnp.pad(kern[:, :r - s], ((0, 0), (s, 0), (0, 0), (0, 0))) for s in range(r)]
    tm = jnp.stack(cols, axis=3).reshape(SSM_GROUPS, S5_LANES, S5_LANES)
    cpt = w[:, 1:].reshape(SSM_GROUPS, S5_LANES, 2 * SSM_STATE)
    prs = jnp.swapaxes(pr[:, r - 1::-1][:, :r], 1, 2)[..., None]
    pis = jnp.swapaxes(pi[:, r - 1::-1][:, :r], 1, 2)[..., None]
    bp_re = prs * bbr[:, :, None, :] - pis * bbi[:, :, None, :]
    bp_im = prs * bbi[:, :, None, :] + pis * bbr[:, :, None, :]
    bpt = jnp.concatenate([bp_re, bp_im], axis=1).reshape(SSM_GROUPS, 2 * SSM_STATE, S5_LANES)
    ar = pr[:, r].reshape(1, SSM_GROUPS * SSM_STATE)
    ai = pi[:, r].reshape(1, SSM_GROUPS * SSM_STATE)
    return tm, cpt, bpt, ar, ai


S5_OCTET = 128 // SSM_GROUP
S5_STEPS = SSM_GROUPS // S5_OCTET


def _oct_spec(t):
    return pl.BlockSpec((t, 128), lambda j: (0, j))


def _state_spec(nb):
    return pl.BlockSpec((nb, S5_OCTET * SSM_STATE), lambda j: (0, j))


def _gmat_spec(a, b):
    return pl.BlockSpec((S5_OCTET, a, b), lambda j: (j, 0, 0))


def _block_rows(ref, nb):
    return [ref[pl.ds(r, nb, stride=S5_BLOCK), :] for r in range(S5_BLOCK)]


def _group_cols(pieces_t, g):
    return jnp.concatenate([p[SSM_GROUP * g:SSM_GROUP * (g + 1)] for p in pieces_t], axis=0)


def _state_cols(re_t, im_t, g):
    return jnp.concatenate([re_t[SSM_STATE * g:SSM_STATE * (g + 1)], im_t[SSM_STATE * g:SSM_STATE * (g + 1)]], axis=0)


def _s5_project(a, mat, name):
    t = a.shape[0]
    nb = t // S5_BLOCK

    def body(a_ref, m_ref, re_ref, im_ref):
        at = [p.T for p in _block_rows(a_ref, nb)]
        for pair in range(S5_OCTET // 2):
            xs = [_dot(m_ref[2 * pair + k], _group_cols(at, 2 * pair + k).astype(BF16)) for k in (0, 1)]
            lanes = slice(128 * pair, 128 * (pair + 1))
            re_ref[:, lanes] = jnp.concatenate([xs[0][:SSM_STATE], xs[1][:SSM_STATE]], axis=0).T
            im_ref[:, lanes] = jnp.concatenate([xs[0][SSM_STATE:], xs[1][SSM_STATE:]], axis=0).T

    return pl.pallas_call(
        body, name=name, grid=(S5_STEPS,),
        in_specs=[_oct_spec(t), _gmat_spec(2 * SSM_STATE, S5_LANES)],
        out_specs=[_state_spec(nb), _state_spec(nb)],
        out_shape=[jax.ShapeDtypeStruct((nb, SSM_GROUPS * SSM_STATE), F32)] * 2,
        compiler_params=_params(1),
    )(a, mat)


_SCAN_LANES = 1024


def _s5_scan_fwd(xre, xim, ar, ai, name):
    nb = xre.shape[0]
    col = pl.BlockSpec((nb, _SCAN_LANES), lambda j: (0, j))
    par = pl.BlockSpec((1, _SCAN_LANES), lambda j: (0, j))

    def body(xre_ref, xim_ref, ar_ref, ai_ref, hre_ref, him_ref):
        a_r, a_i = ar_ref[...], ai_ref[...]

        def step(b, carry):
            hr, hi = carry
            hre_ref[pl.ds(b, 1), :] = hr
            him_ref[pl.ds(b, 1), :] = hi
            xr, xi = xre_ref[pl.ds(b, 1), :], xim_ref[pl.ds(b, 1), :]
            return a_r * hr - a_i * hi + xr, a_r * hi + a_i * hr + xi

        zero = jnp.zeros((1, _SCAN_LANES), F32)
        lax.fori_loop(0, nb, step, (zero, zero))

    return pl.pallas_call(
        body, name=name, grid=(xre.shape[1] // _SCAN_LANES,),
        in_specs=[col, col, par, par], out_specs=[col, col],
        out_shape=[jax.ShapeDtypeStruct(xre.shape, F32)] * 2,
        compiler_params=_params(1),
    )(xre, xim, ar, ai)


def _s5_scan_bwd(dhre, dhim, hre, him, ar, ai, name):
    nb = dhre.shape[0]
    col = pl.BlockSpec((nb, _SCAN_LANES), lambda j: (0, j))
    par = pl.BlockSpec((1, _SCAN_LANES), lambda j: (0, j))

    def body(dhre_ref, dhim_ref, hre_ref, him_ref, ar_ref, ai_ref, dxre_ref, dxim_ref, dar_ref, dai_ref):
        a_r, a_i = ar_ref[...], ai_ref[...]

        def step(s, carry):
            gr, gi, dar, dai = carry
            b = nb - 1 - s
            dxre_ref[pl.ds(b, 1), :] = gr
            dxim_ref[pl.ds(b, 1), :] = gi
            hr, hi = hre_ref[pl.ds(b, 1), :], him_ref[pl.ds(b, 1), :]
            dar = dar + (hr * gr + hi * gi)
            dai = dai + (hr * gi - hi * gr)
            dr, di = dhre_ref[pl.ds(b, 1), :], dhim_ref[pl.ds(b, 1), :]
            return dr + (a_r * gr + a_i * gi), di + (a_r * gi - a_i * gr), dar, dai

        zero = jnp.zeros((1, _SCAN_LANES), F32)
        _, _, dar, dai = lax.fori_loop(0, nb, step, (zero, zero, zero, zero))
        dar_ref[...] = dar
        dai_ref[...] = dai

    return pl.pallas_call(
        body, name=name, grid=(dhre.shape[1] // _SCAN_LANES,),
        in_specs=[col, col, col, col, par, par], out_specs=[col, col, par, par],
        out_shape=[jax.ShapeDtypeStruct(dhre.shape, F32)] * 2 + [jax.ShapeDtypeStruct(ar.shape, F32)] * 2,
        compiler_params=_params(1),
    )(dhre, dhim, hre, him, ar, ai)


def _s5_outputs(u, hre, him, tm, cpt, d, name):
    t = u.shape[0]
    nb = t // S5_BLOCK

    def body(u_ref, hre_ref, him_ref, tm_ref, cpt_ref, d_ref, y_ref):
        u_rows = _block_rows(u_ref, nb)
        ut = [p.T for p in u_rows]
        hre_t, him_t = hre_ref[...].T, him_ref[...].T
        yts = []
        for g in range(S5_OCTET):
            yts.append(_dot(tm_ref[g], _group_cols(ut, g).astype(BF16))
                       + _dot(cpt_ref[g], _state_cols(hre_t, him_t, g).astype(BF16)))
        for r in range(S5_BLOCK):
            rows = jnp.concatenate([yt[SSM_GROUP * r:SSM_GROUP * (r + 1)] for yt in yts], axis=0)
            y_ref[pl.ds(r, nb, stride=S5_BLOCK), :] = rows.T + d_ref[...] * u_rows[r]

    return pl.pallas_call(
        body, name=name, grid=(S5_STEPS,),
        in_specs=[_oct_spec(t), _state_spec(nb), _state_spec(nb), _gmat_spec(S5_LANES, S5_LANES),
                  _gmat_spec(S5_LANES, 2 * SSM_STATE), _oct_spec(1)],
        out_specs=_oct_spec(t),
        out_shape=jax.ShapeDtypeStruct(u.shape, F32),
        compiler_params=_params(1),
    )(u, hre, him, tm, cpt, d.reshape(1, D_MODEL))


def _s5_backward(dy, u, hre, him, dxre, dxim, tmt, bp, d, name):
    t = u.shape[0]
    nb = t // S5_BLOCK

    def body(dy_ref, u_ref, hre_ref, him_ref, dxre_ref, dxim_ref, tmt_ref, bp_ref, d_ref,
             du_ref, dtm_ref, dcpt_ref, dbpt_ref, dd_ref):
        dy_rows, u_rows = _block_rows(dy_ref, nb), _block_rows(u_ref, nb)
        dyt, ut = [p.T for p in dy_rows], [p.T for p in u_rows]
        hre_t, him_t = hre_ref[...].T, him_ref[...].T
        dxre_t, dxim_t = dxre_ref[...].T, dxim_ref[...].T
        duts = []
        for g in range(S5_OCTET):
            dyg, ug = _group_cols(dyt, g).astype(BF16), _group_cols(ut, g).astype(BF16)
            hg = _state_cols(hre_t, him_t, g).astype(BF16)
            dxg = _state_cols(dxre_t, dxim_t, g).astype(BF16)
            duts.append(_dot(tmt_ref[g], dyg) + _dot(bp_ref[g], dxg))
            dtm_ref[g] = _dot_nt(dyg, ug)
            dcpt_ref[g] = _dot_nt(dyg, hg)
            dbpt_ref[g] = _dot_nt(dxg, ug)
        dd = jnp.zeros((1, 128), F32)
        for r in range(S5_BLOCK):
            rows = jnp.concatenate([dut[SSM_GROUP * r:SSM_GROUP * (r + 1)] for dut in duts], axis=0)
            du_ref[pl.ds(r, nb, stride=S5_BLOCK), :] = rows.T + d_ref[...] * dy_rows[r]
            dd = dd + jnp.sum(dy_rows[r] * u_rows[r], axis=0, keepdims=True)
        dd_ref[...] = dd

    return pl.pallas_call(
        body, name=name, grid=(S5_STEPS,),
        in_specs=[_oct_spec(t), _oct_spec(t), _state_spec(nb), _state_spec(nb), _state_spec(nb), _state_spec(nb),
                  _gmat_spec(S5_LANES, S5_LANES), _gmat_spec(S5_LANES, 2 * SSM_STATE), _oct_spec(1)],
        out_specs=[_oct_spec(t), _gmat_spec(S5_LANES, S5_LANES), _gmat_spec(S5_LANES, 2 * SSM_STATE),
                   _gmat_spec(2 * SSM_STATE, S5_LANES), _oct_spec(1)],
        out_shape=[jax.ShapeDtypeStruct(u.shape, F32),
                   jax.ShapeDtypeStruct((SSM_GROUPS, S5_LANES, S5_LANES), F32),
                   jax.ShapeDtypeStruct((SSM_GROUPS, S5_LANES, 2 * SSM_STATE), F32),
                   jax.ShapeDtypeStruct((SSM_GROUPS, 2 * SSM_STATE, S5_LANES), F32),
                   jax.ShapeDtypeStruct((1, D_MODEL), F32)],
        compiler_params=_params(1),
    )(dy, u, hre, him, dxre, dxim, tmt, bp, d.reshape(1, D_MODEL))


def _ssm_out_fwd(y, gate, x, w_glu, b_glu, w_out, name):
    t = x.shape[0]
    rows = ROWS_FWD

    def body(y_ref, g_ref, x_ref, wg_ref, bg_ref, wo_ref, o_ref):
        z0 = _gelu(y_ref[...])
        s = _dot(z0.astype(BF16), wg_ref[...]) + bg_ref[...]
        gate_v = g_ref[...]
        a = (z0 * _sigmoid(s)) * (gate_v * _sigmoid(gate_v))
        o_ref[...] = x_ref[...] + _dot(a.astype(BF16), wo_ref[...])

    return pl.pallas_call(
        body, name=name, grid=(t // rows,),
        in_specs=[_row_spec(rows, D_MODEL)] * 3 + [_const_spec((D_MODEL, D_MODEL)), _const_spec((1, D_MODEL)),
                                                   _const_spec((D_MODEL, D_MODEL))],
        out_specs=_row_spec(rows, D_MODEL),
        out_shape=jax.ShapeDtypeStruct((t, D_MODEL), F32),
        compiler_params=_params(1),
    )(y, gate, x, w_glu, b_glu.reshape(1, D_MODEL), w_out)


def _ssm_out_bwd(dxn, y, gate, w_glu, b_glu, w_out, name):
    t = y.shape[0]
    rows = ROWS_BWD

    def body(dxn_ref, y_ref, g_ref, wg_ref, bg_ref, wo_ref, dy_ref, dg_ref, dwg_ref, dbg_ref, dwo_ref):
        @pl.when(pl.program_id(0) == 0)
        def _():
            dwo_ref[...] = jnp.zeros_like(dwo_ref)
            dwg_ref[...] = jnp.zeros_like(dwg_ref)
            dbg_ref[...] = jnp.zeros_like(dbg_ref)

        yv = y_ref[...]
        z0 = _gelu(yv)
        z0b = z0.astype(BF16)
        sg = _sigmoid(_dot(z0b, wg_ref[...]) + bg_ref[...])
        z = z0 * sg
        gate_v = g_ref[...]
        sgg = _sigmoid(gate_v)
        silu = gate_v * sgg
        dob = dxn_ref[...].astype(BF16)
        da = _dot_nt(dob, wo_ref[...])
        dwo_ref[...] += _dot_tn((z * silu).astype(BF16), dob)
        dz = da * silu
        dg_ref[...] = da * z * (sgg * (1.0 + gate_v * (1.0 - sgg)))
        ds = dz * z0 * (sg * (1.0 - sg))
        dsb = ds.astype(BF16)
        dz0 = dz * sg + _dot_nt(dsb, wg_ref[...])
        dwg_ref[...] += _dot_tn(z0b, dsb)
        dbg_ref[...] += jnp.sum(ds, axis=0, keepdims=True)
        dy_ref[...] = dz0 * _gelu_grad(yv)

    sq = _const_spec((D_MODEL, D_MODEL))
    vec = _const_spec((1, D_MODEL))
    return pl.pallas_call(
        body, name=name, grid=(t // rows,),
        in_specs=[_row_spec(rows, D_MODEL)] * 3 + [sq, vec, sq],
        out_specs=[_row_spec(rows, D_MODEL), _row_spec(rows, D_MODEL), sq, vec, sq],
        out_shape=[jax.ShapeDtypeStruct((t, D_MODEL), F32)] * 2 + [
            jax.ShapeDtypeStruct((D_MODEL, D_MODEL), F32), jax.ShapeDtypeStruct((1, D_MODEL), F32),
            jax.ShapeDtypeStruct((D_MODEL, D_MODEL), F32)],
        compiler_params=_params(1),
    )(dxn, y, gate, w_glu, b_glu.reshape(1, D_MODEL), w_out)


KV_LANES = GQA * ATTN_BLOCK


def _attn_bias(block_is_first):
    kj = lax.broadcasted_iota(jnp.int32, (2 * ATTN_BLOCK, ATTN_BLOCK), 0)
    qi = lax.broadcasted_iota(jnp.int32, (2 * ATTN_BLOCK, ATTN_BLOCK), 1)
    dist = qi + ATTN_BLOCK - kj
    valid = (dist >= 0) & (dist < ATTN_BLOCK) & (jnp.logical_not(block_is_first) | (kj >= ATTN_BLOCK))
    return jnp.tile(jnp.where(valid, 0.0, NEG_INF).astype(F32), (1, GQA))


def _head_cols(a_t, kvh):
    heads = range(kvh * GQA, (kvh + 1) * GQA)
    return jnp.concatenate([a_t[HEAD_DIM * h:HEAD_DIM * (h + 1)] for h in heads], axis=1)


def _head_rows(a_cols):
    stacked = jnp.concatenate([a_cols[:, ATTN_BLOCK * g:ATTN_BLOCK * (g + 1)] for g in range(GQA)], axis=0)
    return stacked.T


def _kv_rows(prev_ref, cur_ref, kvh):
    lanes = slice(HEAD_DIM * kvh, HEAD_DIM * (kvh + 1))
    return jnp.concatenate([prev_ref[:, lanes], cur_ref[:, lanes]], axis=0).astype(BF16)


def _kv_cols(prev_t, cur_t, kvh):
    rows = slice(HEAD_DIM * kvh, HEAD_DIM * (kvh + 1))
    return jnp.concatenate([prev_t[rows], cur_t[rows]], axis=1).astype(BF16)


def _attn_probs(kk, q_cols, sink_row, bias):
    s = _dot(kk, q_cols) * ATTN_SCALE + bias
    m = jnp.maximum(jnp.max(s, axis=0, keepdims=True), sink_row)
    p = jnp.exp(s - m)
    e_sink = jnp.exp(sink_row - m)
    inv = 1.0 / (jnp.sum(p, axis=0, keepdims=True) + e_sink)
    return p * inv, e_sink * inv


def _sink_cols(sinks):
    return jnp.repeat(sinks, ATTN_BLOCK).reshape(N_KV_HEADS, 1, KV_LANES)


def _attn_fwd(q, k, v, sinks, name):
    t = q.shape[0]
    nblk = t // ATTN_BLOCK

    def body(s_ref, q_ref, kc_ref, kp_ref, vc_ref, vp_ref, o_ref):
        bias = _attn_bias(pl.program_id(0) == 0)
        q_t = q_ref[...].T
        vp_t, vc_t = vp_ref[...].T, vc_ref[...].T
        for kvh in range(N_KV_HEADS):
            p, _ = _attn_probs(_kv_rows(kp_ref, kc_ref, kvh), _head_cols(q_t, kvh).astype(BF16), s_ref[kvh], bias)
            o_cols = _dot(_kv_cols(vp_t, vc_t, kvh), p.astype(BF16))
            o_ref[:, GQA * HEAD_DIM * kvh:GQA * HEAD_DIM * (kvh + 1)] = _head_rows(o_cols)

    cur = lambda i: (i, 0)
    prev = lambda i: (jnp.maximum(i - 1, 0), 0)
    return pl.pallas_call(
        body, name=name, grid=(nblk,),
        in_specs=[_const_spec((N_KV_HEADS, 1, KV_LANES)),
                  pl.BlockSpec((ATTN_BLOCK, Q_DIM), cur),
                  pl.BlockSpec((ATTN_BLOCK, KV_DIM), cur), pl.BlockSpec((ATTN_BLOCK, KV_DIM), prev),
                  pl.BlockSpec((ATTN_BLOCK, KV_DIM), cur), pl.BlockSpec((ATTN_BLOCK, KV_DIM), prev)],
        out_specs=pl.BlockSpec((ATTN_BLOCK, Q_DIM), cur),
        out_shape=jax.ShapeDtypeStruct((t, Q_DIM), F32),
        compiler_params=_params(1),
    )(_sink_cols(sinks), q, k, k, v, v)


def _attn_bwd(q, k, v, o, do, sinks, rope, name):
    t = q.shape[0]
    nblk = t // ATTN_BLOCK

    def body(s_ref, q_ref, o_ref, do_ref, kp_ref, kc_ref, vp_ref, vc_ref, cosq_ref, sinq_ref, cosk_ref, sinkey_ref,
             dq_ref, dk_ref, dv_ref, ds_ref, new_k, new_v, wait_k, wait_v):
        n = pl.program_id(0)

        @pl.when(n == 0)
        def _():
            ds_ref[...] = jnp.zeros_like(ds_ref)
            wait_k[...] = jnp.zeros_like(wait_k)
            wait_v[...] = jnp.zeros_like(wait_v)

        @pl.when(n < nblk)
        def _():
            bias = _attn_bias(n == 0)
            q_t, o_t, do_t = q_ref[...].T, o_ref[...].T, do_ref[...].T
            kp_t, kc_t = kp_ref[...].T, kc_ref[...].T
            for kvh in range(N_KV_HEADS):
                q_cols = _head_cols(q_t, kvh).astype(BF16)
                do_cols = _head_cols(do_t, kvh)
                delta = jnp.sum(do_cols * _head_cols(o_t, kvh), axis=0, keepdims=True)
                do_cols = do_cols.astype(BF16)
                p, p_sink = _attn_probs(_kv_rows(kp_ref, kc_ref, kvh), q_cols, s_ref[kvh], bias)
                dp = _dot(_kv_rows(vp_ref, vc_ref, kvh), do_cols)
                ds = (p * (dp - delta) * ATTN_SCALE).astype(BF16)
                lanes = slice(GQA * HEAD_DIM * kvh, GQA * HEAD_DIM * (kvh + 1))
                dq_ref[:, lanes] = _head_rows(_dot(_kv_cols(kp_t, kc_t, kvh), ds))
                head = slice(HEAD_DIM * kvh, HEAD_DIM * (kvh + 1))
                new_k[:, head] = _dot_nt(ds, q_cols)
                new_v[:, head] = _dot_nt(p.astype(BF16), do_cols)
                ds_ref[kvh] += -(p_sink * delta)
            dq_ref[...] = _rope_apply(dq_ref[...], cosq_ref[...], sinq_ref[...], -1.0)

        @pl.when(n == nblk)
        def _():
            new_k[...] = jnp.zeros_like(new_k)
            new_v[...] = jnp.zeros_like(new_v)

        dk_ref[...] = _rope_apply(wait_k[...] + new_k[:ATTN_BLOCK], cosk_ref[...], sinkey_ref[...], -1.0)
        dv_ref[...] = wait_v[...] + new_v[:ATTN_BLOCK]
        wait_k[...] = new_k[ATTN_BLOCK:]
        wait_v[...] = new_v[ATTN_BLOCK:]

    cur = lambda i: (jnp.minimum(i, nblk - 1), 0)
    prev = lambda i: (jnp.maximum(i - 1, 0), 0)
    qs = lambda f: pl.BlockSpec((ATTN_BLOCK, Q_DIM), f)
    ks = lambda f: pl.BlockSpec((ATTN_BLOCK, KV_DIM), f)
    sink_spec = _const_spec((N_KV_HEADS, 1, KV_LANES))
    return pl.pallas_call(
        body, name=name, grid=(nblk + 1,),
        in_specs=[sink_spec, qs(cur), qs(cur), qs(cur), ks(prev), ks(cur), ks(prev), ks(cur),
                  ks(cur), ks(cur), ks(prev), ks(prev)],
        out_specs=[qs(cur), ks(prev), ks(prev), sink_spec],
        out_shape=[jax.ShapeDtypeStruct((t, Q_DIM), F32), jax.ShapeDtypeStruct((t, KV_DIM), F32),
                   jax.ShapeDtypeStruct((t, KV_DIM), F32), jax.ShapeDtypeStruct((N_KV_HEADS, 1, KV_LANES), F32)],
        scratch_shapes=[pltpu.VMEM((2 * ATTN_BLOCK, KV_DIM), F32), pltpu.VMEM((2 * ATTN_BLOCK, KV_DIM), F32),
                        pltpu.VMEM((ATTN_BLOCK, KV_DIM), F32), pltpu.VMEM((ATTN_BLOCK, KV_DIM), F32)],
        compiler_params=_params(1),
    )(_sink_cols(sinks), q, o, do, k, k, v, v, rope[0], rope[1], rope[0], rope[1])


def _attn_out_fwd(o, gate, x, w_out, name):
    t = x.shape[0]
    rows = ROWS_FWD

    def body(o_ref, g_ref, x_ref, wo_ref, xn_ref):
        gate_v = g_ref[...]
        a = o_ref[...] * (gate_v * _sigmoid(gate_v))
        xn_ref[...] = x_ref[...] + _dot(a.astype(BF16), wo_ref[...])

    return pl.pallas_call(
        body, name=name, grid=(t // rows,),
        in_specs=[_row_spec(rows, D_MODEL)] * 3 + [_const_spec((D_MODEL, D_MODEL))],
        out_specs=_row_spec(rows, D_MODEL),
        out_shape=jax.ShapeDtypeStruct((t, D_MODEL), F32),
        compiler_params=_params(1),
    )(o, gate, x, w_out)


def _attn_out_bwd(dxn, o, gate, w_out, name):
    t = o.shape[0]
    rows = ROWS_BWD

    def body(dxn_ref, o_ref, g_ref, wo_ref, do_ref, dg_ref, dwo_ref):
        @pl.when(pl.program_id(0) == 0)
        def _():
            dwo_ref[...] = jnp.zeros_like(dwo_ref)

        gate_v, ov = g_ref[...], o_ref[...]
        sgg = _sigmoid(gate_v)
        silu = gate_v * sgg
        dob = dxn_ref[...].astype(BF16)
        da = _dot_nt(dob, wo_ref[...])
        dwo_ref[...] += _dot_tn((ov * silu).astype(BF16), dob)
        do_ref[...] = da * silu
        dg_ref[...] = da * ov * (sgg * (1.0 + gate_v * (1.0 - sgg)))

    sq = _const_spec((D_MODEL, D_MODEL))
    return pl.pallas_call(
        body, name=name, grid=(t // rows,),
        in_specs=[_row_spec(rows, D_MODEL)] * 3 + [sq],
        out_specs=[_row_spec(rows, D_MODEL), _row_spec(rows, D_MODEL), sq],
        out_shape=[jax.ShapeDtypeStruct((t, D_MODEL), F32)] * 2 + [jax.ShapeDtypeStruct((D_MODEL, D_MODEL), F32)],
        compiler_params=_params(1),
    )(dxn, o, gate, w_out)


def _loss_head(x, norm, target, name):
    t = x.shape[0]
    rows = ROWS_FWD

    def body(x_ref, n_ref, t_ref, loss_ref, dx_ref, dn_ref):
        i = pl.program_id(0)
        xv = x_ref[...]
        rstd = lax.rsqrt(jnp.mean(xv * xv, axis=-1, keepdims=True) + NORM_EPS)
        xhat = xv * rstd
        err = xhat * n_ref[...] - t_ref[...]
        part = 0.5 * jnp.sum(jnp.mean(err * err, axis=-1, keepdims=True), axis=0, keepdims=True)
        dy = err * (1.0 / D_MODEL)
        dn = jnp.sum(dy * xhat, axis=0, keepdims=True)
        dxhat = dy * n_ref[...]
        dx_ref[...] = rstd * (dxhat - xhat * jnp.mean(dxhat * xhat, axis=-1, keepdims=True))

        @pl.when(i == 0)
        def _():
            loss_ref[...] = jnp.zeros((8, 128), F32) + part
            dn_ref[...] = dn

        @pl.when(i > 0)
        def _():
            loss_ref[...] += part
            dn_ref[...] += dn

    return pl.pallas_call(
        body, name=name, grid=(t // rows,),
        in_specs=[_row_spec(rows, D_MODEL), _const_spec((1, D_MODEL)), _row_spec(rows, D_MODEL)],
        out_specs=[_const_spec((8, 128)), _row_spec(rows, D_MODEL), _const_spec((1, D_MODEL))],
        out_shape=[jax.ShapeDtypeStruct((8, 128), F32), jax.ShapeDtypeStruct((t, D_MODEL), F32),
                   jax.ShapeDtypeStruct((1, D_MODEL), F32)],
        compiler_params=_params(1),
    )(x, norm.reshape(1, D_MODEL), target)


N_CHIPS = 4
N_CORES = 2
CHIP_FLIPS = ((0, 1, 0), (1, 0, 0), (1, 1, 0))
CORE_FLIPS = ((0, 0, 1),)


MAX_CHUNKS = {"chip": 8, "core": 16}


def _n_chunks(rows, dtype, most):
    unit = 16 if dtype == BF16 else 8
    return max(n for n in range(1, most + 1) if rows % n == 0 and (rows // n) % unit == 0)


def _exchange(send, among, per_dest, name):
    flips, parties = (CHIP_FLIPS, N_CHIPS) if among == "chip" else (CORE_FLIPS, N_CORES)
    rows, cols = send.shape[-2:]
    chunks = _n_chunks(rows, send.dtype, MAX_CHUNKS[among])
    chunk_rows = rows // chunks

    def body(send_ref, recv_ref, send_sems, recv_sems, local_sems):
        x, y, c = lax.axis_index("x"), lax.axis_index("y"), lax.axis_index("c")

        def number(px, py, pc):
            return 2 * px + py if among == "chip" else pc

        me = number(x, y, c)

        def peer(k):
            fx, fy, fc = flips[k]
            to = (x + fx - 2 * x * fx, y + fy - 2 * y * fy, c + fc - 2 * c * fc)
            return to, number(*to)

        def src(to_number, j):
            part = pl.ds(j * chunk_rows, chunk_rows)
            return send_ref.at[to_number, part] if per_dest else send_ref.at[part]

        def copy(k, j, landing):
            to, to_number = peer(k)
            return pltpu.make_async_remote_copy(
                src_ref=src(to_number, j), dst_ref=recv_ref.at[landing, pl.ds(j * chunk_rows, chunk_rows)],
                send_sem=send_sems.at[k, j], recv_sem=recv_sems.at[k, j],
                device_id=to, device_id_type=pl.DeviceIdType.MESH)

        def own(j):
            return pltpu.make_async_copy(src(me, j), recv_ref.at[me, pl.ds(j * chunk_rows, chunk_rows)], local_sems.at[j])

        for j in range(chunks):
            for k in range(len(flips)):
                copy(k, j, me).start()
            own(j).start()
        for j in range(chunks):
            for k in range(len(flips)):
                copy(k, j, me).wait_send()
                copy(k, j, peer(k)[1]).wait_recv()
            own(j).wait()

    hbm = pl.BlockSpec(memory_space=pltpu.HBM)
    return pl.pallas_call(
        body, name=name, in_specs=[hbm], out_specs=hbm,
        out_shape=jax.ShapeDtypeStruct((parties, rows, cols), send.dtype),
        scratch_shapes=[pltpu.SemaphoreType.DMA((len(flips), chunks)), pltpu.SemaphoreType.DMA((len(flips), chunks)),
                        pltpu.SemaphoreType.DMA((chunks,))],
    )(send)


def _swap_cores(a, name):
    rows, cols = a.shape
    chunks = _n_chunks(rows, a.dtype, MAX_CHUNKS["core"])
    chunk_rows = rows // chunks

    def body(a_ref, got_ref, send_sems, recv_sems):
        sibling = (lax.axis_index("x"), lax.axis_index("y"), 1 - lax.axis_index("c"))

        def copy(j):
            part = pl.ds(j * chunk_rows, chunk_rows)
            return pltpu.make_async_remote_copy(
                src_ref=a_ref.at[part], dst_ref=got_ref.at[part], send_sem=send_sems.at[j], recv_sem=recv_sems.at[j],
                device_id=sibling, device_id_type=pl.DeviceIdType.MESH)

        for j in range(chunks):
            copy(j).start()
        for j in range(chunks):
            copy(j).wait()

    hbm = pl.BlockSpec(memory_space=pltpu.HBM)
    return pl.pallas_call(
        body, name=name, in_specs=[hbm], out_specs=hbm, out_shape=jax.ShapeDtypeStruct(a.shape, a.dtype),
        scratch_shapes=[pltpu.SemaphoreType.DMA((chunks,)), pltpu.SemaphoreType.DMA((chunks,))],
    )(a)


def _by_core(mine, other):
    first = lax.axis_index("c") == 0
    return jnp.stack([jnp.where(first, mine, other), jnp.where(first, other, mine)])


def _all_gather(a, name):
    rows, cols = a.shape
    by_chip = _exchange(a, "chip", False, name + "_chips").reshape(N_CHIPS * rows, cols)
    return _by_core(by_chip, _swap_cores(by_chip, name + "_cores")).reshape(N_CORES, N_CHIPS, rows, cols)


def _sum_parts(parts, out_dtype, name):
    n, cols = parts[0].shape
    rows = min(n, 256)
    while n % rows:
        rows -= 16

    def body(*refs):
        acc = refs[0][...].astype(F32)
        for ref in refs[1:-1]:
            acc = acc + ref[...].astype(F32)
        refs[-1][...] = acc.astype(out_dtype)

    return pl.pallas_call(
        body, name=name, grid=(n // rows,),
        in_specs=[_row_spec(rows, cols)] * len(parts),
        out_specs=_row_spec(rows, cols),
        out_shape=jax.ShapeDtypeStruct((n, cols), out_dtype),
        compiler_params=_params(1),
    )(*parts)


def _reduce_scatter(send, wire_dtype, name):
    rows, cols = send.shape[2:]
    send = send.reshape(N_CORES, N_CHIPS * rows, cols)
    first = lax.axis_index("c") == 0
    keep, give = jnp.where(first, send[0], send[1]), jnp.where(first, send[1], send[0])
    chip_sums = _sum_parts([keep, _swap_cores(give, name + "_cores")], wire_dtype, name + "_core_sum")
    return _exchange(chip_sums.reshape(N_CHIPS, rows, cols), "chip", True, name + "_chips")


def _adamw(parts, w, m, v, name):
    n, cols = w.shape
    k = parts.shape[0]
    rows = min(n, 256)
    while n % rows:
        rows -= 8
    c1 = 1.0 - ADAM_B1 ** ADAM_STEP
    c2 = 1.0 - ADAM_B2 ** ADAM_STEP

    def body(p_ref, w_ref, m_ref, v_ref, g_ref, d_ref, nm_ref, nv_ref):
        g = p_ref[0].astype(F32)
        for s in range(1, k):
            g = g + p_ref[s].astype(F32)
        nm = ADAM_B1 * m_ref[...] + (1.0 - ADAM_B1) * g
        nv = ADAM_B2 * v_ref[...] + (1.0 - ADAM_B2) * (g * g)
        g_ref[...] = g
        nm_ref[...] = nm
        nv_ref[...] = nv
        d_ref[...] = -ADAM_LR * ((nm / c1) / (jnp.sqrt(nv / c2) + ADAM_EPS) + ADAM_WD * w_ref[...])

    blk = _row_spec(rows, cols)
    return pl.pallas_call(
        body, name=name, grid=(n // rows,),
        in_specs=[pl.BlockSpec((k, rows, cols), lambda i: (0, i, 0)), blk, blk, blk],
        out_specs=[blk] * 4,
        out_shape=[jax.ShapeDtypeStruct((n, cols), F32)] * 4,
        compiler_params=_params(1),
    )(parts, w, m, v)


SSM_KEYS = ("norm", "w_in", "a_re", "a_im", "log_step", "b_re", "b_im", "c_re", "c_im", "d", "w_glu", "b_glu", "w_out")
ATTN_KEYS = ("norm", "w_in", "sinks", "w_out")
LAYER_KEYS = (SSM_KEYS, ATTN_KEYS, SSM_KEYS, ATTN_KEYS)
BIG_KEYS = ("w_in", "w_glu", "w_out")
ATTN_SPLITS = (Q_DIM, KV_DIM, KV_DIM, D_MODEL)


def _rope_tables(t):
    pos = jnp.arange(t, dtype=F32)
    inv_freq = ROPE_THETA ** (-jnp.arange(0, HEAD_DIM, 2, dtype=F32) / HEAD_DIM)
    ang = pos[:, None] * inv_freq[None, :]
    cos, sin = jnp.cos(ang), jnp.sin(ang)
    return jnp.tile(jnp.concatenate([cos, cos], axis=1), (1, 2)), jnp.tile(jnp.concatenate([-sin, sin], axis=1), (1, 2))


def _ssm_layer_fwd(i, x, p, w):
    tag = "l%d_" % i
    mats, mats_vjp = jax.vjp(_s5_matrices, p["a_re"], p["a_im"], p["log_step"], p["b_re"], p["b_im"], p["c_re"], p["c_im"])
    tm, cpt, bpt, ar, ai = mats
    mb = dict(tm=tm.astype(BF16), tmt=jnp.swapaxes(tm, 1, 2).astype(BF16), cpt=cpt.astype(BF16),
              cp=jnp.swapaxes(cpt, 1, 2).astype(BF16), bpt=bpt.astype(BF16), bp=jnp.swapaxes(bpt, 1, 2).astype(BF16))
    u, gate = _inproj_fwd(x, p["norm"], w["w_in"], (D_MODEL, D_MODEL), None, tag + "inproj_fwd")
    xre, xim = _s5_project(u, mb["bpt"], tag + "s5_block_inputs")
    hre, him = _s5_scan_fwd(xre, xim, ar, ai, tag + "s5_scan_fwd")
    y = _s5_outputs(u, hre, him, mb["tm"], mb["cpt"], p["d"], tag + "s5_outputs")
    xn = _ssm_out_fwd(y, gate, x, w["w_glu"], p["b_glu"], w["w_out"], tag + "out_fwd")
    return xn, (x, u, gate, y, hre, him, mb, ar, ai, mats_vjp)


def _ssm_layer_bwd(i, dxn, saved, p, w):
    tag = "l%d_" % i
    x, u, gate, y, hre, him, mb, ar, ai, mats_vjp = saved
    dy, dgate, dw_glu, db_glu, dw_out = _ssm_out_bwd(dxn, y, gate, w["w_glu"], p["b_glu"], w["w_out"], tag + "out_bwd")
    dhre, dhim = _s5_project(dy, mb["cp"], tag + "s5_state_grads")
    dxre, dxim, dar, dai = _s5_scan_bwd(dhre, dhim, hre, him, ar, ai, tag + "s5_scan_bwd")
    du, dtm, dcpt, dbpt, dd = _s5_backward(dy, u, hre, him, dxre, dxim, mb["tmt"], mb["bp"], p["d"], tag + "s5_backward")
    da_re, da_im, dlog_step, db_re, db_im, dc_re, dc_im = mats_vjp((dtm, dcpt, dbpt, dar, dai))
    dx, dw_in, dnorm = _inproj_bwd(x, p["norm"], w["w_in"], [du, dgate], dxn, tag + "inproj_bwd")
    grads = dict(norm=dnorm.reshape(D_MODEL), w_in=dw_in, a_re=da_re, a_im=da_im, log_step=dlog_step, b_re=db_re,
                 b_im=db_im, c_re=dc_re, c_im=dc_im, d=dd.reshape(D_MODEL), w_glu=dw_glu, b_glu=db_glu.reshape(D_MODEL),
                 w_out=dw_out)
    return dx, grads


def _attn_layer_fwd(i, x, p, w, rope):
    tag = "l%d_" % i
    q, k, v, gate = _inproj_fwd(x, p["norm"], w["w_in"], ATTN_SPLITS, rope, tag + "inproj_fwd")
    o = _attn_fwd(q, k, v, p["sinks"], tag + "attn_fwd")
    xn = _attn_out_fwd(o, gate, x, w["w_out"], tag + "out_fwd")
    return xn, (x, q, k, v, gate, o)


def _attn_layer_bwd(i, dxn, saved, p, w, rope):
    tag = "l%d_" % i
    x, q, k, v, gate, o = saved
    do, dgate, dw_out = _attn_out_bwd(dxn, o, gate, w["w_out"], tag + "out_bwd")
    dq, dk, dv, dsinks = _attn_bwd(q, k, v, o, do, p["sinks"], rope, tag + "attn_bwd")
    dx, dw_in, dnorm = _inproj_bwd(x, p["norm"], w["w_in"], [dq, dk, dv, dgate], dxn, tag + "inproj_bwd")
    return dx, dict(norm=dnorm.reshape(D_MODEL), w_in=dw_in, sinks=dsinks.reshape(N_Q_HEADS, ATTN_BLOCK).sum(axis=1), w_out=dw_out)


def _local_step(x, target, small, big):
    rope = _rope_tables(x.shape[0])
    saved = []
    for i in range(4):
        if i % 2 == 0:
            x, s = _ssm_layer_fwd(i, x, small[i], big[i])
        else:
            x, s = _attn_layer_fwd(i, x, small[i], big[i], rope)
        saved.append(s)
    loss, dx, dfinal = _loss_head(x, small[4]["norm"], target, "loss_head")
    grads = [None] * 4 + [dict(norm=dfinal.reshape(D_MODEL))]
    for i in (3, 2, 1, 0):
        if i % 2 == 0:
            dx, grads[i] = _ssm_layer_bwd(i, dx, saved[i], small[i], big[i])
        else:
            dx, grads[i] = _attn_layer_bwd(i, dx, saved[i], small[i], big[i], rope)
    return loss[0, 0], dx, grads


def _shard_rows(key, a):
    return a.reshape(-1, D_MODEL)


def _owner_major(key, g):
    if key == "w_in":
        g = g.reshape(D_MODEL, N_CHIPS, N_CORES, -1).transpose(2, 1, 0, 3)
    else:
        g = g.reshape(N_CHIPS, N_CORES, -1, D_MODEL).transpose(1, 0, 2, 3)
    return g.reshape(N_CORES, N_CHIPS, -1, D_MODEL)


def _from_gathered(key, a, shard_shape):
    if key == "w_in":
        cols = shard_shape[1]
        return a.reshape(N_CORES, N_CHIPS, D_MODEL, cols).transpose(2, 1, 0, 3).reshape(D_MODEL, N_DEV * cols)
    return a.transpose(1, 0, 2, 3).reshape(N_DEV * shard_shape[0], shard_shape[1])


SMALL_ROWS = 72


def _pad_rows(flat):
    n = flat.shape[0]
    rows = N_DEV * SMALL_ROWS
    assert n <= rows * D_MODEL
    return jnp.pad(flat, (0, rows * D_MODEL - n)).reshape(rows, D_MODEL)


def kernel(*args):
    names = ["x"]
    layer_names = []
    for i, keys in enumerate(LAYER_KEYS):
        layer_names += ["l%d_%s" % (i, k) for k in keys]
    layer_names.append("final_norm")
    names += layer_names + ["loss_target"] + ["m_" + n for n in layer_names] + ["v_" + n for n in layer_names]
    given = dict(zip(names, args))
    big_names = [n for n in layer_names if n.split("_", 1)[1] in BIG_KEYS]
    small_names = [n for n in layer_names if n not in big_names]

    offsets, rows_at = {}, 0
    for n in big_names:
        offsets[n] = rows_at
        rows_at += given[n].size // D_MODEL
    local_rows = jnp.concatenate([_shard_rows(n, given[n]) for n in big_names], axis=0)
    big_rows = rows_at
    gathered = _all_gather(local_rows.astype(BF16), "gather_weights")
    big = [dict() for _ in range(4)]
    for n in big_names:
        layer, key = int(n[1]), n.split("_", 1)[1]
        rows = given[n].size // D_MODEL
        big[layer][key] = _from_gathered(key, gathered[:, :, offsets[n]:offsets[n] + rows, :], given[n].shape)
    small = [dict() for _ in range(5)]
    for n in small_names:
        if n == "final_norm":
            small[4]["norm"] = given[n]
        else:
            small[int(n[1])][n.split("_", 1)[1]] = given[n]

    loss, dx, grads = _local_step(given["x"][0], given["loss_target"][0], small, big)
    loss = lax.psum(loss, ("x", "y", "c"))

    def grad_of(n):
        return grads[4]["norm"] if n == "final_norm" else grads[int(n[1])][n.split("_", 1)[1]]

    send = jnp.concatenate([_owner_major(n.split("_", 1)[1], grad_of(n)) for n in big_names], axis=2)
    parts = _reduce_scatter(send, BF16, "scatter_weight_grads")
    cat = lambda pre: jnp.concatenate([_shard_rows(n, given[pre + n]) for n in big_names], axis=0)
    g_big, d_big, m_big, v_big = _adamw(parts, cat(""), cat("m_"), cat("v_"), "adamw_matrices")

    flat = lambda f: _pad_rows(jnp.concatenate([f(n).reshape(-1) for n in small_names]))
    parts = _reduce_scatter(flat(grad_of).reshape(N_CORES, N_CHIPS, SMALL_ROWS, D_MODEL), F32, "scatter_small_grads")
    my_slice = _sum_parts([parts[s] for s in range(N_CHIPS)], F32, "sum_small_grads")
    g_all = _all_gather(my_slice, "gather_small_grads").reshape(1, N_DEV * SMALL_ROWS, D_MODEL)
    g_small, d_small, m_small, v_small = _adamw(
        g_all, flat(lambda n: given[n]), flat(lambda n: given["m_" + n]), flat(lambda n: given["v_" + n]), "adamw_small")

    outs = {}
    for tag, a_big, a_small in (("grad_", g_big, g_small), ("delta_", d_big, d_small),
                                ("new_m_", m_big, m_small), ("new_v_", v_big, v_small)):
        for n in big_names:
            rows = given[n].size // D_MODEL
            outs[tag + n] = a_big[offsets[n]:offsets[n] + rows].reshape(given[n].shape)
        a_flat, at = a_small.reshape(-1), 0
        for n in small_names:
            outs[tag + n] = a_flat[at:at + given[n].size].reshape(given[n].shape)
            at += given[n].size
    result = [loss, dx[None]]
    for tag in ("grad_", "delta_", "new_m_", "new_v_"):
        result += [outs[tag + n] for n in layer_names]
    return tuple(result)
```

```python
import functools
import math

import jax
import jax.numpy as jnp
from jax import lax
from jax.experimental import pallas as pl
from jax.experimental.pallas import tpu as pltpu

F32 = jnp.float32
BF16 = jnp.bfloat16

D_MODEL = 1024
SSM_GROUP = 16
SSM_GROUPS = D_MODEL // SSM_GROUP
SSM_STATE = 64
S5_BLOCK = 16
S5_LANES = S5_BLOCK * SSM_GROUP
HEAD_DIM = 64
N_Q_HEADS = 16
N_KV_HEADS = 2
GQA = N_Q_HEADS // N_KV_HEADS
Q_DIM = N_Q_HEADS * HEAD_DIM
KV_DIM = N_KV_HEADS * HEAD_DIM
ATTN_BLOCK = 128
ROPE_THETA = 10000.0
NORM_EPS = 1e-5
NEG_INF = -1e30
ATTN_SCALE = HEAD_DIM ** -0.5
N_DEV = 8

ADAM_LR = 0.001
ADAM_B1 = 0.9
ADAM_B2 = 0.999
ADAM_EPS = 1e-08
ADAM_WD = 0.01
ADAM_STEP = 10

VMEM_LIMIT = 56 * 1024 * 1024
ROWS_FWD = 512
ROWS_BWD = 512

NT = (((1,), (1,)), ((), ()))
TN = (((0,), (0,)), ((), ()))


def _params(n_grid):
    return pltpu.CompilerParams(dimension_semantics=("arbitrary",) * n_grid, vmem_limit_bytes=VMEM_LIMIT)


def _dot(a, b):
    return jnp.dot(a, b, preferred_element_type=F32)


def _dot_nt(a, b):
    return lax.dot_general(a, b, NT, preferred_element_type=F32)


def _dot_tn(a, b):
    return lax.dot_general(a, b, TN, preferred_element_type=F32)


def _sigmoid(x):
    return 1.0 / (1.0 + jnp.exp(-x))


_GELU_K = math.sqrt(2.0 / math.pi)


def _gelu(x):
    return x * (0.5 * (1.0 + jnp.tanh(_GELU_K * (x + 0.044715 * (x * x * x)))))


def _gelu_grad(x):
    t = jnp.tanh(_GELU_K * (x + 0.044715 * (x * x * x)))
    return 0.5 * (1.0 + t) + 0.5 * x * (1.0 - t * t) * (_GELU_K * (1.0 + 3.0 * 0.044715 * (x * x)))


def _row_spec(rows, cols):
    return pl.BlockSpec((rows, cols), lambda i: (i, 0))


def _const_spec(shape):
    zeros = (0,) * len(shape)
    return pl.BlockSpec(shape, lambda i: zeros, pipeline_mode=pl.Buffered(1))


def _rope_apply(t, cos, sin_signed, sign):
    lane = lax.broadcasted_iota(jnp.int32, (1, 128), 1)
    first_half = (lane % HEAD_DIM) < (HEAD_DIM // 2)
    out = []
    for j in range(t.shape[1] // 128):
        tj = t[:, 128 * j:128 * (j + 1)]
        partner = jnp.where(first_half, pltpu.roll(tj, 128 - HEAD_DIM // 2, 1), pltpu.roll(tj, HEAD_DIM // 2, 1))
        out.append(tj * cos + sign * (partner * sin_signed))
    return out[0] if len(out) == 1 else jnp.concatenate(out, axis=1)


def _inproj_fwd(x, norm, w, splits, rope, name):
    t = x.shape[0]
    n = w.shape[1]
    rows = ROWS_FWD

    def body(*refs):
        if rope is None:
            x_ref, n_ref, w_ref = refs[:3]
            outs = refs[3:]
        else:
            x_ref, n_ref, w_ref, cos_ref, sin_ref = refs[:5]
            outs = refs[5:]
        xv = x_ref[...]
        rstd = lax.rsqrt(jnp.mean(xv * xv, axis=-1, keepdims=True) + NORM_EPS)
        h = (xv * rstd) * n_ref[...]
        proj = _dot(h.astype(BF16), w_ref[...])
        off = 0
        for i, width in enumerate(splits):
            piece = proj[:, off:off + width]
            if rope is not None and i < 2:
                piece = _rope_apply(piece, cos_ref[...], sin_ref[...], 1.0)
            outs[i][...] = piece
            off += width

    in_specs = [_row_spec(rows, D_MODEL), _const_spec((1, D_MODEL)), _const_spec((D_MODEL, n))]
    args = [x, norm.reshape(1, D_MODEL), w]
    if rope is not None:
        in_specs += [_row_spec(rows, 128), _row_spec(rows, 128)]
        args += list(rope)
    return pl.pallas_call(
        body, name=name, grid=(t // rows,), in_specs=in_specs,
        out_specs=[_row_spec(rows, width) for width in splits],
        out_shape=[jax.ShapeDtypeStruct((t, width), F32) for width in splits],
        compiler_params=_params(1),
    )(*args)


def _inproj_bwd(x, norm, w, dpieces, dxn, name):
    t = x.shape[0]
    n = w.shape[1]
    rows = ROWS_BWD
    widths = [p.shape[1] for p in dpieces]
    k = len(dpieces)

    def body(*refs):
        x_ref, n_ref, w_ref, dxn_ref = refs[:4]
        d_refs = refs[4:4 + k]
        dx_ref, dw_ref, dn_ref = refs[4 + k:]
        @pl.when(pl.program_id(0) == 0)
        def _():
            dw_ref[...] = jnp.zeros_like(dw_ref)
            dn_ref[...] = jnp.zeros_like(dn_ref)

        xv = x_ref[...]
        rstd = lax.rsqrt(jnp.mean(xv * xv, axis=-1, keepdims=True) + NORM_EPS)
        xhat = xv * rstd
        h = xhat * n_ref[...]
        dproj = [r[...].astype(BF16) for r in d_refs]
        dproj = dproj[0] if k == 1 else jnp.concatenate(dproj, axis=1)
        dh = _dot_nt(dproj, w_ref[...])
        dw_ref[...] += _dot_tn(h.astype(BF16), dproj)
        dn_ref[...] += jnp.sum(dh * xhat, axis=0, keepdims=True)
        dxhat = dh * n_ref[...]
        dx_ref[...] = rstd * (dxhat - xhat * jnp.mean(dxhat * xhat, axis=-1, keepdims=True)) + dxn_ref[...]

    return pl.pallas_call(
        body, name=name, grid=(t // rows,),
        in_specs=[_row_spec(rows, D_MODEL), _const_spec((1, D_MODEL)), _const_spec((D_MODEL, n)),
                  _row_spec(rows, D_MODEL)] + [_row_spec(rows, width) for width in widths],
        out_specs=[_row_spec(rows, D_MODEL), _const_spec((D_MODEL, n)), _const_spec((1, D_MODEL))],
        out_shape=[jax.ShapeDtypeStruct((t, D_MODEL), F32), jax.ShapeDtypeStruct((D_MODEL, n), F32),
                   jax.ShapeDtypeStruct((1, D_MODEL), F32)],
        compiler_params=_params(1),
    )(x, norm.reshape(1, D_MODEL), w, dxn, *dpieces)


def _s5_matrices(a_re, a_im, log_step, b_re, b_im, c_re, c_im):
    r = S5_BLOCK
    step = jnp.exp(log_step)[:, None]
    lr, li = a_re * step, a_im * step
    k = jnp.arange(r + 1, dtype=F32)
    mag = jnp.exp(lr[:, None, :] * k[:, None])
    pr = mag * jnp.cos(li[:, None, :] * k[:, None])
    pi = mag * jnp.sin(li[:, None, :] * k[:, None])
    nr, ni = pr[:, 1] - 1.0, pi[:, 1]
    den = a_re * a_re + a_im * a_im
    qr, qi = (nr * a_re + ni * a_im) / den, (ni * a_re - nr * a_im) / den
    bbr = qr[..., None] * b_re - qi[..., None] * b_im
    bbi = qr[..., None] * b_im + qi[..., None] * b_re
    wr = c_re[:, None] * pr[:, :, None, :] - c_im[:, None] * pi[:, :, None, :]
    wi = c_re[:, None] * pi[:, :, None, :] + c_im[:, None] * pr[:, :, None, :]
    w = jnp.concatenate([wr, -wi], axis=-1)
    bb = jnp.concatenate([bbr, bbi], axis=1)
    kern = jnp.einsum("gxp,gpi->gxi", w[:, :r].reshape(SSM_GROUPS, S5_LANES, 2 * SSM_STATE), bb,
                      precision=lax.Precision.HIGHEST).reshape(SSM_GROUPS, r, SSM_GROUP, SSM_GROUP)
    cpt = w[:, 1:].reshape(SSM_GROUPS, S5_LANES, 2 * SSM_STATE)
    prs = jnp.swapaxes(pr[:, r - 1::-1][:, :r], 1, 2)[..., None]
    pis = jnp.swapaxes(pi[:, r - 1::-1][:, :r], 1, 2)[..., None]
    bp_re = prs * bbr[:, :, None, :] - pis * bbi[:, :, None, :]
    bp_im = prs * bbi[:, :, None, :] + pis * bbr[:, :, None, :]
    bpt = jnp.concatenate([bp_re, bp_im], axis=1).reshape(SSM_GROUPS, 2 * SSM_STATE, S5_LANES)
    ar = pr[:, r].reshape(1, SSM_GROUPS * SSM_STATE)
    ai = pi[:, r].reshape(1, SSM_GROUPS * SSM_STATE)
    return kern, cpt, bpt, ar, ai


def _s5_toeplitz(kern):
    r = S5_BLOCK
    cols = [jnp.pad(kern[:, :r - s], ((0, 0), (s, 0), (0, 0), (0, 0))) for s in range(r)]
    return jnp.stack(cols, axis=3).reshape(SSM_GROUPS, S5_LANES, S5_LANES)


S5_OCTET = 128 // SSM_GROUP
S5_STEPS = SSM_GROUPS // S5_OCTET


def _oct_spec(t):
    return pl.BlockSpec((t, 128), lambda j: (0, j))


def _state_spec(nb):
    return pl.BlockSpec((nb, S5_OCTET * SSM_STATE), lambda j: (0, j))


def _gmat_spec(a, b):
    return pl.BlockSpec((S5_OCTET, a, b), lambda j: (j, 0, 0))


def _block_rows(ref, nb):
    return [ref[pl.ds(r, nb, stride=S5_BLOCK), :] for r in range(S5_BLOCK)]


def _group_cols(pieces_t, g):
    return jnp.concatenate([p[SSM_GROUP * g:SSM_GROUP * (g + 1)] for p in pieces_t], axis=0)


def _state_cols(re_t, im_t, g):
    return jnp.concatenate([re_t[SSM_STATE * g:SSM_STATE * (g + 1)], im_t[SSM_STATE * g:SSM_STATE * (g + 1)]], axis=0)


def _s5_project(a, mat, name):
    t = a.shape[0]
    nb = t // S5_BLOCK

    def body(a_ref, m_ref, re_ref, im_ref):
        at = [p.T for p in _block_rows(a_ref, nb)]
        for pair in range(S5_OCTET // 2):
            xs = [_dot(m_ref[2 * pair + k], _group_cols(at, 2 * pair + k).astype(BF16)) for k in (0, 1)]
            lanes = slice(128 * pair, 128 * (pair + 1))
            re_ref[:, lanes] = jnp.concatenate([xs[0][:SSM_STATE], xs[1][:SSM_STATE]], axis=0).T
            im_ref[:, lanes] = jnp.concatenate([xs[0][SSM_STATE:], xs[1][SSM_STATE:]], axis=0).T

    return pl.pallas_call(
        body, name=name, grid=(S5_STEPS,),
        in_specs=[_oct_spec(t), _gmat_spec(2 * SSM_STATE, S5_LANES)],
        out_specs=[_state_spec(nb), _state_spec(nb)],
        out_shape=[jax.ShapeDtypeStruct((nb, SSM_GROUPS * SSM_STATE), F32)] * 2,
        compiler_params=_params(1),
    )(a, mat)


_SCAN_LANES = 1024


def _s5_scan_fwd(xre, xim, ar, ai, name):
    nb = xre.shape[0]
    col = pl.BlockSpec((nb, _SCAN_LANES), lambda j: (0, j))
    par = pl.BlockSpec((1, _SCAN_LANES), lambda j: (0, j))

    def body(xre_ref, xim_ref, ar_ref, ai_ref, hre_ref, him_ref):
        a_r, a_i = ar_ref[...], ai_ref[...]

        def step(b, carry):
            hr, hi = carry
            hre_ref[pl.ds(b, 1), :] = hr
            him_ref[pl.ds(b, 1), :] = hi
            xr, xi = xre_ref[pl.ds(b, 1), :], xim_ref[pl.ds(b, 1), :]
            return a_r * hr - a_i * hi + xr, a_r * hi + a_i * hr + xi

        zero = jnp.zeros((1, _SCAN_LANES), F32)
        lax.fori_loop(0, nb, step, (zero, zero))

    return pl.pallas_call(
        body, name=name, grid=(xre.shape[1] // _SCAN_LANES,),
        in_specs=[col, col, par, par], out_specs=[col, col],
        out_shape=[jax.ShapeDtypeStruct(xre.shape, F32)] * 2,
        compiler_params=_params(1),
    )(xre, xim, ar, ai)


def _s5_scan_bwd(dhre, dhim, hre, him, ar, ai, name):
    nb = dhre.shape[0]
    col = pl.BlockSpec((nb, _SCAN_LANES), lambda j: (0, j))
    par = pl.BlockSpec((1, _SCAN_LANES), lambda j: (0, j))

    def body(dhre_ref, dhim_ref, hre_ref, him_ref, ar_ref, ai_ref, dxre_ref, dxim_ref, dar_ref, dai_ref):
        a_r, a_i = ar_ref[...], ai_ref[...]

        def step(s, carry):
            gr, gi, dar, dai = carry
            b = nb - 1 - s
            dxre_ref[pl.ds(b, 1), :] = gr
            dxim_ref[pl.ds(b, 1), :] = gi
            hr, hi = hre_ref[pl.ds(b, 1), :], him_ref[pl.ds(b, 1), :]
            dar = dar + (hr * gr + hi * gi)
            dai = dai + (hr * gi - hi * gr)
            dr, di = dhre_ref[pl.ds(b, 1), :], dhim_ref[pl.ds(b, 1), :]
            return dr + (a_r * gr + a_i * gi), di + (a_r * gi - a_i * gr), dar, dai

        zero = jnp.zeros((1, _SCAN_LANES), F32)
        _, _, dar, dai = lax.fori_loop(0, nb, step, (zero, zero, zero, zero))
        dar_ref[...] = dar
        dai_ref[...] = dai

    return pl.pallas_call(
        body, name=name, grid=(dhre.shape[1] // _SCAN_LANES,),
        in_specs=[col, col, col, col, par, par], out_specs=[col, col, par, par],
        out_shape=[jax.ShapeDtypeStruct(dhre.shape, F32)] * 2 + [jax.ShapeDtypeStruct(ar.shape, F32)] * 2,
        compiler_params=_params(1),
    )(dhre, dhim, hre, him, ar, ai)


def _s5_outputs(u, hre, him, tm, cpt, d, name):
    t = u.shape[0]
    nb = t // S5_BLOCK

    def body(u_ref, hre_ref, him_ref, tm_ref, cpt_ref, d_ref, y_ref):
        u_rows = _block_rows(u_ref, nb)
        ut = [p.T for p in u_rows]
        hre_t, him_t = hre_ref[...].T, him_ref[...].T
        yts = []
        for g in range(S5_OCTET):
            yts.append(_dot(tm_ref[g], _group_cols(ut, g).astype(BF16))
                       + _dot(cpt_ref[g], _state_cols(hre_t, him_t, g).astype(BF16)))
        for r in range(S5_BLOCK):
            rows = jnp.concatenate([yt[SSM_GROUP * r:SSM_GROUP * (r + 1)] for yt in yts], axis=0)
            y_ref[pl.ds(r, nb, stride=S5_BLOCK), :] = rows.T + d_ref[...] * u_rows[r]

    return pl.pallas_call(
        body, name=name, grid=(S5_STEPS,),
        in_specs=[_oct_spec(t), _state_spec(nb), _state_spec(nb), _gmat_spec(S5_LANES, S5_LANES),
                  _gmat_spec(S5_LANES, 2 * SSM_STATE), _oct_spec(1)],
        out_specs=_oct_spec(t),
        out_shape=jax.ShapeDtypeStruct(u.shape, F32),
        compiler_params=_params(1),
    )(u, hre, him, tm, cpt, d.reshape(1, D_MODEL))


def _s5_backward(dy, u, hre, him, dxre, dxim, tmt, bp, d, name):
    t = u.shape[0]
    nb = t // S5_BLOCK

    def body(dy_ref, u_ref, hre_ref, him_ref, dxre_ref, dxim_ref, tmt_ref, bp_ref, d_ref,
             du_ref, dk_ref, dcpt_ref, dbpt_ref, dd_ref, dtm_scratch):
        dy_rows, u_rows = _block_rows(dy_ref, nb), _block_rows(u_ref, nb)
        dyt, ut = [p.T for p in dy_rows], [p.T for p in u_rows]
        hre_t, him_t = hre_ref[...].T, him_ref[...].T
        dxre_t, dxim_t = dxre_ref[...].T, dxim_ref[...].T
        duts = []
        for g in range(S5_OCTET):
            dyg, ug = _group_cols(dyt, g).astype(BF16), _group_cols(ut, g).astype(BF16)
            hg = _state_cols(hre_t, him_t, g).astype(BF16)
            dxg = _state_cols(dxre_t, dxim_t, g).astype(BF16)
            duts.append(_dot(tmt_ref[g], dyg) + _dot(bp_ref[g], dxg))
            dtm_scratch[...] = _dot_nt(dyg, ug)
            dk = dtm_scratch[:, :SSM_GROUP]
            for s in range(1, S5_BLOCK):
                below = dtm_scratch[SSM_GROUP * s:, SSM_GROUP * s:SSM_GROUP * (s + 1)]
                dk = dk + jnp.concatenate([below, jnp.zeros((SSM_GROUP * s, SSM_GROUP), F32)], axis=0)
            dk_ref[g] = dk
            dcpt_ref[g] = _dot_nt(dyg, hg)
            dbpt_ref[g] = _dot_nt(dxg, ug)
        dd = jnp.zeros((1, 128), F32)
        for r in range(S5_BLOCK):
            rows = jnp.concatenate([dut[SSM_GROUP * r:SSM_GROUP * (r + 1)] for dut in duts], axis=0)
            du_ref[pl.ds(r, nb, stride=S5_BLOCK), :] = rows.T + d_ref[...] * dy_rows[r]
            dd = dd + jnp.sum(dy_rows[r] * u_rows[r], axis=0, keepdims=True)
        dd_ref[...] = dd

    return pl.pallas_call(
        body, name=name, grid=(S5_STEPS,),
        in_specs=[_oct_spec(t), _oct_spec(t), _state_spec(nb), _state_spec(nb), _state_spec(nb), _state_spec(nb),
                  _gmat_spec(S5_LANES, S5_LANES), _gmat_spec(S5_LANES, 2 * SSM_STATE), _oct_spec(1)],
        out_specs=[_oct_spec(t), _gmat_spec(S5_LANES, SSM_GROUP), _gmat_spec(S5_LANES, 2 * SSM_STATE),
                   _gmat_spec(2 * SSM_STATE, S5_LANES), _oct_spec(1)],
        out_shape=[jax.ShapeDtypeStruct(u.shape, F32),
                   jax.ShapeDtypeStruct((SSM_GROUPS, S5_LANES, SSM_GROUP), F32),
                   jax.ShapeDtypeStruct((SSM_GROUPS, S5_LANES, 2 * SSM_STATE), F32),
                   jax.ShapeDtypeStruct((SSM_GROUPS, 2 * SSM_STATE, S5_LANES), F32),
                   jax.ShapeDtypeStruct((1, D_MODEL), F32)],
        scratch_shapes=[pltpu.VMEM((S5_LANES, S5_LANES), F32)],
        compiler_params=_params(1),
    )(dy, u, hre, him, dxre, dxim, tmt, bp, d.reshape(1, D_MODEL))


def _ssm_out_fwd(y, gate, x, w_glu, b_glu, w_out, name):
    t = x.shape[0]
    rows = ROWS_FWD

    def body(y_ref, g_ref, x_ref, wg_ref, bg_ref, wo_ref, o_ref):
        z0 = _gelu(y_ref[...])
        s = _dot(z0.astype(BF16), wg_ref[...]) + bg_ref[...]
        gate_v = g_ref[...]
        a = (z0 * _sigmoid(s)) * (gate_v * _sigmoid(gate_v))
        o_ref[...] = x_ref[...] + _dot(a.astype(BF16), wo_ref[...])

    return pl.pallas_call(
        body, name=name, grid=(t // rows,),
        in_specs=[_row_spec(rows, D_MODEL)] * 3 + [_const_spec((D_MODEL, D_MODEL)), _const_spec((1, D_MODEL)),
                                                   _const_spec((D_MODEL, D_MODEL))],
        out_specs=_row_spec(rows, D_MODEL),
        out_shape=jax.ShapeDtypeStruct((t, D_MODEL), F32),
        compiler_params=_params(1),
    )(y, gate, x, w_glu, b_glu.reshape(1, D_MODEL), w_out)


def _ssm_out_bwd(dxn, y, gate, w_glu, b_glu, w_out, name):
    t = y.shape[0]
    rows = ROWS_BWD

    def body(dxn_ref, y_ref, g_ref, wg_ref, bg_ref, wo_ref, dy_ref, dg_ref, dwg_ref, dbg_ref, dwo_ref):
        @pl.when(pl.program_id(0) == 0)
        def _():
            dwo_ref[...] = jnp.zeros_like(dwo_ref)
            dwg_ref[...] = jnp.zeros_like(dwg_ref)
            dbg_ref[...] = jnp.zeros_like(dbg_ref)

        yv = y_ref[...]
        z0 = _gelu(yv)
        z0b = z0.astype(BF16)
        sg = _sigmoid(_dot(z0b, wg_ref[...]) + bg_ref[...])
        z = z0 * sg
        gate_v = g_ref[...]
        sgg = _sigmoid(gate_v)
        silu = gate_v * sgg
        dob = dxn_ref[...].astype(BF16)
        da = _dot_nt(dob, wo_ref[...])
        dwo_ref[...] += _dot_tn((z * silu).astype(BF16), dob)
        dz = da * silu
        dg_ref[...] = da * z * (sgg * (1.0 + gate_v * (1.0 - sgg)))
        ds = dz * z0 * (sg * (1.0 - sg))
        dsb = ds.astype(BF16)
        dz0 = dz * sg + _dot_nt(dsb, wg_ref[...])
        dwg_ref[...] += _dot_tn(z0b, dsb)
        dbg_ref[...] += jnp.sum(ds, axis=0, keepdims=True)
        dy_ref[...] = dz0 * _gelu_grad(yv)

    sq = _const_spec((D_MODEL, D_MODEL))
    vec = _const_spec((1, D_MODEL))
    return pl.pallas_call(
        body, name=name, grid=(t // rows,),
        in_specs=[_row_spec(rows, D_MODEL)] * 3 + [sq, vec, sq],
        out_specs=[_row_spec(rows, D_MODEL), _row_spec(rows, D_MODEL), sq, vec, sq],
        out_shape=[jax.ShapeDtypeStruct((t, D_MODEL), F32)] * 2 + [
            jax.ShapeDtypeStruct((D_MODEL, D_MODEL), F32), jax.ShapeDtypeStruct((1, D_MODEL), F32),
            jax.ShapeDtypeStruct((D_MODEL, D_MODEL), F32)],
        compiler_params=_params(1),
    )(dxn, y, gate, w_glu, b_glu.reshape(1, D_MODEL), w_out)


KV_LANES = GQA * ATTN_BLOCK


def _attn_bias(block_is_first):
    kj = lax.broadcasted_iota(jnp.int32, (2 * ATTN_BLOCK, ATTN_BLOCK), 0)
    qi = lax.broadcasted_iota(jnp.int32, (2 * ATTN_BLOCK, ATTN_BLOCK), 1)
    dist = qi + ATTN_BLOCK - kj
    valid = (dist >= 0) & (dist < ATTN_BLOCK) & (jnp.logical_not(block_is_first) | (kj >= ATTN_BLOCK))
    return jnp.tile(jnp.where(valid, 0.0, NEG_INF).astype(F32), (1, GQA))


def _head_cols(a_t, kvh):
    heads = range(kvh * GQA, (kvh + 1) * GQA)
    return jnp.concatenate([a_t[HEAD_DIM * h:HEAD_DIM * (h + 1)] for h in heads], axis=1)


def _head_rows(a_cols):
    stacked = jnp.concatenate([a_cols[:, ATTN_BLOCK * g:ATTN_BLOCK * (g + 1)] for g in range(GQA)], axis=0)
    return stacked.T


def _kv_rows(prev_ref, cur_ref, kvh):
    lanes = slice(HEAD_DIM * kvh, HEAD_DIM * (kvh + 1))
    return jnp.concatenate([prev_ref[:, lanes], cur_ref[:, lanes]], axis=0).astype(BF16)


def _kv_cols(prev_t, cur_t, kvh):
    rows = slice(HEAD_DIM * kvh, HEAD_DIM * (kvh + 1))
    return jnp.concatenate([prev_t[rows], cur_t[rows]], axis=1).astype(BF16)


def _attn_probs(kk, q_cols, sink_row, bias):
    s = _dot(kk, q_cols) * ATTN_SCALE + bias
    m = jnp.maximum(jnp.max(s, axis=0, keepdims=True), sink_row)
    p = jnp.exp(s - m)
    e_sink = jnp.exp(sink_row - m)
    inv = 1.0 / (jnp.sum(p, axis=0, keepdims=True) + e_sink)
    return p * inv, e_sink * inv


def _sink_cols(sinks):
    return jnp.repeat(sinks, ATTN_BLOCK).reshape(N_KV_HEADS, 1, KV_LANES)


def _attn_fwd(q, k, v, sinks, name):
    t = q.shape[0]
    nblk = t // ATTN_BLOCK

    def body(s_ref, q_ref, kc_ref, kp_ref, vc_ref, vp_ref, o_ref):
        bias = _attn_bias(pl.program_id(0) == 0)
        q_t = q_ref[...].T
        vp_t, vc_t = vp_ref[...].T, vc_ref[...].T
        for kvh in range(N_KV_HEADS):
            p, _ = _attn_probs(_kv_rows(kp_ref, kc_ref, kvh), _head_cols(q_t, kvh).astype(BF16), s_ref[kvh], bias)
            o_cols = _dot(_kv_cols(vp_t, vc_t, kvh), p.astype(BF16))
            o_ref[:, GQA * HEAD_DIM * kvh:GQA * HEAD_DIM * (kvh + 1)] = _head_rows(o_cols)

    cur = lambda i: (i, 0)
    prev = lambda i: (jnp.maximum(i - 1, 0), 0)
    return pl.pallas_call(
        body, name=name, grid=(nblk,),
        in_specs=[_const_spec((N_KV_HEADS, 1, KV_LANES)),
                  pl.BlockSpec((ATTN_BLOCK, Q_DIM), cur),
                  pl.BlockSpec((ATTN_BLOCK, KV_DIM), cur), pl.BlockSpec((ATTN_BLOCK, KV_DIM), prev),
                  pl.BlockSpec((ATTN_BLOCK, KV_DIM), cur), pl.BlockSpec((ATTN_BLOCK, KV_DIM), prev)],
        out_specs=pl.BlockSpec((ATTN_BLOCK, Q_DIM), cur),
        out_shape=jax.ShapeDtypeStruct((t, Q_DIM), F32),
        compiler_params=_params(1),
    )(_sink_cols(sinks), q, k, k, v, v)


def _attn_bwd(q, k, v, o, do, sinks, rope, name):
    t = q.shape[0]
    nblk = t // ATTN_BLOCK

    def body(s_ref, q_ref, o_ref, do_ref, kp_ref, kc_ref, vp_ref, vc_ref, cosq_ref, sinq_ref, cosk_ref, sinkey_ref,
             dq_ref, dk_ref, dv_ref, ds_ref, new_k, new_v, wait_k, wait_v):
        n = pl.program_id(0)

        @pl.when(n == 0)
        def _():
            ds_ref[...] = jnp.zeros_like(ds_ref)
            wait_k[...] = jnp.zeros_like(wait_k)
            wait_v[...] = jnp.zeros_like(wait_v)

        @pl.when(n < nblk)
        def _():
            bias = _attn_bias(n == 0)
            q_t, o_t, do_t = q_ref[...].T, o_ref[...].T, do_ref[...].T
            kp_t, kc_t = kp_ref[...].T, kc_ref[...].T
            for kvh in range(N_KV_HEADS):
                q_cols = _head_cols(q_t, kvh).astype(BF16)
                do_cols = _head_cols(do_t, kvh)
                delta = jnp.sum(do_cols * _head_cols(o_t, kvh), axis=0, keepdims=True)
                do_cols = do_cols.astype(BF16)
                p, p_sink = _attn_probs(_kv_rows(kp_ref, kc_ref, kvh), q_cols, s_ref[kvh], bias)
                dp = _dot(_kv_rows(vp_ref, vc_ref, kvh), do_cols)
                ds = (p * (dp - delta) * ATTN_SCALE).astype(BF16)
                lanes = slice(GQA * HEAD_DIM * kvh, GQA * HEAD_DIM * (kvh + 1))
                dq_ref[:, lanes] = _head_rows(_dot(_kv_cols(kp_t, kc_t, kvh), ds))
                head = slice(HEAD_DIM * kvh, HEAD_DIM * (kvh + 1))
                new_k[:, head] = _dot_nt(ds, q_cols)
                new_v[:, head] = _dot_nt(p.astype(BF16), do_cols)
                ds_ref[kvh] += -(p_sink * delta)
            dq_ref[...] = _rope_apply(dq_ref[...], cosq_ref[...], sinq_ref[...], -1.0)

        @pl.when(n == nblk)
        def _():
            new_k[...] = jnp.zeros_like(new_k)
            new_v[...] = jnp.zeros_like(new_v)

        dk_ref[...] = _rope_apply(wait_k[...] + new_k[:ATTN_BLOCK], cosk_ref[...], sinkey_ref[...], -1.0)
        dv_ref[...] = wait_v[...] + new_v[:ATTN_BLOCK]
        wait_k[...] = new_k[ATTN_BLOCK:]
        wait_v[...] = new_v[ATTN_BLOCK:]

    cur = lambda i: (jnp.minimum(i, nblk - 1), 0)
    prev = lambda i: (jnp.maximum(i - 1, 0), 0)
    qs = lambda f: pl.BlockSpec((ATTN_BLOCK, Q_DIM), f)
    ks = lambda f: pl.BlockSpec((ATTN_BLOCK, KV_DIM), f)
    sink_spec = _const_spec((N_KV_HEADS, 1, KV_LANES))
    return pl.pallas_call(
        body, name=name, grid=(nblk + 1,),
        in_specs=[sink_spec, qs(cur), qs(cur), qs(cur), ks(prev), ks(cur), ks(prev), ks(cur),
                  ks(cur), ks(cur), ks(prev), ks(prev)],
        out_specs=[qs(cur), ks(prev), ks(prev), sink_spec],
        out_shape=[jax.ShapeDtypeStruct((t, Q_DIM), F32), jax.ShapeDtypeStruct((t, KV_DIM), F32),
                   jax.ShapeDtypeStruct((t, KV_DIM), F32), jax.ShapeDtypeStruct((N_KV_HEADS, 1, KV_LANES), F32)],
        scratch_shapes=[pltpu.VMEM((2 * ATTN_BLOCK, KV_DIM), F32), pltpu.VMEM((2 * ATTN_BLOCK, KV_DIM), F32),
                        pltpu.VMEM((ATTN_BLOCK, KV_DIM), F32), pltpu.VMEM((ATTN_BLOCK, KV_DIM), F32)],
        compiler_params=_params(1),
    )(_sink_cols(sinks), q, o, do, k, k, v, v, rope[0], rope[1], rope[0], rope[1])


def _attn_out_fwd(o, gate, x, w_out, name):
    t = x.shape[0]
    rows = ROWS_FWD

    def body(o_ref, g_ref, x_ref, wo_ref, xn_ref):
        gate_v = g_ref[...]
        a = o_ref[...] * (gate_v * _sigmoid(gate_v))
        xn_ref[...] = x_ref[...] + _dot(a.astype(BF16), wo_ref[...])

    return pl.pallas_call(
        body, name=name, grid=(t // rows,),
        in_specs=[_row_spec(rows, D_MODEL)] * 3 + [_const_spec((D_MODEL, D_MODEL))],
        out_specs=_row_spec(rows, D_MODEL),
        out_shape=jax.ShapeDtypeStruct((t, D_MODEL), F32),
        compiler_params=_params(1),
    )(o, gate, x, w_out)


def _attn_out_bwd(dxn, o, gate, w_out, name):
    t = o.shape[0]
    rows = ROWS_BWD

    def body(dxn_ref, o_ref, g_ref, wo_ref, do_ref, dg_ref, dwo_ref):
        @pl.when(pl.program_id(0) == 0)
        def _():
            dwo_ref[...] = jnp.zeros_like(dwo_ref)

        gate_v, ov = g_ref[...], o_ref[...]
        sgg = _sigmoid(gate_v)
        silu = gate_v * sgg
        dob = dxn_ref[...].astype(BF16)
        da = _dot_nt(dob, wo_ref[...])
        dwo_ref[...] += _dot_tn((ov * silu).astype(BF16), dob)
        do_ref[...] = da * silu
        dg_ref[...] = da * ov * (sgg * (1.0 + gate_v * (1.0 - sgg)))

    sq = _const_spec((D_MODEL, D_MODEL))
    return pl.pallas_call(
        body, name=name, grid=(t // rows,),
        in_specs=[_row_spec(rows, D_MODEL)] * 3 + [sq],
        out_specs=[_row_spec(rows, D_MODEL), _row_spec(rows, D_MODEL), sq],
        out_shape=[jax.ShapeDtypeStruct((t, D_MODEL), F32)] * 2 + [jax.ShapeDtypeStruct((D_MODEL, D_MODEL), F32)],
        compiler_params=_params(1),
    )(dxn, o, gate, w_out)


def _loss_head(x, norm, target, name):
    t = x.shape[0]
    rows = ROWS_FWD

    def body(x_ref, n_ref, t_ref, loss_ref, dx_ref, dn_ref):
        i = pl.program_id(0)
        xv = x_ref[...]
        rstd = lax.rsqrt(jnp.mean(xv * xv, axis=-1, keepdims=True) + NORM_EPS)
        xhat = xv * rstd
        err = xhat * n_ref[...] - t_ref[...]
        part = 0.5 * jnp.sum(jnp.mean(err * err, axis=-1, keepdims=True), axis=0, keepdims=True)
        dy = err * (1.0 / D_MODEL)
        dn = jnp.sum(dy * xhat, axis=0, keepdims=True)
        dxhat = dy * n_ref[...]
        dx_ref[...] = rstd * (dxhat - xhat * jnp.mean(dxhat * xhat, axis=-1, keepdims=True))

        @pl.when(i == 0)
        def _():
            loss_ref[...] = jnp.zeros((8, 128), F32) + part
            dn_ref[...] = dn

        @pl.when(i > 0)
        def _():
            loss_ref[...] += part
            dn_ref[...] += dn

    return pl.pallas_call(
        body, name=name, grid=(t // rows,),
        in_specs=[_row_spec(rows, D_MODEL), _const_spec((1, D_MODEL)), _row_spec(rows, D_MODEL)],
        out_specs=[_const_spec((8, 128)), _row_spec(rows, D_MODEL), _const_spec((1, D_MODEL))],
        out_shape=[jax.ShapeDtypeStruct((8, 128), F32), jax.ShapeDtypeStruct((t, D_MODEL), F32),
                   jax.ShapeDtypeStruct((1, D_MODEL), F32)],
        compiler_params=_params(1),
    )(x, norm.reshape(1, D_MODEL), target)


N_CHIPS = 4
N_CORES = 2
CHIP_FLIPS = ((0, 1), (1, 0), (1, 1))
ICI_CHUNKS = 2
D2D_CHUNKS = 8


def _n_chunks(rows, dtype, most):
    unit = 16 if dtype == BF16 else 8
    return max(n for n in range(1, most + 1) if rows % n == 0 and (rows // n) % unit == 0)


def _chunks_of(arrays, most):
    out = []
    for a in arrays:
        n = _n_chunks(a.shape[-2], a.dtype, most)
        out.append((n, a.shape[-2] // n))
    return out


def _exchange_chips(sends, per_dest, name):
    n = len(sends)
    chunking = _chunks_of(sends, ICI_CHUNKS)

    def body(*refs):
        send_refs, recv_refs, sems = refs[:n], refs[n:2 * n], refs[2 * n:]
        x, y, c = lax.axis_index("x"), lax.axis_index("y"), lax.axis_index("c")
        me = 2 * x + y

        def peer(k):
            fx, fy = CHIP_FLIPS[k]
            px, py = x + fx - 2 * x * fx, y + fy - 2 * y * fy
            return (px, py, c), 2 * px + py

        started, arriving, own = [], [], []
        for a in range(n):
            send_sems, recv_sems, local_sems = sems[3 * a:3 * a + 3]
            chunks, chunk_rows = chunking[a]
            for j in range(chunks):
                part = pl.ds(j * chunk_rows, chunk_rows)
                src = lambda number: send_refs[a].at[number, part] if per_dest else send_refs[a].at[part]
                for k in range(len(CHIP_FLIPS)):
                    to, to_number = peer(k)
                    remote = lambda landing: pltpu.make_async_remote_copy(
                        src_ref=src(to_number), dst_ref=recv_refs[a].at[landing, part],
                        send_sem=send_sems.at[k, j], recv_sem=recv_sems.at[k, j],
                        device_id=to, device_id_type=pl.DeviceIdType.MESH)
                    started.append(remote(me))
                    arriving.append(remote(to_number))
                own.append(pltpu.make_async_copy(src(me), recv_refs[a].at[me, part], local_sems.at[j]))
        for cp in started + own:
            cp.start()
        for out, arrival in zip(started, arriving):
            out.wait_send()
            arrival.wait_recv()
        for cp in own:
            cp.wait()

    hbm = pl.BlockSpec(memory_space=pltpu.HBM)
    scratch = []
    for chunks, _ in chunking:
        scratch += [pltpu.SemaphoreType.DMA((len(CHIP_FLIPS), chunks)), pltpu.SemaphoreType.DMA((len(CHIP_FLIPS), chunks)),
                    pltpu.SemaphoreType.DMA((chunks,))]
    return pl.pallas_call(
        body, name=name, in_specs=[hbm] * n, out_specs=[hbm] * n,
        out_shape=[jax.ShapeDtypeStruct((N_CHIPS,) + a.shape[-2:], a.dtype) for a in sends],
        scratch_shapes=scratch,
    )(*sends)


def _swap_cores(sends, per_dest, name):
    n = len(sends)
    chunking = _chunks_of(sends, D2D_CHUNKS)

    def body(*refs):
        send_refs, got_refs, sems = refs[:n], refs[n:2 * n], refs[2 * n:]
        c = lax.axis_index("c")
        sibling = (lax.axis_index("x"), lax.axis_index("y"), 1 - c)
        copies = []
        for a in range(n):
            chunks, chunk_rows = chunking[a]
            for j in range(chunks):
                part = pl.ds(j * chunk_rows, chunk_rows)
                copies.append(pltpu.make_async_remote_copy(
                    src_ref=send_refs[a].at[1 - c, part] if per_dest else send_refs[a].at[part],
                    dst_ref=got_refs[a].at[part], send_sem=sems[2 * a].at[j], recv_sem=sems[2 * a + 1].at[j],
                    device_id=sibling, device_id_type=pl.DeviceIdType.MESH))
        for cp in copies:
            cp.start()
        for cp in copies:
            cp.wait()

    hbm = pl.BlockSpec(memory_space=pltpu.HBM)
    scratch = []
    for chunks, _ in chunking:
        scratch += [pltpu.SemaphoreType.DMA((chunks,)), pltpu.SemaphoreType.DMA((chunks,))]
    return pl.pallas_call(
        body, name=name, in_specs=[hbm] * n, out_specs=[hbm] * n,
        out_shape=[jax.ShapeDtypeStruct(a.shape[-2:], a.dtype) for a in sends],
        scratch_shapes=scratch,
    )(*sends)


def _by_core(mine, other):
    first = lax.axis_index("c") == 0
    return jnp.stack([jnp.where(first, mine, other), jnp.where(first, other, mine)])


def _all_gather(arrays, name):
    by_chip = [r.reshape(-1, r.shape[-1]) for r in _exchange_chips(arrays, False, name + "_chips")]
    others = _swap_cores(by_chip, False, name + "_cores")
    return [_by_core(m, o).reshape((N_CORES, N_CHIPS) + a.shape) for m, o, a in zip(by_chip, others, arrays)]


def _sum_core(send, got, out_dtype, name):
    _, n, cols = send.shape
    rows = min(n, 256)
    while n % rows:
        rows -= 16

    def body(c_ref, keep_ref, got_ref, o_ref):
        o_ref[...] = (keep_ref[...].astype(F32) + got_ref[...].astype(F32)).astype(out_dtype)

    return pl.pallas_call(
        body, name=name, out_shape=jax.ShapeDtypeStruct((n, cols), out_dtype),
        grid_spec=pltpu.PrefetchScalarGridSpec(
            num_scalar_prefetch=1, grid=(n // rows,),
            in_specs=[pl.BlockSpec((None, rows, cols), lambda i, c: (c[0], i, 0)),
                      pl.BlockSpec((rows, cols), lambda i, c: (i, 0))],
            out_specs=pl.BlockSpec((rows, cols), lambda i, c: (i, 0))),
        compiler_params=_params(1),
    )(lax.axis_index("c").astype(jnp.int32).reshape(1), send, got)


def _sum_parts(parts, out_dtype, name):
    n, cols = parts[0].shape
    rows = min(n, 256)
    while n % rows:
        rows -= 16

    def body(*refs):
        acc = refs[0][...].astype(F32)
        for ref in refs[1:-1]:
            acc = acc + ref[...].astype(F32)
        refs[-1][...] = acc.astype(out_dtype)

    return pl.pallas_call(
        body, name=name, grid=(n // rows,),
        in_specs=[_row_spec(rows, cols)] * len(parts),
        out_specs=_row_spec(rows, cols),
        out_shape=jax.ShapeDtypeStruct((n, cols), out_dtype),
        compiler_params=_params(1),
    )(*parts)


def _reduce_scatter(sends, wire_dtypes, name):
    halves = [s.reshape(N_CORES, N_CHIPS * s.shape[2], s.shape[3]) for s in sends]
    gots = _swap_cores(halves, True, name + "_cores")
    sums = [_sum_core(h, g, dt, "%s_core_sum%d" % (name, i)).reshape((N_CHIPS,) + s.shape[2:])
            for i, (h, g, dt, s) in enumerate(zip(halves, gots, wire_dtypes, sends))]
    return _exchange_chips(sums, True, name + "_chips")


def _adamw(parts, w, m, v, name):
    n, cols = w.shape
    k = parts.shape[0]
    rows = min(n, 256)
    while n % rows:
        rows -= 8
    c1 = 1.0 - ADAM_B1 ** ADAM_STEP
    c2 = 1.0 - ADAM_B2 ** ADAM_STEP

    def body(p_ref, w_ref, m_ref, v_ref, g_ref, d_ref, nm_ref, nv_ref):
        g = p_ref[0].astype(F32)
        for s in range(1, k):
            g = g + p_ref[s].astype(F32)
        nm = ADAM_B1 * m_ref[...] + (1.0 - ADAM_B1) * g
        nv = ADAM_B2 * v_ref[...] + (1.0 - ADAM_B2) * (g * g)
        g_ref[...] = g
        nm_ref[...] = nm
        nv_ref[...] = nv
        d_ref[...] = -ADAM_LR * ((nm / c1) / (jnp.sqrt(nv / c2) + ADAM_EPS) + ADAM_WD * w_ref[...])

    blk = _row_spec(rows, cols)
    return pl.pallas_call(
        body, name=name, grid=(n // rows,),
        in_specs=[pl.BlockSpec((k, rows, cols), lambda i: (0, i, 0)), blk, blk, blk],
        out_specs=[blk] * 4,
        out_shape=[jax.ShapeDtypeStruct((n, cols), F32)] * 4,
        compiler_params=_params(1),
    )(parts, w, m, v)


SSM_KEYS = ("norm", "w_in", "a_re", "a_im", "log_step", "b_re", "b_im", "c_re", "c_im", "d", "w_glu", "b_glu", "w_out")
ATTN_KEYS = ("norm", "w_in", "sinks", "w_out")
LAYER_KEYS = (SSM_KEYS, ATTN_KEYS, SSM_KEYS, ATTN_KEYS)
BIG_KEYS = ("w_in", "w_glu", "w_out")
ATTN_SPLITS = (Q_DIM, KV_DIM, KV_DIM, D_MODEL)


def _rope_tables(t):
    pos = jnp.arange(t, dtype=F32)
    inv_freq = ROPE_THETA ** (-jnp.arange(0, HEAD_DIM, 2, dtype=F32) / HEAD_DIM)
    ang = pos[:, None] * inv_freq[None, :]
    cos, sin = jnp.cos(ang), jnp.sin(ang)
    return jnp.tile(jnp.concatenate([cos, cos], axis=1), (1, 2)), jnp.tile(jnp.concatenate([-sin, sin], axis=1), (1, 2))


def _ssm_layer_fwd(i, x, p, w):
    tag = "l%d_" % i
    mats, mats_vjp = jax.vjp(_s5_matrices, p["a_re"], p["a_im"], p["log_step"], p["b_re"], p["b_im"], p["c_re"], p["c_im"])
    kern, cpt, bpt, ar, ai = mats
    tm = _s5_toeplitz(kern)
    mb = dict(tm=tm.astype(BF16), tmt=jnp.swapaxes(tm, 1, 2).astype(BF16), cpt=cpt.astype(BF16),
              cp=jnp.swapaxes(cpt, 1, 2).astype(BF16), bpt=bpt.astype(BF16), bp=jnp.swapaxes(bpt, 1, 2).astype(BF16))
    u, gate = _inproj_fwd(x, p["norm"], w["w_in"], (D_MODEL, D_MODEL), None, tag + "inproj_fwd")
    xre, xim = _s5_project(u, mb["bpt"], tag + "s5_block_inputs")
    hre, him = _s5_scan_fwd(xre, xim, ar, ai, tag + "s5_scan_fwd")
    y = _s5_outputs(u, hre, him, mb["tm"], mb["cpt"], p["d"], tag + "s5_outputs")
    xn = _ssm_out_fwd(y, gate, x, w["w_glu"], p["b_glu"], w["w_out"], tag + "out_fwd")
    return xn, (x, u, gate, y, hre, him, mb, ar, ai, mats_vjp)


def _ssm_layer_bwd(i, dxn, saved, p, w):
    tag = "l%d_" % i
    x, u, gate, y, hre, him, mb, ar, ai, mats_vjp = saved
    dy, dgate, dw_glu, db_glu, dw_out = _ssm_out_bwd(dxn, y, gate, w["w_glu"], p["b_glu"], w["w_out"], tag + "out_bwd")
    dhre, dhim = _s5_project(dy, mb["cp"], tag + "s5_state_grads")
    dxre, dxim, dar, dai = _s5_scan_bwd(dhre, dhim, hre, him, ar, ai, tag + "s5_scan_bwd")
    du, dk, dcpt, dbpt, dd = _s5_backward(dy, u, hre, him, dxre, dxim, mb["tmt"], mb["bp"], p["d"], tag + "s5_backward")
    dk = dk.reshape(SSM_GROUPS, S5_BLOCK, SSM_GROUP, SSM_GROUP)
    da_re, da_im, dlog_step, db_re, db_im, dc_re, dc_im = mats_vjp((dk, dcpt, dbpt, dar, dai))
    dx, dw_in, dnorm = _inproj_bwd(x, p["norm"], w["w_in"], [du, dgate], dxn, tag + "inproj_bwd")
    grads = dict(norm=dnorm.reshape(D_MODEL), w_in=dw_in, a_re=da_re, a_im=da_im, log_step=dlog_step, b_re=db_re,
                 b_im=db_im, c_re=dc_re, c_im=dc_im, d=dd.reshape(D_MODEL), w_glu=dw_glu, b_glu=db_glu.reshape(D_MODEL),
                 w_out=dw_out)
    return dx, grads


def _attn_layer_fwd(i, x, p, w, rope):
    tag = "l%d_" % i
    q, k, v, gate = _inproj_fwd(x, p["norm"], w["w_in"], ATTN_SPLITS, rope, tag + "inproj_fwd")
    o = _attn_fwd(q, k, v, p["sinks"], tag + "attn_fwd")
    xn = _attn_out_fwd(o, gate, x, w["w_out"], tag + "out_fwd")
    return xn, (x, q, k, v, gate, o)


def _attn_layer_bwd(i, dxn, saved, p, w, rope):
    tag = "l%d_" % i
    x, q, k, v, gate, o = saved
    do, dgate, dw_out = _attn_out_bwd(dxn, o, gate, w["w_out"], tag + "out_bwd")
    dq, dk, dv, dsinks = _attn_bwd(q, k, v, o, do, p["sinks"], rope, tag + "attn_bwd")
    dx, dw_in, dnorm = _inproj_bwd(x, p["norm"], w["w_in"], [dq, dk, dv, dgate], dxn, tag + "inproj_bwd")
    return dx, dict(norm=dnorm.reshape(D_MODEL), w_in=dw_in, sinks=dsinks.reshape(N_Q_HEADS, ATTN_BLOCK).sum(axis=1), w_out=dw_out)


def _local_step(x, target, small, big):
    rope = _rope_tables(x.shape[0])
    saved = []
    for i in range(4):
        if i % 2 == 0:
            x, s = _ssm_layer_fwd(i, x, small[i], big[i])
        else:
            x, s = _attn_layer_fwd(i, x, small[i], big[i], rope)
        saved.append(s)
    loss, dx, dfinal = _loss_head(x, small[4]["norm"], target, "loss_head")
    grads = [None] * 4 + [dict(norm=dfinal.reshape(D_MODEL))]
    for i in (3, 2, 1, 0):
        if i % 2 == 0:
            dx, grads[i] = _ssm_layer_bwd(i, dx, saved[i], small[i], big[i])
        else:
            dx, grads[i] = _attn_layer_bwd(i, dx, saved[i], small[i], big[i], rope)
    return loss[0, 0], dx, grads


def _owner_major(key, g):
    if key == "w_in":
        return g.reshape(D_MODEL, N_CHIPS, N_CORES, -1).transpose(2, 1, 0, 3)
    return g.reshape(N_CHIPS, N_CORES, -1, D_MODEL).transpose(1, 0, 2, 3)


def _from_gathered(key, a):
    if key == "w_in":
        return a.transpose(2, 1, 0, 3).reshape(D_MODEL, N_DEV * a.shape[3])
    return a.transpose(1, 0, 2, 3).reshape(N_DEV * a.shape[2], a.shape[3])


SMALL_ROWS = 72


def _pad_rows(flat):
    n = flat.shape[0]
    rows = N_DEV * SMALL_ROWS
    assert n <= rows * D_MODEL
    return jnp.pad(flat, (0, rows * D_MODEL - n)).reshape(rows, D_MODEL)


def kernel(*args):
    names = ["x"]
    layer_names = []
    for i, keys in enumerate(LAYER_KEYS):
        layer_names += ["l%d_%s" % (i, k) for k in keys]
    layer_names.append("final_norm")
    names += layer_names + ["loss_target"] + ["m_" + n for n in layer_names] + ["v_" + n for n in layer_names]
    given = dict(zip(names, args))
    big_names = [n for n in layer_names if n.split("_", 1)[1] in BIG_KEYS]
    small_names = [n for n in layer_names if n not in big_names]

    families, offsets = {}, {}
    for n in big_names:
        family = families.setdefault(given[n].shape[1], [])
        offsets[n] = sum(given[other].shape[0] for other in family)
        family.append(n)
    families = list(families.values())
    stack = lambda pre, family: jnp.concatenate([given[pre + n] for n in family], axis=0)
    rows_of = lambda a, n: a[..., offsets[n]:offsets[n] + given[n].shape[0], :]

    gathered = _all_gather([stack("", family).astype(BF16) for family in families], "gather_weights")
    big = [dict() for _ in range(4)]
    for family, g in zip(families, gathered):
        for n in family:
            layer, key = int(n[1]), n.split("_", 1)[1]
            big[layer][key] = _from_gathered(key, rows_of(g, n))
    small = [dict() for _ in range(5)]
    for n in small_names:
        if n == "final_norm":
            small[4]["norm"] = given[n]
        else:
            small[int(n[1])][n.split("_", 1)[1]] = given[n]

    loss, dx, grads = _local_step(given["x"][0], given["loss_target"][0], small, big)
    loss = lax.psum(loss, ("x", "y", "c"))

    def grad_of(n):
        return grads[4]["norm"] if n == "final_norm" else grads[int(n[1])][n.split("_", 1)[1]]

    flat = lambda f: _pad_rows(jnp.concatenate([f(n).reshape(-1) for n in small_names]))
    sends = [jnp.concatenate([_owner_major(n.split("_", 1)[1], grad_of(n)) for n in family], axis=2) for family in families]
    sends.append(flat(grad_of).reshape(N_CORES, N_CHIPS, SMALL_ROWS, D_MODEL))
    parts = _reduce_scatter(sends, [BF16] * len(families) + [F32], "scatter_grads")

    outs = {}
    tags = ("grad_", "delta_", "new_m_", "new_v_")
    for i, family in enumerate(families):
        results = _adamw(parts[i], stack("", family), stack("m_", family), stack("v_", family), "adamw_matrices%d" % i)
        for tag, a in zip(tags, results):
            for n in family:
                outs[tag + n] = rows_of(a, n)

    my_slice = _sum_parts([parts[-1][s] for s in range(N_CHIPS)], F32, "sum_small_grads")
    g_all = _all_gather([my_slice], "gather_small_grads")[0].reshape(1, N_DEV * SMALL_ROWS, D_MODEL)
    results = _adamw(g_all, flat(lambda n: given[n]), flat(lambda n: given["m_" + n]), flat(lambda n: given["v_" + n]),
                     "adamw_small")
    for tag, a in zip(tags, results):
        a_flat, at = a.reshape(-1), 0
        for n in small_names:
            outs[tag + n] = a_flat[at:at + given[n].size].reshape(given[n].shape)
            at += given[n].size
    result = [loss, dx[None]]
    for tag in ("grad_", "delta_", "new_m_", "new_v_"):
        result += [outs[tag + n] for n in layer_names]
    return tuple(result)
```

```python
import functools
import math

import jax
import jax.numpy as jnp
from jax import lax
from jax.experimental import pallas as pl
from jax.experimental.pallas import tpu as pltpu

F32 = jnp.float32
BF16 = jnp.bfloat16

D_MODEL = 1024
SSM_GROUP = 16
SSM_GROUPS = D_MODEL // SSM_GROUP
SSM_STATE = 64
S5_BLOCK = 16
S5_LANES = S5_BLOCK * SSM_GROUP
HEAD_DIM = 64
N_Q_HEADS = 16
N_KV_HEADS = 2
GQA = N_Q_HEADS // N_KV_HEADS
Q_DIM = N_Q_HEADS * HEAD_DIM
KV_DIM = N_KV_HEADS * HEAD_DIM
ATTN_BLOCK = 128
ROPE_THETA = 10000.0
NORM_EPS = 1e-5
NEG_INF = -1e30
ATTN_SCALE = HEAD_DIM ** -0.5
N_DEV = 8

ADAM_LR = 0.001
ADAM_B1 = 0.9
ADAM_B2 = 0.999
ADAM_EPS = 1e-08
ADAM_WD = 0.01
ADAM_STEP = 10

VMEM_LIMIT = 56 * 1024 * 1024
ROWS_FWD = 512
ROWS_BWD = 512

NT = (((1,), (1,)), ((), ()))
TN = (((0,), (0,)), ((), ()))


def _params(n_grid):
    return pltpu.CompilerParams(dimension_semantics=("arbitrary",) * n_grid, vmem_limit_bytes=VMEM_LIMIT)


def _dot(a, b):
    return jnp.dot(a, b, preferred_element_type=F32)


def _dot_nt(a, b):
    return lax.dot_general(a, b, NT, preferred_element_type=F32)


def _dot_tn(a, b):
    return lax.dot_general(a, b, TN, preferred_element_type=F32)


def _sigmoid(x):
    return 1.0 / (1.0 + jnp.exp(-x))


_GELU_K = math.sqrt(2.0 / math.pi)


def _gelu(x):
    return x * (0.5 * (1.0 + jnp.tanh(_GELU_K * (x + 0.044715 * (x * x * x)))))


def _gelu_grad(x):
    t = jnp.tanh(_GELU_K * (x + 0.044715 * (x * x * x)))
    return 0.5 * (1.0 + t) + 0.5 * x * (1.0 - t * t) * (_GELU_K * (1.0 + 3.0 * 0.044715 * (x * x)))


def _row_spec(rows, cols):
    return pl.BlockSpec((rows, cols), lambda i: (i, 0))


def _const_spec(shape):
    zeros = (0,) * len(shape)
    return pl.BlockSpec(shape, lambda i: zeros, pipeline_mode=pl.Buffered(1))


def _rope_apply(t, cos, sin_signed, sign):
    lane = lax.broadcasted_iota(jnp.int32, (1, 128), 1)
    first_half = (lane % HEAD_DIM) < (HEAD_DIM // 2)
    out = []
    for j in range(t.shape[1] // 128):
        tj = t[:, 128 * j:128 * (j + 1)]
        partner = jnp.where(first_half, pltpu.roll(tj, 128 - HEAD_DIM // 2, 1), pltpu.roll(tj, HEAD_DIM // 2, 1))
        out.append(tj * cos + sign * (partner * sin_signed))
    return out[0] if len(out) == 1 else jnp.concatenate(out, axis=1)


def _inproj_fwd(x, norm, w, splits, rope, name):
    t = x.shape[0]
    n = w.shape[1]
    rows = ROWS_FWD

    def body(*refs):
        if rope is None:
            x_ref, n_ref, w_ref = refs[:3]
            outs = refs[3:]
        else:
            x_ref, n_ref, w_ref, cos_ref, sin_ref = refs[:5]
            outs = refs[5:]
        xv = x_ref[...]
        rstd = lax.rsqrt(jnp.mean(xv * xv, axis=-1, keepdims=True) + NORM_EPS)
        h = (xv * rstd) * n_ref[...]
        proj = _dot(h.astype(BF16), w_ref[...])
        off = 0
        for i, width in enumerate(splits):
            piece = proj[:, off:off + width]
            if rope is not None and i < 2:
                piece = _rope_apply(piece, cos_ref[...], sin_ref[...], 1.0)
            outs[i][...] = piece
            off += width

    in_specs = [_row_spec(rows, D_MODEL), _const_spec((1, D_MODEL)), _const_spec((D_MODEL, n))]
    args = [x, norm.reshape(1, D_MODEL), w]
    if rope is not None:
        in_specs += [_row_spec(rows, 128), _row_spec(rows, 128)]
        args += list(rope)
    return pl.pallas_call(
        body, name=name, grid=(t // rows,), in_specs=in_specs,
        out_specs=[_row_spec(rows, width) for width in splits],
        out_shape=[jax.ShapeDtypeStruct((t, width), F32) for width in splits],
        compiler_params=_params(1),
    )(*args)


def _inproj_bwd(x, norm, w, dpieces, dxn, name):
    t = x.shape[0]
    n = w.shape[1]
    rows = ROWS_BWD
    widths = [p.shape[1] for p in dpieces]
    k = len(dpieces)

    def body(*refs):
        x_ref, n_ref, w_ref, dxn_ref = refs[:4]
        d_refs = refs[4:4 + k]
        dx_ref, dw_ref, dn_ref = refs[4 + k:]
        @pl.when(pl.program_id(0) == 0)
        def _():
            dw_ref[...] = jnp.zeros_like(dw_ref)
            dn_ref[...] = jnp.zeros_like(dn_ref)

        xv = x_ref[...]
        rstd = lax.rsqrt(jnp.mean(xv * xv, axis=-1, keepdims=True) + NORM_EPS)
        xhat = xv * rstd
        h = xhat * n_ref[...]
        dproj = [r[...].astype(BF16) for r in d_refs]
        dproj = dproj[0] if k == 1 else jnp.concatenate(dproj, axis=1)
        dh = _dot_nt(dproj, w_ref[...])
        dw_ref[...] += _dot_tn(h.astype(BF16), dproj)
        dn_ref[...] += jnp.sum(dh * xhat, axis=0, keepdims=True)
        dxhat = dh * n_ref[...]
        dx_ref[...] = rstd * (dxhat - xhat * jnp.mean(dxhat * xhat, axis=-1, keepdims=True)) + dxn_ref[...]

    return pl.pallas_call(
        body, name=name, grid=(t // rows,),
        in_specs=[_row_spec(rows, D_MODEL), _const_spec((1, D_MODEL)), _const_spec((D_MODEL, n)),
                  _row_spec(rows, D_MODEL)] + [_row_spec(rows, width) for width in widths],
        out_specs=[_row_spec(rows, D_MODEL), _const_spec((D_MODEL, n)), _const_spec((1, D_MODEL))],
        out_shape=[jax.ShapeDtypeStruct((t, D_MODEL), F32), jax.ShapeDtypeStruct((D_MODEL, n), F32),
                   jax.ShapeDtypeStruct((1, D_MODEL), F32)],
        compiler_params=_params(1),
    )(x, norm.reshape(1, D_MODEL), w, dxn, *dpieces)


def _s5_matrices(a_re, a_im, log_step, b_re, b_im, c_re, c_im):
    r = S5_BLOCK
    step = jnp.exp(log_step)[:, None]
    lr, li = a_re * step, a_im * step
    k = jnp.arange(r + 1, dtype=F32)
    mag = jnp.exp(lr[:, None, :] * k[:, None])
    pr = mag * jnp.cos(li[:, None, :] * k[:, None])
    pi = mag * jnp.sin(li[:, None, :] * k[:, None])
    nr, ni = pr[:, 1] - 1.0, pi[:, 1]
    den = a_re * a_re + a_im * a_im
    qr, qi = (nr * a_re + ni * a_im) / den, (ni * a_re - nr * a_im) / den
    bbr = qr[..., None] * b_re - qi[..., None] * b_im
    bbi = qr[..., None] * b_im + qi[..., None] * b_re
    wr = c_re[:, None] * pr[:, :, None, :] - c_im[:, None] * pi[:, :, None, :]
    wi = c_re[:, None] * pi[:, :, None, :] + c_im[:, None] * pr[:, :, None, :]
    w = jnp.concatenate([wr, -wi], axis=-1)
    bb = jnp.concatenate([bbr, bbi], axis=1)
    kern = jnp.einsum("gxp,gpi->gxi", w[:, :r].reshape(SSM_GROUPS, S5_LANES, 2 * SSM_STATE), bb,
                      precision=lax.Precision.HIGHEST).reshape(SSM_GROUPS, r, SSM_GROUP, SSM_GROUP)
    cpt = w[:, 1:].reshape(SSM_GROUPS, S5_LANES, 2 * SSM_STATE)
    prs = jnp.swapaxes(pr[:, r - 1::-1][:, :r], 1, 2)[..., None]
    pis = jnp.swapaxes(pi[:, r - 1::-1][:, :r], 1, 2)[..., None]
    bp_re = prs * bbr[:, :, None, :] - pis * bbi[:, :, None, :]
    bp_im = prs * bbi[:, :, None, :] + pis * bbr[:, :, None, :]
    bpt = jnp.concatenate([bp_re, bp_im], axis=1).reshape(SSM_GROUPS, 2 * SSM_STATE, S5_LANES)
    ar = pr[:, r].reshape(1, SSM_GROUPS * SSM_STATE)
    ai = pi[:, r].reshape(1, SSM_GROUPS * SSM_STATE)
    return kern, cpt, bpt, ar, ai


def _s5_toeplitz(kern):
    r = S5_BLOCK
    cols = [jnp.pad(kern[:, :r - s], ((0, 0), (s, 0), (0, 0), (0, 0))) for s in range(r)]
    return jnp.stack(cols, axis=3).reshape(SSM_GROUPS, S5_LANES, S5_LANES)


S5_OCTET = 128 // SSM_GROUP
S5_STEPS = SSM_GROUPS // S5_OCTET


def _oct_spec(t):
    return pl.BlockSpec((t, 128), lambda j: (0, j))


def _state_spec(nb):
    return pl.BlockSpec((nb, S5_OCTET * SSM_STATE), lambda j: (0, j))


def _gmat_spec(a, b):
    return pl.BlockSpec((S5_OCTET, a, b), lambda j: (j, 0, 0))


def _block_rows(ref, nb):
    return [ref[pl.ds(r, nb, stride=S5_BLOCK), :] for r in range(S5_BLOCK)]


def _group_cols(pieces_t, g):
    return jnp.concatenate([p[SSM_GROUP * g:SSM_GROUP * (g + 1)] for p in pieces_t], axis=0)


def _state_cols(re_t, im_t, g):
    return jnp.concatenate([re_t[SSM_STATE * g:SSM_STATE * (g + 1)], im_t[SSM_STATE * g:SSM_STATE * (g + 1)]], axis=0)


def _s5_project(a, mat, name):
    t = a.shape[0]
    nb = t // S5_BLOCK

    def body(a_ref, m_ref, re_ref, im_ref):
        at = [p.T for p in _block_rows(a_ref, nb)]
        for pair in range(S5_OCTET // 2):
            xs = [_dot(m_ref[2 * pair + k], _group_cols(at, 2 * pair + k).astype(BF16)) for k in (0, 1)]
            lanes = slice(128 * pair, 128 * (pair + 1))
            re_ref[:, lanes] = jnp.concatenate([xs[0][:SSM_STATE], xs[1][:SSM_STATE]], axis=0).T
            im_ref[:, lanes] = jnp.concatenate([xs[0][SSM_STATE:], xs[1][SSM_STATE:]], axis=0).T

    return pl.pallas_call(
        body, name=name, grid=(S5_STEPS,),
        in_specs=[_oct_spec(t), _gmat_spec(2 * SSM_STATE, S5_LANES)],
        out_specs=[_state_spec(nb), _state_spec(nb)],
        out_shape=[jax.ShapeDtypeStruct((nb, SSM_GROUPS * SSM_STATE), F32)] * 2,
        compiler_params=_params(1),
    )(a, mat)


_SCAN_LANES = 1024


def _s5_scan_fwd(xre, xim, ar, ai, name):
    nb = xre.shape[0]
    col = pl.BlockSpec((nb, _SCAN_LANES), lambda j: (0, j))
    par = pl.BlockSpec((1, _SCAN_LANES), lambda j: (0, j))

    def body(xre_ref, xim_ref, ar_ref, ai_ref, hre_ref, him_ref):
        a_r, a_i = ar_ref[...], ai_ref[...]

        def step(b, carry):
            hr, hi = carry
            hre_ref[pl.ds(b, 1), :] = hr
            him_ref[pl.ds(b, 1), :] = hi
            xr, xi = xre_ref[pl.ds(b, 1), :], xim_ref[pl.ds(b, 1), :]
            return a_r * hr - a_i * hi + xr, a_r * hi + a_i * hr + xi

        zero = jnp.zeros((1, _SCAN_LANES), F32)
        lax.fori_loop(0, nb, step, (zero, zero))

    return pl.pallas_call(
        body, name=name, grid=(xre.shape[1] // _SCAN_LANES,),
        in_specs=[col, col, par, par], out_specs=[col, col],
        out_shape=[jax.ShapeDtypeStruct(xre.shape, F32)] * 2,
        compiler_params=_params(1),
    )(xre, xim, ar, ai)


def _s5_scan_bwd(dhre, dhim, hre, him, ar, ai, name):
    nb = dhre.shape[0]
    col = pl.BlockSpec((nb, _SCAN_LANES), lambda j: (0, j))
    par = pl.BlockSpec((1, _SCAN_LANES), lambda j: (0, j))

    def body(dhre_ref, dhim_ref, hre_ref, him_ref, ar_ref, ai_ref, dxre_ref, dxim_ref, dar_ref, dai_ref):
        a_r, a_i = ar_ref[...], ai_ref[...]

        def step(s, carry):
            gr, gi, dar, dai = carry
            b = nb - 1 - s
            dxre_ref[pl.ds(b, 1), :] = gr
            dxim_ref[pl.ds(b, 1), :] = gi
            hr, hi = hre_ref[pl.ds(b, 1), :], him_ref[pl.ds(b, 1), :]
            dar = dar + (hr * gr + hi * gi)
            dai = dai + (hr * gi - hi * gr)
            dr, di = dhre_ref[pl.ds(b, 1), :], dhim_ref[pl.ds(b, 1), :]
            return dr + (a_r * gr + a_i * gi), di + (a_r * gi - a_i * gr), dar, dai

        zero = jnp.zeros((1, _SCAN_LANES), F32)
        _, _, dar, dai = lax.fori_loop(0, nb, step, (zero, zero, zero, zero))
        dar_ref[...] = dar
        dai_ref[...] = dai

    return pl.pallas_call(
        body, name=name, grid=(dhre.shape[1] // _SCAN_LANES,),
        in_specs=[col, col, col, col, par, par], out_specs=[col, col, par, par],
        out_shape=[jax.ShapeDtypeStruct(dhre.shape, F32)] * 2 + [jax.ShapeDtypeStruct(ar.shape, F32)] * 2,
        compiler_params=_params(1),
    )(dhre, dhim, hre, him, ar, ai)


def _s5_outputs(u, hre, him, tm, cpt, d, name):
    t = u.shape[0]
    nb = t // S5_BLOCK

    def body(u_ref, hre_ref, him_ref, tm_ref, cpt_ref, d_ref, y_ref):
        u_rows = _block_rows(u_ref, nb)
        ut = [p.T for p in u_rows]
        hre_t, him_t = hre_ref[...].T, him_ref[...].T
        yts = []
        for g in range(S5_OCTET):
            yts.append(_dot(tm_ref[g], _group_cols(ut, g).astype(BF16))
                       + _dot(cpt_ref[g], _state_cols(hre_t, him_t, g).astype(BF16)))
        for r in range(S5_BLOCK):
            rows = jnp.concatenate([yt[SSM_GROUP * r:SSM_GROUP * (r + 1)] for yt in yts], axis=0)
            y_ref[pl.ds(r, nb, stride=S5_BLOCK), :] = rows.T + d_ref[...] * u_rows[r]

    return pl.pallas_call(
        body, name=name, grid=(S5_STEPS,),
        in_specs=[_oct_spec(t), _state_spec(nb), _state_spec(nb), _gmat_spec(S5_LANES, S5_LANES),
                  _gmat_spec(S5_LANES, 2 * SSM_STATE), _oct_spec(1)],
        out_specs=_oct_spec(t),
        out_shape=jax.ShapeDtypeStruct(u.shape, F32),
        compiler_params=_params(1),
    )(u, hre, him, tm, cpt, d.reshape(1, D_MODEL))


def _s5_backward(dy, u, hre, him, dxre, dxim, tmt, bp, d, name):
    t = u.shape[0]
    nb = t // S5_BLOCK

    def body(dy_ref, u_ref, hre_ref, him_ref, dxre_ref, dxim_ref, tmt_ref, bp_ref, d_ref,
             du_ref, dk_ref, dcpt_ref, dbpt_ref, dd_ref, dtm_scratch):
        dy_rows, u_rows = _block_rows(dy_ref, nb), _block_rows(u_ref, nb)
        dyt, ut = [p.T for p in dy_rows], [p.T for p in u_rows]
        hre_t, him_t = hre_ref[...].T, him_ref[...].T
        dxre_t, dxim_t = dxre_ref[...].T, dxim_ref[...].T
        duts = []
        for g in range(S5_OCTET):
            dyg, ug = _group_cols(dyt, g).astype(BF16), _group_cols(ut, g).astype(BF16)
            hg = _state_cols(hre_t, him_t, g).astype(BF16)
            dxg = _state_cols(dxre_t, dxim_t, g).astype(BF16)
            duts.append(_dot(tmt_ref[g], dyg) + _dot(bp_ref[g], dxg))
            dtm_scratch[...] = _dot_nt(dyg, ug)
            dk = dtm_scratch[:, :SSM_GROUP]
            for s in range(1, S5_BLOCK):
                below = dtm_scratch[SSM_GROUP * s:, SSM_GROUP * s:SSM_GROUP * (s + 1)]
                dk = dk + jnp.concatenate([below, jnp.zeros((SSM_GROUP * s, SSM_GROUP), F32)], axis=0)
            dk_ref[g] = dk
            dcpt_ref[g] = _dot_nt(dyg, hg)
            dbpt_ref[g] = _dot_nt(dxg, ug)
        dd = jnp.zeros((1, 128), F32)
        for r in range(S5_BLOCK):
            rows = jnp.concatenate([dut[SSM_GROUP * r:SSM_GROUP * (r + 1)] for dut in duts], axis=0)
            du_ref[pl.ds(r, nb, stride=S5_BLOCK), :] = rows.T + d_ref[...] * dy_rows[r]
            dd = dd + jnp.sum(dy_rows[r] * u_rows[r], axis=0, keepdims=True)
        dd_ref[...] = dd

    return pl.pallas_call(
        body, name=name, grid=(S5_STEPS,),
        in_specs=[_oct_spec(t), _oct_spec(t), _state_spec(nb), _state_spec(nb), _state_spec(nb), _state_spec(nb),
                  _gmat_spec(S5_LANES, S5_LANES), _gmat_spec(S5_LANES, 2 * SSM_STATE), _oct_spec(1)],
        out_specs=[_oct_spec(t), _gmat_spec(S5_LANES, SSM_GROUP), _gmat_spec(S5_LANES, 2 * SSM_STATE),
                   _gmat_spec(2 * SSM_STATE, S5_LANES), _oct_spec(1)],
        out_shape=[jax.ShapeDtypeStruct(u.shape, F32),
                   jax.ShapeDtypeStruct((SSM_GROUPS, S5_LANES, SSM_GROUP), F32),
                   jax.ShapeDtypeStruct((SSM_GROUPS, S5_LANES, 2 * SSM_STATE), F32),
                   jax.ShapeDtypeStruct((SSM_GROUPS, 2 * SSM_STATE, S5_LANES), F32),
                   jax.ShapeDtypeStruct((1, D_MODEL), F32)],
        scratch_shapes=[pltpu.VMEM((S5_LANES, S5_LANES), F32)],
        compiler_params=_params(1),
    )(dy, u, hre, him, dxre, dxim, tmt, bp, d.reshape(1, D_MODEL))


def _ssm_out_fwd(y, gate, x, w_glu, b_glu, w_out, name):
    t = x.shape[0]
    rows = ROWS_FWD

    def body(y_ref, g_ref, x_ref, wg_ref, bg_ref, wo_ref, o_ref):
        z0 = _gelu(y_ref[...])
        s = _dot(z0.astype(BF16), wg_ref[...]) + bg_ref[...]
        gate_v = g_ref[...]
        a = (z0 * _sigmoid(s)) * (gate_v * _sigmoid(gate_v))
        o_ref[...] = x_ref[...] + _dot(a.astype(BF16), wo_ref[...])

    return pl.pallas_call(
        body, name=name, grid=(t // rows,),
        in_specs=[_row_spec(rows, D_MODEL)] * 3 + [_const_spec((D_MODEL, D_MODEL)), _const_spec((1, D_MODEL)),
                                                   _const_spec((D_MODEL, D_MODEL))],
        out_specs=_row_spec(rows, D_MODEL),
        out_shape=jax.ShapeDtypeStruct((t, D_MODEL), F32),
        compiler_params=_params(1),
    )(y, gate, x, w_glu, b_glu.reshape(1, D_MODEL), w_out)


def _ssm_out_bwd(dxn, y, gate, w_glu, b_glu, w_out, name):
    t = y.shape[0]
    rows = ROWS_BWD

    def body(dxn_ref, y_ref, g_ref, wg_ref, bg_ref, wo_ref, dy_ref, dg_ref, dwg_ref, dbg_ref, dwo_ref):
        @pl.when(pl.program_id(0) == 0)
        def _():
            dwo_ref[...] = jnp.zeros_like(dwo_ref)
            dwg_ref[...] = jnp.zeros_like(dwg_ref)
            dbg_ref[...] = jnp.zeros_like(dbg_ref)

        yv = y_ref[...]
        z0 = _gelu(yv)
        z0b = z0.astype(BF16)
        sg = _sigmoid(_dot(z0b, wg_ref[...]) + bg_ref[...])
        z = z0 * sg
        gate_v = g_ref[...]
        sgg = _sigmoid(gate_v)
        silu = gate_v * sgg
        dob = dxn_ref[...].astype(BF16)
        da = _dot_nt(dob, wo_ref[...])
        dwo_ref[...] += _dot_tn((z * silu).astype(BF16), dob)
        dz = da * silu
        dg_ref[...] = da * z * (sgg * (1.0 + gate_v * (1.0 - sgg)))
        ds = dz * z0 * (sg * (1.0 - sg))
        dsb = ds.astype(BF16)
        dz0 = dz * sg + _dot_nt(dsb, wg_ref[...])
        dwg_ref[...] += _dot_tn(z0b, dsb)
        dbg_ref[...] += jnp.sum(ds, axis=0, keepdims=True)
        dy_ref[...] = dz0 * _gelu_grad(yv)

    sq = _const_spec((D_MODEL, D_MODEL))
    vec = _const_spec((1, D_MODEL))
    return pl.pallas_call(
        body, name=name, grid=(t // rows,),
        in_specs=[_row_spec(rows, D_MODEL)] * 3 + [sq, vec, sq],
        out_specs=[_row_spec(rows, D_MODEL), _row_spec(rows, D_MODEL), sq, vec, sq],
        out_shape=[jax.ShapeDtypeStruct((t, D_MODEL), F32)] * 2 + [
            jax.ShapeDtypeStruct((D_MODEL, D_MODEL), F32), jax.ShapeDtypeStruct((1, D_MODEL), F32),
            jax.ShapeDtypeStruct((D_MODEL, D_MODEL), F32)],
        compiler_params=_params(1),
    )(dxn, y, gate, w_glu, b_glu.reshape(1, D_MODEL), w_out)


KV_LANES = GQA * ATTN_BLOCK


def _attn_bias(block_is_first):
    kj = lax.broadcasted_iota(jnp.int32, (2 * ATTN_BLOCK, ATTN_BLOCK), 0)
    qi = lax.broadcasted_iota(jnp.int32, (2 * ATTN_BLOCK, ATTN_BLOCK), 1)
    dist = qi + ATTN_BLOCK - kj
    valid = (dist >= 0) & (dist < ATTN_BLOCK) & (jnp.logical_not(block_is_first) | (kj >= ATTN_BLOCK))
    return jnp.tile(jnp.where(valid, 0.0, NEG_INF).astype(F32), (1, GQA))


def _head_cols(a_t, kvh):
    heads = range(kvh * GQA, (kvh + 1) * GQA)
    return jnp.concatenate([a_t[HEAD_DIM * h:HEAD_DIM * (h + 1)] for h in heads], axis=1)


def _head_rows(a_cols):
    stacked = jnp.concatenate([a_cols[:, ATTN_BLOCK * g:ATTN_BLOCK * (g + 1)] for g in range(GQA)], axis=0)
    return stacked.T


def _kv_rows(prev_ref, cur_ref, kvh):
    lanes = slice(HEAD_DIM * kvh, HEAD_DIM * (kvh + 1))
    return jnp.concatenate([prev_ref[:, lanes], cur_ref[:, lanes]], axis=0).astype(BF16)


def _kv_cols(prev_t, cur_t, kvh):
    rows = slice(HEAD_DIM * kvh, HEAD_DIM * (kvh + 1))
    return jnp.concatenate([prev_t[rows], cur_t[rows]], axis=1).astype(BF16)


def _attn_probs(kk, q_cols, sink_row, bias):
    s = _dot(kk, q_cols) * ATTN_SCALE + bias
    m = jnp.maximum(jnp.max(s, axis=0, keepdims=True), sink_row)
    p = jnp.exp(s - m)
    e_sink = jnp.exp(sink_row - m)
    inv = 1.0 / (jnp.sum(p, axis=0, keepdims=True) + e_sink)
    return p * inv, e_sink * inv


def _sink_cols(sinks):
    return jnp.repeat(sinks, ATTN_BLOCK).reshape(N_KV_HEADS, 1, KV_LANES)


def _attn_fwd(q, k, v, sinks, name):
    t = q.shape[0]
    nblk = t // ATTN_BLOCK

    def body(s_ref, q_ref, kc_ref, kp_ref, vc_ref, vp_ref, o_ref):
        bias = _attn_bias(pl.program_id(0) == 0)
        q_t = q_ref[...].T
        vp_t, vc_t = vp_ref[...].T, vc_ref[...].T
        for kvh in range(N_KV_HEADS):
            p, _ = _attn_probs(_kv_rows(kp_ref, kc_ref, kvh), _head_cols(q_t, kvh).astype(BF16), s_ref[kvh], bias)
            o_cols = _dot(_kv_cols(vp_t, vc_t, kvh), p.astype(BF16))
            o_ref[:, GQA * HEAD_DIM * kvh:GQA * HEAD_DIM * (kvh + 1)] = _head_rows(o_cols)

    cur = lambda i: (i, 0)
    prev = lambda i: (jnp.maximum(i - 1, 0), 0)
    return pl.pallas_call(
        body, name=name, grid=(nblk,),
        in_specs=[_const_spec((N_KV_HEADS, 1, KV_LANES)),
                  pl.BlockSpec((ATTN_BLOCK, Q_DIM), cur),
                  pl.BlockSpec((ATTN_BLOCK, KV_DIM), cur), pl.BlockSpec((ATTN_BLOCK, KV_DIM), prev),
                  pl.BlockSpec((ATTN_BLOCK, KV_DIM), cur), pl.BlockSpec((ATTN_BLOCK, KV_DIM), prev)],
        out_specs=pl.BlockSpec((ATTN_BLOCK, Q_DIM), cur),
        out_shape=jax.ShapeDtypeStruct((t, Q_DIM), F32),
        compiler_params=_params(1),
    )(_sink_cols(sinks), q, k, k, v, v)


def _attn_bwd(q, k, v, o, do, sinks, rope, name):
    t = q.shape[0]
    nblk = t // ATTN_BLOCK

    def body(s_ref, q_ref, o_ref, do_ref, kp_ref, kc_ref, vp_ref, vc_ref, cosq_ref, sinq_ref, cosk_ref, sinkey_ref,
             dq_ref, dk_ref, dv_ref, ds_ref, new_k, new_v, wait_k, wait_v):
        n = pl.program_id(0)

        @pl.when(n == 0)
        def _():
            ds_ref[...] = jnp.zeros_like(ds_ref)
            wait_k[...] = jnp.zeros_like(wait_k)
            wait_v[...] = jnp.zeros_like(wait_v)

        @pl.when(n < nblk)
        def _():
            bias = _attn_bias(n == 0)
            q_t, o_t, do_t = q_ref[...].T, o_ref[...].T, do_ref[...].T
            kp_t, kc_t = kp_ref[...].T, kc_ref[...].T
            for kvh in range(N_KV_HEADS):
                q_cols = _head_cols(q_t, kvh).astype(BF16)
                do_cols = _head_cols(do_t, kvh)
                delta = jnp.sum(do_cols * _head_cols(o_t, kvh), axis=0, keepdims=True)
                do_cols = do_cols.astype(BF16)
                p, p_sink = _attn_probs(_kv_rows(kp_ref, kc_ref, kvh), q_cols, s_ref[kvh], bias)
                dp = _dot(_kv_rows(vp_ref, vc_ref, kvh), do_cols)
                ds = (p * (dp - delta) * ATTN_SCALE).astype(BF16)
                lanes = slice(GQA * HEAD_DIM * kvh, GQA * HEAD_DIM * (kvh + 1))
                dq_ref[:, lanes] = _head_rows(_dot(_kv_cols(kp_t, kc_t, kvh), ds))
                head = slice(HEAD_DIM * kvh, HEAD_DIM * (kvh + 1))
                new_k[:, head] = _dot_nt(ds, q_cols)
                new_v[:, head] = _dot_nt(p.astype(BF16), do_cols)
                ds_ref[kvh] += -(p_sink * delta)
            dq_ref[...] = _rope_apply(dq_ref[...], cosq_ref[...], sinq_ref[...], -1.0)

        @pl.when(n == nblk)
        def _():
            new_k[...] = jnp.zeros_like(new_k)
            new_v[...] = jnp.zeros_like(new_v)

        dk_ref[...] = _rope_apply(wait_k[...] + new_k[:ATTN_BLOCK], cosk_ref[...], sinkey_ref[...], -1.0)
        dv_ref[...] = wait_v[...] + new_v[:ATTN_BLOCK]
        wait_k[...] = new_k[ATTN_BLOCK:]
        wait_v[...] = new_v[ATTN_BLOCK:]

    cur = lambda i: (jnp.minimum(i, nblk - 1), 0)
    prev = lambda i: (jnp.maximum(i - 1, 0), 0)
    qs = lambda f: pl.BlockSpec((ATTN_BLOCK, Q_DIM), f)
    ks = lambda f: pl.BlockSpec((ATTN_BLOCK, KV_DIM), f)
    sink_spec = _const_spec((N_KV_HEADS, 1, KV_LANES))
    return pl.pallas_call(
        body, name=name, grid=(nblk + 1,),
        in_specs=[sink_spec, qs(cur), qs(cur), qs(cur), ks(prev), ks(cur), ks(prev), ks(cur),
                  ks(cur), ks(cur), ks(prev), ks(prev)],
        out_specs=[qs(cur), ks(prev), ks(prev), sink_spec],
        out_shape=[jax.ShapeDtypeStruct((t, Q_DIM), F32), jax.ShapeDtypeStruct((t, KV_DIM), F32),
                   jax.ShapeDtypeStruct((t, KV_DIM), F32), jax.ShapeDtypeStruct((N_KV_HEADS, 1, KV_LANES), F32)],
        scratch_shapes=[pltpu.VMEM((2 * ATTN_BLOCK, KV_DIM), F32), pltpu.VMEM((2 * ATTN_BLOCK, KV_DIM), F32),
                        pltpu.VMEM((ATTN_BLOCK, KV_DIM), F32), pltpu.VMEM((ATTN_BLOCK, KV_DIM), F32)],
        compiler_params=_params(1),
    )(_sink_cols(sinks), q, o, do, k, k, v, v, rope[0], rope[1], rope[0], rope[1])


def _attn_out_fwd(o, gate, x, w_out, name):
    t = x.shape[0]
    rows = ROWS_FWD

    def body(o_ref, g_ref, x_ref, wo_ref, xn_ref):
        gate_v = g_ref[...]
        a = o_ref[...] * (gate_v * _sigmoid(gate_v))
        xn_ref[...] = x_ref[...] + _dot(a.astype(BF16), wo_ref[...])

    return pl.pallas_call(
        body, name=name, grid=(t // rows,),
        in_specs=[_row_spec(rows, D_MODEL)] * 3 + [_const_spec((D_MODEL, D_MODEL))],
        out_specs=_row_spec(rows, D_MODEL),
        out_shape=jax.ShapeDtypeStruct((t, D_MODEL), F32),
        compiler_params=_params(1),
    )(o, gate, x, w_out)


def _attn_out_bwd(dxn, o, gate, w_out, name):
    t = o.shape[0]
    rows = ROWS_BWD

    def body(dxn_ref, o_ref, g_ref, wo_ref, do_ref, dg_ref, dwo_ref):
        @pl.when(pl.program_id(0) == 0)
        def _():
            dwo_ref[...] = jnp.zeros_like(dwo_ref)

        gate_v, ov = g_ref[...], o_ref[...]
        sgg = _sigmoid(gate_v)
        silu = gate_v * sgg
        dob = dxn_ref[...].astype(BF16)
        da = _dot_nt(dob, wo_ref[...])
        dwo_ref[...] += _dot_tn((ov * silu).astype(BF16), dob)
        do_ref[...] = da * silu
        dg_ref[...] = da * ov * (sgg * (1.0 + gate_v * (1.0 - sgg)))

    sq = _const_spec((D_MODEL, D_MODEL))
    return pl.pallas_call(
        body, name=name, grid=(t // rows,),
        in_specs=[_row_spec(rows, D_MODEL)] * 3 + [sq],
        out_specs=[_row_spec(rows, D_MODEL), _row_spec(rows, D_MODEL), sq],
        out_shape=[jax.ShapeDtypeStruct((t, D_MODEL), F32)] * 2 + [jax.ShapeDtypeStruct((D_MODEL, D_MODEL), F32)],
        compiler_params=_params(1),
    )(dxn, o, gate, w_out)


def _loss_head(x, norm, target, name):
    t = x.shape[0]
    rows = ROWS_FWD

    def body(x_ref, n_ref, t_ref, loss_ref, dx_ref, dn_ref):
        i = pl.program_id(0)
        xv = x_ref[...]
        rstd = lax.rsqrt(jnp.mean(xv * xv, axis=-1, keepdims=True) + NORM_EPS)
        xhat = xv * rstd
        err = xhat * n_ref[...] - t_ref[...]
        part = 0.5 * jnp.sum(jnp.mean(err * err, axis=-1, keepdims=True), axis=0, keepdims=True)
        dy = err * (1.0 / D_MODEL)
        dn = jnp.sum(dy * xhat, axis=0, keepdims=True)
        dxhat = dy * n_ref[...]
        dx_ref[...] = rstd * (dxhat - xhat * jnp.mean(dxhat * xhat, axis=-1, keepdims=True))

        @pl.when(i == 0)
        def _():
            loss_ref[...] = jnp.zeros((8, 128), F32) + part
            dn_ref[...] = dn

        @pl.when(i > 0)
        def _():
            loss_ref[...] += part
            dn_ref[...] += dn

    return pl.pallas_call(
        body, name=name, grid=(t // rows,),
        in_specs=[_row_spec(rows, D_MODEL), _const_spec((1, D_MODEL)), _row_spec(rows, D_MODEL)],
        out_specs=[_const_spec((8, 128)), _row_spec(rows, D_MODEL), _const_spec((1, D_MODEL))],
        out_shape=[jax.ShapeDtypeStruct((8, 128), F32), jax.ShapeDtypeStruct((t, D_MODEL), F32),
                   jax.ShapeDtypeStruct((1, D_MODEL), F32)],
        compiler_params=_params(1),
    )(x, norm.reshape(1, D_MODEL), target)


N_CHIPS = 4
N_CORES = 2
CHIP_FLIPS = ((0, 1), (1, 0), (1, 1))
ICI_CHUNKS = 2
D2D_CHUNKS = 8


def _n_chunks(rows, dtype, most):
    unit = 16 if dtype == BF16 else 8
    return max(n for n in range(1, most + 1) if rows % n == 0 and (rows // n) % unit == 0)


def _chunks_of(arrays, most):
    out = []
    for a in arrays:
        n = _n_chunks(a.shape[-2], a.dtype, most)
        out.append((n, a.shape[-2] // n))
    return out


def _exchange_chips(sends, per_dest, name):
    n = len(sends)
    chunking = _chunks_of(sends, ICI_CHUNKS)

    def body(*refs):
        send_refs, recv_refs, sems = refs[:n], refs[n:2 * n], refs[2 * n:]
        x, y, c = lax.axis_index("x"), lax.axis_index("y"), lax.axis_index("c")
        me = 2 * x + y

        def peer(k):
            fx, fy = CHIP_FLIPS[k]
            px, py = x + fx - 2 * x * fx, y + fy - 2 * y * fy
            return (px, py, c), 2 * px + py

        started, arriving, own = [], [], []
        for a in range(n):
            send_sems, recv_sems, local_sems = sems[3 * a:3 * a + 3]
            chunks, chunk_rows = chunking[a]
            for j in range(chunks):
                part = pl.ds(j * chunk_rows, chunk_rows)
                src = lambda number: send_refs[a].at[number, part] if per_dest else send_refs[a].at[part]
                for k in range(len(CHIP_FLIPS)):
                    to, to_number = peer(k)
                    remote = lambda landing: pltpu.make_async_remote_copy(
                        src_ref=src(to_number), dst_ref=recv_refs[a].at[landing, part],
                        send_sem=send_sems.at[k, j], recv_sem=recv_sems.at[k, j],
                        device_id=to, device_id_type=pl.DeviceIdType.MESH)
                    started.append(remote(me))
                    arriving.append(remote(to_number))
                own.append(pltpu.make_async_copy(src(me), recv_refs[a].at[me, part], local_sems.at[j]))
        for cp in started + own:
            cp.start()
        for out, arrival in zip(started, arriving):
            out.wait_send()
            arrival.wait_recv()
        for cp in own:
            cp.wait()

    hbm = pl.BlockSpec(memory_space=pltpu.HBM)
    scratch = []
    for chunks, _ in chunking:
        scratch += [pltpu.SemaphoreType.DMA((len(CHIP_FLIPS), chunks)), pltpu.SemaphoreType.DMA((len(CHIP_FLIPS), chunks)),
                    pltpu.SemaphoreType.DMA((chunks,))]
    return pl.pallas_call(
        body, name=name, in_specs=[hbm] * n, out_specs=[hbm] * n,
        out_shape=[jax.ShapeDtypeStruct((N_CHIPS,) + a.shape[-2:], a.dtype) for a in sends],
        scratch_shapes=scratch,
    )(*sends)


def _swap_cores(sends, per_dest, name):
    n = len(sends)
    chunking = _chunks_of(sends, D2D_CHUNKS)

    def body(*refs):
        send_refs, got_refs, sems = refs[:n], refs[n:2 * n], refs[2 * n:]
        c = lax.axis_index("c")
        sibling = (lax.axis_index("x"), lax.axis_index("y"), 1 - c)
        copies = []
        for a in range(n):
            chunks, chunk_rows = chunking[a]
            for j in range(chunks):
                part = pl.ds(j * chunk_rows, chunk_rows)
                copies.append(pltpu.make_async_remote_copy(
                    src_ref=send_refs[a].at[1 - c, part] if per_dest else send_refs[a].at[part],
                    dst_ref=got_refs[a].at[part], send_sem=sems[2 * a].at[j], recv_sem=sems[2 * a + 1].at[j],
                    device_id=sibling, device_id_type=pl.DeviceIdType.MESH))
        for cp in copies:
            cp.start()
        for cp in copies:
            cp.wait()

    hbm = pl.BlockSpec(memory_space=pltpu.HBM)
    scratch = []
    for chunks, _ in chunking:
        scratch += [pltpu.SemaphoreType.DMA((chunks,)), pltpu.SemaphoreType.DMA((chunks,))]
    return pl.pallas_call(
        body, name=name, in_specs=[hbm] * n, out_specs=[hbm] * n,
        out_shape=[jax.ShapeDtypeStruct(a.shape[-2:], a.dtype) for a in sends],
        scratch_shapes=scratch,
    )(*sends)


def _all_gather(arrays, name):
    by_chip = _exchange_chips(arrays, False, name + "_chips")
    others = _swap_cores([r.reshape(-1, r.shape[-1]) for r in by_chip], False, name + "_cores")
    return [(m, o.reshape(m.shape)) for m, o in zip(by_chip, others)]


def _in_device_order(mine, other, axis):
    first = lax.axis_index("c") == 0
    pieces = []
    for m, o in zip(mine, other):
        pieces += [jnp.where(first, m, o), jnp.where(first, o, m)]
    return jnp.concatenate(pieces, axis=axis)


def _sum_core(send, got, out_dtype, name):
    _, n, cols = send.shape
    rows = min(n, 256)
    while n % rows:
        rows -= 16

    def body(c_ref, keep_ref, got_ref, o_ref):
        o_ref[...] = (keep_ref[...].astype(F32) + got_ref[...].astype(F32)).astype(out_dtype)

    return pl.pallas_call(
        body, name=name, out_shape=jax.ShapeDtypeStruct((n, cols), out_dtype),
        grid_spec=pltpu.PrefetchScalarGridSpec(
            num_scalar_prefetch=1, grid=(n // rows,),
            in_specs=[pl.BlockSpec((None, rows, cols), lambda i, c: (c[0], i, 0)),
                      pl.BlockSpec((rows, cols), lambda i, c: (i, 0))],
            out_specs=pl.BlockSpec((rows, cols), lambda i, c: (i, 0))),
        compiler_params=_params(1),
    )(lax.axis_index("c").astype(jnp.int32).reshape(1), send, got)


def _sum_parts(parts, out_dtype, name):
    n, cols = parts[0].shape
    rows = min(n, 256)
    while n % rows:
        rows -= 16

    def body(*refs):
        acc = refs[0][...].astype(F32)
        for ref in refs[1:-1]:
            acc = acc + ref[...].astype(F32)
        refs[-1][...] = acc.astype(out_dtype)

    return pl.pallas_call(
        body, name=name, grid=(n // rows,),
        in_specs=[_row_spec(rows, cols)] * len(parts),
        out_specs=_row_spec(rows, cols),
        out_shape=jax.ShapeDtypeStruct((n, cols), out_dtype),
        compiler_params=_params(1),
    )(*parts)


def _reduce_scatter(sends, wire_dtypes, name):
    halves = [s.reshape(N_CORES, N_CHIPS * s.shape[2], s.shape[3]) for s in sends]
    gots = _swap_cores(halves, True, name + "_cores")
    sums = [_sum_core(h, g, dt, "%s_core_sum%d" % (name, i)).reshape((N_CHIPS,) + s.shape[2:])
            for i, (h, g, dt, s) in enumerate(zip(halves, gots, wire_dtypes, sends))]
    return _exchange_chips(sums, True, name + "_chips")


def _adamw(parts, w, m, v, name):
    n, cols = w.shape
    k = parts.shape[0]
    rows = min(n, 256)
    while n % rows:
        rows -= 8
    c1 = 1.0 - ADAM_B1 ** ADAM_STEP
    c2 = 1.0 - ADAM_B2 ** ADAM_STEP

    def body(p_ref, w_ref, m_ref, v_ref, g_ref, d_ref, nm_ref, nv_ref):
        g = p_ref[0].astype(F32)
        for s in range(1, k):
            g = g + p_ref[s].astype(F32)
        nm = ADAM_B1 * m_ref[...] + (1.0 - ADAM_B1) * g
        nv = ADAM_B2 * v_ref[...] + (1.0 - ADAM_B2) * (g * g)
        g_ref[...] = g
        nm_ref[...] = nm
        nv_ref[...] = nv
        d_ref[...] = -ADAM_LR * ((nm / c1) / (jnp.sqrt(nv / c2) + ADAM_EPS) + ADAM_WD * w_ref[...])

    blk = _row_spec(rows, cols)
    return pl.pallas_call(
        body, name=name, grid=(n // rows,),
        in_specs=[pl.BlockSpec((k, rows, cols), lambda i: (0, i, 0)), blk, blk, blk],
        out_specs=[blk] * 4,
        out_shape=[jax.ShapeDtypeStruct((n, cols), F32)] * 4,
        compiler_params=_params(1),
    )(parts, w, m, v)


SSM_KEYS = ("norm", "w_in", "a_re", "a_im", "log_step", "b_re", "b_im", "c_re", "c_im", "d", "w_glu", "b_glu", "w_out")
ATTN_KEYS = ("norm", "w_in", "sinks", "w_out")
LAYER_KEYS = (SSM_KEYS, ATTN_KEYS, SSM_KEYS, ATTN_KEYS)
BIG_KEYS = ("w_in", "w_glu", "w_out")
ATTN_SPLITS = (Q_DIM, KV_DIM, KV_DIM, D_MODEL)


def _rope_tables(t):
    pos = jnp.arange(t, dtype=F32)
    inv_freq = ROPE_THETA ** (-jnp.arange(0, HEAD_DIM, 2, dtype=F32) / HEAD_DIM)
    ang = pos[:, None] * inv_freq[None, :]
    cos, sin = jnp.cos(ang), jnp.sin(ang)
    return jnp.tile(jnp.concatenate([cos, cos], axis=1), (1, 2)), jnp.tile(jnp.concatenate([-sin, sin], axis=1), (1, 2))


def _ssm_layer_fwd(i, x, p, w):
    tag = "l%d_" % i
    mats, mats_vjp = jax.vjp(_s5_matrices, p["a_re"], p["a_im"], p["log_step"], p["b_re"], p["b_im"], p["c_re"], p["c_im"])
    kern, cpt, bpt, ar, ai = mats
    tm = _s5_toeplitz(kern)
    mb = dict(tm=tm.astype(BF16), tmt=jnp.swapaxes(tm, 1, 2).astype(BF16), cpt=cpt.astype(BF16),
              cp=jnp.swapaxes(cpt, 1, 2).astype(BF16), bpt=bpt.astype(BF16), bp=jnp.swapaxes(bpt, 1, 2).astype(BF16))
    u, gate = _inproj_fwd(x, p["norm"], w["w_in"], (D_MODEL, D_MODEL), None, tag + "inproj_fwd")
    xre, xim = _s5_project(u, mb["bpt"], tag + "s5_block_inputs")
    hre, him = _s5_scan_fwd(xre, xim, ar, ai, tag + "s5_scan_fwd")
    y = _s5_outputs(u, hre, him, mb["tm"], mb["cpt"], p["d"], tag + "s5_outputs")
    xn = _ssm_out_fwd(y, gate, x, w["w_glu"], p["b_glu"], w["w_out"], tag + "out_fwd")
    return xn, (x, u, gate, y, hre, him, mb, ar, ai, mats_vjp)


def _ssm_layer_bwd(i, dxn, saved, p, w):
    tag = "l%d_" % i
    x, u, gate, y, hre, him, mb, ar, ai, mats_vjp = saved
    dy, dgate, dw_glu, db_glu, dw_out = _ssm_out_bwd(dxn, y, gate, w["w_glu"], p["b_glu"], w["w_out"], tag + "out_bwd")
    dhre, dhim = _s5_project(dy, mb["cp"], tag + "s5_state_grads")
    dxre, dxim, dar, dai = _s5_scan_bwd(dhre, dhim, hre, him, ar, ai, tag + "s5_scan_bwd")
    du, dk, dcpt, dbpt, dd = _s5_backward(dy, u, hre, him, dxre, dxim, mb["tmt"], mb["bp"], p["d"], tag + "s5_backward")
    dk = dk.reshape(SSM_GROUPS, S5_BLOCK, SSM_GROUP, SSM_GROUP)
    da_re, da_im, dlog_step, db_re, db_im, dc_re, dc_im = mats_vjp((dk, dcpt, dbpt, dar, dai))
    dx, dw_in, dnorm = _inproj_bwd(x, p["norm"], w["w_in"], [du, dgate], dxn, tag + "inproj_bwd")
    grads = dict(norm=dnorm.reshape(D_MODEL), w_in=dw_in, a_re=da_re, a_im=da_im, log_step=dlog_step, b_re=db_re,
                 b_im=db_im, c_re=dc_re, c_im=dc_im, d=dd.reshape(D_MODEL), w_glu=dw_glu, b_glu=db_glu.reshape(D_MODEL),
                 w_out=dw_out)
    return dx, grads


def _attn_layer_fwd(i, x, p, w, rope):
    tag = "l%d_" % i
    q, k, v, gate = _inproj_fwd(x, p["norm"], w["w_in"], ATTN_SPLITS, rope, tag + "inproj_fwd")
    o = _attn_fwd(q, k, v, p["sinks"], tag + "attn_fwd")
    xn = _attn_out_fwd(o, gate, x, w["w_out"], tag + "out_fwd")
    return xn, (x, q, k, v, gate, o)


def _attn_layer_bwd(i, dxn, saved, p, w, rope):
    tag = "l%d_" % i
    x, q, k, v, gate, o = saved
    do, dgate, dw_out = _attn_out_bwd(dxn, o, gate, w["w_out"], tag + "out_bwd")
    dq, dk, dv, dsinks = _attn_bwd(q, k, v, o, do, p["sinks"], rope, tag + "attn_bwd")
    dx, dw_in, dnorm = _inproj_bwd(x, p["norm"], w["w_in"], [dq, dk, dv, dgate], dxn, tag + "inproj_bwd")
    return dx, dict(norm=dnorm.reshape(D_MODEL), w_in=dw_in, sinks=dsinks.reshape(N_Q_HEADS, ATTN_BLOCK).sum(axis=1), w_out=dw_out)


def _local_step(x, target, small, big):
    rope = _rope_tables(x.shape[0])
    saved = []
    for i in range(4):
        if i % 2 == 0:
            x, s = _ssm_layer_fwd(i, x, small[i], big[i])
        else:
            x, s = _attn_layer_fwd(i, x, small[i], big[i], rope)
        saved.append(s)
    loss, dx, dfinal = _loss_head(x, small[4]["norm"], target, "loss_head")
    grads = [None] * 4 + [dict(norm=dfinal.reshape(D_MODEL))]
    for i in (3, 2, 1, 0):
        if i % 2 == 0:
            dx, grads[i] = _ssm_layer_bwd(i, dx, saved[i], small[i], big[i])
        else:
            dx, grads[i] = _attn_layer_bwd(i, dx, saved[i], small[i], big[i], rope)
    return loss[0, 0], dx, grads


def _owner_major(key, g):
    if key == "w_in":
        return g.reshape(D_MODEL, N_CHIPS, N_CORES, -1).transpose(2, 1, 0, 3)
    return g.reshape(N_CHIPS, N_CORES, -1, D_MODEL).transpose(1, 0, 2, 3)


def _from_gathered(key, mine, other):
    return _in_device_order(list(mine), list(other), 1 if key == "w_in" else 0)


SMALL_ROWS = 72


def _rows_of_small(a):
    flat = a.reshape(-1)
    return jnp.pad(flat, (0, -flat.shape[0] % D_MODEL)).reshape(-1, D_MODEL)


def _stack_small(arrays):
    rows = jnp.concatenate([_rows_of_small(a) for a in arrays], axis=0)
    assert rows.shape[0] <= N_DEV * SMALL_ROWS
    return jnp.pad(rows, ((0, N_DEV * SMALL_ROWS - rows.shape[0]), (0, 0)))


def kernel(*args):
    names = ["x"]
    layer_names = []
    for i, keys in enumerate(LAYER_KEYS):
        layer_names += ["l%d_%s" % (i, k) for k in keys]
    layer_names.append("final_norm")
    names += layer_names + ["loss_target"] + ["m_" + n for n in layer_names] + ["v_" + n for n in layer_names]
    given = dict(zip(names, args))
    big_names = [n for n in layer_names if n.split("_", 1)[1] in BIG_KEYS]
    small_names = [n for n in layer_names if n not in big_names]

    families, offsets = {}, {}
    for n in big_names:
        family = families.setdefault(given[n].shape[1], [])
        offsets[n] = sum(given[other].shape[0] for other in family)
        family.append(n)
    families = list(families.values())
    stack = lambda pre, family: jnp.concatenate([given[pre + n] for n in family], axis=0)
    rows_of = lambda a, n: a[..., offsets[n]:offsets[n] + given[n].shape[0], :]

    gathered = _all_gather([stack("", family).astype(BF16) for family in families], "gather_weights")
    big = [dict() for _ in range(4)]
    for family, (mine, other) in zip(families, gathered):
        for n in family:
            layer, key = int(n[1]), n.split("_", 1)[1]
            big[layer][key] = _from_gathered(key, rows_of(mine, n), rows_of(other, n))
    small = [dict() for _ in range(5)]
    for n in small_names:
        if n == "final_norm":
            small[4]["norm"] = given[n]
        else:
            small[int(n[1])][n.split("_", 1)[1]] = given[n]

    loss, dx, grads = _local_step(given["x"][0], given["loss_target"][0], small, big)
    loss = lax.psum(loss, ("x", "y", "c"))

    def grad_of(n):
        return grads[4]["norm"] if n == "final_norm" else grads[int(n[1])][n.split("_", 1)[1]]

    flat = lambda f: _stack_small([f(n) for n in small_names])
    sends = [jnp.concatenate([_owner_major(n.split("_", 1)[1], grad_of(n)) for n in family], axis=2) for family in families]
    sends.append(flat(grad_of).reshape(N_CORES, N_CHIPS, SMALL_ROWS, D_MODEL))
    parts = _reduce_scatter(sends, [BF16] * len(families) + [F32], "scatter_grads")

    outs = {}
    tags = ("grad_", "delta_", "new_m_", "new_v_")
    for i, family in enumerate(families):
        results = _adamw(parts[i], stack("", family), stack("m_", family), stack("v_", family), "adamw_matrices%d" % i)
        for tag, a in zip(tags, results):
            for n in family:
                outs[tag + n] = rows_of(a, n)

    my_slice = _sum_parts([parts[-1][s] for s in range(N_CHIPS)], F32, "sum_small_grads")
    mine, other = _all_gather([my_slice], "gather_small_grads")[0]
    first = lax.axis_index("c") == 0
    g_all = jnp.concatenate([jnp.where(first, mine, other), jnp.where(first, other, mine)], axis=0)
    g_all = g_all.reshape(1, N_DEV * SMALL_ROWS, D_MODEL)
    results = _adamw(g_all, flat(lambda n: given[n]), flat(lambda n: given["m_" + n]), flat(lambda n: given["v_" + n]),
                     "adamw_small")
    for tag, a in zip(tags, results):
        at = 0
        for n in small_names:
            rows = -(-given[n].size // D_MODEL)
            outs[tag + n] = a[at:at + rows].reshape(-1)[:given[n].size].reshape(given[n].shape)
            at += rows
    result = [loss, dx[None]]
    for tag in ("grad_", "delta_", "new_m_", "new_v_"):
        result += [outs[tag + n] for n in layer_names]
    return tuple(result)
```

```python
import functools
import math

import jax
import jax.numpy as jnp
from jax import lax
from jax.experimental import pallas as pl
from jax.experimental.pallas import tpu as pltpu

F32 = jnp.float32
BF16 = jnp.bfloat16

D_MODEL = 1024
SSM_GROUP = 16
SSM_GROUPS = D_MODEL // SSM_GROUP
SSM_STATE = 64
S5_BLOCK = 16
S5_LANES = S5_BLOCK * SSM_GROUP
HEAD_DIM = 64
N_Q_HEADS = 16
N_KV_HEADS = 2
GQA = N_Q_HEADS // N_KV_HEADS
Q_DIM = N_Q_HEADS * HEAD_DIM
KV_DIM = N_KV_HEADS * HEAD_DIM
ATTN_BLOCK = 128
ROPE_THETA = 10000.0
NORM_EPS = 1e-5
NEG_INF = -1e30
ATTN_SCALE = HEAD_DIM ** -0.5
N_DEV = 8

ADAM_LR = 0.001
ADAM_B1 = 0.9
ADAM_B2 = 0.999
ADAM_EPS = 1e-08
ADAM_WD = 0.01
ADAM_STEP = 10

VMEM_LIMIT = 56 * 1024 * 1024
ROWS_FWD = 512
ROWS_BWD = 512

NT = (((1,), (1,)), ((), ()))
TN = (((0,), (0,)), ((), ()))


def _params(n_grid):
    return pltpu.CompilerParams(dimension_semantics=("arbitrary",) * n_grid, vmem_limit_bytes=VMEM_LIMIT)


def _dot(a, b):
    return jnp.dot(a, b, preferred_element_type=F32)


def _dot_nt(a, b):
    return lax.dot_general(a, b, NT, preferred_element_type=F32)


def _dot_tn(a, b):
    return lax.dot_general(a, b, TN, preferred_element_type=F32)


def _sigmoid(x):
    return 1.0 / (1.0 + jnp.exp(-x))


_GELU_K = math.sqrt(2.0 / math.pi)


def _gelu(x):
    return x * (0.5 * (1.0 + jnp.tanh(_GELU_K * (x + 0.044715 * (x * x * x)))))


def _gelu_grad(x):
    t = jnp.tanh(_GELU_K * (x + 0.044715 * (x * x * x)))
    return 0.5 * (1.0 + t) + 0.5 * x * (1.0 - t * t) * (_GELU_K * (1.0 + 3.0 * 0.044715 * (x * x)))


def _row_spec(rows, cols):
    return pl.BlockSpec((rows, cols), lambda i: (i, 0))


def _const_spec(shape):
    zeros = (0,) * len(shape)
    return pl.BlockSpec(shape, lambda i: zeros, pipeline_mode=pl.Buffered(1))


def _rope_apply(t, cos, sin_signed, sign):
    lane = lax.broadcasted_iota(jnp.int32, (1, 128), 1)
    first_half = (lane % HEAD_DIM) < (HEAD_DIM // 2)
    out = []
    for j in range(t.shape[1] // 128):
        tj = t[:, 128 * j:128 * (j + 1)]
        partner = jnp.where(first_half, pltpu.roll(tj, 128 - HEAD_DIM // 2, 1), pltpu.roll(tj, HEAD_DIM // 2, 1))
        out.append(tj * cos + sign * (partner * sin_signed))
    return out[0] if len(out) == 1 else jnp.concatenate(out, axis=1)


def _inproj_fwd(x, norm, w, splits, rope, name):
    t = x.shape[0]
    n = w.shape[1]
    rows = ROWS_FWD

    def body(*refs):
        if rope is None:
            x_ref, n_ref, w_ref = refs[:3]
            outs = refs[3:]
        else:
            x_ref, n_ref, w_ref, cos_ref, sin_ref = refs[:5]
            outs = refs[5:]
        xv = x_ref[...]
        rstd = lax.rsqrt(jnp.mean(xv * xv, axis=-1, keepdims=True) + NORM_EPS)
        h = (xv * rstd) * n_ref[...]
        proj = _dot(h.astype(BF16), w_ref[...])
        off = 0
        for i, width in enumerate(splits):
            piece = proj[:, off:off + width]
            if rope is not None and i < 2:
                piece = _rope_apply(piece, cos_ref[...], sin_ref[...], 1.0)
            outs[i][...] = piece
            off += width

    in_specs = [_row_spec(rows, D_MODEL), _const_spec((1, D_MODEL)), _const_spec((D_MODEL, n))]
    args = [x, norm.reshape(1, D_MODEL), w]
    if rope is not None:
        in_specs += [_row_spec(rows, 128), _row_spec(rows, 128)]
        args += list(rope)
    return pl.pallas_call(
        body, name=name, grid=(t // rows,), in_specs=in_specs,
        out_specs=[_row_spec(rows, width) for width in splits],
        out_shape=[jax.ShapeDtypeStruct((t, width), F32) for width in splits],
        compiler_params=_params(1),
    )(*args)


def _inproj_bwd(x, norm, w, dpieces, dxn, name):
    t = x.shape[0]
    n = w.shape[1]
    rows = ROWS_BWD
    widths = [p.shape[1] for p in dpieces]
    k = len(dpieces)

    def body(*refs):
        x_ref, n_ref, w_ref, dxn_ref = refs[:4]
        d_refs = refs[4:4 + k]
        dx_ref, dw_ref, dn_ref = refs[4 + k:]
        @pl.when(pl.program_id(0) == 0)
        def _():
            dw_ref[...] = jnp.zeros_like(dw_ref)
            dn_ref[...] = jnp.zeros_like(dn_ref)

        xv = x_ref[...]
        rstd = lax.rsqrt(jnp.mean(xv * xv, axis=-1, keepdims=True) + NORM_EPS)
        xhat = xv * rstd
        h = xhat * n_ref[...]
        dproj = [r[...].astype(BF16) for r in d_refs]
        dproj = dproj[0] if k == 1 else jnp.concatenate(dproj, axis=1)
        dh = _dot_nt(dproj, w_ref[...])
        dw_ref[...] += _dot_tn(h.astype(BF16), dproj)
        dn_ref[...] += jnp.sum(dh * xhat, axis=0, keepdims=True)
        dxhat = dh * n_ref[...]
        dx_ref[...] = rstd * (dxhat - xhat * jnp.mean(dxhat * xhat, axis=-1, keepdims=True)) + dxn_ref[...]

    return pl.pallas_call(
        body, name=name, grid=(t // rows,),
        in_specs=[_row_spec(rows, D_MODEL), _const_spec((1, D_MODEL)), _const_spec((D_MODEL, n)),
                  _row_spec(rows, D_MODEL)] + [_row_spec(rows, width) for width in widths],
        out_specs=[_row_spec(rows, D_MODEL), _const_spec((D_MODEL, n)), _const_spec((1, D_MODEL))],
        out_shape=[jax.ShapeDtypeStruct((t, D_MODEL), F32), jax.ShapeDtypeStruct((D_MODEL, n), F32),
                   jax.ShapeDtypeStruct((1, D_MODEL), F32)],
        compiler_params=_params(1),
    )(x, norm.reshape(1, D_MODEL), w, dxn, *dpieces)


def _s5_matrices(a_re, a_im, log_step, b_re, b_im, c_re, c_im):
    r = S5_BLOCK
    step = jnp.exp(log_step)[:, None]
    lr, li = a_re * step, a_im * step
    k = jnp.arange(r + 1, dtype=F32)
    mag = jnp.exp(lr[:, None, :] * k[:, None])
    pr = mag * jnp.cos(li[:, None, :] * k[:, None])
    pi = mag * jnp.sin(li[:, None, :] * k[:, None])
    nr, ni = pr[:, 1] - 1.0, pi[:, 1]
    den = a_re * a_re + a_im * a_im
    qr, qi = (nr * a_re + ni * a_im) / den, (ni * a_re - nr * a_im) / den
    bbr = qr[..., None] * b_re - qi[..., None] * b_im
    bbi = qr[..., None] * b_im + qi[..., None] * b_re
    wr = c_re[:, None] * pr[:, :, None, :] - c_im[:, None] * pi[:, :, None, :]
    wi = c_re[:, None] * pi[:, :, None, :] + c_im[:, None] * pr[:, :, None, :]
    w = jnp.concatenate([wr, -wi], axis=-1)
    bb = jnp.concatenate([bbr, bbi], axis=1)
    kern = jnp.einsum("gxp,gpi->gxi", w[:, :r].reshape(SSM_GROUPS, S5_LANES, 2 * SSM_STATE), bb,
                      precision=lax.Precision.HIGHEST).reshape(SSM_GROUPS, r, SSM_GROUP, SSM_GROUP)
    cpt = w[:, 1:].reshape(SSM_GROUPS, S5_LANES, 2 * SSM_STATE)
    prs = jnp.swapaxes(pr[:, r - 1::-1][:, :r], 1, 2)[..., None]
    pis = jnp.swapaxes(pi[:, r - 1::-1][:, :r], 1, 2)[..., None]
    bp_re = prs * bbr[:, :, None, :] - pis * bbi[:, :, None, :]
    bp_im = prs * bbi[:, :, None, :] + pis * bbr[:, :, None, :]
    bpt = jnp.concatenate([bp_re, bp_im], axis=1).reshape(SSM_GROUPS, 2 * SSM_STATE, S5_LANES)
    ar = pr[:, r].reshape(1, SSM_GROUPS * SSM_STATE)
    ai = pi[:, r].reshape(1, SSM_GROUPS * SSM_STATE)
    return kern, cpt, bpt, ar, ai


def _s5_toeplitz(kern):
    r = S5_BLOCK
    cols = [jnp.pad(kern[:, :r - s], ((0, 0), (s, 0), (0, 0), (0, 0))) for s in range(r)]
    return jnp.stack(cols, axis=3).reshape(SSM_GROUPS, S5_LANES, S5_LANES)


S5_OCTET = 128 // SSM_GROUP
S5_STEPS = SSM_GROUPS // S5_OCTET


def _oct_spec(t):
    return pl.BlockSpec((t, 128), lambda j: (0, j))


def _state_spec(nb):
    return pl.BlockSpec((nb, S5_OCTET * SSM_STATE), lambda j: (0, j))


def _gmat_spec(a, b):
    return pl.BlockSpec((S5_OCTET, a, b), lambda j: (j, 0, 0))


def _block_rows(ref, nb):
    return [ref[pl.ds(r, nb, stride=S5_BLOCK), :] for r in range(S5_BLOCK)]


def _group_cols(pieces_t, g):
    return jnp.concatenate([p[SSM_GROUP * g:SSM_GROUP * (g + 1)] for p in pieces_t], axis=0)


def _state_cols(re_t, im_t, g):
    return jnp.concatenate([re_t[SSM_STATE * g:SSM_STATE * (g + 1)], im_t[SSM_STATE * g:SSM_STATE * (g + 1)]], axis=0)


def _s5_project(a, mat, name):
    t = a.shape[0]
    nb = t // S5_BLOCK

    def body(a_ref, m_ref, re_ref, im_ref):
        at = [p.T for p in _block_rows(a_ref, nb)]
        for pair in range(S5_OCTET // 2):
            xs = [_dot(m_ref[2 * pair + k], _group_cols(at, 2 * pair + k).astype(BF16)) for k in (0, 1)]
            lanes = slice(128 * pair, 128 * (pair + 1))
            re_ref[:, lanes] = jnp.concatenate([xs[0][:SSM_STATE], xs[1][:SSM_STATE]], axis=0).T
            im_ref[:, lanes] = jnp.concatenate([xs[0][SSM_STATE:], xs[1][SSM_STATE:]], axis=0).T

    return pl.pallas_call(
        body, name=name, grid=(S5_STEPS,),
        in_specs=[_oct_spec(t), _gmat_spec(2 * SSM_STATE, S5_LANES)],
        out_specs=[_state_spec(nb), _state_spec(nb)],
        out_shape=[jax.ShapeDtypeStruct((nb, SSM_GROUPS * SSM_STATE), F32)] * 2,
        compiler_params=_params(1),
    )(a, mat)


_SCAN_LANES = 1024


def _s5_scan_fwd(xre, xim, ar, ai, name):
    nb = xre.shape[0]
    col = pl.BlockSpec((nb, _SCAN_LANES), lambda j: (0, j))
    par = pl.BlockSpec((1, _SCAN_LANES), lambda j: (0, j))

    def body(xre_ref, xim_ref, ar_ref, ai_ref, hre_ref, him_ref):
        a_r, a_i = ar_ref[...], ai_ref[...]

        def step(b, carry):
            hr, hi = carry
            hre_ref[pl.ds(b, 1), :] = hr
            him_ref[pl.ds(b, 1), :] = hi
            xr, xi = xre_ref[pl.ds(b, 1), :], xim_ref[pl.ds(b, 1), :]
            return a_r * hr - a_i * hi + xr, a_r * hi + a_i * hr + xi

        zero = jnp.zeros((1, _SCAN_LANES), F32)
        lax.fori_loop(0, nb, step, (zero, zero))

    return pl.pallas_call(
        body, name=name, grid=(xre.shape[1] // _SCAN_LANES,),
        in_specs=[col, col, par, par], out_specs=[col, col],
        out_shape=[jax.ShapeDtypeStruct(xre.shape, F32)] * 2,
        compiler_params=_params(1),
    )(xre, xim, ar, ai)


def _s5_scan_bwd(dhre, dhim, hre, him, ar, ai, name):
    nb = dhre.shape[0]
    col = pl.BlockSpec((nb, _SCAN_LANES), lambda j: (0, j))
    par = pl.BlockSpec((1, _SCAN_LANES), lambda j: (0, j))

    def body(dhre_ref, dhim_ref, hre_ref, him_ref, ar_ref, ai_ref, dxre_ref, dxim_ref, dar_ref, dai_ref):
        a_r, a_i = ar_ref[...], ai_ref[...]

        def step(s, carry):
            gr, gi, dar, dai = carry
            b = nb - 1 - s
            dxre_ref[pl.ds(b, 1), :] = gr
            dxim_ref[pl.ds(b, 1), :] = gi
            hr, hi = hre_ref[pl.ds(b, 1), :], him_ref[pl.ds(b, 1), :]
            dar = dar + (hr * gr + hi * gi)
            dai = dai + (hr * gi - hi * gr)
            dr, di = dhre_ref[pl.ds(b, 1), :], dhim_ref[pl.ds(b, 1), :]
            return dr + (a_r * gr + a_i * gi), di + (a_r * gi - a_i * gr), dar, dai

        zero = jnp.zeros((1, _SCAN_LANES), F32)
        _, _, dar, dai = lax.fori_loop(0, nb, step, (zero, zero, zero, zero))
        dar_ref[...] = dar
        dai_ref[...] = dai

    return pl.pallas_call(
        body, name=name, grid=(dhre.shape[1] // _SCAN_LANES,),
        in_specs=[col, col, col, col, par, par], out_specs=[col, col, par, par],
        out_shape=[jax.ShapeDtypeStruct(dhre.shape, F32)] * 2 + [jax.ShapeDtypeStruct(ar.shape, F32)] * 2,
        compiler_params=_params(1),
    )(dhre, dhim, hre, him, ar, ai)


def _s5_outputs(u, hre, him, tm, cpt, d, name):
    t = u.shape[0]
    nb = t // S5_BLOCK

    def body(u_ref, hre_ref, him_ref, tm_ref, cpt_ref, d_ref, y_ref):
        u_rows = _block_rows(u_ref, nb)
        ut = [p.T for p in u_rows]
        hre_t, him_t = hre_ref[...].T, him_ref[...].T
        yts = []
        for g in range(S5_OCTET):
            yts.append(_dot(tm_ref[g], _group_cols(ut, g).astype(BF16))
                       + _dot(cpt_ref[g], _state_cols(hre_t, him_t, g).astype(BF16)))
        for r in range(S5_BLOCK):
            rows = jnp.concatenate([yt[SSM_GROUP * r:SSM_GROUP * (r + 1)] for yt in yts], axis=0)
            y_ref[pl.ds(r, nb, stride=S5_BLOCK), :] = rows.T + d_ref[...] * u_rows[r]

    return pl.pallas_call(
        body, name=name, grid=(S5_STEPS,),
        in_specs=[_oct_spec(t), _state_spec(nb), _state_spec(nb), _gmat_spec(S5_LANES, S5_LANES),
                  _gmat_spec(S5_LANES, 2 * SSM_STATE), _oct_spec(1)],
        out_specs=_oct_spec(t),
        out_shape=jax.ShapeDtypeStruct(u.shape, F32),
        compiler_params=_params(1),
    )(u, hre, him, tm, cpt, d.reshape(1, D_MODEL))


def _s5_backward(dy, u, hre, him, dxre, dxim, tmt, bp, d, name):
    t = u.shape[0]
    nb = t // S5_BLOCK

    def body(dy_ref, u_ref, hre_ref, him_ref, dxre_ref, dxim_ref, tmt_ref, bp_ref, d_ref,
             du_ref, dk_ref, dcpt_ref, dbpt_ref, dd_ref, dtm_scratch):
        dy_rows, u_rows = _block_rows(dy_ref, nb), _block_rows(u_ref, nb)
        dyt, ut = [p.T for p in dy_rows], [p.T for p in u_rows]
        hre_t, him_t = hre_ref[...].T, him_ref[...].T
        dxre_t, dxim_t = dxre_ref[...].T, dxim_ref[...].T
        duts = []
        for g in range(S5_OCTET):
            dyg, ug = _group_cols(dyt, g).astype(BF16), _group_cols(ut, g).astype(BF16)
            hg = _state_cols(hre_t, him_t, g).astype(BF16)
            dxg = _state_cols(dxre_t, dxim_t, g).astype(BF16)
            duts.append(_dot(tmt_ref[g], dyg) + _dot(bp_ref[g], dxg))
            dtm_scratch[...] = _dot_nt(dyg, ug)
            dk = dtm_scratch[:, :SSM_GROUP]
            for s in range(1, S5_BLOCK):
                below = dtm_scratch[SSM_GROUP * s:, SSM_GROUP * s:SSM_GROUP * (s + 1)]
                dk = dk + jnp.concatenate([below, jnp.zeros((SSM_GROUP * s, SSM_GROUP), F32)], axis=0)
            dk_ref[g] = dk
            dcpt_ref[g] = _dot_nt(dyg, hg)
            dbpt_ref[g] = _dot_nt(dxg, ug)
        dd = jnp.zeros((1, 128), F32)
        for r in range(S5_BLOCK):
            rows = jnp.concatenate([dut[SSM_GROUP * r:SSM_GROUP * (r + 1)] for dut in duts], axis=0)
            du_ref[pl.ds(r, nb, stride=S5_BLOCK), :] = rows.T + d_ref[...] * dy_rows[r]
            dd = dd + jnp.sum(dy_rows[r] * u_rows[r], axis=0, keepdims=True)
        dd_ref[...] = dd

    return pl.pallas_call(
        body, name=name, grid=(S5_STEPS,),
        in_specs=[_oct_spec(t), _oct_spec(t), _state_spec(nb), _state_spec(nb), _state_spec(nb), _state_spec(nb),
                  _gmat_spec(S5_LANES, S5_LANES), _gmat_spec(S5_LANES, 2 * SSM_STATE), _oct_spec(1)],
        out_specs=[_oct_spec(t), _gmat_spec(S5_LANES, SSM_GROUP), _gmat_spec(S5_LANES, 2 * SSM_STATE),
                   _gmat_spec(2 * SSM_STATE, S5_LANES), _oct_spec(1)],
        out_shape=[jax.ShapeDtypeStruct(u.shape, F32),
                   jax.ShapeDtypeStruct((SSM_GROUPS, S5_LANES, SSM_GROUP), F32),
                   jax.ShapeDtypeStruct((SSM_GROUPS, S5_LANES, 2 * SSM_STATE), F32),
                   jax.ShapeDtypeStruct((SSM_GROUPS, 2 * SSM_STATE, S5_LANES), F32),
                   jax.ShapeDtypeStruct((1, D_MODEL), F32)],
        scratch_shapes=[pltpu.VMEM((S5_LANES, S5_LANES), F32)],
        compiler_params=_params(1),
    )(dy, u, hre, him, dxre, dxim, tmt, bp, d.reshape(1, D_MODEL))


def _ssm_out_fwd(y, gate, x, w_glu, b_glu, w_out, name):
    t = x.shape[0]
    rows = ROWS_FWD

    def body(y_ref, g_ref, x_ref, wg_ref, bg_ref, wo_ref, o_ref):
        z0 = _gelu(y_ref[...])
        s = _dot(z0.astype(BF16), wg_ref[...]) + bg_ref[...]
        gate_v = g_ref[...]
        a = (z0 * _sigmoid(s)) * (gate_v * _sigmoid(gate_v))
        o_ref[...] = x_ref[...] + _dot(a.astype(BF16), wo_ref[...])

    return pl.pallas_call(
        body, name=name, grid=(t // rows,),
        in_specs=[_row_spec(rows, D_MODEL)] * 3 + [_const_spec((D_MODEL, D_MODEL)), _const_spec((1, D_MODEL)),
                                                   _const_spec((D_MODEL, D_MODEL))],
        out_specs=_row_spec(rows, D_MODEL),
        out_shape=jax.ShapeDtypeStruct((t, D_MODEL), F32),
        compiler_params=_params(1),
    )(y, gate, x, w_glu, b_glu.reshape(1, D_MODEL), w_out)


def _ssm_out_bwd(dxn, y, gate, w_glu, b_glu, w_out, name):
    t = y.shape[0]
    rows = ROWS_BWD

    def body(dxn_ref, y_ref, g_ref, wg_ref, bg_ref, wo_ref, dy_ref, dg_ref, dwg_ref, dbg_ref, dwo_ref):
        @pl.when(pl.program_id(0) == 0)
        def _():
            dwo_ref[...] = jnp.zeros_like(dwo_ref)
            dwg_ref[...] = jnp.zeros_like(dwg_ref)
            dbg_ref[...] = jnp.zeros_like(dbg_ref)

        yv = y_ref[...]
        z0 = _gelu(yv)
        z0b = z0.astype(BF16)
        sg = _sigmoid(_dot(z0b, wg_ref[...]) + bg_ref[...])
        z = z0 * sg
        gate_v = g_ref[...]
        sgg = _sigmoid(gate_v)
        silu = gate_v * sgg
        dob = dxn_ref[...].astype(BF16)
        da = _dot_nt(dob, wo_ref[...])
        dwo_ref[...] += _dot_tn((z * silu).astype(BF16), dob)
        dz = da * silu
        dg_ref[...] = da * z * (sgg * (1.0 + gate_v * (1.0 - sgg)))
        ds = dz * z0 * (sg * (1.0 - sg))
        dsb = ds.astype(BF16)
        dz0 = dz * sg + _dot_nt(dsb, wg_ref[...])
        dwg_ref[...] += _dot_tn(z0b, dsb)
        dbg_ref[...] += jnp.sum(ds, axis=0, keepdims=True)
        dy_ref[...] = dz0 * _gelu_grad(yv)

    sq = _const_spec((D_MODEL, D_MODEL))
    vec = _const_spec((1, D_MODEL))
    return pl.pallas_call(
        body, name=name, grid=(t // rows,),
        in_specs=[_row_spec(rows, D_MODEL)] * 3 + [sq, vec, sq],
        out_specs=[_row_spec(rows, D_MODEL), _row_spec(rows, D_MODEL), sq, vec, sq],
        out_shape=[jax.ShapeDtypeStruct((t, D_MODEL), F32)] * 2 + [
            jax.ShapeDtypeStruct((D_MODEL, D_MODEL), F32), jax.ShapeDtypeStruct((1, D_MODEL), F32),
            jax.ShapeDtypeStruct((D_MODEL, D_MODEL), F32)],
        compiler_params=_params(1),
    )(dxn, y, gate, w_glu, b_glu.reshape(1, D_MODEL), w_out)


KV_LANES = GQA * ATTN_BLOCK


def _attn_bias(block_is_first):
    kj = lax.broadcasted_iota(jnp.int32, (2 * ATTN_BLOCK, ATTN_BLOCK), 0)
    qi = lax.broadcasted_iota(jnp.int32, (2 * ATTN_BLOCK, ATTN_BLOCK), 1)
    dist = qi + ATTN_BLOCK - kj
    valid = (dist >= 0) & (dist < ATTN_BLOCK) & (jnp.logical_not(block_is_first) | (kj >= ATTN_BLOCK))
    return jnp.tile(jnp.where(valid, 0.0, NEG_INF).astype(F32), (1, GQA))


def _head_cols(a_t, kvh):
    heads = range(kvh * GQA, (kvh + 1) * GQA)
    return jnp.concatenate([a_t[HEAD_DIM * h:HEAD_DIM * (h + 1)] for h in heads], axis=1)


def _head_rows(a_cols):
    stacked = jnp.concatenate([a_cols[:, ATTN_BLOCK * g:ATTN_BLOCK * (g + 1)] for g in range(GQA)], axis=0)
    return stacked.T


def _kv_rows(prev_ref, cur_ref, kvh):
    lanes = slice(HEAD_DIM * kvh, HEAD_DIM * (kvh + 1))
    return jnp.concatenate([prev_ref[:, lanes], cur_ref[:, lanes]], axis=0).astype(BF16)


def _kv_cols(prev_t, cur_t, kvh):
    rows = slice(HEAD_DIM * kvh, HEAD_DIM * (kvh + 1))
    return jnp.concatenate([prev_t[rows], cur_t[rows]], axis=1).astype(BF16)


def _attn_probs(kk, q_cols, sink_row, bias):
    s = _dot(kk, q_cols) * ATTN_SCALE + bias
    m = jnp.maximum(jnp.max(s, axis=0, keepdims=True), sink_row)
    p = jnp.exp(s - m)
    e_sink = jnp.exp(sink_row - m)
    inv = 1.0 / (jnp.sum(p, axis=0, keepdims=True) + e_sink)
    return p * inv, e_sink * inv


def _sink_cols(sinks):
    return jnp.repeat(sinks, ATTN_BLOCK).reshape(N_KV_HEADS, 1, KV_LANES)


def _attn_fwd(q, k, v, sinks, name, exchange=None):
    t = q.shape[0]
    nblk = t // ATTN_BLOCK

    def body(s_ref, q_ref, kc_ref, kp_ref, vc_ref, vp_ref, o_ref):
        bias = _attn_bias(pl.program_id(0) == 0)
        q_t = q_ref[...].T
        vp_t, vc_t = vp_ref[...].T, vc_ref[...].T
        for kvh in range(N_KV_HEADS):
            p, _ = _attn_probs(_kv_rows(kp_ref, kc_ref, kvh), _head_cols(q_t, kvh).astype(BF16), s_ref[kvh], bias)
            o_cols = _dot(_kv_cols(vp_t, vc_t, kvh), p.astype(BF16))
            o_ref[:, GQA * HEAD_DIM * kvh:GQA * HEAD_DIM * (kvh + 1)] = _head_rows(o_cols)

    cur = lambda i: (i, 0)
    prev = lambda i: (jnp.maximum(i - 1, 0), 0)
    (o,), got = _call_hosting(
        exchange, body, (_sink_cols(sinks), q, k, k, v, v), name=name, steps=nblk,
        in_specs=[_const_spec((N_KV_HEADS, 1, KV_LANES)),
                  pl.BlockSpec((ATTN_BLOCK, Q_DIM), cur),
                  pl.BlockSpec((ATTN_BLOCK, KV_DIM), cur), pl.BlockSpec((ATTN_BLOCK, KV_DIM), prev),
                  pl.BlockSpec((ATTN_BLOCK, KV_DIM), cur), pl.BlockSpec((ATTN_BLOCK, KV_DIM), prev)],
        out_specs=[pl.BlockSpec((ATTN_BLOCK, Q_DIM), cur)],
        out_shape=[jax.ShapeDtypeStruct((t, Q_DIM), F32)])
    return o, got


def _attn_bwd(q, k, v, o, do, sinks, rope, name, exchange=None):
    t = q.shape[0]
    nblk = t // ATTN_BLOCK

    def body(s_ref, q_ref, o_ref, do_ref, kp_ref, kc_ref, vp_ref, vc_ref, cosq_ref, sinq_ref, cosk_ref, sinkey_ref,
             dq_ref, dk_ref, dv_ref, ds_ref, new_k, new_v, wait_k, wait_v):
        n = pl.program_id(0)

        @pl.when(n == 0)
        def _():
            ds_ref[...] = jnp.zeros_like(ds_ref)
            wait_k[...] = jnp.zeros_like(wait_k)
            wait_v[...] = jnp.zeros_like(wait_v)

        @pl.when(n < nblk)
        def _():
            bias = _attn_bias(n == 0)
            q_t, o_t, do_t = q_ref[...].T, o_ref[...].T, do_ref[...].T
            kp_t, kc_t = kp_ref[...].T, kc_ref[...].T
            for kvh in range(N_KV_HEADS):
                q_cols = _head_cols(q_t, kvh).astype(BF16)
                do_cols = _head_cols(do_t, kvh)
                delta = jnp.sum(do_cols * _head_cols(o_t, kvh), axis=0, keepdims=True)
                do_cols = do_cols.astype(BF16)
                p, p_sink = _attn_probs(_kv_rows(kp_ref, kc_ref, kvh), q_cols, s_ref[kvh], bias)
                dp = _dot(_kv_rows(vp_ref, vc_ref, kvh), do_cols)
                ds = (p * (dp - delta) * ATTN_SCALE).astype(BF16)
                lanes = slice(GQA * HEAD_DIM * kvh, GQA * HEAD_DIM * (kvh + 1))
                dq_ref[:, lanes] = _head_rows(_dot(_kv_cols(kp_t, kc_t, kvh), ds))
                head = slice(HEAD_DIM * kvh, HEAD_DIM * (kvh + 1))
                new_k[:, head] = _dot_nt(ds, q_cols)
                new_v[:, head] = _dot_nt(p.astype(BF16), do_cols)
                ds_ref[kvh] += -(p_sink * delta)
            dq_ref[...] = _rope_apply(dq_ref[...], cosq_ref[...], sinq_ref[...], -1.0)

        @pl.when(n == nblk)
        def _():
            new_k[...] = jnp.zeros_like(new_k)
            new_v[...] = jnp.zeros_like(new_v)

        dk_ref[...] = _rope_apply(wait_k[...] + new_k[:ATTN_BLOCK], cosk_ref[...], sinkey_ref[...], -1.0)
        dv_ref[...] = wait_v[...] + new_v[:ATTN_BLOCK]
        wait_k[...] = new_k[ATTN_BLOCK:]
        wait_v[...] = new_v[ATTN_BLOCK:]

    cur = lambda i: (jnp.minimum(i, nblk - 1), 0)
    prev = lambda i: (jnp.maximum(i - 1, 0), 0)
    qs = lambda f: pl.BlockSpec((ATTN_BLOCK, Q_DIM), f)
    ks = lambda f: pl.BlockSpec((ATTN_BLOCK, KV_DIM), f)
    sink_spec = _const_spec((N_KV_HEADS, 1, KV_LANES))
    return _call_hosting(
        exchange, body, (_sink_cols(sinks), q, o, do, k, k, v, v, rope[0], rope[1], rope[0], rope[1]),
        name=name, steps=nblk + 1,
        in_specs=[sink_spec, qs(cur), qs(cur), qs(cur), ks(prev), ks(cur), ks(prev), ks(cur),
                  ks(cur), ks(cur), ks(prev), ks(prev)],
        out_specs=[qs(cur), ks(prev), ks(prev), sink_spec],
        out_shape=[jax.ShapeDtypeStruct((t, Q_DIM), F32), jax.ShapeDtypeStruct((t, KV_DIM), F32),
                   jax.ShapeDtypeStruct((t, KV_DIM), F32), jax.ShapeDtypeStruct((N_KV_HEADS, 1, KV_LANES), F32)],
        scratch_shapes=[pltpu.VMEM((2 * ATTN_BLOCK, KV_DIM), F32), pltpu.VMEM((2 * ATTN_BLOCK, KV_DIM), F32),
                        pltpu.VMEM((ATTN_BLOCK, KV_DIM), F32), pltpu.VMEM((ATTN_BLOCK, KV_DIM), F32)])


def _attn_out_fwd(o, gate, x, w_out, name, exchange=None):
    t = x.shape[0]
    rows = ROWS_FWD

    def body(o_ref, g_ref, x_ref, wo_ref, xn_ref):
        gate_v = g_ref[...]
        a = o_ref[...] * (gate_v * _sigmoid(gate_v))
        xn_ref[...] = x_ref[...] + _dot(a.astype(BF16), wo_ref[...])

    (xn,), got = _call_hosting(
        exchange, body, (o, gate, x, w_out), name=name, steps=t // rows,
        in_specs=[_row_spec(rows, D_MODEL)] * 3 + [_const_spec((D_MODEL, D_MODEL))],
        out_specs=[_row_spec(rows, D_MODEL)],
        out_shape=[jax.ShapeDtypeStruct((t, D_MODEL), F32)])
    return xn, got


def _attn_out_bwd(dxn, o, gate, w_out, name, exchange=None):
    t = o.shape[0]
    rows = ROWS_BWD

    def body(dxn_ref, o_ref, g_ref, wo_ref, do_ref, dg_ref, dwo_ref):
        @pl.when(pl.program_id(0) == 0)
        def _():
            dwo_ref[...] = jnp.zeros_like(dwo_ref)

        gate_v, ov = g_ref[...], o_ref[...]
        sgg = _sigmoid(gate_v)
        silu = gate_v * sgg
        dob = dxn_ref[...].astype(BF16)
        da = _dot_nt(dob, wo_ref[...])
        dwo_ref[...] += _dot_tn((ov * silu).astype(BF16), dob)
        do_ref[...] = da * silu
        dg_ref[...] = da * ov * (sgg * (1.0 + gate_v * (1.0 - sgg)))

    sq = _const_spec((D_MODEL, D_MODEL))
    return _call_hosting(
        exchange, body, (dxn, o, gate, w_out), name=name, steps=t // rows,
        in_specs=[_row_spec(rows, D_MODEL)] * 3 + [sq],
        out_specs=[_row_spec(rows, D_MODEL), _row_spec(rows, D_MODEL), sq],
        out_shape=[jax.ShapeDtypeStruct((t, D_MODEL), F32)] * 2 + [jax.ShapeDtypeStruct((D_MODEL, D_MODEL), F32)])


def _loss_head(x, norm, target, name):
    t = x.shape[0]
    rows = ROWS_FWD

    def body(x_ref, n_ref, t_ref, loss_ref, dx_ref, dn_ref):
        i = pl.program_id(0)
        xv = x_ref[...]
        rstd = lax.rsqrt(jnp.mean(xv * xv, axis=-1, keepdims=True) + NORM_EPS)
        xhat = xv * rstd
        err = xhat * n_ref[...] - t_ref[...]
        part = 0.5 * jnp.sum(jnp.mean(err * err, axis=-1, keepdims=True), axis=0, keepdims=True)
        dy = err * (1.0 / D_MODEL)
        dn = jnp.sum(dy * xhat, axis=0, keepdims=True)
        dxhat = dy * n_ref[...]
        dx_ref[...] = rstd * (dxhat - xhat * jnp.mean(dxhat * xhat, axis=-1, keepdims=True))

        @pl.when(i == 0)
        def _():
            loss_ref[...] = jnp.zeros((8, 128), F32) + part
            dn_ref[...] = dn

        @pl.when(i > 0)
        def _():
            loss_ref[...] += part
            dn_ref[...] += dn

    return pl.pallas_call(
        body, name=name, grid=(t // rows,),
        in_specs=[_row_spec(rows, D_MODEL), _const_spec((1, D_MODEL)), _row_spec(rows, D_MODEL)],
        out_specs=[_const_spec((8, 128)), _row_spec(rows, D_MODEL), _const_spec((1, D_MODEL))],
        out_shape=[jax.ShapeDtypeStruct((8, 128), F32), jax.ShapeDtypeStruct((t, D_MODEL), F32),
                   jax.ShapeDtypeStruct((1, D_MODEL), F32)],
        compiler_params=_params(1),
    )(x, norm.reshape(1, D_MODEL), target)


N_CHIPS = 4
N_CORES = 2
CHIP_FLIPS = ((0, 1), (1, 0), (1, 1))
ICI_CHUNKS = 2
D2D_CHUNKS = 8


def _n_chunks(rows, dtype, most):
    unit = 16 if dtype == BF16 else 8
    return max(n for n in range(1, most + 1) if rows % n == 0 and (rows // n) % unit == 0)


def _chunks_of(arrays, most):
    out = []
    for a in arrays:
        n = _n_chunks(a.shape[-2], a.dtype, most)
        out.append((n, a.shape[-2] // n))
    return out


class _Exchange:
    def __init__(self, arrays, out_shape, scratch, copies):
        self.arrays, self.out_shape, self.scratch, self._copies = arrays, out_shape, scratch, copies

    def start(self, *refs):
        for cp in self._copies(*refs)[0]:
            cp.start()

    def wait(self, *refs):
        for wait in self._copies(*refs)[1]:
            wait()


def _chips_exchange(sends, per_dest):
    n = len(sends)
    chunking = _chunks_of(sends, ICI_CHUNKS)

    def copies(send_refs, recv_refs, sems):
        x, y, c = lax.axis_index("x"), lax.axis_index("y"), lax.axis_index("c")
        me = 2 * x + y

        def peer(k):
            fx, fy = CHIP_FLIPS[k]
            px, py = x + fx - 2 * x * fx, y + fy - 2 * y * fy
            return (px, py, c), 2 * px + py

        to_start, waits = [], []
        for a in range(n):
            send_sems, recv_sems, local_sems = sems[3 * a:3 * a + 3]
            chunks, chunk_rows = chunking[a]
            for j in range(chunks):
                part = pl.ds(j * chunk_rows, chunk_rows)
                src = lambda number: send_refs[a].at[number, part] if per_dest else send_refs[a].at[part]
                for k in range(len(CHIP_FLIPS)):
                    to, to_number = peer(k)
                    remote = lambda landing: pltpu.make_async_remote_copy(
                        src_ref=src(to_number), dst_ref=recv_refs[a].at[landing, part],
                        send_sem=send_sems.at[k, j], recv_sem=recv_sems.at[k, j],
                        device_id=to, device_id_type=pl.DeviceIdType.MESH)
                    to_start.append(remote(me))
                    waits += [remote(me).wait_send, remote(to_number).wait_recv]
                own = pltpu.make_async_copy(src(me), recv_refs[a].at[me, part], local_sems.at[j])
                to_start.append(own)
                waits.append(own.wait)
        return to_start, waits

    scratch = []
    for chunks, _ in chunking:
        scratch += [pltpu.SemaphoreType.DMA((len(CHIP_FLIPS), chunks)), pltpu.SemaphoreType.DMA((len(CHIP_FLIPS), chunks)),
                    pltpu.SemaphoreType.DMA((chunks,))]
    return _Exchange(sends, [jax.ShapeDtypeStruct((N_CHIPS,) + a.shape[-2:], a.dtype) for a in sends], scratch, copies)


def _cores_exchange(sends, per_dest):
    n = len(sends)
    chunking = _chunks_of(sends, D2D_CHUNKS)

    def copies(send_refs, got_refs, sems):
        c = lax.axis_index("c")
        sibling = (lax.axis_index("x"), lax.axis_index("y"), 1 - c)
        to_start = []
        for a in range(n):
            chunks, chunk_rows = chunking[a]
            for j in range(chunks):
                part = pl.ds(j * chunk_rows, chunk_rows)
                to_start.append(pltpu.make_async_remote_copy(
                    src_ref=send_refs[a].at[1 - c, part] if per_dest else send_refs[a].at[part],
                    dst_ref=got_refs[a].at[part], send_sem=sems[2 * a].at[j], recv_sem=sems[2 * a + 1].at[j],
                    device_id=sibling, device_id_type=pl.DeviceIdType.MESH))
        return to_start, [cp.wait for cp in to_start]

    scratch = []
    for chunks, _ in chunking:
        scratch += [pltpu.SemaphoreType.DMA((chunks,)), pltpu.SemaphoreType.DMA((chunks,))]
    return _Exchange(sends, [jax.ShapeDtypeStruct(a.shape[-2:], a.dtype) for a in sends], scratch, copies)


def _run_exchange(exchange, name):
    n = len(exchange.arrays)

    def body(*refs):
        parts = refs[:n], refs[n:2 * n], refs[2 * n:]
        exchange.start(*parts)
        exchange.wait(*parts)

    hbm = pl.BlockSpec(memory_space=pltpu.HBM)
    return pl.pallas_call(body, name=name, in_specs=[hbm] * n, out_specs=[hbm] * n, out_shape=exchange.out_shape,
                          scratch_shapes=exchange.scratch)(*exchange.arrays)


def _call_hosting(exchange, body, args, *, name, steps, in_specs, out_specs, out_shape, scratch_shapes=()):
    common = dict(name=name, grid=(steps,), compiler_params=_params(1))
    if exchange is None:
        return pl.pallas_call(body, in_specs=in_specs, out_specs=out_specs, out_shape=out_shape,
                              scratch_shapes=list(scratch_shapes), **common)(*args), None
    n_in, n_out, n_scratch, k = len(in_specs), len(out_specs), len(scratch_shapes), len(exchange.arrays)

    def hosting(*refs):
        ins, sends = refs[:n_in], refs[n_in:n_in + k]
        outs, recvs = refs[n_in + k:n_in + k + n_out], refs[n_in + k + n_out:n_in + 2 * k + n_out]
        scratch = refs[n_in + 2 * k + n_out:n_in + 2 * k + n_out + n_scratch]
        sems = refs[n_in + 2 * k + n_out + n_scratch:]
        pl.when(pl.program_id(0) == 0)(lambda: exchange.start(sends, recvs, sems))
        body(*ins, *outs, *scratch)
        pl.when(pl.program_id(0) == steps - 1)(lambda: exchange.wait(sends, recvs, sems))

    hbm = pl.BlockSpec(memory_space=pltpu.HBM)
    results = pl.pallas_call(
        hosting, in_specs=list(in_specs) + [hbm] * k, out_specs=list(out_specs) + [hbm] * k,
        out_shape=list(out_shape) + exchange.out_shape, scratch_shapes=list(scratch_shapes) + exchange.scratch, **common,
    )(*args, *exchange.arrays)
    return results[:n_out], results[n_out:]


def _exchange_chips(sends, per_dest, name):
    return _run_exchange(_chips_exchange(sends, per_dest), name)


def _swap_cores(sends, per_dest, name):
    return _run_exchange(_cores_exchange(sends, per_dest), name)


def _all_gather(arrays, name):
    by_chip = _exchange_chips(arrays, False, name + "_chips")
    others = _swap_cores([r.reshape(-1, r.shape[-1]) for r in by_chip], False, name + "_cores")
    return [(m, o.reshape(m.shape)) for m, o in zip(by_chip, others)]


def _in_device_order(mine, other, axis):
    first = lax.axis_index("c") == 0
    pieces = []
    for m, o in zip(mine, other):
        pieces += [jnp.where(first, m, o), jnp.where(first, o, m)]
    return jnp.concatenate(pieces, axis=axis)


def _sum_core(send, got, out_dtype, name):
    _, n, cols = send.shape
    rows = min(n, 256)
    while n % rows:
        rows -= 16

    def body(c_ref, keep_ref, got_ref, o_ref):
        o_ref[...] = (keep_ref[...].astype(F32) + got_ref[...].astype(F32)).astype(out_dtype)

    return pl.pallas_call(
        body, name=name, out_shape=jax.ShapeDtypeStruct((n, cols), out_dtype),
        grid_spec=pltpu.PrefetchScalarGridSpec(
            num_scalar_prefetch=1, grid=(n // rows,),
            in_specs=[pl.BlockSpec((None, rows, cols), lambda i, c: (c[0], i, 0)),
                      pl.BlockSpec((rows, cols), lambda i, c: (i, 0))],
            out_specs=pl.BlockSpec((rows, cols), lambda i, c: (i, 0))),
        compiler_params=_params(1),
    )(lax.axis_index("c").astype(jnp.int32).reshape(1), send, got)


def _sum_parts(parts, out_dtype, name):
    n, cols = parts[0].shape
    rows = min(n, 256)
    while n % rows:
        rows -= 16

    def body(*refs):
        acc = refs[0][...].astype(F32)
        for ref in refs[1:-1]:
            acc = acc + ref[...].astype(F32)
        refs[-1][...] = acc.astype(out_dtype)

    return pl.pallas_call(
        body, name=name, grid=(n // rows,),
        in_specs=[_row_spec(rows, cols)] * len(parts),
        out_specs=_row_spec(rows, cols),
        out_shape=jax.ShapeDtypeStruct((n, cols), out_dtype),
        compiler_params=_params(1),
    )(*parts)


def _reduce_scatter(sends, wire_dtypes, name):
    halves = [s.reshape(N_CORES, N_CHIPS * s.shape[2], s.shape[3]) for s in sends]
    gots = _swap_cores(halves, True, name + "_cores")
    sums = [_sum_core(h, g, dt, "%s_core_sum%d" % (name, i)).reshape((N_CHIPS,) + s.shape[2:])
            for i, (h, g, dt, s) in enumerate(zip(halves, gots, wire_dtypes, sends))]
    return _exchange_chips(sums, True, name + "_chips")


def _adamw(parts, w, m, v, name):
    n, cols = w.shape
    k = parts.shape[0]
    rows = min(n, 256)
    while n % rows:
        rows -= 8
    c1 = 1.0 - ADAM_B1 ** ADAM_STEP
    c2 = 1.0 - ADAM_B2 ** ADAM_STEP

    def body(p_ref, w_ref, m_ref, v_ref, g_ref, d_ref, nm_ref, nv_ref):
        g = p_ref[0].astype(F32)
        for s in range(1, k):
            g = g + p_ref[s].astype(F32)
        nm = ADAM_B1 * m_ref[...] + (1.0 - ADAM_B1) * g
        nv = ADAM_B2 * v_ref[...] + (1.0 - ADAM_B2) * (g * g)
        g_ref[...] = g
        nm_ref[...] = nm
        nv_ref[...] = nv
        d_ref[...] = -ADAM_LR * ((nm / c1) / (jnp.sqrt(nv / c2) + ADAM_EPS) + ADAM_WD * w_ref[...])

    blk = _row_spec(rows, cols)
    return pl.pallas_call(
        body, name=name, grid=(n // rows,),
        in_specs=[pl.BlockSpec((k, rows, cols), lambda i: (0, i, 0)), blk, blk, blk],
        out_specs=[blk] * 4,
        out_shape=[jax.ShapeDtypeStruct((n, cols), F32)] * 4,
        compiler_params=_params(1),
    )(parts, w, m, v)


SSM_KEYS = ("norm", "w_in", "a_re", "a_im", "log_step", "b_re", "b_im", "c_re", "c_im", "d", "w_glu", "b_glu", "w_out")
ATTN_KEYS = ("norm", "w_in", "sinks", "w_out")
LAYER_KEYS = (SSM_KEYS, ATTN_KEYS, SSM_KEYS, ATTN_KEYS)
BIG_KEYS = ("w_in", "w_glu", "w_out")
ATTN_SPLITS = (Q_DIM, KV_DIM, KV_DIM, D_MODEL)


def _rope_tables(t):
    pos = jnp.arange(t, dtype=F32)
    inv_freq = ROPE_THETA ** (-jnp.arange(0, HEAD_DIM, 2, dtype=F32) / HEAD_DIM)
    ang = pos[:, None] * inv_freq[None, :]
    cos, sin = jnp.cos(ang), jnp.sin(ang)
    return jnp.tile(jnp.concatenate([cos, cos], axis=1), (1, 2)), jnp.tile(jnp.concatenate([-sin, sin], axis=1), (1, 2))


def _ssm_layer_fwd(i, x, p, w):
    tag = "l%d_" % i
    mats, mats_vjp = jax.vjp(_s5_matrices, p["a_re"], p["a_im"], p["log_step"], p["b_re"], p["b_im"], p["c_re"], p["c_im"])
    kern, cpt, bpt, ar, ai = mats
    tm = _s5_toeplitz(kern)
    mb = dict(tm=tm.astype(BF16), tmt=jnp.swapaxes(tm, 1, 2).astype(BF16), cpt=cpt.astype(BF16),
              cp=jnp.swapaxes(cpt, 1, 2).astype(BF16), bpt=bpt.astype(BF16), bp=jnp.swapaxes(bpt, 1, 2).astype(BF16))
    u, gate = _inproj_fwd(x, p["norm"], w["w_in"], (D_MODEL, D_MODEL), None, tag + "inproj_fwd")
    xre, xim = _s5_project(u, mb["bpt"], tag + "s5_block_inputs")
    hre, him = _s5_scan_fwd(xre, xim, ar, ai, tag + "s5_scan_fwd")
    y = _s5_outputs(u, hre, him, mb["tm"], mb["cpt"], p["d"], tag + "s5_outputs")
    xn = _ssm_out_fwd(y, gate, x, w["w_glu"], p["b_glu"], w["w_out"], tag + "out_fwd")
    return xn, (x, u, gate, y, hre, him, mb, ar, ai, mats_vjp)


def _ssm_layer_bwd(i, dxn, saved, p, w):
    tag = "l%d_" % i
    x, u, gate, y, hre, him, mb, ar, ai, mats_vjp = saved
    dy, dgate, dw_glu, db_glu, dw_out = _ssm_out_bwd(dxn, y, gate, w["w_glu"], p["b_glu"], w["w_out"], tag + "out_bwd")
    dhre, dhim = _s5_project(dy, mb["cp"], tag + "s5_state_grads")
    dxre, dxim, dar, dai = _s5_scan_bwd(dhre, dhim, hre, him, ar, ai, tag + "s5_scan_bwd")
    du, dk, dcpt, dbpt, dd = _s5_backward(dy, u, hre, him, dxre, dxim, mb["tmt"], mb["bp"], p["d"], tag + "s5_backward")
    dk = dk.reshape(SSM_GROUPS, S5_BLOCK, SSM_GROUP, SSM_GROUP)
    da_re, da_im, dlog_step, db_re, db_im, dc_re, dc_im = mats_vjp((dk, dcpt, dbpt, dar, dai))
    dx, dw_in, dnorm = _inproj_bwd(x, p["norm"], w["w_in"], [du, dgate], dxn, tag + "inproj_bwd")
    grads = dict(norm=dnorm.reshape(D_MODEL), w_in=dw_in, a_re=da_re, a_im=da_im, log_step=dlog_step, b_re=db_re,
                 b_im=db_im, c_re=dc_re, c_im=dc_im, d=dd.reshape(D_MODEL), w_glu=dw_glu, b_glu=db_glu.reshape(D_MODEL),
                 w_out=dw_out)
    return dx, grads


def _attn_layer_fwd(i, x, p, w, rope, gather=None):
    tag = "l%d_" % i
    q, k, v, gate = _inproj_fwd(x, p["norm"], w["w_in"], ATTN_SPLITS, rope, tag + "inproj_fwd")
    if gather is None:
        o, _ = _attn_fwd(q, k, v, p["sinks"], tag + "attn_fwd")
        xn, _ = _attn_out_fwd(o, gate, x, w["w_out"], tag + "out_fwd")
        return xn, (x, q, k, v, gate, o), None
    o, by_chip = _attn_fwd(q, k, v, p["sinks"], tag + "attn_fwd", _chips_exchange(gather, False))
    flat = [r.reshape(-1, r.shape[-1]) for r in by_chip]
    xn, others = _attn_out_fwd(o, gate, x, w["w_out"], tag + "out_fwd", _cores_exchange(flat, False))
    return xn, (x, q, k, v, gate, o), [(m, other.reshape(m.shape)) for m, other in zip(by_chip, others)]


def _attn_layer_bwd(i, dxn, saved, p, w, rope, scatter=None):
    tag = "l%d_" % i
    x, q, k, v, gate, o = saved
    if scatter is None:
        (do, dgate, dw_out), _ = _attn_out_bwd(dxn, o, gate, w["w_out"], tag + "out_bwd")
        (dq, dk, dv, dsinks), parts = _attn_bwd(q, k, v, o, do, p["sinks"], rope, tag + "attn_bwd")
    else:
        sends, wire_dtypes = scatter
        halves = [s.reshape(N_CORES, N_CHIPS * s.shape[2], s.shape[3]) for s in sends]
        (do, dgate, dw_out), gots = _attn_out_bwd(dxn, o, gate, w["w_out"], tag + "out_bwd", _cores_exchange(halves, True))
        sums = [_sum_core(h, g, dt, "%sscatter_core_sum%d" % (tag, j)).reshape((N_CHIPS,) + s.shape[2:])
                for j, (h, g, dt, s) in enumerate(zip(halves, gots, wire_dtypes, sends))]
        (dq, dk, dv, dsinks), parts = _attn_bwd(q, k, v, o, do, p["sinks"], rope, tag + "attn_bwd", _chips_exchange(sums, True))
    dx, dw_in, dnorm = _inproj_bwd(x, p["norm"], w["w_in"], [dq, dk, dv, dgate], dxn, tag + "inproj_bwd")
    grads = dict(norm=dnorm.reshape(D_MODEL), w_in=dw_in, sinks=dsinks.reshape(N_Q_HEADS, ATTN_BLOCK).sum(axis=1), w_out=dw_out)
    return dx, grads, parts


def _local_step(x, target, small, big, late_weights=None, late_sends=None):
    rope = _rope_tables(x.shape[0])
    big = list(big)
    saved = []
    for i in range(4):
        if i % 2 == 0:
            x, s = _ssm_layer_fwd(i, x, small[i], big[i])
        elif i == 1 and late_weights is not None:
            x, s, gathered = _attn_layer_fwd(i, x, small[i], big[i], rope, late_weights[0])
            big[2], big[3] = late_weights[1](gathered)
        else:
            x, s, _ = _attn_layer_fwd(i, x, small[i], big[i], rope)
        saved.append(s)
    loss, dx, dfinal = _loss_head(x, small[4]["norm"], target, "loss_head")
    grads = [None] * 4 + [dict(norm=dfinal.reshape(D_MODEL))]
    late_parts = None
    for i in (3, 2, 1, 0):
        if i % 2 == 0:
            dx, grads[i] = _ssm_layer_bwd(i, dx, saved[i], small[i], big[i])
        elif i == 1 and late_sends is not None:
            dx, grads[i], late_parts = _attn_layer_bwd(i, dx, saved[i], small[i], big[i], rope, late_sends(grads))
        else:
            dx, grads[i], _ = _attn_layer_bwd(i, dx, saved[i], small[i], big[i], rope)
    return loss[0, 0], dx, grads, late_parts


def _owner_major(key, g):
    if key == "w_in":
        return g.reshape(D_MODEL, N_CHIPS, N_CORES, -1).transpose(2, 1, 0, 3)
    return g.reshape(N_CHIPS, N_CORES, -1, D_MODEL).transpose(1, 0, 2, 3)


def _from_gathered(key, mine, other):
    return _in_device_order(list(mine), list(other), 1 if key == "w_in" else 0)


SMALL_ROWS = 72


def _rows_of_small(a):
    flat = a.reshape(-1)
    return jnp.pad(flat, (0, -flat.shape[0] % D_MODEL)).reshape(-1, D_MODEL)


def _stack_small(arrays):
    rows = jnp.concatenate([_rows_of_small(a) for a in arrays], axis=0)
    assert rows.shape[0] <= N_DEV * SMALL_ROWS
    return jnp.pad(rows, ((0, N_DEV * SMALL_ROWS - rows.shape[0]), (0, 0)))


def kernel(*args):
    names = ["x"]
    layer_names = []
    for i, keys in enumerate(LAYER_KEYS):
        layer_names += ["l%d_%s" % (i, k) for k in keys]
    layer_names.append("final_norm")
    names += layer_names + ["loss_target"] + ["m_" + n for n in layer_names] + ["v_" + n for n in layer_names]
    given = dict(zip(names, args))
    big_names = [n for n in layer_names if n.split("_", 1)[1] in BIG_KEYS]
    small_names = [n for n in layer_names if n not in big_names]

    offsets = {}

    def families_of(layers):
        families = {}
        for n in big_names:
            if int(n[1]) in layers:
                family = families.setdefault(given[n].shape[1], [])
                offsets[n] = sum(given[other].shape[0] for other in family)
                family.append(n)
        return list(families.values())

    early, late = families_of((0, 1)), families_of((2, 3))
    stack = lambda pre, family: jnp.concatenate([given[pre + n] for n in family], axis=0)
    rows_of = lambda a, n: a[..., offsets[n]:offsets[n] + given[n].shape[0], :]
    local = lambda families: [stack("", family).astype(BF16) for family in families]

    def assemble(families, gathered):
        big = [dict() for _ in range(4)]
        for family, (mine, other) in zip(families, gathered):
            for n in family:
                big[int(n[1])][n.split("_", 1)[1]] = _from_gathered(n.split("_", 1)[1], rows_of(mine, n), rows_of(other, n))
        return big

    def sends_of(families, grads):
        return [jnp.concatenate([_owner_major(n.split("_", 1)[1], grads[int(n[1])][n.split("_", 1)[1]]) for n in family], axis=2)
                for family in families]

    small = [dict() for _ in range(5)]
    for n in small_names:
        if n == "final_norm":
            small[4]["norm"] = given[n]
        else:
            small[int(n[1])][n.split("_", 1)[1]] = given[n]

    big = assemble(early, _all_gather(local(early), "gather_early_weights"))
    loss, dx, grads, late_parts = _local_step(
        given["x"][0], given["loss_target"][0], small, big,
        late_weights=(local(late), lambda gathered: assemble(late, gathered)[2:]),
        late_sends=lambda grads: (sends_of(late, grads), [BF16] * len(late)))
    loss = lax.psum(loss, ("x", "y", "c"))

    def grad_of(n):
        return grads[4]["norm"] if n == "final_norm" else grads[int(n[1])][n.split("_", 1)[1]]

    flat = lambda f: _stack_small([f(n) for n in small_names])
    sends = sends_of(early, grads) + [flat(grad_of).reshape(N_CORES, N_CHIPS, SMALL_ROWS, D_MODEL)]
    parts = _reduce_scatter(sends, [BF16] * len(early) + [F32], "scatter_grads")

    outs = {}
    tags = ("grad_", "delta_", "new_m_", "new_v_")
    for i, (family, part) in enumerate(zip(early + late, parts[:-1] + late_parts)):
        results = _adamw(part, stack("", family), stack("m_", family), stack("v_", family), "adamw_matrices%d" % i)
        for tag, a in zip(tags, results):
            for n in family:
                outs[tag + n] = rows_of(a, n)

    my_slice = _sum_parts([parts[-1][s] for s in range(N_CHIPS)], F32, "sum_small_grads")
    mine, other = _all_gather([my_slice], "gather_small_grads")[0]
    first = lax.axis_index("c") == 0
    g_all = jnp.concatenate([jnp.where(first, mine, other), jnp.where(first, other, mine)], axis=0)
    g_all = g_all.reshape(1, N_DEV * SMALL_ROWS, D_MODEL)
    results = _adamw(g_all, flat(lambda n: given[n]), flat(lambda n: given["m_" + n]), flat(lambda n: given["v_" + n]),
                     "adamw_small")
    for tag, a in zip(tags, results):
        at = 0
        for n in small_names:
            rows = -(-given[n].size // D_MODEL)
            outs[tag + n] = a[at:at + rows].reshape(-1)[:given[n].size].reshape(given[n].shape)
            at += rows
    result = [loss, dx[None]]
    for tag in ("grad_", "delta_", "new_m_", "new_v_"):
        result += [outs[tag + n] for n in layer_names]
    return tuple(result)
```

```python
import functools
import math

import jax
import jax.numpy as jnp
from jax import lax
from jax.experimental import pallas as pl
from jax.experimental.pallas import tpu as pltpu

F32 = jnp.float32
BF16 = jnp.bfloat16

D_MODEL = 1024
SSM_GROUP = 16
SSM_GROUPS = D_MODEL // SSM_GROUP
SSM_STATE = 64
S5_BLOCK = 16
S5_LANES = S5_BLOCK * SSM_GROUP
HEAD_DIM = 64
N_Q_HEADS = 16
N_KV_HEADS = 2
GQA = N_Q_HEADS // N_KV_HEADS
Q_DIM = N_Q_HEADS * HEAD_DIM
KV_DIM = N_KV_HEADS * HEAD_DIM
ATTN_BLOCK = 128
ROPE_THETA = 10000.0
NORM_EPS = 1e-5
NEG_INF = -1e30
ATTN_SCALE = HEAD_DIM ** -0.5
N_DEV = 8

ADAM_LR = 0.001
ADAM_B1 = 0.9
ADAM_B2 = 0.999
ADAM_EPS = 1e-08
ADAM_WD = 0.01
ADAM_STEP = 10

VMEM_LIMIT = 56 * 1024 * 1024
ROWS_FWD = 512
ROWS_BWD = 512

NT = (((1,), (1,)), ((), ()))
TN = (((0,), (0,)), ((), ()))


def _params(n_grid):
    return pltpu.CompilerParams(dimension_semantics=("arbitrary",) * n_grid, vmem_limit_bytes=VMEM_LIMIT)


def _dot(a, b):
    return jnp.dot(a, b, preferred_element_type=F32)


def _dot_nt(a, b):
    return lax.dot_general(a, b, NT, preferred_element_type=F32)


def _dot_tn(a, b):
    return lax.dot_general(a, b, TN, preferred_element_type=F32)


def _sigmoid(x):
    return 1.0 / (1.0 + jnp.exp(-x))


_GELU_K = math.sqrt(2.0 / math.pi)


def _gelu(x):
    return x * (0.5 * (1.0 + jnp.tanh(_GELU_K * (x + 0.044715 * (x * x * x)))))


def _gelu_grad(x):
    t = jnp.tanh(_GELU_K * (x + 0.044715 * (x * x * x)))
    return 0.5 * (1.0 + t) + 0.5 * x * (1.0 - t * t) * (_GELU_K * (1.0 + 3.0 * 0.044715 * (x * x)))


def _row_spec(rows, cols):
    return pl.BlockSpec((rows, cols), lambda i: (i, 0))


def _const_spec(shape):
    zeros = (0,) * len(shape)
    return pl.BlockSpec(shape, lambda i: zeros, pipeline_mode=pl.Buffered(1))


def _rope_apply(t, cos, sin_signed, sign):
    lane = lax.broadcasted_iota(jnp.int32, (1, 128), 1)
    first_half = (lane % HEAD_DIM) < (HEAD_DIM // 2)
    out = []
    for j in range(t.shape[1] // 128):
        tj = t[:, 128 * j:128 * (j + 1)]
        partner = jnp.where(first_half, pltpu.roll(tj, 128 - HEAD_DIM // 2, 1), pltpu.roll(tj, HEAD_DIM // 2, 1))
        out.append(tj * cos + sign * (partner * sin_signed))
    return out[0] if len(out) == 1 else jnp.concatenate(out, axis=1)


def _inproj_fwd(x, norm, w, splits, dtypes, rope, name):
    t = x.shape[0]
    n = w.shape[1]
    rows = ROWS_FWD

    def body(*refs):
        if rope is None:
            x_ref, n_ref, w_ref = refs[:3]
            outs = refs[3:]
        else:
            x_ref, n_ref, w_ref, cos_ref, sin_ref = refs[:5]
            outs = refs[5:]
        xv = x_ref[...]
        rstd = lax.rsqrt(jnp.mean(xv * xv, axis=-1, keepdims=True) + NORM_EPS)
        h = (xv * rstd) * n_ref[...]
        proj = _dot(h.astype(BF16), w_ref[...])
        off = 0
        for i, width in enumerate(splits):
            piece = proj[:, off:off + width]
            if rope is not None and i < 2:
                piece = _rope_apply(piece, cos_ref[...], sin_ref[...], 1.0)
            outs[i][...] = piece.astype(dtypes[i])
            off += width

    in_specs = [_row_spec(rows, D_MODEL), _const_spec((1, D_MODEL)), _const_spec((D_MODEL, n))]
    args = [x, norm.reshape(1, D_MODEL), w]
    if rope is not None:
        in_specs += [_row_spec(rows, 128), _row_spec(rows, 128)]
        args += list(rope)
    return pl.pallas_call(
        body, name=name, grid=(t // rows,), in_specs=in_specs,
        out_specs=[_row_spec(rows, width) for width in splits],
        out_shape=[jax.ShapeDtypeStruct((t, width), dtype) for width, dtype in zip(splits, dtypes)],
        compiler_params=_params(1),
    )(*args)


def _inproj_bwd(x, norm, w, dpieces, dxn, name):
    t = x.shape[0]
    n = w.shape[1]
    rows = ROWS_BWD
    widths = [p.shape[1] for p in dpieces]
    k = len(dpieces)

    def body(*refs):
        x_ref, n_ref, w_ref, dxn_ref = refs[:4]
        d_refs = refs[4:4 + k]
        dx_ref, dw_ref, dn_ref = refs[4 + k:]
        @pl.when(pl.program_id(0) == 0)
        def _():
            dw_ref[...] = jnp.zeros_like(dw_ref)
            dn_ref[...] = jnp.zeros_like(dn_ref)

        xv = x_ref[...]
        rstd = lax.rsqrt(jnp.mean(xv * xv, axis=-1, keepdims=True) + NORM_EPS)
        xhat = xv * rstd
        h = xhat * n_ref[...]
        dproj = [r[...].astype(BF16) for r in d_refs]
        dproj = dproj[0] if k == 1 else jnp.concatenate(dproj, axis=1)
        dh = _dot_nt(dproj, w_ref[...])
        dw_ref[...] += _dot_tn(h.astype(BF16), dproj)
        dn_ref[...] += jnp.sum(dh * xhat, axis=0, keepdims=True)
        dxhat = dh * n_ref[...]
        dx_ref[...] = rstd * (dxhat - xhat * jnp.mean(dxhat * xhat, axis=-1, keepdims=True)) + dxn_ref[...]

    return pl.pallas_call(
        body, name=name, grid=(t // rows,),
        in_specs=[_row_spec(rows, D_MODEL), _const_spec((1, D_MODEL)), _const_spec((D_MODEL, n)),
                  _row_spec(rows, D_MODEL)] + [_row_spec(rows, width) for width in widths],
        out_specs=[_row_spec(rows, D_MODEL), _const_spec((D_MODEL, n)), _const_spec((1, D_MODEL))],
        out_shape=[jax.ShapeDtypeStruct((t, D_MODEL), F32), jax.ShapeDtypeStruct((D_MODEL, n), F32),
                   jax.ShapeDtypeStruct((1, D_MODEL), F32)],
        compiler_params=_params(1),
    )(x, norm.reshape(1, D_MODEL), w, dxn, *dpieces)


def _s5_matrices(a_re, a_im, log_step, b_re, b_im, c_re, c_im):
    r = S5_BLOCK
    step = jnp.exp(log_step)[:, None]
    lr, li = a_re * step, a_im * step
    k = jnp.arange(r + 1, dtype=F32)
    mag = jnp.exp(lr[:, None, :] * k[:, None])
    pr = mag * jnp.cos(li[:, None, :] * k[:, None])
    pi = mag * jnp.sin(li[:, None, :] * k[:, None])
    nr, ni = pr[:, 1] - 1.0, pi[:, 1]
    den = a_re * a_re + a_im * a_im
    qr, qi = (nr * a_re + ni * a_im) / den, (ni * a_re - nr * a_im) / den
    bbr = qr[..., None] * b_re - qi[..., None] * b_im
    bbi = qr[..., None] * b_im + qi[..., None] * b_re
    wr = c_re[:, None] * pr[:, :, None, :] - c_im[:, None] * pi[:, :, None, :]
    wi = c_re[:, None] * pi[:, :, None, :] + c_im[:, None] * pr[:, :, None, :]
    w = jnp.concatenate([wr, -wi], axis=-1)
    bb = jnp.concatenate([bbr, bbi], axis=1)
    kern = jnp.einsum("gxp,gpi->gxi", w[:, :r].reshape(SSM_GROUPS, S5_LANES, 2 * SSM_STATE), bb,
                      precision=lax.Precision.HIGHEST).reshape(SSM_GROUPS, r, SSM_GROUP, SSM_GROUP)
    cpt = w[:, 1:].reshape(SSM_GROUPS, S5_LANES, 2 * SSM_STATE)
    prs = jnp.swapaxes(pr[:, r - 1::-1][:, :r], 1, 2)[..., None]
    pis = jnp.swapaxes(pi[:, r - 1::-1][:, :r], 1, 2)[..., None]
    bp_re = prs * bbr[:, :, None, :] - pis * bbi[:, :, None, :]
    bp_im = prs * bbi[:, :, None, :] + pis * bbr[:, :, None, :]
    bpt = jnp.concatenate([bp_re, bp_im], axis=1).reshape(SSM_GROUPS, 2 * SSM_STATE, S5_LANES)
    ar = pr[:, r].reshape(1, SSM_GROUPS * SSM_STATE)
    ai = pi[:, r].reshape(1, SSM_GROUPS * SSM_STATE)
    return kern, cpt, bpt, ar, ai


def _s5_toeplitz(kern):
    r = S5_BLOCK
    cols = [jnp.pad(kern[:, :r - s], ((0, 0), (s, 0), (0, 0), (0, 0))) for s in range(r)]
    return jnp.stack(cols, axis=3).reshape(SSM_GROUPS, S5_LANES, S5_LANES)


S5_OCTET = 128 // SSM_GROUP
S5_STEPS = SSM_GROUPS // S5_OCTET


def _oct_spec(t):
    return pl.BlockSpec((t, 128), lambda j: (0, j))


def _state_spec(nb):
    return pl.BlockSpec((nb, S5_OCTET * SSM_STATE), lambda j: (0, j))


def _gmat_spec(a, b):
    return pl.BlockSpec((S5_OCTET, a, b), lambda j: (j, 0, 0))


def _block_rows(ref, nb):
    return [ref[pl.ds(r, nb, stride=S5_BLOCK), :] for r in range(S5_BLOCK)]


def _group_cols(pieces_t, g):
    return jnp.concatenate([p[SSM_GROUP * g:SSM_GROUP * (g + 1)] for p in pieces_t], axis=0)


def _state_cols(re_t, im_t, g):
    return jnp.concatenate([re_t[SSM_STATE * g:SSM_STATE * (g + 1)], im_t[SSM_STATE * g:SSM_STATE * (g + 1)]], axis=0)


def _s5_project(a, mat, name):
    t = a.shape[0]
    nb = t // S5_BLOCK

    def body(a_ref, m_ref, re_ref, im_ref):
        at = [p.T for p in _block_rows(a_ref, nb)]
        for pair in range(S5_OCTET // 2):
            xs = [_dot(m_ref[2 * pair + k], _group_cols(at, 2 * pair + k).astype(BF16)) for k in (0, 1)]
            lanes = slice(128 * pair, 128 * (pair + 1))
            re_ref[:, lanes] = jnp.concatenate([xs[0][:SSM_STATE], xs[1][:SSM_STATE]], axis=0).T
            im_ref[:, lanes] = jnp.concatenate([xs[0][SSM_STATE:], xs[1][SSM_STATE:]], axis=0).T

    return pl.pallas_call(
        body, name=name, grid=(S5_STEPS,),
        in_specs=[_oct_spec(t), _gmat_spec(2 * SSM_STATE, S5_LANES)],
        out_specs=[_state_spec(nb), _state_spec(nb)],
        out_shape=[jax.ShapeDtypeStruct((nb, SSM_GROUPS * SSM_STATE), F32)] * 2,
        compiler_params=_params(1),
    )(a, mat)


_SCAN_LANES = 1024


def _s5_scan_fwd(xre, xim, ar, ai, name):
    nb = xre.shape[0]
    col = pl.BlockSpec((nb, _SCAN_LANES), lambda j: (0, j))
    par = pl.BlockSpec((1, _SCAN_LANES), lambda j: (0, j))

    def body(xre_ref, xim_ref, ar_ref, ai_ref, hre_ref, him_ref):
        a_r, a_i = ar_ref[...], ai_ref[...]

        def step(b, carry):
            hr, hi = carry
            hre_ref[pl.ds(b, 1), :] = hr
            him_ref[pl.ds(b, 1), :] = hi
            xr, xi = xre_ref[pl.ds(b, 1), :], xim_ref[pl.ds(b, 1), :]
            return a_r * hr - a_i * hi + xr, a_r * hi + a_i * hr + xi

        zero = jnp.zeros((1, _SCAN_LANES), F32)
        lax.fori_loop(0, nb, step, (zero, zero))

    return pl.pallas_call(
        body, name=name, grid=(xre.shape[1] // _SCAN_LANES,),
        in_specs=[col, col, par, par], out_specs=[col, col],
        out_shape=[jax.ShapeDtypeStruct(xre.shape, F32)] * 2,
        compiler_params=_params(1),
    )(xre, xim, ar, ai)


def _s5_scan_bwd(dhre, dhim, hre, him, ar, ai, name):
    nb = dhre.shape[0]
    col = pl.BlockSpec((nb, _SCAN_LANES), lambda j: (0, j))
    par = pl.BlockSpec((1, _SCAN_LANES), lambda j: (0, j))

    def body(dhre_ref, dhim_ref, hre_ref, him_ref, ar_ref, ai_ref, dxre_ref, dxim_ref, dar_ref, dai_ref):
        a_r, a_i = ar_ref[...], ai_ref[...]

        def step(s, carry):
            gr, gi, dar, dai = carry
            b = nb - 1 - s
            dxre_ref[pl.ds(b, 1), :] = gr
            dxim_ref[pl.ds(b, 1), :] = gi
            hr, hi = hre_ref[pl.ds(b, 1), :], him_ref[pl.ds(b, 1), :]
            dar = dar + (hr * gr + hi * gi)
            dai = dai + (hr * gi - hi * gr)
            dr, di = dhre_ref[pl.ds(b, 1), :], dhim_ref[pl.ds(b, 1), :]
            return dr + (a_r * gr + a_i * gi), di + (a_r * gi - a_i * gr), dar, dai

        zero = jnp.zeros((1, _SCAN_LANES), F32)
        _, _, dar, dai = lax.fori_loop(0, nb, step, (zero, zero, zero, zero))
        dar_ref[...] = dar
        dai_ref[...] = dai

    return pl.pallas_call(
        body, name=name, grid=(dhre.shape[1] // _SCAN_LANES,),
        in_specs=[col, col, col, col, par, par], out_specs=[col, col, par, par],
        out_shape=[jax.ShapeDtypeStruct(dhre.shape, F32)] * 2 + [jax.ShapeDtypeStruct(ar.shape, F32)] * 2,
        compiler_params=_params(1),
    )(dhre, dhim, hre, him, ar, ai)


def _s5_outputs(u, hre, him, tm, cpt, d, name):
    t = u.shape[0]
    nb = t // S5_BLOCK

    def body(u_ref, hre_ref, him_ref, tm_ref, cpt_ref, d_ref, y_ref):
        u_rows = _block_rows(u_ref, nb)
        ut = [p.T for p in u_rows]
        hre_t, him_t = hre_ref[...].T, him_ref[...].T
        yts = []
        for g in range(S5_OCTET):
            yts.append(_dot(tm_ref[g], _group_cols(ut, g).astype(BF16))
                       + _dot(cpt_ref[g], _state_cols(hre_t, him_t, g).astype(BF16)))
        for r in range(S5_BLOCK):
            rows = jnp.concatenate([yt[SSM_GROUP * r:SSM_GROUP * (r + 1)] for yt in yts], axis=0)
            y_ref[pl.ds(r, nb, stride=S5_BLOCK), :] = rows.T + d_ref[...] * u_rows[r]

    return pl.pallas_call(
        body, name=name, grid=(S5_STEPS,),
        in_specs=[_oct_spec(t), _state_spec(nb), _state_spec(nb), _gmat_spec(S5_LANES, S5_LANES),
                  _gmat_spec(S5_LANES, 2 * SSM_STATE), _oct_spec(1)],
        out_specs=_oct_spec(t),
        out_shape=jax.ShapeDtypeStruct(u.shape, F32),
        compiler_params=_params(1),
    )(u, hre, him, tm, cpt, d.reshape(1, D_MODEL))


def _s5_backward(dy, u, hre, him, dxre, dxim, tmt, bp, d, name):
    t = u.shape[0]
    nb = t // S5_BLOCK

    def body(dy_ref, u_ref, hre_ref, him_ref, dxre_ref, dxim_ref, tmt_ref, bp_ref, d_ref,
             du_ref, dk_ref, dcpt_ref, dbpt_ref, dd_ref, dtm_scratch):
        dy_rows, u_rows = _block_rows(dy_ref, nb), _block_rows(u_ref, nb)
        dyt, ut = [p.T for p in dy_rows], [p.T for p in u_rows]
        hre_t, him_t = hre_ref[...].T, him_ref[...].T
        dxre_t, dxim_t = dxre_ref[...].T, dxim_ref[...].T
        duts = []
        for g in range(S5_OCTET):
            dyg, ug = _group_cols(dyt, g).astype(BF16), _group_cols(ut, g).astype(BF16)
            hg = _state_cols(hre_t, him_t, g).astype(BF16)
            dxg = _state_cols(dxre_t, dxim_t, g).astype(BF16)
            duts.append(_dot(tmt_ref[g], dyg) + _dot(bp_ref[g], dxg))
            dtm_scratch[...] = _dot_nt(dyg, ug)
            dk = dtm_scratch[:, :SSM_GROUP]
            for s in range(1, S5_BLOCK):
                below = dtm_scratch[SSM_GROUP * s:, SSM_GROUP * s:SSM_GROUP * (s + 1)]
                dk = dk + jnp.concatenate([below, jnp.zeros((SSM_GROUP * s, SSM_GROUP), F32)], axis=0)
            dk_ref[g] = dk
            dcpt_ref[g] = _dot_nt(dyg, hg)
            dbpt_ref[g] = _dot_nt(dxg, ug)
        dd = jnp.zeros((1, 128), F32)
        for r in range(S5_BLOCK):
            rows = jnp.concatenate([dut[SSM_GROUP * r:SSM_GROUP * (r + 1)] for dut in duts], axis=0)
            du_ref[pl.ds(r, nb, stride=S5_BLOCK), :] = rows.T + d_ref[...] * dy_rows[r]
            dd = dd + jnp.sum(dy_rows[r] * u_rows[r], axis=0, keepdims=True)
        dd_ref[...] = dd

    return pl.pallas_call(
        body, name=name, grid=(S5_STEPS,),
        in_specs=[_oct_spec(t), _oct_spec(t), _state_spec(nb), _state_spec(nb), _state_spec(nb), _state_spec(nb),
                  _gmat_spec(S5_LANES, S5_LANES), _gmat_spec(S5_LANES, 2 * SSM_STATE), _oct_spec(1)],
        out_specs=[_oct_spec(t), _gmat_spec(S5_LANES, SSM_GROUP), _gmat_spec(S5_LANES, 2 * SSM_STATE),
                   _gmat_spec(2 * SSM_STATE, S5_LANES), _oct_spec(1)],
        out_shape=[jax.ShapeDtypeStruct(u.shape, F32),
                   jax.ShapeDtypeStruct((SSM_GROUPS, S5_LANES, SSM_GROUP), F32),
                   jax.ShapeDtypeStruct((SSM_GROUPS, S5_LANES, 2 * SSM_STATE), F32),
                   jax.ShapeDtypeStruct((SSM_GROUPS, 2 * SSM_STATE, S5_LANES), F32),
                   jax.ShapeDtypeStruct((1, D_MODEL), F32)],
        scratch_shapes=[pltpu.VMEM((S5_LANES, S5_LANES), F32)],
        compiler_params=_params(1),
    )(dy, u, hre, him, dxre, dxim, tmt, bp, d.reshape(1, D_MODEL))


def _ssm_out_fwd(y, gate, x, w_glu, b_glu, w_out, name):
    t = x.shape[0]
    rows = ROWS_FWD

    def body(y_ref, g_ref, x_ref, wg_ref, bg_ref, wo_ref, o_ref):
        z0 = _gelu(y_ref[...])
        s = _dot(z0.astype(BF16), wg_ref[...]) + bg_ref[...]
        gate_v = g_ref[...]
        a = (z0 * _sigmoid(s)) * (gate_v * _sigmoid(gate_v))
        o_ref[...] = x_ref[...] + _dot(a.astype(BF16), wo_ref[...])

    return pl.pallas_call(
        body, name=name, grid=(t // rows,),
        in_specs=[_row_spec(rows, D_MODEL)] * 3 + [_const_spec((D_MODEL, D_MODEL)), _const_spec((1, D_MODEL)),
                                                   _const_spec((D_MODEL, D_MODEL))],
        out_specs=_row_spec(rows, D_MODEL),
        out_shape=jax.ShapeDtypeStruct((t, D_MODEL), F32),
        compiler_params=_params(1),
    )(y, gate, x, w_glu, b_glu.reshape(1, D_MODEL), w_out)


def _ssm_out_bwd(dxn, y, gate, w_glu, b_glu, w_out, name):
    t = y.shape[0]
    rows = ROWS_BWD

    def body(dxn_ref, y_ref, g_ref, wg_ref, bg_ref, wo_ref, dy_ref, dg_ref, dwg_ref, dbg_ref, dwo_ref):
        @pl.when(pl.program_id(0) == 0)
        def _():
            dwo_ref[...] = jnp.zeros_like(dwo_ref)
            dwg_ref[...] = jnp.zeros_like(dwg_ref)
            dbg_ref[...] = jnp.zeros_like(dbg_ref)

        yv = y_ref[...]
        z0 = _gelu(yv)
        z0b = z0.astype(BF16)
        sg = _sigmoid(_dot(z0b, wg_ref[...]) + bg_ref[...])
        z = z0 * sg
        gate_v = g_ref[...]
        sgg = _sigmoid(gate_v)
        silu = gate_v * sgg
        dob = dxn_ref[...].astype(BF16)
        da = _dot_nt(dob, wo_ref[...])
        dwo_ref[...] += _dot_tn((z * silu).astype(BF16), dob)
        dz = da * silu
        dg_ref[...] = (da * z * (sgg * (1.0 + gate_v * (1.0 - sgg)))).astype(BF16)
        ds = dz * z0 * (sg * (1.0 - sg))
        dsb = ds.astype(BF16)
        dz0 = dz * sg + _dot_nt(dsb, wg_ref[...])
        dwg_ref[...] += _dot_tn(z0b, dsb)
        dbg_ref[...] += jnp.sum(ds, axis=0, keepdims=True)
        dy_ref[...] = dz0 * _gelu_grad(yv)

    sq = _const_spec((D_MODEL, D_MODEL))
    vec = _const_spec((1, D_MODEL))
    return pl.pallas_call(
        body, name=name, grid=(t // rows,),
        in_specs=[_row_spec(rows, D_MODEL)] * 3 + [sq, vec, sq],
        out_specs=[_row_spec(rows, D_MODEL), _row_spec(rows, D_MODEL), sq, vec, sq],
        out_shape=[jax.ShapeDtypeStruct((t, D_MODEL), F32), jax.ShapeDtypeStruct((t, D_MODEL), BF16),
                   jax.ShapeDtypeStruct((D_MODEL, D_MODEL), F32), jax.ShapeDtypeStruct((1, D_MODEL), F32),
                   jax.ShapeDtypeStruct((D_MODEL, D_MODEL), F32)],
        compiler_params=_params(1),
    )(dxn, y, gate, w_glu, b_glu.reshape(1, D_MODEL), w_out)


KV_LANES = GQA * ATTN_BLOCK


def _attn_bias(block_is_first):
    kj = lax.broadcasted_iota(jnp.int32, (2 * ATTN_BLOCK, ATTN_BLOCK), 0)
    qi = lax.broadcasted_iota(jnp.int32, (2 * ATTN_BLOCK, ATTN_BLOCK), 1)
    dist = qi + ATTN_BLOCK - kj
    valid = (dist >= 0) & (dist < ATTN_BLOCK) & (jnp.logical_not(block_is_first) | (kj >= ATTN_BLOCK))
    return jnp.tile(jnp.where(valid, 0.0, NEG_INF).astype(F32), (1, GQA))


def _head_cols(a_t, kvh):
    heads = range(kvh * GQA, (kvh + 1) * GQA)
    return jnp.concatenate([a_t[HEAD_DIM * h:HEAD_DIM * (h + 1)] for h in heads], axis=1)


def _head_rows(a_cols):
    stacked = jnp.concatenate([a_cols[:, ATTN_BLOCK * g:ATTN_BLOCK * (g + 1)] for g in range(GQA)], axis=0)
    return stacked.T


def _kv_rows(prev_ref, cur_ref, kvh):
    lanes = slice(HEAD_DIM * kvh, HEAD_DIM * (kvh + 1))
    return jnp.concatenate([prev_ref[:, lanes], cur_ref[:, lanes]], axis=0).astype(BF16)


def _kv_cols(prev_t, cur_t, kvh):
    rows = slice(HEAD_DIM * kvh, HEAD_DIM * (kvh + 1))
    return jnp.concatenate([prev_t[rows], cur_t[rows]], axis=1).astype(BF16)


def _attn_probs(kk, q_cols, sink_row, bias):
    s = _dot(kk, q_cols) * ATTN_SCALE + bias
    m = jnp.maximum(jnp.max(s, axis=0, keepdims=True), sink_row)
    p = jnp.exp(s - m)
    e_sink = jnp.exp(sink_row - m)
    inv = 1.0 / (jnp.sum(p, axis=0, keepdims=True) + e_sink)
    return p * inv, e_sink * inv


def _sink_cols(sinks):
    return jnp.repeat(sinks, ATTN_BLOCK).reshape(N_KV_HEADS, 1, KV_LANES)


def _attn_fwd(q, k, v, sinks, name, exchange=None):
    t = q.shape[0]
    nblk = t // ATTN_BLOCK

    def body(s_ref, q_ref, kc_ref, kp_ref, vc_ref, vp_ref, o_ref):
        bias = _attn_bias(pl.program_id(0) == 0)
        q_t = q_ref[...].astype(F32).T
        vp_t, vc_t = vp_ref[...].astype(F32).T, vc_ref[...].astype(F32).T
        for kvh in range(N_KV_HEADS):
            p, _ = _attn_probs(_kv_rows(kp_ref, kc_ref, kvh), _head_cols(q_t, kvh).astype(BF16), s_ref[kvh], bias)
            o_cols = _dot(_kv_cols(vp_t, vc_t, kvh), p.astype(BF16))
            o_ref[:, GQA * HEAD_DIM * kvh:GQA * HEAD_DIM * (kvh + 1)] = _head_rows(o_cols)

    cur = lambda i: (i, 0)
    prev = lambda i: (jnp.maximum(i - 1, 0), 0)
    (o,), got = _call_hosting(
        exchange, body, (_sink_cols(sinks), q, k, k, v, v), name=name, steps=nblk,
        in_specs=[_const_spec((N_KV_HEADS, 1, KV_LANES)),
                  pl.BlockSpec((ATTN_BLOCK, Q_DIM), cur),
                  pl.BlockSpec((ATTN_BLOCK, KV_DIM), cur), pl.BlockSpec((ATTN_BLOCK, KV_DIM), prev),
                  pl.BlockSpec((ATTN_BLOCK, KV_DIM), cur), pl.BlockSpec((ATTN_BLOCK, KV_DIM), prev)],
        out_specs=[pl.BlockSpec((ATTN_BLOCK, Q_DIM), cur)],
        out_shape=[jax.ShapeDtypeStruct((t, Q_DIM), F32)])
    return o, got


def _attn_bwd(q, k, v, o, do, sinks, rope, name, exchange=None):
    t = q.shape[0]
    nblk = t // ATTN_BLOCK

    def body(s_ref, q_ref, o_ref, do_ref, kp_ref, kc_ref, vp_ref, vc_ref, cosq_ref, sinq_ref, cosk_ref, sinkey_ref,
             dq_ref, dk_ref, dv_ref, ds_ref, new_k, new_v, wait_k, wait_v, dq_rot):
        n = pl.program_id(0)

        @pl.when(n == 0)
        def _():
            ds_ref[...] = jnp.zeros_like(ds_ref)
            wait_k[...] = jnp.zeros_like(wait_k)
            wait_v[...] = jnp.zeros_like(wait_v)

        @pl.when(n < nblk)
        def _():
            bias = _attn_bias(n == 0)
            q_t, o_t, do_t = q_ref[...].astype(F32).T, o_ref[...].T, do_ref[...].T
            kp_t, kc_t = kp_ref[...].astype(F32).T, kc_ref[...].astype(F32).T
            for kvh in range(N_KV_HEADS):
                q_cols = _head_cols(q_t, kvh).astype(BF16)
                do_cols = _head_cols(do_t, kvh)
                delta = jnp.sum(do_cols * _head_cols(o_t, kvh), axis=0, keepdims=True)
                do_cols = do_cols.astype(BF16)
                p, p_sink = _attn_probs(_kv_rows(kp_ref, kc_ref, kvh), q_cols, s_ref[kvh], bias)
                dp = _dot(_kv_rows(vp_ref, vc_ref, kvh), do_cols)
                ds = (p * (dp - delta) * ATTN_SCALE).astype(BF16)
                lanes = slice(GQA * HEAD_DIM * kvh, GQA * HEAD_DIM * (kvh + 1))
                dq_rot[:, lanes] = _head_rows(_dot(_kv_cols(kp_t, kc_t, kvh), ds))
                head = slice(HEAD_DIM * kvh, HEAD_DIM * (kvh + 1))
                new_k[:, head] = _dot_nt(ds, q_cols)
                new_v[:, head] = _dot_nt(p.astype(BF16), do_cols)
                ds_ref[kvh] += -(p_sink * delta)
            dq_ref[...] = _rope_apply(dq_rot[...], cosq_ref[...], sinq_ref[...], -1.0).astype(BF16)

        @pl.when(n == nblk)
        def _():
            new_k[...] = jnp.zeros_like(new_k)
            new_v[...] = jnp.zeros_like(new_v)

        dk_ref[...] = _rope_apply(wait_k[...] + new_k[:ATTN_BLOCK], cosk_ref[...], sinkey_ref[...], -1.0).astype(BF16)
        dv_ref[...] = (wait_v[...] + new_v[:ATTN_BLOCK]).astype(BF16)
        wait_k[...] = new_k[ATTN_BLOCK:]
        wait_v[...] = new_v[ATTN_BLOCK:]

    cur = lambda i: (jnp.minimum(i, nblk - 1), 0)
    prev = lambda i: (jnp.maximum(i - 1, 0), 0)
    qs = lambda f: pl.BlockSpec((ATTN_BLOCK, Q_DIM), f)
    ks = lambda f: pl.BlockSpec((ATTN_BLOCK, KV_DIM), f)
    sink_spec = _const_spec((N_KV_HEADS, 1, KV_LANES))
    return _call_hosting(
        exchange, body, (_sink_cols(sinks), q, o, do, k, k, v, v, rope[0], rope[1], rope[0], rope[1]),
        name=name, steps=nblk + 1,
        in_specs=[sink_spec, qs(cur), qs(cur), qs(cur), ks(prev), ks(cur), ks(prev), ks(cur),
                  ks(cur), ks(cur), ks(prev), ks(prev)],
        out_specs=[qs(cur), ks(prev), ks(prev), sink_spec],
        out_shape=[jax.ShapeDtypeStruct((t, Q_DIM), BF16), jax.ShapeDtypeStruct((t, KV_DIM), BF16),
                   jax.ShapeDtypeStruct((t, KV_DIM), BF16), jax.ShapeDtypeStruct((N_KV_HEADS, 1, KV_LANES), F32)],
        scratch_shapes=[pltpu.VMEM((2 * ATTN_BLOCK, KV_DIM), F32), pltpu.VMEM((2 * ATTN_BLOCK, KV_DIM), F32),
                        pltpu.VMEM((ATTN_BLOCK, KV_DIM), F32), pltpu.VMEM((ATTN_BLOCK, KV_DIM), F32),
                        pltpu.VMEM((ATTN_BLOCK, Q_DIM), F32)])


def _attn_out_fwd(o, gate, x, w_out, name, exchange=None):
    t = x.shape[0]
    rows = ROWS_FWD

    def body(o_ref, g_ref, x_ref, wo_ref, xn_ref):
        gate_v = g_ref[...]
        a = o_ref[...] * (gate_v * _sigmoid(gate_v))
        xn_ref[...] = x_ref[...] + _dot(a.astype(BF16), wo_ref[...])

    (xn,), got = _call_hosting(
        exchange, body, (o, gate, x, w_out), name=name, steps=t // rows,
        in_specs=[_row_spec(rows, D_MODEL)] * 3 + [_const_spec((D_MODEL, D_MODEL))],
        out_specs=[_row_spec(rows, D_MODEL)],
        out_shape=[jax.ShapeDtypeStruct((t, D_MODEL), F32)])
    return xn, got


def _attn_out_bwd(dxn, o, gate, w_out, name, exchange=None):
    t = o.shape[0]
    rows = ROWS_BWD

    def body(dxn_ref, o_ref, g_ref, wo_ref, do_ref, dg_ref, dwo_ref):
        @pl.when(pl.program_id(0) == 0)
        def _():
            dwo_ref[...] = jnp.zeros_like(dwo_ref)

        gate_v, ov = g_ref[...], o_ref[...]
        sgg = _sigmoid(gate_v)
        silu = gate_v * sgg
        dob = dxn_ref[...].astype(BF16)
        da = _dot_nt(dob, wo_ref[...])
        dwo_ref[...] += _dot_tn((ov * silu).astype(BF16), dob)
        do_ref[...] = da * silu
        dg_ref[...] = (da * ov * (sgg * (1.0 + gate_v * (1.0 - sgg)))).astype(BF16)

    sq = _const_spec((D_MODEL, D_MODEL))
    return _call_hosting(
        exchange, body, (dxn, o, gate, w_out), name=name, steps=t // rows,
        in_specs=[_row_spec(rows, D_MODEL)] * 3 + [sq],
        out_specs=[_row_spec(rows, D_MODEL), _row_spec(rows, D_MODEL), sq],
        out_shape=[jax.ShapeDtypeStruct((t, D_MODEL), F32), jax.ShapeDtypeStruct((t, D_MODEL), BF16),
                   jax.ShapeDtypeStruct((D_MODEL, D_MODEL), F32)])


def _loss_head(x, norm, target, name):
    t = x.shape[0]
    rows = ROWS_FWD

    def body(x_ref, n_ref, t_ref, loss_ref, dx_ref, dn_ref):
        i = pl.program_id(0)
        xv = x_ref[...]
        rstd = lax.rsqrt(jnp.mean(xv * xv, axis=-1, keepdims=True) + NORM_EPS)
        xhat = xv * rstd
        err = xhat * n_ref[...] - t_ref[...]
        part = 0.5 * jnp.sum(jnp.mean(err * err, axis=-1, keepdims=True), axis=0, keepdims=True)
        dy = err * (1.0 / D_MODEL)
        dn = jnp.sum(dy * xhat, axis=0, keepdims=True)
        dxhat = dy * n_ref[...]
        dx_ref[...] = rstd * (dxhat - xhat * jnp.mean(dxhat * xhat, axis=-1, keepdims=True))

        @pl.when(i == 0)
        def _():
            loss_ref[...] = jnp.zeros((8, 128), F32) + part
            dn_ref[...] = dn

        @pl.when(i > 0)
        def _():
            loss_ref[...] += part
            dn_ref[...] += dn

    return pl.pallas_call(
        body, name=name, grid=(t // rows,),
        in_specs=[_row_spec(rows, D_MODEL), _const_spec((1, D_MODEL)), _row_spec(rows, D_MODEL)],
        out_specs=[_const_spec((8, 128)), _row_spec(rows, D_MODEL), _const_spec((1, D_MODEL))],
        out_shape=[jax.ShapeDtypeStruct((8, 128), F32), jax.ShapeDtypeStruct((t, D_MODEL), F32),
                   jax.ShapeDtypeStruct((1, D_MODEL), F32)],
        compiler_params=_params(1),
    )(x, norm.reshape(1, D_MODEL), target)


N_CHIPS = 4
N_CORES = 2
CHIP_FLIPS = ((0, 1), (1, 0), (1, 1))
ICI_CHUNKS = 2
D2D_CHUNKS = 8


def _n_chunks(rows, dtype, most):
    unit = 16 if dtype == BF16 else 8
    return max(n for n in range(1, most + 1) if rows % n == 0 and (rows // n) % unit == 0)


def _chunks_of(arrays, most):
    out = []
    for a in arrays:
        n = _n_chunks(a.shape[-2], a.dtype, most)
        out.append((n, a.shape[-2] // n))
    return out


class _Exchange:
    def __init__(self, arrays, out_shape, scratch, copies):
        self.arrays, self.out_shape, self.scratch, self._copies = arrays, out_shape, scratch, copies

    def start(self, *refs):
        for cp in self._copies(*refs)[0]:
            cp.start()

    def wait(self, *refs):
        for wait in self._copies(*refs)[1]:
            wait()


def _chips_exchange(sends, per_dest):
    n = len(sends)
    chunking = _chunks_of(sends, ICI_CHUNKS)

    def copies(send_refs, recv_refs, sems):
        x, y, c = lax.axis_index("x"), lax.axis_index("y"), lax.axis_index("c")
        me = 2 * x + y

        def peer(k):
            fx, fy = CHIP_FLIPS[k]
            px, py = x + fx - 2 * x * fx, y + fy - 2 * y * fy
            return (px, py, c), 2 * px + py

        to_start, waits = [], []
        for a in range(n):
            send_sems, recv_sems, local_sems = sems[3 * a:3 * a + 3]
            chunks, chunk_rows = chunking[a]
            for j in range(chunks):
                part = pl.ds(j * chunk_rows, chunk_rows)
                src = lambda number: send_refs[a].at[number, part] if per_dest else send_refs[a].at[part]
                for k in range(len(CHIP_FLIPS)):
                    to, to_number = peer(k)
                    remote = lambda landing: pltpu.make_async_remote_copy(
                        src_ref=src(to_number), dst_ref=recv_refs[a].at[landing, part],
                        send_sem=send_sems.at[k, j], recv_sem=recv_sems.at[k, j],
                        device_id=to, device_id_type=pl.DeviceIdType.MESH)
                    to_start.append(remote(me))
                    waits += [remote(me).wait_send, remote(to_number).wait_recv]
                own = pltpu.make_async_copy(src(me), recv_refs[a].at[me, part], local_sems.at[j])
                to_start.append(own)
                waits.append(own.wait)
        return to_start, waits

    scratch = []
    for chunks, _ in chunking:
        scratch += [pltpu.SemaphoreType.DMA((len(CHIP_FLIPS), chunks)), pltpu.SemaphoreType.DMA((len(CHIP_FLIPS), chunks)),
                    pltpu.SemaphoreType.DMA((chunks,))]
    return _Exchange(sends, [jax.ShapeDtypeStruct((N_CHIPS,) + a.shape[-2:], a.dtype) for a in sends], scratch, copies)


def _cores_exchange(sends, per_dest):
    n = len(sends)
    chunking = _chunks_of(sends, D2D_CHUNKS)

    def copies(send_refs, got_refs, sems):
        c = lax.axis_index("c")
        sibling = (lax.axis_index("x"), lax.axis_index("y"), 1 - c)
        to_start = []
        for a in range(n):
            chunks, chunk_rows = chunking[a]
            for j in range(chunks):
                part = pl.ds(j * chunk_rows, chunk_rows)
                to_start.append(pltpu.make_async_remote_copy(
                    src_ref=send_refs[a].at[1 - c, part] if per_dest else send_refs[a].at[part],
                    dst_ref=got_refs[a].at[part], send_sem=sems[2 * a].at[j], recv_sem=sems[2 * a + 1].at[j],
                    device_id=sibling, device_id_type=pl.DeviceIdType.MESH))
        return to_start, [cp.wait for cp in to_start]

    scratch = []
    for chunks, _ in chunking:
        scratch += [pltpu.SemaphoreType.DMA((chunks,)), pltpu.SemaphoreType.DMA((chunks,))]
    return _Exchange(sends, [jax.ShapeDtypeStruct(a.shape[-2:], a.dtype) for a in sends], scratch, copies)


def _run_exchange(exchange, name):
    n = len(exchange.arrays)

    def body(*refs):
        parts = refs[:n], refs[n:2 * n], refs[2 * n:]
        exchange.start(*parts)
        exchange.wait(*parts)

    hbm = pl.BlockSpec(memory_space=pltpu.HBM)
    return pl.pallas_call(body, name=name, in_specs=[hbm] * n, out_specs=[hbm] * n, out_shape=exchange.out_shape,
                          scratch_shapes=exchange.scratch)(*exchange.arrays)


def _call_hosting(exchange, body, args, *, name, steps, in_specs, out_specs, out_shape, scratch_shapes=()):
    common = dict(name=name, grid=(steps,), compiler_params=_params(1))
    if exchange is None:
        return pl.pallas_call(body, in_specs=in_specs, out_specs=out_specs, out_shape=out_shape,
                              scratch_shapes=list(scratch_shapes), **common)(*args), None
    n_in, n_out, n_scratch, k = len(in_specs), len(out_specs), len(scratch_shapes), len(exchange.arrays)

    def hosting(*refs):
        ins, sends = refs[:n_in], refs[n_in:n_in + k]
        outs, recvs = refs[n_in + k:n_in + k + n_out], refs[n_in + k + n_out:n_in + 2 * k + n_out]
        scratch = refs[n_in + 2 * k + n_out:n_in + 2 * k + n_out + n_scratch]
        sems = refs[n_in + 2 * k + n_out + n_scratch:]
        pl.when(pl.program_id(0) == 0)(lambda: exchange.start(sends, recvs, sems))
        body(*ins, *outs, *scratch)
        pl.when(pl.program_id(0) == steps - 1)(lambda: exchange.wait(sends, recvs, sems))

    hbm = pl.BlockSpec(memory_space=pltpu.HBM)
    results = pl.pallas_call(
        hosting, in_specs=list(in_specs) + [hbm] * k, out_specs=list(out_specs) + [hbm] * k,
        out_shape=list(out_shape) + exchange.out_shape, scratch_shapes=list(scratch_shapes) + exchange.scratch, **common,
    )(*args, *exchange.arrays)
    return results[:n_out], results[n_out:]


def _exchange_chips(sends, per_dest, name):
    return _run_exchange(_chips_exchange(sends, per_dest), name)


def _swap_cores(sends, per_dest, name):
    return _run_exchange(_cores_exchange(sends, per_dest), name)


def _all_gather(arrays, name):
    by_chip = _exchange_chips(arrays, False, name + "_chips")
    others = _swap_cores([r.reshape(-1, r.shape[-1]) for r in by_chip], False, name + "_cores")
    return [(m, o.reshape(m.shape)) for m, o in zip(by_chip, others)]


def _in_device_order(mine, other, axis):
    first = lax.axis_index("c") == 0
    pieces = []
    for m, o in zip(mine, other):
        pieces += [jnp.where(first, m, o), jnp.where(first, o, m)]
    return jnp.concatenate(pieces, axis=axis)


def _sum_core(send, got, out_dtype, name):
    _, n, cols = send.shape
    rows = min(n, 256)
    while n % rows:
        rows -= 16

    def body(c_ref, keep_ref, got_ref, o_ref):
        o_ref[...] = (keep_ref[...].astype(F32) + got_ref[...].astype(F32)).astype(out_dtype)

    return pl.pallas_call(
        body, name=name, out_shape=jax.ShapeDtypeStruct((n, cols), out_dtype),
        grid_spec=pltpu.PrefetchScalarGridSpec(
            num_scalar_prefetch=1, grid=(n // rows,),
            in_specs=[pl.BlockSpec((None, rows, cols), lambda i, c: (c[0], i, 0)),
                      pl.BlockSpec((rows, cols), lambda i, c: (i, 0))],
            out_specs=pl.BlockSpec((rows, cols), lambda i, c: (i, 0))),
        compiler_params=_params(1),
    )(lax.axis_index("c").astype(jnp.int32).reshape(1), send, got)


def _sum_parts(parts, out_dtype, name):
    n, cols = parts[0].shape
    rows = min(n, 256)
    while n % rows:
        rows -= 16

    def body(*refs):
        acc = refs[0][...].astype(F32)
        for ref in refs[1:-1]:
            acc = acc + ref[...].astype(F32)
        refs[-1][...] = acc.astype(out_dtype)

    return pl.pallas_call(
        body, name=name, grid=(n // rows,),
        in_specs=[_row_spec(rows, cols)] * len(parts),
        out_specs=_row_spec(rows, cols),
        out_shape=jax.ShapeDtypeStruct((n, cols), out_dtype),
        compiler_params=_params(1),
    )(*parts)


def _reduce_scatter(sends, wire_dtypes, name):
    halves = [s.reshape(N_CORES, N_CHIPS * s.shape[2], s.shape[3]) for s in sends]
    gots = _swap_cores(halves, True, name + "_cores")
    sums = [_sum_core(h, g, dt, "%s_core_sum%d" % (name, i)).reshape((N_CHIPS,) + s.shape[2:])
            for i, (h, g, dt, s) in enumerate(zip(halves, gots, wire_dtypes, sends))]
    return _exchange_chips(sums, True, name + "_chips")


def _adamw(parts, w, m, v, name):
    n, cols = w.shape
    k = parts.shape[0]
    rows = min(n, 256)
    while n % rows:
        rows -= 8
    c1 = 1.0 - ADAM_B1 ** ADAM_STEP
    c2 = 1.0 - ADAM_B2 ** ADAM_STEP

    def body(p_ref, w_ref, m_ref, v_ref, g_ref, d_ref, nm_ref, nv_ref):
        g = p_ref[0].astype(F32)
        for s in range(1, k):
            g = g + p_ref[s].astype(F32)
        nm = ADAM_B1 * m_ref[...] + (1.0 - ADAM_B1) * g
        nv = ADAM_B2 * v_ref[...] + (1.0 - ADAM_B2) * (g * g)
        g_ref[...] = g
        nm_ref[...] = nm
        nv_ref[...] = nv
        d_ref[...] = -ADAM_LR * ((nm / c1) / (jnp.sqrt(nv / c2) + ADAM_EPS) + ADAM_WD * w_ref[...])

    blk = _row_spec(rows, cols)
    return pl.pallas_call(
        body, name=name, grid=(n // rows,),
        in_specs=[pl.BlockSpec((k, rows, cols), lambda i: (0, i, 0)), blk, blk, blk],
        out_specs=[blk] * 4,
        out_shape=[jax.ShapeDtypeStruct((n, cols), F32)] * 4,
        compiler_params=_params(1),
    )(parts, w, m, v)


SSM_KEYS = ("norm", "w_in", "a_re", "a_im", "log_step", "b_re", "b_im", "c_re", "c_im", "d", "w_glu", "b_glu", "w_out")
ATTN_KEYS = ("norm", "w_in", "sinks", "w_out")
LAYER_KEYS = (SSM_KEYS, ATTN_KEYS, SSM_KEYS, ATTN_KEYS)
BIG_KEYS = ("w_in", "w_glu", "w_out")
ATTN_SPLITS = (Q_DIM, KV_DIM, KV_DIM, D_MODEL)


def _rope_tables(t):
    pos = jnp.arange(t, dtype=F32)
    inv_freq = ROPE_THETA ** (-jnp.arange(0, HEAD_DIM, 2, dtype=F32) / HEAD_DIM)
    ang = pos[:, None] * inv_freq[None, :]
    cos, sin = jnp.cos(ang), jnp.sin(ang)
    return jnp.tile(jnp.concatenate([cos, cos], axis=1), (1, 2)), jnp.tile(jnp.concatenate([-sin, sin], axis=1), (1, 2))


def _ssm_layer_fwd(i, x, p, w):
    tag = "l%d_" % i
    mats, mats_vjp = jax.vjp(_s5_matrices, p["a_re"], p["a_im"], p["log_step"], p["b_re"], p["b_im"], p["c_re"], p["c_im"])
    kern, cpt, bpt, ar, ai = mats
    tm = _s5_toeplitz(kern)
    mb = dict(tm=tm.astype(BF16), tmt=jnp.swapaxes(tm, 1, 2).astype(BF16), cpt=cpt.astype(BF16),
              cp=jnp.swapaxes(cpt, 1, 2).astype(BF16), bpt=bpt.astype(BF16), bp=jnp.swapaxes(bpt, 1, 2).astype(BF16))
    u, gate = _inproj_fwd(x, p["norm"], w["w_in"], (D_MODEL, D_MODEL), (F32, F32), None, tag + "inproj_fwd")
    xre, xim = _s5_project(u, mb["bpt"], tag + "s5_block_inputs")
    hre, him = _s5_scan_fwd(xre, xim, ar, ai, tag + "s5_scan_fwd")
    y = _s5_outputs(u, hre, him, mb["tm"], mb["cpt"], p["d"], tag + "s5_outputs")
    xn = _ssm_out_fwd(y, gate, x, w["w_glu"], p["b_glu"], w["w_out"], tag + "out_fwd")
    return xn, (x, u, gate, y, hre, him, mb, ar, ai, mats_vjp)


def _ssm_layer_bwd(i, dxn, saved, p, w):
    tag = "l%d_" % i
    x, u, gate, y, hre, him, mb, ar, ai, mats_vjp = saved
    dy, dgate, dw_glu, db_glu, dw_out = _ssm_out_bwd(dxn, y, gate, w["w_glu"], p["b_glu"], w["w_out"], tag + "out_bwd")
    dhre, dhim = _s5_project(dy, mb["cp"], tag + "s5_state_grads")
    dxre, dxim, dar, dai = _s5_scan_bwd(dhre, dhim, hre, him, ar, ai, tag + "s5_scan_bwd")
    du, dk, dcpt, dbpt, dd = _s5_backward(dy, u, hre, him, dxre, dxim, mb["tmt"], mb["bp"], p["d"], tag + "s5_backward")
    dk = dk.reshape(SSM_GROUPS, S5_BLOCK, SSM_GROUP, SSM_GROUP)
    da_re, da_im, dlog_step, db_re, db_im, dc_re, dc_im = mats_vjp((dk, dcpt, dbpt, dar, dai))
    dx, dw_in, dnorm = _inproj_bwd(x, p["norm"], w["w_in"], [du, dgate], dxn, tag + "inproj_bwd")
    grads = dict(norm=dnorm.reshape(D_MODEL), w_in=dw_in, a_re=da_re, a_im=da_im, log_step=dlog_step, b_re=db_re,
                 b_im=db_im, c_re=dc_re, c_im=dc_im, d=dd.reshape(D_MODEL), w_glu=dw_glu, b_glu=db_glu.reshape(D_MODEL),
                 w_out=dw_out)
    return dx, grads


def _attn_layer_fwd(i, x, p, w, rope, gather=None):
    tag = "l%d_" % i
    q, k, v, gate = _inproj_fwd(x, p["norm"], w["w_in"], ATTN_SPLITS, (BF16, BF16, BF16, F32), rope, tag + "inproj_fwd")
    if gather is None:
        o, _ = _attn_fwd(q, k, v, p["sinks"], tag + "attn_fwd")
        xn, _ = _attn_out_fwd(o, gate, x, w["w_out"], tag + "out_fwd")
        return xn, (x, q, k, v, gate, o), None
    o, by_chip = _attn_fwd(q, k, v, p["sinks"], tag + "attn_fwd", _chips_exchange(gather, False))
    flat = [r.reshape(-1, r.shape[-1]) for r in by_chip]
    xn, others = _attn_out_fwd(o, gate, x, w["w_out"], tag + "out_fwd", _cores_exchange(flat, False))
    return xn, (x, q, k, v, gate, o), [(m, other.reshape(m.shape)) for m, other in zip(by_chip, others)]


def _attn_layer_bwd(i, dxn, saved, p, w, rope, scatter=None):
    tag = "l%d_" % i
    x, q, k, v, gate, o = saved
    if scatter is None:
        (do, dgate, dw_out), _ = _attn_out_bwd(dxn, o, gate, w["w_out"], tag + "out_bwd")
        (dq, dk, dv, dsinks), parts = _attn_bwd(q, k, v, o, do, p["sinks"], rope, tag + "attn_bwd")
    else:
        sends, wire_dtypes = scatter
        halves = [s.reshape(N_CORES, N_CHIPS * s.shape[2], s.shape[3]) for s in sends]
        (do, dgate, dw_out), gots = _attn_out_bwd(dxn, o, gate, w["w_out"], tag + "out_bwd", _cores_exchange(halves, True))
        sums = [_sum_core(h, g, dt, "%sscatter_core_sum%d" % (tag, j)).reshape((N_CHIPS,) + s.shape[2:])
                for j, (h, g, dt, s) in enumerate(zip(halves, gots, wire_dtypes, sends))]
        (dq, dk, dv, dsinks), parts = _attn_bwd(q, k, v, o, do, p["sinks"], rope, tag + "attn_bwd", _chips_exchange(sums, True))
    dx, dw_in, dnorm = _inproj_bwd(x, p["norm"], w["w_in"], [dq, dk, dv, dgate], dxn, tag + "inproj_bwd")
    grads = dict(norm=dnorm.reshape(D_MODEL), w_in=dw_in, sinks=dsinks.reshape(N_Q_HEADS, ATTN_BLOCK).sum(axis=1), w_out=dw_out)
    return dx, grads, parts


def _local_step(x, target, small, big, late_weights=None, late_sends=None):
    rope = _rope_tables(x.shape[0])
    big = list(big)
    saved = []
    for i in range(4):
        if i % 2 == 0:
            x, s = _ssm_layer_fwd(i, x, small[i], big[i])
        elif i == 1 and late_weights is not None:
            x, s, gathered = _attn_layer_fwd(i, x, small[i], big[i], rope, late_weights[0])
            big[2], big[3] = late_weights[1](gathered)
        else:
            x, s, _ = _attn_layer_fwd(i, x, small[i], big[i], rope)
        saved.append(s)
    loss, dx, dfinal = _loss_head(x, small[4]["norm"], target, "loss_head")
    grads = [None] * 4 + [dict(norm=dfinal.reshape(D_MODEL))]
    late_parts = None
    for i in (3, 2, 1, 0):
        if i % 2 == 0:
            dx, grads[i] = _ssm_layer_bwd(i, dx, saved[i], small[i], big[i])
        elif i == 1 and late_sends is not None:
            dx, grads[i], late_parts = _attn_layer_bwd(i, dx, saved[i], small[i], big[i], rope, late_sends(grads))
        else:
            dx, grads[i], _ = _attn_layer_bwd(i, dx, saved[i], small[i], big[i], rope)
    return loss[0, 0], dx, grads, late_parts


def _owner_major(key, g):
    if key == "w_in":
        return g.reshape(D_MODEL, N_CHIPS, N_CORES, -1).transpose(2, 1, 0, 3)
    return g.reshape(N_CHIPS, N_CORES, -1, D_MODEL).transpose(1, 0, 2, 3)


def _from_gathered(key, mine, other):
    return _in_device_order(list(mine), list(other), 1 if key == "w_in" else 0)


SMALL_ROWS = 72


def _rows_of_small(a):
    flat = a.reshape(-1)
    return jnp.pad(flat, (0, -flat.shape[0] % D_MODEL)).reshape(-1, D_MODEL)


def _stack_small(arrays):
    rows = jnp.concatenate([_rows_of_small(a) for a in arrays], axis=0)
    assert rows.shape[0] <= N_DEV * SMALL_ROWS
    return jnp.pad(rows, ((0, N_DEV * SMALL_ROWS - rows.shape[0]), (0, 0)))


def kernel(*args):
    names = ["x"]
    layer_names = []
    for i, keys in enumerate(LAYER_KEYS):
        layer_names += ["l%d_%s" % (i, k) for k in keys]
    layer_names.append("final_norm")
    names += layer_names + ["loss_target"] + ["m_" + n for n in layer_names] + ["v_" + n for n in layer_names]
    given = dict(zip(names, args))
    big_names = [n for n in layer_names if n.split("_", 1)[1] in BIG_KEYS]
    small_names = [n for n in layer_names if n not in big_names]

    offsets = {}

    def families_of(layers):
        families = {}
        for n in big_names:
            if int(n[1]) in layers:
                family = families.setdefault(given[n].shape[1], [])
                offsets[n] = sum(given[other].shape[0] for other in family)
                family.append(n)
        return list(families.values())

    early, late = families_of((0, 1)), families_of((2, 3))
    stack = lambda pre, family: jnp.concatenate([given[pre + n] for n in family], axis=0)
    rows_of = lambda a, n: a[..., offsets[n]:offsets[n] + given[n].shape[0], :]
    local = lambda families: [stack("", family).astype(BF16) for family in families]

    def assemble(families, gathered):
        big = [dict() for _ in range(4)]
        for family, (mine, other) in zip(families, gathered):
            for n in family:
                big[int(n[1])][n.split("_", 1)[1]] = _from_gathered(n.split("_", 1)[1], rows_of(mine, n), rows_of(other, n))
        return big

    def sends_of(families, grads):
        return [jnp.concatenate([_owner_major(n.split("_", 1)[1], grads[int(n[1])][n.split("_", 1)[1]]) for n in family], axis=2)
                for family in families]

    small = [dict() for _ in range(5)]
    for n in small_names:
        if n == "final_norm":
            small[4]["norm"] = given[n]
        else:
            small[int(n[1])][n.split("_", 1)[1]] = given[n]

    big = assemble(early, _all_gather(local(early), "gather_early_weights"))
    loss, dx, grads, late_parts = _local_step(
        given["x"][0], given["loss_target"][0], small, big,
        late_weights=(local(late), lambda gathered: assemble(late, gathered)[2:]),
        late_sends=lambda grads: (sends_of(late, grads), [BF16] * len(late)))
    loss = lax.psum(loss, ("x", "y", "c"))

    def grad_of(n):
        return grads[4]["norm"] if n == "final_norm" else grads[int(n[1])][n.split("_", 1)[1]]

    flat = lambda f: _stack_small([f(n) for n in small_names])
    sends = sends_of(early, grads) + [flat(grad_of).reshape(N_CORES, N_CHIPS, SMALL_ROWS, D_MODEL)]
    parts = _reduce_scatter(sends, [BF16] * len(early) + [F32], "scatter_grads")

    outs = {}
    tags = ("grad_", "delta_", "new_m_", "new_v_")
    for i, (family, part) in enumerate(zip(early + late, parts[:-1] + late_parts)):
        results = _adamw(part, stack("", family), stack("m_", family), stack("v_", family), "adamw_matrices%d" % i)
        for tag, a in zip(tags, results):
            for n in family:
                outs[tag + n] = rows_of(a, n)

    my_slice = _sum_parts([parts[-1][s] for s in range(N_CHIPS)], F32, "sum_small_grads")
    mine, other = _all_gather([my_slice], "gather_small_grads")[0]
    first = lax.axis_index("c") == 0
    g_all = jnp.concatenate([jnp.where(first, mine, other), jnp.where(first, other, mine)], axis=0)
    g_all = g_all.reshape(1, N_DEV * SMALL_ROWS, D_MODEL)
    results = _adamw(g_all, flat(lambda n: given[n]), flat(lambda n: given["m_" + n]), flat(lambda n: given["v_" + n]),
                     "adamw_small")
    for tag, a in zip(tags, results):
        at = 0
        for n in small_names:
            rows = -(-given[n].size // D_MODEL)
            outs[tag + n] = a[at:at + rows].reshape(-1)[:given[n].size].reshape(given[n].shape)
            at += rows
    result = [loss, dx[None]]
    for tag in ("grad_", "delta_", "new_m_", "new_v_"):
        result += [outs[tag + n] for n in layer_names]
    return tuple(result)
```

```python
import functools
import math

import jax
import jax.numpy as jnp
from jax import lax
from jax.experimental import pallas as pl
from jax.experimental.pallas import tpu as pltpu

F32 = jnp.float32
BF16 = jnp.bfloat16

D_MODEL = 1024
SSM_GROUP = 16
SSM_GROUPS = D_MODEL // SSM_GROUP
SSM_STATE = 64
S5_BLOCK = 16
S5_LANES = S5_BLOCK * SSM_GROUP
HEAD_DIM = 64
N_Q_HEADS = 16
N_KV_HEADS = 2
GQA = N_Q_HEADS // N_KV_HEADS
Q_DIM = N_Q_HEADS * HEAD_DIM
KV_DIM = N_KV_HEADS * HEAD_DIM
ATTN_BLOCK = 128
ROPE_THETA = 10000.0
NORM_EPS = 1e-5
NEG_INF = -1e30
ATTN_SCALE = HEAD_DIM ** -0.5
N_DEV = 8

ADAM_LR = 0.001
ADAM_B1 = 0.9
ADAM_B2 = 0.999
ADAM_EPS = 1e-08
ADAM_WD = 0.01
ADAM_STEP = 10

VMEM_LIMIT = 56 * 1024 * 1024
ROWS_FWD = 512
ROWS_BWD = 512

NT = (((1,), (1,)), ((), ()))
TN = (((0,), (0,)), ((), ()))


def _params(n_grid):
    return pltpu.CompilerParams(dimension_semantics=("arbitrary",) * n_grid, vmem_limit_bytes=VMEM_LIMIT)


def _dot(a, b):
    return jnp.dot(a, b, preferred_element_type=F32)


def _dot_nt(a, b):
    return lax.dot_general(a, b, NT, preferred_element_type=F32)


def _dot_tn(a, b):
    return lax.dot_general(a, b, TN, preferred_element_type=F32)


def _sigmoid(x):
    return 1.0 / (1.0 + jnp.exp(-x))


_GELU_K = math.sqrt(2.0 / math.pi)


def _gelu(x):
    return x * (0.5 * (1.0 + jnp.tanh(_GELU_K * (x + 0.044715 * (x * x * x)))))


def _gelu_grad(x):
    t = jnp.tanh(_GELU_K * (x + 0.044715 * (x * x * x)))
    return 0.5 * (1.0 + t) + 0.5 * x * (1.0 - t * t) * (_GELU_K * (1.0 + 3.0 * 0.044715 * (x * x)))


def _row_spec(rows, cols):
    return pl.BlockSpec((rows, cols), lambda i: (i, 0))


def _const_spec(shape):
    zeros = (0,) * len(shape)
    return pl.BlockSpec(shape, lambda i: zeros, pipeline_mode=pl.Buffered(1))


def _rope_apply(t, cos, sin_signed, sign):
    lane = lax.broadcasted_iota(jnp.int32, (1, 128), 1)
    first_half = (lane % HEAD_DIM) < (HEAD_DIM // 2)
    out = []
    for j in range(t.shape[1] // 128):
        tj = t[:, 128 * j:128 * (j + 1)]
        partner = jnp.where(first_half, pltpu.roll(tj, 128 - HEAD_DIM // 2, 1), pltpu.roll(tj, HEAD_DIM // 2, 1))
        out.append(tj * cos + sign * (partner * sin_signed))
    return out[0] if len(out) == 1 else jnp.concatenate(out, axis=1)


def _inproj_fwd(x, norm, w, splits, dtypes, rope, name, exchange=None):
    t = x.shape[0]
    n = w.shape[1]
    rows = ROWS_FWD

    def body(*refs):
        if rope is None:
            x_ref, n_ref, w_ref = refs[:3]
            outs = refs[3:]
        else:
            x_ref, n_ref, w_ref, cos_ref, sin_ref = refs[:5]
            outs = refs[5:]
        xv = x_ref[...]
        rstd = lax.rsqrt(jnp.mean(xv * xv, axis=-1, keepdims=True) + NORM_EPS)
        h = (xv * rstd) * n_ref[...]
        proj = _dot(h.astype(BF16), w_ref[...])
        off = 0
        for i, width in enumerate(splits):
            piece = proj[:, off:off + width]
            if rope is not None and i < 2:
                piece = _rope_apply(piece, cos_ref[...], sin_ref[...], 1.0)
            outs[i][...] = piece.astype(dtypes[i])
            off += width

    in_specs = [_row_spec(rows, D_MODEL), _const_spec((1, D_MODEL)), _const_spec((D_MODEL, n))]
    args = [x, norm.reshape(1, D_MODEL), w]
    if rope is not None:
        in_specs += [_row_spec(rows, 128), _row_spec(rows, 128)]
        args += list(rope)
    return _call_hosting(
        exchange, body, args, name=name, steps=t // rows, in_specs=in_specs,
        out_specs=[_row_spec(rows, width) for width in splits],
        out_shape=[jax.ShapeDtypeStruct((t, width), dtype) for width, dtype in zip(splits, dtypes)])


def _inproj_bwd(x, norm, w, dpieces, dxn, name, exchange=None):
    t = x.shape[0]
    n = w.shape[1]
    rows = ROWS_BWD
    widths = [p.shape[1] for p in dpieces]
    k = len(dpieces)

    def body(*refs):
        x_ref, n_ref, w_ref, dxn_ref = refs[:4]
        d_refs = refs[4:4 + k]
        dx_ref, dw_ref, dn_ref = refs[4 + k:]
        @pl.when(pl.program_id(0) == 0)
        def _():
            dw_ref[...] = jnp.zeros_like(dw_ref)
            dn_ref[...] = jnp.zeros_like(dn_ref)

        xv = x_ref[...]
        rstd = lax.rsqrt(jnp.mean(xv * xv, axis=-1, keepdims=True) + NORM_EPS)
        xhat = xv * rstd
        h = xhat * n_ref[...]
        dproj = [r[...].astype(BF16) for r in d_refs]
        dproj = dproj[0] if k == 1 else jnp.concatenate(dproj, axis=1)
        dh = _dot_nt(dproj, w_ref[...])
        dw_ref[...] += _dot_tn(h.astype(BF16), dproj)
        dn_ref[...] += jnp.sum(dh * xhat, axis=0, keepdims=True)
        dxhat = dh * n_ref[...]
        dx_ref[...] = rstd * (dxhat - xhat * jnp.mean(dxhat * xhat, axis=-1, keepdims=True)) + dxn_ref[...]

    return _call_hosting(
        exchange, body, (x, norm.reshape(1, D_MODEL), w, dxn, *dpieces), name=name, steps=t // rows,
        in_specs=[_row_spec(rows, D_MODEL), _const_spec((1, D_MODEL)), _const_spec((D_MODEL, n)),
                  _row_spec(rows, D_MODEL)] + [_row_spec(rows, width) for width in widths],
        out_specs=[_row_spec(rows, D_MODEL), _const_spec((D_MODEL, n)), _const_spec((1, D_MODEL))],
        out_shape=[jax.ShapeDtypeStruct((t, D_MODEL), F32), jax.ShapeDtypeStruct((D_MODEL, n), F32),
                   jax.ShapeDtypeStruct((1, D_MODEL), F32)])


def _s5_matrices(a_re, a_im, log_step, b_re, b_im, c_re, c_im):
    r = S5_BLOCK
    step = jnp.exp(log_step)[:, None]
    lr, li = a_re * step, a_im * step
    k = jnp.arange(r + 1, dtype=F32)
    mag = jnp.exp(lr[:, None, :] * k[:, None])
    pr = mag * jnp.cos(li[:, None, :] * k[:, None])
    pi = mag * jnp.sin(li[:, None, :] * k[:, None])
    nr, ni = pr[:, 1] - 1.0, pi[:, 1]
    den = a_re * a_re + a_im * a_im
    qr, qi = (nr * a_re + ni * a_im) / den, (ni * a_re - nr * a_im) / den
    bbr = qr[..., None] * b_re - qi[..., None] * b_im
    bbi = qr[..., None] * b_im + qi[..., None] * b_re
    wr = c_re[:, None] * pr[:, :, None, :] - c_im[:, None] * pi[:, :, None, :]
    wi = c_re[:, None] * pi[:, :, None, :] + c_im[:, None] * pr[:, :, None, :]
    w = jnp.concatenate([wr, -wi], axis=-1)
    bb = jnp.concatenate([bbr, bbi], axis=1)
    kern = jnp.einsum("gxp,gpi->gxi", w[:, :r].reshape(SSM_GROUPS, S5_LANES, 2 * SSM_STATE), bb,
                      precision=lax.Precision.HIGHEST).reshape(SSM_GROUPS, r, SSM_GROUP, SSM_GROUP)
    cpt = w[:, 1:].reshape(SSM_GROUPS, S5_LANES, 2 * SSM_STATE)
    prs = jnp.swapaxes(pr[:, r - 1::-1][:, :r], 1, 2)[..., None]
    pis = jnp.swapaxes(pi[:, r - 1::-1][:, :r], 1, 2)[..., None]
    bp_re = prs * bbr[:, :, None, :] - pis * bbi[:, :, None, :]
    bp_im = prs * bbi[:, :, None, :] + pis * bbr[:, :, None, :]
    bpt = jnp.concatenate([bp_re, bp_im], axis=1).reshape(SSM_GROUPS, 2 * SSM_STATE, S5_LANES)
    ar = pr[:, r].reshape(1, SSM_GROUPS * SSM_STATE)
    ai = pi[:, r].reshape(1, SSM_GROUPS * SSM_STATE)
    return kern, cpt, bpt, ar, ai


def _s5_toeplitz(kern):
    r = S5_BLOCK
    cols = [jnp.pad(kern[:, :r - s], ((0, 0), (s, 0), (0, 0), (0, 0))) for s in range(r)]
    return jnp.stack(cols, axis=3).reshape(SSM_GROUPS, S5_LANES, S5_LANES)


S5_OCTET = 128 // SSM_GROUP
S5_STEPS = SSM_GROUPS // S5_OCTET


def _oct_spec(t):
    return pl.BlockSpec((t, 128), lambda j: (0, j))


def _state_spec(nb):
    return pl.BlockSpec((nb, S5_OCTET * SSM_STATE), lambda j: (0, j))


def _gmat_spec(a, b):
    return pl.BlockSpec((S5_OCTET, a, b), lambda j: (j, 0, 0))


def _block_rows(ref, nb):
    return [ref[pl.ds(r, nb, stride=S5_BLOCK), :] for r in range(S5_BLOCK)]


def _group_cols(pieces_t, g):
    return jnp.concatenate([p[SSM_GROUP * g:SSM_GROUP * (g + 1)] for p in pieces_t], axis=0)


def _state_cols(re_t, im_t, g):
    return jnp.concatenate([re_t[SSM_STATE * g:SSM_STATE * (g + 1)], im_t[SSM_STATE * g:SSM_STATE * (g + 1)]], axis=0)


def _s5_project(a, mat, name):
    t = a.shape[0]
    nb = t // S5_BLOCK

    def body(a_ref, m_ref, re_ref, im_ref):
        at = [p.T for p in _block_rows(a_ref, nb)]
        for pair in range(S5_OCTET // 2):
            xs = [_dot(m_ref[2 * pair + k], _group_cols(at, 2 * pair + k).astype(BF16)) for k in (0, 1)]
            lanes = slice(128 * pair, 128 * (pair + 1))
            re_ref[:, lanes] = jnp.concatenate([xs[0][:SSM_STATE], xs[1][:SSM_STATE]], axis=0).T
            im_ref[:, lanes] = jnp.concatenate([xs[0][SSM_STATE:], xs[1][SSM_STATE:]], axis=0).T

    return pl.pallas_call(
        body, name=name, grid=(S5_STEPS,),
        in_specs=[_oct_spec(t), _gmat_spec(2 * SSM_STATE, S5_LANES)],
        out_specs=[_state_spec(nb), _state_spec(nb)],
        out_shape=[jax.ShapeDtypeStruct((nb, SSM_GROUPS * SSM_STATE), F32)] * 2,
        compiler_params=_params(1),
    )(a, mat)


_SCAN_LANES = 1024


def _s5_scan_fwd(xre, xim, ar, ai, name):
    nb = xre.shape[0]
    col = pl.BlockSpec((nb, _SCAN_LANES), lambda j: (0, j))
    par = pl.BlockSpec((1, _SCAN_LANES), lambda j: (0, j))

    def body(xre_ref, xim_ref, ar_ref, ai_ref, hre_ref, him_ref):
        a_r, a_i = ar_ref[...], ai_ref[...]

        def step(b, carry):
            hr, hi = carry
            hre_ref[pl.ds(b, 1), :] = hr
            him_ref[pl.ds(b, 1), :] = hi
            xr, xi = xre_ref[pl.ds(b, 1), :], xim_ref[pl.ds(b, 1), :]
            return a_r * hr - a_i * hi + xr, a_r * hi + a_i * hr + xi

        zero = jnp.zeros((1, _SCAN_LANES), F32)
        lax.fori_loop(0, nb, step, (zero, zero))

    return pl.pallas_call(
        body, name=name, grid=(xre.shape[1] // _SCAN_LANES,),
        in_specs=[col, col, par, par], out_specs=[col, col],
        out_shape=[jax.ShapeDtypeStruct(xre.shape, F32)] * 2,
        compiler_params=_params(1),
    )(xre, xim, ar, ai)


def _s5_scan_bwd(dhre, dhim, hre, him, ar, ai, name):
    nb = dhre.shape[0]
    col = pl.BlockSpec((nb, _SCAN_LANES), lambda j: (0, j))
    par = pl.BlockSpec((1, _SCAN_LANES), lambda j: (0, j))

    def body(dhre_ref, dhim_ref, hre_ref, him_ref, ar_ref, ai_ref, dxre_ref, dxim_ref, dar_ref, dai_ref):
        a_r, a_i = ar_ref[...], ai_ref[...]

        def step(s, carry):
            gr, gi, dar, dai = carry
            b = nb - 1 - s
            dxre_ref[pl.ds(b, 1), :] = gr
            dxim_ref[pl.ds(b, 1), :] = gi
            hr, hi = hre_ref[pl.ds(b, 1), :], him_ref[pl.ds(b, 1), :]
            dar = dar + (hr * gr + hi * gi)
            dai = dai + (hr * gi - hi * gr)
            dr, di = dhre_ref[pl.ds(b, 1), :], dhim_ref[pl.ds(b, 1), :]
            return dr + (a_r * gr + a_i * gi), di + (a_r * gi - a_i * gr), dar, dai

        zero = jnp.zeros((1, _SCAN_LANES), F32)
        _, _, dar, dai = lax.fori_loop(0, nb, step, (zero, zero, zero, zero))
        dar_ref[...] = dar
        dai_ref[...] = dai

    return pl.pallas_call(
        body, name=name, grid=(dhre.shape[1] // _SCAN_LANES,),
        in_specs=[col, col, col, col, par, par], out_specs=[col, col, par, par],
        out_shape=[jax.ShapeDtypeStruct(dhre.shape, F32)] * 2 + [jax.ShapeDtypeStruct(ar.shape, F32)] * 2,
        compiler_params=_params(1),
    )(dhre, dhim, hre, him, ar, ai)


def _s5_outputs(u, hre, him, tm, cpt, d, name):
    t = u.shape[0]
    nb = t // S5_BLOCK

    def body(u_ref, hre_ref, him_ref, tm_ref, cpt_ref, d_ref, y_ref):
        u_rows = _block_rows(u_ref, nb)
        ut = [p.T for p in u_rows]
        hre_t, him_t = hre_ref[...].T, him_ref[...].T
        yts = []
        for g in range(S5_OCTET):
            yts.append(_dot(tm_ref[g], _group_cols(ut, g).astype(BF16))
                       + _dot(cpt_ref[g], _state_cols(hre_t, him_t, g).astype(BF16)))
        for r in range(S5_BLOCK):
            rows = jnp.concatenate([yt[SSM_GROUP * r:SSM_GROUP * (r + 1)] for yt in yts], axis=0)
            y_ref[pl.ds(r, nb, stride=S5_BLOCK), :] = rows.T + d_ref[...] * u_rows[r]

    return pl.pallas_call(
        body, name=name, grid=(S5_STEPS,),
        in_specs=[_oct_spec(t), _state_spec(nb), _state_spec(nb), _gmat_spec(S5_LANES, S5_LANES),
                  _gmat_spec(S5_LANES, 2 * SSM_STATE), _oct_spec(1)],
        out_specs=_oct_spec(t),
        out_shape=jax.ShapeDtypeStruct(u.shape, F32),
        compiler_params=_params(1),
    )(u, hre, him, tm, cpt, d.reshape(1, D_MODEL))


def _s5_backward(dy, u, hre, him, dxre, dxim, tmt, bp, d, name):
    t = u.shape[0]
    nb = t // S5_BLOCK

    def body(dy_ref, u_ref, hre_ref, him_ref, dxre_ref, dxim_ref, tmt_ref, bp_ref, d_ref,
             du_ref, dk_ref, dcpt_ref, dbpt_ref, dd_ref, dtm_scratch):
        dy_rows, u_rows = _block_rows(dy_ref, nb), _block_rows(u_ref, nb)
        dyt, ut = [p.T for p in dy_rows], [p.T for p in u_rows]
        hre_t, him_t = hre_ref[...].T, him_ref[...].T
        dxre_t, dxim_t = dxre_ref[...].T, dxim_ref[...].T
        duts = []
        for g in range(S5_OCTET):
            dyg, ug = _group_cols(dyt, g).astype(BF16), _group_cols(ut, g).astype(BF16)
            hg = _state_cols(hre_t, him_t, g).astype(BF16)
            dxg = _state_cols(dxre_t, dxim_t, g).astype(BF16)
            duts.append(_dot(tmt_ref[g], dyg) + _dot(bp_ref[g], dxg))
            dtm_scratch[...] = _dot_nt(dyg, ug)
            dk = dtm_scratch[:, :SSM_GROUP]
            for s in range(1, S5_BLOCK):
                below = dtm_scratch[SSM_GROUP * s:, SSM_GROUP * s:SSM_GROUP * (s + 1)]
                dk = dk + jnp.concatenate([below, jnp.zeros((SSM_GROUP * s, SSM_GROUP), F32)], axis=0)
            dk_ref[g] = dk
            dcpt_ref[g] = _dot_nt(dyg, hg)
            dbpt_ref[g] = _dot_nt(dxg, ug)
        dd = jnp.zeros((1, 128), F32)
        for r in range(S5_BLOCK):
            rows = jnp.concatenate([dut[SSM_GROUP * r:SSM_GROUP * (r + 1)] for dut in duts], axis=0)
            du_ref[pl.ds(r, nb, stride=S5_BLOCK), :] = rows.T + d_ref[...] * dy_rows[r]
            dd = dd + jnp.sum(dy_rows[r] * u_rows[r], axis=0, keepdims=True)
        dd_ref[...] = dd

    return pl.pallas_call(
        body, name=name, grid=(S5_STEPS,),
        in_specs=[_oct_spec(t), _oct_spec(t), _state_spec(nb), _state_spec(nb), _state_spec(nb), _state_spec(nb),
                  _gmat_spec(S5_LANES, S5_LANES), _gmat_spec(S5_LANES, 2 * SSM_STATE), _oct_spec(1)],
        out_specs=[_oct_spec(t), _gmat_spec(S5_LANES, SSM_GROUP), _gmat_spec(S5_LANES, 2 * SSM_STATE),
                   _gmat_spec(2 * SSM_STATE, S5_LANES), _oct_spec(1)],
        out_shape=[jax.ShapeDtypeStruct(u.shape, F32),
                   jax.ShapeDtypeStruct((SSM_GROUPS, S5_LANES, SSM_GROUP), F32),
                   jax.ShapeDtypeStruct((SSM_GROUPS, S5_LANES, 2 * SSM_STATE), F32),
                   jax.ShapeDtypeStruct((SSM_GROUPS, 2 * SSM_STATE, S5_LANES), F32),
                   jax.ShapeDtypeStruct((1, D_MODEL), F32)],
        scratch_shapes=[pltpu.VMEM((S5_LANES, S5_LANES), F32)],
        compiler_params=_params(1),
    )(dy, u, hre, him, dxre, dxim, tmt, bp, d.reshape(1, D_MODEL))


def _ssm_out_fwd(y, gate, x, w_glu, b_glu, w_out, name, exchange=None):
    t = x.shape[0]
    rows = ROWS_FWD

    def body(y_ref, g_ref, x_ref, wg_ref, bg_ref, wo_ref, o_ref):
        z0 = _gelu(y_ref[...])
        s = _dot(z0.astype(BF16), wg_ref[...]) + bg_ref[...]
        gate_v = g_ref[...]
        a = (z0 * _sigmoid(s)) * (gate_v * _sigmoid(gate_v))
        o_ref[...] = x_ref[...] + _dot(a.astype(BF16), wo_ref[...])

    (xn,), got = _call_hosting(
        exchange, body, (y, gate, x, w_glu, b_glu.reshape(1, D_MODEL), w_out), name=name, steps=t // rows,
        in_specs=[_row_spec(rows, D_MODEL)] * 3 + [_const_spec((D_MODEL, D_MODEL)), _const_spec((1, D_MODEL)),
                                                   _const_spec((D_MODEL, D_MODEL))],
        out_specs=[_row_spec(rows, D_MODEL)],
        out_shape=[jax.ShapeDtypeStruct((t, D_MODEL), F32)])
    return xn, got


def _ssm_out_bwd(dxn, y, gate, w_glu, b_glu, w_out, name, exchange=None):
    t = y.shape[0]
    rows = ROWS_BWD

    def body(dxn_ref, y_ref, g_ref, wg_ref, bg_ref, wo_ref, dy_ref, dg_ref, dwg_ref, dbg_ref, dwo_ref):
        @pl.when(pl.program_id(0) == 0)
        def _():
            dwo_ref[...] = jnp.zeros_like(dwo_ref)
            dwg_ref[...] = jnp.zeros_like(dwg_ref)
            dbg_ref[...] = jnp.zeros_like(dbg_ref)

        yv = y_ref[...]
        z0 = _gelu(yv)
        z0b = z0.astype(BF16)
        sg = _sigmoid(_dot(z0b, wg_ref[...]) + bg_ref[...])
        z = z0 * sg
        gate_v = g_ref[...]
        sgg = _sigmoid(gate_v)
        silu = gate_v * sgg
        dob = dxn_ref[...].astype(BF16)
        da = _dot_nt(dob, wo_ref[...])
        dwo_ref[...] += _dot_tn((z * silu).astype(BF16), dob)
        dz = da * silu
        dg_ref[...] = (da * z * (sgg * (1.0 + gate_v * (1.0 - sgg)))).astype(BF16)
        ds = dz * z0 * (sg * (1.0 - sg))
        dsb = ds.astype(BF16)
        dz0 = dz * sg + _dot_nt(dsb, wg_ref[...])
        dwg_ref[...] += _dot_tn(z0b, dsb)
        dbg_ref[...] += jnp.sum(ds, axis=0, keepdims=True)
        dy_ref[...] = dz0 * _gelu_grad(yv)

    sq = _const_spec((D_MODEL, D_MODEL))
    vec = _const_spec((1, D_MODEL))
    return _call_hosting(
        exchange, body, (dxn, y, gate, w_glu, b_glu.reshape(1, D_MODEL), w_out), name=name, steps=t // rows,
        in_specs=[_row_spec(rows, D_MODEL)] * 3 + [sq, vec, sq],
        out_specs=[_row_spec(rows, D_MODEL), _row_spec(rows, D_MODEL), sq, vec, sq],
        out_shape=[jax.ShapeDtypeStruct((t, D_MODEL), F32), jax.ShapeDtypeStruct((t, D_MODEL), BF16),
                   jax.ShapeDtypeStruct((D_MODEL, D_MODEL), F32), jax.ShapeDtypeStruct((1, D_MODEL), F32),
                   jax.ShapeDtypeStruct((D_MODEL, D_MODEL), F32)])


KV_LANES = GQA * ATTN_BLOCK


def _attn_bias(block_is_first):
    kj = lax.broadcasted_iota(jnp.int32, (2 * ATTN_BLOCK, ATTN_BLOCK), 0)
    qi = lax.broadcasted_iota(jnp.int32, (2 * ATTN_BLOCK, ATTN_BLOCK), 1)
    dist = qi + ATTN_BLOCK - kj
    valid = (dist >= 0) & (dist < ATTN_BLOCK) & (jnp.logical_not(block_is_first) | (kj >= ATTN_BLOCK))
    return jnp.tile(jnp.where(valid, 0.0, NEG_INF).astype(F32), (1, GQA))


def _head_cols(a_t, kvh):
    heads = range(kvh * GQA, (kvh + 1) * GQA)
    return jnp.concatenate([a_t[HEAD_DIM * h:HEAD_DIM * (h + 1)] for h in heads], axis=1)


def _head_rows(a_cols):
    stacked = jnp.concatenate([a_cols[:, ATTN_BLOCK * g:ATTN_BLOCK * (g + 1)] for g in range(GQA)], axis=0)
    return stacked.T


def _kv_rows(prev_ref, cur_ref, kvh):
    lanes = slice(HEAD_DIM * kvh, HEAD_DIM * (kvh + 1))
    return jnp.concatenate([prev_ref[:, lanes], cur_ref[:, lanes]], axis=0).astype(BF16)


def _kv_cols(prev_t, cur_t, kvh):
    rows = slice(HEAD_DIM * kvh, HEAD_DIM * (kvh + 1))
    return jnp.concatenate([prev_t[rows], cur_t[rows]], axis=1).astype(BF16)


def _attn_probs(kk, q_cols, sink_row, bias):
    s = _dot(kk, q_cols) * ATTN_SCALE + bias
    m = jnp.maximum(jnp.max(s, axis=0, keepdims=True), sink_row)
    p = jnp.exp(s - m)
    e_sink = jnp.exp(sink_row - m)
    inv = 1.0 / (jnp.sum(p, axis=0, keepdims=True) + e_sink)
    return p * inv, e_sink * inv


def _sink_cols(sinks):
    return jnp.repeat(sinks, ATTN_BLOCK).reshape(N_KV_HEADS, 1, KV_LANES)


def _attn_fwd(q, k, v, sinks, name, exchange=None):
    t = q.shape[0]
    nblk = t // ATTN_BLOCK

    def body(s_ref, q_ref, kc_ref, kp_ref, vc_ref, vp_ref, o_ref):
        bias = _attn_bias(pl.program_id(0) == 0)
        q_t = q_ref[...].astype(F32).T
        vp_t, vc_t = vp_ref[...].astype(F32).T, vc_ref[...].astype(F32).T
        for kvh in range(N_KV_HEADS):
            p, _ = _attn_probs(_kv_rows(kp_ref, kc_ref, kvh), _head_cols(q_t, kvh).astype(BF16), s_ref[kvh], bias)
            o_cols = _dot(_kv_cols(vp_t, vc_t, kvh), p.astype(BF16))
            o_ref[:, GQA * HEAD_DIM * kvh:GQA * HEAD_DIM * (kvh + 1)] = _head_rows(o_cols)

    cur = lambda i: (i, 0)
    prev = lambda i: (jnp.maximum(i - 1, 0), 0)
    (o,), got = _call_hosting(
        exchange, body, (_sink_cols(sinks), q, k, k, v, v), name=name, steps=nblk,
        in_specs=[_const_spec((N_KV_HEADS, 1, KV_LANES)),
                  pl.BlockSpec((ATTN_BLOCK, Q_DIM), cur),
                  pl.BlockSpec((ATTN_BLOCK, KV_DIM), cur), pl.BlockSpec((ATTN_BLOCK, KV_DIM), prev),
                  pl.BlockSpec((ATTN_BLOCK, KV_DIM), cur), pl.BlockSpec((ATTN_BLOCK, KV_DIM), prev)],
        out_specs=[pl.BlockSpec((ATTN_BLOCK, Q_DIM), cur)],
        out_shape=[jax.ShapeDtypeStruct((t, Q_DIM), F32)])
    return o, got


def _attn_bwd(q, k, v, o, do, sinks, rope, name, exchange=None):
    t = q.shape[0]
    nblk = t // ATTN_BLOCK

    def body(s_ref, q_ref, o_ref, do_ref, kp_ref, kc_ref, vp_ref, vc_ref, cosq_ref, sinq_ref, cosk_ref, sinkey_ref,
             dq_ref, dk_ref, dv_ref, ds_ref, new_k, new_v, wait_k, wait_v, dq_rot):
        n = pl.program_id(0)

        @pl.when(n == 0)
        def _():
            ds_ref[...] = jnp.zeros_like(ds_ref)
            wait_k[...] = jnp.zeros_like(wait_k)
            wait_v[...] = jnp.zeros_like(wait_v)

        @pl.when(n < nblk)
        def _():
            bias = _attn_bias(n == 0)
            q_t, o_t, do_t = q_ref[...].astype(F32).T, o_ref[...].T, do_ref[...].T
            kp_t, kc_t = kp_ref[...].astype(F32).T, kc_ref[...].astype(F32).T
            for kvh in range(N_KV_HEADS):
                q_cols = _head_cols(q_t, kvh).astype(BF16)
                do_cols = _head_cols(do_t, kvh)
                delta = jnp.sum(do_cols * _head_cols(o_t, kvh), axis=0, keepdims=True)
                do_cols = do_cols.astype(BF16)
                p, p_sink = _attn_probs(_kv_rows(kp_ref, kc_ref, kvh), q_cols, s_ref[kvh], bias)
                dp = _dot(_kv_rows(vp_ref, vc_ref, kvh), do_cols)
                ds = (p * (dp - delta) * ATTN_SCALE).astype(BF16)
                lanes = slice(GQA * HEAD_DIM * kvh, GQA * HEAD_DIM * (kvh + 1))
                dq_rot[:, lanes] = _head_rows(_dot(_kv_cols(kp_t, kc_t, kvh), ds))
                head = slice(HEAD_DIM * kvh, HEAD_DIM * (kvh + 1))
                new_k[:, head] = _dot_nt(ds, q_cols)
                new_v[:, head] = _dot_nt(p.astype(BF16), do_cols)
                ds_ref[kvh] += -(p_sink * delta)
            dq_ref[...] = _rope_apply(dq_rot[...], cosq_ref[...], sinq_ref[...], -1.0).astype(BF16)

        @pl.when(n == nblk)
        def _():
            new_k[...] = jnp.zeros_like(new_k)
            new_v[...] = jnp.zeros_like(new_v)

        dk_ref[...] = _rope_apply(wait_k[...] + new_k[:ATTN_BLOCK], cosk_ref[...], sinkey_ref[...], -1.0).astype(BF16)
        dv_ref[...] = (wait_v[...] + new_v[:ATTN_BLOCK]).astype(BF16)
        wait_k[...] = new_k[ATTN_BLOCK:]
        wait_v[...] = new_v[ATTN_BLOCK:]

    cur = lambda i: (jnp.minimum(i, nblk - 1), 0)
    prev = lambda i: (jnp.maximum(i - 1, 0), 0)
    qs = lambda f: pl.BlockSpec((ATTN_BLOCK, Q_DIM), f)
    ks = lambda f: pl.BlockSpec((ATTN_BLOCK, KV_DIM), f)
    sink_spec = _const_spec((N_KV_HEADS, 1, KV_LANES))
    return _call_hosting(
        exchange, body, (_sink_cols(sinks), q, o, do, k, k, v, v, rope[0], rope[1], rope[0], rope[1]),
        name=name, steps=nblk + 1,
        in_specs=[sink_spec, qs(cur), qs(cur), qs(cur), ks(prev), ks(cur), ks(prev), ks(cur),
                  ks(cur), ks(cur), ks(prev), ks(prev)],
        out_specs=[qs(cur), ks(prev), ks(prev), sink_spec],
        out_shape=[jax.ShapeDtypeStruct((t, Q_DIM), BF16), jax.ShapeDtypeStruct((t, KV_DIM), BF16),
                   jax.ShapeDtypeStruct((t, KV_DIM), BF16), jax.ShapeDtypeStruct((N_KV_HEADS, 1, KV_LANES), F32)],
        scratch_shapes=[pltpu.VMEM((2 * ATTN_BLOCK, KV_DIM), F32), pltpu.VMEM((2 * ATTN_BLOCK, KV_DIM), F32),
                        pltpu.VMEM((ATTN_BLOCK, KV_DIM), F32), pltpu.VMEM((ATTN_BLOCK, KV_DIM), F32),
                        pltpu.VMEM((ATTN_BLOCK, Q_DIM), F32)])


def _attn_out_fwd(o, gate, x, w_out, name, exchange=None):
    t = x.shape[0]
    rows = ROWS_FWD

    def body(o_ref, g_ref, x_ref, wo_ref, xn_ref):
        gate_v = g_ref[...]
        a = o_ref[...] * (gate_v * _sigmoid(gate_v))
        xn_ref[...] = x_ref[...] + _dot(a.astype(BF16), wo_ref[...])

    (xn,), got = _call_hosting(
        exchange, body, (o, gate, x, w_out), name=name, steps=t // rows,
        in_specs=[_row_spec(rows, D_MODEL)] * 3 + [_const_spec((D_MODEL, D_MODEL))],
        out_specs=[_row_spec(rows, D_MODEL)],
        out_shape=[jax.ShapeDtypeStruct((t, D_MODEL), F32)])
    return xn, got


def _attn_out_bwd(dxn, o, gate, w_out, name, exchange=None):
    t = o.shape[0]
    rows = ROWS_BWD

    def body(dxn_ref, o_ref, g_ref, wo_ref, do_ref, dg_ref, dwo_ref):
        @pl.when(pl.program_id(0) == 0)
        def _():
            dwo_ref[...] = jnp.zeros_like(dwo_ref)

        gate_v, ov = g_ref[...], o_ref[...]
        sgg = _sigmoid(gate_v)
        silu = gate_v * sgg
        dob = dxn_ref[...].astype(BF16)
        da = _dot_nt(dob, wo_ref[...])
        dwo_ref[...] += _dot_tn((ov * silu).astype(BF16), dob)
        do_ref[...] = da * silu
        dg_ref[...] = (da * ov * (sgg * (1.0 + gate_v * (1.0 - sgg)))).astype(BF16)

    sq = _const_spec((D_MODEL, D_MODEL))
    return _call_hosting(
        exchange, body, (dxn, o, gate, w_out), name=name, steps=t // rows,
        in_specs=[_row_spec(rows, D_MODEL)] * 3 + [sq],
        out_specs=[_row_spec(rows, D_MODEL), _row_spec(rows, D_MODEL), sq],
        out_shape=[jax.ShapeDtypeStruct((t, D_MODEL), F32), jax.ShapeDtypeStruct((t, D_MODEL), BF16),
                   jax.ShapeDtypeStruct((D_MODEL, D_MODEL), F32)])


def _loss_head(x, norm, target, name):
    t = x.shape[0]
    rows = ROWS_FWD

    def body(x_ref, n_ref, t_ref, loss_ref, dx_ref, dn_ref):
        i = pl.program_id(0)
        xv = x_ref[...]
        rstd = lax.rsqrt(jnp.mean(xv * xv, axis=-1, keepdims=True) + NORM_EPS)
        xhat = xv * rstd
        err = xhat * n_ref[...] - t_ref[...]
        part = 0.5 * jnp.sum(jnp.mean(err * err, axis=-1, keepdims=True), axis=0, keepdims=True)
        dy = err * (1.0 / D_MODEL)
        dn = jnp.sum(dy * xhat, axis=0, keepdims=True)
        dxhat = dy * n_ref[...]
        dx_ref[...] = rstd * (dxhat - xhat * jnp.mean(dxhat * xhat, axis=-1, keepdims=True))

        @pl.when(i == 0)
        def _():
            loss_ref[...] = jnp.zeros((8, 128), F32) + part
            dn_ref[...] = dn

        @pl.when(i > 0)
        def _():
            loss_ref[...] += part
            dn_ref[...] += dn

    return pl.pallas_call(
        body, name=name, grid=(t // rows,),
        in_specs=[_row_spec(rows, D_MODEL), _const_spec((1, D_MODEL)), _row_spec(rows, D_MODEL)],
        out_specs=[_const_spec((8, 128)), _row_spec(rows, D_MODEL), _const_spec((1, D_MODEL))],
        out_shape=[jax.ShapeDtypeStruct((8, 128), F32), jax.ShapeDtypeStruct((t, D_MODEL), F32),
                   jax.ShapeDtypeStruct((1, D_MODEL), F32)],
        compiler_params=_params(1),
    )(x, norm.reshape(1, D_MODEL), target)


N_CHIPS = 4
N_CORES = 2
CHIP_FLIPS = ((0, 1), (1, 0), (1, 1))
ICI_CHUNKS = 2
D2D_CHUNKS = 8


def _n_chunks(rows, dtype, most):
    unit = 16 if dtype == BF16 else 8
    return max(n for n in range(1, most + 1) if rows % n == 0 and (rows // n) % unit == 0)


def _chunks_of(arrays, most):
    out = []
    for a in arrays:
        n = _n_chunks(a.shape[-2], a.dtype, most)
        out.append((n, a.shape[-2] // n))
    return out


class _Exchange:
    def __init__(self, arrays, out_shape, scratch, copies):
        self.arrays, self.out_shape, self.scratch, self._copies = arrays, out_shape, scratch, copies

    def start(self, *refs):
        for cp in self._copies(*refs)[0]:
            cp.start()

    def wait(self, *refs):
        for wait in self._copies(*refs)[1]:
            wait()


def _chips_exchange(sends, per_dest):
    n = len(sends)
    chunking = _chunks_of(sends, ICI_CHUNKS)

    def copies(send_refs, recv_refs, sems):
        x, y, c = lax.axis_index("x"), lax.axis_index("y"), lax.axis_index("c")
        me = 2 * x + y

        def peer(k):
            fx, fy = CHIP_FLIPS[k]
            px, py = x + fx - 2 * x * fx, y + fy - 2 * y * fy
            return (px, py, c), 2 * px + py

        to_start, waits = [], []
        for a in range(n):
            send_sems, recv_sems, local_sems = sems[3 * a:3 * a + 3]
            chunks, chunk_rows = chunking[a]
            for j in range(chunks):
                part = pl.ds(j * chunk_rows, chunk_rows)
                src = lambda number: send_refs[a].at[number, part] if per_dest else send_refs[a].at[part]
                for k in range(len(CHIP_FLIPS)):
                    to, to_number = peer(k)
                    remote = lambda landing: pltpu.make_async_remote_copy(
                        src_ref=src(to_number), dst_ref=recv_refs[a].at[landing, part],
                        send_sem=send_sems.at[k, j], recv_sem=recv_sems.at[k, j],
                        device_id=to, device_id_type=pl.DeviceIdType.MESH)
                    to_start.append(remote(me))
                    waits += [remote(me).wait_send, remote(to_number).wait_recv]
                own = pltpu.make_async_copy(src(me), recv_refs[a].at[me, part], local_sems.at[j])
                to_start.append(own)
                waits.append(own.wait)
        return to_start, waits

    scratch = []
    for chunks, _ in chunking:
        scratch += [pltpu.SemaphoreType.DMA((len(CHIP_FLIPS), chunks)), pltpu.SemaphoreType.DMA((len(CHIP_FLIPS), chunks)),
                    pltpu.SemaphoreType.DMA((chunks,))]
    return _Exchange(sends, [jax.ShapeDtypeStruct((N_CHIPS,) + a.shape[-2:], a.dtype) for a in sends], scratch, copies)


def _cores_exchange(sends, per_dest):
    n = len(sends)
    chunking = _chunks_of(sends, D2D_CHUNKS)

    def copies(send_refs, got_refs, sems):
        c = lax.axis_index("c")
        sibling = (lax.axis_index("x"), lax.axis_index("y"), 1 - c)
        to_start = []
        for a in range(n):
            chunks, chunk_rows = chunking[a]
            for j in range(chunks):
                part = pl.ds(j * chunk_rows, chunk_rows)
                to_start.append(pltpu.make_async_remote_copy(
                    src_ref=send_refs[a].at[1 - c, part] if per_dest else send_refs[a].at[part],
                    dst_ref=got_refs[a].at[part], send_sem=sems[2 * a].at[j], recv_sem=sems[2 * a + 1].at[j],
                    device_id=sibling, device_id_type=pl.DeviceIdType.MESH))
        return to_start, [cp.wait for cp in to_start]

    scratch = []
    for chunks, _ in chunking:
        scratch += [pltpu.SemaphoreType.DMA((chunks,)), pltpu.SemaphoreType.DMA((chunks,))]
    return _Exchange(sends, [jax.ShapeDtypeStruct(a.shape[-2:], a.dtype) for a in sends], scratch, copies)


def _run_exchange(exchange, name):
    n = len(exchange.arrays)

    def body(*refs):
        parts = refs[:n], refs[n:2 * n], refs[2 * n:]
        exchange.start(*parts)
        exchange.wait(*parts)

    hbm = pl.BlockSpec(memory_space=pltpu.HBM)
    return pl.pallas_call(body, name=name, in_specs=[hbm] * n, out_specs=[hbm] * n, out_shape=exchange.out_shape,
                          scratch_shapes=exchange.scratch)(*exchange.arrays)


def _call_hosting(exchange, body, args, *, name, steps, in_specs, out_specs, out_shape, scratch_shapes=()):
    common = dict(name=name, grid=(steps,), compiler_params=_params(1))
    if exchange is None:
        return pl.pallas_call(body, in_specs=in_specs, out_specs=out_specs, out_shape=out_shape,
                              scratch_shapes=list(scratch_shapes), **common)(*args), None
    n_in, n_out, n_scratch, k = len(in_specs), len(out_specs), len(scratch_shapes), len(exchange.arrays)

    def hosting(*refs):
        ins, sends = refs[:n_in], refs[n_in:n_in + k]
        outs, recvs = refs[n_in + k:n_in + k + n_out], refs[n_in + k + n_out:n_in + 2 * k + n_out]
        scratch = refs[n_in + 2 * k + n_out:n_in + 2 * k + n_out + n_scratch]
        sems = refs[n_in + 2 * k + n_out + n_scratch:]
        pl.when(pl.program_id(0) == 0)(lambda: exchange.start(sends, recvs, sems))
        body(*ins, *outs, *scratch)
        pl.when(pl.program_id(0) == steps - 1)(lambda: exchange.wait(sends, recvs, sems))

    hbm = pl.BlockSpec(memory_space=pltpu.HBM)
    results = pl.pallas_call(
        hosting, in_specs=list(in_specs) + [hbm] * k, out_specs=list(out_specs) + [hbm] * k,
        out_shape=list(out_shape) + exchange.out_shape, scratch_shapes=list(scratch_shapes) + exchange.scratch, **common,
    )(*args, *exchange.arrays)
    return results[:n_out], results[n_out:]


def _exchange_chips(sends, per_dest, name):
    return _run_exchange(_chips_exchange(sends, per_dest), name)


def _swap_cores(sends, per_dest, name):
    return _run_exchange(_cores_exchange(sends, per_dest), name)


def _all_gather(arrays, name):
    by_chip = _exchange_chips(arrays, False, name + "_chips")
    others = _swap_cores([r.reshape(-1, r.shape[-1]) for r in by_chip], False, name + "_cores")
    return [(m, o.reshape(m.shape)) for m, o in zip(by_chip, others)]


def _in_device_order(mine, other, axis):
    first = lax.axis_index("c") == 0
    pieces = []
    for m, o in zip(mine, other):
        pieces += [jnp.where(first, m, o), jnp.where(first, o, m)]
    return jnp.concatenate(pieces, axis=axis)


def _sum_core(send, got, out_dtype, name):
    _, n, cols = send.shape
    rows = min(n, 256)
    while n % rows:
        rows -= 16

    def body(c_ref, keep_ref, got_ref, o_ref):
        o_ref[...] = (keep_ref[...].astype(F32) + got_ref[...].astype(F32)).astype(out_dtype)

    return pl.pallas_call(
        body, name=name, out_shape=jax.ShapeDtypeStruct((n, cols), out_dtype),
        grid_spec=pltpu.PrefetchScalarGridSpec(
            num_scalar_prefetch=1, grid=(n // rows,),
            in_specs=[pl.BlockSpec((None, rows, cols), lambda i, c: (c[0], i, 0)),
                      pl.BlockSpec((rows, cols), lambda i, c: (i, 0))],
            out_specs=pl.BlockSpec((rows, cols), lambda i, c: (i, 0))),
        compiler_params=_params(1),
    )(lax.axis_index("c").astype(jnp.int32).reshape(1), send, got)


def _sum_parts(parts, out_dtype, name):
    n, cols = parts[0].shape
    rows = min(n, 256)
    while n % rows:
        rows -= 16

    def body(*refs):
        acc = refs[0][...].astype(F32)
        for ref in refs[1:-1]:
            acc = acc + ref[...].astype(F32)
        refs[-1][...] = acc.astype(out_dtype)

    return pl.pallas_call(
        body, name=name, grid=(n // rows,),
        in_specs=[_row_spec(rows, cols)] * len(parts),
        out_specs=_row_spec(rows, cols),
        out_shape=jax.ShapeDtypeStruct((n, cols), out_dtype),
        compiler_params=_params(1),
    )(*parts)


def _reduce_scatter(sends, wire_dtypes, name):
    halves = [s.reshape(N_CORES, N_CHIPS * s.shape[2], s.shape[3]) for s in sends]
    gots = _swap_cores(halves, True, name + "_cores")
    sums = [_sum_core(h, g, dt, "%s_core_sum%d" % (name, i)).reshape((N_CHIPS,) + s.shape[2:])
            for i, (h, g, dt, s) in enumerate(zip(halves, gots, wire_dtypes, sends))]
    return _exchange_chips(sums, True, name + "_chips")


def _adamw(parts, w, m, v, name):
    n, cols = w.shape
    k = parts.shape[0]
    rows = min(n, 256)
    while n % rows:
        rows -= 8
    c1 = 1.0 - ADAM_B1 ** ADAM_STEP
    c2 = 1.0 - ADAM_B2 ** ADAM_STEP

    def body(p_ref, w_ref, m_ref, v_ref, g_ref, d_ref, nm_ref, nv_ref):
        g = p_ref[0].astype(F32)
        for s in range(1, k):
            g = g + p_ref[s].astype(F32)
        nm = ADAM_B1 * m_ref[...] + (1.0 - ADAM_B1) * g
        nv = ADAM_B2 * v_ref[...] + (1.0 - ADAM_B2) * (g * g)
        g_ref[...] = g
        nm_ref[...] = nm
        nv_ref[...] = nv
        d_ref[...] = -ADAM_LR * ((nm / c1) / (jnp.sqrt(nv / c2) + ADAM_EPS) + ADAM_WD * w_ref[...])

    blk = _row_spec(rows, cols)
    return pl.pallas_call(
        body, name=name, grid=(n // rows,),
        in_specs=[pl.BlockSpec((k, rows, cols), lambda i: (0, i, 0)), blk, blk, blk],
        out_specs=[blk] * 4,
        out_shape=[jax.ShapeDtypeStruct((n, cols), F32)] * 4,
        compiler_params=_params(1),
    )(parts, w, m, v)


SSM_KEYS = ("norm", "w_in", "a_re", "a_im", "log_step", "b_re", "b_im", "c_re", "c_im", "d", "w_glu", "b_glu", "w_out")
ATTN_KEYS = ("norm", "w_in", "sinks", "w_out")
LAYER_KEYS = (SSM_KEYS, ATTN_KEYS, SSM_KEYS, ATTN_KEYS)
BIG_KEYS = ("w_in", "w_glu", "w_out")
ATTN_SPLITS = (Q_DIM, KV_DIM, KV_DIM, D_MODEL)


def _rope_tables(t):
    pos = jnp.arange(t, dtype=F32)
    inv_freq = ROPE_THETA ** (-jnp.arange(0, HEAD_DIM, 2, dtype=F32) / HEAD_DIM)
    ang = pos[:, None] * inv_freq[None, :]
    cos, sin = jnp.cos(ang), jnp.sin(ang)
    return jnp.tile(jnp.concatenate([cos, cos], axis=1), (1, 2)), jnp.tile(jnp.concatenate([-sin, sin], axis=1), (1, 2))


def _gather_ici_stage(gather):
    return None if gather is None else _chips_exchange(gather, False)


def _gather_d2d_stage(by_chip):
    return None if by_chip is None else _cores_exchange([r.reshape(-1, r.shape[-1]) for r in by_chip], False)


def _gathered(by_chip, others):
    return None if by_chip is None else [(m, other.reshape(m.shape)) for m, other in zip(by_chip, others)]


def _scatter_d2d_stage(scatter):
    if scatter is None:
        return None, None
    halves = [s.reshape(N_CORES, N_CHIPS * s.shape[2], s.shape[3]) for s in scatter[0]]
    return halves, _cores_exchange(halves, True)


def _scatter_ici_stage(scatter, halves, gots, tag):
    if scatter is None:
        return None
    sums = [_sum_core(h, g, dt, "%sscatter_core_sum%d" % (tag, j)).reshape((N_CHIPS,) + s.shape[2:])
            for j, (h, g, dt, s) in enumerate(zip(halves, gots, scatter[1], scatter[0]))]
    return _chips_exchange(sums, True)


def _ssm_layer_fwd(i, x, p, w, gather=None):
    tag = "l%d_" % i
    mats, mats_vjp = jax.vjp(_s5_matrices, p["a_re"], p["a_im"], p["log_step"], p["b_re"], p["b_im"], p["c_re"], p["c_im"])
    kern, cpt, bpt, ar, ai = mats
    tm = _s5_toeplitz(kern)
    mb = dict(tm=tm.astype(BF16), tmt=jnp.swapaxes(tm, 1, 2).astype(BF16), cpt=cpt.astype(BF16),
              cp=jnp.swapaxes(cpt, 1, 2).astype(BF16), bpt=bpt.astype(BF16), bp=jnp.swapaxes(bpt, 1, 2).astype(BF16))
    (u, gate), by_chip = _inproj_fwd(x, p["norm"], w["w_in"], (D_MODEL, D_MODEL), (F32, F32), None, tag + "inproj_fwd",
                                     _gather_ici_stage(gather))
    xre, xim = _s5_project(u, mb["bpt"], tag + "s5_block_inputs")
    hre, him = _s5_scan_fwd(xre, xim, ar, ai, tag + "s5_scan_fwd")
    y = _s5_outputs(u, hre, him, mb["tm"], mb["cpt"], p["d"], tag + "s5_outputs")
    xn, others = _ssm_out_fwd(y, gate, x, w["w_glu"], p["b_glu"], w["w_out"], tag + "out_fwd", _gather_d2d_stage(by_chip))
    return xn, (x, u, gate, y, hre, him, mb, ar, ai, mats_vjp), _gathered(by_chip, others)


def _ssm_layer_bwd(i, dxn, saved, p, w, scatter=None):
    tag = "l%d_" % i
    x, u, gate, y, hre, him, mb, ar, ai, mats_vjp = saved
    halves, d2d_stage = _scatter_d2d_stage(scatter)
    (dy, dgate, dw_glu, db_glu, dw_out), gots = _ssm_out_bwd(dxn, y, gate, w["w_glu"], p["b_glu"], w["w_out"], tag + "out_bwd",
                                                             d2d_stage)
    dhre, dhim = _s5_project(dy, mb["cp"], tag + "s5_state_grads")
    dxre, dxim, dar, dai = _s5_scan_bwd(dhre, dhim, hre, him, ar, ai, tag + "s5_scan_bwd")
    du, dk, dcpt, dbpt, dd = _s5_backward(dy, u, hre, him, dxre, dxim, mb["tmt"], mb["bp"], p["d"], tag + "s5_backward")
    dk = dk.reshape(SSM_GROUPS, S5_BLOCK, SSM_GROUP, SSM_GROUP)
    da_re, da_im, dlog_step, db_re, db_im, dc_re, dc_im = mats_vjp((dk, dcpt, dbpt, dar, dai))
    (dx, dw_in, dnorm), parts = _inproj_bwd(x, p["norm"], w["w_in"], [du, dgate], dxn, tag + "inproj_bwd",
                                            _scatter_ici_stage(scatter, halves, gots, tag))
    grads = dict(norm=dnorm.reshape(D_MODEL), w_in=dw_in, a_re=da_re, a_im=da_im, log_step=dlog_step, b_re=db_re,
                 b_im=db_im, c_re=dc_re, c_im=dc_im, d=dd.reshape(D_MODEL), w_glu=dw_glu, b_glu=db_glu.reshape(D_MODEL),
                 w_out=dw_out)
    return dx, grads, parts


def _attn_layer_fwd(i, x, p, w, rope, gather=None):
    tag = "l%d_" % i
    (q, k, v, gate), _ = _inproj_fwd(x, p["norm"], w["w_in"], ATTN_SPLITS, (BF16, BF16, BF16, F32), rope, tag + "inproj_fwd")
    o, by_chip = _attn_fwd(q, k, v, p["sinks"], tag + "attn_fwd", _gather_ici_stage(gather))
    xn, others = _attn_out_fwd(o, gate, x, w["w_out"], tag + "out_fwd", _gather_d2d_stage(by_chip))
    return xn, (x, q, k, v, gate, o), _gathered(by_chip, others)


def _attn_layer_bwd(i, dxn, saved, p, w, rope, scatter=None):
    tag = "l%d_" % i
    x, q, k, v, gate, o = saved
    halves, d2d_stage = _scatter_d2d_stage(scatter)
    (do, dgate, dw_out), gots = _attn_out_bwd(dxn, o, gate, w["w_out"], tag + "out_bwd", d2d_stage)
    (dq, dk, dv, dsinks), parts = _attn_bwd(q, k, v, o, do, p["sinks"], rope, tag + "attn_bwd",
                                            _scatter_ici_stage(scatter, halves, gots, tag))
    (dx, dw_in, dnorm), _ = _inproj_bwd(x, p["norm"], w["w_in"], [dq, dk, dv, dgate], dxn, tag + "inproj_bwd")
    grads = dict(norm=dnorm.reshape(D_MODEL), w_in=dw_in, sinks=dsinks.reshape(N_Q_HEADS, ATTN_BLOCK).sum(axis=1), w_out=dw_out)
    return dx, grads, parts


def _local_step(x, target, small, big, carried=()):
    rope = _rope_tables(x.shape[0])
    carried = {h: rest for h, *rest in carried}
    big = list(big)
    saved = []
    for i in range(4):
        gather = carried[i][0] if i in carried else None
        if i % 2 == 0:
            x, s, gathered = _ssm_layer_fwd(i, x, small[i], big[i], gather)
        else:
            x, s, gathered = _attn_layer_fwd(i, x, small[i], big[i], rope, gather)
        if gathered is not None:
            for layer, matrices in carried[i][1](gathered).items():
                big[layer] = matrices
        saved.append(s)
    loss, dx, dfinal = _loss_head(x, small[4]["norm"], target, "loss_head")
    grads = [None] * 4 + [dict(norm=dfinal.reshape(D_MODEL))]
    parts = {}
    for i in (3, 2, 1, 0):
        scatter = carried[i][2](grads) if i in carried else None
        if i % 2 == 0:
            dx, grads[i], parts[i] = _ssm_layer_bwd(i, dx, saved[i], small[i], big[i], scatter)
        else:
            dx, grads[i], parts[i] = _attn_layer_bwd(i, dx, saved[i], small[i], big[i], rope, scatter)
    return loss[0, 0], dx, grads, parts


def _owner_major(key, g):
    if key == "w_in":
        return g.reshape(D_MODEL, N_CHIPS, N_CORES, -1).transpose(2, 1, 0, 3)
    return g.reshape(N_CHIPS, N_CORES, -1, D_MODEL).transpose(1, 0, 2, 3)


def _from_gathered(key, mine, other):
    return _in_device_order(list(mine), list(other), 1 if key == "w_in" else 0)


SMALL_ROWS = 72


def _rows_of_small(a):
    flat = a.reshape(-1)
    return jnp.pad(flat, (0, -flat.shape[0] % D_MODEL)).reshape(-1, D_MODEL)


def _stack_small(arrays):
    rows = jnp.concatenate([_rows_of_small(a) for a in arrays], axis=0)
    assert rows.shape[0] <= N_DEV * SMALL_ROWS
    return jnp.pad(rows, ((0, N_DEV * SMALL_ROWS - rows.shape[0]), (0, 0)))


def kernel(*args):
    names = ["x"]
    layer_names = []
    for i, keys in enumerate(LAYER_KEYS):
        layer_names += ["l%d_%s" % (i, k) for k in keys]
    layer_names.append("final_norm")
    names += layer_names + ["loss_target"] + ["m_" + n for n in layer_names] + ["v_" + n for n in layer_names]
    given = dict(zip(names, args))
    big_names = [n for n in layer_names if n.split("_", 1)[1] in BIG_KEYS]
    small_names = [n for n in layer_names if n not in big_names]

    offsets = {}

    def families_of(layers):
        families = {}
        for n in big_names:
            if int(n[1]) in layers:
                family = families.setdefault(given[n].shape[1], [])
                offsets[n] = sum(given[other].shape[0] for other in family)
                family.append(n)
        return list(families.values())

    first, carried_by = families_of((0,)), {0: families_of((1,)), 1: families_of((2, 3))}
    stack = lambda pre, family: jnp.concatenate([given[pre + n] for n in family], axis=0)
    rows_of = lambda a, n: a[..., offsets[n]:offsets[n] + given[n].shape[0], :]
    local = lambda families: [stack("", family).astype(BF16) for family in families]

    def assemble(families, gathered):
        big = {}
        for family, (mine, other) in zip(families, gathered):
            for n in family:
                matrices = big.setdefault(int(n[1]), {})
                matrices[n.split("_", 1)[1]] = _from_gathered(n.split("_", 1)[1], rows_of(mine, n), rows_of(other, n))
        return big

    def sends_of(families, grads):
        return [jnp.concatenate([_owner_major(n.split("_", 1)[1], grads[int(n[1])][n.split("_", 1)[1]]) for n in family], axis=2)
                for family in families]

    small = [dict() for _ in range(5)]
    for n in small_names:
        if n == "final_norm":
            small[4]["norm"] = given[n]
        else:
            small[int(n[1])][n.split("_", 1)[1]] = given[n]

    big = [assemble(first, _all_gather(local(first), "gather_first_weights"))[0], None, None, None]
    carried = [(h, local(families), functools.partial(assemble, families),
                lambda grads, families=families: (sends_of(families, grads), [BF16] * len(families)))
               for h, families in carried_by.items()]
    loss, dx, grads, carried_parts = _local_step(given["x"][0], given["loss_target"][0], small, big, carried)
    loss = lax.psum(loss, ("x", "y", "c"))

    def grad_of(n):
        return grads[4]["norm"] if n == "final_norm" else grads[int(n[1])][n.split("_", 1)[1]]

    flat = lambda f: _stack_small([f(n) for n in small_names])
    sends = sends_of(first, grads) + [flat(grad_of).reshape(N_CORES, N_CHIPS, SMALL_ROWS, D_MODEL)]
    parts = _reduce_scatter(sends, [BF16] * len(first) + [F32], "scatter_grads")

    outs = {}
    tags = ("grad_", "delta_", "new_m_", "new_v_")
    all_families = first + carried_by[0] + carried_by[1]
    all_parts = list(parts[:-1]) + list(carried_parts[0]) + list(carried_parts[1])
    for i, (family, part) in enumerate(zip(all_families, all_parts)):
        results = _adamw(part, stack("", family), stack("m_", family), stack("v_", family), "adamw_matrices%d" % i)
        for tag, a in zip(tags, results):
            for n in family:
                outs[tag + n] = rows_of(a, n)

    my_slice = _sum_parts([parts[-1][s] for s in range(N_CHIPS)], F32, "sum_small_grads")
    mine, other = _all_gather([my_slice], "gather_small_grads")[0]
    first = lax.axis_index("c") == 0
    g_all = jnp.concatenate([jnp.where(first, mine, other), jnp.where(first, other, mine)], axis=0)
    g_all = g_all.reshape(1, N_DEV * SMALL_ROWS, D_MODEL)
    results = _adamw(g_all, flat(lambda n: given[n]), flat(lambda n: given["m_" + n]), flat(lambda n: given["v_" + n]),
                     "adamw_small")
    for tag, a in zip(tags, results):
        at = 0
        for n in small_names:
            rows = -(-given[n].size // D_MODEL)
            outs[tag + n] = a[at:at + rows].reshape(-1)[:given[n].size].reshape(given[n].shape)
            at += rows
    result = [loss, dx[None]]
    for tag in ("grad_", "delta_", "new_m_", "new_v_"):
        result += [outs[tag + n] for n in layer_names]
    return tuple(result)
```

```python
import functools
import math

import jax
import jax.numpy as jnp
from jax import lax
from jax.experimental import pallas as pl
from jax.experimental.pallas import tpu as pltpu

F32 = jnp.float32
BF16 = jnp.bfloat16

D_MODEL = 1024
SSM_GROUP = 16
SSM_GROUPS = D_MODEL // SSM_GROUP
SSM_STATE = 64
S5_BLOCK = 16
S5_LANES = S5_BLOCK * SSM_GROUP
HEAD_DIM = 64
N_Q_HEADS = 16
N_KV_HEADS = 2
GQA = N_Q_HEADS // N_KV_HEADS
Q_DIM = N_Q_HEADS * HEAD_DIM
KV_DIM = N_KV_HEADS * HEAD_DIM
ATTN_BLOCK = 128
ROPE_THETA = 10000.0
NORM_EPS = 1e-5
NEG_INF = -1e30
ATTN_SCALE = HEAD_DIM ** -0.5
N_DEV = 8

ADAM_LR = 0.001
ADAM_B1 = 0.9
ADAM_B2 = 0.999
ADAM_EPS = 1e-08
ADAM_WD = 0.01
ADAM_STEP = 10

VMEM_LIMIT = 56 * 1024 * 1024
ROWS_FWD = 512
ROWS_BWD = 512

NT = (((1,), (1,)), ((), ()))
TN = (((0,), (0,)), ((), ()))


def _params(n_grid):
    return pltpu.CompilerParams(dimension_semantics=("arbitrary",) * n_grid, vmem_limit_bytes=VMEM_LIMIT)


def _dot(a, b):
    return jnp.dot(a, b, preferred_element_type=F32)


def _dot_nt(a, b):
    return lax.dot_general(a, b, NT, preferred_element_type=F32)


def _dot_tn(a, b):
    return lax.dot_general(a, b, TN, preferred_element_type=F32)


def _sigmoid(x):
    return 1.0 / (1.0 + jnp.exp(-x))


_GELU_K = math.sqrt(2.0 / math.pi)


def _gelu(x):
    return x * (0.5 * (1.0 + jnp.tanh(_GELU_K * (x + 0.044715 * (x * x * x)))))


def _gelu_grad(x):
    t = jnp.tanh(_GELU_K * (x + 0.044715 * (x * x * x)))
    return 0.5 * (1.0 + t) + 0.5 * x * (1.0 - t * t) * (_GELU_K * (1.0 + 3.0 * 0.044715 * (x * x)))


def _row_spec(rows, cols):
    return pl.BlockSpec((rows, cols), lambda i: (i, 0))


def _const_spec(shape):
    zeros = (0,) * len(shape)
    return pl.BlockSpec(shape, lambda i: zeros, pipeline_mode=pl.Buffered(1))


def _rope_apply(t, cos, sin_signed, sign):
    lane = lax.broadcasted_iota(jnp.int32, (1, 128), 1)
    first_half = (lane % HEAD_DIM) < (HEAD_DIM // 2)
    out = []
    for j in range(t.shape[1] // 128):
        tj = t[:, 128 * j:128 * (j + 1)]
        partner = jnp.where(first_half, pltpu.roll(tj, 128 - HEAD_DIM // 2, 1), pltpu.roll(tj, HEAD_DIM // 2, 1))
        out.append(tj * cos + sign * (partner * sin_signed))
    return out[0] if len(out) == 1 else jnp.concatenate(out, axis=1)


def _inproj_fwd(x, norm, w, splits, dtypes, rope, name, exchange=None):
    t = x.shape[0]
    n = w.shape[1]
    rows = ROWS_FWD

    def body(*refs):
        if rope is None:
            x_ref, n_ref, w_ref = refs[:3]
            outs = refs[3:]
        else:
            x_ref, n_ref, w_ref, cos_ref, sin_ref = refs[:5]
            outs = refs[5:]
        xv = x_ref[...]
        rstd = lax.rsqrt(jnp.mean(xv * xv, axis=-1, keepdims=True) + NORM_EPS)
        h = (xv * rstd) * n_ref[...]
        proj = _dot(h.astype(BF16), w_ref[...])
        off = 0
        for i, width in enumerate(splits):
            piece = proj[:, off:off + width]
            if rope is not None and i < 2:
                piece = _rope_apply(piece, cos_ref[...], sin_ref[...], 1.0)
            outs[i][...] = piece.astype(dtypes[i])
            off += width

    in_specs = [_row_spec(rows, D_MODEL), _const_spec((1, D_MODEL)), _const_spec((D_MODEL, n))]
    args = [x, norm.reshape(1, D_MODEL), w]
    if rope is not None:
        in_specs += [_row_spec(rows, 128), _row_spec(rows, 128)]
        args += list(rope)
    return _call_hosting(
        exchange, body, args, name=name, steps=t // rows, in_specs=in_specs,
        out_specs=[_row_spec(rows, width) for width in splits],
        out_shape=[jax.ShapeDtypeStruct((t, width), dtype) for width, dtype in zip(splits, dtypes)])


def _inproj_bwd(x, norm, w, dpieces, dxn, name):
    t = x.shape[0]
    n = w.shape[1]
    rows = ROWS_BWD
    widths = [p.shape[1] for p in dpieces]
    k = len(dpieces)

    def body(*refs):
        x_ref, n_ref, w_ref, dxn_ref = refs[:4]
        d_refs = refs[4:4 + k]
        dx_ref, dw_ref, dn_ref = refs[4 + k:]
        @pl.when(pl.program_id(0) == 0)
        def _():
            dw_ref[...] = jnp.zeros_like(dw_ref)
            dn_ref[...] = jnp.zeros_like(dn_ref)

        xv = x_ref[...]
        rstd = lax.rsqrt(jnp.mean(xv * xv, axis=-1, keepdims=True) + NORM_EPS)
        xhat = xv * rstd
        h = xhat * n_ref[...]
        dproj = [r[...].astype(BF16) for r in d_refs]
        dproj = dproj[0] if k == 1 else jnp.concatenate(dproj, axis=1)
        dh = _dot_nt(dproj, w_ref[...])
        dw_ref[...] += _dot_tn(h.astype(BF16), dproj)
        dn_ref[...] += jnp.sum(dh * xhat, axis=0, keepdims=True)
        dxhat = dh * n_ref[...]
        dx_ref[...] = rstd * (dxhat - xhat * jnp.mean(dxhat * xhat, axis=-1, keepdims=True)) + dxn_ref[...]

    return _call_hosting(
        None, body, (x, norm.reshape(1, D_MODEL), w, dxn, *dpieces), name=name, steps=t // rows,
        in_specs=[_row_spec(rows, D_MODEL), _const_spec((1, D_MODEL)), _const_spec((D_MODEL, n)),
                  _row_spec(rows, D_MODEL)] + [_row_spec(rows, width) for width in widths],
        out_specs=[_row_spec(rows, D_MODEL), _const_spec((D_MODEL, n)), _const_spec((1, D_MODEL))],
        out_shape=[jax.ShapeDtypeStruct((t, D_MODEL), F32), jax.ShapeDtypeStruct((D_MODEL, n), F32),
                   jax.ShapeDtypeStruct((1, D_MODEL), F32)])[0]


def _s5_matrices(a_re, a_im, log_step, b_re, b_im, c_re, c_im):
    r = S5_BLOCK
    step = jnp.exp(log_step)[:, None]
    lr, li = a_re * step, a_im * step
    k = jnp.arange(r + 1, dtype=F32)
    mag = jnp.exp(lr[:, None, :] * k[:, None])
    pr = mag * jnp.cos(li[:, None, :] * k[:, None])
    pi = mag * jnp.sin(li[:, None, :] * k[:, None])
    nr, ni = pr[:, 1] - 1.0, pi[:, 1]
    den = a_re * a_re + a_im * a_im
    qr, qi = (nr * a_re + ni * a_im) / den, (ni * a_re - nr * a_im) / den
    bbr = qr[..., None] * b_re - qi[..., None] * b_im
    bbi = qr[..., None] * b_im + qi[..., None] * b_re
    wr = c_re[:, None] * pr[:, :, None, :] - c_im[:, None] * pi[:, :, None, :]
    wi = c_re[:, None] * pi[:, :, None, :] + c_im[:, None] * pr[:, :, None, :]
    w = jnp.concatenate([wr, -wi], axis=-1)
    bb = jnp.concatenate([bbr, bbi], axis=1)
    kern = jnp.einsum("gxp,gpi->gxi", w[:, :r].reshape(SSM_GROUPS, S5_LANES, 2 * SSM_STATE), bb,
                      precision=lax.Precision.HIGHEST).reshape(SSM_GROUPS, r, SSM_GROUP, SSM_GROUP)
    cpt = w[:, 1:].reshape(SSM_GROUPS, S5_LANES, 2 * SSM_STATE)
    prs = jnp.swapaxes(pr[:, r - 1::-1][:, :r], 1, 2)[..., None]
    pis = jnp.swapaxes(pi[:, r - 1::-1][:, :r], 1, 2)[..., None]
    bp_re = prs * bbr[:, :, None, :] - pis * bbi[:, :, None, :]
    bp_im = prs * bbi[:, :, None, :] + pis * bbr[:, :, None, :]
    bpt = jnp.concatenate([bp_re, bp_im], axis=1).reshape(SSM_GROUPS, 2 * SSM_STATE, S5_LANES)
    ar = pr[:, r].reshape(1, SSM_GROUPS * SSM_STATE)
    ai = pi[:, r].reshape(1, SSM_GROUPS * SSM_STATE)
    return kern, cpt, bpt, ar, ai


def _s5_toeplitz(kern):
    r = S5_BLOCK
    cols = [jnp.pad(kern[:, :r - s], ((0, 0), (s, 0), (0, 0), (0, 0))) for s in range(r)]
    return jnp.stack(cols, axis=3).reshape(SSM_GROUPS, S5_LANES, S5_LANES)


S5_OCTET = 128 // SSM_GROUP
S5_STEPS = SSM_GROUPS // S5_OCTET


def _oct_spec(t):
    return pl.BlockSpec((t, 128), lambda j: (0, j))


def _state_spec(nb):
    return pl.BlockSpec((nb, S5_OCTET * SSM_STATE), lambda j: (0, j))


def _gmat_spec(a, b):
    return pl.BlockSpec((S5_OCTET, a, b), lambda j: (j, 0, 0))


def _block_rows(ref, nb):
    return [ref[pl.ds(r, nb, stride=S5_BLOCK), :] for r in range(S5_BLOCK)]


def _group_cols(pieces_t, g):
    return jnp.concatenate([p[SSM_GROUP * g:SSM_GROUP * (g + 1)] for p in pieces_t], axis=0)


def _state_cols(re_t, im_t, g):
    return jnp.concatenate([re_t[SSM_STATE * g:SSM_STATE * (g + 1)], im_t[SSM_STATE * g:SSM_STATE * (g + 1)]], axis=0)


def _s5_project(a, mat, name):
    t = a.shape[0]
    nb = t // S5_BLOCK

    def body(a_ref, m_ref, re_ref, im_ref):
        at = [p.T for p in _block_rows(a_ref, nb)]
        for pair in range(S5_OCTET // 2):
            xs = [_dot(m_ref[2 * pair + k], _group_cols(at, 2 * pair + k).astype(BF16)) for k in (0, 1)]
            lanes = slice(128 * pair, 128 * (pair + 1))
            re_ref[:, lanes] = jnp.concatenate([xs[0][:SSM_STATE], xs[1][:SSM_STATE]], axis=0).T
            im_ref[:, lanes] = jnp.concatenate([xs[0][SSM_STATE:], xs[1][SSM_STATE:]], axis=0).T

    return pl.pallas_call(
        body, name=name, grid=(S5_STEPS,),
        in_specs=[_oct_spec(t), _gmat_spec(2 * SSM_STATE, S5_LANES)],
        out_specs=[_state_spec(nb), _state_spec(nb)],
        out_shape=[jax.ShapeDtypeStruct((nb, SSM_GROUPS * SSM_STATE), F32)] * 2,
        compiler_params=_params(1),
    )(a, mat)


_SCAN_LANES = 1024


def _s5_scan_fwd(xre, xim, ar, ai, name):
    nb = xre.shape[0]
    col = pl.BlockSpec((nb, _SCAN_LANES), lambda j: (0, j))
    par = pl.BlockSpec((1, _SCAN_LANES), lambda j: (0, j))

    def body(xre_ref, xim_ref, ar_ref, ai_ref, hre_ref, him_ref):
        a_r, a_i = ar_ref[...], ai_ref[...]

        def step(b, carry):
            hr, hi = carry
            hre_ref[pl.ds(b, 1), :] = hr
            him_ref[pl.ds(b, 1), :] = hi
            xr, xi = xre_ref[pl.ds(b, 1), :], xim_ref[pl.ds(b, 1), :]
            return a_r * hr - a_i * hi + xr, a_r * hi + a_i * hr + xi

        zero = jnp.zeros((1, _SCAN_LANES), F32)
        lax.fori_loop(0, nb, step, (zero, zero))

    return pl.pallas_call(
        body, name=name, grid=(xre.shape[1] // _SCAN_LANES,),
        in_specs=[col, col, par, par], out_specs=[col, col],
        out_shape=[jax.ShapeDtypeStruct(xre.shape, F32)] * 2,
        compiler_params=_params(1),
    )(xre, xim, ar, ai)


def _s5_scan_bwd(dhre, dhim, hre, him, ar, ai, name):
    nb = dhre.shape[0]
    col = pl.BlockSpec((nb, _SCAN_LANES), lambda j: (0, j))
    par = pl.BlockSpec((1, _SCAN_LANES), lambda j: (0, j))

    def body(dhre_ref, dhim_ref, hre_ref, him_ref, ar_ref, ai_ref, dxre_ref, dxim_ref, dar_ref, dai_ref):
        a_r, a_i = ar_ref[...], ai_ref[...]

        def step(s, carry):
            gr, gi, dar, dai = carry
            b = nb - 1 - s
            dxre_ref[pl.ds(b, 1), :] = gr
            dxim_ref[pl.ds(b, 1), :] = gi
            hr, hi = hre_ref[pl.ds(b, 1), :], him_ref[pl.ds(b, 1), :]
            dar = dar + (hr * gr + hi * gi)
            dai = dai + (hr * gi - hi * gr)
            dr, di = dhre_ref[pl.ds(b, 1), :], dhim_ref[pl.ds(b, 1), :]
            return dr + (a_r * gr + a_i * gi), di + (a_r * gi - a_i * gr), dar, dai

        zero = jnp.zeros((1, _SCAN_LANES), F32)
        _, _, dar, dai = lax.fori_loop(0, nb, step, (zero, zero, zero, zero))
        dar_ref[...] = dar
        dai_ref[...] = dai

    return pl.pallas_call(
        body, name=name, grid=(dhre.shape[1] // _SCAN_LANES,),
        in_specs=[col, col, col, col, par, par], out_specs=[col, col, par, par],
        out_shape=[jax.ShapeDtypeStruct(dhre.shape, F32)] * 2 + [jax.ShapeDtypeStruct(ar.shape, F32)] * 2,
        compiler_params=_params(1),
    )(dhre, dhim, hre, him, ar, ai)


def _s5_outputs(u, hre, him, tm, cpt, d, name):
    t = u.shape[0]
    nb = t // S5_BLOCK

    def body(u_ref, hre_ref, him_ref, tm_ref, cpt_ref, d_ref, y_ref):
        u_rows = _block_rows(u_ref, nb)
        ut = [p.T for p in u_rows]
        hre_t, him_t = hre_ref[...].T, him_ref[...].T
        yts = []
        for g in range(S5_OCTET):
            yts.append(_dot(tm_ref[g], _group_cols(ut, g).astype(BF16))
                       + _dot(cpt_ref[g], _state_cols(hre_t, him_t, g).astype(BF16)))
        for r in range(S5_BLOCK):
            rows = jnp.concatenate([yt[SSM_GROUP * r:SSM_GROUP * (r + 1)] for yt in yts], axis=0)
            y_ref[pl.ds(r, nb, stride=S5_BLOCK), :] = rows.T + d_ref[...] * u_rows[r]

    return pl.pallas_call(
        body, name=name, grid=(S5_STEPS,),
        in_specs=[_oct_spec(t), _state_spec(nb), _state_spec(nb), _gmat_spec(S5_LANES, S5_LANES),
                  _gmat_spec(S5_LANES, 2 * SSM_STATE), _oct_spec(1)],
        out_specs=_oct_spec(t),
        out_shape=jax.ShapeDtypeStruct(u.shape, F32),
        compiler_params=_params(1),
    )(u, hre, him, tm, cpt, d.reshape(1, D_MODEL))


def _s5_backward(dy, u, hre, him, dxre, dxim, tmt, bp, d, name, exchange=None):
    t = u.shape[0]
    nb = t // S5_BLOCK

    def body(dy_ref, u_ref, hre_ref, him_ref, dxre_ref, dxim_ref, tmt_ref, bp_ref, d_ref,
             du_ref, dk_ref, dcpt_ref, dbpt_ref, dd_ref, dtm_scratch):
        dy_rows, u_rows = _block_rows(dy_ref, nb), _block_rows(u_ref, nb)
        dyt, ut = [p.T for p in dy_rows], [p.T for p in u_rows]
        hre_t, him_t = hre_ref[...].T, him_ref[...].T
        dxre_t, dxim_t = dxre_ref[...].T, dxim_ref[...].T
        duts = []
        for g in range(S5_OCTET):
            dyg, ug = _group_cols(dyt, g).astype(BF16), _group_cols(ut, g).astype(BF16)
            hg = _state_cols(hre_t, him_t, g).astype(BF16)
            dxg = _state_cols(dxre_t, dxim_t, g).astype(BF16)
            duts.append(_dot(tmt_ref[g], dyg) + _dot(bp_ref[g], dxg))
            dtm_scratch[...] = _dot_nt(dyg, ug)
            dk = dtm_scratch[:, :SSM_GROUP]
            for s in range(1, S5_BLOCK):
                below = dtm_scratch[SSM_GROUP * s:, SSM_GROUP * s:SSM_GROUP * (s + 1)]
                dk = dk + jnp.concatenate([below, jnp.zeros((SSM_GROUP * s, SSM_GROUP), F32)], axis=0)
            dk_ref[g] = dk
            dcpt_ref[g] = _dot_nt(dyg, hg)
            dbpt_ref[g] = _dot_nt(dxg, ug)
        dd = jnp.zeros((1, 128), F32)
        for r in range(S5_BLOCK):
            rows = jnp.concatenate([dut[SSM_GROUP * r:SSM_GROUP * (r + 1)] for dut in duts], axis=0)
            du_ref[pl.ds(r, nb, stride=S5_BLOCK), :] = rows.T + d_ref[...] * dy_rows[r]
            dd = dd + jnp.sum(dy_rows[r] * u_rows[r], axis=0, keepdims=True)
        dd_ref[...] = dd

    return _call_hosting(
        exchange, body, (dy, u, hre, him, dxre, dxim, tmt, bp, d.reshape(1, D_MODEL)), name=name, steps=S5_STEPS,
        in_specs=[_oct_spec(t), _oct_spec(t), _state_spec(nb), _state_spec(nb), _state_spec(nb), _state_spec(nb),
                  _gmat_spec(S5_LANES, S5_LANES), _gmat_spec(S5_LANES, 2 * SSM_STATE), _oct_spec(1)],
        out_specs=[_oct_spec(t), _gmat_spec(S5_LANES, SSM_GROUP), _gmat_spec(S5_LANES, 2 * SSM_STATE),
                   _gmat_spec(2 * SSM_STATE, S5_LANES), _oct_spec(1)],
        out_shape=[jax.ShapeDtypeStruct(u.shape, F32),
                   jax.ShapeDtypeStruct((SSM_GROUPS, S5_LANES, SSM_GROUP), F32),
                   jax.ShapeDtypeStruct((SSM_GROUPS, S5_LANES, 2 * SSM_STATE), F32),
                   jax.ShapeDtypeStruct((SSM_GROUPS, 2 * SSM_STATE, S5_LANES), F32),
                   jax.ShapeDtypeStruct((1, D_MODEL), F32)],
        scratch_shapes=[pltpu.VMEM((S5_LANES, S5_LANES), F32)])


def _ssm_out_fwd(y, gate, x, w_glu, b_glu, w_out, name, exchange=None):
    t = x.shape[0]
    rows = ROWS_FWD

    def body(y_ref, g_ref, x_ref, wg_ref, bg_ref, wo_ref, o_ref):
        z0 = _gelu(y_ref[...])
        s = _dot(z0.astype(BF16), wg_ref[...]) + bg_ref[...]
        gate_v = g_ref[...].astype(F32)
        a = (z0 * _sigmoid(s)) * (gate_v * _sigmoid(gate_v))
        o_ref[...] = x_ref[...] + _dot(a.astype(BF16), wo_ref[...])

    (xn,), got = _call_hosting(
        exchange, body, (y, gate, x, w_glu, b_glu.reshape(1, D_MODEL), w_out), name=name, steps=t // rows,
        in_specs=[_row_spec(rows, D_MODEL)] * 3 + [_const_spec((D_MODEL, D_MODEL)), _const_spec((1, D_MODEL)),
                                                   _const_spec((D_MODEL, D_MODEL))],
        out_specs=[_row_spec(rows, D_MODEL)],
        out_shape=[jax.ShapeDtypeStruct((t, D_MODEL), F32)])
    return xn, got


def _ssm_out_bwd(dxn, y, gate, w_glu, b_glu, w_out, name, exchange=None):
    t = y.shape[0]
    rows = ROWS_BWD

    def body(dxn_ref, y_ref, g_ref, wg_ref, bg_ref, wo_ref, dy_ref, dg_ref, dwg_ref, dbg_ref, dwo_ref):
        @pl.when(pl.program_id(0) == 0)
        def _():
            dwo_ref[...] = jnp.zeros_like(dwo_ref)
            dwg_ref[...] = jnp.zeros_like(dwg_ref)
            dbg_ref[...] = jnp.zeros_like(dbg_ref)

        yv = y_ref[...]
        z0 = _gelu(yv)
        z0b = z0.astype(BF16)
        sg = _sigmoid(_dot(z0b, wg_ref[...]) + bg_ref[...])
        z = z0 * sg
        gate_v = g_ref[...].astype(F32)
        sgg = _sigmoid(gate_v)
        silu = gate_v * sgg
        dob = dxn_ref[...].astype(BF16)
        da = _dot_nt(dob, wo_ref[...])
        dwo_ref[...] += _dot_tn((z * silu).astype(BF16), dob)
        dz = da * silu
        dg_ref[...] = (da * z * (sgg * (1.0 + gate_v * (1.0 - sgg)))).astype(BF16)
        ds = dz * z0 * (sg * (1.0 - sg))
        dsb = ds.astype(BF16)
        dz0 = dz * sg + _dot_nt(dsb, wg_ref[...])
        dwg_ref[...] += _dot_tn(z0b, dsb)
        dbg_ref[...] += jnp.sum(ds, axis=0, keepdims=True)
        dy_ref[...] = dz0 * _gelu_grad(yv)

    sq = _const_spec((D_MODEL, D_MODEL))
    vec = _const_spec((1, D_MODEL))
    return _call_hosting(
        exchange, body, (dxn, y, gate, w_glu, b_glu.reshape(1, D_MODEL), w_out), name=name, steps=t // rows,
        in_specs=[_row_spec(rows, D_MODEL)] * 3 + [sq, vec, sq],
        out_specs=[_row_spec(rows, D_MODEL), _row_spec(rows, D_MODEL), sq, vec, sq],
        out_shape=[jax.ShapeDtypeStruct((t, D_MODEL), F32), jax.ShapeDtypeStruct((t, D_MODEL), BF16),
                   jax.ShapeDtypeStruct((D_MODEL, D_MODEL), F32), jax.ShapeDtypeStruct((1, D_MODEL), F32),
                   jax.ShapeDtypeStruct((D_MODEL, D_MODEL), F32)])


KV_LANES = GQA * ATTN_BLOCK


def _attn_bias(block_is_first):
    kj = lax.broadcasted_iota(jnp.int32, (2 * ATTN_BLOCK, ATTN_BLOCK), 0)
    qi = lax.broadcasted_iota(jnp.int32, (2 * ATTN_BLOCK, ATTN_BLOCK), 1)
    dist = qi + ATTN_BLOCK - kj
    valid = (dist >= 0) & (dist < ATTN_BLOCK) & (jnp.logical_not(block_is_first) | (kj >= ATTN_BLOCK))
    return jnp.tile(jnp.where(valid, 0.0, NEG_INF).astype(F32), (1, GQA))


def _head_cols(a_t, kvh):
    heads = range(kvh * GQA, (kvh + 1) * GQA)
    return jnp.concatenate([a_t[HEAD_DIM * h:HEAD_DIM * (h + 1)] for h in heads], axis=1)


def _head_rows(a_cols):
    stacked = jnp.concatenate([a_cols[:, ATTN_BLOCK * g:ATTN_BLOCK * (g + 1)] for g in range(GQA)], axis=0)
    return stacked.T


def _kv_rows(prev_ref, cur_ref, kvh):
    lanes = slice(HEAD_DIM * kvh, HEAD_DIM * (kvh + 1))
    return jnp.concatenate([prev_ref[:, lanes], cur_ref[:, lanes]], axis=0).astype(BF16)


def _kv_cols(prev_t, cur_t, kvh):
    rows = slice(HEAD_DIM * kvh, HEAD_DIM * (kvh + 1))
    return jnp.concatenate([prev_t[rows], cur_t[rows]], axis=1).astype(BF16)


def _attn_probs(kk, q_cols, sink_row, bias):
    s = _dot(kk, q_cols) * ATTN_SCALE + bias
    m = jnp.maximum(jnp.max(s, axis=0, keepdims=True), sink_row)
    p = jnp.exp(s - m)
    e_sink = jnp.exp(sink_row - m)
    inv = 1.0 / (jnp.sum(p, axis=0, keepdims=True) + e_sink)
    return p * inv, e_sink * inv


def _sink_cols(sinks):
    return jnp.repeat(sinks, ATTN_BLOCK).reshape(N_KV_HEADS, 1, KV_LANES)


def _attn_fwd(q, k, v, sinks, name, exchange=None):
    t = q.shape[0]
    nblk = t // ATTN_BLOCK

    def body(s_ref, q_ref, kc_ref, kp_ref, vc_ref, vp_ref, o_ref):
        bias = _attn_bias(pl.program_id(0) == 0)
        q_t = q_ref[...].astype(F32).T
        vp_t, vc_t = vp_ref[...].astype(F32).T, vc_ref[...].astype(F32).T
        for kvh in range(N_KV_HEADS):
            p, _ = _attn_probs(_kv_rows(kp_ref, kc_ref, kvh), _head_cols(q_t, kvh).astype(BF16), s_ref[kvh], bias)
            o_cols = _dot(_kv_cols(vp_t, vc_t, kvh), p.astype(BF16))
            o_ref[:, GQA * HEAD_DIM * kvh:GQA * HEAD_DIM * (kvh + 1)] = _head_rows(o_cols)

    cur = lambda i: (i, 0)
    prev = lambda i: (jnp.maximum(i - 1, 0), 0)
    (o,), got = _call_hosting(
        exchange, body, (_sink_cols(sinks), q, k, k, v, v), name=name, steps=nblk,
        in_specs=[_const_spec((N_KV_HEADS, 1, KV_LANES)),
                  pl.BlockSpec((ATTN_BLOCK, Q_DIM), cur),
                  pl.BlockSpec((ATTN_BLOCK, KV_DIM), cur), pl.BlockSpec((ATTN_BLOCK, KV_DIM), prev),
                  pl.BlockSpec((ATTN_BLOCK, KV_DIM), cur), pl.BlockSpec((ATTN_BLOCK, KV_DIM), prev)],
        out_specs=[pl.BlockSpec((ATTN_BLOCK, Q_DIM), cur)],
        out_shape=[jax.ShapeDtypeStruct((t, Q_DIM), F32)])
    return o, got


def _attn_bwd(q, k, v, o, do, sinks, rope, name, exchange=None):
    t = q.shape[0]
    nblk = t // ATTN_BLOCK

    def body(s_ref, q_ref, o_ref, do_ref, kp_ref, kc_ref, vp_ref, vc_ref, cosq_ref, sinq_ref, cosk_ref, sinkey_ref,
             dq_ref, dk_ref, dv_ref, ds_ref, new_k, new_v, wait_k, wait_v, dq_rot):
        n = pl.program_id(0)

        @pl.when(n == 0)
        def _():
            ds_ref[...] = jnp.zeros_like(ds_ref)
            wait_k[...] = jnp.zeros_like(wait_k)
            wait_v[...] = jnp.zeros_like(wait_v)

        @pl.when(n < nblk)
        def _():
            bias = _attn_bias(n == 0)
            q_t, o_t, do_t = q_ref[...].astype(F32).T, o_ref[...].T, do_ref[...].T
            kp_t, kc_t = kp_ref[...].astype(F32).T, kc_ref[...].astype(F32).T
            for kvh in range(N_KV_HEADS):
                q_cols = _head_cols(q_t, kvh).astype(BF16)
                do_cols = _head_cols(do_t, kvh)
                delta = jnp.sum(do_cols * _head_cols(o_t, kvh), axis=0, keepdims=True)
                do_cols = do_cols.astype(BF16)
                p, p_sink = _attn_probs(_kv_rows(kp_ref, kc_ref, kvh), q_cols, s_ref[kvh], bias)
                dp = _dot(_kv_rows(vp_ref, vc_ref, kvh), do_cols)
                ds = (p * (dp - delta) * ATTN_SCALE).astype(BF16)
                lanes = slice(GQA * HEAD_DIM * kvh, GQA * HEAD_DIM * (kvh + 1))
                dq_rot[:, lanes] = _head_rows(_dot(_kv_cols(kp_t, kc_t, kvh), ds))
                head = slice(HEAD_DIM * kvh, HEAD_DIM * (kvh + 1))
                new_k[:, head] = _dot_nt(ds, q_cols)
                new_v[:, head] = _dot_nt(p.astype(BF16), do_cols)
                ds_ref[kvh] += -(p_sink * delta)
            dq_ref[...] = _rope_apply(dq_rot[...], cosq_ref[...], sinq_ref[...], -1.0).astype(BF16)

        @pl.when(n == nblk)
        def _():
            new_k[...] = jnp.zeros_like(new_k)
            new_v[...] = jnp.zeros_like(new_v)

        dk_ref[...] = _rope_apply(wait_k[...] + new_k[:ATTN_BLOCK], cosk_ref[...], sinkey_ref[...], -1.0).astype(BF16)
        dv_ref[...] = (wait_v[...] + new_v[:ATTN_BLOCK]).astype(BF16)
        wait_k[...] = new_k[ATTN_BLOCK:]
        wait_v[...] = new_v[ATTN_BLOCK:]

    cur = lambda i: (jnp.minimum(i, nblk - 1), 0)
    prev = lambda i: (jnp.maximum(i - 1, 0), 0)
    qs = lambda f: pl.BlockSpec((ATTN_BLOCK, Q_DIM), f)
    ks = lambda f: pl.BlockSpec((ATTN_BLOCK, KV_DIM), f)
    sink_spec = _const_spec((N_KV_HEADS, 1, KV_LANES))
    return _call_hosting(
        exchange, body, (_sink_cols(sinks), q, o, do, k, k, v, v, rope[0], rope[1], rope[0], rope[1]),
        name=name, steps=nblk + 1,
        in_specs=[sink_spec, qs(cur), qs(cur), qs(cur), ks(prev), ks(cur), ks(prev), ks(cur),
                  ks(cur), ks(cur), ks(prev), ks(prev)],
        out_specs=[qs(cur), ks(prev), ks(prev), sink_spec],
        out_shape=[jax.ShapeDtypeStruct((t, Q_DIM), BF16), jax.ShapeDtypeStruct((t, KV_DIM), BF16),
                   jax.ShapeDtypeStruct((t, KV_DIM), BF16), jax.ShapeDtypeStruct((N_KV_HEADS, 1, KV_LANES), F32)],
        scratch_shapes=[pltpu.VMEM((2 * ATTN_BLOCK, KV_DIM), F32), pltpu.VMEM((2 * ATTN_BLOCK, KV_DIM), F32),
                        pltpu.VMEM((ATTN_BLOCK, KV_DIM), F32), pltpu.VMEM((ATTN_BLOCK, KV_DIM), F32),
                        pltpu.VMEM((ATTN_BLOCK, Q_DIM), F32)])


def _attn_out_fwd(o, gate, x, w_out, name, exchange=None):
    t = x.shape[0]
    rows = ROWS_FWD

    def body(o_ref, g_ref, x_ref, wo_ref, xn_ref):
        gate_v = g_ref[...].astype(F32)
        a = o_ref[...] * (gate_v * _sigmoid(gate_v))
        xn_ref[...] = x_ref[...] + _dot(a.astype(BF16), wo_ref[...])

    (xn,), got = _call_hosting(
        exchange, body, (o, gate, x, w_out), name=name, steps=t // rows,
        in_specs=[_row_spec(rows, D_MODEL)] * 3 + [_const_spec((D_MODEL, D_MODEL))],
        out_specs=[_row_spec(rows, D_MODEL)],
        out_shape=[jax.ShapeDtypeStruct((t, D_MODEL), F32)])
    return xn, got


def _attn_out_bwd(dxn, o, gate, w_out, name, exchange=None):
    t = o.shape[0]
    rows = ROWS_BWD

    def body(dxn_ref, o_ref, g_ref, wo_ref, do_ref, dg_ref, dwo_ref):
        @pl.when(pl.program_id(0) == 0)
        def _():
            dwo_ref[...] = jnp.zeros_like(dwo_ref)

        gate_v, ov = g_ref[...].astype(F32), o_ref[...]
        sgg = _sigmoid(gate_v)
        silu = gate_v * sgg
        dob = dxn_ref[...].astype(BF16)
        da = _dot_nt(dob, wo_ref[...])
        dwo_ref[...] += _dot_tn((ov * silu).astype(BF16), dob)
        do_ref[...] = da * silu
        dg_ref[...] = (da * ov * (sgg * (1.0 + gate_v * (1.0 - sgg)))).astype(BF16)

    sq = _const_spec((D_MODEL, D_MODEL))
    return _call_hosting(
        exchange, body, (dxn, o, gate, w_out), name=name, steps=t // rows,
        in_specs=[_row_spec(rows, D_MODEL)] * 3 + [sq],
        out_specs=[_row_spec(rows, D_MODEL), _row_spec(rows, D_MODEL), sq],
        out_shape=[jax.ShapeDtypeStruct((t, D_MODEL), F32), jax.ShapeDtypeStruct((t, D_MODEL), BF16),
                   jax.ShapeDtypeStruct((D_MODEL, D_MODEL), F32)])


def _loss_head(x, norm, target, name):
    t = x.shape[0]
    rows = ROWS_FWD

    def body(x_ref, n_ref, t_ref, loss_ref, dx_ref, dn_ref):
        i = pl.program_id(0)
        xv = x_ref[...]
        rstd = lax.rsqrt(jnp.mean(xv * xv, axis=-1, keepdims=True) + NORM_EPS)
        xhat = xv * rstd
        err = xhat * n_ref[...] - t_ref[...]
        part = 0.5 * jnp.sum(jnp.mean(err * err, axis=-1, keepdims=True), axis=0, keepdims=True)
        dy = err * (1.0 / D_MODEL)
        dn = jnp.sum(dy * xhat, axis=0, keepdims=True)
        dxhat = dy * n_ref[...]
        dx_ref[...] = rstd * (dxhat - xhat * jnp.mean(dxhat * xhat, axis=-1, keepdims=True))

        @pl.when(i == 0)
        def _():
            loss_ref[...] = jnp.zeros((8, 128), F32) + part
            dn_ref[...] = dn

        @pl.when(i > 0)
        def _():
            loss_ref[...] += part
            dn_ref[...] += dn

    return pl.pallas_call(
        body, name=name, grid=(t // rows,),
        in_specs=[_row_spec(rows, D_MODEL), _const_spec((1, D_MODEL)), _row_spec(rows, D_MODEL)],
        out_specs=[_const_spec((8, 128)), _row_spec(rows, D_MODEL), _const_spec((1, D_MODEL))],
        out_shape=[jax.ShapeDtypeStruct((8, 128), F32), jax.ShapeDtypeStruct((t, D_MODEL), F32),
                   jax.ShapeDtypeStruct((1, D_MODEL), F32)],
        compiler_params=_params(1),
    )(x, norm.reshape(1, D_MODEL), target)


N_CHIPS = 4
N_CORES = 2
CHIP_FLIPS = ((0, 1), (1, 0), (1, 1))
ICI_CHUNKS = 2
D2D_CHUNKS = 8


def _n_chunks(rows, dtype, most):
    unit = 16 if dtype == BF16 else 8
    return max(n for n in range(1, most + 1) if rows % n == 0 and (rows // n) % unit == 0)


def _chunks_of(arrays, most):
    out = []
    for a in arrays:
        n = _n_chunks(a.shape[-2], a.dtype, most)
        out.append((n, a.shape[-2] // n))
    return out


class _Exchange:
    def __init__(self, arrays, out_shape, scratch, copies):
        self.arrays, self.out_shape, self.scratch, self._copies = arrays, out_shape, scratch, copies

    def start(self, *refs):
        for cp in self._copies(*refs)[0]:
            cp.start()

    def wait(self, *refs):
        for wait in self._copies(*refs)[1]:
            wait()


def _chips_exchange(sends, per_dest):
    n = len(sends)
    chunking = _chunks_of(sends, ICI_CHUNKS)

    def copies(send_refs, recv_refs, sems):
        x, y, c = lax.axis_index("x"), lax.axis_index("y"), lax.axis_index("c")
        me = 2 * x + y

        def peer(k):
            fx, fy = CHIP_FLIPS[k]
            px, py = x + fx - 2 * x * fx, y + fy - 2 * y * fy
            return (px, py, c), 2 * px + py

        to_start, waits = [], []
        for a in range(n):
            send_sems, recv_sems, local_sems = sems[3 * a:3 * a + 3]
            chunks, chunk_rows = chunking[a]
            for j in range(chunks):
                part = pl.ds(j * chunk_rows, chunk_rows)
                src = lambda number: send_refs[a].at[number, part] if per_dest else send_refs[a].at[part]
                for k in range(len(CHIP_FLIPS)):
                    to, to_number = peer(k)
                    remote = lambda landing: pltpu.make_async_remote_copy(
                        src_ref=src(to_number), dst_ref=recv_refs[a].at[landing, part],
                        send_sem=send_sems.at[k, j], recv_sem=recv_sems.at[k, j],
                        device_id=to, device_id_type=pl.DeviceIdType.MESH)
                    to_start.append(remote(me))
                    waits += [remote(me).wait_send, remote(to_number).wait_recv]
                own = pltpu.make_async_copy(src(me), recv_refs[a].at[me, part], local_sems.at[j])
                to_start.append(own)
                waits.append(own.wait)
        return to_start, waits

    scratch = []
    for chunks, _ in chunking:
        scratch += [pltpu.SemaphoreType.DMA((len(CHIP_FLIPS), chunks)), pltpu.SemaphoreType.DMA((len(CHIP_FLIPS), chunks)),
                    pltpu.SemaphoreType.DMA((chunks,))]
    return _Exchange(sends, [jax.ShapeDtypeStruct((N_CHIPS,) + a.shape[-2:], a.dtype) for a in sends], scratch, copies)


def _cores_exchange(sends, per_dest):
    n = len(sends)
    chunking = _chunks_of(sends, D2D_CHUNKS)

    def copies(send_refs, got_refs, sems):
        c = lax.axis_index("c")
        sibling = (lax.axis_index("x"), lax.axis_index("y"), 1 - c)
        to_start = []
        for a in range(n):
            chunks, chunk_rows = chunking[a]
            for j in range(chunks):
                part = pl.ds(j * chunk_rows, chunk_rows)
                to_start.append(pltpu.make_async_remote_copy(
                    src_ref=send_refs[a].at[1 - c, part] if per_dest else send_refs[a].at[part],
                    dst_ref=got_refs[a].at[part], send_sem=sems[2 * a].at[j], recv_sem=sems[2 * a + 1].at[j],
                    device_id=sibling, device_id_type=pl.DeviceIdType.MESH))
        return to_start, [cp.wait for cp in to_start]

    scratch = []
    for chunks, _ in chunking:
        scratch += [pltpu.SemaphoreType.DMA((chunks,)), pltpu.SemaphoreType.DMA((chunks,))]
    return _Exchange(sends, [jax.ShapeDtypeStruct(a.shape[-2:], a.dtype) for a in sends], scratch, copies)


def _run_exchange(exchange, name):
    n = len(exchange.arrays)

    def body(*refs):
        parts = refs[:n], refs[n:2 * n], refs[2 * n:]
        exchange.start(*parts)
        exchange.wait(*parts)

    hbm = pl.BlockSpec(memory_space=pltpu.HBM)
    return pl.pallas_call(body, name=name, in_specs=[hbm] * n, out_specs=[hbm] * n, out_shape=exchange.out_shape,
                          scratch_shapes=exchange.scratch)(*exchange.arrays)


def _call_hosting(exchange, body, args, *, name, steps, in_specs, out_specs, out_shape, scratch_shapes=()):
    common = dict(name=name, grid=(steps,), compiler_params=_params(1))
    if exchange is None:
        return pl.pallas_call(body, in_specs=in_specs, out_specs=out_specs, out_shape=out_shape,
                              scratch_shapes=list(scratch_shapes), **common)(*args), None
    n_in, n_out, n_scratch, k = len(in_specs), len(out_specs), len(scratch_shapes), len(exchange.arrays)

    def hosting(*refs):
        ins, sends = refs[:n_in], refs[n_in:n_in + k]
        outs, recvs = refs[n_in + k:n_in + k + n_out], refs[n_in + k + n_out:n_in + 2 * k + n_out]
        scratch = refs[n_in + 2 * k + n_out:n_in + 2 * k + n_out + n_scratch]
        sems = refs[n_in + 2 * k + n_out + n_scratch:]
        pl.when(pl.program_id(0) == 0)(lambda: exchange.start(sends, recvs, sems))
        body(*ins, *outs, *scratch)
        pl.when(pl.program_id(0) == steps - 1)(lambda: exchange.wait(sends, recvs, sems))

    hbm = pl.BlockSpec(memory_space=pltpu.HBM)
    results = pl.pallas_call(
        hosting, in_specs=list(in_specs) + [hbm] * k, out_specs=list(out_specs) + [hbm] * k,
        out_shape=list(out_shape) + exchange.out_shape, scratch_shapes=list(scratch_shapes) + exchange.scratch, **common,
    )(*args, *exchange.arrays)
    return results[:n_out], results[n_out:]


def _exchange_chips(sends, per_dest, name):
    return _run_exchange(_chips_exchange(sends, per_dest), name)


def _swap_cores(sends, per_dest, name):
    return _run_exchange(_cores_exchange(sends, per_dest), name)


def _all_gather(arrays, name):
    by_chip = _exchange_chips(arrays, False, name + "_chips")
    others = _swap_cores([r.reshape(-1, r.shape[-1]) for r in by_chip], False, name + "_cores")
    return [(m, o.reshape(m.shape)) for m, o in zip(by_chip, others)]


def _in_device_order(mine, other, axis):
    first = lax.axis_index("c") == 0
    pieces = []
    for m, o in zip(mine, other):
        pieces += [jnp.where(first, m, o), jnp.where(first, o, m)]
    return jnp.concatenate(pieces, axis=axis)


def _sum_core(send, got, out_dtype, name):
    _, n, cols = send.shape
    rows = min(n, 256)
    while n % rows:
        rows -= 16

    def body(c_ref, keep_ref, got_ref, o_ref):
        o_ref[...] = (keep_ref[...].astype(F32) + got_ref[...].astype(F32)).astype(out_dtype)

    return pl.pallas_call(
        body, name=name, out_shape=jax.ShapeDtypeStruct((n, cols), out_dtype),
        grid_spec=pltpu.PrefetchScalarGridSpec(
            num_scalar_prefetch=1, grid=(n // rows,),
            in_specs=[pl.BlockSpec((None, rows, cols), lambda i, c: (c[0], i, 0)),
                      pl.BlockSpec((rows, cols), lambda i, c: (i, 0))],
            out_specs=pl.BlockSpec((rows, cols), lambda i, c: (i, 0))),
        compiler_params=_params(1),
    )(lax.axis_index("c").astype(jnp.int32).reshape(1), send, got)


def _sum_parts(parts, out_dtype, name):
    n, cols = parts[0].shape
    rows = min(n, 256)
    while n % rows:
        rows -= 16

    def body(*refs):
        acc = refs[0][...].astype(F32)
        for ref in refs[1:-1]:
            acc = acc + ref[...].astype(F32)
        refs[-1][...] = acc.astype(out_dtype)

    return pl.pallas_call(
        body, name=name, grid=(n // rows,),
        in_specs=[_row_spec(rows, cols)] * len(parts),
        out_specs=_row_spec(rows, cols),
        out_shape=jax.ShapeDtypeStruct((n, cols), out_dtype),
        compiler_params=_params(1),
    )(*parts)


def _reduce_scatter(sends, wire_dtypes, name):
    halves = [s.reshape(N_CORES, N_CHIPS * s.shape[2], s.shape[3]) for s in sends]
    gots = _swap_cores(halves, True, name + "_cores")
    sums = [_sum_core(h, g, dt, "%s_core_sum%d" % (name, i)).reshape((N_CHIPS,) + s.shape[2:])
            for i, (h, g, dt, s) in enumerate(zip(halves, gots, wire_dtypes, sends))]
    return _exchange_chips(sums, True, name + "_chips")


def _adamw(parts, w, m, v, name):
    n, cols = w.shape
    k = parts.shape[0]
    rows = min(n, 256)
    while n % rows:
        rows -= 8
    c1 = 1.0 - ADAM_B1 ** ADAM_STEP
    c2 = 1.0 - ADAM_B2 ** ADAM_STEP

    def body(p_ref, w_ref, m_ref, v_ref, g_ref, d_ref, nm_ref, nv_ref):
        g = p_ref[0].astype(F32)
        for s in range(1, k):
            g = g + p_ref[s].astype(F32)
        nm = ADAM_B1 * m_ref[...] + (1.0 - ADAM_B1) * g
        nv = ADAM_B2 * v_ref[...] + (1.0 - ADAM_B2) * (g * g)
        g_ref[...] = g
        nm_ref[...] = nm
        nv_ref[...] = nv
        d_ref[...] = -ADAM_LR * ((nm / c1) / (jnp.sqrt(nv / c2) + ADAM_EPS) + ADAM_WD * w_ref[...])

    blk = _row_spec(rows, cols)
    return pl.pallas_call(
        body, name=name, grid=(n // rows,),
        in_specs=[pl.BlockSpec((k, rows, cols), lambda i: (0, i, 0)), blk, blk, blk],
        out_specs=[blk] * 4,
        out_shape=[jax.ShapeDtypeStruct((n, cols), F32)] * 4,
        compiler_params=_params(1),
    )(parts, w, m, v)


SSM_KEYS = ("norm", "w_in", "a_re", "a_im", "log_step", "b_re", "b_im", "c_re", "c_im", "d", "w_glu", "b_glu", "w_out")
ATTN_KEYS = ("norm", "w_in", "sinks", "w_out")
LAYER_KEYS = (SSM_KEYS, ATTN_KEYS, SSM_KEYS, ATTN_KEYS)
BIG_KEYS = ("w_in", "w_glu", "w_out")
ATTN_SPLITS = (Q_DIM, KV_DIM, KV_DIM, D_MODEL)


def _rope_tables(t):
    pos = jnp.arange(t, dtype=F32)
    inv_freq = ROPE_THETA ** (-jnp.arange(0, HEAD_DIM, 2, dtype=F32) / HEAD_DIM)
    ang = pos[:, None] * inv_freq[None, :]
    cos, sin = jnp.cos(ang), jnp.sin(ang)
    return jnp.tile(jnp.concatenate([cos, cos], axis=1), (1, 2)), jnp.tile(jnp.concatenate([-sin, sin], axis=1), (1, 2))


def _gather_ici_stage(gather):
    return None if gather is None else _chips_exchange(gather, False)


def _gather_d2d_stage(by_chip):
    return None if by_chip is None else _cores_exchange([r.reshape(-1, r.shape[-1]) for r in by_chip], False)


def _gathered(by_chip, others):
    return None if by_chip is None else [(m, other.reshape(m.shape)) for m, other in zip(by_chip, others)]


def _scatter_d2d_stage(scatter):
    if scatter is None:
        return None, None
    halves = [s.reshape(N_CORES, N_CHIPS * s.shape[2], s.shape[3]) for s in scatter[0]]
    return halves, _cores_exchange(halves, True)


def _scatter_ici_stage(scatter, halves, gots, tag):
    if scatter is None:
        return None
    sums = [_sum_core(h, g, dt, "%sscatter_core_sum%d" % (tag, j)).reshape((N_CHIPS,) + s.shape[2:])
            for j, (h, g, dt, s) in enumerate(zip(halves, gots, scatter[1], scatter[0]))]
    return _chips_exchange(sums, True)


def _ssm_layer_fwd(i, x, p, w, gather=None):
    tag = "l%d_" % i
    mats, mats_vjp = jax.vjp(_s5_matrices, p["a_re"], p["a_im"], p["log_step"], p["b_re"], p["b_im"], p["c_re"], p["c_im"])
    kern, cpt, bpt, ar, ai = mats
    tm = _s5_toeplitz(kern.astype(BF16))
    mb = dict(tm=tm, tmt=jnp.swapaxes(tm, 1, 2), cpt=cpt.astype(BF16),
              cp=jnp.swapaxes(cpt, 1, 2).astype(BF16), bpt=bpt.astype(BF16), bp=jnp.swapaxes(bpt, 1, 2).astype(BF16))
    (u, gate), by_chip = _inproj_fwd(x, p["norm"], w["w_in"], (D_MODEL, D_MODEL), (F32, BF16), None, tag + "inproj_fwd",
                                     _gather_ici_stage(gather))
    xre, xim = _s5_project(u, mb["bpt"], tag + "s5_block_inputs")
    hre, him = _s5_scan_fwd(xre, xim, ar, ai, tag + "s5_scan_fwd")
    y = _s5_outputs(u, hre, him, mb["tm"], mb["cpt"], p["d"], tag + "s5_outputs")
    xn, others = _ssm_out_fwd(y, gate, x, w["w_glu"], p["b_glu"], w["w_out"], tag + "out_fwd", _gather_d2d_stage(by_chip))
    return xn, (x, u, gate, y, hre, him, mb, ar, ai, mats_vjp), _gathered(by_chip, others)


def _ssm_layer_bwd(i, dxn, saved, p, w, scatter=None):
    tag = "l%d_" % i
    x, u, gate, y, hre, him, mb, ar, ai, mats_vjp = saved
    halves, d2d_stage = _scatter_d2d_stage(scatter)
    (dy, dgate, dw_glu, db_glu, dw_out), gots = _ssm_out_bwd(dxn, y, gate, w["w_glu"], p["b_glu"], w["w_out"], tag + "out_bwd",
                                                             d2d_stage)
    dhre, dhim = _s5_project(dy, mb["cp"], tag + "s5_state_grads")
    dxre, dxim, dar, dai = _s5_scan_bwd(dhre, dhim, hre, him, ar, ai, tag + "s5_scan_bwd")
    (du, dk, dcpt, dbpt, dd), parts = _s5_backward(dy, u, hre, him, dxre, dxim, mb["tmt"], mb["bp"], p["d"],
                                                   tag + "s5_backward", _scatter_ici_stage(scatter, halves, gots, tag))
    dk = dk.reshape(SSM_GROUPS, S5_BLOCK, SSM_GROUP, SSM_GROUP)
    da_re, da_im, dlog_step, db_re, db_im, dc_re, dc_im = mats_vjp((dk, dcpt, dbpt, dar, dai))
    dx, dw_in, dnorm = _inproj_bwd(x, p["norm"], w["w_in"], [du, dgate], dxn, tag + "inproj_bwd")
    grads = dict(norm=dnorm.reshape(D_MODEL), w_in=dw_in, a_re=da_re, a_im=da_im, log_step=dlog_step, b_re=db_re,
                 b_im=db_im, c_re=dc_re, c_im=dc_im, d=dd.reshape(D_MODEL), w_glu=dw_glu, b_glu=db_glu.reshape(D_MODEL),
                 w_out=dw_out)
    return dx, grads, parts


def _attn_layer_fwd(i, x, p, w, rope, gather=None):
    tag = "l%d_" % i
    (q, k, v, gate), _ = _inproj_fwd(x, p["norm"], w["w_in"], ATTN_SPLITS, (BF16, BF16, BF16, BF16), rope, tag + "inproj_fwd")
    o, by_chip = _attn_fwd(q, k, v, p["sinks"], tag + "attn_fwd", _gather_ici_stage(gather))
    xn, others = _attn_out_fwd(o, gate, x, w["w_out"], tag + "out_fwd", _gather_d2d_stage(by_chip))
    return xn, (x, q, k, v, gate, o), _gathered(by_chip, others)


def _attn_layer_bwd(i, dxn, saved, p, w, rope, scatter=None):
    tag = "l%d_" % i
    x, q, k, v, gate, o = saved
    halves, d2d_stage = _scatter_d2d_stage(scatter)
    (do, dgate, dw_out), gots = _attn_out_bwd(dxn, o, gate, w["w_out"], tag + "out_bwd", d2d_stage)
    (dq, dk, dv, dsinks), parts = _attn_bwd(q, k, v, o, do, p["sinks"], rope, tag + "attn_bwd",
                                            _scatter_ici_stage(scatter, halves, gots, tag))
    dx, dw_in, dnorm = _inproj_bwd(x, p["norm"], w["w_in"], [dq, dk, dv, dgate], dxn, tag + "inproj_bwd")
    grads = dict(norm=dnorm.reshape(D_MODEL), w_in=dw_in, sinks=dsinks.reshape(N_Q_HEADS, ATTN_BLOCK).sum(axis=1), w_out=dw_out)
    return dx, grads, parts


def _local_step(x, target, small, big, carried=()):
    rope = _rope_tables(x.shape[0])
    carried = {h: rest for h, *rest in carried}
    big = list(big)
    saved = []
    for i in range(4):
        gather = carried[i][0] if i in carried else None
        if i % 2 == 0:
            x, s, gathered = _ssm_layer_fwd(i, x, small[i], big[i], gather)
        else:
            x, s, gathered = _attn_layer_fwd(i, x, small[i], big[i], rope, gather)
        if gathered is not None:
            for layer, matrices in carried[i][1](gathered).items():
                big[layer] = matrices
        saved.append(s)
    loss, dx, dfinal = _loss_head(x, small[4]["norm"], target, "loss_head")
    grads = [None] * 4 + [dict(norm=dfinal.reshape(D_MODEL))]
    parts = {}
    for i in (3, 2, 1, 0):
        scatter = carried[i][2](grads) if i in carried else None
        if i % 2 == 0:
            dx, grads[i], parts[i] = _ssm_layer_bwd(i, dx, saved[i], small[i], big[i], scatter)
        else:
            dx, grads[i], parts[i] = _attn_layer_bwd(i, dx, saved[i], small[i], big[i], rope, scatter)
    return loss[0, 0], dx, grads, parts


def _owner_major(key, g):
    if key == "w_in":
        return g.reshape(D_MODEL, N_CHIPS, N_CORES, -1).transpose(2, 1, 0, 3)
    return g.reshape(N_CHIPS, N_CORES, -1, D_MODEL).transpose(1, 0, 2, 3)


def _from_gathered(key, mine, other):
    return _in_device_order(list(mine), list(other), 1 if key == "w_in" else 0)


SMALL_ROWS = 72


def _rows_of_small(a):
    flat = a.reshape(-1)
    return jnp.pad(flat, (0, -flat.shape[0] % D_MODEL)).reshape(-1, D_MODEL)


def _stack_small(arrays):
    rows = jnp.concatenate([_rows_of_small(a) for a in arrays], axis=0)
    assert rows.shape[0] <= N_DEV * SMALL_ROWS
    return jnp.pad(rows, ((0, N_DEV * SMALL_ROWS - rows.shape[0]), (0, 0)))


def kernel(*args):
    names = ["x"]
    layer_names = []
    for i, keys in enumerate(LAYER_KEYS):
        layer_names += ["l%d_%s" % (i, k) for k in keys]
    layer_names.append("final_norm")
    names += layer_names + ["loss_target"] + ["m_" + n for n in layer_names] + ["v_" + n for n in layer_names]
    given = dict(zip(names, args))
    big_names = [n for n in layer_names if n.split("_", 1)[1] in BIG_KEYS]
    small_names = [n for n in layer_names if n not in big_names]

    offsets = {}

    def families_of(layers):
        families = {}
        for n in big_names:
            if int(n[1]) in layers:
                family = families.setdefault(given[n].shape[1], [])
                offsets[n] = sum(given[other].shape[0] for other in family)
                family.append(n)
        return list(families.values())

    first, carried_by = families_of((0,)), {0: families_of((1,)), 1: families_of((2, 3))}
    stack = lambda pre, family: jnp.concatenate([given[pre + n] for n in family], axis=0)
    rows_of = lambda a, n: a[..., offsets[n]:offsets[n] + given[n].shape[0], :]
    local = lambda families: [stack("", family).astype(BF16) for family in families]

    def assemble(families, gathered):
        big = {}
        for family, (mine, other) in zip(families, gathered):
            for n in family:
                matrices = big.setdefault(int(n[1]), {})
                matrices[n.split("_", 1)[1]] = _from_gathered(n.split("_", 1)[1], rows_of(mine, n), rows_of(other, n))
        return big

    def sends_of(families, grads):
        return [jnp.concatenate([_owner_major(n.split("_", 1)[1], grads[int(n[1])][n.split("_", 1)[1]]) for n in family], axis=2)
                for family in families]

    small = [dict() for _ in range(5)]
    for n in small_names:
        if n == "final_norm":
            small[4]["norm"] = given[n]
        else:
            small[int(n[1])][n.split("_", 1)[1]] = given[n]

    big = [assemble(first, _all_gather(local(first), "gather_first_weights"))[0], None, None, None]
    carried = [(h, local(families), functools.partial(assemble, families),
                lambda grads, families=families: (sends_of(families, grads), [BF16] * len(families)))
               for h, families in carried_by.items()]
    loss, dx, grads, carried_parts = _local_step(given["x"][0], given["loss_target"][0], small, big, carried)
    loss = lax.psum(loss, ("x", "y", "c"))

    def grad_of(n):
        return grads[4]["norm"] if n == "final_norm" else grads[int(n[1])][n.split("_", 1)[1]]

    flat = lambda f: _stack_small([f(n) for n in small_names])
    sends = sends_of(first, grads) + [flat(grad_of).reshape(N_CORES, N_CHIPS, SMALL_ROWS, D_MODEL)]
    parts = _reduce_scatter(sends, [BF16] * len(first) + [F32], "scatter_grads")

    outs = {}
    tags = ("grad_", "delta_", "new_m_", "new_v_")
    all_families = first + carried_by[0] + carried_by[1]
    all_parts = list(parts[:-1]) + list(carried_parts[0]) + list(carried_parts[1])
    for i, (family, part) in enumerate(zip(all_families, all_parts)):
        results = _adamw(part, stack("", family), stack("m_", family), stack("v_", family), "adamw_matrices%d" % i)
        for tag, a in zip(tags, results):
            for n in family:
                outs[tag + n] = rows_of(a, n)

    my_slice = _sum_parts([parts[-1][s] for s in range(N_CHIPS)], F32, "sum_small_grads")
    mine, other = _all_gather([my_slice], "gather_small_grads")[0]
    first = lax.axis_index("c") == 0
    g_all = jnp.concatenate([jnp.where(first, mine, other), jnp.where(first, other, mine)], axis=0)
    g_all = g_all.reshape(1, N_DEV * SMALL_ROWS, D_MODEL)
    results = _adamw(g_all, flat(lambda n: given[n]), flat(lambda n: given["m_" + n]), flat(lambda n: given["v_" + n]),
                     "adamw_small")
    for tag, a in zip(tags, results):
        at = 0
        for n in small_names:
            rows = -(-given[n].size // D_MODEL)
            outs[tag + n] = a[at:at + rows].reshape(-1)[:given[n].size].reshape(given[n].shape)
            at += rows
    result = [loss, dx[None]]
    for tag in ("grad_", "delta_", "new_m_", "new_v_"):
        result += [outs[tag + n] for n in layer_names]
    return tuple(result)
```

```python
import functools
import math

import jax
import jax.numpy as jnp
from jax import lax
from jax.experimental import pallas as pl
from jax.experimental.pallas import tpu as pltpu

F32 = jnp.float32
BF16 = jnp.bfloat16

D_MODEL = 1024
SSM_GROUP = 16
SSM_GROUPS = D_MODEL // SSM_GROUP
SSM_STATE = 64
S5_BLOCK = 16
S5_LANES = S5_BLOCK * SSM_GROUP
HEAD_DIM = 64
N_Q_HEADS = 16
N_KV_HEADS = 2
GQA = N_Q_HEADS // N_KV_HEADS
Q_DIM = N_Q_HEADS * HEAD_DIM
KV_DIM = N_KV_HEADS * HEAD_DIM
ATTN_BLOCK = 128
ROPE_THETA = 10000.0
NORM_EPS = 1e-5
NEG_INF = -1e30
ATTN_SCALE = HEAD_DIM ** -0.5
N_DEV = 8

ADAM_LR = 0.001
ADAM_B1 = 0.9
ADAM_B2 = 0.999
ADAM_EPS = 1e-08
ADAM_WD = 0.01
ADAM_STEP = 10

VMEM_LIMIT = 56 * 1024 * 1024
ROWS_FWD = 512
ROWS_BWD = 512

NT = (((1,), (1,)), ((), ()))
TN = (((0,), (0,)), ((), ()))


def _params(n_grid):
    return pltpu.CompilerParams(dimension_semantics=("arbitrary",) * n_grid, vmem_limit_bytes=VMEM_LIMIT)


def _dot(a, b):
    return jnp.dot(a, b, preferred_element_type=F32)


def _dot_nt(a, b):
    return lax.dot_general(a, b, NT, preferred_element_type=F32)


def _dot_tn(a, b):
    return lax.dot_general(a, b, TN, preferred_element_type=F32)


def _sigmoid(x):
    return 1.0 / (1.0 + jnp.exp(-x))


_GELU_K = math.sqrt(2.0 / math.pi)


def _gelu(x):
    return x * (0.5 * (1.0 + jnp.tanh(_GELU_K * (x + 0.044715 * (x * x * x)))))


def _gelu_grad(x):
    t = jnp.tanh(_GELU_K * (x + 0.044715 * (x * x * x)))
    return 0.5 * (1.0 + t) + 0.5 * x * (1.0 - t * t) * (_GELU_K * (1.0 + 3.0 * 0.044715 * (x * x)))


def _row_spec(rows, cols):
    return pl.BlockSpec((rows, cols), lambda i: (i, 0))


def _const_spec(shape):
    zeros = (0,) * len(shape)
    return pl.BlockSpec(shape, lambda i: zeros, pipeline_mode=pl.Buffered(1))


def _rope_apply(t, cos, sin_signed, sign):
    lane = lax.broadcasted_iota(jnp.int32, (1, 128), 1)
    first_half = (lane % HEAD_DIM) < (HEAD_DIM // 2)
    out = []
    for j in range(t.shape[1] // 128):
        tj = t[:, 128 * j:128 * (j + 1)]
        partner = jnp.where(first_half, pltpu.roll(tj, 128 - HEAD_DIM // 2, 1), pltpu.roll(tj, HEAD_DIM // 2, 1))
        out.append(tj * cos + sign * (partner * sin_signed))
    return out[0] if len(out) == 1 else jnp.concatenate(out, axis=1)


def _inproj_fwd(x, norm, w, splits, dtypes, rope, name, exchange=None):
    t = x.shape[0]
    n = w.shape[1]
    rows = ROWS_FWD

    def body(*refs):
        if rope is None:
            x_ref, n_ref, w_ref = refs[:3]
            outs = refs[3:]
        else:
            x_ref, n_ref, w_ref, cos_ref, sin_ref = refs[:5]
            outs = refs[5:]
        xv = x_ref[...]
        rstd = lax.rsqrt(jnp.mean(xv * xv, axis=-1, keepdims=True) + NORM_EPS)
        h = (xv * rstd) * n_ref[...]
        proj = _dot(h.astype(BF16), w_ref[...])
        off = 0
        for i, width in enumerate(splits):
            piece = proj[:, off:off + width]
            if rope is not None and i < 2:
                piece = _rope_apply(piece, cos_ref[...], sin_ref[...], 1.0)
            outs[i][...] = piece.astype(dtypes[i])
            off += width

    in_specs = [_row_spec(rows, D_MODEL), _const_spec((1, D_MODEL)), _const_spec((D_MODEL, n))]
    args = [x, norm.reshape(1, D_MODEL), w]
    if rope is not None:
        in_specs += [_row_spec(rows, 128), _row_spec(rows, 128)]
        args += list(rope)
    return _call_hosting(
        exchange, body, args, name=name, steps=t // rows, in_specs=in_specs,
        out_specs=[_row_spec(rows, width) for width in splits],
        out_shape=[jax.ShapeDtypeStruct((t, width), dtype) for width, dtype in zip(splits, dtypes)])


def _inproj_bwd(x, norm, w, dpieces, dxn, name):
    t = x.shape[0]
    n = w.shape[1]
    rows = ROWS_BWD
    widths = [p.shape[1] for p in dpieces]
    k = len(dpieces)

    def body(*refs):
        x_ref, n_ref, w_ref, dxn_ref = refs[:4]
        d_refs = refs[4:4 + k]
        dx_ref, dw_ref, dn_ref = refs[4 + k:]
        @pl.when(pl.program_id(0) == 0)
        def _():
            dw_ref[...] = jnp.zeros_like(dw_ref)
            dn_ref[...] = jnp.zeros_like(dn_ref)

        xv = x_ref[...]
        rstd = lax.rsqrt(jnp.mean(xv * xv, axis=-1, keepdims=True) + NORM_EPS)
        xhat = xv * rstd
        h = xhat * n_ref[...]
        dproj = [r[...].astype(BF16) for r in d_refs]
        dproj = dproj[0] if k == 1 else jnp.concatenate(dproj, axis=1)
        dh = _dot_nt(dproj, w_ref[...])
        dw_ref[...] += _dot_tn(h.astype(BF16), dproj)
        dn_ref[...] += jnp.sum(dh * xhat, axis=0, keepdims=True)
        dxhat = dh * n_ref[...]
        dx_ref[...] = rstd * (dxhat - xhat * jnp.mean(dxhat * xhat, axis=-1, keepdims=True)) + dxn_ref[...]

    return _call_hosting(
        None, body, (x, norm.reshape(1, D_MODEL), w, dxn, *dpieces), name=name, steps=t // rows,
        in_specs=[_row_spec(rows, D_MODEL), _const_spec((1, D_MODEL)), _const_spec((D_MODEL, n)),
                  _row_spec(rows, D_MODEL)] + [_row_spec(rows, width) for width in widths],
        out_specs=[_row_spec(rows, D_MODEL), _const_spec((D_MODEL, n)), _const_spec((1, D_MODEL))],
        out_shape=[jax.ShapeDtypeStruct((t, D_MODEL), F32), jax.ShapeDtypeStruct((D_MODEL, n), F32),
                   jax.ShapeDtypeStruct((1, D_MODEL), F32)])[0]


def _s5_matrices(a_re, a_im, log_step, b_re, b_im, c_re, c_im):
    r = S5_BLOCK
    step = jnp.exp(log_step)[:, None]
    lr, li = a_re * step, a_im * step
    k = jnp.arange(r + 1, dtype=F32)
    mag = jnp.exp(lr[:, None, :] * k[:, None])
    pr = mag * jnp.cos(li[:, None, :] * k[:, None])
    pi = mag * jnp.sin(li[:, None, :] * k[:, None])
    nr, ni = pr[:, 1] - 1.0, pi[:, 1]
    den = a_re * a_re + a_im * a_im
    qr, qi = (nr * a_re + ni * a_im) / den, (ni * a_re - nr * a_im) / den
    bbr = qr[..., None] * b_re - qi[..., None] * b_im
    bbi = qr[..., None] * b_im + qi[..., None] * b_re
    wr = c_re[:, None] * pr[:, :, None, :] - c_im[:, None] * pi[:, :, None, :]
    wi = c_re[:, None] * pi[:, :, None, :] + c_im[:, None] * pr[:, :, None, :]
    w = jnp.concatenate([wr, -wi], axis=-1)
    bb = jnp.concatenate([bbr, bbi], axis=1)
    kern = jnp.einsum("gxp,gpi->gxi", w[:, :r].reshape(SSM_GROUPS, S5_LANES, 2 * SSM_STATE), bb,
                      precision=lax.Precision.HIGHEST).reshape(SSM_GROUPS, r, SSM_GROUP, SSM_GROUP)
    cpt = w[:, 1:].reshape(SSM_GROUPS, S5_LANES, 2 * SSM_STATE)
    prs = jnp.swapaxes(pr[:, r - 1::-1][:, :r], 1, 2)[..., None]
    pis = jnp.swapaxes(pi[:, r - 1::-1][:, :r], 1, 2)[..., None]
    bp_re = prs * bbr[:, :, None, :] - pis * bbi[:, :, None, :]
    bp_im = prs * bbi[:, :, None, :] + pis * bbr[:, :, None, :]
    bpt = jnp.concatenate([bp_re, bp_im], axis=1).reshape(SSM_GROUPS, 2 * SSM_STATE, S5_LANES)
    ar = pr[:, r].reshape(1, SSM_GROUPS * SSM_STATE)
    ai = pi[:, r].reshape(1, SSM_GROUPS * SSM_STATE)
    return kern, cpt, bpt, ar, ai


def _s5_toeplitz(kern):
    r = S5_BLOCK
    cols = [jnp.pad(kern[:, :r - s], ((0, 0), (s, 0), (0, 0), (0, 0))) for s in range(r)]
    return jnp.stack(cols, axis=3).reshape(SSM_GROUPS, S5_LANES, S5_LANES)


S5_OCTET = 128 // SSM_GROUP
S5_STEPS = SSM_GROUPS // S5_OCTET


def _oct_spec(t):
    return pl.BlockSpec((t, 128), lambda j: (0, j))


def _state_spec(nb):
    return pl.BlockSpec((nb, S5_OCTET * SSM_STATE), lambda j: (0, j))


def _gmat_spec(a, b):
    return pl.BlockSpec((S5_OCTET, a, b), lambda j: (j, 0, 0))


def _block_rows(ref, nb):
    by_position = jnp.swapaxes(ref[...].reshape(nb, S5_BLOCK, 128), 0, 1)
    return [by_position[r] for r in range(S5_BLOCK)]


def _store_block_rows(ref, pieces, nb):
    ref[...] = jnp.swapaxes(jnp.stack(pieces, axis=0), 0, 1).reshape(nb * S5_BLOCK, 128)


def _group_cols(pieces_t, g):
    return jnp.concatenate([p[SSM_GROUP * g:SSM_GROUP * (g + 1)] for p in pieces_t], axis=0)


def _state_cols(re_t, im_t, g):
    return jnp.concatenate([re_t[SSM_STATE * g:SSM_STATE * (g + 1)], im_t[SSM_STATE * g:SSM_STATE * (g + 1)]], axis=0)


def _s5_project(a, mat, name):
    t = a.shape[0]
    nb = t // S5_BLOCK

    def body(a_ref, m_ref, re_ref, im_ref):
        at = [p.T for p in _block_rows(a_ref, nb)]
        for pair in range(S5_OCTET // 2):
            xs = [_dot(m_ref[2 * pair + k], _group_cols(at, 2 * pair + k).astype(BF16)) for k in (0, 1)]
            lanes = slice(128 * pair, 128 * (pair + 1))
            re_ref[:, lanes] = jnp.concatenate([xs[0][:SSM_STATE], xs[1][:SSM_STATE]], axis=0).T
            im_ref[:, lanes] = jnp.concatenate([xs[0][SSM_STATE:], xs[1][SSM_STATE:]], axis=0).T

    return pl.pallas_call(
        body, name=name, grid=(S5_STEPS,),
        in_specs=[_oct_spec(t), _gmat_spec(2 * SSM_STATE, S5_LANES)],
        out_specs=[_state_spec(nb), _state_spec(nb)],
        out_shape=[jax.ShapeDtypeStruct((nb, SSM_GROUPS * SSM_STATE), F32)] * 2,
        compiler_params=_params(1),
    )(a, mat)


_SCAN_LANES = 2048


def _s5_scan_fwd(xre, xim, ar, ai, name):
    nb = xre.shape[0]
    col = pl.BlockSpec((nb, _SCAN_LANES), lambda j: (0, j))
    par = pl.BlockSpec((1, _SCAN_LANES), lambda j: (0, j))

    def body(xre_ref, xim_ref, ar_ref, ai_ref, hre_ref, him_ref):
        a_r, a_i = ar_ref[...], ai_ref[...]

        def step(b, carry):
            hr, hi = carry
            hre_ref[pl.ds(b, 1), :] = hr
            him_ref[pl.ds(b, 1), :] = hi
            xr, xi = xre_ref[pl.ds(b, 1), :], xim_ref[pl.ds(b, 1), :]
            return a_r * hr - a_i * hi + xr, a_r * hi + a_i * hr + xi

        zero = jnp.zeros((1, _SCAN_LANES), F32)
        lax.fori_loop(0, nb, step, (zero, zero))

    return pl.pallas_call(
        body, name=name, grid=(xre.shape[1] // _SCAN_LANES,),
        in_specs=[col, col, par, par], out_specs=[col, col],
        out_shape=[jax.ShapeDtypeStruct(xre.shape, F32)] * 2,
        compiler_params=_params(1),
    )(xre, xim, ar, ai)


def _s5_scan_bwd(dhre, dhim, hre, him, ar, ai, name):
    nb = dhre.shape[0]
    col = pl.BlockSpec((nb, _SCAN_LANES), lambda j: (0, j))
    par = pl.BlockSpec((1, _SCAN_LANES), lambda j: (0, j))

    def body(dhre_ref, dhim_ref, hre_ref, him_ref, ar_ref, ai_ref, dxre_ref, dxim_ref, dar_ref, dai_ref):
        a_r, a_i = ar_ref[...], ai_ref[...]

        def step(s, carry):
            gr, gi, dar, dai = carry
            b = nb - 1 - s
            dxre_ref[pl.ds(b, 1), :] = gr
            dxim_ref[pl.ds(b, 1), :] = gi
            hr, hi = hre_ref[pl.ds(b, 1), :], him_ref[pl.ds(b, 1), :]
            dar = dar + (hr * gr + hi * gi)
            dai = dai + (hr * gi - hi * gr)
            dr, di = dhre_ref[pl.ds(b, 1), :], dhim_ref[pl.ds(b, 1), :]
            return dr + (a_r * gr + a_i * gi), di + (a_r * gi - a_i * gr), dar, dai

        zero = jnp.zeros((1, _SCAN_LANES), F32)
        _, _, dar, dai = lax.fori_loop(0, nb, step, (zero, zero, zero, zero))
        dar_ref[...] = dar
        dai_ref[...] = dai

    return pl.pallas_call(
        body, name=name, grid=(dhre.shape[1] // _SCAN_LANES,),
        in_specs=[col, col, col, col, par, par], out_specs=[col, col, par, par],
        out_shape=[jax.ShapeDtypeStruct(dhre.shape, F32)] * 2 + [jax.ShapeDtypeStruct(ar.shape, F32)] * 2,
        compiler_params=_params(1),
    )(dhre, dhim, hre, him, ar, ai)


def _s5_outputs(u, hre, him, tm, cpt, d, name):
    t = u.shape[0]
    nb = t // S5_BLOCK

    def body(u_ref, hre_ref, him_ref, tm_ref, cpt_ref, d_ref, y_ref):
        u_rows = _block_rows(u_ref, nb)
        ut = [p.T for p in u_rows]
        hre_t, him_t = hre_ref[...].T, him_ref[...].T
        yts = []
        for g in range(S5_OCTET):
            yts.append(_dot(tm_ref[g], _group_cols(ut, g).astype(BF16))
                       + _dot(cpt_ref[g], _state_cols(hre_t, him_t, g).astype(BF16)))
        y_rows = []
        for r in range(S5_BLOCK):
            rows = jnp.concatenate([yt[SSM_GROUP * r:SSM_GROUP * (r + 1)] for yt in yts], axis=0)
            y_rows.append(rows.T + d_ref[...] * u_rows[r])
        _store_block_rows(y_ref, y_rows, nb)

    return pl.pallas_call(
        body, name=name, grid=(S5_STEPS,),
        in_specs=[_oct_spec(t), _state_spec(nb), _state_spec(nb), _gmat_spec(S5_LANES, S5_LANES),
                  _gmat_spec(S5_LANES, 2 * SSM_STATE), _oct_spec(1)],
        out_specs=_oct_spec(t),
        out_shape=jax.ShapeDtypeStruct(u.shape, F32),
        compiler_params=_params(1),
    )(u, hre, him, tm, cpt, d.reshape(1, D_MODEL))


def _s5_backward(dy, u, hre, him, dxre, dxim, tmt, bp, d, name, exchange=None):
    t = u.shape[0]
    nb = t // S5_BLOCK

    def body(dy_ref, u_ref, hre_ref, him_ref, dxre_ref, dxim_ref, tmt_ref, bp_ref, d_ref,
             du_ref, dk_ref, dcpt_ref, dbpt_ref, dd_ref, dtm_scratch):
        dy_rows, u_rows = _block_rows(dy_ref, nb), _block_rows(u_ref, nb)
        dyt, ut = [p.T for p in dy_rows], [p.T for p in u_rows]
        hre_t, him_t = hre_ref[...].T, him_ref[...].T
        dxre_t, dxim_t = dxre_ref[...].T, dxim_ref[...].T
        duts = []
        for g in range(S5_OCTET):
            dyg, ug = _group_cols(dyt, g).astype(BF16), _group_cols(ut, g).astype(BF16)
            hg = _state_cols(hre_t, him_t, g).astype(BF16)
            dxg = _state_cols(dxre_t, dxim_t, g).astype(BF16)
            duts.append(_dot(tmt_ref[g], dyg) + _dot(bp_ref[g], dxg))
            dtm_scratch[...] = _dot_nt(dyg, ug)
            dk = dtm_scratch[:, :SSM_GROUP]
            for s in range(1, S5_BLOCK):
                below = dtm_scratch[SSM_GROUP * s:, SSM_GROUP * s:SSM_GROUP * (s + 1)]
                dk = dk + jnp.concatenate([below, jnp.zeros((SSM_GROUP * s, SSM_GROUP), F32)], axis=0)
            dk_ref[g] = dk
            dcpt_ref[g] = _dot_nt(dyg, hg)
            dbpt_ref[g] = _dot_nt(dxg, ug)
        dd = jnp.zeros((1, 128), F32)
        du_rows = []
        for r in range(S5_BLOCK):
            rows = jnp.concatenate([dut[SSM_GROUP * r:SSM_GROUP * (r + 1)] for dut in duts], axis=0)
            du_rows.append(rows.T + d_ref[...] * dy_rows[r])
            dd = dd + jnp.sum(dy_rows[r] * u_rows[r], axis=0, keepdims=True)
        _store_block_rows(du_ref, du_rows, nb)
        dd_ref[...] = dd

    return _call_hosting(
        exchange, body, (dy, u, hre, him, dxre, dxim, tmt, bp, d.reshape(1, D_MODEL)), name=name, steps=S5_STEPS,
        in_specs=[_oct_spec(t), _oct_spec(t), _state_spec(nb), _state_spec(nb), _state_spec(nb), _state_spec(nb),
                  _gmat_spec(S5_LANES, S5_LANES), _gmat_spec(S5_LANES, 2 * SSM_STATE), _oct_spec(1)],
        out_specs=[_oct_spec(t), _gmat_spec(S5_LANES, SSM_GROUP), _gmat_spec(S5_LANES, 2 * SSM_STATE),
                   _gmat_spec(2 * SSM_STATE, S5_LANES), _oct_spec(1)],
        out_shape=[jax.ShapeDtypeStruct(u.shape, F32),
                   jax.ShapeDtypeStruct((SSM_GROUPS, S5_LANES, SSM_GROUP), F32),
                   jax.ShapeDtypeStruct((SSM_GROUPS, S5_LANES, 2 * SSM_STATE), F32),
                   jax.ShapeDtypeStruct((SSM_GROUPS, 2 * SSM_STATE, S5_LANES), F32),
                   jax.ShapeDtypeStruct((1, D_MODEL), F32)],
        scratch_shapes=[pltpu.VMEM((S5_LANES, S5_LANES), F32)])


def _ssm_out_fwd(y, gate, x, w_glu, b_glu, w_out, name, exchange=None):
    t = x.shape[0]
    rows = ROWS_FWD

    def body(y_ref, g_ref, x_ref, wg_ref, bg_ref, wo_ref, o_ref):
        z0 = _gelu(y_ref[...])
        s = _dot(z0.astype(BF16), wg_ref[...]) + bg_ref[...]
        gate_v = g_ref[...].astype(F32)
        a = (z0 * _sigmoid(s)) * (gate_v * _sigmoid(gate_v))
        o_ref[...] = x_ref[...] + _dot(a.astype(BF16), wo_ref[...])

    (xn,), got = _call_hosting(
        exchange, body, (y, gate, x, w_glu, b_glu.reshape(1, D_MODEL), w_out), name=name, steps=t // rows,
        in_specs=[_row_spec(rows, D_MODEL)] * 3 + [_const_spec((D_MODEL, D_MODEL)), _const_spec((1, D_MODEL)),
                                                   _const_spec((D_MODEL, D_MODEL))],
        out_specs=[_row_spec(rows, D_MODEL)],
        out_shape=[jax.ShapeDtypeStruct((t, D_MODEL), F32)])
    return xn, got


def _ssm_out_bwd(dxn, y, gate, w_glu, b_glu, w_out, name, exchange=None):
    t = y.shape[0]
    rows = ROWS_BWD

    def body(dxn_ref, y_ref, g_ref, wg_ref, bg_ref, wo_ref, dy_ref, dg_ref, dwg_ref, dbg_ref, dwo_ref):
        @pl.when(pl.program_id(0) == 0)
        def _():
            dwo_ref[...] = jnp.zeros_like(dwo_ref)
            dwg_ref[...] = jnp.zeros_like(dwg_ref)
            dbg_ref[...] = jnp.zeros_like(dbg_ref)

        yv = y_ref[...]
        z0 = _gelu(yv)
        z0b = z0.astype(BF16)
        sg = _sigmoid(_dot(z0b, wg_ref[...]) + bg_ref[...])
        z = z0 * sg
        gate_v = g_ref[...].astype(F32)
        sgg = _sigmoid(gate_v)
        silu = gate_v * sgg
        dob = dxn_ref[...].astype(BF16)
        da = _dot_nt(dob, wo_ref[...])
        dwo_ref[...] += _dot_tn((z * silu).astype(BF16), dob)
        dz = da * silu
        dg_ref[...] = (da * z * (sgg * (1.0 + gate_v * (1.0 - sgg)))).astype(BF16)
        ds = dz * z0 * (sg * (1.0 - sg))
        dsb = ds.astype(BF16)
        dz0 = dz * sg + _dot_nt(dsb, wg_ref[...])
        dwg_ref[...] += _dot_tn(z0b, dsb)
        dbg_ref[...] += jnp.sum(ds, axis=0, keepdims=True)
        dy_ref[...] = dz0 * _gelu_grad(yv)

    sq = _const_spec((D_MODEL, D_MODEL))
    vec = _const_spec((1, D_MODEL))
    return _call_hosting(
        exchange, body, (dxn, y, gate, w_glu, b_glu.reshape(1, D_MODEL), w_out), name=name, steps=t // rows,
        in_specs=[_row_spec(rows, D_MODEL)] * 3 + [sq, vec, sq],
        out_specs=[_row_spec(rows, D_MODEL), _row_spec(rows, D_MODEL), sq, vec, sq],
        out_shape=[jax.ShapeDtypeStruct((t, D_MODEL), F32), jax.ShapeDtypeStruct((t, D_MODEL), BF16),
                   jax.ShapeDtypeStruct((D_MODEL, D_MODEL), F32), jax.ShapeDtypeStruct((1, D_MODEL), F32),
                   jax.ShapeDtypeStruct((D_MODEL, D_MODEL), F32)])


KV_LANES = GQA * ATTN_BLOCK


def _attn_bias(block_is_first):
    kj = lax.broadcasted_iota(jnp.int32, (2 * ATTN_BLOCK, ATTN_BLOCK), 0)
    qi = lax.broadcasted_iota(jnp.int32, (2 * ATTN_BLOCK, ATTN_BLOCK), 1)
    dist = qi + ATTN_BLOCK - kj
    valid = (dist >= 0) & (dist < ATTN_BLOCK) & (jnp.logical_not(block_is_first) | (kj >= ATTN_BLOCK))
    return jnp.tile(jnp.where(valid, 0.0, NEG_INF).astype(F32), (1, GQA))


def _head_cols(a_t, kvh):
    heads = range(kvh * GQA, (kvh + 1) * GQA)
    return jnp.concatenate([a_t[HEAD_DIM * h:HEAD_DIM * (h + 1)] for h in heads], axis=1)


def _head_rows(a_cols):
    stacked = jnp.concatenate([a_cols[:, ATTN_BLOCK * g:ATTN_BLOCK * (g + 1)] for g in range(GQA)], axis=0)
    return stacked.T


def _kv_rows(prev_ref, cur_ref, kvh):
    lanes = slice(HEAD_DIM * kvh, HEAD_DIM * (kvh + 1))
    return jnp.concatenate([prev_ref[:, lanes], cur_ref[:, lanes]], axis=0).astype(BF16)


def _kv_cols(prev_t, cur_t, kvh):
    rows = slice(HEAD_DIM * kvh, HEAD_DIM * (kvh + 1))
    return jnp.concatenate([prev_t[rows], cur_t[rows]], axis=1).astype(BF16)


def _attn_probs(kk, q_cols, sink_row, bias):
    s = _dot(kk, q_cols) * ATTN_SCALE + bias
    m = jnp.maximum(jnp.max(s, axis=0, keepdims=True), sink_row)
    p = jnp.exp(s - m)
    e_sink = jnp.exp(sink_row - m)
    inv = 1.0 / (jnp.sum(p, axis=0, keepdims=True) + e_sink)
    return p * inv, e_sink * inv


def _sink_cols(sinks):
    return jnp.repeat(sinks, ATTN_BLOCK).reshape(N_KV_HEADS, 1, KV_LANES)


def _attn_fwd(q, k, v, sinks, name, exchange=None):
    t = q.shape[0]
    nblk = t // ATTN_BLOCK

    def body(s_ref, q_ref, kc_ref, kp_ref, vc_ref, vp_ref, o_ref):
        bias = _attn_bias(pl.program_id(0) == 0)
        q_t = q_ref[...].astype(F32).T
        vp_t, vc_t = vp_ref[...].astype(F32).T, vc_ref[...].astype(F32).T
        for kvh in range(N_KV_HEADS):
            p, _ = _attn_probs(_kv_rows(kp_ref, kc_ref, kvh), _head_cols(q_t, kvh).astype(BF16), s_ref[kvh], bias)
            o_cols = _dot(_kv_cols(vp_t, vc_t, kvh), p.astype(BF16))
            o_ref[:, GQA * HEAD_DIM * kvh:GQA * HEAD_DIM * (kvh + 1)] = _head_rows(o_cols)

    cur = lambda i: (i, 0)
    prev = lambda i: (jnp.maximum(i - 1, 0), 0)
    (o,), got = _call_hosting(
        exchange, body, (_sink_cols(sinks), q, k, k, v, v), name=name, steps=nblk,
        in_specs=[_const_spec((N_KV_HEADS, 1, KV_LANES)),
                  pl.BlockSpec((ATTN_BLOCK, Q_DIM), cur),
                  pl.BlockSpec((ATTN_BLOCK, KV_DIM), cur), pl.BlockSpec((ATTN_BLOCK, KV_DIM), prev),
                  pl.BlockSpec((ATTN_BLOCK, KV_DIM), cur), pl.BlockSpec((ATTN_BLOCK, KV_DIM), prev)],
        out_specs=[pl.BlockSpec((ATTN_BLOCK, Q_DIM), cur)],
        out_shape=[jax.ShapeDtypeStruct((t, Q_DIM), F32)])
    return o, got


def _attn_bwd(q, k, v, o, do, sinks, rope, name, exchange=None):
    t = q.shape[0]
    nblk = t // ATTN_BLOCK

    def body(s_ref, q_ref, o_ref, do_ref, kp_ref, kc_ref, vp_ref, vc_ref, cosq_ref, sinq_ref, cosk_ref, sinkey_ref,
             dq_ref, dk_ref, dv_ref, ds_ref, new_k, new_v, wait_k, wait_v, dq_rot):
        n = pl.program_id(0)

        @pl.when(n == 0)
        def _():
            ds_ref[...] = jnp.zeros_like(ds_ref)
            wait_k[...] = jnp.zeros_like(wait_k)
            wait_v[...] = jnp.zeros_like(wait_v)

        @pl.when(n < nblk)
        def _():
            bias = _attn_bias(n == 0)
            q_t, o_t, do_t = q_ref[...].astype(F32).T, o_ref[...].T, do_ref[...].T
            kp_t, kc_t = kp_ref[...].astype(F32).T, kc_ref[...].astype(F32).T
            for kvh in range(N_KV_HEADS):
                q_cols = _head_cols(q_t, kvh).astype(BF16)
                do_cols = _head_cols(do_t, kvh)
                delta = jnp.sum(do_cols * _head_cols(o_t, kvh), axis=0, keepdims=True)
                do_cols = do_cols.astype(BF16)
                p, p_sink = _attn_probs(_kv_rows(kp_ref, kc_ref, kvh), q_cols, s_ref[kvh], bias)
                dp = _dot(_kv_rows(vp_ref, vc_ref, kvh), do_cols)
                ds = (p * (dp - delta) * ATTN_SCALE).astype(BF16)
                lanes = slice(GQA * HEAD_DIM * kvh, GQA * HEAD_DIM * (kvh + 1))
                dq_rot[:, lanes] = _head_rows(_dot(_kv_cols(kp_t, kc_t, kvh), ds))
                head = slice(HEAD_DIM * kvh, HEAD_DIM * (kvh + 1))
                new_k[:, head] = _dot_nt(ds, q_cols)
                new_v[:, head] = _dot_nt(p.astype(BF16), do_cols)
                ds_ref[kvh] += -(p_sink * delta)
            dq_ref[...] = _rope_apply(dq_rot[...], cosq_ref[...], sinq_ref[...], -1.0).astype(BF16)

        @pl.when(n == nblk)
        def _():
            new_k[...] = jnp.zeros_like(new_k)
            new_v[...] = jnp.zeros_like(new_v)

        dk_ref[...] = _rope_apply(wait_k[...] + new_k[:ATTN_BLOCK], cosk_ref[...], sinkey_ref[...], -1.0).astype(BF16)
        dv_ref[...] = (wait_v[...] + new_v[:ATTN_BLOCK]).astype(BF16)
        wait_k[...] = new_k[ATTN_BLOCK:]
        wait_v[...] = new_v[ATTN_BLOCK:]

    cur = lambda i: (jnp.minimum(i, nblk - 1), 0)
    prev = lambda i: (jnp.maximum(i - 1, 0), 0)
    qs = lambda f: pl.BlockSpec((ATTN_BLOCK, Q_DIM), f)
    ks = lambda f: pl.BlockSpec((ATTN_BLOCK, KV_DIM), f)
    sink_spec = _const_spec((N_KV_HEADS, 1, KV_LANES))
    return _call_hosting(
        exchange, body, (_sink_cols(sinks), q, o, do, k, k, v, v, rope[0], rope[1], rope[0], rope[1]),
        name=name, steps=nblk + 1,
        in_specs=[sink_spec, qs(cur), qs(cur), qs(cur), ks(prev), ks(cur), ks(prev), ks(cur),
                  ks(cur), ks(cur), ks(prev), ks(prev)],
        out_specs=[qs(cur), ks(prev), ks(prev), sink_spec],
        out_shape=[jax.ShapeDtypeStruct((t, Q_DIM), BF16), jax.ShapeDtypeStruct((t, KV_DIM), BF16),
                   jax.ShapeDtypeStruct((t, KV_DIM), BF16), jax.ShapeDtypeStruct((N_KV_HEADS, 1, KV_LANES), F32)],
        scratch_shapes=[pltpu.VMEM((2 * ATTN_BLOCK, KV_DIM), F32), pltpu.VMEM((2 * ATTN_BLOCK, KV_DIM), F32),
                        pltpu.VMEM((ATTN_BLOCK, KV_DIM), F32), pltpu.VMEM((ATTN_BLOCK, KV_DIM), F32),
                        pltpu.VMEM((ATTN_BLOCK, Q_DIM), F32)])


def _attn_out_fwd(o, gate, x, w_out, name, exchange=None):
    t = x.shape[0]
    rows = ROWS_FWD

    def body(o_ref, g_ref, x_ref, wo_ref, xn_ref):
        gate_v = g_ref[...].astype(F32)
        a = o_ref[...] * (gate_v * _sigmoid(gate_v))
        xn_ref[...] = x_ref[...] + _dot(a.astype(BF16), wo_ref[...])

    (xn,), got = _call_hosting(
        exchange, body, (o, gate, x, w_out), name=name, steps=t // rows,
        in_specs=[_row_spec(rows, D_MODEL)] * 3 + [_const_spec((D_MODEL, D_MODEL))],
        out_specs=[_row_spec(rows, D_MODEL)],
        out_shape=[jax.ShapeDtypeStruct((t, D_MODEL), F32)])
    return xn, got


def _attn_out_bwd(dxn, o, gate, w_out, name, exchange=None):
    t = o.shape[0]
    rows = ROWS_BWD

    def body(dxn_ref, o_ref, g_ref, wo_ref, do_ref, dg_ref, dwo_ref):
        @pl.when(pl.program_id(0) == 0)
        def _():
            dwo_ref[...] = jnp.zeros_like(dwo_ref)

        gate_v, ov = g_ref[...].astype(F32), o_ref[...]
        sgg = _sigmoid(gate_v)
        silu = gate_v * sgg
        dob = dxn_ref[...].astype(BF16)
        da = _dot_nt(dob, wo_ref[...])
        dwo_ref[...] += _dot_tn((ov * silu).astype(BF16), dob)
        do_ref[...] = da * silu
        dg_ref[...] = (da * ov * (sgg * (1.0 + gate_v * (1.0 - sgg)))).astype(BF16)

    sq = _const_spec((D_MODEL, D_MODEL))
    return _call_hosting(
        exchange, body, (dxn, o, gate, w_out), name=name, steps=t // rows,
        in_specs=[_row_spec(rows, D_MODEL)] * 3 + [sq],
        out_specs=[_row_spec(rows, D_MODEL), _row_spec(rows, D_MODEL), sq],
        out_shape=[jax.ShapeDtypeStruct((t, D_MODEL), F32), jax.ShapeDtypeStruct((t, D_MODEL), BF16),
                   jax.ShapeDtypeStruct((D_MODEL, D_MODEL), F32)])


def _loss_head(x, norm, target, name):
    t = x.shape[0]
    rows = ROWS_FWD

    def body(x_ref, n_ref, t_ref, loss_ref, dx_ref, dn_ref):
        i = pl.program_id(0)
        xv = x_ref[...]
        rstd = lax.rsqrt(jnp.mean(xv * xv, axis=-1, keepdims=True) + NORM_EPS)
        xhat = xv * rstd
        err = xhat * n_ref[...] - t_ref[...]
        part = 0.5 * jnp.sum(jnp.mean(err * err, axis=-1, keepdims=True), axis=0, keepdims=True)
        dy = err * (1.0 / D_MODEL)
        dn = jnp.sum(dy * xhat, axis=0, keepdims=True)
        dxhat = dy * n_ref[...]
        dx_ref[...] = rstd * (dxhat - xhat * jnp.mean(dxhat * xhat, axis=-1, keepdims=True))

        @pl.when(i == 0)
        def _():
            loss_ref[...] = jnp.zeros((8, 128), F32) + part
            dn_ref[...] = dn

        @pl.when(i > 0)
        def _():
            loss_ref[...] += part
            dn_ref[...] += dn

    return pl.pallas_call(
        body, name=name, grid=(t // rows,),
        in_specs=[_row_spec(rows, D_MODEL), _const_spec((1, D_MODEL)), _row_spec(rows, D_MODEL)],
        out_specs=[_const_spec((8, 128)), _row_spec(rows, D_MODEL), _const_spec((1, D_MODEL))],
        out_shape=[jax.ShapeDtypeStruct((8, 128), F32), jax.ShapeDtypeStruct((t, D_MODEL), F32),
                   jax.ShapeDtypeStruct((1, D_MODEL), F32)],
        compiler_params=_params(1),
    )(x, norm.reshape(1, D_MODEL), target)


N_CHIPS = 4
N_CORES = 2
CHIP_FLIPS = ((0, 1), (1, 0), (1, 1))
ICI_CHUNKS = 2
D2D_CHUNKS = 8


def _n_chunks(rows, dtype, most):
    unit = 16 if dtype == BF16 else 8
    return max(n for n in range(1, most + 1) if rows % n == 0 and (rows // n) % unit == 0)


def _chunks_of(arrays, most):
    out = []
    for a in arrays:
        n = _n_chunks(a.shape[-2], a.dtype, most)
        out.append((n, a.shape[-2] // n))
    return out


class _Exchange:
    def __init__(self, arrays, out_shape, scratch, copies):
        self.arrays, self.out_shape, self.scratch, self._copies = arrays, out_shape, scratch, copies

    def start(self, *refs):
        for cp in self._copies(*refs)[0]:
            cp.start()

    def wait(self, *refs):
        for wait in self._copies(*refs)[1]:
            wait()


def _chips_exchange(sends, per_dest):
    n = len(sends)
    chunking = _chunks_of(sends, ICI_CHUNKS)

    def copies(send_refs, recv_refs, sems):
        x, y, c = lax.axis_index("x"), lax.axis_index("y"), lax.axis_index("c")
        me = 2 * x + y

        def peer(k):
            fx, fy = CHIP_FLIPS[k]
            px, py = x + fx - 2 * x * fx, y + fy - 2 * y * fy
            return (px, py, c), 2 * px + py

        to_start, waits = [], []
        for a in range(n):
            send_sems, recv_sems, local_sems = sems[3 * a:3 * a + 3]
            chunks, chunk_rows = chunking[a]
            for j in range(chunks):
                part = pl.ds(j * chunk_rows, chunk_rows)
                src = lambda number: send_refs[a].at[number, part] if per_dest else send_refs[a].at[part]
                for k in range(len(CHIP_FLIPS)):
                    to, to_number = peer(k)
                    remote = lambda landing: pltpu.make_async_remote_copy(
                        src_ref=src(to_number), dst_ref=recv_refs[a].at[landing, part],
                        send_sem=send_sems.at[k, j], recv_sem=recv_sems.at[k, j],
                        device_id=to, device_id_type=pl.DeviceIdType.MESH)
                    to_start.append(remote(me))
                    waits += [remote(me).wait_send, remote(to_number).wait_recv]
                own = pltpu.make_async_copy(src(me), recv_refs[a].at[me, part], local_sems.at[j])
                to_start.append(own)
                waits.append(own.wait)
        return to_start, waits

    scratch = []
    for chunks, _ in chunking:
        scratch += [pltpu.SemaphoreType.DMA((len(CHIP_FLIPS), chunks)), pltpu.SemaphoreType.DMA((len(CHIP_FLIPS), chunks)),
                    pltpu.SemaphoreType.DMA((chunks,))]
    return _Exchange(sends, [jax.ShapeDtypeStruct((N_CHIPS,) + a.shape[-2:], a.dtype) for a in sends], scratch, copies)


def _cores_exchange(sends, per_dest):
    n = len(sends)
    chunking = _chunks_of(sends, D2D_CHUNKS)

    def copies(send_refs, got_refs, sems):
        c = lax.axis_index("c")
        sibling = (lax.axis_index("x"), lax.axis_index("y"), 1 - c)
        to_start = []
        for a in range(n):
            chunks, chunk_rows = chunking[a]
            for j in range(chunks):
                part = pl.ds(j * chunk_rows, chunk_rows)
                to_start.append(pltpu.make_async_remote_copy(
                    src_ref=send_refs[a].at[1 - c, part] if per_dest else send_refs[a].at[part],
                    dst_ref=got_refs[a].at[part], send_sem=sems[2 * a].at[j], recv_sem=sems[2 * a + 1].at[j],
                    device_id=sibling, device_id_type=pl.DeviceIdType.MESH))
        return to_start, [cp.wait for cp in to_start]

    scratch = []
    for chunks, _ in chunking:
        scratch += [pltpu.SemaphoreType.DMA((chunks,)), pltpu.SemaphoreType.DMA((chunks,))]
    return _Exchange(sends, [jax.ShapeDtypeStruct(a.shape[-2:], a.dtype) for a in sends], scratch, copies)


def _run_exchange(exchange, name):
    n = len(exchange.arrays)

    def body(*refs):
        parts = refs[:n], refs[n:2 * n], refs[2 * n:]
        exchange.start(*parts)
        exchange.wait(*parts)

    hbm = pl.BlockSpec(memory_space=pltpu.HBM)
    return pl.pallas_call(body, name=name, in_specs=[hbm] * n, out_specs=[hbm] * n, out_shape=exchange.out_shape,
                          scratch_shapes=exchange.scratch)(*exchange.arrays)


def _call_hosting(exchange, body, args, *, name, steps, in_specs, out_specs, out_shape, scratch_shapes=()):
    common = dict(name=name, grid=(steps,), compiler_params=_params(1))
    if exchange is None:
        return pl.pallas_call(body, in_specs=in_specs, out_specs=out_specs, out_shape=out_shape,
                              scratch_shapes=list(scratch_shapes), **common)(*args), None
    n_in, n_out, n_scratch, k = len(in_specs), len(out_specs), len(scratch_shapes), len(exchange.arrays)

    def hosting(*refs):
        ins, sends = refs[:n_in], refs[n_in:n_in + k]
        outs, recvs = refs[n_in + k:n_in + k + n_out], refs[n_in + k + n_out:n_in + 2 * k + n_out]
        scratch = refs[n_in + 2 * k + n_out:n_in + 2 * k + n_out + n_scratch]
        sems = refs[n_in + 2 * k + n_out + n_scratch:]
        pl.when(pl.program_id(0) == 0)(lambda: exchange.start(sends, recvs, sems))
        body(*ins, *outs, *scratch)
        pl.when(pl.program_id(0) == steps - 1)(lambda: exchange.wait(sends, recvs, sems))

    hbm = pl.BlockSpec(memory_space=pltpu.HBM)
    results = pl.pallas_call(
        hosting, in_specs=list(in_specs) + [hbm] * k, out_specs=list(out_specs) + [hbm] * k,
        out_shape=list(out_shape) + exchange.out_shape, scratch_shapes=list(scratch_shapes) + exchange.scratch, **common,
    )(*args, *exchange.arrays)
    return results[:n_out], results[n_out:]


def _exchange_chips(sends, per_dest, name):
    return _run_exchange(_chips_exchange(sends, per_dest), name)


def _swap_cores(sends, per_dest, name):
    return _run_exchange(_cores_exchange(sends, per_dest), name)


def _all_gather(arrays, name):
    by_chip = _exchange_chips(arrays, False, name + "_chips")
    others = _swap_cores([r.reshape(-1, r.shape[-1]) for r in by_chip], False, name + "_cores")
    return [(m, o.reshape(m.shape)) for m, o in zip(by_chip, others)]


def _in_device_order(mine, other, axis):
    first = lax.axis_index("c") == 0
    pieces = []
    for m, o in zip(mine, other):
        pieces += [jnp.where(first, m, o), jnp.where(first, o, m)]
    return jnp.concatenate(pieces, axis=axis)


def _sum_core(send, got, out_dtype, name):
    _, n, cols = send.shape
    rows = min(n, 256)
    while n % rows:
        rows -= 16

    def body(c_ref, keep_ref, got_ref, o_ref):
        o_ref[...] = (keep_ref[...].astype(F32) + got_ref[...].astype(F32)).astype(out_dtype)

    return pl.pallas_call(
        body, name=name, out_shape=jax.ShapeDtypeStruct((n, cols), out_dtype),
        grid_spec=pltpu.PrefetchScalarGridSpec(
            num_scalar_prefetch=1, grid=(n // rows,),
            in_specs=[pl.BlockSpec((None, rows, cols), lambda i, c: (c[0], i, 0)),
                      pl.BlockSpec((rows, cols), lambda i, c: (i, 0))],
            out_specs=pl.BlockSpec((rows, cols), lambda i, c: (i, 0))),
        compiler_params=_params(1),
    )(lax.axis_index("c").astype(jnp.int32).reshape(1), send, got)


def _sum_parts(parts, out_dtype, name):
    n, cols = parts[0].shape
    rows = min(n, 256)
    while n % rows:
        rows -= 16

    def body(*refs):
        acc = refs[0][...].astype(F32)
        for ref in refs[1:-1]:
            acc = acc + ref[...].astype(F32)
        refs[-1][...] = acc.astype(out_dtype)

    return pl.pallas_call(
        body, name=name, grid=(n // rows,),
        in_specs=[_row_spec(rows, cols)] * len(parts),
        out_specs=_row_spec(rows, cols),
        out_shape=jax.ShapeDtypeStruct((n, cols), out_dtype),
        compiler_params=_params(1),
    )(*parts)


def _reduce_scatter(sends, wire_dtypes, name):
    halves = [s.reshape(N_CORES, N_CHIPS * s.shape[2], s.shape[3]) for s in sends]
    gots = _swap_cores(halves, True, name + "_cores")
    sums = [_sum_core(h, g, dt, "%s_core_sum%d" % (name, i)).reshape((N_CHIPS,) + s.shape[2:])
            for i, (h, g, dt, s) in enumerate(zip(halves, gots, wire_dtypes, sends))]
    return _exchange_chips(sums, True, name + "_chips")


def _adamw(parts, w, m, v, name):
    n, cols = w.shape
    k = parts.shape[0]
    rows = min(n, 256)
    while n % rows:
        rows -= 8
    c1 = 1.0 - ADAM_B1 ** ADAM_STEP
    c2 = 1.0 - ADAM_B2 ** ADAM_STEP

    def body(p_ref, w_ref, m_ref, v_ref, g_ref, d_ref, nm_ref, nv_ref):
        g = p_ref[0].astype(F32)
        for s in range(1, k):
            g = g + p_ref[s].astype(F32)
        nm = ADAM_B1 * m_ref[...] + (1.0 - ADAM_B1) * g
        nv = ADAM_B2 * v_ref[...] + (1.0 - ADAM_B2) * (g * g)
        g_ref[...] = g
        nm_ref[...] = nm
        nv_ref[...] = nv
        d_ref[...] = -ADAM_LR * ((nm / c1) / (jnp.sqrt(nv / c2) + ADAM_EPS) + ADAM_WD * w_ref[...])

    blk = _row_spec(rows, cols)
    return pl.pallas_call(
        body, name=name, grid=(n // rows,),
        in_specs=[pl.BlockSpec((k, rows, cols), lambda i: (0, i, 0)), blk, blk, blk],
        out_specs=[blk] * 4,
        out_shape=[jax.ShapeDtypeStruct((n, cols), F32)] * 4,
        compiler_params=_params(1),
    )(parts, w, m, v)


SSM_KEYS = ("norm", "w_in", "a_re", "a_im", "log_step", "b_re", "b_im", "c_re", "c_im", "d", "w_glu", "b_glu", "w_out")
ATTN_KEYS = ("norm", "w_in", "sinks", "w_out")
LAYER_KEYS = (SSM_KEYS, ATTN_KEYS, SSM_KEYS, ATTN_KEYS)
BIG_KEYS = ("w_in", "w_glu", "w_out")
ATTN_SPLITS = (Q_DIM, KV_DIM, KV_DIM, D_MODEL)


def _rope_tables(t):
    pos = jnp.arange(t, dtype=F32)
    inv_freq = ROPE_THETA ** (-jnp.arange(0, HEAD_DIM, 2, dtype=F32) / HEAD_DIM)
    ang = pos[:, None] * inv_freq[None, :]
    cos, sin = jnp.cos(ang), jnp.sin(ang)
    return jnp.tile(jnp.concatenate([cos, cos], axis=1), (1, 2)), jnp.tile(jnp.concatenate([-sin, sin], axis=1), (1, 2))


def _gather_ici_stage(gather):
    return None if gather is None else _chips_exchange(gather, False)


def _gather_d2d_stage(by_chip):
    return None if by_chip is None else _cores_exchange([r.reshape(-1, r.shape[-1]) for r in by_chip], False)


def _gathered(by_chip, others):
    return None if by_chip is None else [(m, other.reshape(m.shape)) for m, other in zip(by_chip, others)]


def _scatter_d2d_stage(scatter):
    if scatter is None:
        return None, None
    halves = [s.reshape(N_CORES, N_CHIPS * s.shape[2], s.shape[3]) for s in scatter[0]]
    return halves, _cores_exchange(halves, True)


def _scatter_ici_stage(scatter, halves, gots, tag):
    if scatter is None:
        return None
    sums = [_sum_core(h, g, dt, "%sscatter_core_sum%d" % (tag, j)).reshape((N_CHIPS,) + s.shape[2:])
            for j, (h, g, dt, s) in enumerate(zip(halves, gots, scatter[1], scatter[0]))]
    return _chips_exchange(sums, True)


def _ssm_layer_fwd(i, x, p, w, gather=None):
    tag = "l%d_" % i
    mats, mats_vjp = jax.vjp(_s5_matrices, p["a_re"], p["a_im"], p["log_step"], p["b_re"], p["b_im"], p["c_re"], p["c_im"])
    kern, cpt, bpt, ar, ai = mats
    tm = _s5_toeplitz(kern.astype(BF16))
    mb = dict(tm=tm, tmt=jnp.swapaxes(tm, 1, 2), cpt=cpt.astype(BF16),
              cp=jnp.swapaxes(cpt, 1, 2).astype(BF16), bpt=bpt.astype(BF16), bp=jnp.swapaxes(bpt, 1, 2).astype(BF16))
    (u, gate), by_chip = _inproj_fwd(x, p["norm"], w["w_in"], (D_MODEL, D_MODEL), (F32, BF16), None, tag + "inproj_fwd",
                                     _gather_ici_stage(gather))
    xre, xim = _s5_project(u, mb["bpt"], tag + "s5_block_inputs")
    hre, him = _s5_scan_fwd(xre, xim, ar, ai, tag + "s5_scan_fwd")
    y = _s5_outputs(u, hre, him, mb["tm"], mb["cpt"], p["d"], tag + "s5_outputs")
    xn, others = _ssm_out_fwd(y, gate, x, w["w_glu"], p["b_glu"], w["w_out"], tag + "out_fwd", _gather_d2d_stage(by_chip))
    return xn, (x, u, gate, y, hre, him, mb, ar, ai, mats_vjp), _gathered(by_chip, others)


def _ssm_layer_bwd(i, dxn, saved, p, w, scatter=None):
    tag = "l%d_" % i
    x, u, gate, y, hre, him, mb, ar, ai, mats_vjp = saved
    halves, d2d_stage = _scatter_d2d_stage(scatter)
    (dy, dgate, dw_glu, db_glu, dw_out), gots = _ssm_out_bwd(dxn, y, gate, w["w_glu"], p["b_glu"], w["w_out"], tag + "out_bwd",
                                                             d2d_stage)
    dhre, dhim = _s5_project(dy, mb["cp"], tag + "s5_state_grads")
    dxre, dxim, dar, dai = _s5_scan_bwd(dhre, dhim, hre, him, ar, ai, tag + "s5_scan_bwd")
    (du, dk, dcpt, dbpt, dd), parts = _s5_backward(dy, u, hre, him, dxre, dxim, mb["tmt"], mb["bp"], p["d"],
                                                   tag + "s5_backward", _scatter_ici_stage(scatter, halves, gots, tag))
    dk = dk.reshape(SSM_GROUPS, S5_BLOCK, SSM_GROUP, SSM_GROUP)
    da_re, da_im, dlog_step, db_re, db_im, dc_re, dc_im = mats_vjp((dk, dcpt, dbpt, dar, dai))
    dx, dw_in, dnorm = _inproj_bwd(x, p["norm"], w["w_in"], [du, dgate], dxn, tag + "inproj_bwd")
    grads = dict(norm=dnorm.reshape(D_MODEL), w_in=dw_in, a_re=da_re, a_im=da_im, log_step=dlog_step, b_re=db_re,
                 b_im=db_im, c_re=dc_re, c_im=dc_im, d=dd.reshape(D_MODEL), w_glu=dw_glu, b_glu=db_glu.reshape(D_MODEL),
                 w_out=dw_out)
    return dx, grads, parts


def _attn_layer_fwd(i, x, p, w, rope, gather=None):
    tag = "l%d_" % i
    (q, k, v, gate), _ = _inproj_fwd(x, p["norm"], w["w_in"], ATTN_SPLITS, (BF16, BF16, BF16, BF16), rope, tag + "inproj_fwd")
    o, by_chip = _attn_fwd(q, k, v, p["sinks"], tag + "attn_fwd", _gather_ici_stage(gather))
    xn, others = _attn_out_fwd(o, gate, x, w["w_out"], tag + "out_fwd", _gather_d2d_stage(by_chip))
    return xn, (x, q, k, v, gate, o), _gathered(by_chip, others)


def _attn_layer_bwd(i, dxn, saved, p, w, rope, scatter=None):
    tag = "l%d_" % i
    x, q, k, v, gate, o = saved
    halves, d2d_stage = _scatter_d2d_stage(scatter)
    (do, dgate, dw_out), gots = _attn_out_bwd(dxn, o, gate, w["w_out"], tag + "out_bwd", d2d_stage)
    (dq, dk, dv, dsinks), parts = _attn_bwd(q, k, v, o, do, p["sinks"], rope, tag + "attn_bwd",
                                            _scatter_ici_stage(scatter, halves, gots, tag))
    dx, dw_in, dnorm = _inproj_bwd(x, p["norm"], w["w_in"], [dq, dk, dv, dgate], dxn, tag + "inproj_bwd")
    grads = dict(norm=dnorm.reshape(D_MODEL), w_in=dw_in, sinks=dsinks.reshape(N_Q_HEADS, ATTN_BLOCK).sum(axis=1), w_out=dw_out)
    return dx, grads, parts


def _local_step(x, target, small, big, carried=()):
    rope = _rope_tables(x.shape[0])
    carried = {h: rest for h, *rest in carried}
    big = list(big)
    saved = []
    for i in range(4):
        gather = carried[i][0] if i in carried else None
        if i % 2 == 0:
            x, s, gathered = _ssm_layer_fwd(i, x, small[i], big[i], gather)
        else:
            x, s, gathered = _attn_layer_fwd(i, x, small[i], big[i], rope, gather)
        if gathered is not None:
            for layer, matrices in carried[i][1](gathered).items():
                big[layer] = matrices
        saved.append(s)
    loss, dx, dfinal = _loss_head(x, small[4]["norm"], target, "loss_head")
    grads = [None] * 4 + [dict(norm=dfinal.reshape(D_MODEL))]
    parts = {}
    for i in (3, 2, 1, 0):
        scatter = carried[i][2](grads) if i in carried else None
        if i % 2 == 0:
            dx, grads[i], parts[i] = _ssm_layer_bwd(i, dx, saved[i], small[i], big[i], scatter)
        else:
            dx, grads[i], parts[i] = _attn_layer_bwd(i, dx, saved[i], small[i], big[i], rope, scatter)
    return loss[0, 0], dx, grads, parts


def _owner_major(key, g):
    if key == "w_in":
        return g.reshape(D_MODEL, N_CHIPS, N_CORES, -1).transpose(2, 1, 0, 3)
    return g.reshape(N_CHIPS, N_CORES, -1, D_MODEL).transpose(1, 0, 2, 3)


def _from_gathered(key, mine, other):
    return _in_device_order(list(mine), list(other), 1 if key == "w_in" else 0)


SMALL_ROWS = 72


def _rows_of_small(a):
    flat = a.reshape(-1)
    return jnp.pad(flat, (0, -flat.shape[0] % D_MODEL)).reshape(-1, D_MODEL)


def _stack_small(arrays):
    rows = jnp.concatenate([_rows_of_small(a) for a in arrays], axis=0)
    assert rows.shape[0] <= N_DEV * SMALL_ROWS
    return jnp.pad(rows, ((0, N_DEV * SMALL_ROWS - rows.shape[0]), (0, 0)))


def kernel(*args):
    names = ["x"]
    layer_names = []
    for i, keys in enumerate(LAYER_KEYS):
        layer_names += ["l%d_%s" % (i, k) for k in keys]
    layer_names.append("final_norm")
    names += layer_names + ["loss_target"] + ["m_" + n for n in layer_names] + ["v_" + n for n in layer_names]
    given = dict(zip(names, args))
    big_names = [n for n in layer_names if n.split("_", 1)[1] in BIG_KEYS]
    small_names = [n for n in layer_names if n not in big_names]

    offsets = {}

    def families_of(layers):
        families = {}
        for n in big_names:
            if int(n[1]) in layers:
                family = families.setdefault(given[n].shape[1], [])
                offsets[n] = sum(given[other].shape[0] for other in family)
                family.append(n)
        return list(families.values())

    first, carried_by = families_of((0,)), {0: families_of((1,)), 1: families_of((2, 3))}
    stack = lambda pre, family: jnp.concatenate([given[pre + n] for n in family], axis=0)
    rows_of = lambda a, n: a[..., offsets[n]:offsets[n] + given[n].shape[0], :]
    local = lambda families: [stack("", family).astype(BF16) for family in families]

    def assemble(families, gathered):
        big = {}
        for family, (mine, other) in zip(families, gathered):
            for n in family:
                matrices = big.setdefault(int(n[1]), {})
                matrices[n.split("_", 1)[1]] = _from_gathered(n.split("_", 1)[1], rows_of(mine, n), rows_of(other, n))
        return big

    def sends_of(families, grads):
        return [jnp.concatenate([_owner_major(n.split("_", 1)[1], grads[int(n[1])][n.split("_", 1)[1]]) for n in family], axis=2)
                for family in families]

    small = [dict() for _ in range(5)]
    for n in small_names:
        if n == "final_norm":
            small[4]["norm"] = given[n]
        else:
            small[int(n[1])][n.split("_", 1)[1]] = given[n]

    big = [assemble(first, _all_gather(local(first), "gather_first_weights"))[0], None, None, None]
    carried = [(h, local(families), functools.partial(assemble, families),
                lambda grads, families=families: (sends_of(families, grads), [BF16] * len(families)))
               for h, families in carried_by.items()]
    loss, dx, grads, carried_parts = _local_step(given["x"][0], given["loss_target"][0], small, big, carried)
    loss = lax.psum(loss, ("x", "y", "c"))

    def grad_of(n):
        return grads[4]["norm"] if n == "final_norm" else grads[int(n[1])][n.split("_", 1)[1]]

    flat = lambda f: _stack_small([f(n) for n in small_names])
    sends = sends_of(first, grads) + [flat(grad_of).reshape(N_CORES, N_CHIPS, SMALL_ROWS, D_MODEL)]
    parts = _reduce_scatter(sends, [BF16] * len(first) + [F32], "scatter_grads")

    outs = {}
    tags = ("grad_", "delta_", "new_m_", "new_v_")
    all_families = first + carried_by[0] + carried_by[1]
    all_parts = list(parts[:-1]) + list(carried_parts[0]) + list(carried_parts[1])
    for i, (family, part) in enumerate(zip(all_families, all_parts)):
        results = _adamw(part, stack("", family), stack("m_", family), stack("v_", family), "adamw_matrices%d" % i)
        for tag, a in zip(tags, results):
            for n in family:
                outs[tag + n] = rows_of(a, n)

    my_slice = _sum_parts([parts[-1][s] for s in range(N_CHIPS)], F32, "sum_small_grads")
    mine, other = _all_gather([my_slice], "gather_small_grads")[0]
    first = lax.axis_index("c") == 0
    g_all = jnp.concatenate([jnp.where(first, mine, other), jnp.where(first, other, mine)], axis=0)
    g_all = g_all.reshape(1, N_DEV * SMALL_ROWS, D_MODEL)
    results = _adamw(g_all, flat(lambda n: given[n]), flat(lambda n: given["m_" + n]), flat(lambda n: given["v_" + n]),
                     "adamw_small")
    for tag, a in zip(tags, results):
        at = 0
        for n in small_names:
            rows = -(-given[n].size // D_MODEL)
            outs[tag + n] = a[at:at + rows].reshape(-1)[:given[n].size].reshape(given[n].shape)
            at += rows
    result = [loss, dx[None]]
    for tag in ("grad_", "delta_", "new_m_", "new_v_"):
        result += [outs[tag + n] for n in layer_names]
    return tuple(result)
```

```python
import functools
import math

import jax
import jax.numpy as jnp
from jax import lax
from jax.experimental import pallas as pl
from jax.experimental.pallas import tpu as pltpu

F32 = jnp.float32
BF16 = jnp.bfloat16

D_MODEL = 1024
SSM_GROUP = 16
SSM_GROUPS = D_MODEL // SSM_GROUP
SSM_STATE = 64
S5_BLOCK = 16
S5_LANES = S5_BLOCK * SSM_GROUP
HEAD_DIM = 64
N_Q_HEADS = 16
N_KV_HEADS = 2
GQA = N_Q_HEADS // N_KV_HEADS
Q_DIM = N_Q_HEADS * HEAD_DIM
KV_DIM = N_KV_HEADS * HEAD_DIM
ATTN_BLOCK = 128
ROPE_THETA = 10000.0
NORM_EPS = 1e-5
NEG_INF = -1e30
ATTN_SCALE = HEAD_DIM ** -0.5
N_DEV = 8

ADAM_LR = 0.001
ADAM_B1 = 0.9
ADAM_B2 = 0.999
ADAM_EPS = 1e-08
ADAM_WD = 0.01
ADAM_STEP = 10

VMEM_LIMIT = 56 * 1024 * 1024
ROWS_FWD = 512
ROWS_BWD = 512

NT = (((1,), (1,)), ((), ()))
TN = (((0,), (0,)), ((), ()))


def _params(n_grid):
    return pltpu.CompilerParams(dimension_semantics=("arbitrary",) * n_grid, vmem_limit_bytes=VMEM_LIMIT)


def _dot(a, b):
    return jnp.dot(a, b, preferred_element_type=F32)


def _dot_nt(a, b):
    return lax.dot_general(a, b, NT, preferred_element_type=F32)


def _dot_tn(a, b):
    return lax.dot_general(a, b, TN, preferred_element_type=F32)


def _sigmoid(x):
    return 1.0 / (1.0 + jnp.exp(-x))


_GELU_K = math.sqrt(2.0 / math.pi)


def _gelu(x):
    return x * (0.5 * (1.0 + jnp.tanh(_GELU_K * (x + 0.044715 * (x * x * x)))))


def _gelu_grad(x):
    t = jnp.tanh(_GELU_K * (x + 0.044715 * (x * x * x)))
    return 0.5 * (1.0 + t) + 0.5 * x * (1.0 - t * t) * (_GELU_K * (1.0 + 3.0 * 0.044715 * (x * x)))


def _row_spec(rows, cols):
    return pl.BlockSpec((rows, cols), lambda i: (i, 0))


def _const_spec(shape):
    zeros = (0,) * len(shape)
    return pl.BlockSpec(shape, lambda i: zeros, pipeline_mode=pl.Buffered(1))


def _rope_apply(t, cos, sin_signed, sign):
    lane = lax.broadcasted_iota(jnp.int32, (1, 128), 1)
    first_half = (lane % HEAD_DIM) < (HEAD_DIM // 2)
    out = []
    for j in range(t.shape[1] // 128):
        tj = t[:, 128 * j:128 * (j + 1)]
        partner = jnp.where(first_half, pltpu.roll(tj, 128 - HEAD_DIM // 2, 1), pltpu.roll(tj, HEAD_DIM // 2, 1))
        out.append(tj * cos + sign * (partner * sin_signed))
    return out[0] if len(out) == 1 else jnp.concatenate(out, axis=1)


def _inproj_fwd(x, norm, w, splits, dtypes, rope, name, exchange=None):
    t = x.shape[0]
    n = w.shape[1]
    rows = ROWS_FWD

    def body(*refs):
        if rope is None:
            x_ref, n_ref, w_ref = refs[:3]
            outs = refs[3:]
        else:
            x_ref, n_ref, w_ref, cos_ref, sin_ref = refs[:5]
            outs = refs[5:]
        xv = x_ref[...]
        rstd = lax.rsqrt(jnp.mean(xv * xv, axis=-1, keepdims=True) + NORM_EPS)
        h = (xv * rstd) * n_ref[...]
        proj = _dot(h.astype(BF16), w_ref[...])
        off = 0
        for i, width in enumerate(splits):
            piece = proj[:, off:off + width]
            if rope is not None and i < 2:
                piece = _rope_apply(piece, cos_ref[...], sin_ref[...], 1.0)
            outs[i][...] = piece.astype(dtypes[i])
            off += width

    in_specs = [_row_spec(rows, D_MODEL), _const_spec((1, D_MODEL)), _const_spec((D_MODEL, n))]
    args = [x, norm.reshape(1, D_MODEL), w]
    if rope is not None:
        in_specs += [_row_spec(rows, 128), _row_spec(rows, 128)]
        args += list(rope)
    return _call_hosting(
        exchange, body, args, name=name, steps=t // rows, in_specs=in_specs,
        out_specs=[_row_spec(rows, width) for width in splits],
        out_shape=[jax.ShapeDtypeStruct((t, width), dtype) for width, dtype in zip(splits, dtypes)])


def _inproj_bwd(x, norm, w, dpieces, dxn, name):
    t = x.shape[0]
    n = w.shape[1]
    rows = ROWS_BWD
    widths = [p.shape[1] for p in dpieces]
    k = len(dpieces)

    def body(*refs):
        x_ref, n_ref, w_ref, dxn_ref = refs[:4]
        d_refs = refs[4:4 + k]
        dx_ref, dw_ref, dn_ref = refs[4 + k:]
        @pl.when(pl.program_id(0) == 0)
        def _():
            dw_ref[...] = jnp.zeros_like(dw_ref)
            dn_ref[...] = jnp.zeros_like(dn_ref)

        xv = x_ref[...]
        rstd = lax.rsqrt(jnp.mean(xv * xv, axis=-1, keepdims=True) + NORM_EPS)
        xhat = xv * rstd
        h = xhat * n_ref[...]
        dproj = [r[...].astype(BF16) for r in d_refs]
        dproj = dproj[0] if k == 1 else jnp.concatenate(dproj, axis=1)
        dh = _dot_nt(dproj, w_ref[...])
        dw_ref[...] += _dot_tn(h.astype(BF16), dproj)
        dn_ref[...] += jnp.sum(dh * xhat, axis=0, keepdims=True)
        dxhat = dh * n_ref[...]
        dx_ref[...] = rstd * (dxhat - xhat * jnp.mean(dxhat * xhat, axis=-1, keepdims=True)) + dxn_ref[...]

    return _call_hosting(
        None, body, (x, norm.reshape(1, D_MODEL), w, dxn, *dpieces), name=name, steps=t // rows,
        in_specs=[_row_spec(rows, D_MODEL), _const_spec((1, D_MODEL)), _const_spec((D_MODEL, n)),
                  _row_spec(rows, D_MODEL)] + [_row_spec(rows, width) for width in widths],
        out_specs=[_row_spec(rows, D_MODEL), _const_spec((D_MODEL, n)), _const_spec((1, D_MODEL))],
        out_shape=[jax.ShapeDtypeStruct((t, D_MODEL), F32), jax.ShapeDtypeStruct((D_MODEL, n), F32),
                   jax.ShapeDtypeStruct((1, D_MODEL), F32)])[0]


def _s5_matrices(a_re, a_im, log_step, b_re, b_im, c_re, c_im):
    r = S5_BLOCK
    step = jnp.exp(log_step)[:, None]
    lr, li = a_re * step, a_im * step
    k = jnp.arange(r + 1, dtype=F32)
    mag = jnp.exp(lr[:, None, :] * k[:, None])
    pr = mag * jnp.cos(li[:, None, :] * k[:, None])
    pi = mag * jnp.sin(li[:, None, :] * k[:, None])
    nr, ni = pr[:, 1] - 1.0, pi[:, 1]
    den = a_re * a_re + a_im * a_im
    qr, qi = (nr * a_re + ni * a_im) / den, (ni * a_re - nr * a_im) / den
    bbr = qr[..., None] * b_re - qi[..., None] * b_im
    bbi = qr[..., None] * b_im + qi[..., None] * b_re
    wr = c_re[:, None] * pr[:, :, None, :] - c_im[:, None] * pi[:, :, None, :]
    wi = c_re[:, None] * pi[:, :, None, :] + c_im[:, None] * pr[:, :, None, :]
    w = jnp.concatenate([wr, -wi], axis=-1)
    bb = jnp.concatenate([bbr, bbi], axis=1)
    kern = jnp.einsum("gxp,gpi->gxi", w[:, :r].reshape(SSM_GROUPS, S5_LANES, 2 * SSM_STATE), bb,
                      precision=lax.Precision.HIGHEST).reshape(SSM_GROUPS, r, SSM_GROUP, SSM_GROUP)
    cpt = w[:, 1:].reshape(SSM_GROUPS, S5_LANES, 2 * SSM_STATE)
    prs = jnp.swapaxes(pr[:, r - 1::-1][:, :r], 1, 2)[..., None]
    pis = jnp.swapaxes(pi[:, r - 1::-1][:, :r], 1, 2)[..., None]
    bp_re = prs * bbr[:, :, None, :] - pis * bbi[:, :, None, :]
    bp_im = prs * bbi[:, :, None, :] + pis * bbr[:, :, None, :]
    bpt = jnp.concatenate([bp_re, bp_im], axis=1).reshape(SSM_GROUPS, 2 * SSM_STATE, S5_LANES)
    ar = pr[:, r].reshape(1, SSM_GROUPS * SSM_STATE)
    ai = pi[:, r].reshape(1, SSM_GROUPS * SSM_STATE)
    return kern, cpt, bpt, ar, ai


def _s5_toeplitz(kern):
    r = S5_BLOCK
    kern_t = jnp.swapaxes(kern.reshape(SSM_GROUPS, S5_LANES, SSM_GROUP), 1, 2)
    rows = [jnp.pad(kern_t[:, :, :S5_LANES - SSM_GROUP * s], ((0, 0), (0, 0), (SSM_GROUP * s, 0))) for s in range(r)]
    return jnp.concatenate(rows, axis=1)


S5_OCTET = 128 // SSM_GROUP
S5_STEPS = SSM_GROUPS // S5_OCTET


def _oct_spec(t):
    return pl.BlockSpec((t, 128), lambda j: (0, j))


def _state_spec(nb):
    return pl.BlockSpec((nb, S5_OCTET * SSM_STATE), lambda j: (0, j))


def _gmat_spec(a, b):
    return pl.BlockSpec((S5_OCTET, a, b), lambda j: (j, 0, 0))


def _block_rows(ref, nb):
    by_position = jnp.swapaxes(ref[...].reshape(nb, S5_BLOCK, 128), 0, 1)
    return [by_position[r] for r in range(S5_BLOCK)]


def _store_block_rows(ref, pieces, nb):
    ref[...] = jnp.swapaxes(jnp.stack(pieces, axis=0), 0, 1).reshape(nb * S5_BLOCK, 128)


def _group_cols(pieces_t, g):
    return jnp.concatenate([p[SSM_GROUP * g:SSM_GROUP * (g + 1)] for p in pieces_t], axis=0)


def _state_cols(re_t, im_t, g):
    return jnp.concatenate([re_t[SSM_STATE * g:SSM_STATE * (g + 1)], im_t[SSM_STATE * g:SSM_STATE * (g + 1)]], axis=0)


def _s5_project(a, mat, name):
    t = a.shape[0]
    nb = t // S5_BLOCK

    def body(a_ref, m_ref, re_ref, im_ref):
        at = [p.T for p in _block_rows(a_ref, nb)]
        for pair in range(S5_OCTET // 2):
            xs = [_dot(m_ref[2 * pair + k], _group_cols(at, 2 * pair + k).astype(BF16)) for k in (0, 1)]
            lanes = slice(128 * pair, 128 * (pair + 1))
            re_ref[:, lanes] = jnp.concatenate([xs[0][:SSM_STATE], xs[1][:SSM_STATE]], axis=0).T
            im_ref[:, lanes] = jnp.concatenate([xs[0][SSM_STATE:], xs[1][SSM_STATE:]], axis=0).T

    return pl.pallas_call(
        body, name=name, grid=(S5_STEPS,),
        in_specs=[_oct_spec(t), _gmat_spec(2 * SSM_STATE, S5_LANES)],
        out_specs=[_state_spec(nb), _state_spec(nb)],
        out_shape=[jax.ShapeDtypeStruct((nb, SSM_GROUPS * SSM_STATE), F32)] * 2,
        compiler_params=_params(1),
    )(a, mat)


_SCAN_LANES = 2048


def _s5_scan_fwd(xre, xim, ar, ai, name):
    nb = xre.shape[0]
    col = pl.BlockSpec((nb, _SCAN_LANES), lambda j: (0, j))
    par = pl.BlockSpec((1, _SCAN_LANES), lambda j: (0, j))

    def body(xre_ref, xim_ref, ar_ref, ai_ref, hre_ref, him_ref):
        a_r, a_i = ar_ref[...], ai_ref[...]

        def step(b, carry):
            hr, hi = carry
            hre_ref[pl.ds(b, 1), :] = hr
            him_ref[pl.ds(b, 1), :] = hi
            xr, xi = xre_ref[pl.ds(b, 1), :], xim_ref[pl.ds(b, 1), :]
            return a_r * hr - a_i * hi + xr, a_r * hi + a_i * hr + xi

        zero = jnp.zeros((1, _SCAN_LANES), F32)
        lax.fori_loop(0, nb, step, (zero, zero))

    return pl.pallas_call(
        body, name=name, grid=(xre.shape[1] // _SCAN_LANES,),
        in_specs=[col, col, par, par], out_specs=[col, col],
        out_shape=[jax.ShapeDtypeStruct(xre.shape, F32)] * 2,
        compiler_params=_params(1),
    )(xre, xim, ar, ai)


def _s5_scan_bwd(dhre, dhim, hre, him, ar, ai, name):
    nb = dhre.shape[0]
    col = pl.BlockSpec((nb, _SCAN_LANES), lambda j: (0, j))
    par = pl.BlockSpec((1, _SCAN_LANES), lambda j: (0, j))

    def body(dhre_ref, dhim_ref, hre_ref, him_ref, ar_ref, ai_ref, dxre_ref, dxim_ref, dar_ref, dai_ref):
        a_r, a_i = ar_ref[...], ai_ref[...]

        def step(s, carry):
            gr, gi, dar, dai = carry
            b = nb - 1 - s
            dxre_ref[pl.ds(b, 1), :] = gr
            dxim_ref[pl.ds(b, 1), :] = gi
            hr, hi = hre_ref[pl.ds(b, 1), :], him_ref[pl.ds(b, 1), :]
            dar = dar + (hr * gr + hi * gi)
            dai = dai + (hr * gi - hi * gr)
            dr, di = dhre_ref[pl.ds(b, 1), :], dhim_ref[pl.ds(b, 1), :]
            return dr + (a_r * gr + a_i * gi), di + (a_r * gi - a_i * gr), dar, dai

        zero = jnp.zeros((1, _SCAN_LANES), F32)
        _, _, dar, dai = lax.fori_loop(0, nb, step, (zero, zero, zero, zero))
        dar_ref[...] = dar
        dai_ref[...] = dai

    return pl.pallas_call(
        body, name=name, grid=(dhre.shape[1] // _SCAN_LANES,),
        in_specs=[col, col, col, col, par, par], out_specs=[col, col, par, par],
        out_shape=[jax.ShapeDtypeStruct(dhre.shape, F32)] * 2 + [jax.ShapeDtypeStruct(ar.shape, F32)] * 2,
        compiler_params=_params(1),
    )(dhre, dhim, hre, him, ar, ai)


def _s5_outputs(u, hre, him, tm, cpt, d, name):
    t = u.shape[0]
    nb = t // S5_BLOCK

    def body(u_ref, hre_ref, him_ref, tm_ref, cpt_ref, d_ref, y_ref):
        u_rows = _block_rows(u_ref, nb)
        ut = [p.T for p in u_rows]
        hre_t, him_t = hre_ref[...].T, him_ref[...].T
        yts = []
        for g in range(S5_OCTET):
            yts.append(_dot(tm_ref[g], _group_cols(ut, g).astype(BF16))
                       + _dot(cpt_ref[g], _state_cols(hre_t, him_t, g).astype(BF16)))
        y_rows = []
        for r in range(S5_BLOCK):
            rows = jnp.concatenate([yt[SSM_GROUP * r:SSM_GROUP * (r + 1)] for yt in yts], axis=0)
            y_rows.append(rows.T + d_ref[...] * u_rows[r])
        _store_block_rows(y_ref, y_rows, nb)

    return pl.pallas_call(
        body, name=name, grid=(S5_STEPS,),
        in_specs=[_oct_spec(t), _state_spec(nb), _state_spec(nb), _gmat_spec(S5_LANES, S5_LANES),
                  _gmat_spec(S5_LANES, 2 * SSM_STATE), _oct_spec(1)],
        out_specs=_oct_spec(t),
        out_shape=jax.ShapeDtypeStruct(u.shape, F32),
        compiler_params=_params(1),
    )(u, hre, him, tm, cpt, d.reshape(1, D_MODEL))


def _s5_backward(dy, u, hre, him, dxre, dxim, tmt, bp, d, name, exchange=None):
    t = u.shape[0]
    nb = t // S5_BLOCK

    def body(dy_ref, u_ref, hre_ref, him_ref, dxre_ref, dxim_ref, tmt_ref, bp_ref, d_ref,
             du_ref, dk_ref, dcpt_ref, dbpt_ref, dd_ref, dtm_scratch):
        dy_rows, u_rows = _block_rows(dy_ref, nb), _block_rows(u_ref, nb)
        dyt, ut = [p.T for p in dy_rows], [p.T for p in u_rows]
        hre_t, him_t = hre_ref[...].T, him_ref[...].T
        dxre_t, dxim_t = dxre_ref[...].T, dxim_ref[...].T
        duts = []
        for g in range(S5_OCTET):
            dyg, ug = _group_cols(dyt, g).astype(BF16), _group_cols(ut, g).astype(BF16)
            hg = _state_cols(hre_t, him_t, g).astype(BF16)
            dxg = _state_cols(dxre_t, dxim_t, g).astype(BF16)
            duts.append(_dot(tmt_ref[g], dyg) + _dot(bp_ref[g], dxg))
            dtm_scratch[...] = _dot_nt(dyg, ug)
            dk = dtm_scratch[:, :SSM_GROUP]
            for s in range(1, S5_BLOCK):
                below = dtm_scratch[SSM_GROUP * s:, SSM_GROUP * s:SSM_GROUP * (s + 1)]
                dk = dk + jnp.concatenate([below, jnp.zeros((SSM_GROUP * s, SSM_GROUP), F32)], axis=0)
            dk_ref[g] = dk
            dcpt_ref[g] = _dot_nt(dyg, hg)
            dbpt_ref[g] = _dot_nt(dxg, ug)
        dd = jnp.zeros((1, 128), F32)
        du_rows = []
        for r in range(S5_BLOCK):
            rows = jnp.concatenate([dut[SSM_GROUP * r:SSM_GROUP * (r + 1)] for dut in duts], axis=0)
            du_rows.append(rows.T + d_ref[...] * dy_rows[r])
            dd = dd + jnp.sum(dy_rows[r] * u_rows[r], axis=0, keepdims=True)
        _store_block_rows(du_ref, du_rows, nb)
        dd_ref[...] = dd

    return _call_hosting(
        exchange, body, (dy, u, hre, him, dxre, dxim, tmt, bp, d.reshape(1, D_MODEL)), name=name, steps=S5_STEPS,
        in_specs=[_oct_spec(t), _oct_spec(t), _state_spec(nb), _state_spec(nb), _state_spec(nb), _state_spec(nb),
                  _gmat_spec(S5_LANES, S5_LANES), _gmat_spec(S5_LANES, 2 * SSM_STATE), _oct_spec(1)],
        out_specs=[_oct_spec(t), _gmat_spec(S5_LANES, SSM_GROUP), _gmat_spec(S5_LANES, 2 * SSM_STATE),
                   _gmat_spec(2 * SSM_STATE, S5_LANES), _oct_spec(1)],
        out_shape=[jax.ShapeDtypeStruct(u.shape, F32),
                   jax.ShapeDtypeStruct((SSM_GROUPS, S5_LANES, SSM_GROUP), F32),
                   jax.ShapeDtypeStruct((SSM_GROUPS, S5_LANES, 2 * SSM_STATE), F32),
                   jax.ShapeDtypeStruct((SSM_GROUPS, 2 * SSM_STATE, S5_LANES), F32),
                   jax.ShapeDtypeStruct((1, D_MODEL), F32)],
        scratch_shapes=[pltpu.VMEM((S5_LANES, S5_LANES), F32)])


def _ssm_out_fwd(y, gate, x, w_glu, b_glu, w_out, name, exchange=None):
    t = x.shape[0]
    rows = ROWS_FWD

    def body(y_ref, g_ref, x_ref, wg_ref, bg_ref, wo_ref, o_ref):
        z0 = _gelu(y_ref[...])
        s = _dot(z0.astype(BF16), wg_ref[...]) + bg_ref[...]
        gate_v = g_ref[...].astype(F32)
        a = (z0 * _sigmoid(s)) * (gate_v * _sigmoid(gate_v))
        o_ref[...] = x_ref[...] + _dot(a.astype(BF16), wo_ref[...])

    (xn,), got = _call_hosting(
        exchange, body, (y, gate, x, w_glu, b_glu.reshape(1, D_MODEL), w_out), name=name, steps=t // rows,
        in_specs=[_row_spec(rows, D_MODEL)] * 3 + [_const_spec((D_MODEL, D_MODEL)), _const_spec((1, D_MODEL)),
                                                   _const_spec((D_MODEL, D_MODEL))],
        out_specs=[_row_spec(rows, D_MODEL)],
        out_shape=[jax.ShapeDtypeStruct((t, D_MODEL), F32)])
    return xn, got


def _ssm_out_bwd(dxn, y, gate, w_glu, b_glu, w_out, name, exchange=None):
    t = y.shape[0]
    rows = ROWS_BWD

    def body(dxn_ref, y_ref, g_ref, wg_ref, bg_ref, wo_ref, dy_ref, dg_ref, dwg_ref, dbg_ref, dwo_ref):
        @pl.when(pl.program_id(0) == 0)
        def _():
            dwo_ref[...] = jnp.zeros_like(dwo_ref)
            dwg_ref[...] = jnp.zeros_like(dwg_ref)
            dbg_ref[...] = jnp.zeros_like(dbg_ref)

        yv = y_ref[...]
        z0 = _gelu(yv)
        z0b = z0.astype(BF16)
        sg = _sigmoid(_dot(z0b, wg_ref[...]) + bg_ref[...])
        z = z0 * sg
        gate_v = g_ref[...].astype(F32)
        sgg = _sigmoid(gate_v)
        silu = gate_v * sgg
        dob = dxn_ref[...].astype(BF16)
        da = _dot_nt(dob, wo_ref[...])
        dwo_ref[...] += _dot_tn((z * silu).astype(BF16), dob)
        dz = da * silu
        dg_ref[...] = (da * z * (sgg * (1.0 + gate_v * (1.0 - sgg)))).astype(BF16)
        ds = dz * z0 * (sg * (1.0 - sg))
        dsb = ds.astype(BF16)
        dz0 = dz * sg + _dot_nt(dsb, wg_ref[...])
        dwg_ref[...] += _dot_tn(z0b, dsb)
        dbg_ref[...] += jnp.sum(ds, axis=0, keepdims=True)
        dy_ref[...] = dz0 * _gelu_grad(yv)

    sq = _const_spec((D_MODEL, D_MODEL))
    vec = _const_spec((1, D_MODEL))
    return _call_hosting(
        exchange, body, (dxn, y, gate, w_glu, b_glu.reshape(1, D_MODEL), w_out), name=name, steps=t // rows,
        in_specs=[_row_spec(rows, D_MODEL)] * 3 + [sq, vec, sq],
        out_specs=[_row_spec(rows, D_MODEL), _row_spec(rows, D_MODEL), sq, vec, sq],
        out_shape=[jax.ShapeDtypeStruct((t, D_MODEL), F32), jax.ShapeDtypeStruct((t, D_MODEL), BF16),
                   jax.ShapeDtypeStruct((D_MODEL, D_MODEL), F32), jax.ShapeDtypeStruct((1, D_MODEL), F32),
                   jax.ShapeDtypeStruct((D_MODEL, D_MODEL), F32)])


KV_LANES = GQA * ATTN_BLOCK


def _attn_bias(block_is_first):
    kj = lax.broadcasted_iota(jnp.int32, (2 * ATTN_BLOCK, ATTN_BLOCK), 0)
    qi = lax.broadcasted_iota(jnp.int32, (2 * ATTN_BLOCK, ATTN_BLOCK), 1)
    dist = qi + ATTN_BLOCK - kj
    valid = (dist >= 0) & (dist < ATTN_BLOCK) & (jnp.logical_not(block_is_first) | (kj >= ATTN_BLOCK))
    return jnp.tile(jnp.where(valid, 0.0, NEG_INF).astype(F32), (1, GQA))


def _head_cols(a_t, kvh):
    heads = range(kvh * GQA, (kvh + 1) * GQA)
    return jnp.concatenate([a_t[HEAD_DIM * h:HEAD_DIM * (h + 1)] for h in heads], axis=1)


def _head_rows(a_cols):
    stacked = jnp.concatenate([a_cols[:, ATTN_BLOCK * g:ATTN_BLOCK * (g + 1)] for g in range(GQA)], axis=0)
    return stacked.T


def _kv_rows(prev_ref, cur_ref, kvh):
    lanes = slice(HEAD_DIM * kvh, HEAD_DIM * (kvh + 1))
    return jnp.concatenate([prev_ref[:, lanes], cur_ref[:, lanes]], axis=0).astype(BF16)


def _kv_cols(prev_t, cur_t, kvh):
    rows = slice(HEAD_DIM * kvh, HEAD_DIM * (kvh + 1))
    return jnp.concatenate([prev_t[rows], cur_t[rows]], axis=1).astype(BF16)


def _attn_probs(kk, q_cols, sink_row, bias):
    s = _dot(kk, q_cols) * ATTN_SCALE + bias
    m = jnp.maximum(jnp.max(s, axis=0, keepdims=True), sink_row)
    p = jnp.exp(s - m)
    e_sink = jnp.exp(sink_row - m)
    inv = 1.0 / (jnp.sum(p, axis=0, keepdims=True) + e_sink)
    return p * inv, e_sink * inv


def _sink_cols(sinks):
    return jnp.repeat(sinks, ATTN_BLOCK).reshape(N_KV_HEADS, 1, KV_LANES)


def _attn_fwd(q, k, v, sinks, name, exchange=None):
    t = q.shape[0]
    nblk = t // ATTN_BLOCK

    def body(s_ref, q_ref, kc_ref, kp_ref, vc_ref, vp_ref, o_ref):
        bias = _attn_bias(pl.program_id(0) == 0)
        q_t = q_ref[...].astype(F32).T
        vp_t, vc_t = vp_ref[...].astype(F32).T, vc_ref[...].astype(F32).T
        for kvh in range(N_KV_HEADS):
            p, _ = _attn_probs(_kv_rows(kp_ref, kc_ref, kvh), _head_cols(q_t, kvh).astype(BF16), s_ref[kvh], bias)
            o_cols = _dot(_kv_cols(vp_t, vc_t, kvh), p.astype(BF16))
            o_ref[:, GQA * HEAD_DIM * kvh:GQA * HEAD_DIM * (kvh + 1)] = _head_rows(o_cols)

    cur = lambda i: (i, 0)
    prev = lambda i: (jnp.maximum(i - 1, 0), 0)
    (o,), got = _call_hosting(
        exchange, body, (_sink_cols(sinks), q, k, k, v, v), name=name, steps=nblk,
        in_specs=[_const_spec((N_KV_HEADS, 1, KV_LANES)),
                  pl.BlockSpec((ATTN_BLOCK, Q_DIM), cur),
                  pl.BlockSpec((ATTN_BLOCK, KV_DIM), cur), pl.BlockSpec((ATTN_BLOCK, KV_DIM), prev),
                  pl.BlockSpec((ATTN_BLOCK, KV_DIM), cur), pl.BlockSpec((ATTN_BLOCK, KV_DIM), prev)],
        out_specs=[pl.BlockSpec((ATTN_BLOCK, Q_DIM), cur)],
        out_shape=[jax.ShapeDtypeStruct((t, Q_DIM), F32)])
    return o, got


def _attn_bwd(q, k, v, o, do, sinks, rope, name, exchange=None):
    t = q.shape[0]
    nblk = t // ATTN_BLOCK

    def body(s_ref, q_ref, o_ref, do_ref, kp_ref, kc_ref, vp_ref, vc_ref, cosq_ref, sinq_ref, cosk_ref, sinkey_ref,
             dq_ref, dk_ref, dv_ref, ds_ref, new_k, new_v, wait_k, wait_v, dq_rot):
        n = pl.program_id(0)

        @pl.when(n == 0)
        def _():
            ds_ref[...] = jnp.zeros_like(ds_ref)
            wait_k[...] = jnp.zeros_like(wait_k)
            wait_v[...] = jnp.zeros_like(wait_v)

        @pl.when(n < nblk)
        def _():
            bias = _attn_bias(n == 0)
            q_t, o_t, do_t = q_ref[...].astype(F32).T, o_ref[...].T, do_ref[...].T
            kp_t, kc_t = kp_ref[...].astype(F32).T, kc_ref[...].astype(F32).T
            for kvh in range(N_KV_HEADS):
                q_cols = _head_cols(q_t, kvh).astype(BF16)
                do_cols = _head_cols(do_t, kvh)
                delta = jnp.sum(do_cols * _head_cols(o_t, kvh), axis=0, keepdims=True)
                do_cols = do_cols.astype(BF16)
                p, p_sink = _attn_probs(_kv_rows(kp_ref, kc_ref, kvh), q_cols, s_ref[kvh], bias)
                dp = _dot(_kv_rows(vp_ref, vc_ref, kvh), do_cols)
                ds = (p * (dp - delta) * ATTN_SCALE).astype(BF16)
                lanes = slice(GQA * HEAD_DIM * kvh, GQA * HEAD_DIM * (kvh + 1))
                dq_rot[:, lanes] = _head_rows(_dot(_kv_cols(kp_t, kc_t, kvh), ds))
                head = slice(HEAD_DIM * kvh, HEAD_DIM * (kvh + 1))
                new_k[:, head] = _dot_nt(ds, q_cols)
                new_v[:, head] = _dot_nt(p.astype(BF16), do_cols)
                ds_ref[kvh] += -(p_sink * delta)
            dq_ref[...] = _rope_apply(dq_rot[...], cosq_ref[...], sinq_ref[...], -1.0).astype(BF16)

        @pl.when(n == nblk)
        def _():
            new_k[...] = jnp.zeros_like(new_k)
            new_v[...] = jnp.zeros_like(new_v)

        dk_ref[...] = _rope_apply(wait_k[...] + new_k[:ATTN_BLOCK], cosk_ref[...], sinkey_ref[...], -1.0).astype(BF16)
        dv_ref[...] = (wait_v[...] + new_v[:ATTN_BLOCK]).astype(BF16)
        wait_k[...] = new_k[ATTN_BLOCK:]
        wait_v[...] = new_v[ATTN_BLOCK:]

    cur = lambda i: (jnp.minimum(i, nblk - 1), 0)
    prev = lambda i: (jnp.maximum(i - 1, 0), 0)
    qs = lambda f: pl.BlockSpec((ATTN_BLOCK, Q_DIM), f)
    ks = lambda f: pl.BlockSpec((ATTN_BLOCK, KV_DIM), f)
    sink_spec = _const_spec((N_KV_HEADS, 1, KV_LANES))
    return _call_hosting(
        exchange, body, (_sink_cols(sinks), q, o, do, k, k, v, v, rope[0], rope[1], rope[0], rope[1]),
        name=name, steps=nblk + 1,
        in_specs=[sink_spec, qs(cur), qs(cur), qs(cur), ks(prev), ks(cur), ks(prev), ks(cur),
                  ks(cur), ks(cur), ks(prev), ks(prev)],
        out_specs=[qs(cur), ks(prev), ks(prev), sink_spec],
        out_shape=[jax.ShapeDtypeStruct((t, Q_DIM), BF16), jax.ShapeDtypeStruct((t, KV_DIM), BF16),
                   jax.ShapeDtypeStruct((t, KV_DIM), BF16), jax.ShapeDtypeStruct((N_KV_HEADS, 1, KV_LANES), F32)],
        scratch_shapes=[pltpu.VMEM((2 * ATTN_BLOCK, KV_DIM), F32), pltpu.VMEM((2 * ATTN_BLOCK, KV_DIM), F32),
                        pltpu.VMEM((ATTN_BLOCK, KV_DIM), F32), pltpu.VMEM((ATTN_BLOCK, KV_DIM), F32),
                        pltpu.VMEM((ATTN_BLOCK, Q_DIM), F32)])


def _attn_out_fwd(o, gate, x, w_out, name, exchange=None):
    t = x.shape[0]
    rows = ROWS_FWD

    def body(o_ref, g_ref, x_ref, wo_ref, xn_ref):
        gate_v = g_ref[...].astype(F32)
        a = o_ref[...] * (gate_v * _sigmoid(gate_v))
        xn_ref[...] = x_ref[...] + _dot(a.astype(BF16), wo_ref[...])

    (xn,), got = _call_hosting(
        exchange, body, (o, gate, x, w_out), name=name, steps=t // rows,
        in_specs=[_row_spec(rows, D_MODEL)] * 3 + [_const_spec((D_MODEL, D_MODEL))],
        out_specs=[_row_spec(rows, D_MODEL)],
        out_shape=[jax.ShapeDtypeStruct((t, D_MODEL), F32)])
    return xn, got


def _attn_out_bwd(dxn, o, gate, w_out, name, exchange=None):
    t = o.shape[0]
    rows = ROWS_BWD

    def body(dxn_ref, o_ref, g_ref, wo_ref, do_ref, dg_ref, dwo_ref):
        @pl.when(pl.program_id(0) == 0)
        def _():
            dwo_ref[...] = jnp.zeros_like(dwo_ref)

        gate_v, ov = g_ref[...].astype(F32), o_ref[...]
        sgg = _sigmoid(gate_v)
        silu = gate_v * sgg
        dob = dxn_ref[...].astype(BF16)
        da = _dot_nt(dob, wo_ref[...])
        dwo_ref[...] += _dot_tn((ov * silu).astype(BF16), dob)
        do_ref[...] = da * silu
        dg_ref[...] = (da * ov * (sgg * (1.0 + gate_v * (1.0 - sgg)))).astype(BF16)

    sq = _const_spec((D_MODEL, D_MODEL))
    return _call_hosting(
        exchange, body, (dxn, o, gate, w_out), name=name, steps=t // rows,
        in_specs=[_row_spec(rows, D_MODEL)] * 3 + [sq],
        out_specs=[_row_spec(rows, D_MODEL), _row_spec(rows, D_MODEL), sq],
        out_shape=[jax.ShapeDtypeStruct((t, D_MODEL), F32), jax.ShapeDtypeStruct((t, D_MODEL), BF16),
                   jax.ShapeDtypeStruct((D_MODEL, D_MODEL), F32)])


def _loss_head(x, norm, target, name):
    t = x.shape[0]
    rows = ROWS_FWD

    def body(x_ref, n_ref, t_ref, loss_ref, dx_ref, dn_ref):
        i = pl.program_id(0)
        xv = x_ref[...]
        rstd = lax.rsqrt(jnp.mean(xv * xv, axis=-1, keepdims=True) + NORM_EPS)
        xhat = xv * rstd
        err = xhat * n_ref[...] - t_ref[...]
        part = 0.5 * jnp.sum(jnp.mean(err * err, axis=-1, keepdims=True), axis=0, keepdims=True)
        dy = err * (1.0 / D_MODEL)
        dn = jnp.sum(dy * xhat, axis=0, keepdims=True)
        dxhat = dy * n_ref[...]
        dx_ref[...] = rstd * (dxhat - xhat * jnp.mean(dxhat * xhat, axis=-1, keepdims=True))

        @pl.when(i == 0)
        def _():
            loss_ref[...] = jnp.zeros((8, 128), F32) + part
            dn_ref[...] = dn

        @pl.when(i > 0)
        def _():
            loss_ref[...] += part
            dn_ref[...] += dn

    return pl.pallas_call(
        body, name=name, grid=(t // rows,),
        in_specs=[_row_spec(rows, D_MODEL), _const_spec((1, D_MODEL)), _row_spec(rows, D_MODEL)],
        out_specs=[_const_spec((8, 128)), _row_spec(rows, D_MODEL), _const_spec((1, D_MODEL))],
        out_shape=[jax.ShapeDtypeStruct((8, 128), F32), jax.ShapeDtypeStruct((t, D_MODEL), F32),
                   jax.ShapeDtypeStruct((1, D_MODEL), F32)],
        compiler_params=_params(1),
    )(x, norm.reshape(1, D_MODEL), target)


N_CHIPS = 4
N_CORES = 2
CHIP_FLIPS = ((0, 1), (1, 0), (1, 1))
ICI_CHUNKS = 2
D2D_CHUNKS = 8


def _n_chunks(rows, dtype, most):
    unit = 16 if dtype == BF16 else 8
    return max(n for n in range(1, most + 1) if rows % n == 0 and (rows // n) % unit == 0)


def _chunks_of(arrays, most):
    out = []
    for a in arrays:
        n = _n_chunks(a.shape[-2], a.dtype, most)
        out.append((n, a.shape[-2] // n))
    return out


class _Exchange:
    def __init__(self, arrays, out_shape, scratch, copies):
        self.arrays, self.out_shape, self.scratch, self._copies = arrays, out_shape, scratch, copies

    def start(self, *refs):
        for cp in self._copies(*refs)[0]:
            cp.start()

    def wait(self, *refs):
        for wait in self._copies(*refs)[1]:
            wait()


def _chips_exchange(sends, per_dest):
    n = len(sends)
    chunking = _chunks_of(sends, ICI_CHUNKS)

    def copies(send_refs, recv_refs, sems):
        x, y, c = lax.axis_index("x"), lax.axis_index("y"), lax.axis_index("c")
        me = 2 * x + y

        def peer(k):
            fx, fy = CHIP_FLIPS[k]
            px, py = x + fx - 2 * x * fx, y + fy - 2 * y * fy
            return (px, py, c), 2 * px + py

        to_start, waits = [], []
        for a in range(n):
            send_sems, recv_sems, local_sems = sems[3 * a:3 * a + 3]
            chunks, chunk_rows = chunking[a]
            for j in range(chunks):
                part = pl.ds(j * chunk_rows, chunk_rows)
                src = lambda number: send_refs[a].at[number, part] if per_dest else send_refs[a].at[part]
                for k in range(len(CHIP_FLIPS)):
                    to, to_number = peer(k)
                    remote = lambda landing: pltpu.make_async_remote_copy(
                        src_ref=src(to_number), dst_ref=recv_refs[a].at[landing, part],
                        send_sem=send_sems.at[k, j], recv_sem=recv_sems.at[k, j],
                        device_id=to, device_id_type=pl.DeviceIdType.MESH)
                    to_start.append(remote(me))
                    waits += [remote(me).wait_send, remote(to_number).wait_recv]
                own = pltpu.make_async_copy(src(me), recv_refs[a].at[me, part], local_sems.at[j])
                to_start.append(own)
                waits.append(own.wait)
        return to_start, waits

    scratch = []
    for chunks, _ in chunking:
        scratch += [pltpu.SemaphoreType.DMA((len(CHIP_FLIPS), chunks)), pltpu.SemaphoreType.DMA((len(CHIP_FLIPS), chunks)),
                    pltpu.SemaphoreType.DMA((chunks,))]
    return _Exchange(sends, [jax.ShapeDtypeStruct((N_CHIPS,) + a.shape[-2:], a.dtype) for a in sends], scratch, copies)


def _cores_exchange(sends, per_dest):
    n = len(sends)
    chunking = _chunks_of(sends, D2D_CHUNKS)

    def copies(send_refs, got_refs, sems):
        c = lax.axis_index("c")
        sibling = (lax.axis_index("x"), lax.axis_index("y"), 1 - c)
        to_start = []
        for a in range(n):
            chunks, chunk_rows = chunking[a]
            for j in range(chunks):
                part = pl.ds(j * chunk_rows, chunk_rows)
                to_start.append(pltpu.make_async_remote_copy(
                    src_ref=send_refs[a].at[1 - c, part] if per_dest else send_refs[a].at[part],
                    dst_ref=got_refs[a].at[part], send_sem=sems[2 * a].at[j], recv_sem=sems[2 * a + 1].at[j],
                    device_id=sibling, device_id_type=pl.DeviceIdType.MESH))
        return to_start, [cp.wait for cp in to_start]

    scratch = []
    for chunks, _ in chunking:
        scratch += [pltpu.SemaphoreType.DMA((chunks,)), pltpu.SemaphoreType.DMA((chunks,))]
    return _Exchange(sends, [jax.ShapeDtypeStruct(a.shape[-2:], a.dtype) for a in sends], scratch, copies)


def _run_exchange(exchange, name):
    n = len(exchange.arrays)

    def body(*refs):
        parts = refs[:n], refs[n:2 * n], refs[2 * n:]
        exchange.start(*parts)
        exchange.wait(*parts)

    hbm = pl.BlockSpec(memory_space=pltpu.HBM)
    return pl.pallas_call(body, name=name, in_specs=[hbm] * n, out_specs=[hbm] * n, out_shape=exchange.out_shape,
                          scratch_shapes=exchange.scratch)(*exchange.arrays)


def _call_hosting(exchange, body, args, *, name, steps, in_specs, out_specs, out_shape, scratch_shapes=()):
    common = dict(name=name, grid=(steps,), compiler_params=_params(1))
    if exchange is None:
        return pl.pallas_call(body, in_specs=in_specs, out_specs=out_specs, out_shape=out_shape,
                              scratch_shapes=list(scratch_shapes), **common)(*args), None
    n_in, n_out, n_scratch, k = len(in_specs), len(out_specs), len(scratch_shapes), len(exchange.arrays)

    def hosting(*refs):
        ins, sends = refs[:n_in], refs[n_in:n_in + k]
        outs, recvs = refs[n_in + k:n_in + k + n_out], refs[n_in + k + n_out:n_in + 2 * k + n_out]
        scratch = refs[n_in + 2 * k + n_out:n_in + 2 * k + n_out + n_scratch]
        sems = refs[n_in + 2 * k + n_out + n_scratch:]
        pl.when(pl.program_id(0) == 0)(lambda: exchange.start(sends, recvs, sems))
        body(*ins, *outs, *scratch)
        pl.when(pl.program_id(0) == steps - 1)(lambda: exchange.wait(sends, recvs, sems))

    hbm = pl.BlockSpec(memory_space=pltpu.HBM)
    results = pl.pallas_call(
        hosting, in_specs=list(in_specs) + [hbm] * k, out_specs=list(out_specs) + [hbm] * k,
        out_shape=list(out_shape) + exchange.out_shape, scratch_shapes=list(scratch_shapes) + exchange.scratch, **common,
    )(*args, *exchange.arrays)
    return results[:n_out], results[n_out:]


def _exchange_chips(sends, per_dest, name):
    return _run_exchange(_chips_exchange(sends, per_dest), name)


def _swap_cores(sends, per_dest, name):
    return _run_exchange(_cores_exchange(sends, per_dest), name)


def _all_gather(arrays, name):
    by_chip = _exchange_chips(arrays, False, name + "_chips")
    others = _swap_cores([r.reshape(-1, r.shape[-1]) for r in by_chip], False, name + "_cores")
    return [(m, o.reshape(m.shape)) for m, o in zip(by_chip, others)]


def _in_device_order(mine, other, axis):
    first = lax.axis_index("c") == 0
    pieces = []
    for m, o in zip(mine, other):
        pieces += [jnp.where(first, m, o), jnp.where(first, o, m)]
    return jnp.concatenate(pieces, axis=axis)


def _sum_core(send, got, out_dtype, name):
    _, n, cols = send.shape
    rows = min(n, 256)
    while n % rows:
        rows -= 16

    def body(c_ref, keep_ref, got_ref, o_ref):
        o_ref[...] = (keep_ref[...].astype(F32) + got_ref[...].astype(F32)).astype(out_dtype)

    return pl.pallas_call(
        body, name=name, out_shape=jax.ShapeDtypeStruct((n, cols), out_dtype),
        grid_spec=pltpu.PrefetchScalarGridSpec(
            num_scalar_prefetch=1, grid=(n // rows,),
            in_specs=[pl.BlockSpec((None, rows, cols), lambda i, c: (c[0], i, 0)),
                      pl.BlockSpec((rows, cols), lambda i, c: (i, 0))],
            out_specs=pl.BlockSpec((rows, cols), lambda i, c: (i, 0))),
        compiler_params=_params(1),
    )(lax.axis_index("c").astype(jnp.int32).reshape(1), send, got)


def _sum_parts(parts, out_dtype, name):
    n, cols = parts[0].shape
    rows = min(n, 256)
    while n % rows:
        rows -= 16

    def body(*refs):
        acc = refs[0][...].astype(F32)
        for ref in refs[1:-1]:
            acc = acc + ref[...].astype(F32)
        refs[-1][...] = acc.astype(out_dtype)

    return pl.pallas_call(
        body, name=name, grid=(n // rows,),
        in_specs=[_row_spec(rows, cols)] * len(parts),
        out_specs=_row_spec(rows, cols),
        out_shape=jax.ShapeDtypeStruct((n, cols), out_dtype),
        compiler_params=_params(1),
    )(*parts)


def _reduce_scatter(sends, wire_dtypes, name):
    halves = [s.reshape(N_CORES, N_CHIPS * s.shape[2], s.shape[3]) for s in sends]
    gots = _swap_cores(halves, True, name + "_cores")
    sums = [_sum_core(h, g, dt, "%s_core_sum%d" % (name, i)).reshape((N_CHIPS,) + s.shape[2:])
            for i, (h, g, dt, s) in enumerate(zip(halves, gots, wire_dtypes, sends))]
    return _exchange_chips(sums, True, name + "_chips")


def _adamw(parts, w, m, v, name):
    n, cols = w.shape
    k = parts.shape[0]
    rows = min(n, 256)
    while n % rows:
        rows -= 8
    c1 = 1.0 - ADAM_B1 ** ADAM_STEP
    c2 = 1.0 - ADAM_B2 ** ADAM_STEP

    def body(p_ref, w_ref, m_ref, v_ref, g_ref, d_ref, nm_ref, nv_ref):
        g = p_ref[0].astype(F32)
        for s in range(1, k):
            g = g + p_ref[s].astype(F32)
        nm = ADAM_B1 * m_ref[...] + (1.0 - ADAM_B1) * g
        nv = ADAM_B2 * v_ref[...] + (1.0 - ADAM_B2) * (g * g)
        g_ref[...] = g
        nm_ref[...] = nm
        nv_ref[...] = nv
        d_ref[...] = -ADAM_LR * ((nm / c1) / (jnp.sqrt(nv / c2) + ADAM_EPS) + ADAM_WD * w_ref[...])

    blk = _row_spec(rows, cols)
    return pl.pallas_call(
        body, name=name, grid=(n // rows,),
        in_specs=[pl.BlockSpec((k, rows, cols), lambda i: (0, i, 0)), blk, blk, blk],
        out_specs=[blk] * 4,
        out_shape=[jax.ShapeDtypeStruct((n, cols), F32)] * 4,
        compiler_params=_params(1),
    )(parts, w, m, v)


SSM_KEYS = ("norm", "w_in", "a_re", "a_im", "log_step", "b_re", "b_im", "c_re", "c_im", "d", "w_glu", "b_glu", "w_out")
ATTN_KEYS = ("norm", "w_in", "sinks", "w_out")
LAYER_KEYS = (SSM_KEYS, ATTN_KEYS, SSM_KEYS, ATTN_KEYS)
BIG_KEYS = ("w_in", "w_glu", "w_out")
ATTN_SPLITS = (Q_DIM, KV_DIM, KV_DIM, D_MODEL)


def _rope_tables(t):
    pos = jnp.arange(t, dtype=F32)
    inv_freq = ROPE_THETA ** (-jnp.arange(0, HEAD_DIM, 2, dtype=F32) / HEAD_DIM)
    ang = pos[:, None] * inv_freq[None, :]
    cos, sin = jnp.cos(ang), jnp.sin(ang)
    return jnp.tile(jnp.concatenate([cos, cos], axis=1), (1, 2)), jnp.tile(jnp.concatenate([-sin, sin], axis=1), (1, 2))


def _gather_ici_stage(gather):
    return None if gather is None else _chips_exchange(gather, False)


def _gather_d2d_stage(by_chip):
    return None if by_chip is None else _cores_exchange([r.reshape(-1, r.shape[-1]) for r in by_chip], False)


def _gathered(by_chip, others):
    return None if by_chip is None else [(m, other.reshape(m.shape)) for m, other in zip(by_chip, others)]


def _scatter_d2d_stage(scatter):
    if scatter is None:
        return None, None
    halves = [s.reshape(N_CORES, N_CHIPS * s.shape[2], s.shape[3]) for s in scatter[0]]
    return halves, _cores_exchange(halves, True)


def _scatter_ici_stage(scatter, halves, gots, tag):
    if scatter is None:
        return None
    sums = [_sum_core(h, g, dt, "%sscatter_core_sum%d" % (tag, j)).reshape((N_CHIPS,) + s.shape[2:])
            for j, (h, g, dt, s) in enumerate(zip(halves, gots, scatter[1], scatter[0]))]
    return _chips_exchange(sums, True)


def _ssm_layer_fwd(i, x, p, w, gather=None):
    tag = "l%d_" % i
    mats, mats_vjp = jax.vjp(_s5_matrices, p["a_re"], p["a_im"], p["log_step"], p["b_re"], p["b_im"], p["c_re"], p["c_im"])
    kern, cpt, bpt, ar, ai = mats
    tmt = _s5_toeplitz(kern.astype(BF16))
    mb = dict(tm=jnp.swapaxes(tmt, 1, 2), tmt=tmt, cpt=cpt.astype(BF16),
              cp=jnp.swapaxes(cpt, 1, 2).astype(BF16), bpt=bpt.astype(BF16), bp=jnp.swapaxes(bpt, 1, 2).astype(BF16))
    (u, gate), by_chip = _inproj_fwd(x, p["norm"], w["w_in"], (D_MODEL, D_MODEL), (F32, BF16), None, tag + "inproj_fwd",
                                     _gather_ici_stage(gather))
    xre, xim = _s5_project(u, mb["bpt"], tag + "s5_block_inputs")
    hre, him = _s5_scan_fwd(xre, xim, ar, ai, tag + "s5_scan_fwd")
    y = _s5_outputs(u, hre, him, mb["tm"], mb["cpt"], p["d"], tag + "s5_outputs")
    xn, others = _ssm_out_fwd(y, gate, x, w["w_glu"], p["b_glu"], w["w_out"], tag + "out_fwd", _gather_d2d_stage(by_chip))
    return xn, (x, u, gate, y, hre, him, mb, ar, ai, mats_vjp), _gathered(by_chip, others)


def _ssm_layer_bwd(i, dxn, saved, p, w, scatter=None):
    tag = "l%d_" % i
    x, u, gate, y, hre, him, mb, ar, ai, mats_vjp = saved
    halves, d2d_stage = _scatter_d2d_stage(scatter)
    (dy, dgate, dw_glu, db_glu, dw_out), gots = _ssm_out_bwd(dxn, y, gate, w["w_glu"], p["b_glu"], w["w_out"], tag + "out_bwd",
                                                             d2d_stage)
    dhre, dhim = _s5_project(dy, mb["cp"], tag + "s5_state_grads")
    dxre, dxim, dar, dai = _s5_scan_bwd(dhre, dhim, hre, him, ar, ai, tag + "s5_scan_bwd")
    (du, dk, dcpt, dbpt, dd), parts = _s5_backward(dy, u, hre, him, dxre, dxim, mb["tmt"], mb["bp"], p["d"],
                                                   tag + "s5_backward", _scatter_ici_stage(scatter, halves, gots, tag))
    dk = dk.reshape(SSM_GROUPS, S5_BLOCK, SSM_GROUP, SSM_GROUP)
    da_re, da_im, dlog_step, db_re, db_im, dc_re, dc_im = mats_vjp((dk, dcpt, dbpt, dar, dai))
    dx, dw_in, dnorm = _inproj_bwd(x, p["norm"], w["w_in"], [du, dgate], dxn, tag + "inproj_bwd")
    grads = dict(norm=dnorm.reshape(D_MODEL), w_in=dw_in, a_re=da_re, a_im=da_im, log_step=dlog_step, b_re=db_re,
                 b_im=db_im, c_re=dc_re, c_im=dc_im, d=dd.reshape(D_MODEL), w_glu=dw_glu, b_glu=db_glu.reshape(D_MODEL),
                 w_out=dw_out)
    return dx, grads, parts


def _attn_layer_fwd(i, x, p, w, rope, gather=None):
    tag = "l%d_" % i
    (q, k, v, gate), _ = _inproj_fwd(x, p["norm"], w["w_in"], ATTN_SPLITS, (BF16, BF16, BF16, BF16), rope, tag + "inproj_fwd")
    o, by_chip = _attn_fwd(q, k, v, p["sinks"], tag + "attn_fwd", _gather_ici_stage(gather))
    xn, others = _attn_out_fwd(o, gate, x, w["w_out"], tag + "out_fwd", _gather_d2d_stage(by_chip))
    return xn, (x, q, k, v, gate, o), _gathered(by_chip, others)


def _attn_layer_bwd(i, dxn, saved, p, w, rope, scatter=None):
    tag = "l%d_" % i
    x, q, k, v, gate, o = saved
    halves, d2d_stage = _scatter_d2d_stage(scatter)
    (do, dgate, dw_out), gots = _attn_out_bwd(dxn, o, gate, w["w_out"], tag + "out_bwd", d2d_stage)
    (dq, dk, dv, dsinks), parts = _attn_bwd(q, k, v, o, do, p["sinks"], rope, tag + "attn_bwd",
                                            _scatter_ici_stage(scatter, halves, gots, tag))
    dx, dw_in, dnorm = _inproj_bwd(x, p["norm"], w["w_in"], [dq, dk, dv, dgate], dxn, tag + "inproj_bwd")
    grads = dict(norm=dnorm.reshape(D_MODEL), w_in=dw_in, sinks=dsinks.reshape(N_Q_HEADS, ATTN_BLOCK).sum(axis=1), w_out=dw_out)
    return dx, grads, parts


def _local_step(x, target, small, big, carried=()):
    rope = _rope_tables(x.shape[0])
    carried = {h: rest for h, *rest in carried}
    big = list(big)
    saved = []
    for i in range(4):
        gather = carried[i][0] if i in carried else None
        if i % 2 == 0:
            x, s, gathered = _ssm_layer_fwd(i, x, small[i], big[i], gather)
        else:
            x, s, gathered = _attn_layer_fwd(i, x, small[i], big[i], rope, gather)
        if gathered is not None:
            for layer, matrices in carried[i][1](gathered).items():
                big[layer] = matrices
        saved.append(s)
    loss, dx, dfinal = _loss_head(x, small[4]["norm"], target, "loss_head")
    grads = [None] * 4 + [dict(norm=dfinal.reshape(D_MODEL))]
    parts = {}
    for i in (3, 2, 1, 0):
        scatter = carried[i][2](grads) if i in carried else None
        if i % 2 == 0:
            dx, grads[i], parts[i] = _ssm_layer_bwd(i, dx, saved[i], small[i], big[i], scatter)
        else:
            dx, grads[i], parts[i] = _attn_layer_bwd(i, dx, saved[i], small[i], big[i], rope, scatter)
    return loss[0, 0], dx, grads, parts


def _owner_major(key, g):
    if key == "w_in":
        return g.reshape(D_MODEL, N_CHIPS, N_CORES, -1).transpose(2, 1, 0, 3)
    return g.reshape(N_CHIPS, N_CORES, -1, D_MODEL).transpose(1, 0, 2, 3)


def _from_gathered(key, mine, other):
    return _in_device_order(list(mine), list(other), 1 if key == "w_in" else 0)


SMALL_ROWS = 88
SMALL_TILE = 8 * D_MODEL


def _rows_of_small(a):
    flat = a.reshape(-1)
    return jnp.pad(flat, (0, -flat.shape[0] % SMALL_TILE)).reshape(-1, D_MODEL)


def _stack_small(arrays):
    rows = jnp.concatenate([_rows_of_small(a) for a in arrays], axis=0)
    assert rows.shape[0] <= N_DEV * SMALL_ROWS
    return jnp.pad(rows, ((0, N_DEV * SMALL_ROWS - rows.shape[0]), (0, 0)))


def kernel(*args):
    names = ["x"]
    layer_names = []
    for i, keys in enumerate(LAYER_KEYS):
        layer_names += ["l%d_%s" % (i, k) for k in keys]
    layer_names.append("final_norm")
    names += layer_names + ["loss_target"] + ["m_" + n for n in layer_names] + ["v_" + n for n in layer_names]
    given = dict(zip(names, args))
    big_names = [n for n in layer_names if n.split("_", 1)[1] in BIG_KEYS]
    small_names = [n for n in layer_names if n not in big_names]

    offsets = {}

    def families_of(layers):
        families = {}
        for n in big_names:
            if int(n[1]) in layers:
                family = families.setdefault(given[n].shape[1], [])
                offsets[n] = sum(given[other].shape[0] for other in family)
                family.append(n)
        return list(families.values())

    first, carried_by = families_of((0,)), {0: families_of((1,)), 1: families_of((2, 3))}
    stack = lambda pre, family: jnp.concatenate([given[pre + n] for n in family], axis=0)
    rows_of = lambda a, n: a[..., offsets[n]:offsets[n] + given[n].shape[0], :]
    local = lambda families: [stack("", family).astype(BF16) for family in families]

    def assemble(families, gathered):
        big = {}
        for family, (mine, other) in zip(families, gathered):
            for n in family:
                matrices = big.setdefault(int(n[1]), {})
                matrices[n.split("_", 1)[1]] = _from_gathered(n.split("_", 1)[1], rows_of(mine, n), rows_of(other, n))
        return big

    def sends_of(families, grads):
        return [jnp.concatenate([_owner_major(n.split("_", 1)[1], grads[int(n[1])][n.split("_", 1)[1]]) for n in family], axis=2)
                for family in families]

    small = [dict() for _ in range(5)]
    for n in small_names:
        if n == "final_norm":
            small[4]["norm"] = given[n]
        else:
            small[int(n[1])][n.split("_", 1)[1]] = given[n]

    big = [assemble(first, _all_gather(local(first), "gather_first_weights"))[0], None, None, None]
    carried = [(h, local(families), functools.partial(assemble, families),
                lambda grads, families=families: (sends_of(families, grads), [BF16] * len(families)))
               for h, families in carried_by.items()]
    loss, dx, grads, carried_parts = _local_step(given["x"][0], given["loss_target"][0], small, big, carried)
    loss = lax.psum(loss, ("x", "y", "c"))

    def grad_of(n):
        return grads[4]["norm"] if n == "final_norm" else grads[int(n[1])][n.split("_", 1)[1]]

    flat = lambda f: _stack_small([f(n) for n in small_names])
    sends = sends_of(first, grads) + [flat(grad_of).reshape(N_CORES, N_CHIPS, SMALL_ROWS, D_MODEL)]
    parts = _reduce_scatter(sends, [BF16] * len(first) + [F32], "scatter_grads")

    outs = {}
    tags = ("grad_", "delta_", "new_m_", "new_v_")
    all_families = first + carried_by[0] + carried_by[1]
    all_parts = list(parts[:-1]) + list(carried_parts[0]) + list(carried_parts[1])
    for i, (family, part) in enumerate(zip(all_families, all_parts)):
        results = _adamw(part, stack("", family), stack("m_", family), stack("v_", family), "adamw_matrices%d" % i)
        for tag, a in zip(tags, results):
            for n in family:
                outs[tag + n] = rows_of(a, n)

    my_slice = _sum_parts([parts[-1][s] for s in range(N_CHIPS)], F32, "sum_small_grads")
    mine, other = _all_gather([my_slice], "gather_small_grads")[0]
    first = lax.axis_index("c") == 0
    g_all = jnp.concatenate([jnp.where(first, mine, other), jnp.where(first, other, mine)], axis=0)
    g_all = g_all.reshape(1, N_DEV * SMALL_ROWS, D_MODEL)
    results = _adamw(g_all, flat(lambda n: given[n]), flat(lambda n: given["m_" + n]), flat(lambda n: given["v_" + n]),
                     "adamw_small")
    for tag, a in zip(tags, results):
        at = 0
        for n in small_names:
            rows = -(-given[n].size // SMALL_TILE) * 8
            outs[tag + n] = a[at:at + rows].reshape(-1)[:given[n].size].reshape(given[n].shape)
            at += rows
    result = [loss, dx[None]]
    for tag in ("grad_", "delta_", "new_m_", "new_v_"):
        result += [outs[tag + n] for n in layer_names]
    return tuple(result)
```

```python
import functools
import math

import jax
import jax.numpy as jnp
from jax import lax
from jax.experimental import pallas as pl
from jax.experimental.pallas import tpu as pltpu

F32 = jnp.float32
BF16 = jnp.bfloat16

D_MODEL = 1024
SSM_GROUP = 16
SSM_GROUPS = D_MODEL // SSM_GROUP
SSM_STATE = 64
S5_BLOCK = 16
S5_LANES = S5_BLOCK * SSM_GROUP
HEAD_DIM = 64
N_Q_HEADS = 16
N_KV_HEADS = 2
GQA = N_Q_HEADS // N_KV_HEADS
Q_DIM = N_Q_HEADS * HEAD_DIM
KV_DIM = N_KV_HEADS * HEAD_DIM
ATTN_BLOCK = 128
ROPE_THETA = 10000.0
NORM_EPS = 1e-5
NEG_INF = -1e30
ATTN_SCALE = HEAD_DIM ** -0.5
N_DEV = 8

ADAM_LR = 0.001
ADAM_B1 = 0.9
ADAM_B2 = 0.999
ADAM_EPS = 1e-08
ADAM_WD = 0.01
ADAM_STEP = 10

VMEM_LIMIT = 56 * 1024 * 1024
ROWS_FWD = 512
ROWS_BWD = 512

NT = (((1,), (1,)), ((), ()))
TN = (((0,), (0,)), ((), ()))


def _params(n_grid):
    return pltpu.CompilerParams(dimension_semantics=("arbitrary",) * n_grid, vmem_limit_bytes=VMEM_LIMIT)


def _dot(a, b):
    return jnp.dot(a, b, preferred_element_type=F32)


def _dot_nt(a, b):
    return lax.dot_general(a, b, NT, preferred_element_type=F32)


def _dot_tn(a, b):
    return lax.dot_general(a, b, TN, preferred_element_type=F32)


def _sigmoid(x):
    return 1.0 / (1.0 + jnp.exp(-x))


_GELU_K = math.sqrt(2.0 / math.pi)


def _gelu(x):
    return x * (0.5 * (1.0 + jnp.tanh(_GELU_K * (x + 0.044715 * (x * x * x)))))


def _gelu_grad(x):
    t = jnp.tanh(_GELU_K * (x + 0.044715 * (x * x * x)))
    return 0.5 * (1.0 + t) + 0.5 * x * (1.0 - t * t) * (_GELU_K * (1.0 + 3.0 * 0.044715 * (x * x)))


def _row_spec(rows, cols):
    return pl.BlockSpec((rows, cols), lambda i: (i, 0))


def _const_spec(shape):
    zeros = (0,) * len(shape)
    return pl.BlockSpec(shape, lambda i: zeros, pipeline_mode=pl.Buffered(1))


def _rope_apply(t, cos, sin_signed, sign):
    lane = lax.broadcasted_iota(jnp.int32, (1, 128), 1)
    first_half = (lane % HEAD_DIM) < (HEAD_DIM // 2)
    out = []
    for j in range(t.shape[1] // 128):
        tj = t[:, 128 * j:128 * (j + 1)]
        partner = jnp.where(first_half, pltpu.roll(tj, 128 - HEAD_DIM // 2, 1), pltpu.roll(tj, HEAD_DIM // 2, 1))
        out.append(tj * cos + sign * (partner * sin_signed))
    return out[0] if len(out) == 1 else jnp.concatenate(out, axis=1)


def _inproj_fwd(x, norm, w, splits, dtypes, rope, name, exchange=None):
    t = x.shape[0]
    n = w.shape[1]
    rows = ROWS_FWD

    def body(*refs):
        if rope is None:
            x_ref, n_ref, w_ref = refs[:3]
            outs = refs[3:]
        else:
            x_ref, n_ref, w_ref, cos_ref, sin_ref = refs[:5]
            outs = refs[5:]
        xv = x_ref[...]
        rstd = lax.rsqrt(jnp.mean(xv * xv, axis=-1, keepdims=True) + NORM_EPS)
        h = (xv * rstd) * n_ref[...]
        proj = _dot(h.astype(BF16), w_ref[...])
        off = 0
        for i, width in enumerate(splits):
            piece = proj[:, off:off + width]
            if rope is not None and i < 2:
                piece = _rope_apply(piece, cos_ref[...], sin_ref[...], 1.0)
            outs[i][...] = piece.astype(dtypes[i])
            off += width

    in_specs = [_row_spec(rows, D_MODEL), _const_spec((1, D_MODEL)), _const_spec((D_MODEL, n))]
    args = [x, norm.reshape(1, D_MODEL), w]
    if rope is not None:
        in_specs += [_row_spec(rows, 128), _row_spec(rows, 128)]
        args += list(rope)
    return _call_hosting(
        exchange, body, args, name=name, steps=t // rows, in_specs=in_specs,
        out_specs=[_row_spec(rows, width) for width in splits],
        out_shape=[jax.ShapeDtypeStruct((t, width), dtype) for width, dtype in zip(splits, dtypes)])


def _inproj_bwd(x, norm, w, dpieces, dxn, name):
    t = x.shape[0]
    n = w.shape[1]
    rows = ROWS_BWD
    widths = [p.shape[1] for p in dpieces]
    k = len(dpieces)

    def body(*refs):
        x_ref, n_ref, w_ref, dxn_ref = refs[:4]
        d_refs = refs[4:4 + k]
        dx_ref, dw_ref, dn_ref = refs[4 + k:]
        @pl.when(pl.program_id(0) == 0)
        def _():
            dw_ref[...] = jnp.zeros_like(dw_ref)
            dn_ref[...] = jnp.zeros_like(dn_ref)

        xv = x_ref[...]
        rstd = lax.rsqrt(jnp.mean(xv * xv, axis=-1, keepdims=True) + NORM_EPS)
        xhat = xv * rstd
        h = xhat * n_ref[...]
        dproj = [r[...].astype(BF16) for r in d_refs]
        dproj = dproj[0] if k == 1 else jnp.concatenate(dproj, axis=1)
        dh = _dot_nt(dproj, w_ref[...])
        dw_ref[...] += _dot_tn(h.astype(BF16), dproj)
        dn_ref[...] += jnp.sum(dh * xhat, axis=0, keepdims=True)
        dxhat = dh * n_ref[...]
        dx_ref[...] = rstd * (dxhat - xhat * jnp.mean(dxhat * xhat, axis=-1, keepdims=True)) + dxn_ref[...]

    return _call_hosting(
        None, body, (x, norm.reshape(1, D_MODEL), w, dxn, *dpieces), name=name, steps=t // rows,
        in_specs=[_row_spec(rows, D_MODEL), _const_spec((1, D_MODEL)), _const_spec((D_MODEL, n)),
                  _row_spec(rows, D_MODEL)] + [_row_spec(rows, width) for width in widths],
        out_specs=[_row_spec(rows, D_MODEL), _const_spec((D_MODEL, n)), _const_spec((1, D_MODEL))],
        out_shape=[jax.ShapeDtypeStruct((t, D_MODEL), F32), jax.ShapeDtypeStruct((D_MODEL, n), F32),
                   jax.ShapeDtypeStruct((1, D_MODEL), F32)])[0]


def _s5_matrices(a_re, a_im, log_step, b_re, b_im, c_re, c_im):
    r = S5_BLOCK
    step = jnp.exp(log_step)[:, None]
    lr, li = a_re * step, a_im * step
    k = jnp.arange(r + 1, dtype=F32)
    mag = jnp.exp(lr[:, None, :] * k[:, None])
    pr = mag * jnp.cos(li[:, None, :] * k[:, None])
    pi = mag * jnp.sin(li[:, None, :] * k[:, None])
    nr, ni = pr[:, 1] - 1.0, pi[:, 1]
    den = a_re * a_re + a_im * a_im
    qr, qi = (nr * a_re + ni * a_im) / den, (ni * a_re - nr * a_im) / den
    bbr = qr[..., None] * b_re - qi[..., None] * b_im
    bbi = qr[..., None] * b_im + qi[..., None] * b_re
    wr = c_re[:, None] * pr[:, :, None, :] - c_im[:, None] * pi[:, :, None, :]
    wi = c_re[:, None] * pi[:, :, None, :] + c_im[:, None] * pr[:, :, None, :]
    w = jnp.concatenate([wr, -wi], axis=-1)
    bb = jnp.concatenate([bbr, bbi], axis=1)
    kern = jnp.einsum("gxp,gpi->gxi", w[:, :r].reshape(SSM_GROUPS, S5_LANES, 2 * SSM_STATE), bb,
                      precision=lax.Precision.HIGHEST).reshape(SSM_GROUPS, r, SSM_GROUP, SSM_GROUP)
    cpt = w[:, 1:].reshape(SSM_GROUPS, S5_LANES, 2 * SSM_STATE)
    prs = jnp.swapaxes(pr[:, r - 1::-1][:, :r], 1, 2)[..., None]
    pis = jnp.swapaxes(pi[:, r - 1::-1][:, :r], 1, 2)[..., None]
    bp_re = prs * bbr[:, :, None, :] - pis * bbi[:, :, None, :]
    bp_im = prs * bbi[:, :, None, :] + pis * bbr[:, :, None, :]
    bpt = jnp.concatenate([bp_re, bp_im], axis=1).reshape(SSM_GROUPS, 2 * SSM_STATE, S5_LANES)
    ar = pr[:, r].reshape(1, SSM_GROUPS * SSM_STATE)
    ai = pi[:, r].reshape(1, SSM_GROUPS * SSM_STATE)
    return kern, cpt, bpt, ar, ai


def _s5_toeplitz(kern):
    r = S5_BLOCK
    kern_t = jnp.swapaxes(kern.reshape(SSM_GROUPS, S5_LANES, SSM_GROUP), 1, 2)
    rows = [jnp.pad(kern_t[:, :, :S5_LANES - SSM_GROUP * s], ((0, 0), (0, 0), (SSM_GROUP * s, 0))) for s in range(r)]
    return jnp.concatenate(rows, axis=1)


S5_OCTET = 128 // SSM_GROUP
S5_STEPS = SSM_GROUPS // S5_OCTET


def _oct_spec(t):
    return pl.BlockSpec((t, 128), lambda j: (0, j))


def _state_spec(nb):
    return pl.BlockSpec((nb, S5_OCTET * SSM_STATE), lambda j: (0, j))


def _gmat_spec(a, b):
    return pl.BlockSpec((S5_OCTET, a, b), lambda j: (j, 0, 0))


def _block_rows(ref, nb):
    by_position = jnp.swapaxes(ref[...].reshape(nb, S5_BLOCK, 128), 0, 1)
    return [by_position[r] for r in range(S5_BLOCK)]


def _store_block_rows(ref, pieces, nb):
    ref[...] = jnp.swapaxes(jnp.stack(pieces, axis=0), 0, 1).reshape(nb * S5_BLOCK, 128)


def _group_cols(pieces_t, g):
    return jnp.concatenate([p[SSM_GROUP * g:SSM_GROUP * (g + 1)] for p in pieces_t], axis=0)


def _state_cols(re_t, im_t, g):
    return jnp.concatenate([re_t[SSM_STATE * g:SSM_STATE * (g + 1)], im_t[SSM_STATE * g:SSM_STATE * (g + 1)]], axis=0)


def _s5_project(a, mat, name):
    t = a.shape[0]
    nb = t // S5_BLOCK

    def body(a_ref, m_ref, re_ref, im_ref):
        at = [p.T for p in _block_rows(a_ref, nb)]
        for pair in range(S5_OCTET // 2):
            xs = [_dot(m_ref[2 * pair + k], _group_cols(at, 2 * pair + k).astype(BF16)) for k in (0, 1)]
            lanes = slice(128 * pair, 128 * (pair + 1))
            re_ref[:, lanes] = jnp.concatenate([xs[0][:SSM_STATE], xs[1][:SSM_STATE]], axis=0).T
            im_ref[:, lanes] = jnp.concatenate([xs[0][SSM_STATE:], xs[1][SSM_STATE:]], axis=0).T

    return pl.pallas_call(
        body, name=name, grid=(S5_STEPS,),
        in_specs=[_oct_spec(t), _gmat_spec(2 * SSM_STATE, S5_LANES)],
        out_specs=[_state_spec(nb), _state_spec(nb)],
        out_shape=[jax.ShapeDtypeStruct((nb, SSM_GROUPS * SSM_STATE), F32)] * 2,
        compiler_params=_params(1),
    )(a, mat)


_SCAN_LANES = 2048


def _s5_scan_fwd(xre, xim, ar, ai, name):
    nb = xre.shape[0]
    col = pl.BlockSpec((nb, _SCAN_LANES), lambda j: (0, j))
    par = pl.BlockSpec((1, _SCAN_LANES), lambda j: (0, j))

    def body(xre_ref, xim_ref, ar_ref, ai_ref, hre_ref, him_ref):
        a_r, a_i = ar_ref[...], ai_ref[...]

        def step(b, carry):
            hr, hi = carry
            hre_ref[pl.ds(b, 1), :] = hr
            him_ref[pl.ds(b, 1), :] = hi
            xr, xi = xre_ref[pl.ds(b, 1), :], xim_ref[pl.ds(b, 1), :]
            return a_r * hr - a_i * hi + xr, a_r * hi + a_i * hr + xi

        zero = jnp.zeros((1, _SCAN_LANES), F32)
        lax.fori_loop(0, nb, step, (zero, zero))

    return pl.pallas_call(
        body, name=name, grid=(xre.shape[1] // _SCAN_LANES,),
        in_specs=[col, col, par, par], out_specs=[col, col],
        out_shape=[jax.ShapeDtypeStruct(xre.shape, F32)] * 2,
        compiler_params=_params(1),
    )(xre, xim, ar, ai)


def _s5_scan_bwd(dhre, dhim, hre, him, ar, ai, name):
    nb = dhre.shape[0]
    col = pl.BlockSpec((nb, _SCAN_LANES), lambda j: (0, j))
    par = pl.BlockSpec((1, _SCAN_LANES), lambda j: (0, j))

    def body(dhre_ref, dhim_ref, hre_ref, him_ref, ar_ref, ai_ref, dxre_ref, dxim_ref, dar_ref, dai_ref):
        a_r, a_i = ar_ref[...], ai_ref[...]

        def step(s, carry):
            gr, gi, dar, dai = carry
            b = nb - 1 - s
            dxre_ref[pl.ds(b, 1), :] = gr
            dxim_ref[pl.ds(b, 1), :] = gi
            hr, hi = hre_ref[pl.ds(b, 1), :], him_ref[pl.ds(b, 1), :]
            dar = dar + (hr * gr + hi * gi)
            dai = dai + (hr * gi - hi * gr)
            dr, di = dhre_ref[pl.ds(b, 1), :], dhim_ref[pl.ds(b, 1), :]
            return dr + (a_r * gr + a_i * gi), di + (a_r * gi - a_i * gr), dar, dai

        zero = jnp.zeros((1, _SCAN_LANES), F32)
        _, _, dar, dai = lax.fori_loop(0, nb, step, (zero, zero, zero, zero))
        dar_ref[...] = dar
        dai_ref[...] = dai

    return pl.pallas_call(
        body, name=name, grid=(dhre.shape[1] // _SCAN_LANES,),
        in_specs=[col, col, col, col, par, par], out_specs=[col, col, par, par],
        out_shape=[jax.ShapeDtypeStruct(dhre.shape, F32)] * 2 + [jax.ShapeDtypeStruct(ar.shape, F32)] * 2,
        compiler_params=_params(1),
    )(dhre, dhim, hre, him, ar, ai)


def _s5_outputs(u, hre, him, tm, cpt, d, name):
    t = u.shape[0]
    nb = t // S5_BLOCK

    def body(u_ref, hre_ref, him_ref, tm_ref, cpt_ref, d_ref, y_ref):
        u_rows = _block_rows(u_ref, nb)
        ut = [p.T for p in u_rows]
        hre_t, him_t = hre_ref[...].T, him_ref[...].T
        yts = []
        for g in range(S5_OCTET):
            yts.append(_dot(tm_ref[g], _group_cols(ut, g).astype(BF16))
                       + _dot(cpt_ref[g], _state_cols(hre_t, him_t, g).astype(BF16)))
        y_rows = []
        for r in range(S5_BLOCK):
            rows = jnp.concatenate([yt[SSM_GROUP * r:SSM_GROUP * (r + 1)] for yt in yts], axis=0)
            y_rows.append(rows.T + d_ref[...] * u_rows[r])
        _store_block_rows(y_ref, y_rows, nb)

    return pl.pallas_call(
        body, name=name, grid=(S5_STEPS,),
        in_specs=[_oct_spec(t), _state_spec(nb), _state_spec(nb), _gmat_spec(S5_LANES, S5_LANES),
                  _gmat_spec(S5_LANES, 2 * SSM_STATE), _oct_spec(1)],
        out_specs=_oct_spec(t),
        out_shape=jax.ShapeDtypeStruct(u.shape, F32),
        compiler_params=_params(1),
    )(u, hre, him, tm, cpt, d.reshape(1, D_MODEL))


def _s5_backward(dy, u, hre, him, dxre, dxim, tmt, bp, d, name, exchange=None):
    t = u.shape[0]
    nb = t // S5_BLOCK

    def body(dy_ref, u_ref, hre_ref, him_ref, dxre_ref, dxim_ref, tmt_ref, bp_ref, d_ref,
             du_ref, dk_ref, dcpt_ref, dbpt_ref, dd_ref, dtm_scratch):
        dy_rows, u_rows = _block_rows(dy_ref, nb), _block_rows(u_ref, nb)
        dyt, ut = [p.T for p in dy_rows], [p.T for p in u_rows]
        hre_t, him_t = hre_ref[...].T, him_ref[...].T
        dxre_t, dxim_t = dxre_ref[...].T, dxim_ref[...].T
        duts = []
        for g in range(S5_OCTET):
            dyg, ug = _group_cols(dyt, g).astype(BF16), _group_cols(ut, g).astype(BF16)
            hg = _state_cols(hre_t, him_t, g).astype(BF16)
            dxg = _state_cols(dxre_t, dxim_t, g).astype(BF16)
            duts.append(_dot(tmt_ref[g], dyg) + _dot(bp_ref[g], dxg))
            dtm_scratch[...] = _dot_nt(dyg, ug)
            dk = dtm_scratch[:, :SSM_GROUP]
            for s in range(1, S5_BLOCK):
                below = dtm_scratch[SSM_GROUP * s:, SSM_GROUP * s:SSM_GROUP * (s + 1)]
                dk = dk + jnp.concatenate([below, jnp.zeros((SSM_GROUP * s, SSM_GROUP), F32)], axis=0)
            dk_ref[g] = dk
            dcpt_ref[g] = _dot_nt(dyg, hg)
            dbpt_ref[g] = _dot_nt(dxg, ug)
        dd = jnp.zeros((1, 128), F32)
        du_rows = []
        for r in range(S5_BLOCK):
            rows = jnp.concatenate([dut[SSM_GROUP * r:SSM_GROUP * (r + 1)] for dut in duts], axis=0)
            du_rows.append(rows.T + d_ref[...] * dy_rows[r])
            dd = dd + jnp.sum(dy_rows[r] * u_rows[r], axis=0, keepdims=True)
        _store_block_rows(du_ref, du_rows, nb)
        dd_ref[...] = dd

    return _call_hosting(
        exchange, body, (dy, u, hre, him, dxre, dxim, tmt, bp, d.reshape(1, D_MODEL)), name=name, steps=S5_STEPS,
        in_specs=[_oct_spec(t), _oct_spec(t), _state_spec(nb), _state_spec(nb), _state_spec(nb), _state_spec(nb),
                  _gmat_spec(S5_LANES, S5_LANES), _gmat_spec(S5_LANES, 2 * SSM_STATE), _oct_spec(1)],
        out_specs=[_oct_spec(t), _gmat_spec(S5_LANES, SSM_GROUP), _gmat_spec(S5_LANES, 2 * SSM_STATE),
                   _gmat_spec(2 * SSM_STATE, S5_LANES), _oct_spec(1)],
        out_shape=[jax.ShapeDtypeStruct(u.shape, F32),
                   jax.ShapeDtypeStruct((SSM_GROUPS, S5_LANES, SSM_GROUP), F32),
                   jax.ShapeDtypeStruct((SSM_GROUPS, S5_LANES, 2 * SSM_STATE), F32),
                   jax.ShapeDtypeStruct((SSM_GROUPS, 2 * SSM_STATE, S5_LANES), F32),
                   jax.ShapeDtypeStruct((1, D_MODEL), F32)],
        scratch_shapes=[pltpu.VMEM((S5_LANES, S5_LANES), F32)])


def _ssm_out_fwd(y, gate, x, w_glu, b_glu, w_out, name, exchange=None):
    t = x.shape[0]
    rows = ROWS_FWD

    def body(y_ref, g_ref, x_ref, wg_ref, bg_ref, wo_ref, o_ref):
        z0 = _gelu(y_ref[...])
        s = _dot(z0.astype(BF16), wg_ref[...]) + bg_ref[...]
        gate_v = g_ref[...].astype(F32)
        a = (z0 * _sigmoid(s)) * (gate_v * _sigmoid(gate_v))
        o_ref[...] = x_ref[...] + _dot(a.astype(BF16), wo_ref[...])

    (xn,), got = _call_hosting(
        exchange, body, (y, gate, x, w_glu, b_glu.reshape(1, D_MODEL), w_out), name=name, steps=t // rows,
        in_specs=[_row_spec(rows, D_MODEL)] * 3 + [_const_spec((D_MODEL, D_MODEL)), _const_spec((1, D_MODEL)),
                                                   _const_spec((D_MODEL, D_MODEL))],
        out_specs=[_row_spec(rows, D_MODEL)],
        out_shape=[jax.ShapeDtypeStruct((t, D_MODEL), F32)])
    return xn, got


def _ssm_out_bwd(dxn, y, gate, w_glu, b_glu, w_out, name, exchange=None):
    t = y.shape[0]
    rows = ROWS_BWD

    def body(dxn_ref, y_ref, g_ref, wg_ref, bg_ref, wo_ref, dy_ref, dg_ref, dwg_ref, dbg_ref, dwo_ref):
        @pl.when(pl.program_id(0) == 0)
        def _():
            dwo_ref[...] = jnp.zeros_like(dwo_ref)
            dwg_ref[...] = jnp.zeros_like(dwg_ref)
            dbg_ref[...] = jnp.zeros_like(dbg_ref)

        yv = y_ref[...]
        z0 = _gelu(yv)
        z0b = z0.astype(BF16)
        sg = _sigmoid(_dot(z0b, wg_ref[...]) + bg_ref[...])
        z = z0 * sg
        gate_v = g_ref[...].astype(F32)
        sgg = _sigmoid(gate_v)
        silu = gate_v * sgg
        dob = dxn_ref[...].astype(BF16)
        da = _dot_nt(dob, wo_ref[...])
        dwo_ref[...] += _dot_tn((z * silu).astype(BF16), dob)
        dz = da * silu
        dg_ref[...] = (da * z * (sgg * (1.0 + gate_v * (1.0 - sgg)))).astype(BF16)
        ds = dz * z0 * (sg * (1.0 - sg))
        dsb = ds.astype(BF16)
        dz0 = dz * sg + _dot_nt(dsb, wg_ref[...])
        dwg_ref[...] += _dot_tn(z0b, dsb)
        dbg_ref[...] += jnp.sum(ds, axis=0, keepdims=True)
        dy_ref[...] = dz0 * _gelu_grad(yv)

    sq = _const_spec((D_MODEL, D_MODEL))
    vec = _const_spec((1, D_MODEL))
    return _call_hosting(
        exchange, body, (dxn, y, gate, w_glu, b_glu.reshape(1, D_MODEL), w_out), name=name, steps=t // rows,
        in_specs=[_row_spec(rows, D_MODEL)] * 3 + [sq, vec, sq],
        out_specs=[_row_spec(rows, D_MODEL), _row_spec(rows, D_MODEL), sq, vec, sq],
        out_shape=[jax.ShapeDtypeStruct((t, D_MODEL), F32), jax.ShapeDtypeStruct((t, D_MODEL), BF16),
                   jax.ShapeDtypeStruct((D_MODEL, D_MODEL), F32), jax.ShapeDtypeStruct((1, D_MODEL), F32),
                   jax.ShapeDtypeStruct((D_MODEL, D_MODEL), F32)])


KV_LANES = GQA * ATTN_BLOCK


def _attn_bias(block_is_first):
    kj = lax.broadcasted_iota(jnp.int32, (2 * ATTN_BLOCK, ATTN_BLOCK), 0)
    qi = lax.broadcasted_iota(jnp.int32, (2 * ATTN_BLOCK, ATTN_BLOCK), 1)
    dist = qi + ATTN_BLOCK - kj
    valid = (dist >= 0) & (dist < ATTN_BLOCK) & (jnp.logical_not(block_is_first) | (kj >= ATTN_BLOCK))
    return jnp.tile(jnp.where(valid, 0.0, NEG_INF).astype(F32), (1, GQA))


def _head_cols(a_t, kvh):
    heads = range(kvh * GQA, (kvh + 1) * GQA)
    return jnp.concatenate([a_t[HEAD_DIM * h:HEAD_DIM * (h + 1)] for h in heads], axis=1)


def _head_rows(a_cols):
    stacked = jnp.concatenate([a_cols[:, ATTN_BLOCK * g:ATTN_BLOCK * (g + 1)] for g in range(GQA)], axis=0)
    return stacked.T


def _kv_rows(prev_ref, cur_ref, kvh):
    lanes = slice(HEAD_DIM * kvh, HEAD_DIM * (kvh + 1))
    return jnp.concatenate([prev_ref[:, lanes], cur_ref[:, lanes]], axis=0).astype(BF16)


def _kv_cols(prev_t, cur_t, kvh):
    rows = slice(HEAD_DIM * kvh, HEAD_DIM * (kvh + 1))
    return jnp.concatenate([prev_t[rows], cur_t[rows]], axis=1).astype(BF16)


def _attn_probs(kk, q_cols, sink_row, bias):
    s = _dot(kk, q_cols) * ATTN_SCALE + bias
    m = jnp.maximum(jnp.max(s, axis=0, keepdims=True), sink_row)
    p = jnp.exp(s - m)
    e_sink = jnp.exp(sink_row - m)
    inv = 1.0 / (jnp.sum(p, axis=0, keepdims=True) + e_sink)
    return p * inv, e_sink * inv


def _sink_cols(sinks):
    return jnp.repeat(sinks, ATTN_BLOCK).reshape(N_KV_HEADS, 1, KV_LANES)


def _attn_fwd(q, k, v, sinks, name, exchange=None):
    t = q.shape[0]
    nblk = t // ATTN_BLOCK

    def body(s_ref, q_ref, kc_ref, kp_ref, vc_ref, vp_ref, o_ref):
        bias = _attn_bias(pl.program_id(0) == 0)
        q_t = q_ref[...].astype(F32).T
        vp_t, vc_t = vp_ref[...].astype(F32).T, vc_ref[...].astype(F32).T
        for kvh in range(N_KV_HEADS):
            p, _ = _attn_probs(_kv_rows(kp_ref, kc_ref, kvh), _head_cols(q_t, kvh).astype(BF16), s_ref[kvh], bias)
            o_cols = _dot(_kv_cols(vp_t, vc_t, kvh), p.astype(BF16))
            o_ref[:, GQA * HEAD_DIM * kvh:GQA * HEAD_DIM * (kvh + 1)] = _head_rows(o_cols)

    cur = lambda i: (i, 0)
    prev = lambda i: (jnp.maximum(i - 1, 0), 0)
    (o,), got = _call_hosting(
        exchange, body, (_sink_cols(sinks), q, k, k, v, v), name=name, steps=nblk,
        in_specs=[_const_spec((N_KV_HEADS, 1, KV_LANES)),
                  pl.BlockSpec((ATTN_BLOCK, Q_DIM), cur),
                  pl.BlockSpec((ATTN_BLOCK, KV_DIM), cur), pl.BlockSpec((ATTN_BLOCK, KV_DIM), prev),
                  pl.BlockSpec((ATTN_BLOCK, KV_DIM), cur), pl.BlockSpec((ATTN_BLOCK, KV_DIM), prev)],
        out_specs=[pl.BlockSpec((ATTN_BLOCK, Q_DIM), cur)],
        out_shape=[jax.ShapeDtypeStruct((t, Q_DIM), F32)])
    return o, got


def _attn_bwd(q, k, v, o, do, sinks, rope, name, exchange=None):
    t = q.shape[0]
    nblk = t // ATTN_BLOCK

    def body(s_ref, q_ref, o_ref, do_ref, kp_ref, kc_ref, vp_ref, vc_ref, cosq_ref, sinq_ref, cosk_ref, sinkey_ref,
             dq_ref, dk_ref, dv_ref, ds_ref, new_k, new_v, wait_k, wait_v, dq_rot):
        n = pl.program_id(0)

        @pl.when(n == 0)
        def _():
            ds_ref[...] = jnp.zeros_like(ds_ref)
            wait_k[...] = jnp.zeros_like(wait_k)
            wait_v[...] = jnp.zeros_like(wait_v)

        @pl.when(n < nblk)
        def _():
            bias = _attn_bias(n == 0)
            q_t, o_t, do_t = q_ref[...].astype(F32).T, o_ref[...].T, do_ref[...].T
            kp_t, kc_t = kp_ref[...].astype(F32).T, kc_ref[...].astype(F32).T
            for kvh in range(N_KV_HEADS):
                q_cols = _head_cols(q_t, kvh).astype(BF16)
                do_cols = _head_cols(do_t, kvh)
                delta = jnp.sum(do_cols * _head_cols(o_t, kvh), axis=0, keepdims=True)
                do_cols = do_cols.astype(BF16)
                p, p_sink = _attn_probs(_kv_rows(kp_ref, kc_ref, kvh), q_cols, s_ref[kvh], bias)
                dp = _dot(_kv_rows(vp_ref, vc_ref, kvh), do_cols)
                ds = (p * (dp - delta) * ATTN_SCALE).astype(BF16)
                lanes = slice(GQA * HEAD_DIM * kvh, GQA * HEAD_DIM * (kvh + 1))
                dq_rot[:, lanes] = _head_rows(_dot(_kv_cols(kp_t, kc_t, kvh), ds))
                head = slice(HEAD_DIM * kvh, HEAD_DIM * (kvh + 1))
                new_k[:, head] = _dot_nt(ds, q_cols)
                new_v[:, head] = _dot_nt(p.astype(BF16), do_cols)
                ds_ref[kvh] += -(p_sink * delta)
            dq_ref[...] = _rope_apply(dq_rot[...], cosq_ref[...], sinq_ref[...], -1.0).astype(BF16)

        @pl.when(n == nblk)
        def _():
            new_k[...] = jnp.zeros_like(new_k)
            new_v[...] = jnp.zeros_like(new_v)

        dk_ref[...] = _rope_apply(wait_k[...] + new_k[:ATTN_BLOCK], cosk_ref[...], sinkey_ref[...], -1.0).astype(BF16)
        dv_ref[...] = (wait_v[...] + new_v[:ATTN_BLOCK]).astype(BF16)
        wait_k[...] = new_k[ATTN_BLOCK:]
        wait_v[...] = new_v[ATTN_BLOCK:]

    cur = lambda i: (jnp.minimum(i, nblk - 1), 0)
    prev = lambda i: (jnp.maximum(i - 1, 0), 0)
    qs = lambda f: pl.BlockSpec((ATTN_BLOCK, Q_DIM), f)
    ks = lambda f: pl.BlockSpec((ATTN_BLOCK, KV_DIM), f)
    sink_spec = _const_spec((N_KV_HEADS, 1, KV_LANES))
    return _call_hosting(
        exchange, body, (_sink_cols(sinks), q, o, do, k, k, v, v, rope[0], rope[1], rope[0], rope[1]),
        name=name, steps=nblk + 1,
        in_specs=[sink_spec, qs(cur), qs(cur), qs(cur), ks(prev), ks(cur), ks(prev), ks(cur),
                  ks(cur), ks(cur), ks(prev), ks(prev)],
        out_specs=[qs(cur), ks(prev), ks(prev), sink_spec],
        out_shape=[jax.ShapeDtypeStruct((t, Q_DIM), BF16), jax.ShapeDtypeStruct((t, KV_DIM), BF16),
                   jax.ShapeDtypeStruct((t, KV_DIM), BF16), jax.ShapeDtypeStruct((N_KV_HEADS, 1, KV_LANES), F32)],
        scratch_shapes=[pltpu.VMEM((2 * ATTN_BLOCK, KV_DIM), F32), pltpu.VMEM((2 * ATTN_BLOCK, KV_DIM), F32),
                        pltpu.VMEM((ATTN_BLOCK, KV_DIM), F32), pltpu.VMEM((ATTN_BLOCK, KV_DIM), F32),
                        pltpu.VMEM((ATTN_BLOCK, Q_DIM), F32)])


def _attn_out_fwd(o, gate, x, w_out, name, exchange=None):
    t = x.shape[0]
    rows = ROWS_FWD

    def body(o_ref, g_ref, x_ref, wo_ref, xn_ref):
        gate_v = g_ref[...].astype(F32)
        a = o_ref[...] * (gate_v * _sigmoid(gate_v))
        xn_ref[...] = x_ref[...] + _dot(a.astype(BF16), wo_ref[...])

    (xn,), got = _call_hosting(
        exchange, body, (o, gate, x, w_out), name=name, steps=t // rows,
        in_specs=[_row_spec(rows, D_MODEL)] * 3 + [_const_spec((D_MODEL, D_MODEL))],
        out_specs=[_row_spec(rows, D_MODEL)],
        out_shape=[jax.ShapeDtypeStruct((t, D_MODEL), F32)])
    return xn, got


def _attn_out_bwd(dxn, o, gate, w_out, name, exchange=None):
    t = o.shape[0]
    rows = ROWS_BWD

    def body(dxn_ref, o_ref, g_ref, wo_ref, do_ref, dg_ref, dwo_ref):
        @pl.when(pl.program_id(0) == 0)
        def _():
            dwo_ref[...] = jnp.zeros_like(dwo_ref)

        gate_v, ov = g_ref[...].astype(F32), o_ref[...]
        sgg = _sigmoid(gate_v)
        silu = gate_v * sgg
        dob = dxn_ref[...].astype(BF16)
        da = _dot_nt(dob, wo_ref[...])
        dwo_ref[...] += _dot_tn((ov * silu).astype(BF16), dob)
        do_ref[...] = da * silu
        dg_ref[...] = (da * ov * (sgg * (1.0 + gate_v * (1.0 - sgg)))).astype(BF16)

    sq = _const_spec((D_MODEL, D_MODEL))
    return _call_hosting(
        exchange, body, (dxn, o, gate, w_out), name=name, steps=t // rows,
        in_specs=[_row_spec(rows, D_MODEL)] * 3 + [sq],
        out_specs=[_row_spec(rows, D_MODEL), _row_spec(rows, D_MODEL), sq],
        out_shape=[jax.ShapeDtypeStruct((t, D_MODEL), F32), jax.ShapeDtypeStruct((t, D_MODEL), BF16),
                   jax.ShapeDtypeStruct((D_MODEL, D_MODEL), F32)])


def _loss_head(x, norm, target, name):
    t = x.shape[0]
    rows = ROWS_FWD

    def body(x_ref, n_ref, t_ref, loss_ref, dx_ref, dn_ref):
        i = pl.program_id(0)
        xv = x_ref[...]
        rstd = lax.rsqrt(jnp.mean(xv * xv, axis=-1, keepdims=True) + NORM_EPS)
        xhat = xv * rstd
        err = xhat * n_ref[...] - t_ref[...]
        part = 0.5 * jnp.sum(jnp.mean(err * err, axis=-1, keepdims=True), axis=0, keepdims=True)
        dy = err * (1.0 / D_MODEL)
        dn = jnp.sum(dy * xhat, axis=0, keepdims=True)
        dxhat = dy * n_ref[...]
        dx_ref[...] = rstd * (dxhat - xhat * jnp.mean(dxhat * xhat, axis=-1, keepdims=True))

        @pl.when(i == 0)
        def _():
            loss_ref[...] = jnp.zeros((8, 128), F32) + part
            dn_ref[...] = dn

        @pl.when(i > 0)
        def _():
            loss_ref[...] += part
            dn_ref[...] += dn

    return pl.pallas_call(
        body, name=name, grid=(t // rows,),
        in_specs=[_row_spec(rows, D_MODEL), _const_spec((1, D_MODEL)), _row_spec(rows, D_MODEL)],
        out_specs=[_const_spec((8, 128)), _row_spec(rows, D_MODEL), _const_spec((1, D_MODEL))],
        out_shape=[jax.ShapeDtypeStruct((8, 128), F32), jax.ShapeDtypeStruct((t, D_MODEL), F32),
                   jax.ShapeDtypeStruct((1, D_MODEL), F32)],
        compiler_params=_params(1),
    )(x, norm.reshape(1, D_MODEL), target)


N_CHIPS = 4
N_CORES = 2
CHIP_FLIPS = ((0, 1), (1, 0), (1, 1))
ICI_CHUNKS = 2
D2D_CHUNKS = 8


def _n_chunks(rows, dtype, most):
    unit = 16 if dtype == BF16 else 8
    return max(n for n in range(1, most + 1) if rows % n == 0 and (rows // n) % unit == 0)


def _chunks_of(arrays, most):
    out = []
    for a in arrays:
        n = _n_chunks(a.shape[-2], a.dtype, most)
        out.append((n, a.shape[-2] // n))
    return out


class _Exchange:
    def __init__(self, arrays, out_shape, scratch, copies):
        self.arrays, self.out_shape, self.scratch, self._copies = arrays, out_shape, scratch, copies

    def start(self, *refs):
        for cp in self._copies(*refs)[0]:
            cp.start()

    def wait(self, *refs):
        for wait in self._copies(*refs)[1]:
            wait()


def _chips_exchange(sends, per_dest):
    n = len(sends)
    chunking = _chunks_of(sends, ICI_CHUNKS)

    def copies(send_refs, recv_refs, sems):
        x, y, c = lax.axis_index("x"), lax.axis_index("y"), lax.axis_index("c")
        me = 2 * x + y

        def peer(k):
            fx, fy = CHIP_FLIPS[k]
            px, py = x + fx - 2 * x * fx, y + fy - 2 * y * fy
            return (px, py, c), 2 * px + py

        to_start, waits = [], []
        for a in range(n):
            send_sems, recv_sems, local_sems = sems[3 * a:3 * a + 3]
            chunks, chunk_rows = chunking[a]
            for j in range(chunks):
                part = pl.ds(j * chunk_rows, chunk_rows)
                src = lambda number: send_refs[a].at[number, part] if per_dest else send_refs[a].at[part]
                for k in range(len(CHIP_FLIPS)):
                    to, to_number = peer(k)
                    remote = lambda landing: pltpu.make_async_remote_copy(
                        src_ref=src(to_number), dst_ref=recv_refs[a].at[landing, part],
                        send_sem=send_sems.at[k, j], recv_sem=recv_sems.at[k, j],
                        device_id=to, device_id_type=pl.DeviceIdType.MESH)
                    to_start.append(remote(me))
                    waits += [remote(me).wait_send, remote(to_number).wait_recv]
                own = pltpu.make_async_copy(src(me), recv_refs[a].at[me, part], local_sems.at[j])
                to_start.append(own)
                waits.append(own.wait)
        return to_start, waits

    scratch = []
    for chunks, _ in chunking:
        scratch += [pltpu.SemaphoreType.DMA((len(CHIP_FLIPS), chunks)), pltpu.SemaphoreType.DMA((len(CHIP_FLIPS), chunks)),
                    pltpu.SemaphoreType.DMA((chunks,))]
    return _Exchange(sends, [jax.ShapeDtypeStruct((N_CHIPS,) + a.shape[-2:], a.dtype) for a in sends], scratch, copies)


def _cores_exchange(sends, per_dest):
    n = len(sends)
    chunking = _chunks_of(sends, D2D_CHUNKS)

    def copies(send_refs, got_refs, sems):
        c = lax.axis_index("c")
        sibling = (lax.axis_index("x"), lax.axis_index("y"), 1 - c)
        to_start = []
        for a in range(n):
            chunks, chunk_rows = chunking[a]
            for j in range(chunks):
                part = pl.ds(j * chunk_rows, chunk_rows)
                to_start.append(pltpu.make_async_remote_copy(
                    src_ref=send_refs[a].at[1 - c, part] if per_dest else send_refs[a].at[part],
                    dst_ref=got_refs[a].at[part], send_sem=sems[2 * a].at[j], recv_sem=sems[2 * a + 1].at[j],
                    device_id=sibling, device_id_type=pl.DeviceIdType.MESH))
        return to_start, [cp.wait for cp in to_start]

    scratch = []
    for chunks, _ in chunking:
        scratch += [pltpu.SemaphoreType.DMA((chunks,)), pltpu.SemaphoreType.DMA((chunks,))]
    return _Exchange(sends, [jax.ShapeDtypeStruct(a.shape[-2:], a.dtype) for a in sends], scratch, copies)


def _run_exchange(exchange, name):
    n = len(exchange.arrays)

    def body(*refs):
        parts = refs[:n], refs[n:2 * n], refs[2 * n:]
        exchange.start(*parts)
        exchange.wait(*parts)

    hbm = pl.BlockSpec(memory_space=pltpu.HBM)
    return pl.pallas_call(body, name=name, in_specs=[hbm] * n, out_specs=[hbm] * n, out_shape=exchange.out_shape,
                          scratch_shapes=exchange.scratch)(*exchange.arrays)


def _call_hosting(exchange, body, args, *, name, steps, in_specs, out_specs, out_shape, scratch_shapes=()):
    common = dict(name=name, grid=(steps,), compiler_params=_params(1))
    if exchange is None:
        return pl.pallas_call(body, in_specs=in_specs, out_specs=out_specs, out_shape=out_shape,
                              scratch_shapes=list(scratch_shapes), **common)(*args), None
    n_in, n_out, n_scratch, k = len(in_specs), len(out_specs), len(scratch_shapes), len(exchange.arrays)

    def hosting(*refs):
        ins, sends = refs[:n_in], refs[n_in:n_in + k]
        outs, recvs = refs[n_in + k:n_in + k + n_out], refs[n_in + k + n_out:n_in + 2 * k + n_out]
        scratch = refs[n_in + 2 * k + n_out:n_in + 2 * k + n_out + n_scratch]
        sems = refs[n_in + 2 * k + n_out + n_scratch:]
        pl.when(pl.program_id(0) == 0)(lambda: exchange.start(sends, recvs, sems))
        body(*ins, *outs, *scratch)
        pl.when(pl.program_id(0) == steps - 1)(lambda: exchange.wait(sends, recvs, sems))

    hbm = pl.BlockSpec(memory_space=pltpu.HBM)
    results = pl.pallas_call(
        hosting, in_specs=list(in_specs) + [hbm] * k, out_specs=list(out_specs) + [hbm] * k,
        out_shape=list(out_shape) + exchange.out_shape, scratch_shapes=list(scratch_shapes) + exchange.scratch, **common,
    )(*args, *exchange.arrays)
    return results[:n_out], results[n_out:]


def _exchange_chips(sends, per_dest, name):
    return _run_exchange(_chips_exchange(sends, per_dest), name)


def _swap_cores(sends, per_dest, name):
    return _run_exchange(_cores_exchange(sends, per_dest), name)


def _all_gather(arrays, name):
    by_chip = _exchange_chips(arrays, False, name + "_chips")
    others = _swap_cores([r.reshape(-1, r.shape[-1]) for r in by_chip], False, name + "_cores")
    return [(m, o.reshape(m.shape)) for m, o in zip(by_chip, others)]


def _in_device_order(mine, other, axis):
    first = lax.axis_index("c") == 0
    pieces = []
    for m, o in zip(mine, other):
        pieces += [jnp.where(first, m, o), jnp.where(first, o, m)]
    return jnp.concatenate(pieces, axis=axis)


def _sum_core(send, got, out_dtype, name):
    _, n, cols = send.shape
    rows = min(n, 256)
    while n % rows:
        rows -= 16

    def body(c_ref, keep_ref, got_ref, o_ref):
        o_ref[...] = (keep_ref[...].astype(F32) + got_ref[...].astype(F32)).astype(out_dtype)

    return pl.pallas_call(
        body, name=name, out_shape=jax.ShapeDtypeStruct((n, cols), out_dtype),
        grid_spec=pltpu.PrefetchScalarGridSpec(
            num_scalar_prefetch=1, grid=(n // rows,),
            in_specs=[pl.BlockSpec((None, rows, cols), lambda i, c: (c[0], i, 0)),
                      pl.BlockSpec((rows, cols), lambda i, c: (i, 0))],
            out_specs=pl.BlockSpec((rows, cols), lambda i, c: (i, 0))),
        compiler_params=_params(1),
    )(lax.axis_index("c").astype(jnp.int32).reshape(1), send, got)


def _sum_parts(parts, out_dtype, name):
    n, cols = parts[0].shape
    rows = min(n, 256)
    while n % rows:
        rows -= 16

    def body(*refs):
        acc = refs[0][...].astype(F32)
        for ref in refs[1:-1]:
            acc = acc + ref[...].astype(F32)
        refs[-1][...] = acc.astype(out_dtype)

    return pl.pallas_call(
        body, name=name, grid=(n // rows,),
        in_specs=[_row_spec(rows, cols)] * len(parts),
        out_specs=_row_spec(rows, cols),
        out_shape=jax.ShapeDtypeStruct((n, cols), out_dtype),
        compiler_params=_params(1),
    )(*parts)


def _reduce_scatter(sends, wire_dtypes, name):
    halves = [s.reshape(N_CORES, N_CHIPS * s.shape[2], s.shape[3]) for s in sends]
    gots = _swap_cores(halves, True, name + "_cores")
    sums = [_sum_core(h, g, dt, "%s_core_sum%d" % (name, i)).reshape((N_CHIPS,) + s.shape[2:])
            for i, (h, g, dt, s) in enumerate(zip(halves, gots, wire_dtypes, sends))]
    return _exchange_chips(sums, True, name + "_chips")


def _adamw(parts, w, m, v, name):
    n, cols = w.shape
    k = parts.shape[0]
    rows = min(n, 256)
    while n % rows:
        rows -= 8
    c1 = 1.0 - ADAM_B1 ** ADAM_STEP
    c2 = 1.0 - ADAM_B2 ** ADAM_STEP

    def body(p_ref, w_ref, m_ref, v_ref, g_ref, d_ref, nm_ref, nv_ref):
        g = p_ref[0].astype(F32)
        for s in range(1, k):
            g = g + p_ref[s].astype(F32)
        nm = ADAM_B1 * m_ref[...] + (1.0 - ADAM_B1) * g
        nv = ADAM_B2 * v_ref[...] + (1.0 - ADAM_B2) * (g * g)
        g_ref[...] = g
        nm_ref[...] = nm
        nv_ref[...] = nv
        d_ref[...] = -ADAM_LR * ((nm / c1) / (jnp.sqrt(nv / c2) + ADAM_EPS) + ADAM_WD * w_ref[...])

    blk = _row_spec(rows, cols)
    return pl.pallas_call(
        body, name=name, grid=(n // rows,),
        in_specs=[pl.BlockSpec((k, rows, cols), lambda i: (0, i, 0)), blk, blk, blk],
        out_specs=[blk] * 4,
        out_shape=[jax.ShapeDtypeStruct((n, cols), F32)] * 4,
        compiler_params=_params(1),
    )(parts, w, m, v)


SSM_KEYS = ("norm", "w_in", "a_re", "a_im", "log_step", "b_re", "b_im", "c_re", "c_im", "d", "w_glu", "b_glu", "w_out")
ATTN_KEYS = ("norm", "w_in", "sinks", "w_out")
LAYER_KEYS = (SSM_KEYS, ATTN_KEYS, SSM_KEYS, ATTN_KEYS)
BIG_KEYS = ("w_in", "w_glu", "w_out")
ATTN_SPLITS = (Q_DIM, KV_DIM, KV_DIM, D_MODEL)


def _rope_tables(t):
    pos = jnp.arange(t, dtype=F32)
    inv_freq = ROPE_THETA ** (-jnp.arange(0, HEAD_DIM, 2, dtype=F32) / HEAD_DIM)
    ang = pos[:, None] * inv_freq[None, :]
    cos, sin = jnp.cos(ang), jnp.sin(ang)
    return jnp.tile(jnp.concatenate([cos, cos], axis=1), (1, 2)), jnp.tile(jnp.concatenate([-sin, sin], axis=1), (1, 2))


def _gather_ici_stage(gather):
    return None if gather is None else _chips_exchange(gather, False)


def _gather_d2d_stage(by_chip):
    return None if by_chip is None else _cores_exchange([r.reshape(-1, r.shape[-1]) for r in by_chip], False)


def _gathered(by_chip, others):
    return None if by_chip is None else [(m, other.reshape(m.shape)) for m, other in zip(by_chip, others)]


def _scatter_d2d_stage(scatter):
    if scatter is None:
        return None, None
    halves = [s.reshape(N_CORES, N_CHIPS * s.shape[2], s.shape[3]) for s in scatter[0]]
    return halves, _cores_exchange(halves, True)


def _scatter_ici_stage(scatter, halves, gots, tag):
    if scatter is None:
        return None
    sums = [_sum_core(h, g, dt, "%sscatter_core_sum%d" % (tag, j)).reshape((N_CHIPS,) + s.shape[2:])
            for j, (h, g, dt, s) in enumerate(zip(halves, gots, scatter[1], scatter[0]))]
    return _chips_exchange(sums, True)


def _ssm_layer_fwd(i, x, p, w, gather=None):
    tag = "l%d_" % i
    mats, mats_vjp = jax.vjp(_s5_matrices, p["a_re"], p["a_im"], p["log_step"], p["b_re"], p["b_im"], p["c_re"], p["c_im"])
    kern, cpt, bpt, ar, ai = mats
    tmt = _s5_toeplitz(kern.astype(BF16))
    mb = dict(tm=jnp.swapaxes(tmt, 1, 2), tmt=tmt, cpt=cpt.astype(BF16),
              cp=jnp.swapaxes(cpt, 1, 2).astype(BF16), bpt=bpt.astype(BF16), bp=jnp.swapaxes(bpt, 1, 2).astype(BF16))
    (u, gate), by_chip = _inproj_fwd(x, p["norm"], w["w_in"], (D_MODEL, D_MODEL), (F32, BF16), None, tag + "inproj_fwd",
                                     _gather_ici_stage(gather))
    xre, xim = _s5_project(u, mb["bpt"], tag + "s5_block_inputs")
    hre, him = _s5_scan_fwd(xre, xim, ar, ai, tag + "s5_scan_fwd")
    y = _s5_outputs(u, hre, him, mb["tm"], mb["cpt"], p["d"], tag + "s5_outputs")
    xn, others = _ssm_out_fwd(y, gate, x, w["w_glu"], p["b_glu"], w["w_out"], tag + "out_fwd", _gather_d2d_stage(by_chip))
    return xn, (x, u, gate, y, hre, him, mb, ar, ai, mats_vjp), _gathered(by_chip, others)


def _ssm_layer_bwd(i, dxn, saved, p, w, scatter=None):
    tag = "l%d_" % i
    x, u, gate, y, hre, him, mb, ar, ai, mats_vjp = saved
    halves, d2d_stage = _scatter_d2d_stage(scatter)
    (dy, dgate, dw_glu, db_glu, dw_out), gots = _ssm_out_bwd(dxn, y, gate, w["w_glu"], p["b_glu"], w["w_out"], tag + "out_bwd",
                                                             d2d_stage)
    dhre, dhim = _s5_project(dy, mb["cp"], tag + "s5_state_grads")
    dxre, dxim, dar, dai = _s5_scan_bwd(dhre, dhim, hre, him, ar, ai, tag + "s5_scan_bwd")
    (du, dk, dcpt, dbpt, dd), parts = _s5_backward(dy, u, hre, him, dxre, dxim, mb["tmt"], mb["bp"], p["d"],
                                                   tag + "s5_backward", _scatter_ici_stage(scatter, halves, gots, tag))
    dk = dk.reshape(SSM_GROUPS, S5_BLOCK, SSM_GROUP, SSM_GROUP)
    da_re, da_im, dlog_step, db_re, db_im, dc_re, dc_im = mats_vjp((dk, dcpt, dbpt, dar, dai))
    dx, dw_in, dnorm = _inproj_bwd(x, p["norm"], w["w_in"], [du, dgate], dxn, tag + "inproj_bwd")
    grads = dict(norm=dnorm.reshape(D_MODEL), w_in=dw_in, a_re=da_re, a_im=da_im, log_step=dlog_step, b_re=db_re,
                 b_im=db_im, c_re=dc_re, c_im=dc_im, d=dd.reshape(D_MODEL), w_glu=dw_glu, b_glu=db_glu.reshape(D_MODEL),
                 w_out=dw_out)
    return dx, grads, parts


def _attn_layer_fwd(i, x, p, w, rope, gather=None):
    tag = "l%d_" % i
    (q, k, v, gate), _ = _inproj_fwd(x, p["norm"], w["w_in"], ATTN_SPLITS, (BF16, BF16, BF16, BF16), rope, tag + "inproj_fwd")
    o, by_chip = _attn_fwd(q, k, v, p["sinks"], tag + "attn_fwd", _gather_ici_stage(gather))
    xn, others = _attn_out_fwd(o, gate, x, w["w_out"], tag + "out_fwd", _gather_d2d_stage(by_chip))
    return xn, (x, q, k, v, gate, o), _gathered(by_chip, others)


def _attn_layer_bwd(i, dxn, saved, p, w, rope, scatter=None):
    tag = "l%d_" % i
    x, q, k, v, gate, o = saved
    halves, d2d_stage = _scatter_d2d_stage(scatter)
    (do, dgate, dw_out), gots = _attn_out_bwd(dxn, o, gate, w["w_out"], tag + "out_bwd", d2d_stage)
    (dq, dk, dv, dsinks), parts = _attn_bwd(q, k, v, o, do, p["sinks"], rope, tag + "attn_bwd",
                                            _scatter_ici_stage(scatter, halves, gots, tag))
    dx, dw_in, dnorm = _inproj_bwd(x, p["norm"], w["w_in"], [dq, dk, dv, dgate], dxn, tag + "inproj_bwd")
    grads = dict(norm=dnorm.reshape(D_MODEL), w_in=dw_in, sinks=dsinks.reshape(N_Q_HEADS, ATTN_BLOCK).sum(axis=1), w_out=dw_out)
    return dx, grads, parts


def _local_step(x, target, small, big, carried=()):
    rope = _rope_tables(x.shape[0])
    carried = {h: rest for h, *rest in carried}
    big = list(big)
    saved = []
    for i in range(4):
        gather = carried[i][0] if i in carried else None
        if i % 2 == 0:
            x, s, gathered = _ssm_layer_fwd(i, x, small[i], big[i], gather)
        else:
            x, s, gathered = _attn_layer_fwd(i, x, small[i], big[i], rope, gather)
        if gathered is not None:
            for layer, matrices in carried[i][1](gathered).items():
                big[layer] = matrices
        saved.append(s)
    loss, dx, dfinal = _loss_head(x, small[4]["norm"], target, "loss_head")
    grads = [None] * 4 + [dict(norm=dfinal.reshape(D_MODEL))]
    parts = {}
    for i in (3, 2, 1, 0):
        scatter = carried[i][2](grads) if i in carried else None
        if i % 2 == 0:
            dx, grads[i], parts[i] = _ssm_layer_bwd(i, dx, saved[i], small[i], big[i], scatter)
        else:
            dx, grads[i], parts[i] = _attn_layer_bwd(i, dx, saved[i], small[i], big[i], rope, scatter)
    return loss[0, 0], dx, grads, parts


def _owner_major(key, g):
    if key == "w_in":
        return g.reshape(D_MODEL, N_CHIPS, N_CORES, -1).transpose(2, 1, 0, 3)
    return g.reshape(N_CHIPS, N_CORES, -1, D_MODEL).transpose(1, 0, 2, 3)


def _from_gathered(key, mine, other):
    return _in_device_order(list(mine), list(other), 1 if key == "w_in" else 0)


SMALL_ROWS = 72
SMALL_TILE = 8 * D_MODEL


def _whole_tiles(a):
    return a.size % SMALL_TILE == 0


def _stack_small(arrays):
    rest = [a.reshape(-1) for a in arrays if not _whole_tiles(a)]
    rest = jnp.concatenate([jnp.pad(r, (0, -r.shape[0] % D_MODEL)) for r in rest])
    rest = jnp.pad(rest, (0, -rest.shape[0] % SMALL_TILE)).reshape(-1, D_MODEL)
    rows = jnp.concatenate([a.reshape(-1, D_MODEL) for a in arrays if _whole_tiles(a)] + [rest], axis=0)
    assert rows.shape[0] <= N_DEV * SMALL_ROWS
    return jnp.pad(rows, ((0, N_DEV * SMALL_ROWS - rows.shape[0]), (0, 0)))


def _unstack_small(stacked, like):
    tile_rows = sum(a.size for a in like if _whole_tiles(a)) // D_MODEL
    rest = stacked[tile_rows:].reshape(-1)
    out, at_row, at_rest = [], 0, 0
    for a in like:
        if _whole_tiles(a):
            out.append(stacked[at_row:at_row + a.size // D_MODEL].reshape(a.shape))
            at_row += a.size // D_MODEL
        else:
            out.append(rest[at_rest:at_rest + a.size].reshape(a.shape))
            at_rest += a.size + -a.size % D_MODEL
    return out


def kernel(*args):
    names = ["x"]
    layer_names = []
    for i, keys in enumerate(LAYER_KEYS):
        layer_names += ["l%d_%s" % (i, k) for k in keys]
    layer_names.append("final_norm")
    names += layer_names + ["loss_target"] + ["m_" + n for n in layer_names] + ["v_" + n for n in layer_names]
    given = dict(zip(names, args))
    big_names = [n for n in layer_names if n.split("_", 1)[1] in BIG_KEYS]
    small_names = [n for n in layer_names if n not in big_names]

    offsets = {}

    def families_of(layers):
        families = {}
        for n in big_names:
            if int(n[1]) in layers:
                family = families.setdefault(given[n].shape[1], [])
                offsets[n] = sum(given[other].shape[0] for other in family)
                family.append(n)
        return list(families.values())

    first, carried_by = families_of((0,)), {0: families_of((1,)), 1: families_of((2, 3))}
    stack = lambda pre, family: jnp.concatenate([given[pre + n] for n in family], axis=0)
    rows_of = lambda a, n: a[..., offsets[n]:offsets[n] + given[n].shape[0], :]
    local = lambda families: [stack("", family).astype(BF16) for family in families]

    def assemble(families, gathered):
        big = {}
        for family, (mine, other) in zip(families, gathered):
            for n in family:
                matrices = big.setdefault(int(n[1]), {})
                matrices[n.split("_", 1)[1]] = _from_gathered(n.split("_", 1)[1], rows_of(mine, n), rows_of(other, n))
        return big

    def sends_of(families, grads):
        return [jnp.concatenate([_owner_major(n.split("_", 1)[1], grads[int(n[1])][n.split("_", 1)[1]]) for n in family], axis=2)
                for family in families]

    small = [dict() for _ in range(5)]
    for n in small_names:
        if n == "final_norm":
            small[4]["norm"] = given[n]
        else:
            small[int(n[1])][n.split("_", 1)[1]] = given[n]

    big = [assemble(first, _all_gather(local(first), "gather_first_weights"))[0], None, None, None]
    carried = [(h, local(families), functools.partial(assemble, families),
                lambda grads, families=families: (sends_of(families, grads), [BF16] * len(families)))
               for h, families in carried_by.items()]
    loss, dx, grads, carried_parts = _local_step(given["x"][0], given["loss_target"][0], small, big, carried)
    loss = lax.psum(loss, ("x", "y", "c"))

    def grad_of(n):
        return grads[4]["norm"] if n == "final_norm" else grads[int(n[1])][n.split("_", 1)[1]]

    flat = lambda f: _stack_small([f(n) for n in small_names])
    sends = sends_of(first, grads) + [flat(grad_of).reshape(N_CORES, N_CHIPS, SMALL_ROWS, D_MODEL)]
    parts = _reduce_scatter(sends, [BF16] * len(first) + [F32], "scatter_grads")

    outs = {}
    tags = ("grad_", "delta_", "new_m_", "new_v_")
    all_families = first + carried_by[0] + carried_by[1]
    all_parts = list(parts[:-1]) + list(carried_parts[0]) + list(carried_parts[1])
    for i, (family, part) in enumerate(zip(all_families, all_parts)):
        results = _adamw(part, stack("", family), stack("m_", family), stack("v_", family), "adamw_matrices%d" % i)
        for tag, a in zip(tags, results):
            for n in family:
                outs[tag + n] = rows_of(a, n)

    my_slice = _sum_parts([parts[-1][s] for s in range(N_CHIPS)], F32, "sum_small_grads")
    mine, other = _all_gather([my_slice], "gather_small_grads")[0]
    first = lax.axis_index("c") == 0
    g_all = jnp.concatenate([jnp.where(first, mine, other), jnp.where(first, other, mine)], axis=0)
    g_all = g_all.reshape(1, N_DEV * SMALL_ROWS, D_MODEL)
    results = _adamw(g_all, flat(lambda n: given[n]), flat(lambda n: given["m_" + n]), flat(lambda n: given["v_" + n]),
                     "adamw_small")
    for tag, a in zip(tags, results):
        for n, piece in zip(small_names, _unstack_small(a, [given[n] for n in small_names])):
            outs[tag + n] = piece
    result = [loss, dx[None]]
    for tag in ("grad_", "delta_", "new_m_", "new_v_"):
        result += [outs[tag + n] for n in layer_names]
    return tuple(result)
```

```python
import functools
import math

import jax
import jax.numpy as jnp
from jax import lax
from jax.experimental import pallas as pl
from jax.experimental.pallas import tpu as pltpu

F32 = jnp.float32
BF16 = jnp.bfloat16

D_MODEL = 1024
SSM_GROUP = 16
SSM_GROUPS = D_MODEL // SSM_GROUP
SSM_STATE = 64
S5_BLOCK = 16
S5_LANES = S5_BLOCK * SSM_GROUP
HEAD_DIM = 64
N_Q_HEADS = 16
N_KV_HEADS = 2
GQA = N_Q_HEADS // N_KV_HEADS
Q_DIM = N_Q_HEADS * HEAD_DIM
KV_DIM = N_KV_HEADS * HEAD_DIM
ATTN_BLOCK = 128
ROPE_THETA = 10000.0
NORM_EPS = 1e-5
NEG_INF = -1e30
ATTN_SCALE = HEAD_DIM ** -0.5
N_DEV = 8

ADAM_LR = 0.001
ADAM_B1 = 0.9
ADAM_B2 = 0.999
ADAM_EPS = 1e-08
ADAM_WD = 0.01
ADAM_STEP = 10

VMEM_LIMIT = 56 * 1024 * 1024
ROWS_FWD = 512
ROWS_BWD = 512

NT = (((1,), (1,)), ((), ()))
TN = (((0,), (0,)), ((), ()))


def _params(n_grid):
    return pltpu.CompilerParams(dimension_semantics=("arbitrary",) * n_grid, vmem_limit_bytes=VMEM_LIMIT)


def _dot(a, b):
    return jnp.dot(a, b, preferred_element_type=F32)


def _dot_nt(a, b):
    return lax.dot_general(a, b, NT, preferred_element_type=F32)


def _dot_tn(a, b):
    return lax.dot_general(a, b, TN, preferred_element_type=F32)


def _sigmoid(x):
    return 1.0 / (1.0 + jnp.exp(-x))


_GELU_K = math.sqrt(2.0 / math.pi)


def _gelu(x):
    return x * (0.5 * (1.0 + jnp.tanh(_GELU_K * (x + 0.044715 * (x * x * x)))))


def _gelu_grad(x):
    t = jnp.tanh(_GELU_K * (x + 0.044715 * (x * x * x)))
    return 0.5 * (1.0 + t) + 0.5 * x * (1.0 - t * t) * (_GELU_K * (1.0 + 3.0 * 0.044715 * (x * x)))


def _row_spec(rows, cols):
    return pl.BlockSpec((rows, cols), lambda i: (i, 0))


def _const_spec(shape):
    zeros = (0,) * len(shape)
    return pl.BlockSpec(shape, lambda i: zeros, pipeline_mode=pl.Buffered(1))


def _rope_apply(t, cos, sin_signed, sign):
    lane = lax.broadcasted_iota(jnp.int32, (1, 128), 1)
    first_half = (lane % HEAD_DIM) < (HEAD_DIM // 2)
    out = []
    for j in range(t.shape[1] // 128):
        tj = t[:, 128 * j:128 * (j + 1)]
        partner = jnp.where(first_half, pltpu.roll(tj, 128 - HEAD_DIM // 2, 1), pltpu.roll(tj, HEAD_DIM // 2, 1))
        out.append(tj * cos + sign * (partner * sin_signed))
    return out[0] if len(out) == 1 else jnp.concatenate(out, axis=1)


def _inproj_fwd(x, norm, w, splits, dtypes, rope, name, exchange=None):
    t = x.shape[0]
    n = w.shape[1]
    rows = ROWS_FWD

    def body(*refs):
        if rope is None:
            x_ref, n_ref, w_ref = refs[:3]
            outs = refs[3:]
        else:
            x_ref, n_ref, w_ref, cos_ref, sin_ref = refs[:5]
            outs = refs[5:]
        xv = x_ref[...]
        rstd = lax.rsqrt(jnp.mean(xv * xv, axis=-1, keepdims=True) + NORM_EPS)
        h = (xv * rstd) * n_ref[...]
        proj = _dot(h.astype(BF16), w_ref[...])
        off = 0
        for i, width in enumerate(splits):
            piece = proj[:, off:off + width]
            if rope is not None and i < 2:
                piece = _rope_apply(piece, cos_ref[...], sin_ref[...], 1.0)
            outs[i][...] = piece.astype(dtypes[i])
            off += width

    in_specs = [_row_spec(rows, D_MODEL), _const_spec((1, D_MODEL)), _const_spec((D_MODEL, n))]
    args = [x, norm.reshape(1, D_MODEL), w]
    if rope is not None:
        in_specs += [_row_spec(rows, 128), _row_spec(rows, 128)]
        args += list(rope)
    return _call_hosting(
        exchange, body, args, name=name, steps=t // rows, in_specs=in_specs,
        out_specs=[_row_spec(rows, width) for width in splits],
        out_shape=[jax.ShapeDtypeStruct((t, width), dtype) for width, dtype in zip(splits, dtypes)])


def _inproj_bwd(x, norm, w, dpieces, dxn, name):
    t = x.shape[0]
    n = w.shape[1]
    rows = ROWS_BWD
    widths = [p.shape[1] for p in dpieces]
    k = len(dpieces)

    def body(*refs):
        x_ref, n_ref, w_ref, dxn_ref = refs[:4]
        d_refs = refs[4:4 + k]
        dx_ref, dw_ref, dn_ref = refs[4 + k:]
        @pl.when(pl.program_id(0) == 0)
        def _():
            dw_ref[...] = jnp.zeros_like(dw_ref)
            dn_ref[...] = jnp.zeros_like(dn_ref)

        xv = x_ref[...]
        rstd = lax.rsqrt(jnp.mean(xv * xv, axis=-1, keepdims=True) + NORM_EPS)
        xhat = xv * rstd
        h = xhat * n_ref[...]
        dproj = [r[...].astype(BF16) for r in d_refs]
        dproj = dproj[0] if k == 1 else jnp.concatenate(dproj, axis=1)
        dh = _dot_nt(dproj, w_ref[...])
        dw_ref[...] += _dot_tn(h.astype(BF16), dproj)
        dn_ref[...] += jnp.sum(dh * xhat, axis=0, keepdims=True)
        dxhat = dh * n_ref[...]
        dx_ref[...] = rstd * (dxhat - xhat * jnp.mean(dxhat * xhat, axis=-1, keepdims=True)) + dxn_ref[...]

    return _call_hosting(
        None, body, (x, norm.reshape(1, D_MODEL), w, dxn, *dpieces), name=name, steps=t // rows,
        in_specs=[_row_spec(rows, D_MODEL), _const_spec((1, D_MODEL)), _const_spec((D_MODEL, n)),
                  _row_spec(rows, D_MODEL)] + [_row_spec(rows, width) for width in widths],
        out_specs=[_row_spec(rows, D_MODEL), _const_spec((D_MODEL, n)), _const_spec((1, D_MODEL))],
        out_shape=[jax.ShapeDtypeStruct((t, D_MODEL), F32), jax.ShapeDtypeStruct((D_MODEL, n), F32),
                   jax.ShapeDtypeStruct((1, D_MODEL), F32)])[0]


def _s5_matrices(a_re, a_im, log_step, b_re, b_im, c_re, c_im):
    r = S5_BLOCK
    step = jnp.exp(log_step)[:, None]
    lr, li = a_re * step, a_im * step
    k = jnp.arange(r + 1, dtype=F32)
    mag = jnp.exp(lr[:, None, :] * k[:, None])
    pr = mag * jnp.cos(li[:, None, :] * k[:, None])
    pi = mag * jnp.sin(li[:, None, :] * k[:, None])
    nr, ni = pr[:, 1] - 1.0, pi[:, 1]
    den = a_re * a_re + a_im * a_im
    qr, qi = (nr * a_re + ni * a_im) / den, (ni * a_re - nr * a_im) / den
    bbr = qr[..., None] * b_re - qi[..., None] * b_im
    bbi = qr[..., None] * b_im + qi[..., None] * b_re
    wr = c_re[:, None] * pr[:, :, None, :] - c_im[:, None] * pi[:, :, None, :]
    wi = c_re[:, None] * pi[:, :, None, :] + c_im[:, None] * pr[:, :, None, :]
    w = jnp.concatenate([wr, -wi], axis=-1)
    bb = jnp.concatenate([bbr, bbi], axis=1)
    kern = jnp.einsum("gxp,gpi->gxi", w[:, :r].reshape(SSM_GROUPS, S5_LANES, 2 * SSM_STATE), bb,
                      precision=lax.Precision.HIGHEST).reshape(SSM_GROUPS, r, SSM_GROUP, SSM_GROUP)
    cpt = w[:, 1:].reshape(SSM_GROUPS, S5_LANES, 2 * SSM_STATE)
    prs = jnp.swapaxes(pr[:, r - 1::-1][:, :r], 1, 2)[..., None]
    pis = jnp.swapaxes(pi[:, r - 1::-1][:, :r], 1, 2)[..., None]
    bp_re = prs * bbr[:, :, None, :] - pis * bbi[:, :, None, :]
    bp_im = prs * bbi[:, :, None, :] + pis * bbr[:, :, None, :]
    bpt = jnp.concatenate([bp_re, bp_im], axis=1).reshape(SSM_GROUPS, 2 * SSM_STATE, S5_LANES)
    ar = pr[:, r].reshape(1, SSM_GROUPS * SSM_STATE)
    ai = pi[:, r].reshape(1, SSM_GROUPS * SSM_STATE)
    return kern, cpt, bpt, ar, ai


def _s5_toeplitz(kern):
    r = S5_BLOCK
    cols = [jnp.pad(kern[:, :r - s], ((0, 0), (s, 0), (0, 0), (0, 0))) for s in range(r)]
    return jnp.stack(cols, axis=3).reshape(SSM_GROUPS, S5_LANES, S5_LANES)


S5_OCTET = 128 // SSM_GROUP
S5_STEPS = SSM_GROUPS // S5_OCTET


def _oct_spec(t):
    return pl.BlockSpec((t, 128), lambda j: (0, j))


def _state_spec(nb):
    return pl.BlockSpec((nb, S5_OCTET * SSM_STATE), lambda j: (0, j))


def _gmat_spec(a, b):
    return pl.BlockSpec((S5_OCTET, a, b), lambda j: (j, 0, 0))


def _block_rows(ref, nb):
    by_position = jnp.swapaxes(ref[...].reshape(nb, S5_BLOCK, 128), 0, 1)
    return [by_position[r] for r in range(S5_BLOCK)]


def _store_block_rows(ref, pieces, nb):
    ref[...] = jnp.swapaxes(jnp.stack(pieces, axis=0), 0, 1).reshape(nb * S5_BLOCK, 128)


def _group_cols(pieces_t, g):
    return jnp.concatenate([p[SSM_GROUP * g:SSM_GROUP * (g + 1)] for p in pieces_t], axis=0)


def _state_cols(re_t, im_t, g):
    return jnp.concatenate([re_t[SSM_STATE * g:SSM_STATE * (g + 1)], im_t[SSM_STATE * g:SSM_STATE * (g + 1)]], axis=0)


def _s5_project(a, mat, name):
    t = a.shape[0]
    nb = t // S5_BLOCK

    def body(a_ref, m_ref, re_ref, im_ref):
        at = [p.T for p in _block_rows(a_ref, nb)]
        for pair in range(S5_OCTET // 2):
            xs = [_dot(m_ref[2 * pair + k], _group_cols(at, 2 * pair + k).astype(BF16)) for k in (0, 1)]
            lanes = slice(128 * pair, 128 * (pair + 1))
            re_ref[:, lanes] = jnp.concatenate([xs[0][:SSM_STATE], xs[1][:SSM_STATE]], axis=0).T
            im_ref[:, lanes] = jnp.concatenate([xs[0][SSM_STATE:], xs[1][SSM_STATE:]], axis=0).T

    return pl.pallas_call(
        body, name=name, grid=(S5_STEPS,),
        in_specs=[_oct_spec(t), _gmat_spec(2 * SSM_STATE, S5_LANES)],
        out_specs=[_state_spec(nb), _state_spec(nb)],
        out_shape=[jax.ShapeDtypeStruct((nb, SSM_GROUPS * SSM_STATE), F32)] * 2,
        compiler_params=_params(1),
    )(a, mat)


_SCAN_LANES = 2048


def _s5_scan_fwd(xre, xim, ar, ai, name):
    nb = xre.shape[0]
    col = pl.BlockSpec((nb, _SCAN_LANES), lambda j: (0, j))
    par = pl.BlockSpec((1, _SCAN_LANES), lambda j: (0, j))

    def body(xre_ref, xim_ref, ar_ref, ai_ref, hre_ref, him_ref):
        a_r, a_i = ar_ref[...], ai_ref[...]

        def step(b, carry):
            hr, hi = carry
            hre_ref[pl.ds(b, 1), :] = hr
            him_ref[pl.ds(b, 1), :] = hi
            xr, xi = xre_ref[pl.ds(b, 1), :], xim_ref[pl.ds(b, 1), :]
            return a_r * hr - a_i * hi + xr, a_r * hi + a_i * hr + xi

        zero = jnp.zeros((1, _SCAN_LANES), F32)
        lax.fori_loop(0, nb, step, (zero, zero))

    return pl.pallas_call(
        body, name=name, grid=(xre.shape[1] // _SCAN_LANES,),
        in_specs=[col, col, par, par], out_specs=[col, col],
        out_shape=[jax.ShapeDtypeStruct(xre.shape, F32)] * 2,
        compiler_params=_params(1),
    )(xre, xim, ar, ai)


def _s5_scan_bwd(dhre, dhim, hre, him, ar, ai, name):
    nb = dhre.shape[0]
    col = pl.BlockSpec((nb, _SCAN_LANES), lambda j: (0, j))
    par = pl.BlockSpec((1, _SCAN_LANES), lambda j: (0, j))

    def body(dhre_ref, dhim_ref, hre_ref, him_ref, ar_ref, ai_ref, dxre_ref, dxim_ref, dar_ref, dai_ref):
        a_r, a_i = ar_ref[...], ai_ref[...]

        def step(s, carry):
            gr, gi, dar, dai = carry
            b = nb - 1 - s
            dxre_ref[pl.ds(b, 1), :] = gr
            dxim_ref[pl.ds(b, 1), :] = gi
            hr, hi = hre_ref[pl.ds(b, 1), :], him_ref[pl.ds(b, 1), :]
            dar = dar + (hr * gr + hi * gi)
            dai = dai + (hr * gi - hi * gr)
            dr, di = dhre_ref[pl.ds(b, 1), :], dhim_ref[pl.ds(b, 1), :]
            return dr + (a_r * gr + a_i * gi), di + (a_r * gi - a_i * gr), dar, dai

        zero = jnp.zeros((1, _SCAN_LANES), F32)
        _, _, dar, dai = lax.fori_loop(0, nb, step, (zero, zero, zero, zero))
        dar_ref[...] = dar
        dai_ref[...] = dai

    return pl.pallas_call(
        body, name=name, grid=(dhre.shape[1] // _SCAN_LANES,),
        in_specs=[col, col, col, col, par, par], out_specs=[col, col, par, par],
        out_shape=[jax.ShapeDtypeStruct(dhre.shape, F32)] * 2 + [jax.ShapeDtypeStruct(ar.shape, F32)] * 2,
        compiler_params=_params(1),
    )(dhre, dhim, hre, him, ar, ai)


def _s5_outputs(u, hre, him, tm, cpt, d, name):
    t = u.shape[0]
    nb = t // S5_BLOCK

    def body(u_ref, hre_ref, him_ref, tm_ref, cpt_ref, d_ref, y_ref):
        u_rows = _block_rows(u_ref, nb)
        ut = [p.T for p in u_rows]
        hre_t, him_t = hre_ref[...].T, him_ref[...].T
        yts = []
        for g in range(S5_OCTET):
            yts.append(_dot(tm_ref[g], _group_cols(ut, g).astype(BF16))
                       + _dot(cpt_ref[g], _state_cols(hre_t, him_t, g).astype(BF16)))
        y_rows = []
        for r in range(S5_BLOCK):
            rows = jnp.concatenate([yt[SSM_GROUP * r:SSM_GROUP * (r + 1)] for yt in yts], axis=0)
            y_rows.append(rows.T + d_ref[...] * u_rows[r])
        _store_block_rows(y_ref, y_rows, nb)

    return pl.pallas_call(
        body, name=name, grid=(S5_STEPS,),
        in_specs=[_oct_spec(t), _state_spec(nb), _state_spec(nb), _gmat_spec(S5_LANES, S5_LANES),
                  _gmat_spec(S5_LANES, 2 * SSM_STATE), _oct_spec(1)],
        out_specs=_oct_spec(t),
        out_shape=jax.ShapeDtypeStruct(u.shape, F32),
        compiler_params=_params(1),
    )(u, hre, him, tm, cpt, d.reshape(1, D_MODEL))


def _s5_backward(dy, u, hre, him, dxre, dxim, tmt, bp, d, name, exchange=None):
    t = u.shape[0]
    nb = t // S5_BLOCK

    def body(dy_ref, u_ref, hre_ref, him_ref, dxre_ref, dxim_ref, tmt_ref, bp_ref, d_ref,
             du_ref, dk_ref, dcpt_ref, dbpt_ref, dd_ref, dtm_scratch):
        dy_rows, u_rows = _block_rows(dy_ref, nb), _block_rows(u_ref, nb)
        dyt, ut = [p.T for p in dy_rows], [p.T for p in u_rows]
        hre_t, him_t = hre_ref[...].T, him_ref[...].T
        dxre_t, dxim_t = dxre_ref[...].T, dxim_ref[...].T
        duts = []
        for g in range(S5_OCTET):
            dyg, ug = _group_cols(dyt, g).astype(BF16), _group_cols(ut, g).astype(BF16)
            hg = _state_cols(hre_t, him_t, g).astype(BF16)
            dxg = _state_cols(dxre_t, dxim_t, g).astype(BF16)
            duts.append(_dot(tmt_ref[g], dyg) + _dot(bp_ref[g], dxg))
            dtm_scratch[...] = _dot_nt(dyg, ug)
            dk = dtm_scratch[:, :SSM_GROUP]
            for s in range(1, S5_BLOCK):
                below = dtm_scratch[SSM_GROUP * s:, SSM_GROUP * s:SSM_GROUP * (s + 1)]
                dk = dk + jnp.concatenate([below, jnp.zeros((SSM_GROUP * s, SSM_GROUP), F32)], axis=0)
            dk_ref[g] = dk
            dcpt_ref[g] = _dot_nt(dyg, hg)
            dbpt_ref[g] = _dot_nt(dxg, ug)
        dd = jnp.zeros((1, 128), F32)
        du_rows = []
        for r in range(S5_BLOCK):
            rows = jnp.concatenate([dut[SSM_GROUP * r:SSM_GROUP * (r + 1)] for dut in duts], axis=0)
            du_rows.append(rows.T + d_ref[...] * dy_rows[r])
            dd = dd + jnp.sum(dy_rows[r] * u_rows[r], axis=0, keepdims=True)
        _store_block_rows(du_ref, du_rows, nb)
        dd_ref[...] = dd

    return _call_hosting(
        exchange, body, (dy, u, hre, him, dxre, dxim, tmt, bp, d.reshape(1, D_MODEL)), name=name, steps=S5_STEPS,
        in_specs=[_oct_spec(t), _oct_spec(t), _state_spec(nb), _state_spec(nb), _state_spec(nb), _state_spec(nb),
                  _gmat_spec(S5_LANES, S5_LANES), _gmat_spec(S5_LANES, 2 * SSM_STATE), _oct_spec(1)],
        out_specs=[_oct_spec(t), _gmat_spec(S5_LANES, SSM_GROUP), _gmat_spec(S5_LANES, 2 * SSM_STATE),
                   _gmat_spec(2 * SSM_STATE, S5_LANES), _oct_spec(1)],
        out_shape=[jax.ShapeDtypeStruct(u.shape, F32),
                   jax.ShapeDtypeStruct((SSM_GROUPS, S5_LANES, SSM_GROUP), F32),
                   jax.ShapeDtypeStruct((SSM_GROUPS, S5_LANES, 2 * SSM_STATE), F32),
                   jax.ShapeDtypeStruct((SSM_GROUPS, 2 * SSM_STATE, S5_LANES), F32),
                   jax.ShapeDtypeStruct((1, D_MODEL), F32)],
        scratch_shapes=[pltpu.VMEM((S5_LANES, S5_LANES), F32)])


def _ssm_out_fwd(y, gate, x, w_glu, b_glu, w_out, name, exchange=None):
    t = x.shape[0]
    rows = ROWS_FWD

    def body(y_ref, g_ref, x_ref, wg_ref, bg_ref, wo_ref, o_ref):
        z0 = _gelu(y_ref[...])
        s = _dot(z0.astype(BF16), wg_ref[...]) + bg_ref[...]
        gate_v = g_ref[...].astype(F32)
        a = (z0 * _sigmoid(s)) * (gate_v * _sigmoid(gate_v))
        o_ref[...] = x_ref[...] + _dot(a.astype(BF16), wo_ref[...])

    (xn,), got = _call_hosting(
        exchange, body, (y, gate, x, w_glu, b_glu.reshape(1, D_MODEL), w_out), name=name, steps=t // rows,
        in_specs=[_row_spec(rows, D_MODEL)] * 3 + [_const_spec((D_MODEL, D_MODEL)), _const_spec((1, D_MODEL)),
                                                   _const_spec((D_MODEL, D_MODEL))],
        out_specs=[_row_spec(rows, D_MODEL)],
        out_shape=[jax.ShapeDtypeStruct((t, D_MODEL), F32)])
    return xn, got


def _ssm_out_bwd(dxn, y, gate, w_glu, b_glu, w_out, name, exchange=None):
    t = y.shape[0]
    rows = ROWS_BWD

    def body(dxn_ref, y_ref, g_ref, wg_ref, bg_ref, wo_ref, dy_ref, dg_ref, dwg_ref, dbg_ref, dwo_ref):
        @pl.when(pl.program_id(0) == 0)
        def _():
            dwo_ref[...] = jnp.zeros_like(dwo_ref)
            dwg_ref[...] = jnp.zeros_like(dwg_ref)
            dbg_ref[...] = jnp.zeros_like(dbg_ref)

        yv = y_ref[...]
        z0 = _gelu(yv)
        z0b = z0.astype(BF16)
        sg = _sigmoid(_dot(z0b, wg_ref[...]) + bg_ref[...])
        z = z0 * sg
        gate_v = g_ref[...].astype(F32)
        sgg = _sigmoid(gate_v)
        silu = gate_v * sgg
        dob = dxn_ref[...].astype(BF16)
        da = _dot_nt(dob, wo_ref[...])
        dwo_ref[...] += _dot_tn((z * silu).astype(BF16), dob)
        dz = da * silu
        dg_ref[...] = (da * z * (sgg * (1.0 + gate_v * (1.0 - sgg)))).astype(BF16)
        ds = dz * z0 * (sg * (1.0 - sg))
        dsb = ds.astype(BF16)
        dz0 = dz * sg + _dot_nt(dsb, wg_ref[...])
        dwg_ref[...] += _dot_tn(z0b, dsb)
        dbg_ref[...] += jnp.sum(ds, axis=0, keepdims=True)
        dy_ref[...] = dz0 * _gelu_grad(yv)

    sq = _const_spec((D_MODEL, D_MODEL))
    vec = _const_spec((1, D_MODEL))
    return _call_hosting(
        exchange, body, (dxn, y, gate, w_glu, b_glu.reshape(1, D_MODEL), w_out), name=name, steps=t // rows,
        in_specs=[_row_spec(rows, D_MODEL)] * 3 + [sq, vec, sq],
        out_specs=[_row_spec(rows, D_MODEL), _row_spec(rows, D_MODEL), sq, vec, sq],
        out_shape=[jax.ShapeDtypeStruct((t, D_MODEL), F32), jax.ShapeDtypeStruct((t, D_MODEL), BF16),
                   jax.ShapeDtypeStruct((D_MODEL, D_MODEL), F32), jax.ShapeDtypeStruct((1, D_MODEL), F32),
                   jax.ShapeDtypeStruct((D_MODEL, D_MODEL), F32)])


KV_LANES = GQA * ATTN_BLOCK


def _attn_bias(block_is_first):
    kj = lax.broadcasted_iota(jnp.int32, (2 * ATTN_BLOCK, ATTN_BLOCK), 0)
    qi = lax.broadcasted_iota(jnp.int32, (2 * ATTN_BLOCK, ATTN_BLOCK), 1)
    dist = qi + ATTN_BLOCK - kj
    valid = (dist >= 0) & (dist < ATTN_BLOCK) & (jnp.logical_not(block_is_first) | (kj >= ATTN_BLOCK))
    return jnp.tile(jnp.where(valid, 0.0, NEG_INF).astype(F32), (1, GQA))


def _head_cols(a_t, kvh):
    heads = range(kvh * GQA, (kvh + 1) * GQA)
    return jnp.concatenate([a_t[HEAD_DIM * h:HEAD_DIM * (h + 1)] for h in heads], axis=1)


def _head_rows(a_cols):
    stacked = jnp.concatenate([a_cols[:, ATTN_BLOCK * g:ATTN_BLOCK * (g + 1)] for g in range(GQA)], axis=0)
    return stacked.T


def _kv_rows(prev_ref, cur_ref, kvh):
    lanes = slice(HEAD_DIM * kvh, HEAD_DIM * (kvh + 1))
    return jnp.concatenate([prev_ref[:, lanes], cur_ref[:, lanes]], axis=0).astype(BF16)


def _kv_cols(prev_t, cur_t, kvh):
    rows = slice(HEAD_DIM * kvh, HEAD_DIM * (kvh + 1))
    return jnp.concatenate([prev_t[rows], cur_t[rows]], axis=1).astype(BF16)


def _attn_probs(kk, q_cols, sink_row, bias):
    s = _dot(kk, q_cols) * ATTN_SCALE + bias
    m = jnp.maximum(jnp.max(s, axis=0, keepdims=True), sink_row)
    p = jnp.exp(s - m)
    e_sink = jnp.exp(sink_row - m)
    inv = 1.0 / (jnp.sum(p, axis=0, keepdims=True) + e_sink)
    return p * inv, e_sink * inv


def _sink_cols(sinks):
    return jnp.repeat(sinks, ATTN_BLOCK).reshape(N_KV_HEADS, 1, KV_LANES)


def _attn_fwd(q, k, v, sinks, name, exchange=None):
    t = q.shape[0]
    nblk = t // ATTN_BLOCK

    def body(s_ref, q_ref, kc_ref, kp_ref, vc_ref, vp_ref, o_ref):
        bias = _attn_bias(pl.program_id(0) == 0)
        q_t = q_ref[...].astype(F32).T
        vp_t, vc_t = vp_ref[...].astype(F32).T, vc_ref[...].astype(F32).T
        for kvh in range(N_KV_HEADS):
            p, _ = _attn_probs(_kv_rows(kp_ref, kc_ref, kvh), _head_cols(q_t, kvh).astype(BF16), s_ref[kvh], bias)
            o_cols = _dot(_kv_cols(vp_t, vc_t, kvh), p.astype(BF16))
            o_ref[:, GQA * HEAD_DIM * kvh:GQA * HEAD_DIM * (kvh + 1)] = _head_rows(o_cols)

    cur = lambda i: (i, 0)
    prev = lambda i: (jnp.maximum(i - 1, 0), 0)
    (o,), got = _call_hosting(
        exchange, body, (_sink_cols(sinks), q, k, k, v, v), name=name, steps=nblk,
        in_specs=[_const_spec((N_KV_HEADS, 1, KV_LANES)),
                  pl.BlockSpec((ATTN_BLOCK, Q_DIM), cur),
                  pl.BlockSpec((ATTN_BLOCK, KV_DIM), cur), pl.BlockSpec((ATTN_BLOCK, KV_DIM), prev),
                  pl.BlockSpec((ATTN_BLOCK, KV_DIM), cur), pl.BlockSpec((ATTN_BLOCK, KV_DIM), prev)],
        out_specs=[pl.BlockSpec((ATTN_BLOCK, Q_DIM), cur)],
        out_shape=[jax.ShapeDtypeStruct((t, Q_DIM), F32)])
    return o, got


def _attn_bwd(q, k, v, o, do, sinks, rope, name, exchange=None):
    t = q.shape[0]
    nblk = t // ATTN_BLOCK

    def body(s_ref, q_ref, o_ref, do_ref, kp_ref, kc_ref, vp_ref, vc_ref, cosq_ref, sinq_ref, cosk_ref, sinkey_ref,
             dq_ref, dk_ref, dv_ref, ds_ref, new_k, new_v, wait_k, wait_v, dq_rot):
        n = pl.program_id(0)

        @pl.when(n == 0)
        def _():
            ds_ref[...] = jnp.zeros_like(ds_ref)
            wait_k[...] = jnp.zeros_like(wait_k)
            wait_v[...] = jnp.zeros_like(wait_v)

        @pl.when(n < nblk)
        def _():
            bias = _attn_bias(n == 0)
            q_t, o_t, do_t = q_ref[...].astype(F32).T, o_ref[...].T, do_ref[...].T
            kp_t, kc_t = kp_ref[...].astype(F32).T, kc_ref[...].astype(F32).T
            for kvh in range(N_KV_HEADS):
                q_cols = _head_cols(q_t, kvh).astype(BF16)
                do_cols = _head_cols(do_t, kvh)
                delta = jnp.sum(do_cols * _head_cols(o_t, kvh), axis=0, keepdims=True)
                do_cols = do_cols.astype(BF16)
                p, p_sink = _attn_probs(_kv_rows(kp_ref, kc_ref, kvh), q_cols, s_ref[kvh], bias)
                dp = _dot(_kv_rows(vp_ref, vc_ref, kvh), do_cols)
                ds = (p * (dp - delta) * ATTN_SCALE).astype(BF16)
                lanes = slice(GQA * HEAD_DIM * kvh, GQA * HEAD_DIM * (kvh + 1))
                dq_rot[:, lanes] = _head_rows(_dot(_kv_cols(kp_t, kc_t, kvh), ds))
                head = slice(HEAD_DIM * kvh, HEAD_DIM * (kvh + 1))
                new_k[:, head] = _dot_nt(ds, q_cols)
                new_v[:, head] = _dot_nt(p.astype(BF16), do_cols)
                ds_ref[kvh] += -(p_sink * delta)
            dq_ref[...] = _rope_apply(dq_rot[...], cosq_ref[...], sinq_ref[...], -1.0).astype(BF16)

        @pl.when(n == nblk)
        def _():
            new_k[...] = jnp.zeros_like(new_k)
            new_v[...] = jnp.zeros_like(new_v)

        dk_ref[...] = _rope_apply(wait_k[...] + new_k[:ATTN_BLOCK], cosk_ref[...], sinkey_ref[...], -1.0).astype(BF16)
        dv_ref[...] = (wait_v[...] + new_v[:ATTN_BLOCK]).astype(BF16)
        wait_k[...] = new_k[ATTN_BLOCK:]
        wait_v[...] = new_v[ATTN_BLOCK:]

    cur = lambda i: (jnp.minimum(i, nblk - 1), 0)
    prev = lambda i: (jnp.maximum(i - 1, 0), 0)
    qs = lambda f: pl.BlockSpec((ATTN_BLOCK, Q_DIM), f)
    ks = lambda f: pl.BlockSpec((ATTN_BLOCK, KV_DIM), f)
    sink_spec = _const_spec((N_KV_HEADS, 1, KV_LANES))
    return _call_hosting(
        exchange, body, (_sink_cols(sinks), q, o, do, k, k, v, v, rope[0], rope[1], rope[0], rope[1]),
        name=name, steps=nblk + 1,
        in_specs=[sink_spec, qs(cur), qs(cur), qs(cur), ks(prev), ks(cur), ks(prev), ks(cur),
                  ks(cur), ks(cur), ks(prev), ks(prev)],
        out_specs=[qs(cur), ks(prev), ks(prev), sink_spec],
        out_shape=[jax.ShapeDtypeStruct((t, Q_DIM), BF16), jax.ShapeDtypeStruct((t, KV_DIM), BF16),
                   jax.ShapeDtypeStruct((t, KV_DIM), BF16), jax.ShapeDtypeStruct((N_KV_HEADS, 1, KV_LANES), F32)],
        scratch_shapes=[pltpu.VMEM((2 * ATTN_BLOCK, KV_DIM), F32), pltpu.VMEM((2 * ATTN_BLOCK, KV_DIM), F32),
                        pltpu.VMEM((ATTN_BLOCK, KV_DIM), F32), pltpu.VMEM((ATTN_BLOCK, KV_DIM), F32),
                        pltpu.VMEM((ATTN_BLOCK, Q_DIM), F32)])


def _attn_out_fwd(o, gate, x, w_out, name, exchange=None):
    t = x.shape[0]
    rows = ROWS_FWD

    def body(o_ref, g_ref, x_ref, wo_ref, xn_ref):
        gate_v = g_ref[...].astype(F32)
        a = o_ref[...] * (gate_v * _sigmoid(gate_v))
        xn_ref[...] = x_ref[...] + _dot(a.astype(BF16), wo_ref[...])

    (xn,), got = _call_hosting(
        exchange, body, (o, gate, x, w_out), name=name, steps=t // rows,
        in_specs=[_row_spec(rows, D_MODEL)] * 3 + [_const_spec((D_MODEL, D_MODEL))],
        out_specs=[_row_spec(rows, D_MODEL)],
        out_shape=[jax.ShapeDtypeStruct((t, D_MODEL), F32)])
    return xn, got


def _attn_out_bwd(dxn, o, gate, w_out, name, exchange=None):
    t = o.shape[0]
    rows = ROWS_BWD

    def body(dxn_ref, o_ref, g_ref, wo_ref, do_ref, dg_ref, dwo_ref):
        @pl.when(pl.program_id(0) == 0)
        def _():
            dwo_ref[...] = jnp.zeros_like(dwo_ref)

        gate_v, ov = g_ref[...].astype(F32), o_ref[...]
        sgg = _sigmoid(gate_v)
        silu = gate_v * sgg
        dob = dxn_ref[...].astype(BF16)
        da = _dot_nt(dob, wo_ref[...])
        dwo_ref[...] += _dot_tn((ov * silu).astype(BF16), dob)
        do_ref[...] = da * silu
        dg_ref[...] = (da * ov * (sgg * (1.0 + gate_v * (1.0 - sgg)))).astype(BF16)

    sq = _const_spec((D_MODEL, D_MODEL))
    return _call_hosting(
        exchange, body, (dxn, o, gate, w_out), name=name, steps=t // rows,
        in_specs=[_row_spec(rows, D_MODEL)] * 3 + [sq],
        out_specs=[_row_spec(rows, D_MODEL), _row_spec(rows, D_MODEL), sq],
        out_shape=[jax.ShapeDtypeStruct((t, D_MODEL), F32), jax.ShapeDtypeStruct((t, D_MODEL), BF16),
                   jax.ShapeDtypeStruct((D_MODEL, D_MODEL), F32)])


def _attn_out_loss(o, gate, x, w_out, norm, target, name):
    t = x.shape[0]
    rows = ROWS_FWD

    def body(o_ref, g_ref, x_ref, wo_ref, n_ref, t_ref, loss_ref, dx_ref, dn_ref):
        i = pl.program_id(0)
        gate_v = g_ref[...].astype(F32)
        a = o_ref[...] * (gate_v * _sigmoid(gate_v))
        xv = x_ref[...] + _dot(a.astype(BF16), wo_ref[...])
        rstd = lax.rsqrt(jnp.mean(xv * xv, axis=-1, keepdims=True) + NORM_EPS)
        xhat = xv * rstd
        err = xhat * n_ref[...] - t_ref[...]
        part = 0.5 * jnp.sum(jnp.mean(err * err, axis=-1, keepdims=True), axis=0, keepdims=True)
        dy = err * (1.0 / D_MODEL)
        dn = jnp.sum(dy * xhat, axis=0, keepdims=True)
        dxhat = dy * n_ref[...]
        dx_ref[...] = rstd * (dxhat - xhat * jnp.mean(dxhat * xhat, axis=-1, keepdims=True))

        @pl.when(i == 0)
        def _():
            loss_ref[...] = jnp.zeros((8, 128), F32) + part
            dn_ref[...] = dn

        @pl.when(i > 0)
        def _():
            loss_ref[...] += part
            dn_ref[...] += dn

    return pl.pallas_call(
        body, name=name, grid=(t // rows,),
        in_specs=[_row_spec(rows, D_MODEL)] * 3 + [_const_spec((D_MODEL, D_MODEL)), _const_spec((1, D_MODEL)),
                                                   _row_spec(rows, D_MODEL)],
        out_specs=[_const_spec((8, 128)), _row_spec(rows, D_MODEL), _const_spec((1, D_MODEL))],
        out_shape=[jax.ShapeDtypeStruct((8, 128), F32), jax.ShapeDtypeStruct((t, D_MODEL), F32),
                   jax.ShapeDtypeStruct((1, D_MODEL), F32)],
        compiler_params=_params(1),
    )(o, gate, x, w_out, norm.reshape(1, D_MODEL), target)


N_CHIPS = 4
N_CORES = 2
CHIP_FLIPS = ((0, 1), (1, 0), (1, 1))
ICI_CHUNKS = 2
D2D_CHUNKS = 8


def _n_chunks(rows, dtype, most):
    unit = 16 if dtype == BF16 else 8
    return max(n for n in range(1, most + 1) if rows % n == 0 and (rows // n) % unit == 0)


def _chunks_of(arrays, most):
    out = []
    for a in arrays:
        n = _n_chunks(a.shape[-2], a.dtype, most)
        out.append((n, a.shape[-2] // n))
    return out


class _Exchange:
    def __init__(self, arrays, out_shape, scratch, copies):
        self.arrays, self.out_shape, self.scratch, self._copies = arrays, out_shape, scratch, copies

    def start(self, *refs):
        for cp in self._copies(*refs)[0]:
            cp.start()

    def wait(self, *refs):
        for wait in self._copies(*refs)[1]:
            wait()


def _chips_exchange(sends, per_dest):
    n = len(sends)
    chunking = _chunks_of(sends, ICI_CHUNKS)

    def copies(send_refs, recv_refs, sems):
        x, y, c = lax.axis_index("x"), lax.axis_index("y"), lax.axis_index("c")
        me = 2 * x + y

        def peer(k):
            fx, fy = CHIP_FLIPS[k]
            px, py = x + fx - 2 * x * fx, y + fy - 2 * y * fy
            return (px, py, c), 2 * px + py

        to_start, waits = [], []
        for a in range(n):
            send_sems, recv_sems, local_sems = sems[3 * a:3 * a + 3]
            chunks, chunk_rows = chunking[a]
            for j in range(chunks):
                part = pl.ds(j * chunk_rows, chunk_rows)
                src = lambda number: send_refs[a].at[number, part] if per_dest else send_refs[a].at[part]
                for k in range(len(CHIP_FLIPS)):
                    to, to_number = peer(k)
                    remote = lambda landing: pltpu.make_async_remote_copy(
                        src_ref=src(to_number), dst_ref=recv_refs[a].at[landing, part],
                        send_sem=send_sems.at[k, j], recv_sem=recv_sems.at[k, j],
                        device_id=to, device_id_type=pl.DeviceIdType.MESH)
                    to_start.append(remote(me))
                    waits += [remote(me).wait_send, remote(to_number).wait_recv]
                own = pltpu.make_async_copy(src(me), recv_refs[a].at[me, part], local_sems.at[j])
                to_start.append(own)
                waits.append(own.wait)
        return to_start, waits

    scratch = []
    for chunks, _ in chunking:
        scratch += [pltpu.SemaphoreType.DMA((len(CHIP_FLIPS), chunks)), pltpu.SemaphoreType.DMA((len(CHIP_FLIPS), chunks)),
                    pltpu.SemaphoreType.DMA((chunks,))]
    return _Exchange(sends, [jax.ShapeDtypeStruct((N_CHIPS,) + a.shape[-2:], a.dtype) for a in sends], scratch, copies)


def _cores_exchange(sends, per_dest):
    n = len(sends)
    chunking = _chunks_of(sends, D2D_CHUNKS)

    def copies(send_refs, got_refs, sems):
        c = lax.axis_index("c")
        sibling = (lax.axis_index("x"), lax.axis_index("y"), 1 - c)
        to_start = []
        for a in range(n):
            chunks, chunk_rows = chunking[a]
            for j in range(chunks):
                part = pl.ds(j * chunk_rows, chunk_rows)
                to_start.append(pltpu.make_async_remote_copy(
                    src_ref=send_refs[a].at[1 - c, part] if per_dest else send_refs[a].at[part],
                    dst_ref=got_refs[a].at[part], send_sem=sems[2 * a].at[j], recv_sem=sems[2 * a + 1].at[j],
                    device_id=sibling, device_id_type=pl.DeviceIdType.MESH))
        return to_start, [cp.wait for cp in to_start]

    scratch = []
    for chunks, _ in chunking:
        scratch += [pltpu.SemaphoreType.DMA((chunks,)), pltpu.SemaphoreType.DMA((chunks,))]
    return _Exchange(sends, [jax.ShapeDtypeStruct(a.shape[-2:], a.dtype) for a in sends], scratch, copies)


def _run_exchange(exchange, name):
    n = len(exchange.arrays)

    def body(*refs):
        parts = refs[:n], refs[n:2 * n], refs[2 * n:]
        exchange.start(*parts)
        exchange.wait(*parts)

    hbm = pl.BlockSpec(memory_space=pltpu.HBM)
    return pl.pallas_call(body, name=name, in_specs=[hbm] * n, out_specs=[hbm] * n, out_shape=exchange.out_shape,
                          scratch_shapes=exchange.scratch)(*exchange.arrays)


def _call_hosting(exchange, body, args, *, name, steps, in_specs, out_specs, out_shape, scratch_shapes=()):
    common = dict(name=name, grid=(steps,), compiler_params=_params(1))
    if exchange is None:
        return pl.pallas_call(body, in_specs=in_specs, out_specs=out_specs, out_shape=out_shape,
                              scratch_shapes=list(scratch_shapes), **common)(*args), None
    n_in, n_out, n_scratch, k = len(in_specs), len(out_specs), len(scratch_shapes), len(exchange.arrays)

    def hosting(*refs):
        ins, sends = refs[:n_in], refs[n_in:n_in + k]
        outs, recvs = refs[n_in + k:n_in + k + n_out], refs[n_in + k + n_out:n_in + 2 * k + n_out]
        scratch = refs[n_in + 2 * k + n_out:n_in + 2 * k + n_out + n_scratch]
        sems = refs[n_in + 2 * k + n_out + n_scratch:]
        pl.when(pl.program_id(0) == 0)(lambda: exchange.start(sends, recvs, sems))
        body(*ins, *outs, *scratch)
        pl.when(pl.program_id(0) == steps - 1)(lambda: exchange.wait(sends, recvs, sems))

    hbm = pl.BlockSpec(memory_space=pltpu.HBM)
    results = pl.pallas_call(
        hosting, in_specs=list(in_specs) + [hbm] * k, out_specs=list(out_specs) + [hbm] * k,
        out_shape=list(out_shape) + exchange.out_shape, scratch_shapes=list(scratch_shapes) + exchange.scratch, **common,
    )(*args, *exchange.arrays)
    return results[:n_out], results[n_out:]


def _exchange_chips(sends, per_dest, name):
    return _run_exchange(_chips_exchange(sends, per_dest), name)


def _swap_cores(sends, per_dest, name):
    return _run_exchange(_cores_exchange(sends, per_dest), name)


def _all_gather(arrays, name):
    by_chip = _exchange_chips(arrays, False, name + "_chips")
    others = _swap_cores([r.reshape(-1, r.shape[-1]) for r in by_chip], False, name + "_cores")
    return [(m, o.reshape(m.shape)) for m, o in zip(by_chip, others)]


def _in_device_order(mine, other, axis):
    first = lax.axis_index("c") == 0
    pieces = []
    for m, o in zip(mine, other):
        pieces += [jnp.where(first, m, o), jnp.where(first, o, m)]
    return jnp.concatenate(pieces, axis=axis)


def _sum_core(send, got, out_dtype, name):
    _, n, cols = send.shape
    rows = min(n, 256)
    while n % rows:
        rows -= 16

    def body(c_ref, keep_ref, got_ref, o_ref):
        o_ref[...] = (keep_ref[...].astype(F32) + got_ref[...].astype(F32)).astype(out_dtype)

    return pl.pallas_call(
        body, name=name, out_shape=jax.ShapeDtypeStruct((n, cols), out_dtype),
        grid_spec=pltpu.PrefetchScalarGridSpec(
            num_scalar_prefetch=1, grid=(n // rows,),
            in_specs=[pl.BlockSpec((None, rows, cols), lambda i, c: (c[0], i, 0)),
                      pl.BlockSpec((rows, cols), lambda i, c: (i, 0))],
            out_specs=pl.BlockSpec((rows, cols), lambda i, c: (i, 0))),
        compiler_params=_params(1),
    )(lax.axis_index("c").astype(jnp.int32).reshape(1), send, got)


def _sum_parts(parts, out_dtype, name):
    n, cols = parts[0].shape
    rows = min(n, 256)
    while n % rows:
        rows -= 16

    def body(*refs):
        acc = refs[0][...].astype(F32)
        for ref in refs[1:-1]:
            acc = acc + ref[...].astype(F32)
        refs[-1][...] = acc.astype(out_dtype)

    return pl.pallas_call(
        body, name=name, grid=(n // rows,),
        in_specs=[_row_spec(rows, cols)] * len(parts),
        out_specs=_row_spec(rows, cols),
        out_shape=jax.ShapeDtypeStruct((n, cols), out_dtype),
        compiler_params=_params(1),
    )(*parts)


def _reduce_scatter(sends, wire_dtypes, name):
    halves = [s.reshape(N_CORES, N_CHIPS * s.shape[2], s.shape[3]) for s in sends]
    gots = _swap_cores(halves, True, name + "_cores")
    sums = [_sum_core(h, g, dt, "%s_core_sum%d" % (name, i)).reshape((N_CHIPS,) + s.shape[2:])
            for i, (h, g, dt, s) in enumerate(zip(halves, gots, wire_dtypes, sends))]
    return _exchange_chips(sums, True, name + "_chips")


def _adamw(parts, w, m, v, name):
    n, cols = w.shape
    k = parts.shape[0]
    rows = min(n, 256)
    while n % rows:
        rows -= 8
    c1 = 1.0 - ADAM_B1 ** ADAM_STEP
    c2 = 1.0 - ADAM_B2 ** ADAM_STEP

    def body(p_ref, w_ref, m_ref, v_ref, g_ref, d_ref, nm_ref, nv_ref):
        g = p_ref[0].astype(F32)
        for s in range(1, k):
            g = g + p_ref[s].astype(F32)
        nm = ADAM_B1 * m_ref[...] + (1.0 - ADAM_B1) * g
        nv = ADAM_B2 * v_ref[...] + (1.0 - ADAM_B2) * (g * g)
        g_ref[...] = g
        nm_ref[...] = nm
        nv_ref[...] = nv
        d_ref[...] = -ADAM_LR * ((nm / c1) / (jnp.sqrt(nv / c2) + ADAM_EPS) + ADAM_WD * w_ref[...])

    blk = _row_spec(rows, cols)
    return pl.pallas_call(
        body, name=name, grid=(n // rows,),
        in_specs=[pl.BlockSpec((k, rows, cols), lambda i: (0, i, 0)), blk, blk, blk],
        out_specs=[blk] * 4,
        out_shape=[jax.ShapeDtypeStruct((n, cols), F32)] * 4,
        compiler_params=_params(1),
    )(parts, w, m, v)


SSM_KEYS = ("norm", "w_in", "a_re", "a_im", "log_step", "b_re", "b_im", "c_re", "c_im", "d", "w_glu", "b_glu", "w_out")
ATTN_KEYS = ("norm", "w_in", "sinks", "w_out")
LAYER_KEYS = (SSM_KEYS, ATTN_KEYS, SSM_KEYS, ATTN_KEYS)
BIG_KEYS = ("w_in", "w_glu", "w_out")
ATTN_SPLITS = (Q_DIM, KV_DIM, KV_DIM, D_MODEL)


def _rope_tables(t):
    pos = jnp.arange(t, dtype=F32)
    inv_freq = ROPE_THETA ** (-jnp.arange(0, HEAD_DIM, 2, dtype=F32) / HEAD_DIM)
    ang = pos[:, None] * inv_freq[None, :]
    cos, sin = jnp.cos(ang), jnp.sin(ang)
    return jnp.tile(jnp.concatenate([cos, cos], axis=1), (1, 2)), jnp.tile(jnp.concatenate([-sin, sin], axis=1), (1, 2))


def _gather_ici_stage(gather):
    return None if gather is None else _chips_exchange(gather, False)


def _gather_d2d_stage(by_chip):
    return None if by_chip is None else _cores_exchange([r.reshape(-1, r.shape[-1]) for r in by_chip], False)


def _gathered(by_chip, others):
    return None if by_chip is None else [(m, other.reshape(m.shape)) for m, other in zip(by_chip, others)]


def _scatter_d2d_stage(scatter):
    if scatter is None:
        return None, None
    halves = [s.reshape(N_CORES, N_CHIPS * s.shape[2], s.shape[3]) for s in scatter[0]]
    return halves, _cores_exchange(halves, True)


def _scatter_ici_stage(scatter, halves, gots, tag):
    if scatter is None:
        return None
    sums = [_sum_core(h, g, dt, "%sscatter_core_sum%d" % (tag, j)).reshape((N_CHIPS,) + s.shape[2:])
            for j, (h, g, dt, s) in enumerate(zip(halves, gots, scatter[1], scatter[0]))]
    return _chips_exchange(sums, True)


def _ssm_layer_fwd(i, x, p, w, gather=None):
    tag = "l%d_" % i
    mats, mats_vjp = jax.vjp(_s5_matrices, p["a_re"], p["a_im"], p["log_step"], p["b_re"], p["b_im"], p["c_re"], p["c_im"])
    kern, cpt, bpt, ar, ai = mats
    tm = _s5_toeplitz(kern.astype(BF16))
    mb = dict(tm=tm, tmt=jnp.swapaxes(tm, 1, 2), cpt=cpt.astype(BF16),
              cp=jnp.swapaxes(cpt, 1, 2).astype(BF16), bpt=bpt.astype(BF16), bp=jnp.swapaxes(bpt, 1, 2).astype(BF16))
    (u, gate), by_chip = _inproj_fwd(x, p["norm"], w["w_in"], (D_MODEL, D_MODEL), (F32, BF16), None, tag + "inproj_fwd",
                                     _gather_ici_stage(gather))
    xre, xim = _s5_project(u, mb["bpt"], tag + "s5_block_inputs")
    hre, him = _s5_scan_fwd(xre, xim, ar, ai, tag + "s5_scan_fwd")
    y = _s5_outputs(u, hre, him, mb["tm"], mb["cpt"], p["d"], tag + "s5_outputs")
    xn, others = _ssm_out_fwd(y, gate, x, w["w_glu"], p["b_glu"], w["w_out"], tag + "out_fwd", _gather_d2d_stage(by_chip))
    return xn, (x, u, gate, y, hre, him, mb, ar, ai, mats_vjp), _gathered(by_chip, others)


def _ssm_layer_bwd(i, dxn, saved, p, w, scatter=None):
    tag = "l%d_" % i
    x, u, gate, y, hre, him, mb, ar, ai, mats_vjp = saved
    halves, d2d_stage = _scatter_d2d_stage(scatter)
    (dy, dgate, dw_glu, db_glu, dw_out), gots = _ssm_out_bwd(dxn, y, gate, w["w_glu"], p["b_glu"], w["w_out"], tag + "out_bwd",
                                                             d2d_stage)
    dhre, dhim = _s5_project(dy, mb["cp"], tag + "s5_state_grads")
    dxre, dxim, dar, dai = _s5_scan_bwd(dhre, dhim, hre, him, ar, ai, tag + "s5_scan_bwd")
    (du, dk, dcpt, dbpt, dd), parts = _s5_backward(dy, u, hre, him, dxre, dxim, mb["tmt"], mb["bp"], p["d"],
                                                   tag + "s5_backward", _scatter_ici_stage(scatter, halves, gots, tag))
    dk = dk.reshape(SSM_GROUPS, S5_BLOCK, SSM_GROUP, SSM_GROUP)
    da_re, da_im, dlog_step, db_re, db_im, dc_re, dc_im = mats_vjp((dk, dcpt, dbpt, dar, dai))
    dx, dw_in, dnorm = _inproj_bwd(x, p["norm"], w["w_in"], [du, dgate], dxn, tag + "inproj_bwd")
    grads = dict(norm=dnorm.reshape(D_MODEL), w_in=dw_in, a_re=da_re, a_im=da_im, log_step=dlog_step, b_re=db_re,
                 b_im=db_im, c_re=dc_re, c_im=dc_im, d=dd.reshape(D_MODEL), w_glu=dw_glu, b_glu=db_glu.reshape(D_MODEL),
                 w_out=dw_out)
    return dx, grads, parts


def _attn_layer_fwd(i, x, p, w, rope, gather=None, head=None):
    tag = "l%d_" % i
    (q, k, v, gate), _ = _inproj_fwd(x, p["norm"], w["w_in"], ATTN_SPLITS, (BF16, BF16, BF16, BF16), rope, tag + "inproj_fwd")
    o, by_chip = _attn_fwd(q, k, v, p["sinks"], tag + "attn_fwd", _gather_ici_stage(gather))
    if head is not None:
        return _attn_out_loss(o, gate, x, w["w_out"], head[0], head[1], tag + "out_loss"), (x, q, k, v, gate, o), None
    xn, others = _attn_out_fwd(o, gate, x, w["w_out"], tag + "out_fwd", _gather_d2d_stage(by_chip))
    return xn, (x, q, k, v, gate, o), _gathered(by_chip, others)


def _attn_layer_bwd(i, dxn, saved, p, w, rope, scatter=None):
    tag = "l%d_" % i
    x, q, k, v, gate, o = saved
    halves, d2d_stage = _scatter_d2d_stage(scatter)
    (do, dgate, dw_out), gots = _attn_out_bwd(dxn, o, gate, w["w_out"], tag + "out_bwd", d2d_stage)
    (dq, dk, dv, dsinks), parts = _attn_bwd(q, k, v, o, do, p["sinks"], rope, tag + "attn_bwd",
                                            _scatter_ici_stage(scatter, halves, gots, tag))
    dx, dw_in, dnorm = _inproj_bwd(x, p["norm"], w["w_in"], [dq, dk, dv, dgate], dxn, tag + "inproj_bwd")
    grads = dict(norm=dnorm.reshape(D_MODEL), w_in=dw_in, sinks=dsinks.reshape(N_Q_HEADS, ATTN_BLOCK).sum(axis=1), w_out=dw_out)
    return dx, grads, parts


def _local_step(x, target, small, big, carried=()):
    rope = _rope_tables(x.shape[0])
    carried = {h: rest for h, *rest in carried}
    big = list(big)
    saved = []
    for i in range(4):
        gather = carried[i][0] if i in carried else None
        if i % 2 == 0:
            x, s, gathered = _ssm_layer_fwd(i, x, small[i], big[i], gather)
        else:
            head = (small[4]["norm"], target) if i == 3 else None
            x, s, gathered = _attn_layer_fwd(i, x, small[i], big[i], rope, gather, head)
        if gathered is not None:
            for layer, matrices in carried[i][1](gathered).items():
                big[layer] = matrices
        saved.append(s)
    loss, dx, dfinal = x
    grads = [None] * 4 + [dict(norm=dfinal.reshape(D_MODEL))]
    parts = {}
    for i in (3, 2, 1, 0):
        scatter = carried[i][2](grads) if i in carried else None
        if i % 2 == 0:
            dx, grads[i], parts[i] = _ssm_layer_bwd(i, dx, saved[i], small[i], big[i], scatter)
        else:
            dx, grads[i], parts[i] = _attn_layer_bwd(i, dx, saved[i], small[i], big[i], rope, scatter)
    return loss[0, 0], dx, grads, parts


def _owner_major(key, g):
    if key == "w_in":
        return g.reshape(D_MODEL, N_CHIPS, N_CORES, -1).transpose(2, 1, 0, 3)
    return g.reshape(N_CHIPS, N_CORES, -1, D_MODEL).transpose(1, 0, 2, 3)


def _from_gathered(key, mine, other):
    return _in_device_order(list(mine), list(other), 1 if key == "w_in" else 0)


SMALL_ROWS = 72


def _rows_of_small(a):
    flat = a.reshape(-1)
    return jnp.pad(flat, (0, -flat.shape[0] % D_MODEL)).reshape(-1, D_MODEL)


def _stack_small(arrays):
    rows = jnp.concatenate([_rows_of_small(a) for a in arrays], axis=0)
    assert rows.shape[0] <= N_DEV * SMALL_ROWS
    return jnp.pad(rows, ((0, N_DEV * SMALL_ROWS - rows.shape[0]), (0, 0)))


def kernel(*args):
    names = ["x"]
    layer_names = []
    for i, keys in enumerate(LAYER_KEYS):
        layer_names += ["l%d_%s" % (i, k) for k in keys]
    layer_names.append("final_norm")
    names += layer_names + ["loss_target"] + ["m_" + n for n in layer_names] + ["v_" + n for n in layer_names]
    given = dict(zip(names, args))
    big_names = [n for n in layer_names if n.split("_", 1)[1] in BIG_KEYS]
    small_names = [n for n in layer_names if n not in big_names]

    offsets = {}

    def families_of(layers):
        families = {}
        for n in big_names:
            if int(n[1]) in layers:
                family = families.setdefault(given[n].shape[1], [])
                offsets[n] = sum(given[other].shape[0] for other in family)
                family.append(n)
        return list(families.values())

    first, carried_by = families_of((0,)), {0: families_of((1,)), 1: families_of((2, 3))}
    stack = lambda pre, family: jnp.concatenate([given[pre + n] for n in family], axis=0)
    rows_of = lambda a, n: a[..., offsets[n]:offsets[n] + given[n].shape[0], :]
    local = lambda families: [stack("", family).astype(BF16) for family in families]

    def assemble(families, gathered):
        big = {}
        for family, (mine, other) in zip(families, gathered):
            for n in family:
                matrices = big.setdefault(int(n[1]), {})
                matrices[n.split("_", 1)[1]] = _from_gathered(n.split("_", 1)[1], rows_of(mine, n), rows_of(other, n))
        return big

    def sends_of(families, grads):
        return [jnp.concatenate([_owner_major(n.split("_", 1)[1], grads[int(n[1])][n.split("_", 1)[1]]) for n in family], axis=2)
                for family in families]

    small = [dict() for _ in range(5)]
    for n in small_names:
        if n == "final_norm":
            small[4]["norm"] = given[n]
        else:
            small[int(n[1])][n.split("_", 1)[1]] = given[n]

    big = [assemble(first, _all_gather(local(first), "gather_first_weights"))[0], None, None, None]
    carried = [(h, local(families), functools.partial(assemble, families),
                lambda grads, families=families: (sends_of(families, grads), [BF16] * len(families)))
               for h, families in carried_by.items()]
    loss, dx, grads, carried_parts = _local_step(given["x"][0], given["loss_target"][0], small, big, carried)
    loss = lax.psum(loss, ("x", "y", "c"))

    def grad_of(n):
        return grads[4]["norm"] if n == "final_norm" else grads[int(n[1])][n.split("_", 1)[1]]

    flat = lambda f: _stack_small([f(n) for n in small_names])
    sends = sends_of(first, grads) + [flat(grad_of).reshape(N_CORES, N_CHIPS, SMALL_ROWS, D_MODEL)]
    parts = _reduce_scatter(sends, [BF16] * len(first) + [F32], "scatter_grads")

    outs = {}
    tags = ("grad_", "delta_", "new_m_", "new_v_")
    all_families = first + carried_by[0] + carried_by[1]
    all_parts = list(parts[:-1]) + list(carried_parts[0]) + list(carried_parts[1])
    for i, (family, part) in enumerate(zip(all_families, all_parts)):
        results = _adamw(part, stack("", family), stack("m_", family), stack("v_", family), "adamw_matrices%d" % i)
        for tag, a in zip(tags, results):
            for n in family:
                outs[tag + n] = rows_of(a, n)

    my_slice = _sum_parts([parts[-1][s] for s in range(N_CHIPS)], F32, "sum_small_grads")
    mine, other = _all_gather([my_slice], "gather_small_grads")[0]
    first = lax.axis_index("c") == 0
    g_all = jnp.concatenate([jnp.where(first, mine, other), jnp.where(first, other, mine)], axis=0)
    g_all = g_all.reshape(1, N_DEV * SMALL_ROWS, D_MODEL)
    results = _adamw(g_all, flat(lambda n: given[n]), flat(lambda n: given["m_" + n]), flat(lambda n: given["v_" + n]),
                     "adamw_small")
    for tag, a in zip(tags, results):
        at = 0
        for n in small_names:
            rows = -(-given[n].size // D_MODEL)
            outs[tag + n] = a[at:at + rows].reshape(-1)[:given[n].size].reshape(given[n].shape)
            at += rows
    result = [loss, dx[None]]
    for tag in ("grad_", "delta_", "new_m_", "new_v_"):
        result += [outs[tag + n] for n in layer_names]
    return tuple(result)
```

```python
import functools
import math

import jax
import jax.numpy as jnp
from jax import lax
from jax.experimental import pallas as pl
from jax.experimental.pallas import tpu as pltpu

F32 = jnp.float32
BF16 = jnp.bfloat16

D_MODEL = 1024
SSM_GROUP = 16
SSM_GROUPS = D_MODEL // SSM_GROUP
SSM_STATE = 64
S5_BLOCK = 16
S5_LANES = S5_BLOCK * SSM_GROUP
HEAD_DIM = 64
N_Q_HEADS = 16
N_KV_HEADS = 2
GQA = N_Q_HEADS // N_KV_HEADS
Q_DIM = N_Q_HEADS * HEAD_DIM
KV_DIM = N_KV_HEADS * HEAD_DIM
ATTN_BLOCK = 128
ROPE_THETA = 10000.0
NORM_EPS = 1e-5
NEG_INF = -1e30
ATTN_SCALE = HEAD_DIM ** -0.5
N_DEV = 8

ADAM_LR = 0.001
ADAM_B1 = 0.9
ADAM_B2 = 0.999
ADAM_EPS = 1e-08
ADAM_WD = 0.01
ADAM_STEP = 10

VMEM_LIMIT = 56 * 1024 * 1024
ROWS_FWD = 512
ROWS_BWD = 512

NT = (((1,), (1,)), ((), ()))
TN = (((0,), (0,)), ((), ()))


def _params(n_grid):
    return pltpu.CompilerParams(dimension_semantics=("arbitrary",) * n_grid, vmem_limit_bytes=VMEM_LIMIT)


def _dot(a, b):
    return jnp.dot(a, b, preferred_element_type=F32)


def _dot_nt(a, b):
    return lax.dot_general(a, b, NT, preferred_element_type=F32)


def _dot_tn(a, b):
    return lax.dot_general(a, b, TN, preferred_element_type=F32)


def _sigmoid(x):
    return 1.0 / (1.0 + jnp.exp(-x))


_GELU_K = math.sqrt(2.0 / math.pi)


def _gelu(x):
    return x * (0.5 * (1.0 + jnp.tanh(_GELU_K * (x + 0.044715 * (x * x * x)))))


def _gelu_grad(x):
    t = jnp.tanh(_GELU_K * (x + 0.044715 * (x * x * x)))
    return 0.5 * (1.0 + t) + 0.5 * x * (1.0 - t * t) * (_GELU_K * (1.0 + 3.0 * 0.044715 * (x * x)))


def _row_spec(rows, cols):
    return pl.BlockSpec((rows, cols), lambda i: (i, 0))


def _const_spec(shape):
    zeros = (0,) * len(shape)
    return pl.BlockSpec(shape, lambda i: zeros, pipeline_mode=pl.Buffered(1))


def _rope_apply(t, cos, sin_signed, sign):
    lane = lax.broadcasted_iota(jnp.int32, (1, 128), 1)
    first_half = (lane % HEAD_DIM) < (HEAD_DIM // 2)
    out = []
    for j in range(t.shape[1] // 128):
        tj = t[:, 128 * j:128 * (j + 1)]
        partner = jnp.where(first_half, pltpu.roll(tj, 128 - HEAD_DIM // 2, 1), pltpu.roll(tj, HEAD_DIM // 2, 1))
        out.append(tj * cos + sign * (partner * sin_signed))
    return out[0] if len(out) == 1 else jnp.concatenate(out, axis=1)


def _inproj_fwd(x, norm, w, splits, dtypes, rope, name, exchange=None):
    t = x.shape[0]
    n = w.shape[1]
    rows = ROWS_FWD

    def body(*refs):
        if rope is None:
            x_ref, n_ref, w_ref = refs[:3]
            outs = refs[3:]
        else:
            x_ref, n_ref, w_ref, cos_ref, sin_ref = refs[:5]
            outs = refs[5:]
        xv = x_ref[...]
        rstd = lax.rsqrt(jnp.mean(xv * xv, axis=-1, keepdims=True) + NORM_EPS)
        h = (xv * rstd) * n_ref[...]
        proj = _dot(h.astype(BF16), w_ref[...])
        off = 0
        for i, width in enumerate(splits):
            piece = proj[:, off:off + width]
            if rope is not None and i < 2:
                piece = _rope_apply(piece, cos_ref[...], sin_ref[...], 1.0)
            outs[i][...] = piece.astype(dtypes[i])
            off += width

    in_specs = [_row_spec(rows, D_MODEL), _const_spec((1, D_MODEL)), _const_spec((D_MODEL, n))]
    args = [x, norm.reshape(1, D_MODEL), w]
    if rope is not None:
        in_specs += [_row_spec(rows, 128), _row_spec(rows, 128)]
        args += list(rope)
    return _call_hosting(
        exchange, body, args, name=name, steps=t // rows, in_specs=in_specs,
        out_specs=[_row_spec(rows, width) for width in splits],
        out_shape=[jax.ShapeDtypeStruct((t, width), dtype) for width, dtype in zip(splits, dtypes)])


def _inproj_bwd(x, norm, w, dpieces, dxn, name):
    t = x.shape[0]
    n = w.shape[1]
    rows = ROWS_BWD
    widths = [p.shape[1] for p in dpieces]
    k = len(dpieces)

    def body(*refs):
        x_ref, n_ref, w_ref, dxn_ref = refs[:4]
        d_refs = refs[4:4 + k]
        dx_ref, dw_ref, dn_ref = refs[4 + k:]
        @pl.when(pl.program_id(0) == 0)
        def _():
            dw_ref[...] = jnp.zeros_like(dw_ref)
            dn_ref[...] = jnp.zeros_like(dn_ref)

        xv = x_ref[...]
        rstd = lax.rsqrt(jnp.mean(xv * xv, axis=-1, keepdims=True) + NORM_EPS)
        xhat = xv * rstd
        h = xhat * n_ref[...]
        dproj = [r[...].astype(BF16) for r in d_refs]
        dproj = dproj[0] if k == 1 else jnp.concatenate(dproj, axis=1)
        dh = _dot_nt(dproj, w_ref[...])
        dw_ref[...] += _dot_tn(h.astype(BF16), dproj)
        dn_ref[...] += jnp.sum(dh * xhat, axis=0, keepdims=True)
        dxhat = dh * n_ref[...]
        dx_ref[...] = rstd * (dxhat - xhat * jnp.mean(dxhat * xhat, axis=-1, keepdims=True)) + dxn_ref[...]

    return _call_hosting(
        None, body, (x, norm.reshape(1, D_MODEL), w, dxn, *dpieces), name=name, steps=t // rows,
        in_specs=[_row_spec(rows, D_MODEL), _const_spec((1, D_MODEL)), _const_spec((D_MODEL, n)),
                  _row_spec(rows, D_MODEL)] + [_row_spec(rows, width) for width in widths],
        out_specs=[_row_spec(rows, D_MODEL), _const_spec((D_MODEL, n)), _const_spec((1, D_MODEL))],
        out_shape=[jax.ShapeDtypeStruct((t, D_MODEL), F32), jax.ShapeDtypeStruct((D_MODEL, n), F32),
                   jax.ShapeDtypeStruct((1, D_MODEL), F32)])[0]


def _s5_matrices(a_re, a_im, log_step, b_re, b_im, c_re, c_im):
    r = S5_BLOCK
    step = jnp.exp(log_step)[:, None]
    lr, li = a_re * step, a_im * step
    k = jnp.arange(r + 1, dtype=F32)
    mag = jnp.exp(lr[:, None, :] * k[:, None])
    pr = mag * jnp.cos(li[:, None, :] * k[:, None])
    pi = mag * jnp.sin(li[:, None, :] * k[:, None])
    nr, ni = pr[:, 1] - 1.0, pi[:, 1]
    den = a_re * a_re + a_im * a_im
    qr, qi = (nr * a_re + ni * a_im) / den, (ni * a_re - nr * a_im) / den
    bbr = qr[..., None] * b_re - qi[..., None] * b_im
    bbi = qr[..., None] * b_im + qi[..., None] * b_re
    wr = c_re[:, None] * pr[:, :, None, :] - c_im[:, None] * pi[:, :, None, :]
    wi = c_re[:, None] * pi[:, :, None, :] + c_im[:, None] * pr[:, :, None, :]
    w = jnp.concatenate([wr, -wi], axis=-1)
    bb = jnp.concatenate([bbr, bbi], axis=1)
    kern = jnp.einsum("gxp,gpi->gxi", w[:, :r].reshape(SSM_GROUPS, S5_LANES, 2 * SSM_STATE), bb,
                      precision=lax.Precision.HIGHEST).reshape(SSM_GROUPS, r, SSM_GROUP, SSM_GROUP)
    cpt = w[:, 1:].reshape(SSM_GROUPS, S5_LANES, 2 * SSM_STATE)
    prs = jnp.swapaxes(pr[:, r - 1::-1][:, :r], 1, 2)[..., None]
    pis = jnp.swapaxes(pi[:, r - 1::-1][:, :r], 1, 2)[..., None]
    bp_re = prs * bbr[:, :, None, :] - pis * bbi[:, :, None, :]
    bp_im = prs * bbi[:, :, None, :] + pis * bbr[:, :, None, :]
    bpt = jnp.concatenate([bp_re, bp_im], axis=1).reshape(SSM_GROUPS, 2 * SSM_STATE, S5_LANES)
    ar = pr[:, r].reshape(1, SSM_GROUPS * SSM_STATE)
    ai = pi[:, r].reshape(1, SSM_GROUPS * SSM_STATE)
    return kern, cpt, bpt, ar, ai


def _s5_toeplitz(kern):
    r = S5_BLOCK
    cols = [jnp.pad(kern[:, :r - s], ((0, 0), (s, 0), (0, 0), (0, 0))) for s in range(r)]
    return jnp.stack(cols, axis=3).reshape(SSM_GROUPS, S5_LANES, S5_LANES)


S5_OCTET = 128 // SSM_GROUP
S5_STEPS = SSM_GROUPS // S5_OCTET


def _oct_spec(t):
    return pl.BlockSpec((t, 128), lambda j: (0, j))


def _state_spec(nb):
    return pl.BlockSpec((nb, S5_OCTET * SSM_STATE), lambda j: (0, j))


def _gmat_spec(a, b):
    return pl.BlockSpec((S5_OCTET, a, b), lambda j: (j, 0, 0))


def _block_rows(ref, nb):
    by_position = jnp.swapaxes(ref[...].reshape(nb, S5_BLOCK, 128), 0, 1)
    return [by_position[r] for r in range(S5_BLOCK)]


def _store_block_rows(ref, pieces, nb):
    ref[...] = jnp.swapaxes(jnp.stack(pieces, axis=0), 0, 1).reshape(nb * S5_BLOCK, 128)


def _group_cols(pieces_t, g):
    return jnp.concatenate([p[SSM_GROUP * g:SSM_GROUP * (g + 1)] for p in pieces_t], axis=0)


def _state_cols(re_t, im_t, g):
    return jnp.concatenate([re_t[SSM_STATE * g:SSM_STATE * (g + 1)], im_t[SSM_STATE * g:SSM_STATE * (g + 1)]], axis=0)


def _s5_project(a, mat, name):
    t = a.shape[0]
    nb = t // S5_BLOCK

    def body(a_ref, m_ref, re_ref, im_ref):
        at = [p.T for p in _block_rows(a_ref, nb)]
        for pair in range(S5_OCTET // 2):
            xs = [_dot(m_ref[2 * pair + k], _group_cols(at, 2 * pair + k).astype(BF16)) for k in (0, 1)]
            lanes = slice(128 * pair, 128 * (pair + 1))
            re_ref[:, lanes] = jnp.concatenate([xs[0][:SSM_STATE], xs[1][:SSM_STATE]], axis=0).T
            im_ref[:, lanes] = jnp.concatenate([xs[0][SSM_STATE:], xs[1][SSM_STATE:]], axis=0).T

    return pl.pallas_call(
        body, name=name, grid=(S5_STEPS,),
        in_specs=[_oct_spec(t), _gmat_spec(2 * SSM_STATE, S5_LANES)],
        out_specs=[_state_spec(nb), _state_spec(nb)],
        out_shape=[jax.ShapeDtypeStruct((nb, SSM_GROUPS * SSM_STATE), F32)] * 2,
        compiler_params=_params(1),
    )(a, mat)


_SCAN_LANES = 2048


def _s5_scan_fwd(xre, xim, ar, ai, name):
    nb = xre.shape[0]
    col = pl.BlockSpec((nb, _SCAN_LANES), lambda j: (0, j))
    par = pl.BlockSpec((1, _SCAN_LANES), lambda j: (0, j))

    def body(xre_ref, xim_ref, ar_ref, ai_ref, hre_ref, him_ref):
        a_r, a_i = ar_ref[...], ai_ref[...]

        def step(b, carry):
            hr, hi = carry
            hre_ref[pl.ds(b, 1), :] = hr
            him_ref[pl.ds(b, 1), :] = hi
            xr, xi = xre_ref[pl.ds(b, 1), :], xim_ref[pl.ds(b, 1), :]
            return a_r * hr - a_i * hi + xr, a_r * hi + a_i * hr + xi

        zero = jnp.zeros((1, _SCAN_LANES), F32)
        lax.fori_loop(0, nb, step, (zero, zero))

    return pl.pallas_call(
        body, name=name, grid=(xre.shape[1] // _SCAN_LANES,),
        in_specs=[col, col, par, par], out_specs=[col, col],
        out_shape=[jax.ShapeDtypeStruct(xre.shape, F32)] * 2,
        compiler_params=_params(1),
    )(xre, xim, ar, ai)


def _s5_scan_bwd(dhre, dhim, hre, him, ar, ai, name):
    nb = dhre.shape[0]
    col = pl.BlockSpec((nb, _SCAN_LANES), lambda j: (0, j))
    par = pl.BlockSpec((1, _SCAN_LANES), lambda j: (0, j))

    def body(dhre_ref, dhim_ref, hre_ref, him_ref, ar_ref, ai_ref, dxre_ref, dxim_ref, dar_ref, dai_ref):
        a_r, a_i = ar_ref[...], ai_ref[...]

        def step(s, carry):
            gr, gi, dar, dai = carry
            b = nb - 1 - s
            dxre_ref[pl.ds(b, 1), :] = gr
            dxim_ref[pl.ds(b, 1), :] = gi
            hr, hi = hre_ref[pl.ds(b, 1), :], him_ref[pl.ds(b, 1), :]
            dar = dar + (hr * gr + hi * gi)
            dai = dai + (hr * gi - hi * gr)
            dr, di = dhre_ref[pl.ds(b, 1), :], dhim_ref[pl.ds(b, 1), :]
            return dr + (a_r * gr + a_i * gi), di + (a_r * gi - a_i * gr), dar, dai

        zero = jnp.zeros((1, _SCAN_LANES), F32)
        _, _, dar, dai = lax.fori_loop(0, nb, step, (zero, zero, zero, zero))
        dar_ref[...] = dar
        dai_ref[...] = dai

    return pl.pallas_call(
        body, name=name, grid=(dhre.shape[1] // _SCAN_LANES,),
        in_specs=[col, col, col, col, par, par], out_specs=[col, col, par, par],
        out_shape=[jax.ShapeDtypeStruct(dhre.shape, F32)] * 2 + [jax.ShapeDtypeStruct(ar.shape, F32)] * 2,
        compiler_params=_params(1),
    )(dhre, dhim, hre, him, ar, ai)


def _s5_outputs(u, hre, him, tm, cpt, d, name):
    t = u.shape[0]
    nb = t // S5_BLOCK

    def body(u_ref, hre_ref, him_ref, tm_ref, cpt_ref, d_ref, y_ref):
        u_rows = _block_rows(u_ref, nb)
        ut = [p.T for p in u_rows]
        hre_t, him_t = hre_ref[...].T, him_ref[...].T
        yts = []
        for g in range(S5_OCTET):
            yts.append(_dot(tm_ref[g], _group_cols(ut, g).astype(BF16))
                       + _dot(cpt_ref[g], _state_cols(hre_t, him_t, g).astype(BF16)))
        y_rows = []
        for r in range(S5_BLOCK):
            rows = jnp.concatenate([yt[SSM_GROUP * r:SSM_GROUP * (r + 1)] for yt in yts], axis=0)
            y_rows.append(rows.T + d_ref[...] * u_rows[r])
        _store_block_rows(y_ref, y_rows, nb)

    return pl.pallas_call(
        body, name=name, grid=(S5_STEPS,),
        in_specs=[_oct_spec(t), _state_spec(nb), _state_spec(nb), _gmat_spec(S5_LANES, S5_LANES),
                  _gmat_spec(S5_LANES, 2 * SSM_STATE), _oct_spec(1)],
        out_specs=_oct_spec(t),
        out_shape=jax.ShapeDtypeStruct(u.shape, F32),
        compiler_params=_params(1),
    )(u, hre, him, tm, cpt, d.reshape(1, D_MODEL))


def _s5_backward(dy, u, hre, him, dxre, dxim, tmt, bp, d, name, exchange=None):
    t = u.shape[0]
    nb = t // S5_BLOCK

    def body(dy_ref, u_ref, hre_ref, him_ref, dxre_ref, dxim_ref, tmt_ref, bp_ref, d_ref,
             du_ref, dk_ref, dcpt_ref, dbpt_ref, dd_ref, dtm_scratch):
        dy_rows, u_rows = _block_rows(dy_ref, nb), _block_rows(u_ref, nb)
        dyt, ut = [p.T for p in dy_rows], [p.T for p in u_rows]
        hre_t, him_t = hre_ref[...].T, him_ref[...].T
        dxre_t, dxim_t = dxre_ref[...].T, dxim_ref[...].T
        duts = []
        for g in range(S5_OCTET):
            dyg, ug = _group_cols(dyt, g).astype(BF16), _group_cols(ut, g).astype(BF16)
            hg = _state_cols(hre_t, him_t, g).astype(BF16)
            dxg = _state_cols(dxre_t, dxim_t, g).astype(BF16)
            duts.append(_dot(tmt_ref[g], dyg) + _dot(bp_ref[g], dxg))
            dtm_scratch[...] = _dot_nt(dyg, ug)
            dk = dtm_scratch[:, :SSM_GROUP]
            for s in range(1, S5_BLOCK):
                below = dtm_scratch[SSM_GROUP * s:, SSM_GROUP * s:SSM_GROUP * (s + 1)]
                dk = dk + jnp.concatenate([below, jnp.zeros((SSM_GROUP * s, SSM_GROUP), F32)], axis=0)
            dk_ref[g] = dk
            dcpt_ref[g] = _dot_nt(dyg, hg)
            dbpt_ref[g] = _dot_nt(dxg, ug)
        dd = jnp.zeros((1, 128), F32)
        du_rows = []
        for r in range(S5_BLOCK):
            rows = jnp.concatenate([dut[SSM_GROUP * r:SSM_GROUP * (r + 1)] for dut in duts], axis=0)
            du_rows.append(rows.T + d_ref[...] * dy_rows[r])
            dd = dd + jnp.sum(dy_rows[r] * u_rows[r], axis=0, keepdims=True)
        _store_block_rows(du_ref, du_rows, nb)
        dd_ref[...] = dd

    return _call_hosting(
        exchange, body, (dy, u, hre, him, dxre, dxim, tmt, bp, d.reshape(1, D_MODEL)), name=name, steps=S5_STEPS,
        in_specs=[_oct_spec(t), _oct_spec(t), _state_spec(nb), _state_spec(nb), _state_spec(nb), _state_spec(nb),
                  _gmat_spec(S5_LANES, S5_LANES), _gmat_spec(S5_LANES, 2 * SSM_STATE), _oct_spec(1)],
        out_specs=[_oct_spec(t), _gmat_spec(S5_LANES, SSM_GROUP), _gmat_spec(S5_LANES, 2 * SSM_STATE),
                   _gmat_spec(2 * SSM_STATE, S5_LANES), _oct_spec(1)],
        out_shape=[jax.ShapeDtypeStruct(u.shape, F32),
                   jax.ShapeDtypeStruct((SSM_GROUPS, S5_LANES, SSM_GROUP), F32),
                   jax.ShapeDtypeStruct((SSM_GROUPS, S5_LANES, 2 * SSM_STATE), F32),
                   jax.ShapeDtypeStruct((SSM_GROUPS, 2 * SSM_STATE, S5_LANES), F32),
                   jax.ShapeDtypeStruct((1, D_MODEL), F32)],
        scratch_shapes=[pltpu.VMEM((S5_LANES, S5_LANES), F32)])


def _ssm_out_fwd(y, gate, x, w_glu, b_glu, w_out, name, exchange=None):
    t = x.shape[0]
    rows = ROWS_FWD

    def body(y_ref, g_ref, x_ref, wg_ref, bg_ref, wo_ref, o_ref):
        z0 = _gelu(y_ref[...])
        s = _dot(z0.astype(BF16), wg_ref[...]) + bg_ref[...]
        gate_v = g_ref[...].astype(F32)
        a = (z0 * _sigmoid(s)) * (gate_v * _sigmoid(gate_v))
        o_ref[...] = x_ref[...] + _dot(a.astype(BF16), wo_ref[...])

    (xn,), got = _call_hosting(
        exchange, body, (y, gate, x, w_glu, b_glu.reshape(1, D_MODEL), w_out), name=name, steps=t // rows,
        in_specs=[_row_spec(rows, D_MODEL)] * 3 + [_const_spec((D_MODEL, D_MODEL)), _const_spec((1, D_MODEL)),
                                                   _const_spec((D_MODEL, D_MODEL))],
        out_specs=[_row_spec(rows, D_MODEL)],
        out_shape=[jax.ShapeDtypeStruct((t, D_MODEL), F32)])
    return xn, got


def _ssm_out_bwd(dxn, y, gate, w_glu, b_glu, w_out, name, exchange=None):
    t = y.shape[0]
    rows = ROWS_BWD

    def body(dxn_ref, y_ref, g_ref, wg_ref, bg_ref, wo_ref, dy_ref, dg_ref, dwg_ref, dbg_ref, dwo_ref):
        @pl.when(pl.program_id(0) == 0)
        def _():
            dwo_ref[...] = jnp.zeros_like(dwo_ref)
            dwg_ref[...] = jnp.zeros_like(dwg_ref)
            dbg_ref[...] = jnp.zeros_like(dbg_ref)

        yv = y_ref[...]
        z0 = _gelu(yv)
        z0b = z0.astype(BF16)
        sg = _sigmoid(_dot(z0b, wg_ref[...]) + bg_ref[...])
        z = z0 * sg
        gate_v = g_ref[...].astype(F32)
        sgg = _sigmoid(gate_v)
        silu = gate_v * sgg
        dob = dxn_ref[...].astype(BF16)
        da = _dot_nt(dob, wo_ref[...])
        dwo_ref[...] += _dot_tn((z * silu).astype(BF16), dob)
        dz = da * silu
        dg_ref[...] = (da * z * (sgg * (1.0 + gate_v * (1.0 - sgg)))).astype(BF16)
        ds = dz * z0 * (sg * (1.0 - sg))
        dsb = ds.astype(BF16)
        dz0 = dz * sg + _dot_nt(dsb, wg_ref[...])
        dwg_ref[...] += _dot_tn(z0b, dsb)
        dbg_ref[...] += jnp.sum(ds, axis=0, keepdims=True)
        dy_ref[...] = dz0 * _gelu_grad(yv)

    sq = _const_spec((D_MODEL, D_MODEL))
    vec = _const_spec((1, D_MODEL))
    return _call_hosting(
        exchange, body, (dxn, y, gate, w_glu, b_glu.reshape(1, D_MODEL), w_out), name=name, steps=t // rows,
        in_specs=[_row_spec(rows, D_MODEL)] * 3 + [sq, vec, sq],
        out_specs=[_row_spec(rows, D_MODEL), _row_spec(rows, D_MODEL), sq, vec, sq],
        out_shape=[jax.ShapeDtypeStruct((t, D_MODEL), F32), jax.ShapeDtypeStruct((t, D_MODEL), BF16),
                   jax.ShapeDtypeStruct((D_MODEL, D_MODEL), F32), jax.ShapeDtypeStruct((1, D_MODEL), F32),
                   jax.ShapeDtypeStruct((D_MODEL, D_MODEL), F32)])


KV_LANES = GQA * ATTN_BLOCK


def _attn_bias(block_is_first):
    kj = lax.broadcasted_iota(jnp.int32, (2 * ATTN_BLOCK, ATTN_BLOCK), 0)
    qi = lax.broadcasted_iota(jnp.int32, (2 * ATTN_BLOCK, ATTN_BLOCK), 1)
    dist = qi + ATTN_BLOCK - kj
    valid = (dist >= 0) & (dist < ATTN_BLOCK) & (jnp.logical_not(block_is_first) | (kj >= ATTN_BLOCK))
    return jnp.tile(jnp.where(valid, 0.0, NEG_INF).astype(F32), (1, GQA))


def _head_cols(a_t, kvh):
    heads = range(kvh * GQA, (kvh + 1) * GQA)
    return jnp.concatenate([a_t[HEAD_DIM * h:HEAD_DIM * (h + 1)] for h in heads], axis=1)


def _head_rows(a_cols):
    stacked = jnp.concatenate([a_cols[:, ATTN_BLOCK * g:ATTN_BLOCK * (g + 1)] for g in range(GQA)], axis=0)
    return stacked.T


def _kv_rows(prev_ref, cur_ref, kvh):
    lanes = slice(HEAD_DIM * kvh, HEAD_DIM * (kvh + 1))
    return jnp.concatenate([prev_ref[:, lanes], cur_ref[:, lanes]], axis=0).astype(BF16)


def _kv_cols(prev_t, cur_t, kvh):
    rows = slice(HEAD_DIM * kvh, HEAD_DIM * (kvh + 1))
    return jnp.concatenate([prev_t[rows], cur_t[rows]], axis=1).astype(BF16)


def _attn_probs(kk, q_cols, sink_row, bias):
    s = _dot(kk, q_cols) * ATTN_SCALE + bias
    m = jnp.maximum(jnp.max(s, axis=0, keepdims=True), sink_row)
    p = jnp.exp(s - m)
    e_sink = jnp.exp(sink_row - m)
    inv = 1.0 / (jnp.sum(p, axis=0, keepdims=True) + e_sink)
    return p * inv, e_sink * inv


def _sink_cols(sinks):
    return jnp.repeat(sinks, ATTN_BLOCK).reshape(N_KV_HEADS, 1, KV_LANES)


def _attn_fwd(q, k, v, sinks, name, exchange=None):
    t = q.shape[0]
    nblk = t // ATTN_BLOCK

    def body(s_ref, q_ref, kc_ref, kp_ref, vc_ref, vp_ref, o_ref):
        bias = _attn_bias(pl.program_id(0) == 0)
        q_t = q_ref[...].astype(F32).T
        vp_t, vc_t = vp_ref[...].astype(F32).T, vc_ref[...].astype(F32).T
        for kvh in range(N_KV_HEADS):
            p, _ = _attn_probs(_kv_rows(kp_ref, kc_ref, kvh), _head_cols(q_t, kvh).astype(BF16), s_ref[kvh], bias)
            o_cols = _dot(_kv_cols(vp_t, vc_t, kvh), p.astype(BF16))
            o_ref[:, GQA * HEAD_DIM * kvh:GQA * HEAD_DIM * (kvh + 1)] = _head_rows(o_cols)

    cur = lambda i: (i, 0)
    prev = lambda i: (jnp.maximum(i - 1, 0), 0)
    (o,), got = _call_hosting(
        exchange, body, (_sink_cols(sinks), q, k, k, v, v), name=name, steps=nblk,
        in_specs=[_const_spec((N_KV_HEADS, 1, KV_LANES)),
                  pl.BlockSpec((ATTN_BLOCK, Q_DIM), cur),
                  pl.BlockSpec((ATTN_BLOCK, KV_DIM), cur), pl.BlockSpec((ATTN_BLOCK, KV_DIM), prev),
                  pl.BlockSpec((ATTN_BLOCK, KV_DIM), cur), pl.BlockSpec((ATTN_BLOCK, KV_DIM), prev)],
        out_specs=[pl.BlockSpec((ATTN_BLOCK, Q_DIM), cur)],
        out_shape=[jax.ShapeDtypeStruct((t, Q_DIM), F32)])
    return o, got


def _attn_bwd(q, k, v, o, do, sinks, rope, name, exchange=None):
    t = q.shape[0]
    nblk = t // ATTN_BLOCK

    def body(s_ref, q_ref, o_ref, do_ref, kp_ref, kc_ref, vp_ref, vc_ref, cosq_ref, sinq_ref, cosk_ref, sinkey_ref,
             dq_ref, dk_ref, dv_ref, ds_ref, new_k, new_v, wait_k, wait_v, dq_rot):
        n = pl.program_id(0)

        @pl.when(n == 0)
        def _():
            ds_ref[...] = jnp.zeros_like(ds_ref)
            wait_k[...] = jnp.zeros_like(wait_k)
            wait_v[...] = jnp.zeros_like(wait_v)

        @pl.when(n < nblk)
        def _():
            bias = _attn_bias(n == 0)
            q_t, o_t, do_t = q_ref[...].astype(F32).T, o_ref[...].T, do_ref[...].T
            kp_t, kc_t = kp_ref[...].astype(F32).T, kc_ref[...].astype(F32).T
            for kvh in range(N_KV_HEADS):
                q_cols = _head_cols(q_t, kvh).astype(BF16)
                do_cols = _head_cols(do_t, kvh)
                delta = jnp.sum(do_cols * _head_cols(o_t, kvh), axis=0, keepdims=True)
                do_cols = do_cols.astype(BF16)
                p, p_sink = _attn_probs(_kv_rows(kp_ref, kc_ref, kvh), q_cols, s_ref[kvh], bias)
                dp = _dot(_kv_rows(vp_ref, vc_ref, kvh), do_cols)
                ds = (p * (dp - delta) * ATTN_SCALE).astype(BF16)
                lanes = slice(GQA * HEAD_DIM * kvh, GQA * HEAD_DIM * (kvh + 1))
                dq_rot[:, lanes] = _head_rows(_dot(_kv_cols(kp_t, kc_t, kvh), ds))
                head = slice(HEAD_DIM * kvh, HEAD_DIM * (kvh + 1))
                new_k[:, head] = _dot_nt(ds, q_cols)
                new_v[:, head] = _dot_nt(p.astype(BF16), do_cols)
                ds_ref[kvh] += -(p_sink * delta)
            dq_ref[...] = _rope_apply(dq_rot[...], cosq_ref[...], sinq_ref[...], -1.0).astype(BF16)

        @pl.when(n == nblk)
        def _():
            new_k[...] = jnp.zeros_like(new_k)
            new_v[...] = jnp.zeros_like(new_v)

        dk_ref[...] = _rope_apply(wait_k[...] + new_k[:ATTN_BLOCK], cosk_ref[...], sinkey_ref[...], -1.0).astype(BF16)
        dv_ref[...] = (wait_v[...] + new_v[:ATTN_BLOCK]).astype(BF16)
        wait_k[...] = new_k[ATTN_BLOCK:]
        wait_v[...] = new_v[ATTN_BLOCK:]

    cur = lambda i: (jnp.minimum(i, nblk - 1), 0)
    prev = lambda i: (jnp.maximum(i - 1, 0), 0)
    qs = lambda f: pl.BlockSpec((ATTN_BLOCK, Q_DIM), f)
    ks = lambda f: pl.BlockSpec((ATTN_BLOCK, KV_DIM), f)
    sink_spec = _const_spec((N_KV_HEADS, 1, KV_LANES))
    return _call_hosting(
        exchange, body, (_sink_cols(sinks), q, o, do, k, k, v, v, rope[0], rope[1], rope[0], rope[1]),
        name=name, steps=nblk + 1,
        in_specs=[sink_spec, qs(cur), qs(cur), qs(cur), ks(prev), ks(cur), ks(prev), ks(cur),
                  ks(cur), ks(cur), ks(prev), ks(prev)],
        out_specs=[qs(cur), ks(prev), ks(prev), sink_spec],
        out_shape=[jax.ShapeDtypeStruct((t, Q_DIM), BF16), jax.ShapeDtypeStruct((t, KV_DIM), BF16),
                   jax.ShapeDtypeStruct((t, KV_DIM), BF16), jax.ShapeDtypeStruct((N_KV_HEADS, 1, KV_LANES), F32)],
        scratch_shapes=[pltpu.VMEM((2 * ATTN_BLOCK, KV_DIM), F32), pltpu.VMEM((2 * ATTN_BLOCK, KV_DIM), F32),
                        pltpu.VMEM((ATTN_BLOCK, KV_DIM), F32), pltpu.VMEM((ATTN_BLOCK, KV_DIM), F32),
                        pltpu.VMEM((ATTN_BLOCK, Q_DIM), F32)])


def _attn_out_fwd(o, gate, x, w_out, name, exchange=None):
    t = x.shape[0]
    rows = ROWS_FWD

    def body(o_ref, g_ref, x_ref, wo_ref, xn_ref):
        gate_v = g_ref[...].astype(F32)
        a = o_ref[...] * (gate_v * _sigmoid(gate_v))
        xn_ref[...] = x_ref[...] + _dot(a.astype(BF16), wo_ref[...])

    (xn,), got = _call_hosting(
        exchange, body, (o, gate, x, w_out), name=name, steps=t // rows,
        in_specs=[_row_spec(rows, D_MODEL)] * 3 + [_const_spec((D_MODEL, D_MODEL))],
        out_specs=[_row_spec(rows, D_MODEL)],
        out_shape=[jax.ShapeDtypeStruct((t, D_MODEL), F32)])
    return xn, got


def _attn_out_bwd(dxn, o, gate, w_out, name, exchange=None):
    t = o.shape[0]
    rows = ROWS_BWD

    def body(dxn_ref, o_ref, g_ref, wo_ref, do_ref, dg_ref, dwo_ref):
        @pl.when(pl.program_id(0) == 0)
        def _():
            dwo_ref[...] = jnp.zeros_like(dwo_ref)

        gate_v, ov = g_ref[...].astype(F32), o_ref[...]
        sgg = _sigmoid(gate_v)
        silu = gate_v * sgg
        dob = dxn_ref[...].astype(BF16)
        da = _dot_nt(dob, wo_ref[...])
        dwo_ref[...] += _dot_tn((ov * silu).astype(BF16), dob)
        do_ref[...] = da * silu
        dg_ref[...] = (da * ov * (sgg * (1.0 + gate_v * (1.0 - sgg)))).astype(BF16)

    sq = _const_spec((D_MODEL, D_MODEL))
    return _call_hosting(
        exchange, body, (dxn, o, gate, w_out), name=name, steps=t // rows,
        in_specs=[_row_spec(rows, D_MODEL)] * 3 + [sq],
        out_specs=[_row_spec(rows, D_MODEL), _row_spec(rows, D_MODEL), sq],
        out_shape=[jax.ShapeDtypeStruct((t, D_MODEL), F32), jax.ShapeDtypeStruct((t, D_MODEL), BF16),
                   jax.ShapeDtypeStruct((D_MODEL, D_MODEL), F32)])


def _attn_out_loss(o, gate, x, w_out, norm, target, name):
    t = x.shape[0]
    rows = ROWS_FWD

    def body(o_ref, g_ref, x_ref, wo_ref, n_ref, t_ref, loss_ref, dx_ref, dn_ref):
        i = pl.program_id(0)
        gate_v = g_ref[...].astype(F32)
        a = o_ref[...] * (gate_v * _sigmoid(gate_v))
        xv = x_ref[...] + _dot(a.astype(BF16), wo_ref[...])
        rstd = lax.rsqrt(jnp.mean(xv * xv, axis=-1, keepdims=True) + NORM_EPS)
        xhat = xv * rstd
        err = xhat * n_ref[...] - t_ref[...]
        part = 0.5 * jnp.sum(jnp.mean(err * err, axis=-1, keepdims=True), axis=0, keepdims=True)
        dy = err * (1.0 / D_MODEL)
        dn = jnp.sum(dy * xhat, axis=0, keepdims=True)
        dxhat = dy * n_ref[...]
        dx_ref[...] = rstd * (dxhat - xhat * jnp.mean(dxhat * xhat, axis=-1, keepdims=True))

        @pl.when(i == 0)
        def _():
            loss_ref[...] = jnp.zeros((8, 128), F32) + part
            dn_ref[...] = dn

        @pl.when(i > 0)
        def _():
            loss_ref[...] += part
            dn_ref[...] += dn

    return pl.pallas_call(
        body, name=name, grid=(t // rows,),
        in_specs=[_row_spec(rows, D_MODEL)] * 3 + [_const_spec((D_MODEL, D_MODEL)), _const_spec((1, D_MODEL)),
                                                   _row_spec(rows, D_MODEL)],
        out_specs=[_const_spec((8, 128)), _row_spec(rows, D_MODEL), _const_spec((1, D_MODEL))],
        out_shape=[jax.ShapeDtypeStruct((8, 128), F32), jax.ShapeDtypeStruct((t, D_MODEL), F32),
                   jax.ShapeDtypeStruct((1, D_MODEL), F32)],
        compiler_params=_params(1),
    )(o, gate, x, w_out, norm.reshape(1, D_MODEL), target)


N_CHIPS = 4
N_CORES = 2
CHIP_FLIPS = ((0, 1), (1, 0), (1, 1))
ICI_CHUNKS = 2
D2D_CHUNKS = 8


def _n_chunks(rows, dtype, most):
    unit = 16 if dtype == BF16 else 8
    return max(n for n in range(1, most + 1) if rows % n == 0 and (rows // n) % unit == 0)


def _chunks_of(arrays, most):
    out = []
    for a in arrays:
        n = _n_chunks(a.shape[-2], a.dtype, most)
        out.append((n, a.shape[-2] // n))
    return out


class _Exchange:
    def __init__(self, arrays, out_shape, scratch, copies):
        self.arrays, self.out_shape, self.scratch, self._copies = arrays, out_shape, scratch, copies

    def start(self, *refs):
        for cp in self._copies(*refs)[0]:
            cp.start()

    def wait(self, *refs):
        for wait in self._copies(*refs)[1]:
            wait()


def _chips_exchange(sends, per_dest):
    n = len(sends)
    chunking = _chunks_of(sends, ICI_CHUNKS)

    def copies(send_refs, recv_refs, sems):
        x, y, c = lax.axis_index("x"), lax.axis_index("y"), lax.axis_index("c")
        me = 2 * x + y

        def peer(k):
            fx, fy = CHIP_FLIPS[k]
            px, py = x + fx - 2 * x * fx, y + fy - 2 * y * fy
            return (px, py, c), 2 * px + py

        to_start, waits = [], []
        for a in range(n):
            send_sems, recv_sems, local_sems = sems[3 * a:3 * a + 3]
            chunks, chunk_rows = chunking[a]
            for j in range(chunks):
                part = pl.ds(j * chunk_rows, chunk_rows)
                src = lambda number: send_refs[a].at[number, part] if per_dest else send_refs[a].at[part]
                for k in range(len(CHIP_FLIPS)):
                    to, to_number = peer(k)
                    remote = lambda landing: pltpu.make_async_remote_copy(
                        src_ref=src(to_number), dst_ref=recv_refs[a].at[landing, part],
                        send_sem=send_sems.at[k, j], recv_sem=recv_sems.at[k, j],
                        device_id=to, device_id_type=pl.DeviceIdType.MESH)
                    to_start.append(remote(me))
                    waits += [remote(me).wait_send, remote(to_number).wait_recv]
                own = pltpu.make_async_copy(src(me), recv_refs[a].at[me, part], local_sems.at[j])
                to_start.append(own)
                waits.append(own.wait)
        return to_start, waits

    scratch = []
    for chunks, _ in chunking:
        scratch += [pltpu.SemaphoreType.DMA((len(CHIP_FLIPS), chunks)), pltpu.SemaphoreType.DMA((len(CHIP_FLIPS), chunks)),
                    pltpu.SemaphoreType.DMA((chunks,))]
    return _Exchange(sends, [jax.ShapeDtypeStruct((N_CHIPS,) + a.shape[-2:], a.dtype) for a in sends], scratch, copies)


def _cores_exchange(sends, per_dest):
    n = len(sends)
    chunking = _chunks_of(sends, D2D_CHUNKS)

    def copies(send_refs, got_refs, sems):
        c = lax.axis_index("c")
        sibling = (lax.axis_index("x"), lax.axis_index("y"), 1 - c)
        to_start = []
        for a in range(n):
            chunks, chunk_rows = chunking[a]
            for j in range(chunks):
                part = pl.ds(j * chunk_rows, chunk_rows)
                to_start.append(pltpu.make_async_remote_copy(
                    src_ref=send_refs[a].at[1 - c, part] if per_dest else send_refs[a].at[part],
                    dst_ref=got_refs[a].at[part], send_sem=sems[2 * a].at[j], recv_sem=sems[2 * a + 1].at[j],
                    device_id=sibling, device_id_type=pl.DeviceIdType.MESH))
        return to_start, [cp.wait for cp in to_start]

    scratch = []
    for chunks, _ in chunking:
        scratch += [pltpu.SemaphoreType.DMA((chunks,)), pltpu.SemaphoreType.DMA((chunks,))]
    return _Exchange(sends, [jax.ShapeDtypeStruct(a.shape[-2:], a.dtype) for a in sends], scratch, copies)


def _run_exchange(exchange, name):
    n = len(exchange.arrays)

    def body(*refs):
        parts = refs[:n], refs[n:2 * n], refs[2 * n:]
        exchange.start(*parts)
        exchange.wait(*parts)

    hbm = pl.BlockSpec(memory_space=pltpu.HBM)
    return pl.pallas_call(body, name=name, in_specs=[hbm] * n, out_specs=[hbm] * n, out_shape=exchange.out_shape,
                          scratch_shapes=exchange.scratch)(*exchange.arrays)


def _call_hosting(exchange, body, args, *, name, steps, in_specs, out_specs, out_shape, scratch_shapes=()):
    common = dict(name=name, grid=(steps,), compiler_params=_params(1))
    if exchange is None:
        return pl.pallas_call(body, in_specs=in_specs, out_specs=out_specs, out_shape=out_shape,
                              scratch_shapes=list(scratch_shapes), **common)(*args), None
    n_in, n_out, n_scratch, k = len(in_specs), len(out_specs), len(scratch_shapes), len(exchange.arrays)

    def hosting(*refs):
        ins, sends = refs[:n_in], refs[n_in:n_in + k]
        outs, recvs = refs[n_in + k:n_in + k + n_out], refs[n_in + k + n_out:n_in + 2 * k + n_out]
        scratch = refs[n_in + 2 * k + n_out:n_in + 2 * k + n_out + n_scratch]
        sems = refs[n_in + 2 * k + n_out + n_scratch:]
        pl.when(pl.program_id(0) == 0)(lambda: exchange.start(sends, recvs, sems))
        body(*ins, *outs, *scratch)
        pl.when(pl.program_id(0) == steps - 1)(lambda: exchange.wait(sends, recvs, sems))

    hbm = pl.BlockSpec(memory_space=pltpu.HBM)
    results = pl.pallas_call(
        hosting, in_specs=list(in_specs) + [hbm] * k, out_specs=list(out_specs) + [hbm] * k,
        out_shape=list(out_shape) + exchange.out_shape, scratch_shapes=list(scratch_shapes) + exchange.scratch, **common,
    )(*args, *exchange.arrays)
    return results[:n_out], results[n_out:]


def _exchange_chips(sends, per_dest, name):
    return _run_exchange(_chips_exchange(sends, per_dest), name)


def _swap_cores(sends, per_dest, name):
    return _run_exchange(_cores_exchange(sends, per_dest), name)


def _all_gather(arrays, name):
    by_chip = _exchange_chips(arrays, False, name + "_chips")
    others = _swap_cores([r.reshape(-1, r.shape[-1]) for r in by_chip], False, name + "_cores")
    return [(m, o.reshape(m.shape)) for m, o in zip(by_chip, others)]


def _in_device_order(mine, other, axis):
    first = lax.axis_index("c") == 0
    pieces = []
    for m, o in zip(mine, other):
        pieces += [jnp.where(first, m, o), jnp.where(first, o, m)]
    return jnp.concatenate(pieces, axis=axis)


def _sum_core(send, got, out_dtype, name):
    _, n, cols = send.shape
    rows = min(n, 256)
    while n % rows:
        rows -= 16

    def body(c_ref, keep_ref, got_ref, o_ref):
        o_ref[...] = (keep_ref[...].astype(F32) + got_ref[...].astype(F32)).astype(out_dtype)

    return pl.pallas_call(
        body, name=name, out_shape=jax.ShapeDtypeStruct((n, cols), out_dtype),
        grid_spec=pltpu.PrefetchScalarGridSpec(
            num_scalar_prefetch=1, grid=(n // rows,),
            in_specs=[pl.BlockSpec((None, rows, cols), lambda i, c: (c[0], i, 0)),
                      pl.BlockSpec((rows, cols), lambda i, c: (i, 0))],
            out_specs=pl.BlockSpec((rows, cols), lambda i, c: (i, 0))),
        compiler_params=_params(1),
    )(lax.axis_index("c").astype(jnp.int32).reshape(1), send, got)


def _sum_parts(parts, out_dtype, name):
    n, cols = parts[0].shape
    rows = min(n, 256)
    while n % rows:
        rows -= 16

    def body(*refs):
        acc = refs[0][...].astype(F32)
        for ref in refs[1:-1]:
            acc = acc + ref[...].astype(F32)
        refs[-1][...] = acc.astype(out_dtype)

    return pl.pallas_call(
        body, name=name, grid=(n // rows,),
        in_specs=[_row_spec(rows, cols)] * len(parts),
        out_specs=_row_spec(rows, cols),
        out_shape=jax.ShapeDtypeStruct((n, cols), out_dtype),
        compiler_params=_params(1),
    )(*parts)


def _reduce_scatter(sends, wire_dtypes, name):
    halves = [s.reshape(N_CORES, N_CHIPS * s.shape[2], s.shape[3]) for s in sends]
    gots = _swap_cores(halves, True, name + "_cores")
    sums = [_sum_core(h, g, dt, "%s_core_sum%d" % (name, i)).reshape((N_CHIPS,) + s.shape[2:])
            for i, (h, g, dt, s) in enumerate(zip(halves, gots, wire_dtypes, sends))]
    return _exchange_chips(sums, True, name + "_chips")


def _adamw(parts, w, m, v, name):
    n, cols = w.shape
    k = parts.shape[0]
    rows = min(n, 256)
    while n % rows:
        rows -= 8
    c1 = 1.0 - ADAM_B1 ** ADAM_STEP
    c2 = 1.0 - ADAM_B2 ** ADAM_STEP

    def body(p_ref, w_ref, m_ref, v_ref, g_ref, d_ref, nm_ref, nv_ref):
        g = p_ref[0].astype(F32)
        for s in range(1, k):
            g = g + p_ref[s].astype(F32)
        nm = ADAM_B1 * m_ref[...] + (1.0 - ADAM_B1) * g
        nv = ADAM_B2 * v_ref[...] + (1.0 - ADAM_B2) * (g * g)
        g_ref[...] = g
        nm_ref[...] = nm
        nv_ref[...] = nv
        d_ref[...] = -ADAM_LR * ((nm / c1) / (jnp.sqrt(nv / c2) + ADAM_EPS) + ADAM_WD * w_ref[...])

    blk = _row_spec(rows, cols)
    return pl.pallas_call(
        body, name=name, grid=(n // rows,),
        in_specs=[pl.BlockSpec((k, rows, cols), lambda i: (0, i, 0)), blk, blk, blk],
        out_specs=[blk] * 4,
        out_shape=[jax.ShapeDtypeStruct((n, cols), F32)] * 4,
        compiler_params=_params(1),
    )(parts, w, m, v)


SSM_KEYS = ("norm", "w_in", "a_re", "a_im", "log_step", "b_re", "b_im", "c_re", "c_im", "d", "w_glu", "b_glu", "w_out")
ATTN_KEYS = ("norm", "w_in", "sinks", "w_out")
LAYER_KEYS = (SSM_KEYS, ATTN_KEYS, SSM_KEYS, ATTN_KEYS)
BIG_KEYS = ("w_in", "w_glu", "w_out")
ATTN_SPLITS = (Q_DIM, KV_DIM, KV_DIM, D_MODEL)


def _rope_tables(t):
    pos = jnp.arange(t, dtype=F32)
    inv_freq = ROPE_THETA ** (-jnp.arange(0, HEAD_DIM, 2, dtype=F32) / HEAD_DIM)
    ang = pos[:, None] * inv_freq[None, :]
    cos, sin = jnp.cos(ang), jnp.sin(ang)
    return jnp.tile(jnp.concatenate([cos, cos], axis=1), (1, 2)), jnp.tile(jnp.concatenate([-sin, sin], axis=1), (1, 2))


def _gather_ici_stage(gather):
    return None if gather is None else _chips_exchange(gather, False)


def _gather_d2d_stage(by_chip):
    return None if by_chip is None else _cores_exchange([r.reshape(-1, r.shape[-1]) for r in by_chip], False)


def _gathered(by_chip, others):
    return None if by_chip is None else [(m, other.reshape(m.shape)) for m, other in zip(by_chip, others)]


def _scatter_d2d_stage(scatter):
    if scatter is None:
        return None, None
    halves = [s.reshape(N_CORES, N_CHIPS * s.shape[2], s.shape[3]) for s in scatter[0]]
    return halves, _cores_exchange(halves, True)


def _scatter_ici_stage(scatter, halves, gots, tag):
    if scatter is None:
        return None
    sums = [_sum_core(h, g, dt, "%sscatter_core_sum%d" % (tag, j)).reshape((N_CHIPS,) + s.shape[2:])
            for j, (h, g, dt, s) in enumerate(zip(halves, gots, scatter[1], scatter[0]))]
    return _chips_exchange(sums, True)


def _ssm_layer_fwd(i, x, p, w, gather=None):
    tag = "l%d_" % i
    mats, mats_vjp = jax.vjp(_s5_matrices, p["a_re"], p["a_im"], p["log_step"], p["b_re"], p["b_im"], p["c_re"], p["c_im"])
    kern, cpt, bpt, ar, ai = mats
    tm = _s5_toeplitz(kern.astype(BF16))
    mb = dict(tm=tm, tmt=jnp.swapaxes(tm, 1, 2), cpt=cpt.astype(BF16),
              cp=jnp.swapaxes(cpt, 1, 2).astype(BF16), bpt=bpt.astype(BF16), bp=jnp.swapaxes(bpt, 1, 2).astype(BF16))
    (u, gate), by_chip = _inproj_fwd(x, p["norm"], w["w_in"], (D_MODEL, D_MODEL), (F32, BF16), None, tag + "inproj_fwd",
                                     _gather_ici_stage(gather))
    xre, xim = _s5_project(u, mb["bpt"], tag + "s5_block_inputs")
    hre, him = _s5_scan_fwd(xre, xim, ar, ai, tag + "s5_scan_fwd")
    y = _s5_outputs(u, hre, him, mb["tm"], mb["cpt"], p["d"], tag + "s5_outputs")
    xn, others = _ssm_out_fwd(y, gate, x, w["w_glu"], p["b_glu"], w["w_out"], tag + "out_fwd", _gather_d2d_stage(by_chip))
    return xn, (x, u, gate, y, hre, him, mb, ar, ai, mats_vjp), _gathered(by_chip, others)


def _ssm_layer_bwd(i, dxn, saved, p, w, scatter=None):
    tag = "l%d_" % i
    x, u, gate, y, hre, him, mb, ar, ai, mats_vjp = saved
    halves, d2d_stage = _scatter_d2d_stage(scatter)
    (dy, dgate, dw_glu, db_glu, dw_out), gots = _ssm_out_bwd(dxn, y, gate, w["w_glu"], p["b_glu"], w["w_out"], tag + "out_bwd",
                                                             d2d_stage)
    dhre, dhim = _s5_project(dy, mb["cp"], tag + "s5_state_grads")
    dxre, dxim, dar, dai = _s5_scan_bwd(dhre, dhim, hre, him, ar, ai, tag + "s5_scan_bwd")
    (du, dk, dcpt, dbpt, dd), parts = _s5_backward(dy, u, hre, him, dxre, dxim, mb["tmt"], mb["bp"], p["d"],
                                                   tag + "s5_backward", _scatter_ici_stage(scatter, halves, gots, tag))
    dk = dk.reshape(SSM_GROUPS, S5_BLOCK, SSM_GROUP, SSM_GROUP)
    da_re, da_im, dlog_step, db_re, db_im, dc_re, dc_im = mats_vjp((dk, dcpt, dbpt, dar, dai))
    dx, dw_in, dnorm = _inproj_bwd(x, p["norm"], w["w_in"], [du, dgate], dxn, tag + "inproj_bwd")
    grads = dict(norm=dnorm.reshape(D_MODEL), w_in=dw_in, a_re=da_re, a_im=da_im, log_step=dlog_step, b_re=db_re,
                 b_im=db_im, c_re=dc_re, c_im=dc_im, d=dd.reshape(D_MODEL), w_glu=dw_glu, b_glu=db_glu.reshape(D_MODEL),
                 w_out=dw_out)
    return dx, grads, parts


def _attn_layer_fwd(i, x, p, w, rope, gather=None, head=None):
    tag = "l%d_" % i
    (q, k, v, gate), _ = _inproj_fwd(x, p["norm"], w["w_in"], ATTN_SPLITS, (BF16, BF16, BF16, BF16), rope, tag + "inproj_fwd")
    o, by_chip = _attn_fwd(q, k, v, p["sinks"], tag + "attn_fwd", _gather_ici_stage(gather))
    if head is not None:
        return _attn_out_loss(o, gate, x, w["w_out"], head[0], head[1], tag + "out_loss"), (x, q, k, v, gate, o), None
    xn, others = _attn_out_fwd(o, gate, x, w["w_out"], tag + "out_fwd", _gather_d2d_stage(by_chip))
    return xn, (x, q, k, v, gate, o), _gathered(by_chip, others)


def _attn_layer_bwd(i, dxn, saved, p, w, rope, scatter=None):
    tag = "l%d_" % i
    x, q, k, v, gate, o = saved
    halves, d2d_stage = _scatter_d2d_stage(scatter)
    (do, dgate, dw_out), gots = _attn_out_bwd(dxn, o, gate, w["w_out"], tag + "out_bwd", d2d_stage)
    (dq, dk, dv, dsinks), parts = _attn_bwd(q, k, v, o, do, p["sinks"], rope, tag + "attn_bwd",
                                            _scatter_ici_stage(scatter, halves, gots, tag))
    dx, dw_in, dnorm = _inproj_bwd(x, p["norm"], w["w_in"], [dq, dk, dv, dgate], dxn, tag + "inproj_bwd")
    grads = dict(norm=dnorm.reshape(D_MODEL), w_in=dw_in, sinks=dsinks.reshape(N_Q_HEADS, ATTN_BLOCK).sum(axis=1), w_out=dw_out)
    return dx, grads, parts


def _local_step(x, target, small, big, carried=()):
    rope = _rope_tables(x.shape[0])
    carried = {h: rest for h, *rest in carried}
    big = list(big)
    saved = []
    for i in range(4):
        gather = carried[i][0] if i in carried else None
        if i % 2 == 0:
            x, s, gathered = _ssm_layer_fwd(i, x, small[i], big[i], gather)
        else:
            head = (small[4]["norm"], target) if i == 3 else None
            x, s, gathered = _attn_layer_fwd(i, x, small[i], big[i], rope, gather, head)
        if gathered is not None:
            for layer, matrices in carried[i][1](gathered).items():
                big[layer] = matrices
        saved.append(s)
    loss, dx, dfinal = x
    grads = [None] * 4 + [dict(norm=dfinal.reshape(D_MODEL))]
    parts = {}
    for i in (3, 2, 1, 0):
        scatter = carried[i][2](grads) if i in carried else None
        if i % 2 == 0:
            dx, grads[i], parts[i] = _ssm_layer_bwd(i, dx, saved[i], small[i], big[i], scatter)
        else:
            dx, grads[i], parts[i] = _attn_layer_bwd(i, dx, saved[i], small[i], big[i], rope, scatter)
    return loss[0, 0], dx, grads, parts


def _owner_major(key, g):
    if key == "w_in":
        return g.reshape(D_MODEL, N_CHIPS, N_CORES, -1).transpose(2, 1, 0, 3)
    return g.reshape(N_CHIPS, N_CORES, -1, D_MODEL).transpose(1, 0, 2, 3)


def _from_gathered(key, mine, other):
    return _in_device_order(list(mine), list(other), 1 if key == "w_in" else 0)


SMALL_ROWS = 72


def _rows_of_small(a):
    flat = a.reshape(-1)
    return jnp.pad(flat, (0, -flat.shape[0] % D_MODEL)).reshape(-1, D_MODEL)


def _stack_small(arrays):
    rows = jnp.concatenate([_rows_of_small(a) for a in arrays], axis=0)
    assert rows.shape[0] <= N_DEV * SMALL_ROWS
    return jnp.pad(rows, ((0, N_DEV * SMALL_ROWS - rows.shape[0]), (0, 0)))


def kernel(*args):
    names = ["x"]
    layer_names = []
    for i, keys in enumerate(LAYER_KEYS):
        layer_names += ["l%d_%s" % (i, k) for k in keys]
    layer_names.append("final_norm")
    names += layer_names + ["loss_target"] + ["m_" + n for n in layer_names] + ["v_" + n for n in layer_names]
    given = dict(zip(names, args))
    big_names = [n for n in layer_names if n.split("_", 1)[1] in BIG_KEYS]
    small_names = [n for n in layer_names if n not in big_names]

    offsets = {}

    def families_of(layers):
        families = {}
        for n in big_names:
            if int(n[1]) in layers:
                family = families.setdefault(given[n].shape[1], [])
                offsets[n] = sum(given[other].shape[0] for other in family)
                family.append(n)
        return list(families.values())

    first, carried_by = families_of((0,)), {0: families_of((1,)), 1: families_of((2, 3))}
    stack = lambda pre, family: jnp.concatenate([given[pre + n] for n in family], axis=0)
    rows_of = lambda a, n: a[..., offsets[n]:offsets[n] + given[n].shape[0], :]
    local = lambda families: [stack("", family).astype(BF16) for family in families]

    def assemble(families, gathered):
        big = {}
        for family, (mine, other) in zip(families, gathered):
            for n in family:
                matrices = big.setdefault(int(n[1]), {})
                matrices[n.split("_", 1)[1]] = _from_gathered(n.split("_", 1)[1], rows_of(mine, n), rows_of(other, n))
        return big

    def sends_of(families, grads):
        return [jnp.concatenate([_owner_major(n.split("_", 1)[1], grads[int(n[1])][n.split("_", 1)[1]]) for n in family], axis=2)
                for family in families]

    small = [dict() for _ in range(5)]
    for n in small_names:
        if n == "final_norm":
            small[4]["norm"] = given[n]
        else:
            small[int(n[1])][n.split("_", 1)[1]] = given[n]

    big = [assemble(first, _all_gather(local(first), "gather_first_weights"))[0], None, None, None]
    carried = [(h, local(families), functools.partial(assemble, families),
                lambda grads, families=families: (sends_of(families, grads), [BF16] * len(families)))
               for h, families in carried_by.items()]
    my_loss, dx, grads, carried_parts = _local_step(given["x"][0], given["loss_target"][0], small, big, carried)

    def grad_of(n):
        return grads[4]["norm"] if n == "final_norm" else grads[int(n[1])][n.split("_", 1)[1]]

    flat = lambda f, last=jnp.zeros((1,), F32): _stack_small([f(n) for n in small_names] + [last])
    loss_row = sum(-(-given[n].size // D_MODEL) for n in small_names)
    sends = sends_of(first, grads) + [flat(grad_of, my_loss.reshape(1)).reshape(N_CORES, N_CHIPS, SMALL_ROWS, D_MODEL)]
    parts = _reduce_scatter(sends, [BF16] * len(first) + [F32], "scatter_grads")

    outs = {}
    tags = ("grad_", "delta_", "new_m_", "new_v_")
    all_families = first + carried_by[0] + carried_by[1]
    all_parts = list(parts[:-1]) + list(carried_parts[0]) + list(carried_parts[1])
    for i, (family, part) in enumerate(zip(all_families, all_parts)):
        results = _adamw(part, stack("", family), stack("m_", family), stack("v_", family), "adamw_matrices%d" % i)
        for tag, a in zip(tags, results):
            for n in family:
                outs[tag + n] = rows_of(a, n)

    my_slice = _sum_parts([parts[-1][s] for s in range(N_CHIPS)], F32, "sum_small_grads")
    mine, other = _all_gather([my_slice], "gather_small_grads")[0]
    first = lax.axis_index("c") == 0
    g_all = jnp.concatenate([jnp.where(first, mine, other), jnp.where(first, other, mine)], axis=0)
    g_all = g_all.reshape(1, N_DEV * SMALL_ROWS, D_MODEL)
    loss = g_all[0, loss_row, 0]
    results = _adamw(g_all, flat(lambda n: given[n]), flat(lambda n: given["m_" + n]), flat(lambda n: given["v_" + n]),
                     "adamw_small")
    for tag, a in zip(tags, results):
        at = 0
        for n in small_names:
            rows = -(-given[n].size // D_MODEL)
            outs[tag + n] = a[at:at + rows].reshape(-1)[:given[n].size].reshape(given[n].shape)
            at += rows
    result = [loss, dx[None]]
    for tag in ("grad_", "delta_", "new_m_", "new_v_"):
        result += [outs[tag + n] for n in layer_names]
    return tuple(result)
```
